```python
import math
import jax
import jax.numpy as jnp
from jax import lax
import numpy as np

D_MODEL = 1024
BATCH = 8
SEQ = 2048
DEPTH = 2

EPS = 1e-6
S5_CH = 512
S5_GROUP = 16
S5_GROUPS = S5_CH // S5_GROUP
S5_STATE = 64
S5_DT_MIN = 1e-3
S5_DT_MAX = 1e-1
FOX_HEADS = 8
FOX_DH = 64
FOX_W = FOX_HEADS * FOX_DH
Q_BLOCK = 128
FOX_BIAS_INIT = 3.0
RET_HEADS = 4
RET_DK = 128
RET_DV = 128
RET_QK_W = RET_HEADS * RET_DK
RET_V_W = RET_HEADS * RET_DV
ROPE_BASE = 10000.0
GLA_HEADS = 4
GLA_DK = 64
GLA_DV = 128
GLA_QK_W = GLA_HEADS * GLA_DK
GLA_V_W = GLA_HEADS * GLA_DV
GLA_RANK = 16
GLA_TAU = 16.0
CHUNK = 64
EVEN_IN = S5_CH + 3 * FOX_W + FOX_HEADS
ODD_IN = 2 * RET_QK_W + 2 * RET_V_W + 2 * GLA_QK_W + 2 * GLA_V_W + GLA_RANK
MIX_W_EVEN = S5_CH + FOX_W
MIX_W_ODD = RET_V_W + GLA_V_W
N_EXPERTS = 32
TOP_K = 4
D_EXPERT = 1024
SWIGLU_LIMIT = 7.0
SWIGLU_ALPHA = 1.702
MOE_BLOCK = 256
N_EVEN = (DEPTH + 1) // 2
N_ODD = DEPTH // 2

kernel_name = 'hybrid_s5_fox_retnet_gla_moe'


def rms_norm(x, g):
    xf = x.astype(jnp.float32)
    y = xf * lax.rsqrt(jnp.mean(xf * xf, axis=-1, keepdims=True) + EPS)
    return (y * g.astype(jnp.float32)).astype(x.dtype)


def head_rms(x):
    xf = x.astype(jnp.float32)
    return xf * lax.rsqrt(jnp.mean(xf * xf, axis=-1, keepdims=True) + EPS)


def split_cols(z, widths):
    outs, start = [], 0
    for w in widths:
        outs.append(z[..., start:start + w])
        start += w
    return outs


def rotary(x, pos):
    half = x.shape[-1] // 2
    inv = ROPE_BASE ** (-jnp.arange(half, dtype=jnp.float32) / half)
    ang = pos.astype(jnp.float32)[:, None] * inv[None, :]
    cos = jnp.cos(ang)[None, :, None, :]
    sin = jnp.sin(ang)[None, :, None, :]
    xf = x.astype(jnp.float32)
    x1, x2 = xf[..., :half], xf[..., half:]
    return jnp.concatenate([x1 * cos - x2 * sin, x1 * sin + x2 * cos], axis=-1)


def to_chunks(t):
    bsz, seq, nh, d = t.shape
    return t.reshape(bsz, seq // CHUNK, CHUNK, nh, d).transpose(1, 0, 3, 2, 4)


def from_chunks(t):
    n, bsz, nh, cl, d = t.shape
    return t.transpose(1, 0, 3, 2, 4).reshape(bsz, n * cl, nh, d)


def _complex_affine_combine(left, right):
    ar1, ai1, br1, bi1 = left
    ar2, ai2, br2, bi2 = right
    return (ar2 * ar1 - ai2 * ai1,
            ar2 * ai1 + ai2 * ar1,
            ar2 * br1 - ai2 * bi1 + br2,
            ar2 * bi1 + ai2 * br1 + bi2)


def s5_mixer(u, lam_re, lam_im, log_dt, b_re, b_im, c_re, c_im, d_skip, glu_w, glu_b):
    f32 = jnp.float32
    bsz, seq, _ = u.shape
    uf = u.astype(f32)
    ug = uf.reshape(bsz, seq, S5_GROUPS, S5_GROUP)
    lr, li = lam_re.astype(f32), lam_im.astype(f32)
    dt = jnp.exp(log_dt.astype(f32))[:, None]
    mag = jnp.exp(lr * dt)
    a_re, a_im = mag * jnp.cos(li * dt), mag * jnp.sin(li * dt)
    den = lr * lr + li * li
    n_re, n_im = a_re - 1.0, a_im
    z_re = (n_re * lr + n_im * li) / den
    z_im = (n_im * lr - n_re * li) / den
    br, bi = b_re.astype(f32), b_im.astype(f32)
    bb_re = z_re[..., None] * br - z_im[..., None] * bi
    bb_im = z_re[..., None] * bi + z_im[..., None] * br
    drive_re = jnp.einsum('bsgc,gpc->bsgp', ug, bb_re)
    drive_im = jnp.einsum('bsgc,gpc->bsgp', ug, bb_im)
    shape_a = (1, seq, S5_GROUPS, S5_STATE)
    a_re_s = jnp.broadcast_to(a_re[None, None], shape_a)
    a_im_s = jnp.broadcast_to(a_im[None, None], shape_a)
    _, _, h_re, h_im = lax.associative_scan(
        _complex_affine_combine, (a_re_s, a_im_s, drive_re, drive_im), axis=1)
    y = (jnp.einsum('bsgp,gcp->bsgc', h_re, c_re.astype(f32))
         - jnp.einsum('bsgp,gcp->bsgc', h_im, c_im.astype(f32)))
    y = y.reshape(bsz, seq, S5_CH) + d_skip.astype(f32) * uf
    g = jax.nn.gelu(y)
    out = g * jax.nn.sigmoid(g @ glu_w.astype(f32) + glu_b.astype(f32))
    return out.astype(u.dtype)


def fox_attention(q, k, v, f_logit):
    seq = q.shape[1]
    cum = jnp.cumsum(jax.nn.log_sigmoid(f_logit.astype(jnp.float32)), axis=1).transpose(0, 2, 1)
    scale = FOX_DH ** -0.5
    outs = []
    for blk in range(seq // Q_BLOCK):
        q0, q1 = blk * Q_BLOCK, (blk + 1) * Q_BLOCK
        s = jnp.einsum('bqhd,bkhd->bhqk', q[:, q0:q1], k[:, :q1]).astype(jnp.float32) * scale
        s = s + cum[:, :, q0:q1, None] - cum[:, :, None, :q1]
        mask = (q0 + jnp.arange(Q_BLOCK))[:, None] >= jnp.arange(q1)[None, :]
        p = jax.nn.softmax(jnp.where(mask, s, -jnp.inf), axis=-1)
        outs.append(jnp.einsum('bhqk,bkhd->bqhd', p.astype(v.dtype), v[:, :q1]))
    return jnp.concatenate(outs, axis=1)


def retention(q, k, v):
    f32 = jnp.float32
    bsz, seq, nh, dk = q.shape
    log_g = jnp.log(1.0 - jnp.exp2(-5.0 - jnp.arange(nh, dtype=f32)))
    idx = jnp.arange(CHUNK, dtype=f32)
    rel = idx[:, None] - idx[None, :]
    dmat = jnp.where(rel >= 0, jnp.exp(log_g[:, None, None] * jnp.maximum(rel, 0.0)), 0.0)
    xi = jnp.exp(log_g[:, None] * (idx + 1.0))[..., None]
    zeta = jnp.exp(log_g[:, None] * (CHUNK - 1.0 - idx))[..., None]
    g_chunk = jnp.exp(log_g * CHUNK)[:, None, None]
    qc = to_chunks(q.astype(f32))
    kc = to_chunks(k.astype(f32) * dk ** -0.5)
    vc = to_chunks(v.astype(f32))

    def step(state, inp):
        qi, ki, vi = inp
        scores = jnp.einsum('bhqd,bhkd->bhqk', qi, ki) * dmat
        o = (jnp.einsum('bhqk,bhkv->bhqv', scores, vi)
             + jnp.einsum('bhqd,bhdv->bhqv', qi * xi, state))
        state = g_chunk * state + jnp.einsum('bhkd,bhkv->bhdv', ki, vi * zeta)
        return state, o

    state0 = jnp.zeros((bsz, nh, dk, v.shape[-1]), f32)
    _, o = lax.scan(step, state0, (qc, kc, vc))
    return from_chunks(o)


def gla(q, k, v, log_a):
    f32 = jnp.float32
    bsz, seq, nh, dk = q.shape
    qc = to_chunks(q.astype(f32) * dk ** -0.5)
    kc = to_chunks(k.astype(f32))
    vc = to_chunks(v.astype(f32))
    ac = to_chunks(log_a.astype(f32))
    causal = jnp.tril(jnp.ones((CHUNK, CHUNK), dtype=bool))[:, :, None]

    def step(state, inp):
        qi, ki, vi, ai = inp
        b = jnp.cumsum(ai, axis=2)
        o_inter = jnp.einsum('bhqd,bhdv->bhqv', qi * jnp.exp(b), state)
        decay = jnp.exp(jnp.where(causal, b[:, :, :, None, :] - b[:, :, None, :, :], -jnp.inf))
        scores = jnp.einsum('bhqd,bhkd,bhqkd->bhqk', qi, ki, decay)
        o = o_inter + jnp.einsum('bhqk,bhkv->bhqv', scores, vi)
        b_last = b[:, :, -1:, :]
        state = (jnp.exp(b_last)[:, :, 0, :, None] * state
                 + jnp.einsum('bhkd,bhkv->bhdv', ki * jnp.exp(b_last - b), vi))
        return state, o

    state0 = jnp.zeros((bsz, nh, dk, v.shape[-1]), f32)
    _, o = lax.scan(step, state0, (qc, kc, vc, ac))
    return from_chunks(o)


def even_mixer(h, w_in, b_f, lam_re, lam_im, log_dt, b_re, b_im, c_re, c_im, d_skip,
               glu_w, glu_b, w_out):
    bsz, seq, _ = h.shape
    z = h @ w_in
    u, q, k, v, f = split_cols(z, (S5_CH, FOX_W, FOX_W, FOX_W, FOX_HEADS))
    y_a = s5_mixer(u, lam_re, lam_im, log_dt, b_re, b_im, c_re, c_im, d_skip, glu_w, glu_b)
    hs = (bsz, seq, FOX_HEADS, FOX_DH)
    y_b = fox_attention(q.reshape(hs), k.reshape(hs), v.reshape(hs), f + b_f).reshape(bsz, seq, FOX_W)
    return jnp.concatenate([y_a, y_b.astype(y_a.dtype)], axis=-1) @ w_out


def odd_mixer(h, w_in, gla_w_up, gla_b_gate, w_out):
    f32 = jnp.float32
    bsz, seq, _ = h.shape
    z = h @ w_in
    rq, rk, rv, rg, gq, gk, gv, glr, gr = split_cols(
        z, (RET_QK_W, RET_QK_W, RET_V_W, RET_V_W, GLA_QK_W, GLA_QK_W, GLA_V_W, GLA_RANK, GLA_V_W))
    pos = jnp.arange(seq)
    rq = rotary(rq.reshape(bsz, seq, RET_HEADS, RET_DK), pos)
    rk = rotary(rk.reshape(bsz, seq, RET_HEADS, RET_DK), pos)
    ret = retention(rq, rk, rv.reshape(bsz, seq, RET_HEADS, RET_DV))
    y_c = jax.nn.silu(rg.astype(f32)) * head_rms(ret).reshape(bsz, seq, RET_V_W)
    log_a = jax.nn.log_sigmoid((glr @ gla_w_up + gla_b_gate).astype(f32)) / GLA_TAU
    o = gla(gq.reshape(bsz, seq, GLA_HEADS, GLA_DK), gk.reshape(bsz, seq, GLA_HEADS, GLA_DK),
            gv.reshape(bsz, seq, GLA_HEADS, GLA_DV), log_a.reshape(bsz, seq, GLA_HEADS, GLA_DK))
    y_d = jax.nn.silu(gr.astype(f32)) * head_rms(o).reshape(bsz, seq, GLA_V_W)
    y = jnp.concatenate([y_c, y_d], axis=-1).astype(h.dtype)
    return y @ w_out


def clamped_swiglu(gu):
    x_glu, x_lin = gu[..., :D_EXPERT], gu[..., D_EXPERT:]
    x_glu = jnp.minimum(x_glu, SWIGLU_LIMIT)
    x_lin = jnp.clip(x_lin, -SWIGLU_LIMIT, SWIGLU_LIMIT)
    return x_glu * jax.nn.sigmoid(SWIGLU_ALPHA * x_glu) * (x_lin + 1.0)


def moe_ffn(h, router_w, router_b, w_gu, b_gu, w_down, b_down):
    bsz, seq, d = h.shape
    n_tok = bsz * seq
    n_assign = n_tok * TOP_K
    ht = h.reshape(n_tok, d)
    logits = (ht @ router_w + router_b).astype(jnp.float32)
    top_v, top_i = lax.top_k(logits, TOP_K)
    gates = jax.nn.softmax(top_v, axis=-1)
    flat_e = top_i.reshape(-1)
    flat_tok = jnp.arange(n_assign, dtype=jnp.int32) // TOP_K
    flat_w = gates.reshape(-1)
    order = jnp.argsort(flat_e)
    se, st, sw = flat_e[order], flat_tok[order], flat_w[order]
    counts = jnp.bincount(flat_e, length=N_EXPERTS)
    padded = (counts + MOE_BLOCK - 1) // MOE_BLOCK * MOE_BLOCK
    pad_end = jnp.cumsum(padded)
    pad_start = pad_end - padded
    grp_start = jnp.cumsum(counts) - counts
    dest = pad_start[se] + jnp.arange(n_assign, dtype=jnp.int32) - grp_start[se]
    n_blocks = n_assign // MOE_BLOCK + N_EXPERTS
    slot_tok = jnp.full((n_blocks * MOE_BLOCK,), n_tok, dtype=jnp.int32).at[dest].set(st)
    block_expert = jnp.minimum(
        jnp.searchsorted(pad_end, jnp.arange(n_blocks) * MOE_BLOCK, side='right'), N_EXPERTS - 1)
    h_pad = jnp.concatenate([ht, jnp.zeros((1, d), ht.dtype)], axis=0)
    xb = h_pad[slot_tok].reshape(n_blocks, MOE_BLOCK, d)

    def expert_block(args):
        xblk, e = args
        gu = xblk @ w_gu[e] + b_gu[e]
        return clamped_swiglu(gu) @ w_down[e] + b_down[e]

    yb = lax.map(expert_block, (xb, block_expert)).reshape(n_blocks * MOE_BLOCK, d)
    y = yb[dest] * sw[:, None].astype(yb.dtype)
    out = jnp.zeros((n_tok, d), yb.dtype).at[st].add(y)
    return out.reshape(bsz, seq, d)


def setup_inputs(seed: int = 0) -> dict:
    key = jax.random.key(seed)
    keys = iter(jax.random.split(key, 48))

    def normal(shape, scale):
        return jax.random.normal(next(keys), shape, jnp.float32) * scale

    def gain(shape):
        return 1.0 + normal(shape, 0.01)

    d = D_MODEL
    x = normal((BATCH, SEQ, d), 1.0)
    c = normal((BATCH, d), 1.0)
    ada_w = normal((DEPTH, d, 6 * d), 0.5 * d ** -0.5)
    ada_b = normal((DEPTH, 6 * d), 0.01)
    norm_pre_mix = gain((DEPTH, d))
    norm_post_mix = gain((DEPTH, d))
    norm_pre_ffn = gain((DEPTH, d))
    norm_post_ffn = gain((DEPTH, d))
    ev_w_in = normal((N_EVEN, d, EVEN_IN), d ** -0.5)
    fox_b_f = FOX_BIAS_INIT + normal((N_EVEN, FOX_HEADS), 0.5)
    s5_lam_re = -0.5 + normal((N_EVEN, S5_GROUPS, S5_STATE), 0.01)
    s5_lam_im = jnp.pi * jnp.arange(S5_STATE, dtype=jnp.float32) + normal((N_EVEN, S5_GROUPS, S5_STATE), 0.01)
    s5_log_dt = jax.random.uniform(next(keys), (N_EVEN, S5_GROUPS), jnp.float32,
                                   math.log(S5_DT_MIN), math.log(S5_DT_MAX))
    s5_b_re = normal((N_EVEN, S5_GROUPS, S5_STATE, S5_GROUP), (2 * S5_GROUP) ** -0.5)
    s5_b_im = normal((N_EVEN, S5_GROUPS, S5_STATE, S5_GROUP), (2 * S5_GROUP) ** -0.5)
    s5_c_re = normal((N_EVEN, S5_GROUPS, S5_GROUP, S5_STATE), 1.0)
    s5_c_im = normal((N_EVEN, S5_GROUPS, S5_GROUP, S5_STATE), 1.0)
    s5_d = normal((N_EVEN, S5_CH), 1.0)
    s5_glu_w = normal((N_EVEN, S5_CH, S5_CH), S5_CH ** -0.5)
    s5_glu_b = normal((N_EVEN, S5_CH), 0.01)
    ev_w_out = normal((N_EVEN, MIX_W_EVEN, d), MIX_W_EVEN ** -0.5)
    od_w_in = normal((N_ODD, d, ODD_IN), d ** -0.5)
    gla_w_up = normal((N_ODD, GLA_RANK, GLA_QK_W), GLA_RANK ** -0.5)
    gla_b_gate = normal((N_ODD, GLA_QK_W), 0.1)
    od_w_out = normal((N_ODD, MIX_W_ODD, d), MIX_W_ODD ** -0.5)
    router_w = normal((DEPTH, d, N_EXPERTS), d ** -0.5)
    router_b = normal((DEPTH, N_EXPERTS), 0.01)
    exp_w_gu = normal((DEPTH, N_EXPERTS, d, 2 * D_EXPERT), d ** -0.5)
    exp_b_gu = normal((DEPTH, N_EXPERTS, 2 * D_EXPERT), 0.01)
    exp_w_down = normal((DEPTH, N_EXPERTS, D_EXPERT, d), D_EXPERT ** -0.5)
    exp_b_down = normal((DEPTH, N_EXPERTS, d), 0.01)
    return {'x': x, 'c': c, 'ada_w': ada_w, 'ada_b': ada_b,
            'norm_pre_mix': norm_pre_mix, 'norm_post_mix': norm_post_mix,
            'norm_pre_ffn': norm_pre_ffn, 'norm_post_ffn': norm_post_ffn,
            'ev_w_in': ev_w_in, 'fox_b_f': fox_b_f,
            's5_lam_re': s5_lam_re, 's5_lam_im': s5_lam_im, 's5_log_dt': s5_log_dt,
            's5_b_re': s5_b_re, 's5_b_im': s5_b_im, 's5_c_re': s5_c_re, 's5_c_im': s5_c_im,
            's5_d': s5_d, 's5_glu_w': s5_glu_w, 's5_glu_b': s5_glu_b, 'ev_w_out': ev_w_out,
            'od_w_in': od_w_in, 'gla_w_up': gla_w_up, 'gla_b_gate': gla_b_gate, 'od_w_out': od_w_out,
            'router_w': router_w, 'router_b': router_b,
            'exp_w_gu': exp_w_gu, 'exp_b_gu': exp_b_gu, 'exp_w_down': exp_w_down, 'exp_b_down': exp_b_down}


def reference(x, c, ada_w, ada_b, norm_pre_mix, norm_post_mix, norm_pre_ffn, norm_post_ffn,
              ev_w_in, fox_b_f, s5_lam_re, s5_lam_im, s5_log_dt, s5_b_re, s5_b_im, s5_c_re, s5_c_im,
              s5_d, s5_glu_w, s5_glu_b, ev_w_out, od_w_in, gla_w_up, gla_b_gate, od_w_out,
              router_w, router_b, exp_w_gu, exp_b_gu, exp_w_down, exp_b_down):
    for l in range(DEPTH):
        mod = (jax.nn.silu(c) @ ada_w[l] + ada_b[l])[:, None, :]
        sh1, sc1, g1, sh2, sc2, g2 = jnp.split(mod, 6, axis=-1)
        h = rms_norm(x, norm_pre_mix[l]) * (1.0 + sc1) + sh1
        i = l // 2
        if l % 2 == 0:
            m = even_mixer(h, ev_w_in[i], fox_b_f[i], s5_lam_re[i], s5_lam_im[i], s5_log_dt[i],
                           s5_b_re[i], s5_b_im[i], s5_c_re[i], s5_c_im[i], s5_d[i],
                           s5_glu_w[i], s5_glu_b[i], ev_w_out[i])
        else:
            m = odd_mixer(h, od_w_in[i], gla_w_up[i], gla_b_gate[i], od_w_out[i])
        x = x + g1 * rms_norm(m, norm_post_mix[l])
        h = rms_norm(x, norm_pre_ffn[l]) * (1.0 + sc2) + sh2
        f = moe_ffn(h, router_w[l], router_b[l], exp_w_gu[l], exp_b_gu[l], exp_w_down[l], exp_b_down[l])
        x = x + g2 * rms_norm(f, norm_post_ffn[l])
    return x
```

```python
import functools
import math

import jax
import jax.numpy as jnp
from jax import lax
from jax.experimental import pallas as pl
from jax.experimental.pallas import tpu as pltpu

F32 = jnp.float32
BF16 = jnp.bfloat16
HIGHEST = lax.Precision.HIGHEST

D_MODEL = 1024
DEPTH = 2
EPS = 1e-6
S5_CH = 512
S5_GROUP = 16
S5_GROUPS = S5_CH // S5_GROUP
S5_STATE = 64
S5_CHUNK = 16
FOX_HEADS = 8
FOX_DH = 64
FOX_W = FOX_HEADS * FOX_DH
RET_HEADS = 4
RET_DK = 128
RET_DV = 128
ROPE_BASE = 10000.0
GLA_HEADS = 4
GLA_DK = 64
GLA_DV = 128
GLA_RANK = 16
GLA_TAU = 16.0
GLA_CHUNK = 64
GLA_SUB = 16
N_EXPERTS = 32
TOP_K = 4
D_EXPERT = 1024
SWIGLU_LIMIT = 7.0
SWIGLU_ALPHA = 1.702
MOE_BLOCK = 256

LANES = 128
SUBLANES = 8
VMEM_LIMIT = 56 * 1024 * 1024


def _cparams(*sem):
    return pltpu.CompilerParams(dimension_semantics=sem, vmem_limit_bytes=VMEM_LIMIT)


def _bdot(a, b):
    return jnp.dot(a, b, preferred_element_type=F32)


def _dot_nt(a, b):
    return lax.dot_general(a, b, (((1,), (1,)), ((), ())), preferred_element_type=F32)


def _dot_tn(a, b):
    return lax.dot_general(a, b, (((0,), (0,)), ((), ())), preferred_element_type=F32)


def _split3(x):
    hi = x.astype(BF16)
    r = x - hi.astype(F32)
    mid = r.astype(BF16)
    lo = (r - mid.astype(F32)).astype(BF16)
    return hi, mid, lo


def _dot01(m01, x):
    hi, mid, lo = _split3(x)
    return _bdot(m01, hi) + _bdot(m01, mid) + _bdot(m01, lo)


def _lower_tri(n, strict=False):
    r = lax.broadcasted_iota(jnp.int32, (n, n), 0)
    c = lax.broadcasted_iota(jnp.int32, (n, n), 1)
    return ((r > c) if strict else (r >= c)).astype(BF16)


def _log_sigmoid(x):
    return jnp.minimum(x, 0.0) - jnp.log1p(jnp.exp(-jnp.abs(x)))


def _sigmoid(x):
    return 1.0 / (1.0 + jnp.exp(-x))


def _silu(x):
    return x * _sigmoid(x)


def _rms(x):
    return x * lax.rsqrt(jnp.mean(x * x, axis=-1, keepdims=True) + EPS)


def _mod_kernel(c_ref, w_ref, b_ref, o_ref):
    c = c_ref[...]
    o_ref[0] = jnp.dot(_silu(c), w_ref[0], preferred_element_type=F32, precision=HIGHEST) + b_ref[0]


def _modulation(c, ada_w, ada_b):
    depth, d, n = ada_w.shape
    bsz = c.shape[0]
    tn = D_MODEL
    return pl.pallas_call(
        _mod_kernel,
        grid=(depth, n // tn),
        in_specs=[pl.BlockSpec((bsz, d), lambda l, j: (0, 0)),
                  pl.BlockSpec((1, d, tn), lambda l, j: (l, 0, j)),
                  pl.BlockSpec((1, 1, tn), lambda l, j: (l, 0, j))],
        out_specs=pl.BlockSpec((1, bsz, tn), lambda l, j: (l, 0, j)),
        out_shape=jax.ShapeDtypeStruct((depth, bsz, n), F32),
        compiler_params=_cparams("parallel", "parallel"),
        name="adaln_mod",
    )(c, ada_w, ada_b.reshape(depth, 1, n))


def _mod_spec(chunk, tiles_per_seq):
    return pl.BlockSpec((1, 1, D_MODEL), lambda i: (i // tiles_per_seq, 0, chunk))


def _prenorm(x, g_ref, sc_ref, sh_ref):
    return _rms(x) * (g_ref[...] * (1.0 + sc_ref[0])) + sh_ref[0]


def _in_even_kernel(x_ref, sh_ref, sc_ref, g_ref, w_ref, bf_ref,
                    u_ref, q_ref, k_ref, v_ref, cum_ref, carry_ref, *, tiles_per_seq):
    i = pl.program_id(0)
    tm = x_ref.shape[0]
    h = _prenorm(x_ref[...], g_ref, sc_ref, sh_ref)
    z = _bdot(h.astype(BF16), w_ref[...])
    u_ref[...] = z[:, 0:S5_CH]
    q_ref[...] = (z[:, S5_CH:S5_CH + FOX_W] * (FOX_DH ** -0.5)).astype(BF16)
    k_ref[...] = z[:, S5_CH + FOX_W:S5_CH + 2 * FOX_W].astype(BF16)
    v_ref[...] = z[:, S5_CH + 2 * FOX_W:S5_CH + 3 * FOX_W].astype(BF16)
    ls = _log_sigmoid(z[:, S5_CH + 3 * FOX_W:] + bf_ref[...])

    @pl.when(i % tiles_per_seq == 0)
    def _():
        carry_ref[...] = jnp.zeros_like(carry_ref)

    cum = _dot01(_lower_tri(tm), ls) + carry_ref[...]
    cum_ref[...] = cum
    carry_ref[...] = cum[tm - 1:tm, :]


def _in_even(x2, mod3, gain, w_in, b_f, seq, tm):
    t, d = x2.shape
    tiles_per_seq = seq // tm
    nw = S5_CH + 3 * FOX_W
    w = jnp.concatenate([w_in[:, :nw], jnp.pad(w_in[:, nw:], ((0, 0), (0, LANES - FOX_HEADS)))],
                        axis=1).astype(BF16)
    bf = jnp.pad(b_f, (0, LANES - FOX_HEADS)).reshape(1, LANES)
    tok = lambda n: pl.BlockSpec((tm, n), lambda i: (i, 0))
    full = lambda a: pl.BlockSpec(a.shape, lambda i: (0,) * a.ndim)
    return pl.pallas_call(
        functools.partial(_in_even_kernel, tiles_per_seq=tiles_per_seq),
        grid=(t // tm,),
        in_specs=[tok(d), _mod_spec(0, tiles_per_seq), _mod_spec(1, tiles_per_seq),
                  full(gain), full(w), full(bf)],
        out_specs=[tok(S5_CH), tok(FOX_W), tok(FOX_W), tok(FOX_W), tok(LANES)],
        out_shape=[jax.ShapeDtypeStruct((t, S5_CH), F32),
                   jax.ShapeDtypeStruct((t, FOX_W), BF16),
                   jax.ShapeDtypeStruct((t, FOX_W), BF16),
                   jax.ShapeDtypeStruct((t, FOX_W), BF16),
                   jax.ShapeDtypeStruct((t, LANES), F32)],
        scratch_shapes=[pltpu.VMEM((1, LANES), F32)],
        compiler_params=_cparams("arbitrary"),
        name="in_proj_even",
    )(x2, mod3, mod3, gain, w, bf)


def _s5_kernel(u_ref, toep_ref, wst_ref, cst_ref, a_ref, y_ref, e_scr, hp_scr, *, n_chunks, nb):
    u = u_ref[0]
    e_scr[...] = _bdot(u, wst_ref[0])
    two_p = 2 * S5_STATE
    a1 = jnp.broadcast_to(a_ref[0, 0:1, :], (nb, two_p))
    a2 = jnp.broadcast_to(a_ref[0, 1:2, :], (nb, two_p))
    a2s = jnp.broadcast_to(a_ref[0, 2:3, :], (nb, two_p))

    def body(c, carry):
        h, hs = carry
        r = pl.multiple_of(c * nb, nb)
        hp_scr[pl.ds(r, nb), :] = h
        e = e_scr[pl.ds(r, nb), :]
        return a1 * h + a2 * hs + e[:, :two_p], a1 * hs + a2s * h + e[:, two_p:]

    zero = jnp.zeros((nb, two_p), F32)
    lax.fori_loop(0, n_chunks, body, (zero, zero))
    y_ref[0] = _bdot(u, toep_ref[0]) + _bdot(hp_scr[...].astype(BF16), cst_ref[0])


def _s5_tables(lam_re, lam_im, log_dt, b_re, b_im, c_re, c_im):
    ell, p, g = S5_CHUNK, S5_STATE, S5_GROUPS
    lr, li = lam_re.astype(F32), lam_im.astype(F32)
    dt = jnp.exp(log_dt.astype(F32))[:, None]
    mag = jnp.exp(lr * dt)
    a_re, a_im = mag * jnp.cos(li * dt), mag * jnp.sin(li * dt)
    den = lr * lr + li * li
    n_re, n_im = a_re - 1.0, a_im
    z_re = (n_re * lr + n_im * li) / den
    z_im = (n_im * lr - n_re * li) / den
    br, bi = b_re.astype(F32), b_im.astype(F32)
    bb_re = z_re[..., None] * br - z_im[..., None] * bi
    bb_im = z_re[..., None] * bi + z_im[..., None] * br
    j = jnp.arange(ell + 1, dtype=F32)[:, None, None]
    pmag = jnp.exp(j * (lr * dt)[None])
    pr, pi = pmag * jnp.cos(j * (li * dt)[None]), pmag * jnp.sin(j * (li * dt)[None])
    w_re = pr[..., None] * bb_re[None] - pi[..., None] * bb_im[None]
    w_im = pr[..., None] * bb_im[None] + pi[..., None] * bb_re[None]
    cr, ci = c_re.astype(F32), c_im.astype(F32)
    kern = (jnp.einsum('gcp,jgpd->jgcd', cr, w_re[:ell], precision=HIGHEST)
            - jnp.einsum('gcp,jgpd->jgcd', ci, w_im[:ell], precision=HIGHEST))
    s_idx = jnp.arange(ell)[:, None]
    t_idx = jnp.arange(ell)[None, :]
    lag = jnp.clip(t_idx - s_idx, 0, ell - 1)
    toep = kern[lag]
    toep = jnp.where((t_idx >= s_idx)[:, :, None, None, None], toep, 0.0)
    toep = toep.transpose(2, 0, 4, 1, 3).reshape(g, ell * S5_GROUP, ell * S5_GROUP)
    rev = jnp.arange(ell - 1, -1, -1)
    st_re = w_re[rev].transpose(1, 0, 3, 2).reshape(g, ell * S5_GROUP, p)
    st_im = w_im[rev].transpose(1, 0, 3, 2).reshape(g, ell * S5_GROUP, p)
    wst = jnp.concatenate([st_re, st_im, st_im, st_re], axis=-1)
    p1r, p1i = pr[1:], pi[1:]
    c_hr = cr[None] * p1r[:, :, None, :] - ci[None] * p1i[:, :, None, :]
    c_hi = -cr[None] * p1i[:, :, None, :] - ci[None] * p1r[:, :, None, :]
    cst = jnp.concatenate([c_hr, c_hi], axis=-1).transpose(1, 3, 0, 2).reshape(g, 2 * p, ell * S5_GROUP)
    al_r, al_i = pr[ell], pi[ell]
    a_rows = jnp.stack([jnp.concatenate([al_r, al_r], -1), jnp.concatenate([-al_i, al_i], -1),
                        jnp.concatenate([al_i, -al_i], -1)], axis=1)
    a_rows = jnp.pad(a_rows, ((0, 0), (0, SUBLANES - 3), (0, 0)))
    return toep.astype(BF16), wst.astype(BF16), cst.astype(BF16), a_rows


def _s5_scan(u, bsz, seq, tables):
    toep, wst, cst, a_rows = tables
    ell, g = S5_CHUNK, S5_GROUPS
    n_chunks = seq // ell
    rows = n_chunks * bsz
    w = ell * S5_GROUP
    ug = (u.reshape(bsz, n_chunks, ell, g, S5_GROUP).transpose(3, 1, 0, 2, 4)
          .reshape(g, rows, w).astype(BF16))
    grp = lambda a: pl.BlockSpec((1,) + a.shape[1:], lambda i: (i, 0, 0))
    yg = pl.pallas_call(
        functools.partial(_s5_kernel, n_chunks=n_chunks, nb=bsz),
        grid=(g,),
        in_specs=[grp(ug), grp(toep), grp(wst), grp(cst), grp(a_rows)],
        out_specs=pl.BlockSpec((1, rows, w), lambda i: (i, 0, 0)),
        out_shape=jax.ShapeDtypeStruct((g, rows, w), F32),
        scratch_shapes=[pltpu.VMEM((rows, 4 * S5_STATE), F32), pltpu.VMEM((rows, 2 * S5_STATE), F32)],
        compiler_params=_cparams("parallel"),
        name="s5_scan",
    )(ug, toep, wst, cst, a_rows)
    return (yg.reshape(g, n_chunks, bsz, ell, S5_GROUP).transpose(2, 1, 3, 0, 4)
            .reshape(bsz * seq, S5_CH))


def _fox_kernel(q_ref, k_ref, v_ref, cum_ref, o_ref, *, tq):
    i = pl.program_id(2)
    q2 = q_ref[...]
    lane = lax.broadcasted_iota(jnp.int32, (1, LANES), 1)
    first = lane < FOX_DH
    zero = jnp.zeros_like(q2)
    qh = (jnp.where(first, q2, zero), jnp.where(first, zero, q2))
    row = lax.broadcasted_iota(jnp.int32, (tq, tq), 0)
    col = lax.broadcasted_iota(jnp.int32, (tq, tq), 1)
    causal = row >= col
    i0 = pl.multiple_of(i * tq, tq)
    c0 = [jnp.max(cum_ref[0, 0, h:h + 1, pl.ds(i0, tq)], axis=-1, keepdims=True) for h in range(2)]

    def block(j, carry, diag):
        j0 = pl.multiple_of(j * tq, tq)
        kj = k_ref[pl.ds(j0, tq), :]
        vj = v_ref[pl.ds(j0, tq), :]
        out = []
        for h in range(2):
            m, l, acc = carry[h]
            s = _dot_nt(qh[h], kj) + (c0[h] - cum_ref[0, 0, h:h + 1, pl.ds(j0, tq)])
            if diag:
                s = jnp.where(causal, s, -jnp.inf)
            m_new = jnp.maximum(m, jnp.max(s, axis=-1, keepdims=True))
            p = jnp.exp(s - m_new)
            alpha = jnp.exp(m - m_new)
            l = alpha * l + jnp.sum(p, axis=-1, keepdims=True)
            acc = alpha * acc + _bdot(p.astype(BF16), vj)
            out.append((m_new, l, acc))
        return tuple(out)

    init = tuple((jnp.full((tq, 1), -jnp.inf, F32), jnp.zeros((tq, 1), F32), jnp.zeros((tq, LANES), F32))
                 for _ in range(2))
    carry = lax.fori_loop(0, i, lambda j, c: block(j, c, False), init)
    carry = block(i, carry, True)
    o = [carry[h][2] / carry[h][1] for h in range(2)]
    o_ref[...] = jnp.where(first, o[0], o[1]).astype(o_ref.dtype)


def _fox(q, k, v, cum, bsz, seq, tq):
    t = q.shape[0]
    pairs = FOX_HEADS // 2
    nq = seq // tq
    cum_t = cum[:, :FOX_HEADS].reshape(bsz, seq, pairs, 2).transpose(0, 2, 3, 1)
    return pl.pallas_call(
        functools.partial(_fox_kernel, tq=tq),
        grid=(bsz, pairs, nq),
        in_specs=[pl.BlockSpec((tq, LANES), lambda b, p, i: (b * nq + i, p)),
                  pl.BlockSpec((seq, LANES), lambda b, p, i: (b, p)),
                  pl.BlockSpec((seq, LANES), lambda b, p, i: (b, p)),
                  pl.BlockSpec((1, 1, 2, seq), lambda b, p, i: (b, p, 0, 0))],
        out_specs=pl.BlockSpec((tq, LANES), lambda b, p, i: (b * nq + i, p)),
        out_shape=jax.ShapeDtypeStruct((t, FOX_W), BF16),
        compiler_params=_cparams("parallel", "parallel", "arbitrary"),
        name="fox_attention",
    )(q, k, v, cum_t)


_ODD_COLS = (("rq", 512), ("rk", 512), ("rv", 512), ("rg", 512), ("gq", 256), ("gk", 256),
             ("gv", 512), ("gr", 512), ("glr", LANES))


def _odd_offsets():
    off, out = 0, {}
    for name, w in _ODD_COLS:
        out[name] = (off, off + w)
        off += w
    return out, off


def _in_odd_kernel(x_ref, sh_ref, sc_ref, g_ref, w_ref, cos_ref, sin_ref, wup_ref, bg_ref,
                   rq_ref, rk_ref, rv_ref, sg_ref, gq_ref, gk_ref, gv_ref, sr_ref, la_ref):
    h = _prenorm(x_ref[...], g_ref, sc_ref, sh_ref)
    z = _bdot(h.astype(BF16), w_ref[...])
    off, _ = _odd_offsets()
    col = lambda n: z[:, off[n][0]:off[n][1]]
    cos, sin = cos_ref[...], sin_ref[...]

    def rope(t, scale):
        heads = []
        for hd in range(RET_HEADS):
            th = t[:, hd * RET_DK:(hd + 1) * RET_DK]
            heads.append((th * cos + pltpu.roll(th, RET_DK // 2, 1) * sin) * scale)
        return jnp.concatenate(heads, axis=1).astype(BF16)

    rq_ref[...] = rope(col("rq"), 1.0)
    rk_ref[...] = rope(col("rk"), RET_DK ** -0.5)
    rv_ref[...] = col("rv").astype(BF16)
    sg_ref[...] = _silu(col("rg"))
    gq_ref[...] = col("gq") * (GLA_DK ** -0.5)
    gk_ref[...] = col("gk")
    gv_ref[...] = col("gv").astype(BF16)
    sr_ref[...] = _silu(col("gr"))
    gate = jnp.dot(col("glr"), wup_ref[...], preferred_element_type=F32, precision=HIGHEST) + bg_ref[...]
    la_ref[...] = _log_sigmoid(gate) * (1.0 / GLA_TAU)


def _in_odd(x2, mod3, gain, w_in, w_up, b_gate, seq, tm):
    t, d = x2.shape
    tps = seq // tm
    ref_w = (512, 512, 512, 512, 256, 256, 512, GLA_RANK, 512)
    starts = [0]
    for wd in ref_w:
        starts.append(starts[-1] + wd)
    seg = lambda j: w_in[:, starts[j]:starts[j + 1]]
    w = jnp.concatenate([seg(0), seg(1), seg(2), seg(3), seg(4), seg(5), seg(6), seg(8),
                         jnp.pad(seg(7), ((0, 0), (0, LANES - GLA_RANK)))], axis=1).astype(BF16)
    wup = jnp.pad(w_up.astype(F32), ((0, LANES - GLA_RANK), (0, 0)))
    bg = b_gate.reshape(1, -1).astype(F32)
    half = RET_DK // 2
    inv = ROPE_BASE ** (-jnp.arange(half, dtype=F32) / half)
    ang = jnp.arange(seq, dtype=F32)[:, None] * inv[None, :]
    cos = jnp.concatenate([jnp.cos(ang), jnp.cos(ang)], axis=1)
    sin = jnp.concatenate([-jnp.sin(ang), jnp.sin(ang)], axis=1)
    tok = lambda n: pl.BlockSpec((tm, n), lambda i: (i, 0))
    full = lambda a: pl.BlockSpec(a.shape, lambda i: (0,) * a.ndim)
    pos = pl.BlockSpec((tm, RET_DK), lambda i: (i % tps, 0))
    widths = (512, 512, 512, 512, 256, 256, 512, 512, 256)
    dtypes = (BF16, BF16, BF16, F32, F32, F32, BF16, F32, F32)
    return pl.pallas_call(
        _in_odd_kernel,
        grid=(t // tm,),
        in_specs=[tok(d), _mod_spec(0, tps), _mod_spec(1, tps), full(gain), full(w), pos, pos,
                  full(wup), full(bg)],
        out_specs=[tok(n) for n in widths],
        out_shape=[jax.ShapeDtypeStruct((t, n), dt) for n, dt in zip(widths, dtypes)],
        compiler_params=_cparams("parallel"),
        name="in_proj_odd",
    )(x2, mod3, mod3, gain, w, cos, sin, wup, bg)


RET_CHUNK = 256


def _ret_kernel(q_ref, k_ref, v_ref, sg_ref, dm_ref, xi_ref, zeta_ref, gl_ref, y_ref, st_ref):
    @pl.when(pl.program_id(2) == 0)
    def _():
        st_ref[...] = jnp.zeros_like(st_ref)

    q, k, v = q_ref[...], k_ref[...], v_ref[...]
    st = st_ref[...]
    s = _dot_nt(q, k) * dm_ref[0]
    o = _bdot(s.astype(BF16), v) + _bdot((q.astype(F32) * xi_ref[0]).astype(BF16), st.astype(BF16))
    st_ref[...] = gl_ref[0, 0:1, :] * st + _dot_tn((k.astype(F32) * zeta_ref[0]).astype(BF16), v)
    y_ref[...] = (sg_ref[...] * _rms(o)).astype(y_ref.dtype)


def _retention(rq, rk, rv, sg, bsz, seq):
    t = rq.shape[0]
    ell = min(RET_CHUNK, seq)
    nc = seq // ell
    log_g = jnp.log(1.0 - jnp.exp2(-5.0 - jnp.arange(RET_HEADS, dtype=F32)))
    idx = jnp.arange(ell, dtype=F32)
    rel = idx[:, None] - idx[None, :]
    dmat = jnp.where(rel >= 0, jnp.exp(log_g[:, None, None] * jnp.maximum(rel, 0.0)), 0.0)
    lanes = lambda a: jnp.broadcast_to(a[..., None], a.shape + (RET_DK,))
    xi = lanes(jnp.exp(log_g[:, None] * (idx + 1.0)))
    zeta = lanes(jnp.exp(log_g[:, None] * (ell - 1.0 - idx)))
    gl = jnp.broadcast_to(jnp.exp(log_g * ell)[:, None, None], (RET_HEADS, SUBLANES, RET_DV))
    blk = pl.BlockSpec((ell, RET_DK), lambda b, h, c: (b * nc + c, h))
    per_head = lambda a: pl.BlockSpec((1,) + a.shape[1:], lambda b, h, c: (h, 0, 0))
    return pl.pallas_call(
        _ret_kernel,
        grid=(bsz, RET_HEADS, nc),
        in_specs=[blk, blk, blk, blk, per_head(dmat), per_head(xi), per_head(zeta), per_head(gl)],
        out_specs=blk,
        out_shape=jax.ShapeDtypeStruct((t, RET_HEADS * RET_DV), BF16),
        scratch_shapes=[pltpu.VMEM((RET_DK, RET_DV), F32)],
        compiler_params=_cparams("parallel", "parallel", "arbitrary"),
        name="retention",
    )(rq, rk, rv, sg, dmat, xi, zeta, gl)


def _gla_kernel(q_ref, k_ref, la_ref, v_ref, sg_ref, y_ref, st_ref, b_scr, v_scr, p_scr, r_scr):
    @pl.when(pl.program_id(2) == 0)
    def _():
        st_ref[...] = jnp.zeros_like(st_ref)

    ell, sub = GLA_CHUNK, GLA_SUB
    n_sub = ell // sub
    q, k = q_ref[...], k_ref[...]
    b = _dot01(_lower_tri(ell), la_ref[...])
    b_scr[...] = b
    v_bf = v_ref[...]
    v_scr[...] = v_bf.astype(F32)
    lane = lax.broadcasted_iota(jnp.int32, (1, LANES), 1)
    first = lane < GLA_DK
    head = (first, jnp.logical_not(first))
    pick = lambda h, a: jnp.where(head[h], a, 0.0).astype(BF16)
    vh = [v_bf[:, h * GLA_DV:(h + 1) * GLA_DV] for h in range(2)]
    b_last = b[ell - 1:ell, :]
    st = st_ref[...]
    st_bf = st.astype(BF16)

    qe = q * jnp.exp(b)
    o = [_dot_nt(pick(h, qe), st_bf) for h in range(2)]

    off = [[jnp.zeros((sub, GLA_DV), F32)] for _ in range(2)]
    for i in range(1, n_sub):
        lo = i * sub
        ref_row = b[lo - 1:lo, :]
        qi = q[lo:lo + sub] * jnp.exp(b[lo:lo + sub] - ref_row)
        kj = (k[:lo] * jnp.exp(ref_row - b[:lo])).astype(BF16)
        for h in range(2):
            a = _dot_nt(pick(h, qi), kj)
            off[h].append(_bdot(a.astype(BF16), vh[h][:lo]))

    tau = lax.broadcasted_iota(jnp.int32, (sub, LANES), 0)
    for i in range(n_sub):
        lo = i * sub
        qi, bi = q[lo:lo + sub], b[lo:lo + sub]
        for s in range(sub):
            k_row = k_ref[pl.ds(lo + s, 1), :]
            b_row = b_scr[pl.ds(lo + s, 1), :]
            w = jnp.exp(jnp.minimum(bi - b_row, 0.0))
            p_scr[pl.ds((lo + s) * sub, sub), :] = jnp.where(tau >= s, qi * k_row * w, 0.0).astype(BF16)
    rsub = lax.broadcasted_iota(jnp.int32, (LANES, LANES), 0)
    p_all = p_scr[...]
    for h in range(2):
        ind = ((rsub < GLA_DK) if h == 0 else (rsub >= GLA_DK)).astype(BF16)
        r_scr[h] = _bdot(p_all, ind)
    diag = [[], []]
    for i in range(n_sub):
        lo = i * sub
        for h in range(2):
            acc = jnp.zeros((sub, GLA_DV), F32)
            for s in range(sub):
                v_row = v_scr[pl.ds(lo + s, 1), h * GLA_DV:(h + 1) * GLA_DV]
                acc = acc + r_scr[h, pl.ds((lo + s) * sub, sub), :] * v_row
            diag[h].append(acc)

    kh = (k * jnp.exp(b_last - b)).astype(BF16)
    upd = [_dot_tn(vh[h], kh) for h in range(2)]
    st_ref[...] = st * jnp.exp(b_last) + jnp.where(first, upd[0], upd[1])

    sg = sg_ref[...]
    ys = []
    for h in range(2):
        oh = o[h] + jnp.concatenate(off[h], axis=0) + jnp.concatenate(diag[h], axis=0)
        ys.append(sg[:, h * GLA_DV:(h + 1) * GLA_DV] * _rms(oh))
    y_ref[...] = jnp.concatenate(ys, axis=1).astype(y_ref.dtype)


def _gla(gq, gk, la, gv, sr, bsz, seq):
    t = gq.shape[0]
    ell = GLA_CHUNK
    nc = seq // ell
    pairs = GLA_HEADS // 2
    qk = pl.BlockSpec((ell, LANES), lambda b, p, c: (b * nc + c, p))
    vv = pl.BlockSpec((ell, 2 * GLA_DV), lambda b, p, c: (b * nc + c, p))
    return pl.pallas_call(
        _gla_kernel,
        grid=(bsz, pairs, nc),
        in_specs=[qk, qk, qk, vv, vv],
        out_specs=vv,
        out_shape=jax.ShapeDtypeStruct((t, GLA_HEADS * GLA_DV), BF16),
        scratch_shapes=[pltpu.VMEM((GLA_DV, LANES), F32),
                        pltpu.VMEM((ell, LANES), F32),
                        pltpu.VMEM((ell, 2 * GLA_DV), F32),
                        pltpu.VMEM((ell * GLA_SUB, LANES), BF16),
                        pltpu.VMEM((2, ell * GLA_SUB, LANES), F32)],
        compiler_params=_cparams("parallel", "parallel", "arbitrary"),
        name="gla",
    )(gq, gk, la, gv, sr)


def _post_tail(m, x_ref, g1_ref, gpost_ref, sh2_ref, sc2_ref, gpre_ref, rw_ref, rb_ref,
               x1_ref, h2_ref, topi_ref, gate_ref, rank_ref, cnt_ref, carry_ref):
    i = pl.program_id(0)
    tm = m.shape[0]
    x1 = x_ref[...] + g1_ref[0] * (_rms(m) * gpost_ref[...])
    x1_ref[...] = x1
    h2 = _rms(x1) * (gpre_ref[...] * (1.0 + sc2_ref[0])) + sh2_ref[0]
    for c in range(D_MODEL // LANES):
        h2_ref[pl.ds(c, tm, stride=SUBLANES), :] = h2[:, c * LANES:(c + 1) * LANES]
    logits = jnp.dot(h2, rw_ref[...], preferred_element_type=F32, precision=HIGHEST) + rb_ref[...]

    lane = lax.broadcasted_iota(jnp.int32, (tm, LANES), 1)
    lanef = lane.astype(F32)
    work = logits
    topv = jnp.full((tm, LANES), -jnp.inf, F32)
    topi = jnp.zeros((tm, LANES), F32)
    onehot = jnp.zeros((tm, LANES), F32)
    hits = []
    for k in range(TOP_K):
        mx = jnp.max(work, axis=-1, keepdims=True)
        idx = jnp.min(jnp.where(work == mx, lanef, float(LANES)), axis=-1, keepdims=True)
        hit = lanef == idx
        hits.append(hit)
        topv = jnp.where(lane == k, mx, topv)
        topi = jnp.where(lane == k, idx, topi)
        onehot = onehot + hit.astype(F32)
        work = jnp.where(hit, -jnp.inf, work)
    e = jnp.exp(topv - jnp.max(topv, axis=-1, keepdims=True))
    gate_ref[...] = e / jnp.sum(e, axis=-1, keepdims=True)
    topi_ref[...] = topi.astype(jnp.int32)

    @pl.when(i == 0)
    def _():
        carry_ref[...] = jnp.zeros_like(carry_ref)

    before = _bdot(_lower_tri(tm, strict=True), onehot.astype(BF16)) + carry_ref[...]
    rank = jnp.zeros((tm, LANES), F32)
    for k in range(TOP_K):
        rk = jnp.sum(jnp.where(hits[k], before, 0.0), axis=-1, keepdims=True)
        rank = jnp.where(lane == k, rk, rank)
    rank_ref[...] = rank.astype(jnp.int32)
    total = before[tm - 1:tm, :] + onehot[tm - 1:tm, :]
    carry_ref[...] = total
    cnt_ref[...] = jnp.broadcast_to(total, cnt_ref.shape)


def _out_even_kernel(ys_ref, u_ref, yb_ref, d_ref, gw_ref, gb_ref, wa_ref, wb_ref, *rest):
    y = ys_ref[...] + d_ref[...] * u_ref[...]
    g = jax.nn.gelu(y)
    ya = g * _sigmoid(_bdot(g.astype(BF16), gw_ref[...]) + gb_ref[...])
    m = _bdot(ya.astype(BF16), wa_ref[...]) + _bdot(yb_ref[...], wb_ref[...])
    _post_tail(m, *rest)


def _out_odd_kernel(yc_ref, yd_ref, wa_ref, wb_ref, *rest):
    m = _bdot(yc_ref[...], wa_ref[...]) + _bdot(yd_ref[...], wb_ref[...])
    _post_tail(m, *rest)


def _mixer_out(body, mix_args, mix_specs, x2, mod3, g_post, g_pre, router_w, router_b, seq, tm):
    t, d = x2.shape
    tps = seq // tm
    rw = jnp.pad(router_w.astype(F32), ((0, 0), (0, LANES - N_EXPERTS)))
    rb = jnp.pad(router_b.astype(F32), (0, LANES - N_EXPERTS), constant_values=-1e30).reshape(1, LANES)
    tok = lambda n: pl.BlockSpec((tm, n), lambda i: (i, 0))
    full = lambda a: pl.BlockSpec(a.shape, lambda i: (0,) * a.ndim)
    tail_args = [x2, mod3, g_post, mod3, mod3, g_pre, rw, rb]
    tail_specs = [tok(d), _mod_spec(2, tps), full(g_post), _mod_spec(3, tps), _mod_spec(4, tps),
                  full(g_pre), full(rw), full(rb)]
    return pl.pallas_call(
        body,
        grid=(t // tm,),
        in_specs=mix_specs + tail_specs,
        out_specs=[tok(d), pl.BlockSpec((tm * SUBLANES, LANES), lambda i: (i, 0)),
                   tok(LANES), tok(LANES), tok(LANES), pl.BlockSpec((SUBLANES, LANES), lambda i: (0, 0))],
        out_shape=[jax.ShapeDtypeStruct((t, d), F32),
                   jax.ShapeDtypeStruct((t * SUBLANES, LANES), F32),
                   jax.ShapeDtypeStruct((t, LANES), jnp.int32),
                   jax.ShapeDtypeStruct((t, LANES), F32),
                   jax.ShapeDtypeStruct((t, LANES), jnp.int32),
                   jax.ShapeDtypeStruct((SUBLANES, LANES), F32)],
        scratch_shapes=[pltpu.VMEM((1, LANES), F32)],
        compiler_params=_cparams("arbitrary"),
        name="mixer_out_router",
    )(*mix_args, *tail_args)


def _out_even(ys, u, yb, d_skip, glu_w, glu_b, w_out, *tail, seq, tm):
    tok = lambda n: pl.BlockSpec((tm, n), lambda i: (i, 0))
    full = lambda a: pl.BlockSpec(a.shape, lambda i: (0,) * a.ndim)
    args = [ys, u, yb, d_skip.reshape(1, -1), glu_w.astype(BF16), glu_b.reshape(1, -1),
            w_out[:S5_CH].astype(BF16), w_out[S5_CH:].astype(BF16)]
    specs = [tok(S5_CH), tok(S5_CH), tok(FOX_W)] + [full(a) for a in args[3:]]
    return _mixer_out(_out_even_kernel, args, specs, *tail, seq, tm)


def _out_odd(yc, yd, w_out, *tail, seq, tm):
    tok = lambda n: pl.BlockSpec((tm, n), lambda i: (i, 0))
    full = lambda a: pl.BlockSpec(a.shape, lambda i: (0,) * a.ndim)
    nc = yc.shape[1]
    args = [yc, yd, w_out[:nc].astype(BF16), w_out[nc:].astype(BF16)]
    specs = [tok(nc), tok(yd.shape[1])] + [full(a) for a in args[2:]]
    return _mixer_out(_out_odd_kernel, args, specs, *tail, seq, tm)


def _route_kernel(topi_ref, rank_ref, cnt_ref, dest_ref, blk_ref, meta_ref):
    tm = topi_ref.shape[0]
    cnt = cnt_ref[...]
    padded = jnp.floor((cnt + (MOE_BLOCK - 1.0)) * (1.0 / MOE_BLOCK)) * MOE_BLOCK
    r = lax.broadcasted_iota(jnp.int32, (LANES, LANES), 0)
    c = lax.broadcasted_iota(jnp.int32, (LANES, LANES), 1)
    hi, mid, lo = _split3(padded)
    incl = (r <= c).astype(BF16)
    pad_end = _bdot(hi, incl) + _bdot(mid, incl) + _bdot(lo, incl)
    pad_start = pad_end - padded
    lane = lax.broadcasted_iota(jnp.int32, (tm, LANES), 1)
    lanef = lane.astype(F32)
    topi = topi_ref[...].astype(F32)
    start_row = pad_start[0:1, :]
    dest = jnp.zeros((tm, LANES), F32)
    for k in range(TOP_K):
        idx = jnp.sum(jnp.where(lane == k, topi, 0.0), axis=-1, keepdims=True)
        st = jnp.sum(jnp.where(lanef == idx, start_row, 0.0), axis=-1, keepdims=True)
        dest = jnp.where(lane == k, st, dest)
    dest_ref[...] = dest.astype(jnp.int32) + rank_ref[...]

    nb = blk_ref.shape[1]
    end_col = jnp.sum(jnp.where(r == c, jnp.broadcast_to(pad_end[0:1, :], (LANES, LANES)), 0.0),
                      axis=-1, keepdims=True)
    jpos = lax.broadcasted_iota(jnp.int32, (LANES, nb), 1).astype(F32) * MOE_BLOCK
    esub = lax.broadcasted_iota(jnp.int32, (LANES, nb), 0)
    done = jnp.where((end_col <= jpos) & (esub < N_EXPERTS), 1.0, 0.0)
    be = jnp.minimum(jnp.sum(done, axis=0, keepdims=True), N_EXPERTS - 1.0)
    blk_ref[...] = jnp.broadcast_to(be, blk_ref.shape).astype(jnp.int32)
    lane1 = lax.broadcasted_iota(jnp.int32, (SUBLANES, LANES), 1)
    n_valid = jnp.sum(jnp.where(lane1 == N_EXPERTS - 1, pad_end, 0.0), axis=-1, keepdims=True) * (1.0 / MOE_BLOCK)
    sub1 = lax.broadcasted_iota(jnp.int32, (SUBLANES, LANES), 0)
    meta = jnp.where(sub1 == 0, pad_start + cnt, jnp.where(sub1 == 1, pad_end, jnp.broadcast_to(n_valid, (SUBLANES, LANES))))
    meta_ref[...] = meta.astype(jnp.int32)


def _route(topi, rank, cnt, n_blocks, tm):
    t = topi.shape[0]
    nb_pad = -(-n_blocks // LANES) * LANES
    tok = pl.BlockSpec((tm, LANES), lambda i: (i, 0))
    fix = lambda n: pl.BlockSpec((SUBLANES, n), lambda i: (0, 0))
    return pl.pallas_call(
        _route_kernel,
        grid=(t // tm,),
        in_specs=[tok, tok, fix(LANES)],
        out_specs=[tok, fix(nb_pad), fix(LANES)],
        out_shape=[jax.ShapeDtypeStruct((t, LANES), jnp.int32),
                   jax.ShapeDtypeStruct((SUBLANES, nb_pad), jnp.int32),
                   jax.ShapeDtypeStruct((SUBLANES, LANES), jnp.int32)],
        compiler_params=_cparams("arbitrary"),
        name="route_plan",
    )(topi, rank, cnt)


def _dispatch_kernel(pad_ref, dest_ref, h_ref, xb_ref, zero_ref, sem_z, sem_s):
    i = pl.program_id(0)
    tm = h_ref.shape[0]

    @pl.when(i == 0)
    def _():
        zero_ref[...] = jnp.zeros_like(zero_ref)
        sizes = [1 << b for b in range(int(math.log2(MOE_BLOCK)) - 1, -1, -1)]

        def fill(e, carry, do_wait):
            start = pad_ref[0, e]
            n_pad = pad_ref[1, e] - start
            off = start
            for sz in sizes:
                take = (n_pad & sz) != 0
                cp = pltpu.make_async_copy(zero_ref.at[pl.ds(0, sz)], xb_ref.at[pl.ds(off, sz)], sem_z)

                @pl.when(take)
                def _():
                    if do_wait:
                        cp.wait()
                    else:
                        cp.start()
                off = off + jnp.where(take, sz, 0)
            return carry

        half = zero_ref.shape[0]

        def fill_unused(j, carry, do_wait):
            for part in range(MOE_BLOCK // half):
                cp = pltpu.make_async_copy(zero_ref, xb_ref.at[pl.ds(j * MOE_BLOCK + part * half, half)], sem_z)
                if do_wait:
                    cp.wait()
                else:
                    cp.start()
            return carry

        n_blocks = xb_ref.shape[0] // MOE_BLOCK
        lax.fori_loop(0, N_EXPERTS, lambda e, c: fill(e, c, False), 0)
        lax.fori_loop(pad_ref[2, 0], n_blocks, lambda j, c: fill_unused(j, c, False), 0)
        lax.fori_loop(0, N_EXPERTS, lambda e, c: fill(e, c, True), 0)
        lax.fori_loop(pad_ref[2, 0], n_blocks, lambda j, c: fill_unused(j, c, True), 0)

    def issue(r, carry):
        for k in range(TOP_K):
            pltpu.make_async_copy(h_ref.at[r], xb_ref.at[dest_ref[r * TOP_K + k]], sem_s).start()
        return carry

    lax.fori_loop(0, tm, issue, 0)
    for k in range(TOP_K):
        pltpu.make_async_copy(h_ref, xb_ref.at[pl.ds(0, tm)], sem_s).wait()


def _dispatch(h2t, dest, meta, n_slots, tm):
    t = h2t.shape[0] // SUBLANES
    h3 = h2t.reshape(t, SUBLANES, LANES)
    return pl.pallas_call(
        _dispatch_kernel,
        grid_spec=pltpu.PrefetchScalarGridSpec(
            num_scalar_prefetch=1,
            grid=(t // tm,),
            in_specs=[pl.BlockSpec((tm * TOP_K,), lambda i, p: (i,), memory_space=pltpu.SMEM),
                      pl.BlockSpec((tm, SUBLANES, LANES), lambda i, p: (i, 0, 0))],
            out_specs=pl.BlockSpec(memory_space=pl.ANY),
            scratch_shapes=[pltpu.VMEM((MOE_BLOCK // 2, SUBLANES, LANES), F32),
                            pltpu.SemaphoreType.DMA, pltpu.SemaphoreType.DMA]),
        out_shape=jax.ShapeDtypeStruct((n_slots, SUBLANES, LANES), F32),
        compiler_params=_cparams("arbitrary"),
        name="moe_dispatch",
    )(meta[:3, :N_EXPERTS], dest, h3)


def _expert_kernel(be_ref, nv_ref, x_ref, wgu_ref, bgu_ref, wd_ref, bd_ref, y_ref, wgu_bf, wd_bf):
    j = pl.program_id(0)
    valid = j < nv_ref[0]
    prev = be_ref[jnp.maximum(j - 1, 0)]

    @pl.when(valid & ((j == 0) | (be_ref[j] != prev)))
    def _():
        wgu_bf[...] = wgu_ref[0].astype(BF16)
        wd_bf[...] = wd_ref[0].astype(BF16)

    @pl.when(valid)
    def _():
        x = jnp.concatenate([x_ref[pl.ds(c, MOE_BLOCK, stride=SUBLANES), :] for c in range(D_MODEL // LANES)],
                            axis=1).astype(BF16)
        gu = _bdot(x, wgu_bf[...]) + bgu_ref[0]
        x_glu = jnp.minimum(gu[:, :D_EXPERT], SWIGLU_LIMIT)
        x_lin = jnp.clip(gu[:, D_EXPERT:], -SWIGLU_LIMIT, SWIGLU_LIMIT)
        act = x_glu * _sigmoid(SWIGLU_ALPHA * x_glu) * (x_lin + 1.0)
        y = _bdot(act.astype(BF16), wd_bf[...]) + bd_ref[0]
        for c in range(D_MODEL // LANES):
            y_ref[pl.ds(c, MOE_BLOCK, stride=SUBLANES), :] = y[:, c * LANES:(c + 1) * LANES]

    @pl.when(jnp.logical_not(valid))
    def _():
        y_ref[...] = jnp.zeros_like(y_ref)


def _experts(xb, block_expert, n_valid, w_gu, b_gu, w_down, b_down, layer):
    n_slots = xb.shape[0]
    n_blocks = n_slots // MOE_BLOCK
    rows = MOE_BLOCK * SUBLANES
    x2 = xb.reshape(n_slots * SUBLANES, LANES)
    depth, ne, d, de2 = w_gu.shape
    last = lambda j, be, nv: jnp.minimum(j, nv[0] - 1)
    wmap = lambda j, be, nv: (layer, be[last(j, be, nv)], 0, 0)
    return pl.pallas_call(
        _expert_kernel,
        grid_spec=pltpu.PrefetchScalarGridSpec(
            num_scalar_prefetch=2,
            grid=(n_blocks,),
            in_specs=[pl.BlockSpec((rows, LANES), lambda j, be, nv: (last(j, be, nv), 0)),
                      pl.BlockSpec((None, 1, d, de2), wmap),
                      pl.BlockSpec((None, 1, 1, de2), wmap),
                      pl.BlockSpec((None, 1, de2 // 2, d), wmap),
                      pl.BlockSpec((None, 1, 1, d), wmap)],
            out_specs=pl.BlockSpec((rows, LANES), lambda j, be, nv: (j, 0)),
            scratch_shapes=[pltpu.VMEM((d, de2), BF16), pltpu.VMEM((de2 // 2, d), BF16)]),
        out_shape=jax.ShapeDtypeStruct((n_slots * SUBLANES, LANES), F32),
        compiler_params=_cparams("arbitrary"),
        name="moe_experts",
    )(block_expert, n_valid, x2, w_gu, b_gu.reshape(depth, ne, 1, de2), w_down, b_down.reshape(depth, ne, 1, d))


def _combine_kernel(dest_ref, yb_ref, gate_ref, x1_ref, g2_ref, gpost_ref, o_ref, buf, sem):
    tm = x1_ref.shape[0]

    def issue(r, carry):
        for k in range(TOP_K):
            src = pl.multiple_of(dest_ref[r * TOP_K + k] * SUBLANES, SUBLANES)
            dst = pl.multiple_of((k * tm + r) * SUBLANES, SUBLANES)
            pltpu.make_async_copy(yb_ref.at[pl.ds(src, SUBLANES), :], buf.at[pl.ds(dst, SUBLANES), :], sem).start()
        return carry

    lax.fori_loop(0, tm, issue, 0)
    pltpu.make_async_copy(yb_ref.at[pl.ds(0, TOP_K * tm * SUBLANES), :], buf, sem).wait()
    gates = gate_ref[...]
    gk = [jnp.broadcast_to(gates[:, k:k + 1], (tm, LANES)) for k in range(TOP_K)]
    b2 = buf
    cols = []
    for c in range(D_MODEL // LANES):
        acc = jnp.zeros((tm, LANES), F32)
        for k in range(TOP_K):
            acc = acc + gk[k] * b2[pl.ds(k * tm * SUBLANES + c, tm, stride=SUBLANES), :]
        cols.append(acc)
    f = jnp.concatenate(cols, axis=1)
    o_ref[...] = x1_ref[...] + g2_ref[0] * (_rms(f) * gpost_ref[...])


def _combine(yb, dest, gates, x1, mod3, g_post, seq, tm):
    t, d = x1.shape
    tps = seq // tm
    return pl.pallas_call(
        _combine_kernel,
        grid=(t // tm,),
        in_specs=[pl.BlockSpec((tm * TOP_K,), lambda i: (i,), memory_space=pltpu.SMEM),
                  pl.BlockSpec(memory_space=pl.ANY),
                  pl.BlockSpec((tm, LANES), lambda i: (i, 0)),
                  pl.BlockSpec((tm, d), lambda i: (i, 0)),
                  _mod_spec(5, tps),
                  pl.BlockSpec(g_post.shape, lambda i: (0, 0))],
        out_specs=pl.BlockSpec((tm, d), lambda i: (i, 0)),
        out_shape=jax.ShapeDtypeStruct((t, d), F32),
        scratch_shapes=[pltpu.VMEM((TOP_K * tm * SUBLANES, LANES), F32), pltpu.SemaphoreType.DMA],
        compiler_params=_cparams("arbitrary"),
        name="moe_combine",
    )(dest, yb, gates, x1, mod3, g_post)


def _moe(h2t, topi, gates, rank, cnt, x1, mod3, g_post, w_gu, b_gu, w_down, b_down, layer, seq):
    t = x1.shape[0]
    n_blocks = t * TOP_K // MOE_BLOCK + N_EXPERTS
    dest_l, blk, meta = _route(topi, rank, cnt, n_blocks, min(1024, t))
    dest = dest_l[:, :TOP_K].reshape(t * TOP_K)
    xb = _dispatch(h2t, dest, meta, n_blocks * MOE_BLOCK, MOE_BLOCK)
    yb = _experts(xb, blk[0, :n_blocks], meta[2, :1], w_gu, b_gu, w_down, b_down, layer)
    return _combine(yb, dest, gates, x1, mod3, g_post, seq, MOE_BLOCK)


TOKEN_TILE = 512
FOX_TILE = 256


def kernel(x, c, ada_w, ada_b, norm_pre_mix, norm_post_mix, norm_pre_ffn, norm_post_ffn, ev_w_in, fox_b_f, s5_lam_re, s5_lam_im, s5_log_dt, s5_b_re, s5_b_im, s5_c_re, s5_c_im, s5_d, s5_glu_w, s5_glu_b, ev_w_out, od_w_in, gla_w_up, gla_b_gate, od_w_out, router_w, router_b, exp_w_gu, exp_b_gu, exp_w_down, exp_b_down):
    bsz, seq, d = x.shape
    t = bsz * seq
    tm = min(TOKEN_TILE, seq)
    x2 = x.reshape(t, d)
    mod = _modulation(c, ada_w, ada_b)
    for l in range(DEPTH):
        i = l // 2
        mod3 = mod[l].reshape(bsz, 1, 6 * d)
        row = lambda a: a[l].reshape(1, -1)
        tail = (x2, mod3, row(norm_post_mix), row(norm_pre_ffn), router_w[l], router_b[l])
        if l % 2 == 0:
            u, q, k, v, cum = _in_even(x2, mod3, row(norm_pre_mix), ev_w_in[i], fox_b_f[i], seq, tm)
            tables = _s5_tables(s5_lam_re[i], s5_lam_im[i], s5_log_dt[i], s5_b_re[i], s5_b_im[i],
                                s5_c_re[i], s5_c_im[i])
            ys = _s5_scan(u, bsz, seq, tables)
            yb = _fox(q, k, v, cum, bsz, seq, min(FOX_TILE, seq))
            outs = _out_even(ys, u, yb, s5_d[i], s5_glu_w[i], s5_glu_b[i], ev_w_out[i], *tail, seq=seq, tm=tm)
        else:
            rq, rk, rv, sg, gq, gk, gv, sr, la = _in_odd(x2, mod3, row(norm_pre_mix), od_w_in[i],
                                                         gla_w_up[i], gla_b_gate[i], seq, tm)
            yc = _retention(rq, rk, rv, sg, bsz, seq)
            yd = _gla(gq, gk, la, gv, sr, bsz, seq)
            outs = _out_odd(yc, yd, od_w_out[i], *tail, seq=seq, tm=tm)
        x1, h2t, topi, gates, rank, cnt = outs
        x2 = _moe(h2t, topi, gates, rank, cnt, x1, mod3, row(norm_post_ffn),
                  exp_w_gu, exp_b_gu, exp_w_down, exp_b_down, l, seq)
    return x2.reshape(bsz, seq, d)
```

```python
import functools
import math

import jax
import jax.numpy as jnp
from jax import lax
from jax.experimental import pallas as pl
from jax.experimental.pallas import tpu as pltpu

F32 = jnp.float32
BF16 = jnp.bfloat16
HIGHEST = lax.Precision.HIGHEST

D_MODEL = 1024
DEPTH = 2
EPS = 1e-6
S5_CH = 512
S5_GROUP = 16
S5_GROUPS = S5_CH // S5_GROUP
S5_STATE = 64
S5_CHUNK = 16
FOX_HEADS = 8
FOX_DH = 64
FOX_W = FOX_HEADS * FOX_DH
FOX_HPS = 8
RET_HEADS = 4
RET_DK = 128
RET_DV = 128
ROPE_BASE = 10000.0
GLA_HEADS = 4
GLA_DK = 64
GLA_DV = 128
GLA_RANK = 16
GLA_TAU = 16.0
GLA_CHUNK = 64
GLA_SUB = 16
N_EXPERTS = 32
TOP_K = 4
D_EXPERT = 1024
SWIGLU_LIMIT = 7.0
SWIGLU_ALPHA = 1.702
MOE_BLOCK = 256

LANES = 128
SUBLANES = 8
VMEM_LIMIT = 56 * 1024 * 1024


def _cparams(*sem):
    return pltpu.CompilerParams(dimension_semantics=sem, vmem_limit_bytes=VMEM_LIMIT)


def _bdot(a, b):
    return jnp.dot(a, b, preferred_element_type=F32)


def _dot_nt(a, b):
    return lax.dot_general(a, b, (((1,), (1,)), ((), ())), preferred_element_type=F32)


def _dot_tn(a, b):
    return lax.dot_general(a, b, (((0,), (0,)), ((), ())), preferred_element_type=F32)


def _split3(x):
    hi = x.astype(BF16)
    r = x - hi.astype(F32)
    mid = r.astype(BF16)
    lo = (r - mid.astype(F32)).astype(BF16)
    return hi, mid, lo


def _dot01(m01, x):
    hi, mid, lo = _split3(x)
    return _bdot(m01, hi) + _bdot(m01, mid) + _bdot(m01, lo)


def _lower_tri(n, strict=False):
    r = lax.broadcasted_iota(jnp.int32, (n, n), 0)
    c = lax.broadcasted_iota(jnp.int32, (n, n), 1)
    return ((r > c) if strict else (r >= c)).astype(BF16)


def _log_sigmoid(x):
    return jnp.minimum(x, 0.0) - jnp.log1p(jnp.exp(-jnp.abs(x)))


def _sigmoid(x):
    return 1.0 / (1.0 + jnp.exp(-x))


def _silu(x):
    return x * _sigmoid(x)


def _rms(x):
    return x * lax.rsqrt(jnp.mean(x * x, axis=-1, keepdims=True) + EPS)


def _mod_kernel(c_ref, w_ref, b_ref, o_ref):
    c = c_ref[...]
    o_ref[0] = jnp.dot(_silu(c), w_ref[0], preferred_element_type=F32, precision=HIGHEST) + b_ref[0]


def _modulation(c, ada_w, ada_b):
    depth, d, n = ada_w.shape
    bsz = c.shape[0]
    tn = D_MODEL
    return pl.pallas_call(
        _mod_kernel,
        grid=(depth, n // tn),
        in_specs=[pl.BlockSpec((bsz, d), lambda l, j: (0, 0)),
                  pl.BlockSpec((1, d, tn), lambda l, j: (l, 0, j)),
                  pl.BlockSpec((1, 1, tn), lambda l, j: (l, 0, j))],
        out_specs=pl.BlockSpec((1, bsz, tn), lambda l, j: (l, 0, j)),
        out_shape=jax.ShapeDtypeStruct((depth, bsz, n), F32),
        compiler_params=_cparams("parallel", "parallel"),
        name="adaln_mod",
    )(c, ada_w, ada_b.reshape(depth, 1, n))


def _mod_spec(chunk, tiles_per_seq):
    return pl.BlockSpec((1, 1, D_MODEL), lambda i: (i // tiles_per_seq, 0, chunk))


def _prenorm(x, g_ref, sc_ref, sh_ref):
    return _rms(x) * (g_ref[...] * (1.0 + sc_ref[0])) + sh_ref[0]


def _in_even_kernel(x_ref, sh_ref, sc_ref, g_ref, w_ref, bf_ref,
                    u_ref, q_ref, k_ref, v_ref, carry_ref, *, tiles_per_seq):
    i = pl.program_id(0)
    tm = x_ref.shape[0]
    h = _prenorm(x_ref[...], g_ref, sc_ref, sh_ref)
    z = _bdot(h.astype(BF16), w_ref[...])
    u_ref[...] = z[:, 0:S5_CH]
    v_ref[...] = z[:, S5_CH + 2 * FOX_W:S5_CH + 3 * FOX_W].astype(BF16)
    ls = _log_sigmoid(z[:, S5_CH + 3 * FOX_W:] + bf_ref[...])

    @pl.when(i % tiles_per_seq == 0)
    def _():
        carry_ref[...] = jnp.zeros_like(carry_ref)

    cum = _dot01(_lower_tri(tm), ls) + carry_ref[...]
    carry_ref[...] = cum[tm - 1:tm, :]

    lane = lax.broadcasted_iota(jnp.int32, (1, LANES), 1)
    feat = lane < FOX_DH
    ones = jnp.where(lane < FOX_DH + 3, 1.0, 0.0)
    for hd in range(FOX_HEADS):
        blk = (hd * FOX_DH) // LANES * LANES
        qs = z[:, S5_CH + blk:S5_CH + blk + LANES] * (FOX_DH ** -0.5)
        ks = z[:, S5_CH + FOX_W + blk:S5_CH + FOX_W + blk + LANES]
        if (hd * FOX_DH) % LANES:
            qs = pltpu.roll(qs, LANES - FOX_DH, 1)
            ks = pltpu.roll(ks, LANES - FOX_DH, 1)
        nf = jnp.broadcast_to(-cum[:, hd:hd + 1], (tm, LANES))
        hi = nf.astype(BF16).astype(F32)
        mid = (nf - hi).astype(BF16).astype(F32)
        lo = nf - hi - mid
        bias = jnp.where(lane == FOX_DH, hi, jnp.where(lane == FOX_DH + 1, mid,
                                                       jnp.where(lane == FOX_DH + 2, lo, 0.0)))
        q_ref[:, hd * LANES:(hd + 1) * LANES] = jnp.where(feat, qs, ones).astype(BF16)
        k_ref[:, hd * LANES:(hd + 1) * LANES] = jnp.where(feat, ks, bias).astype(BF16)


def _in_even(x2, mod3, gain, w_in, b_f, seq, tm):
    t, d = x2.shape
    tiles_per_seq = seq // tm
    nw = S5_CH + 3 * FOX_W
    w = jnp.concatenate([w_in[:, :nw], jnp.pad(w_in[:, nw:], ((0, 0), (0, LANES - FOX_HEADS)))],
                        axis=1).astype(BF16)
    bf = jnp.pad(b_f, (0, LANES - FOX_HEADS)).reshape(1, LANES)
    tok = lambda n: pl.BlockSpec((tm, n), lambda i: (i, 0))
    full = lambda a: pl.BlockSpec(a.shape, lambda i: (0,) * a.ndim)
    return pl.pallas_call(
        functools.partial(_in_even_kernel, tiles_per_seq=tiles_per_seq),
        grid=(t // tm,),
        in_specs=[tok(d), _mod_spec(0, tiles_per_seq), _mod_spec(1, tiles_per_seq),
                  full(gain), full(w), full(bf)],
        out_specs=[tok(S5_CH), tok(FOX_HEADS * LANES), tok(FOX_HEADS * LANES), tok(FOX_W)],
        out_shape=[jax.ShapeDtypeStruct((t, S5_CH), F32),
                   jax.ShapeDtypeStruct((t, FOX_HEADS * LANES), BF16),
                   jax.ShapeDtypeStruct((t, FOX_HEADS * LANES), BF16),
                   jax.ShapeDtypeStruct((t, FOX_W), BF16)],
        scratch_shapes=[pltpu.VMEM((1, LANES), F32)],
        compiler_params=_cparams("arbitrary"),
        name="in_proj_even",
    )(x2, mod3, mod3, gain, w, bf)


def _s5_kernel(u_ref, toep_ref, wst_ref, cst_ref, a_ref, y_ref, e_scr, hp_scr, *, n_chunks, nb):
    u = u_ref[0]
    e_scr[...] = _bdot(u, wst_ref[0])
    two_p = 2 * S5_STATE
    a1 = jnp.broadcast_to(a_ref[0, 0:1, :], (nb, two_p))
    a2 = jnp.broadcast_to(a_ref[0, 1:2, :], (nb, two_p))
    a2s = jnp.broadcast_to(a_ref[0, 2:3, :], (nb, two_p))

    def body(c, carry):
        h, hs = carry
        r = pl.multiple_of(c * nb, nb)
        hp_scr[pl.ds(r, nb), :] = h
        e = e_scr[pl.ds(r, nb), :]
        return a1 * h + a2 * hs + e[:, :two_p], a1 * hs + a2s * h + e[:, two_p:]

    zero = jnp.zeros((nb, two_p), F32)
    lax.fori_loop(0, n_chunks, body, (zero, zero))
    y_ref[0] = _bdot(u, toep_ref[0]) + _bdot(hp_scr[...].astype(BF16), cst_ref[0])


def _s5_tables(lam_re, lam_im, log_dt, b_re, b_im, c_re, c_im):
    ell, p, g = S5_CHUNK, S5_STATE, S5_GROUPS
    lr, li = lam_re.astype(F32), lam_im.astype(F32)
    dt = jnp.exp(log_dt.astype(F32))[:, None]
    mag = jnp.exp(lr * dt)
    a_re, a_im = mag * jnp.cos(li * dt), mag * jnp.sin(li * dt)
    den = lr * lr + li * li
    n_re, n_im = a_re - 1.0, a_im
    z_re = (n_re * lr + n_im * li) / den
    z_im = (n_im * lr - n_re * li) / den
    br, bi = b_re.astype(F32), b_im.astype(F32)
    bb_re = z_re[..., None] * br - z_im[..., None] * bi
    bb_im = z_re[..., None] * bi + z_im[..., None] * br
    j = jnp.arange(ell + 1, dtype=F32)[:, None, None]
    pmag = jnp.exp(j * (lr * dt)[None])
    pr, pi = pmag * jnp.cos(j * (li * dt)[None]), pmag * jnp.sin(j * (li * dt)[None])
    w_re = pr[..., None] * bb_re[None] - pi[..., None] * bb_im[None]
    w_im = pr[..., None] * bb_im[None] + pi[..., None] * bb_re[None]
    cr, ci = c_re.astype(F32), c_im.astype(F32)
    kern = (jnp.einsum('gcp,jgpd->jgcd', cr, w_re[:ell], precision=HIGHEST)
            - jnp.einsum('gcp,jgpd->jgcd', ci, w_im[:ell], precision=HIGHEST))
    s_idx = jnp.arange(ell)[:, None]
    t_idx = jnp.arange(ell)[None, :]
    lag = jnp.clip(t_idx - s_idx, 0, ell - 1)
    toep = kern[lag]
    toep = jnp.where((t_idx >= s_idx)[:, :, None, None, None], toep, 0.0)
    toep = toep.transpose(2, 0, 4, 1, 3).reshape(g, ell * S5_GROUP, ell * S5_GROUP)
    rev = jnp.arange(ell - 1, -1, -1)
    st_re = w_re[rev].transpose(1, 0, 3, 2).reshape(g, ell * S5_GROUP, p)
    st_im = w_im[rev].transpose(1, 0, 3, 2).reshape(g, ell * S5_GROUP, p)
    wst = jnp.concatenate([st_re, st_im, st_im, st_re], axis=-1)
    p1r, p1i = pr[1:], pi[1:]
    c_hr = cr[None] * p1r[:, :, None, :] - ci[None] * p1i[:, :, None, :]
    c_hi = -cr[None] * p1i[:, :, None, :] - ci[None] * p1r[:, :, None, :]
    cst = jnp.concatenate([c_hr, c_hi], axis=-1).transpose(1, 3, 0, 2).reshape(g, 2 * p, ell * S5_GROUP)
    al_r, al_i = pr[ell], pi[ell]
    a_rows = jnp.stack([jnp.concatenate([al_r, al_r], -1), jnp.concatenate([-al_i, al_i], -1),
                        jnp.concatenate([al_i, -al_i], -1)], axis=1)
    a_rows = jnp.pad(a_rows, ((0, 0), (0, SUBLANES - 3), (0, 0)))
    return toep.astype(BF16), wst.astype(BF16), cst.astype(BF16), a_rows


def _s5_scan(u, bsz, seq, tables):
    toep, wst, cst, a_rows = tables
    ell, g = S5_CHUNK, S5_GROUPS
    n_chunks = seq // ell
    rows = n_chunks * bsz
    w = ell * S5_GROUP
    ug = (u.reshape(bsz, n_chunks, ell, g, S5_GROUP).transpose(3, 1, 0, 2, 4)
          .reshape(g, rows, w).astype(BF16))
    grp = lambda a: pl.BlockSpec((1,) + a.shape[1:], lambda i: (i, 0, 0))
    yg = pl.pallas_call(
        functools.partial(_s5_kernel, n_chunks=n_chunks, nb=bsz),
        grid=(g,),
        in_specs=[grp(ug), grp(toep), grp(wst), grp(cst), grp(a_rows)],
        out_specs=pl.BlockSpec((1, rows, w), lambda i: (i, 0, 0)),
        out_shape=jax.ShapeDtypeStruct((g, rows, w), F32),
        scratch_shapes=[pltpu.VMEM((rows, 4 * S5_STATE), F32), pltpu.VMEM((rows, 2 * S5_STATE), F32)],
        compiler_params=_cparams("parallel"),
        name="s5_scan",
    )(ug, toep, wst, cst, a_rows)
    return (yg.reshape(g, n_chunks, bsz, ell, S5_GROUP).transpose(2, 1, 3, 0, 4)
            .reshape(bsz * seq, S5_CH))


def _fox_kernel(q_ref, k_ref, vt_ref, o_ref, *, tq, tk):
    i = pl.program_id(2)
    nh = FOX_HPS
    q = [q_ref[:, h * LANES:(h + 1) * LANES] for h in range(nh)]
    key = lax.broadcasted_iota(jnp.int32, (tk, tq), 0)
    qry = lax.broadcasted_iota(jnp.int32, (tk, tq), 1)
    per_q = tq // tk

    def block(j0, carry, mask):
        kj = k_ref[pl.ds(j0, tk), :]
        vtj = vt_ref[:, pl.ds(j0, tk)]
        ss = [_dot_nt(kj[:, h * LANES:(h + 1) * LANES], q[h]) for h in range(nh)]
        stats = []
        for h in range(nh):
            m, l, acc = carry[h]
            s = ss[h] if mask is None else jnp.where(mask, ss[h], -jnp.inf)
            m_new = jnp.maximum(m, jnp.max(s, axis=0, keepdims=True))
            p = jnp.exp(s - m_new)
            alpha = jnp.exp(m - m_new)
            stats.append((m_new, alpha * l + jnp.sum(p, axis=0, keepdims=True), alpha, p.astype(BF16)))
        out = []
        for h in range(nh):
            m_new, l, alpha, p = stats[h]
            acc = alpha * carry[h][2] + _bdot(vtj[h * FOX_DH:(h + 1) * FOX_DH, :], p)
            out.append((m_new, l, acc))
        return tuple(out)

    init = tuple((jnp.full((1, tq), -jnp.inf, F32), jnp.zeros((1, tq), F32), jnp.zeros((FOX_DH, tq), F32))
                 for _ in range(nh))
    carry = lax.fori_loop(0, i * per_q, lambda j, c: block(pl.multiple_of(j * tk, tk), c, None), init)
    for d in range(per_q):
        carry = block(pl.multiple_of(i * tq + d * tk, tk), carry, key + d * tk <= qry)
    for g in range(nh // 2):
        o_t = jnp.concatenate([carry[h][2] / carry[h][1] for h in (2 * g, 2 * g + 1)], axis=0)
        o_ref[:, g * LANES:(g + 1) * LANES] = o_t.T.astype(o_ref.dtype)


def _fox(q_aug, k_aug, v, bsz, seq, tq, tk):
    t = v.shape[0]
    nh = FOX_HPS
    groups = FOX_HEADS // nh
    nq = seq // tq
    v_t = v.reshape(bsz, seq, FOX_W).transpose(0, 2, 1).reshape(bsz * FOX_W, seq)
    return pl.pallas_call(
        functools.partial(_fox_kernel, tq=tq, tk=tk),
        grid=(bsz, groups, nq),
        in_specs=[pl.BlockSpec((tq, nh * LANES), lambda b, p, i: (b * nq + i, p)),
                  pl.BlockSpec((seq, nh * LANES), lambda b, p, i: (b, p)),
                  pl.BlockSpec((nh * FOX_DH, seq), lambda b, p, i: (b * groups + p, 0))],
        out_specs=pl.BlockSpec((tq, nh * FOX_DH), lambda b, p, i: (b * nq + i, p)),
        out_shape=jax.ShapeDtypeStruct((t, FOX_W), BF16),
        compiler_params=_cparams("parallel", "parallel", "arbitrary"),
        name="fox_attention",
    )(q_aug, k_aug, v_t)


_ODD_COLS = (("rq", 512), ("rk", 512), ("rv", 512), ("rg", 512), ("gq", 256), ("gk", 256),
             ("gv", 512), ("gr", 512), ("glr", LANES))


def _odd_offsets():
    off, out = 0, {}
    for name, w in _ODD_COLS:
        out[name] = (off, off + w)
        off += w
    return out, off


def _in_odd_kernel(x_ref, sh_ref, sc_ref, g_ref, w_ref, cos_ref, sin_ref, wup_ref, bg_ref,
                   rq_ref, rk_ref, rv_ref, sg_ref, gq_ref, gk_ref, gv_ref, sr_ref, la_ref):
    h = _prenorm(x_ref[...], g_ref, sc_ref, sh_ref)
    z = _bdot(h.astype(BF16), w_ref[...])
    off, _ = _odd_offsets()
    col = lambda n: z[:, off[n][0]:off[n][1]]
    cos, sin = cos_ref[...], sin_ref[...]

    def rope(t, scale):
        heads = []
        for hd in range(RET_HEADS):
            th = t[:, hd * RET_DK:(hd + 1) * RET_DK]
            heads.append((th * cos + pltpu.roll(th, RET_DK // 2, 1) * sin) * scale)
        return jnp.concatenate(heads, axis=1).astype(BF16)

    rq_ref[...] = rope(col("rq"), 1.0)
    rk_ref[...] = rope(col("rk"), RET_DK ** -0.5)
    rv_ref[...] = col("rv").astype(BF16)
    sg_ref[...] = _silu(col("rg"))
    gq_ref[...] = col("gq") * (GLA_DK ** -0.5)
    gk_ref[...] = col("gk")
    gv_ref[...] = col("gv").astype(BF16)
    sr_ref[...] = _silu(col("gr"))
    gate = jnp.dot(col("glr"), wup_ref[...], preferred_element_type=F32, precision=HIGHEST) + bg_ref[...]
    la_ref[...] = _log_sigmoid(gate) * (1.0 / GLA_TAU)


def _in_odd(x2, mod3, gain, w_in, w_up, b_gate, seq, tm):
    t, d = x2.shape
    tps = seq // tm
    ref_w = (512, 512, 512, 512, 256, 256, 512, GLA_RANK, 512)
    starts = [0]
    for wd in ref_w:
        starts.append(starts[-1] + wd)
    seg = lambda j: w_in[:, starts[j]:starts[j + 1]]
    w = jnp.concatenate([seg(0), seg(1), seg(2), seg(3), seg(4), seg(5), seg(6), seg(8),
                         jnp.pad(seg(7), ((0, 0), (0, LANES - GLA_RANK)))], axis=1).astype(BF16)
    wup = jnp.pad(w_up.astype(F32), ((0, LANES - GLA_RANK), (0, 0)))
    bg = b_gate.reshape(1, -1).astype(F32)
    half = RET_DK // 2
    inv = ROPE_BASE ** (-jnp.arange(half, dtype=F32) / half)
    ang = jnp.arange(seq, dtype=F32)[:, None] * inv[None, :]
    cos = jnp.concatenate([jnp.cos(ang), jnp.cos(ang)], axis=1)
    sin = jnp.concatenate([-jnp.sin(ang), jnp.sin(ang)], axis=1)
    tok = lambda n: pl.BlockSpec((tm, n), lambda i: (i, 0))
    full = lambda a: pl.BlockSpec(a.shape, lambda i: (0,) * a.ndim)
    pos = pl.BlockSpec((tm, RET_DK), lambda i: (i % tps, 0))
    widths = (512, 512, 512, 512, 256, 256, 512, 512, 256)
    dtypes = (BF16, BF16, BF16, F32, F32, F32, BF16, F32, F32)
    return pl.pallas_call(
        _in_odd_kernel,
        grid=(t // tm,),
        in_specs=[tok(d), _mod_spec(0, tps), _mod_spec(1, tps), full(gain), full(w), pos, pos,
                  full(wup), full(bg)],
        out_specs=[tok(n) for n in widths],
        out_shape=[jax.ShapeDtypeStruct((t, n), dt) for n, dt in zip(widths, dtypes)],
        compiler_params=_cparams("parallel"),
        name="in_proj_odd",
    )(x2, mod3, mod3, gain, w, cos, sin, wup, bg)


RET_CHUNK = 256


def _ret_kernel(q_ref, k_ref, v_ref, sg_ref, dm_ref, xi_ref, zeta_ref, gl_ref, y_ref, st_ref):
    @pl.when(pl.program_id(2) == 0)
    def _():
        st_ref[...] = jnp.zeros_like(st_ref)

    q, k, v = q_ref[...], k_ref[...], v_ref[...]
    st = st_ref[...]
    s = _dot_nt(q, k) * dm_ref[0]
    o = _bdot(s.astype(BF16), v) + _bdot((q.astype(F32) * xi_ref[0]).astype(BF16), st.astype(BF16))
    st_ref[...] = gl_ref[0, 0:1, :] * st + _dot_tn((k.astype(F32) * zeta_ref[0]).astype(BF16), v)
    y_ref[...] = (sg_ref[...] * _rms(o)).astype(y_ref.dtype)


def _retention(rq, rk, rv, sg, bsz, seq):
    t = rq.shape[0]
    ell = min(RET_CHUNK, seq)
    nc = seq // ell
    log_g = jnp.log(1.0 - jnp.exp2(-5.0 - jnp.arange(RET_HEADS, dtype=F32)))
    idx = jnp.arange(ell, dtype=F32)
    rel = idx[:, None] - idx[None, :]
    dmat = jnp.where(rel >= 0, jnp.exp(log_g[:, None, None] * jnp.maximum(rel, 0.0)), 0.0)
    lanes = lambda a: jnp.broadcast_to(a[..., None], a.shape + (RET_DK,))
    xi = lanes(jnp.exp(log_g[:, None] * (idx + 1.0)))
    zeta = lanes(jnp.exp(log_g[:, None] * (ell - 1.0 - idx)))
    gl = jnp.broadcast_to(jnp.exp(log_g * ell)[:, None, None], (RET_HEADS, SUBLANES, RET_DV))
    blk = pl.BlockSpec((ell, RET_DK), lambda b, h, c: (b * nc + c, h))
    per_head = lambda a: pl.BlockSpec((1,) + a.shape[1:], lambda b, h, c: (h, 0, 0))
    return pl.pallas_call(
        _ret_kernel,
        grid=(bsz, RET_HEADS, nc),
        in_specs=[blk, blk, blk, blk, per_head(dmat), per_head(xi), per_head(zeta), per_head(gl)],
        out_specs=blk,
        out_shape=jax.ShapeDtypeStruct((t, RET_HEADS * RET_DV), BF16),
        scratch_shapes=[pltpu.VMEM((RET_DK, RET_DV), F32)],
        compiler_params=_cparams("parallel", "parallel", "arbitrary"),
        name="retention",
    )(rq, rk, rv, sg, dmat, xi, zeta, gl)


def _gla_kernel(q_ref, k_ref, la_ref, v_ref, sg_ref, y_ref, st_ref, b_scr, v_scr, p_scr, r_scr):
    @pl.when(pl.program_id(2) == 0)
    def _():
        st_ref[...] = jnp.zeros_like(st_ref)

    ell, sub = GLA_CHUNK, GLA_SUB
    n_sub = ell // sub
    q, k = q_ref[...], k_ref[...]
    b = _dot01(_lower_tri(ell), la_ref[...])
    b_scr[...] = b
    v_bf = v_ref[...]
    v_scr[...] = v_bf.astype(F32)
    lane = lax.broadcasted_iota(jnp.int32, (1, LANES), 1)
    first = lane < GLA_DK
    head = (first, jnp.logical_not(first))
    pick = lambda h, a: jnp.where(head[h], a, 0.0).astype(BF16)
    vh = [v_bf[:, h * GLA_DV:(h + 1) * GLA_DV] for h in range(2)]
    b_last = b[ell - 1:ell, :]
    st = st_ref[...]
    st_bf = st.astype(BF16)

    qe = q * jnp.exp(b)
    o = [_dot_nt(pick(h, qe), st_bf) for h in range(2)]

    off = [[jnp.zeros((sub, GLA_DV), F32)] for _ in range(2)]
    for i in range(1, n_sub):
        lo = i * sub
        ref_row = b[lo - 1:lo, :]
        qi = q[lo:lo + sub] * jnp.exp(b[lo:lo + sub] - ref_row)
        kj = (k[:lo] * jnp.exp(ref_row - b[:lo])).astype(BF16)
        for h in range(2):
            a = _dot_nt(pick(h, qi), kj)
            off[h].append(_bdot(a.astype(BF16), vh[h][:lo]))

    tau = lax.broadcasted_iota(jnp.int32, (sub, LANES), 0)
    for i in range(n_sub):
        lo = i * sub
        qi, bi = q[lo:lo + sub], b[lo:lo + sub]
        for s in range(sub):
            k_row = k_ref[pl.ds(lo + s, 1), :]
            b_row = b_scr[pl.ds(lo + s, 1), :]
            w = jnp.exp(jnp.minimum(bi - b_row, 0.0))
            p_scr[pl.ds((lo + s) * sub, sub), :] = jnp.where(tau >= s, qi * k_row * w, 0.0).astype(BF16)
    rsub = lax.broadcasted_iota(jnp.int32, (LANES, LANES), 0)
    p_all = p_scr[...]
    for h in range(2):
        ind = ((rsub < GLA_DK) if h == 0 else (rsub >= GLA_DK)).astype(BF16)
        r_scr[h] = _bdot(p_all, ind)
    diag = [[], []]
    for i in range(n_sub):
        lo = i * sub
        for h in range(2):
            acc = jnp.zeros((sub, GLA_DV), F32)
            for s in range(sub):
                v_row = v_scr[pl.ds(lo + s, 1), h * GLA_DV:(h + 1) * GLA_DV]
                acc = acc + r_scr[h, pl.ds((lo + s) * sub, sub), :] * v_row
            diag[h].append(acc)

    kh = (k * jnp.exp(b_last - b)).astype(BF16)
    upd = [_dot_tn(vh[h], kh) for h in range(2)]
    st_ref[...] = st * jnp.exp(b_last) + jnp.where(first, upd[0], upd[1])

    sg = sg_ref[...]
    ys = []
    for h in range(2):
        oh = o[h] + jnp.concatenate(off[h], axis=0) + jnp.concatenate(diag[h], axis=0)
        ys.append(sg[:, h * GLA_DV:(h + 1) * GLA_DV] * _rms(oh))
    y_ref[...] = jnp.concatenate(ys, axis=1).astype(y_ref.dtype)


def _gla(gq, gk, la, gv, sr, bsz, seq):
    t = gq.shape[0]
    ell = GLA_CHUNK
    nc = seq // ell
    pairs = GLA_HEADS // 2
    qk = pl.BlockSpec((ell, LANES), lambda b, p, c: (b * nc + c, p))
    vv = pl.BlockSpec((ell, 2 * GLA_DV), lambda b, p, c: (b * nc + c, p))
    return pl.pallas_call(
        _gla_kernel,
        grid=(bsz, pairs, nc),
        in_specs=[qk, qk, qk, vv, vv],
        out_specs=vv,
        out_shape=jax.ShapeDtypeStruct((t, GLA_HEADS * GLA_DV), BF16),
        scratch_shapes=[pltpu.VMEM((GLA_DV, LANES), F32),
                        pltpu.VMEM((ell, LANES), F32),
                        pltpu.VMEM((ell, 2 * GLA_DV), F32),
                        pltpu.VMEM((ell * GLA_SUB, LANES), BF16),
                        pltpu.VMEM((2, ell * GLA_SUB, LANES), F32)],
        compiler_params=_cparams("parallel", "parallel", "arbitrary"),
        name="gla",
    )(gq, gk, la, gv, sr)


def _post_tail(m, x_ref, g1_ref, gpost_ref, sh2_ref, sc2_ref, gpre_ref, rw_ref, rb_ref,
               x1_ref, h2_ref, topi_ref, gate_ref, rank_ref, cnt_ref, carry_ref):
    i = pl.program_id(0)
    tm = m.shape[0]
    x1 = x_ref[...] + g1_ref[0] * (_rms(m) * gpost_ref[...])
    x1_ref[...] = x1
    h2 = _rms(x1) * (gpre_ref[...] * (1.0 + sc2_ref[0])) + sh2_ref[0]
    for c in range(D_MODEL // LANES):
        h2_ref[pl.ds(c, tm, stride=SUBLANES), :] = h2[:, c * LANES:(c + 1) * LANES]
    logits = jnp.dot(h2, rw_ref[...], preferred_element_type=F32, precision=HIGHEST) + rb_ref[...]

    lane = lax.broadcasted_iota(jnp.int32, (tm, LANES), 1)
    lanef = lane.astype(F32)
    work = logits
    topv = jnp.full((tm, LANES), -jnp.inf, F32)
    topi = jnp.zeros((tm, LANES), F32)
    onehot = jnp.zeros((tm, LANES), F32)
    hits = []
    for k in range(TOP_K):
        mx = jnp.max(work, axis=-1, keepdims=True)
        idx = jnp.min(jnp.where(work == mx, lanef, float(LANES)), axis=-1, keepdims=True)
        hit = lanef == idx
        hits.append(hit)
        topv = jnp.where(lane == k, mx, topv)
        topi = jnp.where(lane == k, idx, topi)
        onehot = onehot + hit.astype(F32)
        work = jnp.where(hit, -jnp.inf, work)
    e = jnp.exp(topv - jnp.max(topv, axis=-1, keepdims=True))
    gate_ref[...] = e / jnp.sum(e, axis=-1, keepdims=True)
    topi_ref[...] = topi.astype(jnp.int32)

    @pl.when(i == 0)
    def _():
        carry_ref[...] = jnp.zeros_like(carry_ref)

    before = _bdot(_lower_tri(tm, strict=True), onehot.astype(BF16)) + carry_ref[...]
    rank = jnp.zeros((tm, LANES), F32)
    for k in range(TOP_K):
        rk = jnp.sum(jnp.where(hits[k], before, 0.0), axis=-1, keepdims=True)
        rank = jnp.where(lane == k, rk, rank)
    rank_ref[...] = rank.astype(jnp.int32)
    total = before[tm - 1:tm, :] + onehot[tm - 1:tm, :]
    carry_ref[...] = total
    cnt_ref[...] = jnp.broadcast_to(total, cnt_ref.shape)


def _out_even_kernel(ys_ref, u_ref, yb_ref, d_ref, gw_ref, gb_ref, wa_ref, wb_ref, *rest):
    y = ys_ref[...] + d_ref[...] * u_ref[...]
    g = jax.nn.gelu(y)
    ya = g * _sigmoid(_bdot(g.astype(BF16), gw_ref[...]) + gb_ref[...])
    m = _bdot(ya.astype(BF16), wa_ref[...]) + _bdot(yb_ref[...], wb_ref[...])
    _post_tail(m, *rest)


def _out_odd_kernel(yc_ref, yd_ref, wa_ref, wb_ref, *rest):
    m = _bdot(yc_ref[...], wa_ref[...]) + _bdot(yd_ref[...], wb_ref[...])
    _post_tail(m, *rest)


def _mixer_out(body, mix_args, mix_specs, x2, mod3, g_post, g_pre, router_w, router_b, seq, tm):
    t, d = x2.shape
    tps = seq // tm
    rw = jnp.pad(router_w.astype(F32), ((0, 0), (0, LANES - N_EXPERTS)))
    rb = jnp.pad(router_b.astype(F32), (0, LANES - N_EXPERTS), constant_values=-1e30).reshape(1, LANES)
    tok = lambda n: pl.BlockSpec((tm, n), lambda i: (i, 0))
    full = lambda a: pl.BlockSpec(a.shape, lambda i: (0,) * a.ndim)
    tail_args = [x2, mod3, g_post, mod3, mod3, g_pre, rw, rb]
    tail_specs = [tok(d), _mod_spec(2, tps), full(g_post), _mod_spec(3, tps), _mod_spec(4, tps),
                  full(g_pre), full(rw), full(rb)]
    return pl.pallas_call(
        body,
        grid=(t // tm,),
        in_specs=mix_specs + tail_specs,
        out_specs=[tok(d), pl.BlockSpec((tm * SUBLANES, LANES), lambda i: (i, 0)),
                   tok(LANES), tok(LANES), tok(LANES), pl.BlockSpec((SUBLANES, LANES), lambda i: (0, 0))],
        out_shape=[jax.ShapeDtypeStruct((t, d), F32),
                   jax.ShapeDtypeStruct((t * SUBLANES, LANES), F32),
                   jax.ShapeDtypeStruct((t, LANES), jnp.int32),
                   jax.ShapeDtypeStruct((t, LANES), F32),
                   jax.ShapeDtypeStruct((t, LANES), jnp.int32),
                   jax.ShapeDtypeStruct((SUBLANES, LANES), F32)],
        scratch_shapes=[pltpu.VMEM((1, LANES), F32)],
        compiler_params=_cparams("arbitrary"),
        name="mixer_out_router",
    )(*mix_args, *tail_args)


def _out_even(ys, u, yb, d_skip, glu_w, glu_b, w_out, *tail, seq, tm):
    tok = lambda n: pl.BlockSpec((tm, n), lambda i: (i, 0))
    full = lambda a: pl.BlockSpec(a.shape, lambda i: (0,) * a.ndim)
    args = [ys, u, yb, d_skip.reshape(1, -1), glu_w.astype(BF16), glu_b.reshape(1, -1),
            w_out[:S5_CH].astype(BF16), w_out[S5_CH:].astype(BF16)]
    specs = [tok(S5_CH), tok(S5_CH), tok(FOX_W)] + [full(a) for a in args[3:]]
    return _mixer_out(_out_even_kernel, args, specs, *tail, seq, tm)


def _out_odd(yc, yd, w_out, *tail, seq, tm):
    tok = lambda n: pl.BlockSpec((tm, n), lambda i: (i, 0))
    full = lambda a: pl.BlockSpec(a.shape, lambda i: (0,) * a.ndim)
    nc = yc.shape[1]
    args = [yc, yd, w_out[:nc].astype(BF16), w_out[nc:].astype(BF16)]
    specs = [tok(nc), tok(yd.shape[1])] + [full(a) for a in args[2:]]
    return _mixer_out(_out_odd_kernel, args, specs, *tail, seq, tm)


def _route_kernel(topi_ref, rank_ref, cnt_ref, dest_ref, blk_ref, meta_ref):
    tm = topi_ref.shape[0]
    cnt = cnt_ref[...]
    padded = jnp.floor((cnt + (MOE_BLOCK - 1.0)) * (1.0 / MOE_BLOCK)) * MOE_BLOCK
    r = lax.broadcasted_iota(jnp.int32, (LANES, LANES), 0)
    c = lax.broadcasted_iota(jnp.int32, (LANES, LANES), 1)
    hi, mid, lo = _split3(padded)
    incl = (r <= c).astype(BF16)
    pad_end = _bdot(hi, incl) + _bdot(mid, incl) + _bdot(lo, incl)
    pad_start = pad_end - padded
    lane = lax.broadcasted_iota(jnp.int32, (tm, LANES), 1)
    lanef = lane.astype(F32)
    topi = topi_ref[...].astype(F32)
    start_row = pad_start[0:1, :]
    dest = jnp.zeros((tm, LANES), F32)
    for k in range(TOP_K):
        idx = jnp.sum(jnp.where(lane == k, topi, 0.0), axis=-1, keepdims=True)
        st = jnp.sum(jnp.where(lanef == idx, start_row, 0.0), axis=-1, keepdims=True)
        dest = jnp.where(lane == k, st, dest)
    dest_ref[...] = dest.astype(jnp.int32) + rank_ref[...]

    nb = blk_ref.shape[1]
    end_col = jnp.sum(jnp.where(r == c, jnp.broadcast_to(pad_end[0:1, :], (LANES, LANES)), 0.0),
                      axis=-1, keepdims=True)
    jpos = lax.broadcasted_iota(jnp.int32, (LANES, nb), 1).astype(F32) * MOE_BLOCK
    esub = lax.broadcasted_iota(jnp.int32, (LANES, nb), 0)
    done = jnp.where((end_col <= jpos) & (esub < N_EXPERTS), 1.0, 0.0)
    be = jnp.minimum(jnp.sum(done, axis=0, keepdims=True), N_EXPERTS - 1.0)
    blk_ref[...] = jnp.broadcast_to(be, blk_ref.shape).astype(jnp.int32)
    lane1 = lax.broadcasted_iota(jnp.int32, (SUBLANES, LANES), 1)
    n_valid = jnp.sum(jnp.where(lane1 == N_EXPERTS - 1, pad_end, 0.0), axis=-1, keepdims=True) * (1.0 / MOE_BLOCK)
    sub1 = lax.broadcasted_iota(jnp.int32, (SUBLANES, LANES), 0)
    meta = jnp.where(sub1 == 0, pad_start + cnt, jnp.where(sub1 == 1, pad_end, jnp.broadcast_to(n_valid, (SUBLANES, LANES))))
    meta_ref[...] = meta.astype(jnp.int32)


def _route(topi, rank, cnt, n_blocks, tm):
    t = topi.shape[0]
    nb_pad = -(-n_blocks // LANES) * LANES
    tok = pl.BlockSpec((tm, LANES), lambda i: (i, 0))
    fix = lambda n: pl.BlockSpec((SUBLANES, n), lambda i: (0, 0))
    return pl.pallas_call(
        _route_kernel,
        grid=(t // tm,),
        in_specs=[tok, tok, fix(LANES)],
        out_specs=[tok, fix(nb_pad), fix(LANES)],
        out_shape=[jax.ShapeDtypeStruct((t, LANES), jnp.int32),
                   jax.ShapeDtypeStruct((SUBLANES, nb_pad), jnp.int32),
                   jax.ShapeDtypeStruct((SUBLANES, LANES), jnp.int32)],
        compiler_params=_cparams("arbitrary"),
        name="route_plan",
    )(topi, rank, cnt)


def _dispatch_kernel(pad_ref, dest_ref, h_ref, xb_ref, zero_ref, sem_z, sem_s):
    i = pl.program_id(0)
    tm = h_ref.shape[0]

    @pl.when(i == 0)
    def _():
        zero_ref[...] = jnp.zeros_like(zero_ref)
        sizes = [1 << b for b in range(int(math.log2(MOE_BLOCK)) - 1, -1, -1)]

        def fill(e, carry, do_wait):
            start = pad_ref[0, e]
            n_pad = pad_ref[1, e] - start
            off = start
            for sz in sizes:
                take = (n_pad & sz) != 0
                cp = pltpu.make_async_copy(zero_ref.at[pl.ds(0, sz)], xb_ref.at[pl.ds(off, sz)], sem_z)

                @pl.when(take)
                def _():
                    if do_wait:
                        cp.wait()
                    else:
                        cp.start()
                off = off + jnp.where(take, sz, 0)
            return carry

        half = zero_ref.shape[0]

        def fill_unused(j, carry, do_wait):
            for part in range(MOE_BLOCK // half):
                cp = pltpu.make_async_copy(zero_ref, xb_ref.at[pl.ds(j * MOE_BLOCK + part * half, half)], sem_z)
                if do_wait:
                    cp.wait()
                else:
                    cp.start()
            return carry

        n_blocks = xb_ref.shape[0] // MOE_BLOCK
        lax.fori_loop(0, N_EXPERTS, lambda e, c: fill(e, c, False), 0)
        lax.fori_loop(pad_ref[2, 0], n_blocks, lambda j, c: fill_unused(j, c, False), 0)
        lax.fori_loop(0, N_EXPERTS, lambda e, c: fill(e, c, True), 0)
        lax.fori_loop(pad_ref[2, 0], n_blocks, lambda j, c: fill_unused(j, c, True), 0)

    def issue(r, carry):
        for k in range(TOP_K):
            pltpu.make_async_copy(h_ref.at[r], xb_ref.at[dest_ref[r * TOP_K + k]], sem_s).start()
        return carry

    lax.fori_loop(0, tm, issue, 0)
    for k in range(TOP_K):
        pltpu.make_async_copy(h_ref, xb_ref.at[pl.ds(0, tm)], sem_s).wait()


def _dispatch(h2t, dest, meta, n_slots, tm):
    t = h2t.shape[0] // SUBLANES
    h3 = h2t.reshape(t, SUBLANES, LANES)
    return pl.pallas_call(
        _dispatch_kernel,
        grid_spec=pltpu.PrefetchScalarGridSpec(
            num_scalar_prefetch=1,
            grid=(t // tm,),
            in_specs=[pl.BlockSpec((tm * TOP_K,), lambda i, p: (i,), memory_space=pltpu.SMEM),
                      pl.BlockSpec((tm, SUBLANES, LANES), lambda i, p: (i, 0, 0))],
            out_specs=pl.BlockSpec(memory_space=pl.ANY),
            scratch_shapes=[pltpu.VMEM((MOE_BLOCK // 2, SUBLANES, LANES), F32),
                            pltpu.SemaphoreType.DMA, pltpu.SemaphoreType.DMA]),
        out_shape=jax.ShapeDtypeStruct((n_slots, SUBLANES, LANES), F32),
        compiler_params=_cparams("arbitrary"),
        name="moe_dispatch",
    )(meta[:3, :N_EXPERTS], dest, h3)


def _expert_kernel(be_ref, nv_ref, x_ref, wgu_ref, bgu_ref, wd_ref, bd_ref, y_ref, wgu_bf, wd_bf):
    j = pl.program_id(0)
    valid = j < nv_ref[0]
    prev = be_ref[jnp.maximum(j - 1, 0)]

    @pl.when(valid & ((j == 0) | (be_ref[j] != prev)))
    def _():
        wgu_bf[...] = wgu_ref[0].astype(BF16)
        wd_bf[...] = wd_ref[0].astype(BF16)

    @pl.when(valid)
    def _():
        x = jnp.concatenate([x_ref[pl.ds(c, MOE_BLOCK, stride=SUBLANES), :] for c in range(D_MODEL // LANES)],
                            axis=1).astype(BF16)
        gu = _bdot(x, wgu_bf[...]) + bgu_ref[0]
        x_glu = jnp.minimum(gu[:, :D_EXPERT], SWIGLU_LIMIT)
        x_lin = jnp.clip(gu[:, D_EXPERT:], -SWIGLU_LIMIT, SWIGLU_LIMIT)
        act = x_glu * _sigmoid(SWIGLU_ALPHA * x_glu) * (x_lin + 1.0)
        y = _bdot(act.astype(BF16), wd_bf[...]) + bd_ref[0]
        for c in range(D_MODEL // LANES):
            y_ref[pl.ds(c, MOE_BLOCK, stride=SUBLANES), :] = y[:, c * LANES:(c + 1) * LANES]

    @pl.when(jnp.logical_not(valid))
    def _():
        y_ref[...] = jnp.zeros_like(y_ref)


def _experts(xb, block_expert, n_valid, w_gu, b_gu, w_down, b_down, layer):
    n_slots = xb.shape[0]
    n_blocks = n_slots // MOE_BLOCK
    rows = MOE_BLOCK * SUBLANES
    x2 = xb.reshape(n_slots * SUBLANES, LANES)
    depth, ne, d, de2 = w_gu.shape
    last = lambda j, be, nv: jnp.minimum(j, nv[0] - 1)
    wmap = lambda j, be, nv: (layer, be[last(j, be, nv)], 0, 0)
    return pl.pallas_call(
        _expert_kernel,
        grid_spec=pltpu.PrefetchScalarGridSpec(
            num_scalar_prefetch=2,
            grid=(n_blocks,),
            in_specs=[pl.BlockSpec((rows, LANES), lambda j, be, nv: (last(j, be, nv), 0)),
                      pl.BlockSpec((None, 1, d, de2), wmap),
                      pl.BlockSpec((None, 1, 1, de2), wmap),
                      pl.BlockSpec((None, 1, de2 // 2, d), wmap),
                      pl.BlockSpec((None, 1, 1, d), wmap)],
            out_specs=pl.BlockSpec((rows, LANES), lambda j, be, nv: (j, 0)),
            scratch_shapes=[pltpu.VMEM((d, de2), BF16), pltpu.VMEM((de2 // 2, d), BF16)]),
        out_shape=jax.ShapeDtypeStruct((n_slots * SUBLANES, LANES), F32),
        compiler_params=_cparams("arbitrary"),
        name="moe_experts",
    )(block_expert, n_valid, x2, w_gu, b_gu.reshape(depth, ne, 1, de2), w_down, b_down.reshape(depth, ne, 1, d))


def _combine_kernel(dest_ref, yb_ref, gate_ref, x1_ref, g2_ref, gpost_ref, o_ref, buf, sem):
    tm = x1_ref.shape[0]

    def issue(r, carry):
        for k in range(TOP_K):
            src = pl.multiple_of(dest_ref[r * TOP_K + k] * SUBLANES, SUBLANES)
            dst = pl.multiple_of((k * tm + r) * SUBLANES, SUBLANES)
            pltpu.make_async_copy(yb_ref.at[pl.ds(src, SUBLANES), :], buf.at[pl.ds(dst, SUBLANES), :], sem).start()
        return carry

    lax.fori_loop(0, tm, issue, 0)
    pltpu.make_async_copy(yb_ref.at[pl.ds(0, TOP_K * tm * SUBLANES), :], buf, sem).wait()
    gates = gate_ref[...]
    gk = [jnp.broadcast_to(gates[:, k:k + 1], (tm, LANES)) for k in range(TOP_K)]
    b2 = buf
    cols = []
    for c in range(D_MODEL // LANES):
        acc = jnp.zeros((tm, LANES), F32)
        for k in range(TOP_K):
            acc = acc + gk[k] * b2[pl.ds(k * tm * SUBLANES + c, tm, stride=SUBLANES), :]
        cols.append(acc)
    f = jnp.concatenate(cols, axis=1)
    o_ref[...] = x1_ref[...] + g2_ref[0] * (_rms(f) * gpost_ref[...])


def _combine(yb, dest, gates, x1, mod3, g_post, seq, tm):
    t, d = x1.shape
    tps = seq // tm
    return pl.pallas_call(
        _combine_kernel,
        grid=(t // tm,),
        in_specs=[pl.BlockSpec((tm * TOP_K,), lambda i: (i,), memory_space=pltpu.SMEM),
                  pl.BlockSpec(memory_space=pl.ANY),
                  pl.BlockSpec((tm, LANES), lambda i: (i, 0)),
                  pl.BlockSpec((tm, d), lambda i: (i, 0)),
                  _mod_spec(5, tps),
                  pl.BlockSpec(g_post.shape, lambda i: (0, 0))],
        out_specs=pl.BlockSpec((tm, d), lambda i: (i, 0)),
        out_shape=jax.ShapeDtypeStruct((t, d), F32),
        scratch_shapes=[pltpu.VMEM((TOP_K * tm * SUBLANES, LANES), F32), pltpu.SemaphoreType.DMA],
        compiler_params=_cparams("arbitrary"),
        name="moe_combine",
    )(dest, yb, gates, x1, mod3, g_post)


def _moe(h2t, topi, gates, rank, cnt, x1, mod3, g_post, w_gu, b_gu, w_down, b_down, layer, seq):
    t = x1.shape[0]
    n_blocks = t * TOP_K // MOE_BLOCK + N_EXPERTS
    dest_l, blk, meta = _route(topi, rank, cnt, n_blocks, min(1024, t))
    dest = dest_l[:, :TOP_K].reshape(t * TOP_K)
    xb = _dispatch(h2t, dest, meta, n_blocks * MOE_BLOCK, MOE_BLOCK)
    yb = _experts(xb, blk[0, :n_blocks], meta[2, :1], w_gu, b_gu, w_down, b_down, layer)
    return _combine(yb, dest, gates, x1, mod3, g_post, seq, MOE_BLOCK)


TOKEN_TILE = 512
FOX_Q_TILE = 256
FOX_K_TILE = 256


def kernel(x, c, ada_w, ada_b, norm_pre_mix, norm_post_mix, norm_pre_ffn, norm_post_ffn, ev_w_in, fox_b_f, s5_lam_re, s5_lam_im, s5_log_dt, s5_b_re, s5_b_im, s5_c_re, s5_c_im, s5_d, s5_glu_w, s5_glu_b, ev_w_out, od_w_in, gla_w_up, gla_b_gate, od_w_out, router_w, router_b, exp_w_gu, exp_b_gu, exp_w_down, exp_b_down):
    bsz, seq, d = x.shape
    t = bsz * seq
    tm = min(TOKEN_TILE, seq)
    x2 = x.reshape(t, d)
    mod = _modulation(c, ada_w, ada_b)
    for l in range(DEPTH):
        i = l // 2
        mod3 = mod[l].reshape(bsz, 1, 6 * d)
        row = lambda a: a[l].reshape(1, -1)
        tail = (x2, mod3, row(norm_post_mix), row(norm_pre_ffn), router_w[l], router_b[l])
        if l % 2 == 0:
            u, q, k, v = _in_even(x2, mod3, row(norm_pre_mix), ev_w_in[i], fox_b_f[i], seq, tm)
            tables = _s5_tables(s5_lam_re[i], s5_lam_im[i], s5_log_dt[i], s5_b_re[i], s5_b_im[i],
                                s5_c_re[i], s5_c_im[i])
            ys = _s5_scan(u, bsz, seq, tables)
            yb = _fox(q, k, v, bsz, seq, min(FOX_Q_TILE, seq), min(FOX_K_TILE, seq))
            outs = _out_even(ys, u, yb, s5_d[i], s5_glu_w[i], s5_glu_b[i], ev_w_out[i], *tail, seq=seq, tm=tm)
        else:
            rq, rk, rv, sg, gq, gk, gv, sr, la = _in_odd(x2, mod3, row(norm_pre_mix), od_w_in[i],
                                                         gla_w_up[i], gla_b_gate[i], seq, tm)
            yc = _retention(rq, rk, rv, sg, bsz, seq)
            yd = _gla(gq, gk, la, gv, sr, bsz, seq)
            outs = _out_odd(yc, yd, od_w_out[i], *tail, seq=seq, tm=tm)
        x1, h2t, topi, gates, rank, cnt = outs
        x2 = _moe(h2t, topi, gates, rank, cnt, x1, mod3, row(norm_post_ffn),
                  exp_w_gu, exp_b_gu, exp_w_down, exp_b_down, l, seq)
    return x2.reshape(bsz, seq, d)
```

```python
import functools
import math

import jax
import jax.numpy as jnp
from jax import lax
from jax.experimental import pallas as pl
from jax.experimental.pallas import tpu as pltpu

F32 = jnp.float32
BF16 = jnp.bfloat16
HIGHEST = lax.Precision.HIGHEST

D_MODEL = 1024
DEPTH = 2
EPS = 1e-6
S5_CH = 512
S5_GROUP = 16
S5_GROUPS = S5_CH // S5_GROUP
S5_STATE = 64
S5_CHUNK = 16
FOX_HEADS = 8
FOX_DH = 64
FOX_W = FOX_HEADS * FOX_DH
FOX_HPS = 8
RET_HEADS = 4
RET_DK = 128
RET_DV = 128
ROPE_BASE = 10000.0
GLA_HEADS = 4
GLA_DK = 64
GLA_DV = 128
GLA_RANK = 16
GLA_TAU = 16.0
GLA_CHUNK = 64
GLA_SUB = 16
N_EXPERTS = 32
TOP_K = 4
D_EXPERT = 1024
SWIGLU_LIMIT = 7.0
SWIGLU_ALPHA = 1.702
MOE_BLOCK = 256

LANES = 128
SUBLANES = 8
VMEM_LIMIT = 56 * 1024 * 1024


def _cparams(*sem):
    return pltpu.CompilerParams(dimension_semantics=sem, vmem_limit_bytes=VMEM_LIMIT)


def _bdot(a, b):
    return jnp.dot(a, b, preferred_element_type=F32)


def _dot_nt(a, b):
    return lax.dot_general(a, b, (((1,), (1,)), ((), ())), preferred_element_type=F32)


def _dot_tn(a, b):
    return lax.dot_general(a, b, (((0,), (0,)), ((), ())), preferred_element_type=F32)


def _split3(x):
    hi = x.astype(BF16)
    r = x - hi.astype(F32)
    mid = r.astype(BF16)
    lo = (r - mid.astype(F32)).astype(BF16)
    return hi, mid, lo


def _dot01(m01, x):
    hi, mid, lo = _split3(x)
    return _bdot(m01, hi) + _bdot(m01, mid) + _bdot(m01, lo)


def _lower_tri(n, strict=False):
    r = lax.broadcasted_iota(jnp.int32, (n, n), 0)
    c = lax.broadcasted_iota(jnp.int32, (n, n), 1)
    return ((r > c) if strict else (r >= c)).astype(BF16)


def _log_sigmoid(x):
    return jnp.minimum(x, 0.0) - jnp.log1p(jnp.exp(-jnp.abs(x)))


def _sigmoid(x):
    return 1.0 / (1.0 + jnp.exp(-x))


def _silu(x):
    return x * _sigmoid(x)


def _rms(x):
    return x * lax.rsqrt(jnp.mean(x * x, axis=-1, keepdims=True) + EPS)


def _mod_kernel(c_ref, w_ref, b_ref, o_ref):
    c = c_ref[...]
    o_ref[0] = jnp.dot(_silu(c), w_ref[0], preferred_element_type=F32, precision=HIGHEST) + b_ref[0]


def _modulation(c, ada_w, ada_b):
    depth, d, n = ada_w.shape
    bsz = c.shape[0]
    tn = D_MODEL
    return pl.pallas_call(
        _mod_kernel,
        grid=(depth, n // tn),
        in_specs=[pl.BlockSpec((bsz, d), lambda l, j: (0, 0)),
                  pl.BlockSpec((1, d, tn), lambda l, j: (l, 0, j)),
                  pl.BlockSpec((1, 1, tn), lambda l, j: (l, 0, j))],
        out_specs=pl.BlockSpec((1, bsz, tn), lambda l, j: (l, 0, j)),
        out_shape=jax.ShapeDtypeStruct((depth, bsz, n), F32),
        compiler_params=_cparams("parallel", "parallel"),
        name="adaln_mod",
    )(c, ada_w, ada_b.reshape(depth, 1, n))


def _mod_spec(chunk, tiles_per_seq):
    return pl.BlockSpec((1, 1, D_MODEL), lambda i: (i // tiles_per_seq, 0, chunk))


def _prenorm(x, g_ref, sc_ref, sh_ref):
    return _rms(x) * (g_ref[...] * (1.0 + sc_ref[0])) + sh_ref[0]


def _in_even_kernel(x_ref, sh_ref, sc_ref, g_ref, w_ref, bf_ref,
                    u_ref, q_ref, k_ref, v_ref, carry_ref, *, tiles_per_seq):
    i = pl.program_id(0)
    tm = x_ref.shape[0]
    h = _prenorm(x_ref[...], g_ref, sc_ref, sh_ref)
    z = _bdot(h.astype(BF16), w_ref[...])
    u_ref[...] = z[:, 0:S5_CH]
    v_ref[...] = z[:, S5_CH + 2 * FOX_W:S5_CH + 3 * FOX_W].astype(BF16)
    ls = _log_sigmoid(z[:, S5_CH + 3 * FOX_W:] + bf_ref[...])

    @pl.when(i % tiles_per_seq == 0)
    def _():
        carry_ref[...] = jnp.zeros_like(carry_ref)

    cum = _dot01(_lower_tri(tm), ls) + carry_ref[...]
    carry_ref[...] = cum[tm - 1:tm, :]

    lane = lax.broadcasted_iota(jnp.int32, (1, LANES), 1)
    feat = lane < FOX_DH
    ones = jnp.where(lane < FOX_DH + 3, 1.0, 0.0)
    for hd in range(FOX_HEADS):
        blk = (hd * FOX_DH) // LANES * LANES
        qs = z[:, S5_CH + blk:S5_CH + blk + LANES] * (FOX_DH ** -0.5)
        ks = z[:, S5_CH + FOX_W + blk:S5_CH + FOX_W + blk + LANES]
        if (hd * FOX_DH) % LANES:
            qs = pltpu.roll(qs, LANES - FOX_DH, 1)
            ks = pltpu.roll(ks, LANES - FOX_DH, 1)
        nf = jnp.broadcast_to(-cum[:, hd:hd + 1], (tm, LANES))
        hi = nf.astype(BF16).astype(F32)
        mid = (nf - hi).astype(BF16).astype(F32)
        lo = nf - hi - mid
        bias = jnp.where(lane == FOX_DH, hi, jnp.where(lane == FOX_DH + 1, mid,
                                                       jnp.where(lane == FOX_DH + 2, lo, 0.0)))
        q_ref[:, hd * LANES:(hd + 1) * LANES] = jnp.where(feat, qs, ones).astype(BF16)
        k_ref[:, hd * LANES:(hd + 1) * LANES] = jnp.where(feat, ks, bias).astype(BF16)


def _in_even(x2, mod3, gain, w_in, b_f, seq, tm):
    t, d = x2.shape
    tiles_per_seq = seq // tm
    nw = S5_CH + 3 * FOX_W
    w = jnp.concatenate([w_in[:, :nw], jnp.pad(w_in[:, nw:], ((0, 0), (0, LANES - FOX_HEADS)))],
                        axis=1).astype(BF16)
    bf = jnp.pad(b_f, (0, LANES - FOX_HEADS)).reshape(1, LANES)
    tok = lambda n: pl.BlockSpec((tm, n), lambda i: (i, 0))
    full = lambda a: pl.BlockSpec(a.shape, lambda i: (0,) * a.ndim)
    return pl.pallas_call(
        functools.partial(_in_even_kernel, tiles_per_seq=tiles_per_seq),
        grid=(t // tm,),
        in_specs=[tok(d), _mod_spec(0, tiles_per_seq), _mod_spec(1, tiles_per_seq),
                  full(gain), full(w), full(bf)],
        out_specs=[tok(S5_CH), tok(FOX_HEADS * LANES), tok(FOX_HEADS * LANES), tok(FOX_W)],
        out_shape=[jax.ShapeDtypeStruct((t, S5_CH), F32),
                   jax.ShapeDtypeStruct((t, FOX_HEADS * LANES), BF16),
                   jax.ShapeDtypeStruct((t, FOX_HEADS * LANES), BF16),
                   jax.ShapeDtypeStruct((t, FOX_W), BF16)],
        scratch_shapes=[pltpu.VMEM((1, LANES), F32)],
        compiler_params=_cparams("arbitrary"),
        name="in_proj_even",
    )(x2, mod3, mod3, gain, w, bf)


def _s5_kernel(u_ref, toep_ref, wst_ref, cst_ref, a_ref, y_ref, e_scr, hp_scr, *, n_chunks, nb):
    u = u_ref[0]
    e_scr[...] = _bdot(u, wst_ref[0])
    two_p = 2 * S5_STATE
    a1 = jnp.broadcast_to(a_ref[0, 0:1, :], (nb, two_p))
    a2 = jnp.broadcast_to(a_ref[0, 1:2, :], (nb, two_p))
    a2s = jnp.broadcast_to(a_ref[0, 2:3, :], (nb, two_p))

    def body(c, carry):
        h, hs = carry
        r = pl.multiple_of(c * nb, nb)
        hp_scr[pl.ds(r, nb), :] = h
        e = e_scr[pl.ds(r, nb), :]
        return a1 * h + a2 * hs + e[:, :two_p], a1 * hs + a2s * h + e[:, two_p:]

    zero = jnp.zeros((nb, two_p), F32)
    lax.fori_loop(0, n_chunks, body, (zero, zero))
    y_ref[0] = _bdot(u, toep_ref[0]) + _bdot(hp_scr[...].astype(BF16), cst_ref[0])


def _s5_tables(lam_re, lam_im, log_dt, b_re, b_im, c_re, c_im):
    ell, p, g = S5_CHUNK, S5_STATE, S5_GROUPS
    lr, li = lam_re.astype(F32), lam_im.astype(F32)
    dt = jnp.exp(log_dt.astype(F32))[:, None]
    mag = jnp.exp(lr * dt)
    a_re, a_im = mag * jnp.cos(li * dt), mag * jnp.sin(li * dt)
    den = lr * lr + li * li
    n_re, n_im = a_re - 1.0, a_im
    z_re = (n_re * lr + n_im * li) / den
    z_im = (n_im * lr - n_re * li) / den
    br, bi = b_re.astype(F32), b_im.astype(F32)
    bb_re = z_re[..., None] * br - z_im[..., None] * bi
    bb_im = z_re[..., None] * bi + z_im[..., None] * br
    j = jnp.arange(ell + 1, dtype=F32)[:, None, None]
    pmag = jnp.exp(j * (lr * dt)[None])
    pr, pi = pmag * jnp.cos(j * (li * dt)[None]), pmag * jnp.sin(j * (li * dt)[None])
    w_re = pr[..., None] * bb_re[None] - pi[..., None] * bb_im[None]
    w_im = pr[..., None] * bb_im[None] + pi[..., None] * bb_re[None]
    cr, ci = c_re.astype(F32), c_im.astype(F32)
    kern = (jnp.einsum('gcp,jgpd->jgcd', cr, w_re[:ell], precision=HIGHEST)
            - jnp.einsum('gcp,jgpd->jgcd', ci, w_im[:ell], precision=HIGHEST))
    s_idx = jnp.arange(ell)[:, None]
    t_idx = jnp.arange(ell)[None, :]
    lag = jnp.clip(t_idx - s_idx, 0, ell - 1)
    toep = kern[lag]
    toep = jnp.where((t_idx >= s_idx)[:, :, None, None, None], toep, 0.0)
    toep = toep.transpose(2, 0, 4, 1, 3).reshape(g, ell * S5_GROUP, ell * S5_GROUP)
    rev = jnp.arange(ell - 1, -1, -1)
    st_re = w_re[rev].transpose(1, 0, 3, 2).reshape(g, ell * S5_GROUP, p)
    st_im = w_im[rev].transpose(1, 0, 3, 2).reshape(g, ell * S5_GROUP, p)
    wst = jnp.concatenate([st_re, st_im, st_im, st_re], axis=-1)
    p1r, p1i = pr[1:], pi[1:]
    c_hr = cr[None] * p1r[:, :, None, :] - ci[None] * p1i[:, :, None, :]
    c_hi = -cr[None] * p1i[:, :, None, :] - ci[None] * p1r[:, :, None, :]
    cst = jnp.concatenate([c_hr, c_hi], axis=-1).transpose(1, 3, 0, 2).reshape(g, 2 * p, ell * S5_GROUP)
    al_r, al_i = pr[ell], pi[ell]
    a_rows = jnp.stack([jnp.concatenate([al_r, al_r], -1), jnp.concatenate([-al_i, al_i], -1),
                        jnp.concatenate([al_i, -al_i], -1)], axis=1)
    a_rows = jnp.pad(a_rows, ((0, 0), (0, SUBLANES - 3), (0, 0)))
    return toep.astype(BF16), wst.astype(BF16), cst.astype(BF16), a_rows


def _s5_scan(u, bsz, seq, tables):
    toep, wst, cst, a_rows = tables
    ell, g = S5_CHUNK, S5_GROUPS
    n_chunks = seq // ell
    rows = n_chunks * bsz
    w = ell * S5_GROUP
    ug = (u.reshape(bsz, n_chunks, ell, g, S5_GROUP).transpose(3, 1, 0, 2, 4)
          .reshape(g, rows, w).astype(BF16))
    grp = lambda a: pl.BlockSpec((1,) + a.shape[1:], lambda i: (i, 0, 0))
    yg = pl.pallas_call(
        functools.partial(_s5_kernel, n_chunks=n_chunks, nb=bsz),
        grid=(g,),
        in_specs=[grp(ug), grp(toep), grp(wst), grp(cst), grp(a_rows)],
        out_specs=pl.BlockSpec((1, rows, w), lambda i: (i, 0, 0)),
        out_shape=jax.ShapeDtypeStruct((g, rows, w), F32),
        scratch_shapes=[pltpu.VMEM((rows, 4 * S5_STATE), F32), pltpu.VMEM((rows, 2 * S5_STATE), F32)],
        compiler_params=_cparams("parallel"),
        name="s5_scan",
    )(ug, toep, wst, cst, a_rows)
    return (yg.reshape(g, n_chunks, bsz, ell, S5_GROUP).transpose(2, 1, 3, 0, 4)
            .reshape(bsz * seq, S5_CH))


def _fox_kernel(q_ref, k_ref, vt_ref, o_ref, *, tq, tk):
    i = pl.program_id(2)
    nh = FOX_HPS
    q = [q_ref[:, h * LANES:(h + 1) * LANES] for h in range(nh)]
    key = lax.broadcasted_iota(jnp.int32, (tk, tq), 0)
    qry = lax.broadcasted_iota(jnp.int32, (tk, tq), 1)
    per_q = tq // tk

    def block(j0, carry, mask):
        kj = k_ref[pl.ds(j0, tk), :]
        vtj = vt_ref[:, pl.ds(j0, tk)]
        ss = [_dot_nt(kj[:, h * LANES:(h + 1) * LANES], q[h]) for h in range(nh)]
        stats = []
        for h in range(nh):
            m, l, acc = carry[h]
            s = ss[h] if mask is None else jnp.where(mask, ss[h], -jnp.inf)
            m_new = jnp.maximum(m, jnp.max(s, axis=0, keepdims=True))
            p = jnp.exp(s - m_new)
            alpha = jnp.exp(m - m_new)
            stats.append((m_new, alpha * l + jnp.sum(p, axis=0, keepdims=True), alpha, p.astype(BF16)))
        out = []
        for h in range(nh):
            m_new, l, alpha, p = stats[h]
            acc = alpha * carry[h][2] + _bdot(vtj[h * FOX_DH:(h + 1) * FOX_DH, :], p)
            out.append((m_new, l, acc))
        return tuple(out)

    init = tuple((jnp.full((1, tq), -jnp.inf, F32), jnp.zeros((1, tq), F32), jnp.zeros((FOX_DH, tq), F32))
                 for _ in range(nh))
    carry = lax.fori_loop(0, i * per_q, lambda j, c: block(pl.multiple_of(j * tk, tk), c, None), init)
    for d in range(per_q):
        carry = block(pl.multiple_of(i * tq + d * tk, tk), carry, key + d * tk <= qry)
    for g in range(nh // 2):
        o_t = jnp.concatenate([carry[h][2] / carry[h][1] for h in (2 * g, 2 * g + 1)], axis=0)
        o_ref[:, g * LANES:(g + 1) * LANES] = o_t.T.astype(o_ref.dtype)


def _fox(q_aug, k_aug, v, bsz, seq, tq, tk):
    t = v.shape[0]
    nh = FOX_HPS
    groups = FOX_HEADS // nh
    nq = seq // tq
    v_t = v.reshape(bsz, seq, FOX_W).transpose(0, 2, 1).reshape(bsz * FOX_W, seq)
    return pl.pallas_call(
        functools.partial(_fox_kernel, tq=tq, tk=tk),
        grid=(bsz, groups, nq),
        in_specs=[pl.BlockSpec((tq, nh * LANES), lambda b, p, i: (b * nq + i, p)),
                  pl.BlockSpec((seq, nh * LANES), lambda b, p, i: (b, p)),
                  pl.BlockSpec((nh * FOX_DH, seq), lambda b, p, i: (b * groups + p, 0))],
        out_specs=pl.BlockSpec((tq, nh * FOX_DH), lambda b, p, i: (b * nq + i, p)),
        out_shape=jax.ShapeDtypeStruct((t, FOX_W), BF16),
        compiler_params=_cparams("parallel", "parallel", "arbitrary"),
        name="fox_attention",
    )(q_aug, k_aug, v_t)


_ODD_COLS = (("rq", 512), ("rk", 512), ("rv", 512), ("rg", 512), ("gq", 256), ("gk", 256),
             ("gv", 512), ("gr", 512), ("glr", LANES))


def _odd_offsets():
    off, out = 0, {}
    for name, w in _ODD_COLS:
        out[name] = (off, off + w)
        off += w
    return out, off


def _in_odd_kernel(x_ref, sh_ref, sc_ref, g_ref, w_ref, cos_ref, sin_ref, wup_ref, bg_ref,
                   rq_ref, rk_ref, rv_ref, sg_ref, gq_ref, gk_ref, gv_ref, sr_ref, la_ref):
    h = _prenorm(x_ref[...], g_ref, sc_ref, sh_ref)
    z = _bdot(h.astype(BF16), w_ref[...])
    off, _ = _odd_offsets()
    col = lambda n: z[:, off[n][0]:off[n][1]]
    cos, sin = cos_ref[...], sin_ref[...]

    def rope(t, scale):
        heads = []
        for hd in range(RET_HEADS):
            th = t[:, hd * RET_DK:(hd + 1) * RET_DK]
            heads.append((th * cos + pltpu.roll(th, RET_DK // 2, 1) * sin) * scale)
        return jnp.concatenate(heads, axis=1).astype(BF16)

    rq_ref[...] = rope(col("rq"), 1.0)
    rk_ref[...] = rope(col("rk"), RET_DK ** -0.5)
    rv_ref[...] = col("rv").astype(BF16)
    sg_ref[...] = _silu(col("rg"))
    gq_ref[...] = col("gq") * (GLA_DK ** -0.5)
    gk_ref[...] = col("gk")
    gv_ref[...] = col("gv").astype(BF16)
    sr_ref[...] = _silu(col("gr"))
    gate = jnp.dot(col("glr"), wup_ref[...], preferred_element_type=F32, precision=HIGHEST) + bg_ref[...]
    la_ref[...] = _log_sigmoid(gate) * (1.0 / GLA_TAU)


def _in_odd(x2, mod3, gain, w_in, w_up, b_gate, seq, tm):
    t, d = x2.shape
    tps = seq // tm
    ref_w = (512, 512, 512, 512, 256, 256, 512, GLA_RANK, 512)
    starts = [0]
    for wd in ref_w:
        starts.append(starts[-1] + wd)
    seg = lambda j: w_in[:, starts[j]:starts[j + 1]]
    w = jnp.concatenate([seg(0), seg(1), seg(2), seg(3), seg(4), seg(5), seg(6), seg(8),
                         jnp.pad(seg(7), ((0, 0), (0, LANES - GLA_RANK)))], axis=1).astype(BF16)
    wup = jnp.pad(w_up.astype(F32), ((0, LANES - GLA_RANK), (0, 0)))
    bg = b_gate.reshape(1, -1).astype(F32)
    half = RET_DK // 2
    inv = ROPE_BASE ** (-jnp.arange(half, dtype=F32) / half)
    ang = jnp.arange(seq, dtype=F32)[:, None] * inv[None, :]
    cos = jnp.concatenate([jnp.cos(ang), jnp.cos(ang)], axis=1)
    sin = jnp.concatenate([-jnp.sin(ang), jnp.sin(ang)], axis=1)
    tok = lambda n: pl.BlockSpec((tm, n), lambda i: (i, 0))
    full = lambda a: pl.BlockSpec(a.shape, lambda i: (0,) * a.ndim)
    pos = pl.BlockSpec((tm, RET_DK), lambda i: (i % tps, 0))
    widths = (512, 512, 512, 512, 256, 256, 512, 512, 256)
    dtypes = (BF16, BF16, BF16, F32, F32, F32, BF16, F32, F32)
    return pl.pallas_call(
        _in_odd_kernel,
        grid=(t // tm,),
        in_specs=[tok(d), _mod_spec(0, tps), _mod_spec(1, tps), full(gain), full(w), pos, pos,
                  full(wup), full(bg)],
        out_specs=[tok(n) for n in widths],
        out_shape=[jax.ShapeDtypeStruct((t, n), dt) for n, dt in zip(widths, dtypes)],
        compiler_params=_cparams("parallel"),
        name="in_proj_odd",
    )(x2, mod3, mod3, gain, w, cos, sin, wup, bg)


RET_CHUNK = 256


def _ret_kernel(q_ref, k_ref, v_ref, sg_ref, dm_ref, xi_ref, zeta_ref, gl_ref, y_ref, st_ref):
    @pl.when(pl.program_id(1) == 0)
    def _():
        st_ref[...] = jnp.zeros_like(st_ref)

    heads = range(RET_HEADS)
    col = lambda h: slice(h * RET_DK, (h + 1) * RET_DK)
    q = [q_ref[:, col(h)] for h in heads]
    k = [k_ref[:, col(h)] for h in heads]
    v = [v_ref[:, col(h)] for h in heads]
    st = [st_ref[h] for h in heads]
    s = [_dot_nt(q[h], k[h]) for h in heads]
    inter = [_bdot((q[h].astype(F32) * xi_ref[h]).astype(BF16), st[h].astype(BF16)) for h in heads]
    upd = [_dot_tn((k[h].astype(F32) * zeta_ref[h]).astype(BF16), v[h]) for h in heads]
    for h in heads:
        o = _bdot((s[h] * dm_ref[h]).astype(BF16), v[h]) + inter[h]
        st_ref[h] = gl_ref[h, 0:1, :] * st[h] + upd[h]
        y_ref[:, col(h)] = (sg_ref[:, col(h)] * _rms(o)).astype(y_ref.dtype)


def _retention(rq, rk, rv, sg, bsz, seq):
    t = rq.shape[0]
    ell = min(RET_CHUNK, seq)
    nc = seq // ell
    log_g = jnp.log(1.0 - jnp.exp2(-5.0 - jnp.arange(RET_HEADS, dtype=F32)))
    idx = jnp.arange(ell, dtype=F32)
    rel = idx[:, None] - idx[None, :]
    dmat = jnp.where(rel >= 0, jnp.exp(log_g[:, None, None] * jnp.maximum(rel, 0.0)), 0.0)
    lanes = lambda a: jnp.broadcast_to(a[..., None], a.shape + (RET_DK,))
    xi = lanes(jnp.exp(log_g[:, None] * (idx + 1.0)))
    zeta = lanes(jnp.exp(log_g[:, None] * (ell - 1.0 - idx)))
    gl = jnp.broadcast_to(jnp.exp(log_g * ell)[:, None, None], (RET_HEADS, SUBLANES, RET_DV))
    blk = pl.BlockSpec((ell, RET_HEADS * RET_DK), lambda b, c: (b * nc + c, 0))
    full = lambda a: pl.BlockSpec(a.shape, lambda b, c: (0, 0, 0))
    return pl.pallas_call(
        _ret_kernel,
        grid=(bsz, nc),
        in_specs=[blk, blk, blk, blk, full(dmat), full(xi), full(zeta), full(gl)],
        out_specs=blk,
        out_shape=jax.ShapeDtypeStruct((t, RET_HEADS * RET_DV), BF16),
        scratch_shapes=[pltpu.VMEM((RET_HEADS, RET_DK, RET_DV), F32)],
        compiler_params=_cparams("parallel", "arbitrary"),
        name="retention",
    )(rq, rk, rv, sg, dmat, xi, zeta, gl)


def _gla_kernel(q_ref, k_ref, la_ref, v_ref, sg_ref, y_ref, st_ref, b_scr, v_scr, p_scr, r_scr):
    @pl.when(pl.program_id(1) == 0)
    def _():
        st_ref[...] = jnp.zeros_like(st_ref)

    ell, sub = GLA_CHUNK, GLA_SUB
    n_sub = ell // sub
    nb = q_ref.shape[0]
    pairs = GLA_HEADS // 2
    streams = [(bl, p) for bl in range(nb) for p in range(pairs)]
    lane = lax.broadcasted_iota(jnp.int32, (1, LANES), 1)
    first = lane < GLA_DK
    head = (first, jnp.logical_not(first))
    pick = lambda h, a: jnp.where(head[h], a, 0.0).astype(BF16)
    tri = _lower_tri(ell)
    tau = lax.broadcasted_iota(jnp.int32, (sub, LANES), 0)
    rsub = lax.broadcasted_iota(jnp.int32, (LANES, 2 * LANES), 0)
    csub = lax.broadcasted_iota(jnp.int32, (LANES, 2 * LANES), 1)
    ind = ((rsub < GLA_DK) == (csub < LANES)).astype(BF16)

    val = {}
    for sid, (bl, p) in enumerate(streams):
        qk = slice(p * LANES, (p + 1) * LANES)
        q, k = q_ref[bl, :, qk], k_ref[bl, :, qk]
        b = _dot01(tri, la_ref[bl, :, qk])
        b_scr[sid] = b
        v_bf = v_ref[bl, :, p * 2 * GLA_DV:(p + 1) * 2 * GLA_DV]
        v_scr[sid] = v_bf.astype(F32)
        st = st_ref[sid]
        val[sid] = dict(q=q, k=k, b=b, st=st, vh=[v_bf[:, h * GLA_DV:(h + 1) * GLA_DV] for h in range(2)])

    for sid in val:
        d = val[sid]
        qe = d["q"] * jnp.exp(d["b"])
        st_bf = d["st"].astype(BF16)
        d["o"] = [_dot_nt(pick(h, qe), st_bf) for h in range(2)]

    row = lax.broadcasted_iota(jnp.int32, (ell, LANES), 0)
    for sid in val:
        d = val[sid]
        q, k, b = d["q"], d["k"], d["b"]
        qa, ka = [], []
        for i in range(1, n_sub):
            lo = i * sub
            ref_row = b[lo - 1:lo, :]
            in_i = (row >= lo) & (row < lo + sub)
            qa.append(jnp.where(in_i, q * jnp.exp(jnp.minimum(b - ref_row, 0.0)), 0.0))
            ka.append(jnp.where(row < lo, k * jnp.exp(jnp.minimum(ref_row - b, 0.0)), 0.0))
        k_cat = jnp.concatenate(ka, axis=1).astype(BF16)
        d["a_off"] = [_dot_nt(jnp.concatenate([pick(h, x) for x in qa], axis=1), k_cat) for h in range(2)]
    for sid in val:
        d = val[sid]
        d["off"] = [_bdot(d["a_off"][h].astype(BF16), d["vh"][h]) for h in range(2)]

    for sid, (bl, p) in enumerate(streams):
        d = val[sid]
        q, b = d["q"], d["b"]
        for i in range(n_sub):
            lo = i * sub
            qi, bi = q[lo:lo + sub], b[lo:lo + sub]
            for s in range(sub):
                k_row = k_ref[bl, pl.ds(lo + s, 1), p * LANES:(p + 1) * LANES]
                b_row = b_scr[sid, pl.ds(lo + s, 1), :]
                w = jnp.exp(jnp.minimum(bi - b_row, 0.0))
                p_scr[sid, pl.ds((lo + s) * sub, sub), :] = jnp.where(tau >= s, qi * k_row * w, 0.0).astype(BF16)
    for sid in val:
        r_scr[sid] = _bdot(p_scr[sid], ind)
    for sid in val:
        diag = [[], []]
        for i in range(n_sub):
            lo = i * sub
            for h in range(2):
                acc = jnp.zeros((sub, GLA_DV), F32)
                for s in range(sub):
                    v_row = v_scr[sid, pl.ds(lo + s, 1), h * GLA_DV:(h + 1) * GLA_DV]
                    acc = acc + r_scr[sid, pl.ds((lo + s) * sub, sub), h * LANES:(h + 1) * LANES] * v_row
                diag[h].append(acc)
        val[sid]["diag"] = diag

    for sid, (bl, p) in enumerate(streams):
        d = val[sid]
        b_last = d["b"][ell - 1:ell, :]
        kh = (d["k"] * jnp.exp(b_last - d["b"])).astype(BF16)
        upd = [_dot_tn(d["vh"][h], kh) for h in range(2)]
        st_ref[sid] = d["st"] * jnp.exp(b_last) + jnp.where(first, upd[0], upd[1])
        for h in range(2):
            oh = d["o"][h] + d["off"][h] + jnp.concatenate(d["diag"][h], axis=0)
            cols = slice((2 * p + h) * GLA_DV, (2 * p + h + 1) * GLA_DV)
            y_ref[bl, :, cols] = (sg_ref[bl, :, cols] * _rms(oh)).astype(y_ref.dtype)


GLA_BATCHES = 2


def _gla(gq, gk, la, gv, sr, bsz, seq):
    t = gq.shape[0]
    ell = GLA_CHUNK
    nc = seq // ell
    nb = min(GLA_BATCHES, bsz)
    ns = nb * (GLA_HEADS // 2)
    r3 = lambda a: a.reshape(bsz, seq, a.shape[1])
    spec = lambda w: pl.BlockSpec((nb, ell, w), lambda b, c: (b, c, 0))
    wq, wv = GLA_HEADS * GLA_DK, GLA_HEADS * GLA_DV
    y = pl.pallas_call(
        _gla_kernel,
        grid=(bsz // nb, nc),
        in_specs=[spec(wq), spec(wq), spec(wq), spec(wv), spec(wv)],
        out_specs=spec(wv),
        out_shape=jax.ShapeDtypeStruct((bsz, seq, wv), BF16),
        scratch_shapes=[pltpu.VMEM((ns, GLA_DV, LANES), F32),
                        pltpu.VMEM((ns, ell, LANES), F32),
                        pltpu.VMEM((ns, ell, 2 * GLA_DV), F32),
                        pltpu.VMEM((ns, ell * GLA_SUB, LANES), BF16),
                        pltpu.VMEM((ns, ell * GLA_SUB, 2 * LANES), F32)],
        compiler_params=_cparams("parallel", "arbitrary"),
        name="gla",
    )(r3(gq), r3(gk), r3(la), r3(gv), r3(sr))
    return y.reshape(t, wv)


def _post_tail(m, x_ref, g1_ref, gpost_ref, sh2_ref, sc2_ref, gpre_ref, rw_ref, rb_ref,
               x1_ref, h2_ref, topi_ref, gate_ref, rank_ref, cnt_ref, carry_ref):
    i = pl.program_id(0)
    tm = m.shape[0]
    x1 = x_ref[...] + g1_ref[0] * (_rms(m) * gpost_ref[...])
    x1_ref[...] = x1
    h2 = _rms(x1) * (gpre_ref[...] * (1.0 + sc2_ref[0])) + sh2_ref[0]
    for c in range(D_MODEL // LANES):
        h2_ref[pl.ds(c, tm, stride=SUBLANES), :] = h2[:, c * LANES:(c + 1) * LANES]
    h_hi = h2.astype(BF16)
    h_lo = (h2 - h_hi.astype(F32)).astype(BF16)
    w_hi, w_lo = rw_ref[0], rw_ref[1]
    logits = _bdot(h_hi, w_hi) + _bdot(h_lo, w_hi) + _bdot(h_hi, w_lo) + rb_ref[...]

    lane = lax.broadcasted_iota(jnp.int32, (tm, LANES), 1)
    lanef = lane.astype(F32)
    work = logits
    topv = jnp.full((tm, LANES), -jnp.inf, F32)
    topi = jnp.zeros((tm, LANES), F32)
    onehot = jnp.zeros((tm, LANES), F32)
    hits = []
    for k in range(TOP_K):
        mx = jnp.max(work, axis=-1, keepdims=True)
        idx = jnp.min(jnp.where(work == mx, lanef, float(LANES)), axis=-1, keepdims=True)
        hit = lanef == idx
        hits.append(hit)
        topv = jnp.where(lane == k, mx, topv)
        topi = jnp.where(lane == k, idx, topi)
        onehot = onehot + hit.astype(F32)
        work = jnp.where(hit, -jnp.inf, work)
    e = jnp.exp(topv - jnp.max(topv, axis=-1, keepdims=True))
    gate_ref[...] = e / jnp.sum(e, axis=-1, keepdims=True)
    topi_ref[...] = topi.astype(jnp.int32)

    @pl.when(i == 0)
    def _():
        carry_ref[...] = jnp.zeros_like(carry_ref)

    before = _bdot(_lower_tri(tm, strict=True), onehot.astype(BF16)) + carry_ref[...]
    rank = jnp.zeros((tm, LANES), F32)
    for k in range(TOP_K):
        rk = jnp.sum(jnp.where(hits[k], before, 0.0), axis=-1, keepdims=True)
        rank = jnp.where(lane == k, rk, rank)
    rank_ref[...] = rank.astype(jnp.int32)
    total = before[tm - 1:tm, :] + onehot[tm - 1:tm, :]
    carry_ref[...] = total
    cnt_ref[...] = jnp.broadcast_to(total, cnt_ref.shape)


def _out_even_kernel(ys_ref, u_ref, yb_ref, d_ref, gw_ref, gb_ref, wa_ref, wb_ref, *rest):
    y = ys_ref[...] + d_ref[...] * u_ref[...]
    g = jax.nn.gelu(y)
    ya = g * _sigmoid(_bdot(g.astype(BF16), gw_ref[...]) + gb_ref[...])
    m = _bdot(ya.astype(BF16), wa_ref[...]) + _bdot(yb_ref[...], wb_ref[...])
    _post_tail(m, *rest)


def _out_odd_kernel(yc_ref, yd_ref, wa_ref, wb_ref, *rest):
    m = _bdot(yc_ref[...], wa_ref[...]) + _bdot(yd_ref[...], wb_ref[...])
    _post_tail(m, *rest)


def _mixer_out(body, mix_args, mix_specs, x2, mod3, g_post, g_pre, router_w, router_b, seq, tm):
    t, d = x2.shape
    tps = seq // tm
    rw32 = jnp.pad(router_w.astype(F32), ((0, 0), (0, LANES - N_EXPERTS)))
    rw_hi = rw32.astype(BF16)
    rw = jnp.stack([rw_hi, (rw32 - rw_hi.astype(F32)).astype(BF16)])
    rb = jnp.pad(router_b.astype(F32), (0, LANES - N_EXPERTS), constant_values=-1e30).reshape(1, LANES)
    tok = lambda n: pl.BlockSpec((tm, n), lambda i: (i, 0))
    full = lambda a: pl.BlockSpec(a.shape, lambda i: (0,) * a.ndim)
    tail_args = [x2, mod3, g_post, mod3, mod3, g_pre, rw, rb]
    tail_specs = [tok(d), _mod_spec(2, tps), full(g_post), _mod_spec(3, tps), _mod_spec(4, tps),
                  full(g_pre), full(rw), full(rb)]
    return pl.pallas_call(
        body,
        grid=(t // tm,),
        in_specs=mix_specs + tail_specs,
        out_specs=[tok(d), pl.BlockSpec((tm * SUBLANES, LANES), lambda i: (i, 0)),
                   tok(LANES), tok(LANES), tok(LANES), pl.BlockSpec((SUBLANES, LANES), lambda i: (0, 0))],
        out_shape=[jax.ShapeDtypeStruct((t, d), F32),
                   jax.ShapeDtypeStruct((t * SUBLANES, LANES), F32),
                   jax.ShapeDtypeStruct((t, LANES), jnp.int32),
                   jax.ShapeDtypeStruct((t, LANES), F32),
                   jax.ShapeDtypeStruct((t, LANES), jnp.int32),
                   jax.ShapeDtypeStruct((SUBLANES, LANES), F32)],
        scratch_shapes=[pltpu.VMEM((1, LANES), F32)],
        compiler_params=_cparams("arbitrary"),
        name="mixer_out_router",
    )(*mix_args, *tail_args)


def _out_even(ys, u, yb, d_skip, glu_w, glu_b, w_out, *tail, seq, tm):
    tok = lambda n: pl.BlockSpec((tm, n), lambda i: (i, 0))
    full = lambda a: pl.BlockSpec(a.shape, lambda i: (0,) * a.ndim)
    args = [ys, u, yb, d_skip.reshape(1, -1), glu_w.astype(BF16), glu_b.reshape(1, -1),
            w_out[:S5_CH].astype(BF16), w_out[S5_CH:].astype(BF16)]
    specs = [tok(S5_CH), tok(S5_CH), tok(FOX_W)] + [full(a) for a in args[3:]]
    return _mixer_out(_out_even_kernel, args, specs, *tail, seq, tm)


def _out_odd(yc, yd, w_out, *tail, seq, tm):
    tok = lambda n: pl.BlockSpec((tm, n), lambda i: (i, 0))
    full = lambda a: pl.BlockSpec(a.shape, lambda i: (0,) * a.ndim)
    nc = yc.shape[1]
    args = [yc, yd, w_out[:nc].astype(BF16), w_out[nc:].astype(BF16)]
    specs = [tok(nc), tok(yd.shape[1])] + [full(a) for a in args[2:]]
    return _mixer_out(_out_odd_kernel, args, specs, *tail, seq, tm)


def _route_kernel(topi_ref, rank_ref, cnt_ref, dest_ref, blk_ref, meta_ref):
    tm = topi_ref.shape[0]
    cnt = cnt_ref[...]
    padded = jnp.floor((cnt + (MOE_BLOCK - 1.0)) * (1.0 / MOE_BLOCK)) * MOE_BLOCK
    r = lax.broadcasted_iota(jnp.int32, (LANES, LANES), 0)
    c = lax.broadcasted_iota(jnp.int32, (LANES, LANES), 1)
    hi, mid, lo = _split3(padded)
    incl = (r <= c).astype(BF16)
    pad_end = _bdot(hi, incl) + _bdot(mid, incl) + _bdot(lo, incl)
    pad_start = pad_end - padded
    lane = lax.broadcasted_iota(jnp.int32, (tm, LANES), 1)
    lanef = lane.astype(F32)
    topi = topi_ref[...].astype(F32)
    start_row = pad_start[0:1, :]
    dest = jnp.zeros((tm, LANES), F32)
    for k in range(TOP_K):
        idx = jnp.sum(jnp.where(lane == k, topi, 0.0), axis=-1, keepdims=True)
        st = jnp.sum(jnp.where(lanef == idx, start_row, 0.0), axis=-1, keepdims=True)
        dest = jnp.where(lane == k, st, dest)
    dest_ref[...] = dest.astype(jnp.int32) + rank_ref[...]

    nb = blk_ref.shape[1]
    end_col = jnp.sum(jnp.where(r == c, jnp.broadcast_to(pad_end[0:1, :], (LANES, LANES)), 0.0),
                      axis=-1, keepdims=True)
    jpos = lax.broadcasted_iota(jnp.int32, (LANES, nb), 1).astype(F32) * MOE_BLOCK
    esub = lax.broadcasted_iota(jnp.int32, (LANES, nb), 0)
    done = jnp.where((end_col <= jpos) & (esub < N_EXPERTS), 1.0, 0.0)
    be = jnp.minimum(jnp.sum(done, axis=0, keepdims=True), N_EXPERTS - 1.0)
    blk_ref[...] = jnp.broadcast_to(be, blk_ref.shape).astype(jnp.int32)
    lane1 = lax.broadcasted_iota(jnp.int32, (SUBLANES, LANES), 1)
    n_valid = jnp.sum(jnp.where(lane1 == N_EXPERTS - 1, pad_end, 0.0), axis=-1, keepdims=True) * (1.0 / MOE_BLOCK)
    sub1 = lax.broadcasted_iota(jnp.int32, (SUBLANES, LANES), 0)
    meta = jnp.where(sub1 == 0, pad_start + cnt, jnp.where(sub1 == 1, pad_end, jnp.broadcast_to(n_valid, (SUBLANES, LANES))))
    meta_ref[...] = meta.astype(jnp.int32)


def _route(topi, rank, cnt, n_blocks, tm):
    t = topi.shape[0]
    nb_pad = -(-n_blocks // LANES) * LANES
    tok = pl.BlockSpec((tm, LANES), lambda i: (i, 0))
    fix = lambda n: pl.BlockSpec((SUBLANES, n), lambda i: (0, 0))
    return pl.pallas_call(
        _route_kernel,
        grid=(t // tm,),
        in_specs=[tok, tok, fix(LANES)],
        out_specs=[tok, fix(nb_pad), fix(LANES)],
        out_shape=[jax.ShapeDtypeStruct((t, LANES), jnp.int32),
                   jax.ShapeDtypeStruct((SUBLANES, nb_pad), jnp.int32),
                   jax.ShapeDtypeStruct((SUBLANES, LANES), jnp.int32)],
        compiler_params=_cparams("arbitrary"),
        name="route_plan",
    )(topi, rank, cnt)


def _dispatch_kernel(pad_ref, dest_ref, h_ref, xb_ref, zero_ref, sem_z, sem_s):
    i = pl.program_id(0)
    tm = h_ref.shape[0]

    @pl.when(i == 0)
    def _():
        zero_ref[...] = jnp.zeros_like(zero_ref)
        sizes = [1 << b for b in range(int(math.log2(MOE_BLOCK)) - 1, -1, -1)]

        def fill(e, carry, do_wait):
            start = pad_ref[0, e]
            n_pad = pad_ref[1, e] - start
            off = start
            for sz in sizes:
                take = (n_pad & sz) != 0
                cp = pltpu.make_async_copy(zero_ref.at[pl.ds(0, sz)], xb_ref.at[pl.ds(off, sz)], sem_z)

                @pl.when(take)
                def _():
                    if do_wait:
                        cp.wait()
                    else:
                        cp.start()
                off = off + jnp.where(take, sz, 0)
            return carry

        half = zero_ref.shape[0]

        def fill_unused(j, carry, do_wait):
            for part in range(MOE_BLOCK // half):
                cp = pltpu.make_async_copy(zero_ref, xb_ref.at[pl.ds(j * MOE_BLOCK + part * half, half)], sem_z)
                if do_wait:
                    cp.wait()
                else:
                    cp.start()
            return carry

        n_blocks = xb_ref.shape[0] // MOE_BLOCK
        lax.fori_loop(0, N_EXPERTS, lambda e, c: fill(e, c, False), 0)
        lax.fori_loop(pad_ref[2, 0], n_blocks, lambda j, c: fill_unused(j, c, False), 0)
        lax.fori_loop(0, N_EXPERTS, lambda e, c: fill(e, c, True), 0)
        lax.fori_loop(pad_ref[2, 0], n_blocks, lambda j, c: fill_unused(j, c, True), 0)

    def issue(r, carry):
        for k in range(TOP_K):
            pltpu.make_async_copy(h_ref.at[r], xb_ref.at[dest_ref[r * TOP_K + k]], sem_s).start(priority=k % 2)
        return carry

    lax.fori_loop(0, tm, issue, 0)
    for k in range(TOP_K):
        pltpu.make_async_copy(h_ref, xb_ref.at[pl.ds(0, tm)], sem_s).wait()


def _dispatch(h2t, dest, meta, n_slots, tm):
    t = h2t.shape[0] // SUBLANES
    h3 = h2t.reshape(t, SUBLANES, LANES)
    return pl.pallas_call(
        _dispatch_kernel,
        grid_spec=pltpu.PrefetchScalarGridSpec(
            num_scalar_prefetch=1,
            grid=(t // tm,),
            in_specs=[pl.BlockSpec((tm * TOP_K,), lambda i, p: (i,), memory_space=pltpu.SMEM),
                      pl.BlockSpec((tm, SUBLANES, LANES), lambda i, p: (i, 0, 0))],
            out_specs=pl.BlockSpec(memory_space=pl.ANY),
            scratch_shapes=[pltpu.VMEM((MOE_BLOCK // 2, SUBLANES, LANES), F32),
                            pltpu.SemaphoreType.DMA, pltpu.SemaphoreType.DMA]),
        out_shape=jax.ShapeDtypeStruct((n_slots, SUBLANES, LANES), F32),
        compiler_params=_cparams("arbitrary"),
        name="moe_dispatch",
    )(meta[:3, :N_EXPERTS], dest, h3)


def _expert_kernel(be_ref, nv_ref, x_ref, wgu_ref, bgu_ref, wd_ref, bd_ref, y_ref, wgu_bf, wd_bf):
    j = pl.program_id(0)
    valid = j < nv_ref[0]
    prev = be_ref[jnp.maximum(j - 1, 0)]

    @pl.when(valid & ((j == 0) | (be_ref[j] != prev)))
    def _():
        wgu_bf[...] = wgu_ref[0].astype(BF16)
        wd_bf[...] = wd_ref[0].astype(BF16)

    @pl.when(valid)
    def _():
        x = jnp.concatenate([x_ref[pl.ds(c, MOE_BLOCK, stride=SUBLANES), :] for c in range(D_MODEL // LANES)],
                            axis=1).astype(BF16)
        gu = _bdot(x, wgu_bf[...]) + bgu_ref[0]
        x_glu = jnp.minimum(gu[:, :D_EXPERT], SWIGLU_LIMIT)
        x_lin = jnp.clip(gu[:, D_EXPERT:], -SWIGLU_LIMIT, SWIGLU_LIMIT)
        act = x_glu * _sigmoid(SWIGLU_ALPHA * x_glu) * (x_lin + 1.0)
        y = _bdot(act.astype(BF16), wd_bf[...]) + bd_ref[0]
        for c in range(D_MODEL // LANES):
            y_ref[pl.ds(c, MOE_BLOCK, stride=SUBLANES), :] = y[:, c * LANES:(c + 1) * LANES]

    @pl.when(jnp.logical_not(valid))
    def _():
        y_ref[...] = jnp.zeros_like(y_ref)


def _experts(xb, block_expert, n_valid, w_gu, b_gu, w_down, b_down, layer):
    n_slots = xb.shape[0]
    n_blocks = n_slots // MOE_BLOCK
    rows = MOE_BLOCK * SUBLANES
    x2 = xb.reshape(n_slots * SUBLANES, LANES)
    depth, ne, d, de2 = w_gu.shape
    last = lambda j, be, nv: jnp.minimum(j, nv[0] - 1)
    wmap = lambda j, be, nv: (layer, be[last(j, be, nv)], 0, 0)
    return pl.pallas_call(
        _expert_kernel,
        grid_spec=pltpu.PrefetchScalarGridSpec(
            num_scalar_prefetch=2,
            grid=(n_blocks,),
            in_specs=[pl.BlockSpec((rows, LANES), lambda j, be, nv: (last(j, be, nv), 0)),
                      pl.BlockSpec((None, 1, d, de2), wmap),
                      pl.BlockSpec((None, 1, 1, de2), wmap),
                      pl.BlockSpec((None, 1, de2 // 2, d), wmap),
                      pl.BlockSpec((None, 1, 1, d), wmap)],
            out_specs=pl.BlockSpec((rows, LANES), lambda j, be, nv: (j, 0)),
            scratch_shapes=[pltpu.VMEM((d, de2), BF16), pltpu.VMEM((de2 // 2, d), BF16)]),
        out_shape=jax.ShapeDtypeStruct((n_slots * SUBLANES, LANES), F32),
        compiler_params=_cparams("arbitrary"),
        name="moe_experts",
    )(block_expert, n_valid, x2, w_gu, b_gu.reshape(depth, ne, 1, de2), w_down, b_down.reshape(depth, ne, 1, d))


def _combine_kernel(dest_ref, yb_ref, gate_ref, x1_ref, g2_ref, gpost_ref, o_ref, buf, sem):
    tm = x1_ref.shape[0]

    def issue(r, carry):
        for k in range(TOP_K):
            src = pl.multiple_of(dest_ref[r * TOP_K + k] * SUBLANES, SUBLANES)
            dst = pl.multiple_of((k * tm + r) * SUBLANES, SUBLANES)
            pltpu.make_async_copy(yb_ref.at[pl.ds(src, SUBLANES), :], buf.at[pl.ds(dst, SUBLANES), :],
                                  sem).start(priority=k % 2)
        return carry

    lax.fori_loop(0, tm, issue, 0)
    pltpu.make_async_copy(yb_ref.at[pl.ds(0, TOP_K * tm * SUBLANES), :], buf, sem).wait()
    gates = gate_ref[...]
    gk = [jnp.broadcast_to(gates[:, k:k + 1], (tm, LANES)) for k in range(TOP_K)]
    b2 = buf
    cols = []
    for c in range(D_MODEL // LANES):
        acc = jnp.zeros((tm, LANES), F32)
        for k in range(TOP_K):
            acc = acc + gk[k] * b2[pl.ds(k * tm * SUBLANES + c, tm, stride=SUBLANES), :]
        cols.append(acc)
    f = jnp.concatenate(cols, axis=1)
    o_ref[...] = x1_ref[...] + g2_ref[0] * (_rms(f) * gpost_ref[...])


def _combine(yb, dest, gates, x1, mod3, g_post, seq, tm):
    t, d = x1.shape
    tps = seq // tm
    return pl.pallas_call(
        _combine_kernel,
        grid=(t // tm,),
        in_specs=[pl.BlockSpec((tm * TOP_K,), lambda i: (i,), memory_space=pltpu.SMEM),
                  pl.BlockSpec(memory_space=pl.ANY),
                  pl.BlockSpec((tm, LANES), lambda i: (i, 0)),
                  pl.BlockSpec((tm, d), lambda i: (i, 0)),
                  _mod_spec(5, tps),
                  pl.BlockSpec(g_post.shape, lambda i: (0, 0))],
        out_specs=pl.BlockSpec((tm, d), lambda i: (i, 0)),
        out_shape=jax.ShapeDtypeStruct((t, d), F32),
        scratch_shapes=[pltpu.VMEM((TOP_K * tm * SUBLANES, LANES), F32), pltpu.SemaphoreType.DMA],
        compiler_params=_cparams("arbitrary"),
        name="moe_combine",
    )(dest, yb, gates, x1, mod3, g_post)


def _moe(h2t, topi, gates, rank, cnt, x1, mod3, g_post, w_gu, b_gu, w_down, b_down, layer, seq):
    t = x1.shape[0]
    n_blocks = t * TOP_K // MOE_BLOCK + N_EXPERTS
    dest_l, blk, meta = _route(topi, rank, cnt, n_blocks, min(1024, t))
    dest = dest_l[:, :TOP_K].reshape(t * TOP_K)
    xb = _dispatch(h2t, dest, meta, n_blocks * MOE_BLOCK, MOE_BLOCK)
    yb = _experts(xb, blk[0, :n_blocks], meta[2, :1], w_gu, b_gu, w_down, b_down, layer)
    return _combine(yb, dest, gates, x1, mod3, g_post, seq, MOE_BLOCK)


TOKEN_TILE = 512
FOX_Q_TILE = 256
FOX_K_TILE = 256


def kernel(x, c, ada_w, ada_b, norm_pre_mix, norm_post_mix, norm_pre_ffn, norm_post_ffn, ev_w_in, fox_b_f, s5_lam_re, s5_lam_im, s5_log_dt, s5_b_re, s5_b_im, s5_c_re, s5_c_im, s5_d, s5_glu_w, s5_glu_b, ev_w_out, od_w_in, gla_w_up, gla_b_gate, od_w_out, router_w, router_b, exp_w_gu, exp_b_gu, exp_w_down, exp_b_down):
    bsz, seq, d = x.shape
    t = bsz * seq
    tm = min(TOKEN_TILE, seq)
    x2 = x.reshape(t, d)
    mod = _modulation(c, ada_w, ada_b)
    for l in range(DEPTH):
        i = l // 2
        mod3 = mod[l].reshape(bsz, 1, 6 * d)
        row = lambda a: a[l].reshape(1, -1)
        tail = (x2, mod3, row(norm_post_mix), row(norm_pre_ffn), router_w[l], router_b[l])
        if l % 2 == 0:
            u, q, k, v = _in_even(x2, mod3, row(norm_pre_mix), ev_w_in[i], fox_b_f[i], seq, tm)
            tables = _s5_tables(s5_lam_re[i], s5_lam_im[i], s5_log_dt[i], s5_b_re[i], s5_b_im[i],
                                s5_c_re[i], s5_c_im[i])
            ys = _s5_scan(u, bsz, seq, tables)
            yb = _fox(q, k, v, bsz, seq, min(FOX_Q_TILE, seq), min(FOX_K_TILE, seq))
            outs = _out_even(ys, u, yb, s5_d[i], s5_glu_w[i], s5_glu_b[i], ev_w_out[i], *tail, seq=seq, tm=tm)
        else:
            rq, rk, rv, sg, gq, gk, gv, sr, la = _in_odd(x2, mod3, row(norm_pre_mix), od_w_in[i],
                                                         gla_w_up[i], gla_b_gate[i], seq, tm)
            yc = _retention(rq, rk, rv, sg, bsz, seq)
            yd = _gla(gq, gk, la, gv, sr, bsz, seq)
            outs = _out_odd(yc, yd, od_w_out[i], *tail, seq=seq, tm=tm)
        x1, h2t, topi, gates, rank, cnt = outs
        x2 = _moe(h2t, topi, gates, rank, cnt, x1, mod3, row(norm_post_ffn),
                  exp_w_gu, exp_b_gu, exp_w_down, exp_b_down, l, seq)
    return x2.reshape(bsz, seq, d)
```

```python
import functools
import math

import jax
import jax.numpy as jnp
from jax import lax
from jax.experimental import pallas as pl
from jax.experimental.pallas import tpu as pltpu

F32 = jnp.float32
BF16 = jnp.bfloat16
HIGHEST = lax.Precision.HIGHEST

D_MODEL = 1024
DEPTH = 2
EPS = 1e-6
S5_CH = 512
S5_GROUP = 16
S5_GROUPS = S5_CH // S5_GROUP
S5_STATE = 64
S5_CHUNK = 16
FOX_HEADS = 8
FOX_DH = 64
FOX_W = FOX_HEADS * FOX_DH
FOX_HPS = 8
RET_HEADS = 4
RET_DK = 128
RET_DV = 128
ROPE_BASE = 10000.0
GLA_HEADS = 4
GLA_DK = 64
GLA_DV = 128
GLA_RANK = 16
GLA_TAU = 16.0
GLA_CHUNK = 64
GLA_SUB = 16
N_EXPERTS = 32
TOP_K = 4
D_EXPERT = 1024
SWIGLU_LIMIT = 7.0
SWIGLU_ALPHA = 1.702
MOE_BLOCK = 256

LANES = 128
SUBLANES = 8
VMEM_LIMIT = 56 * 1024 * 1024


def _cparams(*sem):
    return pltpu.CompilerParams(dimension_semantics=sem, vmem_limit_bytes=VMEM_LIMIT)


def _bdot(a, b):
    return jnp.dot(a, b, preferred_element_type=F32)


def _dot_nt(a, b):
    return lax.dot_general(a, b, (((1,), (1,)), ((), ())), preferred_element_type=F32)


def _dot_tn(a, b):
    return lax.dot_general(a, b, (((0,), (0,)), ((), ())), preferred_element_type=F32)


def _split3(x):
    hi = x.astype(BF16)
    r = x - hi.astype(F32)
    mid = r.astype(BF16)
    lo = (r - mid.astype(F32)).astype(BF16)
    return hi, mid, lo


def _dot01(m01, x):
    hi, mid, lo = _split3(x)
    return _bdot(m01, hi) + _bdot(m01, mid) + _bdot(m01, lo)


def _lower_tri(n, strict=False):
    r = lax.broadcasted_iota(jnp.int32, (n, n), 0)
    c = lax.broadcasted_iota(jnp.int32, (n, n), 1)
    return ((r > c) if strict else (r >= c)).astype(BF16)


def _log_sigmoid(x):
    return jnp.minimum(x, 0.0) - jnp.log1p(jnp.exp(-jnp.abs(x)))


def _sigmoid(x):
    return 1.0 / (1.0 + jnp.exp(-x))


def _silu(x):
    return x * _sigmoid(x)


def _rms(x):
    return x * lax.rsqrt(jnp.mean(x * x, axis=-1, keepdims=True) + EPS)


def _mod_kernel(c_ref, w_ref, b_ref, o_ref):
    c = c_ref[...]
    o_ref[0] = jnp.dot(_silu(c), w_ref[0], preferred_element_type=F32, precision=HIGHEST) + b_ref[0]


def _modulation(c, ada_w, ada_b):
    depth, d, n = ada_w.shape
    bsz = c.shape[0]
    tn = D_MODEL
    return pl.pallas_call(
        _mod_kernel,
        grid=(depth, n // tn),
        in_specs=[pl.BlockSpec((bsz, d), lambda l, j: (0, 0)),
                  pl.BlockSpec((1, d, tn), lambda l, j: (l, 0, j)),
                  pl.BlockSpec((1, 1, tn), lambda l, j: (l, 0, j))],
        out_specs=pl.BlockSpec((1, bsz, tn), lambda l, j: (l, 0, j)),
        out_shape=jax.ShapeDtypeStruct((depth, bsz, n), F32),
        compiler_params=_cparams("parallel", "parallel"),
        name="adaln_mod",
    )(c, ada_w, ada_b.reshape(depth, 1, n))


def _mod_spec(chunk, tiles_per_seq):
    return pl.BlockSpec((1, 1, D_MODEL), lambda i: (i // tiles_per_seq, 0, chunk))


def _prenorm(x, g_ref, sc_ref, sh_ref):
    return _rms(x) * (g_ref[...] * (1.0 + sc_ref[0])) + sh_ref[0]


def _in_even_kernel(x_ref, sh_ref, sc_ref, g_ref, w_ref, bf_ref,
                    u_ref, q_ref, k_ref, v_ref, carry_ref, *, tiles_per_seq):
    i = pl.program_id(0)
    tm = x_ref.shape[0]
    h = _prenorm(x_ref[...], g_ref, sc_ref, sh_ref)
    z = _bdot(h.astype(BF16), w_ref[...])
    u_ref[...] = z[:, 0:S5_CH]
    v_ref[...] = z[:, S5_CH + 2 * FOX_W:S5_CH + 3 * FOX_W].astype(BF16)
    ls = _log_sigmoid(z[:, S5_CH + 3 * FOX_W:] + bf_ref[...])

    @pl.when(i % tiles_per_seq == 0)
    def _():
        carry_ref[...] = jnp.zeros_like(carry_ref)

    cum = _dot01(_lower_tri(tm), ls) + carry_ref[...]
    carry_ref[...] = cum[tm - 1:tm, :]

    lane = lax.broadcasted_iota(jnp.int32, (1, LANES), 1)
    feat = lane < FOX_DH
    ones = jnp.where(lane < FOX_DH + 3, 1.0, 0.0)
    for hd in range(FOX_HEADS):
        blk = (hd * FOX_DH) // LANES * LANES
        qs = z[:, S5_CH + blk:S5_CH + blk + LANES] * (FOX_DH ** -0.5)
        ks = z[:, S5_CH + FOX_W + blk:S5_CH + FOX_W + blk + LANES]
        if (hd * FOX_DH) % LANES:
            qs = pltpu.roll(qs, LANES - FOX_DH, 1)
            ks = pltpu.roll(ks, LANES - FOX_DH, 1)
        nf = jnp.broadcast_to(-cum[:, hd:hd + 1], (tm, LANES))
        hi = nf.astype(BF16).astype(F32)
        mid = (nf - hi).astype(BF16).astype(F32)
        lo = nf - hi - mid
        bias = jnp.where(lane == FOX_DH, hi, jnp.where(lane == FOX_DH + 1, mid,
                                                       jnp.where(lane == FOX_DH + 2, lo, 0.0)))
        q_ref[:, hd * LANES:(hd + 1) * LANES] = jnp.where(feat, qs, ones).astype(BF16)
        k_ref[:, hd * LANES:(hd + 1) * LANES] = jnp.where(feat, ks, bias).astype(BF16)


def _in_even(x2, mod3, gain, w_in, b_f, seq, tm):
    t, d = x2.shape
    tiles_per_seq = seq // tm
    nw = S5_CH + 3 * FOX_W
    w = jnp.concatenate([w_in[:, :nw], jnp.pad(w_in[:, nw:], ((0, 0), (0, LANES - FOX_HEADS)))],
                        axis=1).astype(BF16)
    bf = jnp.pad(b_f, (0, LANES - FOX_HEADS)).reshape(1, LANES)
    tok = lambda n: pl.BlockSpec((tm, n), lambda i: (i, 0))
    full = lambda a: pl.BlockSpec(a.shape, lambda i: (0,) * a.ndim)
    return pl.pallas_call(
        functools.partial(_in_even_kernel, tiles_per_seq=tiles_per_seq),
        grid=(t // tm,),
        in_specs=[tok(d), _mod_spec(0, tiles_per_seq), _mod_spec(1, tiles_per_seq),
                  full(gain), full(w), full(bf)],
        out_specs=[tok(S5_CH), tok(FOX_HEADS * LANES), tok(FOX_HEADS * LANES), tok(FOX_W)],
        out_shape=[jax.ShapeDtypeStruct((t, S5_CH), F32),
                   jax.ShapeDtypeStruct((t, FOX_HEADS * LANES), BF16),
                   jax.ShapeDtypeStruct((t, FOX_HEADS * LANES), BF16),
                   jax.ShapeDtypeStruct((t, FOX_W), BF16)],
        scratch_shapes=[pltpu.VMEM((1, LANES), F32)],
        compiler_params=_cparams("arbitrary"),
        name="in_proj_even",
    )(x2, mod3, mod3, gain, w, bf)


def _s5_kernel(u_ref, toep_ref, wst_ref, cst_ref, a_ref, y_ref, e_scr, hp_scr, *, n_chunks, nb):
    u = u_ref[0]
    e_scr[...] = _bdot(u, wst_ref[0])
    two_p = 2 * S5_STATE
    a1 = jnp.broadcast_to(a_ref[0, 0:1, :], (nb, two_p))
    a2 = jnp.broadcast_to(a_ref[0, 1:2, :], (nb, two_p))
    a2s = jnp.broadcast_to(a_ref[0, 2:3, :], (nb, two_p))

    def body(c, carry):
        h, hs = carry
        r = pl.multiple_of(c * nb, nb)
        hp_scr[pl.ds(r, nb), :] = h
        e = e_scr[pl.ds(r, nb), :]
        return a1 * h + a2 * hs + e[:, :two_p], a1 * hs + a2s * h + e[:, two_p:]

    zero = jnp.zeros((nb, two_p), F32)
    lax.fori_loop(0, n_chunks, body, (zero, zero))
    y_ref[0] = _bdot(u, toep_ref[0]) + _bdot(hp_scr[...].astype(BF16), cst_ref[0])


def _s5_tables(lam_re, lam_im, log_dt, b_re, b_im, c_re, c_im):
    ell, p, g = S5_CHUNK, S5_STATE, S5_GROUPS
    lr, li = lam_re.astype(F32), lam_im.astype(F32)
    dt = jnp.exp(log_dt.astype(F32))[:, None]
    mag = jnp.exp(lr * dt)
    a_re, a_im = mag * jnp.cos(li * dt), mag * jnp.sin(li * dt)
    den = lr * lr + li * li
    n_re, n_im = a_re - 1.0, a_im
    z_re = (n_re * lr + n_im * li) / den
    z_im = (n_im * lr - n_re * li) / den
    br, bi = b_re.astype(F32), b_im.astype(F32)
    bb_re = z_re[..., None] * br - z_im[..., None] * bi
    bb_im = z_re[..., None] * bi + z_im[..., None] * br
    j = jnp.arange(ell + 1, dtype=F32)[:, None, None]
    pmag = jnp.exp(j * (lr * dt)[None])
    pr, pi = pmag * jnp.cos(j * (li * dt)[None]), pmag * jnp.sin(j * (li * dt)[None])
    w_re = pr[..., None] * bb_re[None] - pi[..., None] * bb_im[None]
    w_im = pr[..., None] * bb_im[None] + pi[..., None] * bb_re[None]
    cr, ci = c_re.astype(F32), c_im.astype(F32)
    kern = (jnp.einsum('gcp,jgpd->jgcd', cr, w_re[:ell], precision=HIGHEST)
            - jnp.einsum('gcp,jgpd->jgcd', ci, w_im[:ell], precision=HIGHEST))
    s_idx = jnp.arange(ell)[:, None]
    t_idx = jnp.arange(ell)[None, :]
    lag = jnp.clip(t_idx - s_idx, 0, ell - 1)
    toep = kern[lag]
    toep = jnp.where((t_idx >= s_idx)[:, :, None, None, None], toep, 0.0)
    toep = toep.transpose(2, 0, 4, 1, 3).reshape(g, ell * S5_GROUP, ell * S5_GROUP)
    rev = jnp.arange(ell - 1, -1, -1)
    st_re = w_re[rev].transpose(1, 0, 3, 2).reshape(g, ell * S5_GROUP, p)
    st_im = w_im[rev].transpose(1, 0, 3, 2).reshape(g, ell * S5_GROUP, p)
    wst = jnp.concatenate([st_re, st_im, st_im, st_re], axis=-1)
    p1r, p1i = pr[1:], pi[1:]
    c_hr = cr[None] * p1r[:, :, None, :] - ci[None] * p1i[:, :, None, :]
    c_hi = -cr[None] * p1i[:, :, None, :] - ci[None] * p1r[:, :, None, :]
    cst = jnp.concatenate([c_hr, c_hi], axis=-1).transpose(1, 3, 0, 2).reshape(g, 2 * p, ell * S5_GROUP)
    al_r, al_i = pr[ell], pi[ell]
    a_rows = jnp.stack([jnp.concatenate([al_r, al_r], -1), jnp.concatenate([-al_i, al_i], -1),
                        jnp.concatenate([al_i, -al_i], -1)], axis=1)
    a_rows = jnp.pad(a_rows, ((0, 0), (0, SUBLANES - 3), (0, 0)))
    return toep.astype(BF16), wst.astype(BF16), cst.astype(BF16), a_rows


def _s5_scan(u, bsz, seq, tables):
    toep, wst, cst, a_rows = tables
    ell, g = S5_CHUNK, S5_GROUPS
    n_chunks = seq // ell
    rows = n_chunks * bsz
    w = ell * S5_GROUP
    ug = (u.reshape(bsz, n_chunks, ell, g, S5_GROUP).transpose(3, 1, 0, 2, 4)
          .reshape(g, rows, w).astype(BF16))
    grp = lambda a: pl.BlockSpec((1,) + a.shape[1:], lambda i: (i, 0, 0))
    yg = pl.pallas_call(
        functools.partial(_s5_kernel, n_chunks=n_chunks, nb=bsz),
        grid=(g,),
        in_specs=[grp(ug), grp(toep), grp(wst), grp(cst), grp(a_rows)],
        out_specs=pl.BlockSpec((1, rows, w), lambda i: (i, 0, 0)),
        out_shape=jax.ShapeDtypeStruct((g, rows, w), F32),
        scratch_shapes=[pltpu.VMEM((rows, 4 * S5_STATE), F32), pltpu.VMEM((rows, 2 * S5_STATE), F32)],
        compiler_params=_cparams("parallel"),
        name="s5_scan",
    )(ug, toep, wst, cst, a_rows)
    return (yg.reshape(g, n_chunks, bsz, ell, S5_GROUP).transpose(2, 1, 3, 0, 4)
            .reshape(bsz * seq, S5_CH))


def _fox_kernel(q_ref, k_ref, vt_ref, o_ref, *, tq, tk):
    i = pl.program_id(2)
    nh = FOX_HPS
    q = [q_ref[:, h * LANES:(h + 1) * LANES] for h in range(nh)]
    key = lax.broadcasted_iota(jnp.int32, (tk, tq), 0)
    qry = lax.broadcasted_iota(jnp.int32, (tk, tq), 1)
    per_q = tq // tk

    def block(j0, carry, mask):
        kj = k_ref[pl.ds(j0, tk), :]
        vtj = vt_ref[:, pl.ds(j0, tk)]
        ss = [_dot_nt(kj[:, h * LANES:(h + 1) * LANES], q[h]) for h in range(nh)]
        stats = []
        for h in range(nh):
            m, l, acc = carry[h]
            s = ss[h] if mask is None else jnp.where(mask, ss[h], -jnp.inf)
            m_new = jnp.maximum(m, jnp.max(s, axis=0, keepdims=True))
            p = jnp.exp(s - m_new)
            alpha = jnp.exp(m - m_new)
            stats.append((m_new, alpha * l + jnp.sum(p, axis=0, keepdims=True), alpha, p.astype(BF16)))
        out = []
        for h in range(nh):
            m_new, l, alpha, p = stats[h]
            acc = alpha * carry[h][2] + _bdot(vtj[h * FOX_DH:(h + 1) * FOX_DH, :], p)
            out.append((m_new, l, acc))
        return tuple(out)

    init = tuple((jnp.full((1, tq), -jnp.inf, F32), jnp.zeros((1, tq), F32), jnp.zeros((FOX_DH, tq), F32))
                 for _ in range(nh))
    carry = lax.fori_loop(0, i * per_q, lambda j, c: block(pl.multiple_of(j * tk, tk), c, None), init)
    for d in range(per_q):
        carry = block(pl.multiple_of(i * tq + d * tk, tk), carry, key + d * tk <= qry)
    for g in range(nh // 2):
        o_t = jnp.concatenate([carry[h][2] / carry[h][1] for h in (2 * g, 2 * g + 1)], axis=0)
        o_ref[:, g * LANES:(g + 1) * LANES] = o_t.T.astype(o_ref.dtype)


def _fox(q_aug, k_aug, v, bsz, seq, tq, tk):
    t = v.shape[0]
    nh = FOX_HPS
    groups = FOX_HEADS // nh
    nq = seq // tq
    v_t = v.reshape(bsz, seq, FOX_W).transpose(0, 2, 1).reshape(bsz * FOX_W, seq)
    return pl.pallas_call(
        functools.partial(_fox_kernel, tq=tq, tk=tk),
        grid=(bsz, groups, nq),
        in_specs=[pl.BlockSpec((tq, nh * LANES), lambda b, p, i: (b * nq + i, p)),
                  pl.BlockSpec((seq, nh * LANES), lambda b, p, i: (b, p)),
                  pl.BlockSpec((nh * FOX_DH, seq), lambda b, p, i: (b * groups + p, 0))],
        out_specs=pl.BlockSpec((tq, nh * FOX_DH), lambda b, p, i: (b * nq + i, p)),
        out_shape=jax.ShapeDtypeStruct((t, FOX_W), BF16),
        compiler_params=_cparams("parallel", "parallel", "arbitrary"),
        name="fox_attention",
    )(q_aug, k_aug, v_t)


_ODD_COLS = (("rq", 512), ("rk", 512), ("rv", 512), ("rg", 512), ("gq", 256), ("gk", 256),
             ("gv", 512), ("gr", 512), ("glr", LANES))


def _odd_offsets():
    off, out = 0, {}
    for name, w in _ODD_COLS:
        out[name] = (off, off + w)
        off += w
    return out, off


def _in_odd_kernel(x_ref, sh_ref, sc_ref, g_ref, w_ref, cos_ref, sin_ref, wup_ref, bg_ref,
                   rq_ref, rk_ref, rv_ref, sg_ref, gq_ref, gk_ref, gv_ref, sr_ref, la_ref):
    h = _prenorm(x_ref[...], g_ref, sc_ref, sh_ref)
    z = _bdot(h.astype(BF16), w_ref[...])
    off, _ = _odd_offsets()
    col = lambda n: z[:, off[n][0]:off[n][1]]
    cos, sin = cos_ref[...], sin_ref[...]

    def rope(t, scale):
        heads = []
        for hd in range(RET_HEADS):
            th = t[:, hd * RET_DK:(hd + 1) * RET_DK]
            heads.append((th * cos + pltpu.roll(th, RET_DK // 2, 1) * sin) * scale)
        return jnp.concatenate(heads, axis=1).astype(BF16)

    rq_ref[...] = rope(col("rq"), 1.0)
    rk_ref[...] = rope(col("rk"), RET_DK ** -0.5)
    rv_ref[...] = col("rv").astype(BF16)
    sg_ref[...] = _silu(col("rg"))
    gq_ref[...] = col("gq") * (GLA_DK ** -0.5)
    gk_ref[...] = col("gk")
    gv_ref[...] = col("gv").astype(BF16)
    sr_ref[...] = _silu(col("gr"))
    gate = jnp.dot(col("glr"), wup_ref[...], preferred_element_type=F32, precision=HIGHEST) + bg_ref[...]
    la_ref[...] = _log_sigmoid(gate) * (1.0 / GLA_TAU)


def _in_odd(x2, mod3, gain, w_in, w_up, b_gate, seq, tm):
    t, d = x2.shape
    tps = seq // tm
    ref_w = (512, 512, 512, 512, 256, 256, 512, GLA_RANK, 512)
    starts = [0]
    for wd in ref_w:
        starts.append(starts[-1] + wd)
    seg = lambda j: w_in[:, starts[j]:starts[j + 1]]
    w = jnp.concatenate([seg(0), seg(1), seg(2), seg(3), seg(4), seg(5), seg(6), seg(8),
                         jnp.pad(seg(7), ((0, 0), (0, LANES - GLA_RANK)))], axis=1).astype(BF16)
    wup = jnp.pad(w_up.astype(F32), ((0, LANES - GLA_RANK), (0, 0)))
    bg = b_gate.reshape(1, -1).astype(F32)
    half = RET_DK // 2
    inv = ROPE_BASE ** (-jnp.arange(half, dtype=F32) / half)
    ang = jnp.arange(seq, dtype=F32)[:, None] * inv[None, :]
    cos = jnp.concatenate([jnp.cos(ang), jnp.cos(ang)], axis=1)
    sin = jnp.concatenate([-jnp.sin(ang), jnp.sin(ang)], axis=1)
    tok = lambda n: pl.BlockSpec((tm, n), lambda i: (i, 0))
    full = lambda a: pl.BlockSpec(a.shape, lambda i: (0,) * a.ndim)
    pos = pl.BlockSpec((tm, RET_DK), lambda i: (i % tps, 0))
    widths = (512, 512, 512, 512, 256, 256, 512, 512, 256)
    dtypes = (BF16, BF16, BF16, F32, F32, F32, BF16, F32, F32)
    return pl.pallas_call(
        _in_odd_kernel,
        grid=(t // tm,),
        in_specs=[tok(d), _mod_spec(0, tps), _mod_spec(1, tps), full(gain), full(w), pos, pos,
                  full(wup), full(bg)],
        out_specs=[tok(n) for n in widths],
        out_shape=[jax.ShapeDtypeStruct((t, n), dt) for n, dt in zip(widths, dtypes)],
        compiler_params=_cparams("parallel"),
        name="in_proj_odd",
    )(x2, mod3, mod3, gain, w, cos, sin, wup, bg)


RET_CHUNK = 256


def _ret_kernel(q_ref, k_ref, v_ref, sg_ref, dm_ref, xi_ref, zeta_ref, gl_ref, y_ref, st_ref):
    @pl.when(pl.program_id(1) == 0)
    def _():
        st_ref[...] = jnp.zeros_like(st_ref)

    heads = range(RET_HEADS)
    col = lambda h: slice(h * RET_DK, (h + 1) * RET_DK)
    q = [q_ref[:, col(h)] for h in heads]
    k = [k_ref[:, col(h)] for h in heads]
    v = [v_ref[:, col(h)] for h in heads]
    st = [st_ref[h] for h in heads]
    s = [_dot_nt(q[h], k[h]) for h in heads]
    inter = [_bdot((q[h].astype(F32) * xi_ref[h]).astype(BF16), st[h].astype(BF16)) for h in heads]
    upd = [_dot_tn((k[h].astype(F32) * zeta_ref[h]).astype(BF16), v[h]) for h in heads]
    for h in heads:
        o = _bdot((s[h] * dm_ref[h]).astype(BF16), v[h]) + inter[h]
        st_ref[h] = gl_ref[h, 0:1, :] * st[h] + upd[h]
        y_ref[:, col(h)] = (sg_ref[:, col(h)] * _rms(o)).astype(y_ref.dtype)


def _retention(rq, rk, rv, sg, bsz, seq):
    t = rq.shape[0]
    ell = min(RET_CHUNK, seq)
    nc = seq // ell
    log_g = jnp.log(1.0 - jnp.exp2(-5.0 - jnp.arange(RET_HEADS, dtype=F32)))
    idx = jnp.arange(ell, dtype=F32)
    rel = idx[:, None] - idx[None, :]
    dmat = jnp.where(rel >= 0, jnp.exp(log_g[:, None, None] * jnp.maximum(rel, 0.0)), 0.0)
    lanes = lambda a: jnp.broadcast_to(a[..., None], a.shape + (RET_DK,))
    xi = lanes(jnp.exp(log_g[:, None] * (idx + 1.0)))
    zeta = lanes(jnp.exp(log_g[:, None] * (ell - 1.0 - idx)))
    gl = jnp.broadcast_to(jnp.exp(log_g * ell)[:, None, None], (RET_HEADS, SUBLANES, RET_DV))
    blk = pl.BlockSpec((ell, RET_HEADS * RET_DK), lambda b, c: (b * nc + c, 0))
    full = lambda a: pl.BlockSpec(a.shape, lambda b, c: (0, 0, 0))
    return pl.pallas_call(
        _ret_kernel,
        grid=(bsz, nc),
        in_specs=[blk, blk, blk, blk, full(dmat), full(xi), full(zeta), full(gl)],
        out_specs=blk,
        out_shape=jax.ShapeDtypeStruct((t, RET_HEADS * RET_DV), BF16),
        scratch_shapes=[pltpu.VMEM((RET_HEADS, RET_DK, RET_DV), F32)],
        compiler_params=_cparams("parallel", "arbitrary"),
        name="retention",
    )(rq, rk, rv, sg, dmat, xi, zeta, gl)


def _gla_kernel(q_ref, k_ref, la_ref, v_ref, sg_ref, y_ref, st_ref, b_scr, v_scr, p_scr, r_scr):
    @pl.when(pl.program_id(1) == 0)
    def _():
        st_ref[...] = jnp.zeros_like(st_ref)

    ell, sub = GLA_CHUNK, GLA_SUB
    n_sub = ell // sub
    nb = q_ref.shape[0]
    pairs = GLA_HEADS // 2
    streams = [(bl, p) for bl in range(nb) for p in range(pairs)]
    lane = lax.broadcasted_iota(jnp.int32, (1, LANES), 1)
    first = lane < GLA_DK
    head = (first, jnp.logical_not(first))
    pick = lambda h, a: jnp.where(head[h], a, 0.0).astype(BF16)
    tri = _lower_tri(ell)
    tau = lax.broadcasted_iota(jnp.int32, (sub, LANES), 0)
    rsub = lax.broadcasted_iota(jnp.int32, (LANES, 2 * LANES), 0)
    csub = lax.broadcasted_iota(jnp.int32, (LANES, 2 * LANES), 1)
    ind = ((rsub < GLA_DK) == (csub < LANES)).astype(BF16)

    val = {}
    for sid, (bl, p) in enumerate(streams):
        qk = slice(p * LANES, (p + 1) * LANES)
        q, k = q_ref[bl, :, qk], k_ref[bl, :, qk]
        b = _dot01(tri, la_ref[bl, :, qk])
        b_scr[sid] = b
        v_bf = v_ref[bl, :, p * 2 * GLA_DV:(p + 1) * 2 * GLA_DV]
        v_scr[sid] = v_bf.astype(F32)
        st = st_ref[sid]
        val[sid] = dict(q=q, k=k, b=b, st=st, vh=[v_bf[:, h * GLA_DV:(h + 1) * GLA_DV] for h in range(2)])

    for sid in val:
        d = val[sid]
        qe = d["q"] * jnp.exp(d["b"])
        st_bf = d["st"].astype(BF16)
        d["o"] = [_dot_nt(pick(h, qe), st_bf) for h in range(2)]

    row = lax.broadcasted_iota(jnp.int32, (ell, LANES), 0)
    for sid in val:
        d = val[sid]
        q, k, b = d["q"], d["k"], d["b"]
        qa, ka = [], []
        for i in range(1, n_sub):
            lo = i * sub
            ref_row = b[lo - 1:lo, :]
            in_i = (row >= lo) & (row < lo + sub)
            qa.append(jnp.where(in_i, q * jnp.exp(jnp.minimum(b - ref_row, 0.0)), 0.0))
            ka.append(jnp.where(row < lo, k * jnp.exp(jnp.minimum(ref_row - b, 0.0)), 0.0))
        k_cat = jnp.concatenate(ka, axis=1).astype(BF16)
        d["a_off"] = [_dot_nt(jnp.concatenate([pick(h, x) for x in qa], axis=1), k_cat) for h in range(2)]
    for sid in val:
        d = val[sid]
        d["off"] = [_bdot(d["a_off"][h].astype(BF16), d["vh"][h]) for h in range(2)]

    for sid, (bl, p) in enumerate(streams):
        d = val[sid]
        q, b = d["q"], d["b"]
        for i in range(n_sub):
            lo = i * sub
            qi, bi = q[lo:lo + sub], b[lo:lo + sub]
            for s in range(sub):
                k_row = k_ref[bl, pl.ds(lo + s, 1), p * LANES:(p + 1) * LANES]
                b_row = b_scr[sid, pl.ds(lo + s, 1), :]
                w = jnp.exp(jnp.minimum(bi - b_row, 0.0))
                p_scr[sid, pl.ds((lo + s) * sub, sub), :] = jnp.where(tau >= s, qi * k_row * w, 0.0).astype(BF16)
    for sid in val:
        r_scr[sid] = _bdot(p_scr[sid], ind)
    for sid in val:
        diag = [[], []]
        for i in range(n_sub):
            lo = i * sub
            for h in range(2):
                acc = jnp.zeros((sub, GLA_DV), F32)
                for s in range(sub):
                    v_row = v_scr[sid, pl.ds(lo + s, 1), h * GLA_DV:(h + 1) * GLA_DV]
                    acc = acc + r_scr[sid, pl.ds((lo + s) * sub, sub), h * LANES:(h + 1) * LANES] * v_row
                diag[h].append(acc)
        val[sid]["diag"] = diag

    for sid, (bl, p) in enumerate(streams):
        d = val[sid]
        b_last = d["b"][ell - 1:ell, :]
        kh = (d["k"] * jnp.exp(b_last - d["b"])).astype(BF16)
        upd = [_dot_tn(d["vh"][h], kh) for h in range(2)]
        st_ref[sid] = d["st"] * jnp.exp(b_last) + jnp.where(first, upd[0], upd[1])
        for h in range(2):
            oh = d["o"][h] + d["off"][h] + jnp.concatenate(d["diag"][h], axis=0)
            cols = slice((2 * p + h) * GLA_DV, (2 * p + h + 1) * GLA_DV)
            y_ref[bl, :, cols] = (sg_ref[bl, :, cols] * _rms(oh)).astype(y_ref.dtype)


GLA_BATCHES = 2


def _gla(gq, gk, la, gv, sr, bsz, seq):
    t = gq.shape[0]
    ell = GLA_CHUNK
    nc = seq // ell
    nb = min(GLA_BATCHES, bsz)
    ns = nb * (GLA_HEADS // 2)
    r3 = lambda a: a.reshape(bsz, seq, a.shape[1])
    spec = lambda w: pl.BlockSpec((nb, ell, w), lambda b, c: (b, c, 0))
    wq, wv = GLA_HEADS * GLA_DK, GLA_HEADS * GLA_DV
    y = pl.pallas_call(
        _gla_kernel,
        grid=(bsz // nb, nc),
        in_specs=[spec(wq), spec(wq), spec(wq), spec(wv), spec(wv)],
        out_specs=spec(wv),
        out_shape=jax.ShapeDtypeStruct((bsz, seq, wv), BF16),
        scratch_shapes=[pltpu.VMEM((ns, GLA_DV, LANES), F32),
                        pltpu.VMEM((ns, ell, LANES), F32),
                        pltpu.VMEM((ns, ell, 2 * GLA_DV), F32),
                        pltpu.VMEM((ns, ell * GLA_SUB, LANES), BF16),
                        pltpu.VMEM((ns, ell * GLA_SUB, 2 * LANES), F32)],
        compiler_params=_cparams("parallel", "arbitrary"),
        name="gla",
    )(r3(gq), r3(gk), r3(la), r3(gv), r3(sr))
    return y.reshape(t, wv)


def _post_tail(m, x_ref, g1_ref, gpost_ref, sh2_ref, sc2_ref, gpre_ref, rw_ref, rb_ref,
               x1_ref, h2_ref, topi_ref, gate_ref, rank_ref, cnt_ref, carry_ref):
    i = pl.program_id(0)
    tm = m.shape[0]
    x1 = x_ref[...] + g1_ref[0] * (_rms(m) * gpost_ref[...])
    x1_ref[...] = x1
    h2 = _rms(x1) * (gpre_ref[...] * (1.0 + sc2_ref[0])) + sh2_ref[0]
    for c in range(D_MODEL // LANES):
        h2_ref[pl.ds(c, tm, stride=SUBLANES), :] = h2[:, c * LANES:(c + 1) * LANES]
    h_hi = h2.astype(BF16)
    h_lo = (h2 - h_hi.astype(F32)).astype(BF16)
    w_hi, w_lo = rw_ref[0], rw_ref[1]
    logits = _bdot(h_hi, w_hi) + _bdot(h_lo, w_hi) + _bdot(h_hi, w_lo) + rb_ref[...]

    lane = lax.broadcasted_iota(jnp.int32, (tm, LANES), 1)
    lanef = lane.astype(F32)
    work = logits
    topv = jnp.full((tm, LANES), -jnp.inf, F32)
    topi = jnp.zeros((tm, LANES), F32)
    onehot = jnp.zeros((tm, LANES), F32)
    hits = []
    for k in range(TOP_K):
        mx = jnp.max(work, axis=-1, keepdims=True)
        idx = jnp.min(jnp.where(work == mx, lanef, float(LANES)), axis=-1, keepdims=True)
        hit = lanef == idx
        hits.append(hit)
        topv = jnp.where(lane == k, mx, topv)
        topi = jnp.where(lane == k, idx, topi)
        onehot = onehot + hit.astype(F32)
        work = jnp.where(hit, -jnp.inf, work)
    e = jnp.exp(topv - jnp.max(topv, axis=-1, keepdims=True))
    gate_ref[...] = e / jnp.sum(e, axis=-1, keepdims=True)
    topi_ref[...] = topi.astype(jnp.int32)

    @pl.when(i == 0)
    def _():
        carry_ref[...] = jnp.zeros_like(carry_ref)

    before = _bdot(_lower_tri(tm, strict=True), onehot.astype(BF16)) + carry_ref[...]
    rank = jnp.zeros((tm, LANES), F32)
    for k in range(TOP_K):
        rk = jnp.sum(jnp.where(hits[k], before, 0.0), axis=-1, keepdims=True)
        rank = jnp.where(lane == k, rk, rank)
    rank_ref[...] = rank.astype(jnp.int32)
    total = before[tm - 1:tm, :] + onehot[tm - 1:tm, :]
    carry_ref[...] = total
    cnt_ref[...] = jnp.broadcast_to(total, cnt_ref.shape)


def _out_even_kernel(ys_ref, u_ref, yb_ref, d_ref, gw_ref, gb_ref, wa_ref, wb_ref, *rest):
    y = ys_ref[...] + d_ref[...] * u_ref[...]
    g = jax.nn.gelu(y)
    ya = g * _sigmoid(_bdot(g.astype(BF16), gw_ref[...]) + gb_ref[...])
    m = _bdot(ya.astype(BF16), wa_ref[...]) + _bdot(yb_ref[...], wb_ref[...])
    _post_tail(m, *rest)


def _out_odd_kernel(yc_ref, yd_ref, wa_ref, wb_ref, *rest):
    m = _bdot(yc_ref[...], wa_ref[...]) + _bdot(yd_ref[...], wb_ref[...])
    _post_tail(m, *rest)


def _mixer_out(body, mix_args, mix_specs, x2, mod3, g_post, g_pre, router_w, router_b, seq, tm):
    t, d = x2.shape
    tps = seq // tm
    rw32 = jnp.pad(router_w.astype(F32), ((0, 0), (0, LANES - N_EXPERTS)))
    rw_hi = rw32.astype(BF16)
    rw = jnp.stack([rw_hi, (rw32 - rw_hi.astype(F32)).astype(BF16)])
    rb = jnp.pad(router_b.astype(F32), (0, LANES - N_EXPERTS), constant_values=-1e30).reshape(1, LANES)
    tok = lambda n: pl.BlockSpec((tm, n), lambda i: (i, 0))
    full = lambda a: pl.BlockSpec(a.shape, lambda i: (0,) * a.ndim)
    tail_args = [x2, mod3, g_post, mod3, mod3, g_pre, rw, rb]
    tail_specs = [tok(d), _mod_spec(2, tps), full(g_post), _mod_spec(3, tps), _mod_spec(4, tps),
                  full(g_pre), full(rw), full(rb)]
    return pl.pallas_call(
        body,
        grid=(t // tm,),
        in_specs=mix_specs + tail_specs,
        out_specs=[tok(d), pl.BlockSpec((tm * SUBLANES, LANES), lambda i: (i, 0)),
                   tok(LANES), tok(LANES), tok(LANES), pl.BlockSpec((SUBLANES, LANES), lambda i: (0, 0))],
        out_shape=[jax.ShapeDtypeStruct((t, d), F32),
                   jax.ShapeDtypeStruct((t * SUBLANES, LANES), F32),
                   jax.ShapeDtypeStruct((t, LANES), jnp.int32),
                   jax.ShapeDtypeStruct((t, LANES), F32),
                   jax.ShapeDtypeStruct((t, LANES), jnp.int32),
                   jax.ShapeDtypeStruct((SUBLANES, LANES), F32)],
        scratch_shapes=[pltpu.VMEM((1, LANES), F32)],
        compiler_params=_cparams("arbitrary"),
        name="mixer_out_router",
    )(*mix_args, *tail_args)


def _out_even(ys, u, yb, d_skip, glu_w, glu_b, w_out, *tail, seq, tm):
    tok = lambda n: pl.BlockSpec((tm, n), lambda i: (i, 0))
    full = lambda a: pl.BlockSpec(a.shape, lambda i: (0,) * a.ndim)
    args = [ys, u, yb, d_skip.reshape(1, -1), glu_w.astype(BF16), glu_b.reshape(1, -1),
            w_out[:S5_CH].astype(BF16), w_out[S5_CH:].astype(BF16)]
    specs = [tok(S5_CH), tok(S5_CH), tok(FOX_W)] + [full(a) for a in args[3:]]
    return _mixer_out(_out_even_kernel, args, specs, *tail, seq, tm)


def _out_odd(yc, yd, w_out, *tail, seq, tm):
    tok = lambda n: pl.BlockSpec((tm, n), lambda i: (i, 0))
    full = lambda a: pl.BlockSpec(a.shape, lambda i: (0,) * a.ndim)
    nc = yc.shape[1]
    args = [yc, yd, w_out[:nc].astype(BF16), w_out[nc:].astype(BF16)]
    specs = [tok(nc), tok(yd.shape[1])] + [full(a) for a in args[2:]]
    return _mixer_out(_out_odd_kernel, args, specs, *tail, seq, tm)


def _route_kernel(topi_ref, rank_ref, cnt_ref, dest_ref, blk_ref, meta_ref):
    tm = topi_ref.shape[0]
    cnt = cnt_ref[...]
    padded = jnp.floor((cnt + (MOE_BLOCK - 1.0)) * (1.0 / MOE_BLOCK)) * MOE_BLOCK
    r = lax.broadcasted_iota(jnp.int32, (LANES, LANES), 0)
    c = lax.broadcasted_iota(jnp.int32, (LANES, LANES), 1)
    hi, mid, lo = _split3(padded)
    incl = (r <= c).astype(BF16)
    pad_end = _bdot(hi, incl) + _bdot(mid, incl) + _bdot(lo, incl)
    pad_start = pad_end - padded
    lane = lax.broadcasted_iota(jnp.int32, (tm, LANES), 1)
    lanef = lane.astype(F32)
    topi = topi_ref[...].astype(F32)
    start_row = pad_start[0:1, :]
    dest = jnp.zeros((tm, LANES), F32)
    for k in range(TOP_K):
        idx = jnp.sum(jnp.where(lane == k, topi, 0.0), axis=-1, keepdims=True)
        st = jnp.sum(jnp.where(lanef == idx, start_row, 0.0), axis=-1, keepdims=True)
        dest = jnp.where(lane == k, st, dest)
    dest_ref[...] = dest.astype(jnp.int32) + rank_ref[...]

    nb = blk_ref.shape[1]
    end_col = jnp.sum(jnp.where(r == c, jnp.broadcast_to(pad_end[0:1, :], (LANES, LANES)), 0.0),
                      axis=-1, keepdims=True)
    jpos = lax.broadcasted_iota(jnp.int32, (LANES, nb), 1).astype(F32) * MOE_BLOCK
    esub = lax.broadcasted_iota(jnp.int32, (LANES, nb), 0)
    done = jnp.where((end_col <= jpos) & (esub < N_EXPERTS), 1.0, 0.0)
    be = jnp.minimum(jnp.sum(done, axis=0, keepdims=True), N_EXPERTS - 1.0)
    blk_ref[...] = jnp.broadcast_to(be, blk_ref.shape).astype(jnp.int32)
    lane1 = lax.broadcasted_iota(jnp.int32, (SUBLANES, LANES), 1)
    n_valid = jnp.sum(jnp.where(lane1 == N_EXPERTS - 1, pad_end, 0.0), axis=-1, keepdims=True) * (1.0 / MOE_BLOCK)
    sub1 = lax.broadcasted_iota(jnp.int32, (SUBLANES, LANES), 0)
    meta = jnp.where(sub1 == 0, pad_start + cnt, jnp.where(sub1 == 1, pad_end, jnp.broadcast_to(n_valid, (SUBLANES, LANES))))
    meta_ref[...] = meta.astype(jnp.int32)


def _route(topi, rank, cnt, n_blocks, tm):
    t = topi.shape[0]
    nb_pad = -(-n_blocks // LANES) * LANES
    tok = pl.BlockSpec((tm, LANES), lambda i: (i, 0))
    fix = lambda n: pl.BlockSpec((SUBLANES, n), lambda i: (0, 0))
    return pl.pallas_call(
        _route_kernel,
        grid=(t // tm,),
        in_specs=[tok, tok, fix(LANES)],
        out_specs=[tok, fix(nb_pad), fix(LANES)],
        out_shape=[jax.ShapeDtypeStruct((t, LANES), jnp.int32),
                   jax.ShapeDtypeStruct((SUBLANES, nb_pad), jnp.int32),
                   jax.ShapeDtypeStruct((SUBLANES, LANES), jnp.int32)],
        compiler_params=_cparams("arbitrary"),
        name="route_plan",
    )(topi, rank, cnt)


def _dispatch_kernel(pad_ref, dest_ref, h_ref, xb_ref, zero_ref, sem_z, sem_s):
    i = pl.program_id(0)
    tm = h_ref.shape[0]

    @pl.when(i == 0)
    def _():
        zero_ref[...] = jnp.zeros_like(zero_ref)
        sizes = [1 << b for b in range(int(math.log2(MOE_BLOCK)) - 1, -1, -1)]

        def fill(e, carry, do_wait):
            start = pad_ref[0, e]
            n_pad = pad_ref[1, e] - start
            off = start
            for sz in sizes:
                take = (n_pad & sz) != 0
                cp = pltpu.make_async_copy(zero_ref.at[pl.ds(0, sz)], xb_ref.at[pl.ds(off, sz)], sem_z)

                @pl.when(take)
                def _():
                    if do_wait:
                        cp.wait()
                    else:
                        cp.start()
                off = off + jnp.where(take, sz, 0)
            return carry

        half = zero_ref.shape[0]

        def fill_unused(j, carry, do_wait):
            for part in range(MOE_BLOCK // half):
                cp = pltpu.make_async_copy(zero_ref, xb_ref.at[pl.ds(j * MOE_BLOCK + part * half, half)], sem_z)
                if do_wait:
                    cp.wait()
                else:
                    cp.start()
            return carry

        n_blocks = xb_ref.shape[0] // MOE_BLOCK
        lax.fori_loop(0, N_EXPERTS, lambda e, c: fill(e, c, False), 0)
        lax.fori_loop(pad_ref[2, 0], n_blocks, lambda j, c: fill_unused(j, c, False), 0)
        lax.fori_loop(0, N_EXPERTS, lambda e, c: fill(e, c, True), 0)
        lax.fori_loop(pad_ref[2, 0], n_blocks, lambda j, c: fill_unused(j, c, True), 0)

    def issue(r, carry):
        for k in range(TOP_K):
            pltpu.make_async_copy(h_ref.at[r], xb_ref.at[dest_ref[r * TOP_K + k]], sem_s).start(priority=k % 2)
        return carry

    lax.fori_loop(0, tm, issue, 0)
    for k in range(TOP_K):
        pltpu.make_async_copy(h_ref, xb_ref.at[pl.ds(0, tm)], sem_s).wait()


def _dispatch(h2t, dest, meta, n_slots, tm):
    t = h2t.shape[0] // SUBLANES
    h3 = h2t.reshape(t, SUBLANES, LANES)
    return pl.pallas_call(
        _dispatch_kernel,
        grid_spec=pltpu.PrefetchScalarGridSpec(
            num_scalar_prefetch=1,
            grid=(t // tm,),
            in_specs=[pl.BlockSpec((tm * TOP_K,), lambda i, p: (i,), memory_space=pltpu.SMEM),
                      pl.BlockSpec((tm, SUBLANES, LANES), lambda i, p: (i, 0, 0))],
            out_specs=pl.BlockSpec(memory_space=pl.ANY),
            scratch_shapes=[pltpu.VMEM((MOE_BLOCK // 2, SUBLANES, LANES), F32),
                            pltpu.SemaphoreType.DMA, pltpu.SemaphoreType.DMA]),
        out_shape=jax.ShapeDtypeStruct((n_slots, SUBLANES, LANES), F32),
        compiler_params=_cparams("arbitrary"),
        name="moe_dispatch",
    )(meta[:3, :N_EXPERTS], dest, h3)


def _expert_kernel(be_ref, nv_ref, nxt_ref, x_ref, wgu_hbm, bgu_ref, wd_hbm, bd_ref, y_ref,
                   wgu_f32, wd_f32, wgu_bf, wd_bf, sem, *, layer):
    j = pl.program_id(0)
    valid = j < nv_ref[0]
    first = valid & ((j == 0) | (be_ref[j] != be_ref[jnp.maximum(j - 1, 0)]))

    def weight_copies(e):
        return (pltpu.make_async_copy(wgu_hbm.at[layer, e], wgu_f32, sem.at[0]),
                pltpu.make_async_copy(wd_hbm.at[layer, e], wd_f32, sem.at[1]))

    @pl.when(j == 0)
    def _():
        for cp in weight_copies(be_ref[0]):
            cp.start()

    @pl.when(first)
    def _():
        for cp in weight_copies(be_ref[j]):
            cp.wait()
        wgu_bf[...] = wgu_f32[...].astype(BF16)
        wd_bf[...] = wd_f32[...].astype(BF16)

        @pl.when(nxt_ref[j] >= 0)
        def _():
            for cp in weight_copies(nxt_ref[j]):
                cp.start()

    @pl.when(valid)
    def _():
        x = jnp.concatenate([x_ref[pl.ds(c, MOE_BLOCK, stride=SUBLANES), :] for c in range(D_MODEL // LANES)],
                            axis=1).astype(BF16)
        gu = _bdot(x, wgu_bf[...]) + bgu_ref[0]
        x_glu = jnp.minimum(gu[:, :D_EXPERT], SWIGLU_LIMIT)
        x_lin = jnp.clip(gu[:, D_EXPERT:], -SWIGLU_LIMIT, SWIGLU_LIMIT)
        act = x_glu * _sigmoid(SWIGLU_ALPHA * x_glu) * (x_lin + 1.0)
        y = _bdot(act.astype(BF16), wd_bf[...]) + bd_ref[0]
        for c in range(D_MODEL // LANES):
            y_ref[pl.ds(c, MOE_BLOCK, stride=SUBLANES), :] = y[:, c * LANES:(c + 1) * LANES]

    @pl.when(jnp.logical_not(valid))
    def _():
        y_ref[...] = jnp.zeros_like(y_ref)


def _experts(xb, block_expert, n_valid, w_gu, b_gu, w_down, b_down, layer):
    n_slots = xb.shape[0]
    n_blocks = n_slots // MOE_BLOCK
    rows = MOE_BLOCK * SUBLANES
    x2 = xb.reshape(n_slots * SUBLANES, LANES)
    depth, ne, d, de2 = w_gu.shape
    idx = jnp.arange(n_blocks, dtype=jnp.int32)
    is_first = ((idx == 0) | (block_expert != jnp.roll(block_expert, 1))) & (idx < n_valid[0])
    first_at = lax.cummin(jnp.where(is_first, idx, n_blocks)[::-1])[::-1]
    next_first = jnp.concatenate([first_at[1:], jnp.full((1,), n_blocks, jnp.int32)])
    nxt = jnp.where(next_first < n_blocks, block_expert[jnp.minimum(next_first, n_blocks - 1)], -1)
    last = lambda j, be, nv, nx: jnp.minimum(j, nv[0] - 1)
    bmap = lambda j, be, nv, nx: (layer, be[last(j, be, nv, nx)], 0, 0)
    return pl.pallas_call(
        functools.partial(_expert_kernel, layer=layer),
        grid_spec=pltpu.PrefetchScalarGridSpec(
            num_scalar_prefetch=3,
            grid=(n_blocks,),
            in_specs=[pl.BlockSpec((rows, LANES), lambda j, be, nv, nx: (last(j, be, nv, nx), 0)),
                      pl.BlockSpec(memory_space=pl.ANY),
                      pl.BlockSpec((None, 1, 1, de2), bmap),
                      pl.BlockSpec(memory_space=pl.ANY),
                      pl.BlockSpec((None, 1, 1, d), bmap)],
            out_specs=pl.BlockSpec((rows, LANES), lambda j, be, nv, nx: (j, 0)),
            scratch_shapes=[pltpu.VMEM((d, de2), F32), pltpu.VMEM((de2 // 2, d), F32),
                            pltpu.VMEM((d, de2), BF16), pltpu.VMEM((de2 // 2, d), BF16),
                            pltpu.SemaphoreType.DMA((2,))]),
        out_shape=jax.ShapeDtypeStruct((n_slots * SUBLANES, LANES), F32),
        compiler_params=_cparams("arbitrary"),
        name="moe_experts",
    )(block_expert, n_valid, nxt.astype(jnp.int32), x2, w_gu, b_gu.reshape(depth, ne, 1, de2), w_down,
      b_down.reshape(depth, ne, 1, d))


def _combine_kernel(dest_ref, dest_next_ref, yb_ref, gate_ref, x1_ref, g2_ref, gpost_ref, o_ref, buf, sem):
    i = pl.program_id(0)
    tm = x1_ref.shape[0]
    slot = i % 2

    def gather(idx_ref, into):
        def issue(r, carry):
            for k in range(TOP_K):
                src = pl.multiple_of(idx_ref[r * TOP_K + k] * SUBLANES, SUBLANES)
                dst = pl.multiple_of((k * tm + r) * SUBLANES, SUBLANES)
                pltpu.make_async_copy(yb_ref.at[pl.ds(src, SUBLANES), :], buf.at[into, pl.ds(dst, SUBLANES), :],
                                      sem.at[into]).start(priority=k % 2)
            return carry
        lax.fori_loop(0, tm, issue, 0)

    @pl.when(i == 0)
    def _():
        gather(dest_ref, 0)

    @pl.when(i + 1 < pl.num_programs(0))
    def _():
        gather(dest_next_ref, 1 - slot)

    pltpu.make_async_copy(yb_ref.at[pl.ds(0, TOP_K * tm * SUBLANES), :], buf.at[slot], sem.at[slot]).wait()
    gates = gate_ref[...]
    gk = [jnp.broadcast_to(gates[:, k:k + 1], (tm, LANES)) for k in range(TOP_K)]
    b2 = buf.at[slot]
    cols = []
    for c in range(D_MODEL // LANES):
        acc = jnp.zeros((tm, LANES), F32)
        for k in range(TOP_K):
            acc = acc + gk[k] * b2[pl.ds(k * tm * SUBLANES + c, tm, stride=SUBLANES), :]
        cols.append(acc)
    f = jnp.concatenate(cols, axis=1)
    o_ref[...] = x1_ref[...] + g2_ref[0] * (_rms(f) * gpost_ref[...])


def _combine(yb, dest, gates, x1, mod3, g_post, seq, tm):
    t, d = x1.shape
    tps = seq // tm
    n = t // tm
    return pl.pallas_call(
        _combine_kernel,
        grid=(n,),
        in_specs=[pl.BlockSpec((tm * TOP_K,), lambda i: (i,), memory_space=pltpu.SMEM),
                  pl.BlockSpec((tm * TOP_K,), lambda i: (jnp.minimum(i + 1, n - 1),), memory_space=pltpu.SMEM),
                  pl.BlockSpec(memory_space=pl.ANY),
                  pl.BlockSpec((tm, LANES), lambda i: (i, 0)),
                  pl.BlockSpec((tm, d), lambda i: (i, 0)),
                  _mod_spec(5, tps),
                  pl.BlockSpec(g_post.shape, lambda i: (0, 0))],
        out_specs=pl.BlockSpec((tm, d), lambda i: (i, 0)),
        out_shape=jax.ShapeDtypeStruct((t, d), F32),
        scratch_shapes=[pltpu.VMEM((2, TOP_K * tm * SUBLANES, LANES), F32), pltpu.SemaphoreType.DMA((2,))],
        compiler_params=_cparams("arbitrary"),
        name="moe_combine",
    )(dest, dest, yb, gates, x1, mod3, g_post)


def _moe(h2t, topi, gates, rank, cnt, x1, mod3, g_post, w_gu, b_gu, w_down, b_down, layer, seq):
    t = x1.shape[0]
    n_blocks = t * TOP_K // MOE_BLOCK + N_EXPERTS
    dest_l, blk, meta = _route(topi, rank, cnt, n_blocks, min(1024, t))
    dest = dest_l[:, :TOP_K].reshape(t * TOP_K)
    xb = _dispatch(h2t, dest, meta, n_blocks * MOE_BLOCK, MOE_BLOCK)
    yb = _experts(xb, blk[0, :n_blocks], meta[2, :1], w_gu, b_gu, w_down, b_down, layer)
    return _combine(yb, dest, gates, x1, mod3, g_post, seq, MOE_BLOCK)


TOKEN_TILE = 512
FOX_Q_TILE = 256
FOX_K_TILE = 256


def kernel(x, c, ada_w, ada_b, norm_pre_mix, norm_post_mix, norm_pre_ffn, norm_post_ffn, ev_w_in, fox_b_f, s5_lam_re, s5_lam_im, s5_log_dt, s5_b_re, s5_b_im, s5_c_re, s5_c_im, s5_d, s5_glu_w, s5_glu_b, ev_w_out, od_w_in, gla_w_up, gla_b_gate, od_w_out, router_w, router_b, exp_w_gu, exp_b_gu, exp_w_down, exp_b_down):
    bsz, seq, d = x.shape
    t = bsz * seq
    tm = min(TOKEN_TILE, seq)
    x2 = x.reshape(t, d)
    mod = _modulation(c, ada_w, ada_b)
    for l in range(DEPTH):
        i = l // 2
        mod3 = mod[l].reshape(bsz, 1, 6 * d)
        row = lambda a: a[l].reshape(1, -1)
        tail = (x2, mod3, row(norm_post_mix), row(norm_pre_ffn), router_w[l], router_b[l])
        if l % 2 == 0:
            u, q, k, v = _in_even(x2, mod3, row(norm_pre_mix), ev_w_in[i], fox_b_f[i], seq, tm)
            tables = _s5_tables(s5_lam_re[i], s5_lam_im[i], s5_log_dt[i], s5_b_re[i], s5_b_im[i],
                                s5_c_re[i], s5_c_im[i])
            ys = _s5_scan(u, bsz, seq, tables)
            yb = _fox(q, k, v, bsz, seq, min(FOX_Q_TILE, seq), min(FOX_K_TILE, seq))
            outs = _out_even(ys, u, yb, s5_d[i], s5_glu_w[i], s5_glu_b[i], ev_w_out[i], *tail, seq=seq, tm=tm)
        else:
            rq, rk, rv, sg, gq, gk, gv, sr, la = _in_odd(x2, mod3, row(norm_pre_mix), od_w_in[i],
                                                         gla_w_up[i], gla_b_gate[i], seq, tm)
            yc = _retention(rq, rk, rv, sg, bsz, seq)
            yd = _gla(gq, gk, la, gv, sr, bsz, seq)
            outs = _out_odd(yc, yd, od_w_out[i], *tail, seq=seq, tm=tm)
        x1, h2t, topi, gates, rank, cnt = outs
        x2 = _moe(h2t, topi, gates, rank, cnt, x1, mod3, row(norm_post_ffn),
                  exp_w_gu, exp_b_gu, exp_w_down, exp_b_down, l, seq)
    return x2.reshape(bsz, seq, d)
```

```python
import functools
import math

import jax
import jax.numpy as jnp
from jax import lax
from jax.experimental import pallas as pl
from jax.experimental.pallas import tpu as pltpu

F32 = jnp.float32
BF16 = jnp.bfloat16
HIGHEST = lax.Precision.HIGHEST

D_MODEL = 1024
DEPTH = 2
EPS = 1e-6
S5_CH = 512
S5_GROUP = 16
S5_GROUPS = S5_CH // S5_GROUP
S5_STATE = 64
S5_CHUNK = 8
FOX_HEADS = 8
FOX_DH = 64
FOX_W = FOX_HEADS * FOX_DH
FOX_HPS = 8
RET_HEADS = 4
RET_DK = 128
RET_DV = 128
ROPE_BASE = 10000.0
GLA_HEADS = 4
GLA_DK = 64
GLA_DV = 128
GLA_RANK = 16
GLA_TAU = 16.0
GLA_CHUNK = 64
GLA_SUB = 16
N_EXPERTS = 32
TOP_K = 4
D_EXPERT = 1024
SWIGLU_LIMIT = 7.0
SWIGLU_ALPHA = 1.702
MOE_BLOCK = 256

LANES = 128
SUBLANES = 8
VMEM_LIMIT = 56 * 1024 * 1024


def _cparams(*sem):
    return pltpu.CompilerParams(dimension_semantics=sem, vmem_limit_bytes=VMEM_LIMIT)


def _bdot(a, b):
    return jnp.dot(a, b, preferred_element_type=F32)


def _dot_nt(a, b):
    return lax.dot_general(a, b, (((1,), (1,)), ((), ())), preferred_element_type=F32)


def _dot_tn(a, b):
    return lax.dot_general(a, b, (((0,), (0,)), ((), ())), preferred_element_type=F32)


def _split3(x):
    hi = x.astype(BF16)
    r = x - hi.astype(F32)
    mid = r.astype(BF16)
    lo = (r - mid.astype(F32)).astype(BF16)
    return hi, mid, lo


def _dot01(m01, x):
    hi, mid, lo = _split3(x)
    return _bdot(m01, hi) + _bdot(m01, mid) + _bdot(m01, lo)


def _lower_tri(n, strict=False):
    r = lax.broadcasted_iota(jnp.int32, (n, n), 0)
    c = lax.broadcasted_iota(jnp.int32, (n, n), 1)
    return ((r > c) if strict else (r >= c)).astype(BF16)


def _log_sigmoid(x):
    return jnp.minimum(x, 0.0) - jnp.log1p(jnp.exp(-jnp.abs(x)))


def _sigmoid(x):
    return 1.0 / (1.0 + jnp.exp(-x))


def _silu(x):
    return x * _sigmoid(x)


def _rms(x):
    return x * lax.rsqrt(jnp.mean(x * x, axis=-1, keepdims=True) + EPS)


def _mod_kernel(c_ref, w_ref, b_ref, o_ref):
    c = c_ref[...]
    o_ref[0] = jnp.dot(_silu(c), w_ref[0], preferred_element_type=F32, precision=HIGHEST) + b_ref[0]


def _modulation(c, ada_w, ada_b):
    depth, d, n = ada_w.shape
    bsz = c.shape[0]
    tn = D_MODEL
    return pl.pallas_call(
        _mod_kernel,
        grid=(depth, n // tn),
        in_specs=[pl.BlockSpec((bsz, d), lambda l, j: (0, 0)),
                  pl.BlockSpec((1, d, tn), lambda l, j: (l, 0, j)),
                  pl.BlockSpec((1, 1, tn), lambda l, j: (l, 0, j))],
        out_specs=pl.BlockSpec((1, bsz, tn), lambda l, j: (l, 0, j)),
        out_shape=jax.ShapeDtypeStruct((depth, bsz, n), F32),
        compiler_params=_cparams("parallel", "parallel"),
        name="adaln_mod",
    )(c, ada_w, ada_b.reshape(depth, 1, n))


def _mod_spec(chunk, tiles_per_seq):
    return pl.BlockSpec((1, 1, D_MODEL), lambda i: (i // tiles_per_seq, 0, chunk))


def _prenorm(x, g_ref, sc_ref, sh_ref):
    return _rms(x) * (g_ref[...] * (1.0 + sc_ref[0])) + sh_ref[0]


def _in_even_kernel(x_ref, sh_ref, sc_ref, g_ref, w_ref, bf_ref,
                    u_ref, q_ref, k_ref, v_ref, carry_ref, *, tiles_per_seq):
    i = pl.program_id(0)
    tm = x_ref.shape[0]
    h = _prenorm(x_ref[...], g_ref, sc_ref, sh_ref)
    z = _bdot(h.astype(BF16), w_ref[...])
    u_ref[...] = z[:, 0:S5_CH]
    v_ref[...] = z[:, S5_CH + 2 * FOX_W:S5_CH + 3 * FOX_W].astype(BF16)
    ls = _log_sigmoid(z[:, S5_CH + 3 * FOX_W:] + bf_ref[...])

    @pl.when(i % tiles_per_seq == 0)
    def _():
        carry_ref[...] = jnp.zeros_like(carry_ref)

    cum = _dot01(_lower_tri(tm), ls) + carry_ref[...]
    carry_ref[...] = cum[tm - 1:tm, :]

    lane = lax.broadcasted_iota(jnp.int32, (1, LANES), 1)
    feat = lane < FOX_DH
    ones = jnp.where(lane < FOX_DH + 3, 1.0, 0.0)
    for hd in range(FOX_HEADS):
        blk = (hd * FOX_DH) // LANES * LANES
        qs = z[:, S5_CH + blk:S5_CH + blk + LANES] * (FOX_DH ** -0.5)
        ks = z[:, S5_CH + FOX_W + blk:S5_CH + FOX_W + blk + LANES]
        if (hd * FOX_DH) % LANES:
            qs = pltpu.roll(qs, LANES - FOX_DH, 1)
            ks = pltpu.roll(ks, LANES - FOX_DH, 1)
        nf = jnp.broadcast_to(-cum[:, hd:hd + 1], (tm, LANES))
        hi = nf.astype(BF16).astype(F32)
        mid = (nf - hi).astype(BF16).astype(F32)
        lo = nf - hi - mid
        bias = jnp.where(lane == FOX_DH, hi, jnp.where(lane == FOX_DH + 1, mid,
                                                       jnp.where(lane == FOX_DH + 2, lo, 0.0)))
        q_ref[:, hd * LANES:(hd + 1) * LANES] = jnp.where(feat, qs, ones).astype(BF16)
        k_ref[:, hd * LANES:(hd + 1) * LANES] = jnp.where(feat, ks, bias).astype(BF16)


def _in_even(x2, mod3, gain, w_in, b_f, seq, tm):
    t, d = x2.shape
    tiles_per_seq = seq // tm
    nw = S5_CH + 3 * FOX_W
    w = jnp.concatenate([w_in[:, :nw], jnp.pad(w_in[:, nw:], ((0, 0), (0, LANES - FOX_HEADS)))],
                        axis=1).astype(BF16)
    bf = jnp.pad(b_f, (0, LANES - FOX_HEADS)).reshape(1, LANES)
    tok = lambda n: pl.BlockSpec((tm, n), lambda i: (i, 0))
    full = lambda a: pl.BlockSpec(a.shape, lambda i: (0,) * a.ndim)
    return pl.pallas_call(
        functools.partial(_in_even_kernel, tiles_per_seq=tiles_per_seq),
        grid=(t // tm,),
        in_specs=[tok(d), _mod_spec(0, tiles_per_seq), _mod_spec(1, tiles_per_seq),
                  full(gain), full(w), full(bf)],
        out_specs=[tok(S5_CH), tok(FOX_HEADS * LANES), tok(FOX_HEADS * LANES), tok(FOX_W)],
        out_shape=[jax.ShapeDtypeStruct((t, S5_CH), F32),
                   jax.ShapeDtypeStruct((t, FOX_HEADS * LANES), BF16),
                   jax.ShapeDtypeStruct((t, FOX_HEADS * LANES), BF16),
                   jax.ShapeDtypeStruct((t, FOX_W), BF16)],
        scratch_shapes=[pltpu.VMEM((1, LANES), F32)],
        compiler_params=_cparams("arbitrary"),
        name="in_proj_even",
    )(x2, mod3, mod3, gain, w, bf)


S5_TILE_GROUPS = LANES // S5_GROUP
S5_SEQ_PARTS = 4


def _s5_kernel(u_ref, wt_ref, ws_ref, wc_ref, a_ref, y_ref, x_scr, e_scr, hp_scr, h_scr, *, nb, ncl):
    ell = S5_CHUNK
    sw = S5_TILE_GROUPS * 2 * S5_STATE

    @pl.when(pl.program_id(1) == 0)
    def _():
        h_scr[...] = jnp.zeros_like(h_scr)

    for b in range(nb):
        for t in range(ell):
            x_scr[b * ncl:(b + 1) * ncl, t * LANES:(t + 1) * LANES] = (
                u_ref[b, pl.ds(t, ncl, stride=ell), :].astype(BF16))
    x = x_scr[...]
    e = _bdot(x, ws_ref[0])
    tg = S5_TILE_GROUPS
    for j in range(2 * tg):
        e_scr[j] = e[:, j * LANES:(j + 1) * LANES]
    a1 = jnp.broadcast_to(a_ref[0, 0:1, :], (nb, sw))
    a2 = jnp.broadcast_to(a_ref[0, 1:2, :], (nb, sw))
    a2s = jnp.broadcast_to(a_ref[0, 2:3, :], (nb, sw))

    def body(c, carry):
        h, hs = carry
        rows_c = pl.ds(c, nb, stride=ncl)
        for j in range(tg):
            hp_scr[j, rows_c, :] = h[:, j * LANES:(j + 1) * LANES]
        e1 = jnp.concatenate([e_scr[j, rows_c, :] for j in range(tg)], axis=1)
        e2 = jnp.concatenate([e_scr[tg + j, rows_c, :] for j in range(tg)], axis=1)
        return a1 * h + a2 * hs + e1, a1 * hs + a2s * h + e2

    h, hs = lax.fori_loop(0, ncl, body, (h_scr[0], h_scr[1]))
    h_scr[0] = h
    h_scr[1] = hs
    hp = jnp.concatenate([hp_scr[j] for j in range(tg)], axis=1).astype(BF16)
    y = _bdot(x, wt_ref[0]) + _bdot(hp, wc_ref[0])
    for b in range(nb):
        for t in range(ell):
            y_ref[b, pl.ds(t, ncl, stride=ell), :] = y[b * ncl:(b + 1) * ncl, t * LANES:(t + 1) * LANES]


def _s5_tables(lam_re, lam_im, log_dt, b_re, b_im, c_re, c_im):
    ell, p, g = S5_CHUNK, S5_STATE, S5_GROUPS
    lr, li = lam_re.astype(F32), lam_im.astype(F32)
    dt = jnp.exp(log_dt.astype(F32))[:, None]
    mag = jnp.exp(lr * dt)
    a_re, a_im = mag * jnp.cos(li * dt), mag * jnp.sin(li * dt)
    den = lr * lr + li * li
    n_re, n_im = a_re - 1.0, a_im
    z_re = (n_re * lr + n_im * li) / den
    z_im = (n_im * lr - n_re * li) / den
    br, bi = b_re.astype(F32), b_im.astype(F32)
    bb_re = z_re[..., None] * br - z_im[..., None] * bi
    bb_im = z_re[..., None] * bi + z_im[..., None] * br
    j = jnp.arange(ell + 1, dtype=F32)[:, None, None]
    pmag = jnp.exp(j * (lr * dt)[None])
    pr, pi = pmag * jnp.cos(j * (li * dt)[None]), pmag * jnp.sin(j * (li * dt)[None])
    w_re = pr[..., None] * bb_re[None] - pi[..., None] * bb_im[None]
    w_im = pr[..., None] * bb_im[None] + pi[..., None] * bb_re[None]
    cr, ci = c_re.astype(F32), c_im.astype(F32)
    kern = (jnp.einsum('gcp,jgpd->jgcd', cr, w_re[:ell], precision=HIGHEST)
            - jnp.einsum('gcp,jgpd->jgcd', ci, w_im[:ell], precision=HIGHEST))
    s_idx = jnp.arange(ell)[:, None]
    t_idx = jnp.arange(ell)[None, :]
    lag = jnp.clip(t_idx - s_idx, 0, ell - 1)
    toep = kern[lag]
    toep = jnp.where((t_idx >= s_idx)[:, :, None, None, None], toep, 0.0)
    toep = toep.transpose(2, 0, 4, 1, 3)
    rev = jnp.arange(ell - 1, -1, -1)
    st_re = w_re[rev].transpose(1, 0, 3, 2)
    st_im = w_im[rev].transpose(1, 0, 3, 2)
    wst = jnp.stack([jnp.concatenate([st_re, st_im], -1), jnp.concatenate([st_im, st_re], -1)], axis=3)
    p1r, p1i = pr[1:], pi[1:]
    c_hr = cr[None] * p1r[:, :, None, :] - ci[None] * p1i[:, :, None, :]
    c_hi = -cr[None] * p1i[:, :, None, :] - ci[None] * p1r[:, :, None, :]
    cst = jnp.concatenate([c_hr, c_hi], axis=-1).transpose(1, 3, 0, 2)
    al_r, al_i = pr[ell], pi[ell]
    a_rows = jnp.stack([jnp.concatenate([al_r, al_r], -1), jnp.concatenate([-al_i, al_i], -1),
                        jnp.concatenate([al_i, -al_i], -1)], axis=1)

    tg = S5_TILE_GROUPS
    nt = g // tg
    eye = jnp.eye(tg, dtype=F32)
    tile = lambda a: a.reshape((nt, tg) + a.shape[1:])
    wt = jnp.einsum('kgscth,gj->ksgctjh', tile(toep), eye).reshape(nt, ell * LANES, ell * LANES)
    ws = jnp.einsum('kgscxp,gj->ksgcxjp', tile(wst), eye).reshape(nt, ell * LANES, 2 * tg * 2 * p)
    wc = jnp.einsum('kgpth,gj->kgptjh', tile(cst), eye).reshape(nt, tg * 2 * p, ell * LANES)
    a_t = tile(a_rows).transpose(0, 2, 1, 3).reshape(nt, 3, tg * 2 * p)
    a_t = jnp.pad(a_t, ((0, 0), (0, SUBLANES - 3), (0, 0)))
    return wt.astype(BF16), ws.astype(BF16), wc.astype(BF16), a_t


def _s5_scan(u, bsz, seq, tables):
    wt, ws, wc, a_t = tables
    ell = S5_CHUNK
    nt = S5_GROUPS // S5_TILE_GROUPS
    parts = S5_SEQ_PARTS if seq % (S5_SEQ_PARTS * ell * 2 * SUBLANES) == 0 else 1
    ncl = seq // parts // ell
    rows = bsz * ncl
    sw = S5_TILE_GROUPS * 2 * S5_STATE
    u3 = u.reshape(bsz, seq, S5_CH)
    io = pl.BlockSpec((bsz, seq // parts, LANES), lambda k, s: (0, s, k))
    per_tile = lambda a: pl.BlockSpec((1,) + a.shape[1:], lambda k, s: (k, 0, 0))
    y = pl.pallas_call(
        functools.partial(_s5_kernel, nb=bsz, ncl=ncl),
        grid=(nt, parts),
        in_specs=[io, per_tile(wt), per_tile(ws), per_tile(wc), per_tile(a_t)],
        out_specs=io,
        out_shape=jax.ShapeDtypeStruct((bsz, seq, S5_CH), F32),
        scratch_shapes=[pltpu.VMEM((rows, ell * LANES), BF16),
                        pltpu.VMEM((2 * S5_TILE_GROUPS, rows, LANES), F32),
                        pltpu.VMEM((S5_TILE_GROUPS, rows, LANES), F32),
                        pltpu.VMEM((2, bsz, sw), F32)],
        compiler_params=_cparams("parallel", "arbitrary"),
        name="s5_scan",
    )(u3, wt, ws, wc, a_t)
    return y.reshape(bsz * seq, S5_CH)


def _fox_kernel(q_ref, k_ref, vt_ref, o_ref, *, tq, tk):
    i = pl.program_id(2)
    nh = FOX_HPS
    q = [q_ref[:, h * LANES:(h + 1) * LANES] for h in range(nh)]
    key = lax.broadcasted_iota(jnp.int32, (tk, tq), 0)
    qry = lax.broadcasted_iota(jnp.int32, (tk, tq), 1)
    per_q = tq // tk

    def block(j0, carry, mask):
        kj = k_ref[pl.ds(j0, tk), :]
        vtj = vt_ref[:, pl.ds(j0, tk)]
        ss = [_dot_nt(kj[:, h * LANES:(h + 1) * LANES], q[h]) for h in range(nh)]
        stats = []
        for h in range(nh):
            m, l, acc = carry[h]
            s = ss[h] if mask is None else jnp.where(mask, ss[h], -jnp.inf)
            m_new = jnp.maximum(m, jnp.max(s, axis=0, keepdims=True))
            p = jnp.exp(s - m_new)
            alpha = jnp.exp(m - m_new)
            stats.append((m_new, alpha * l + jnp.sum(p, axis=0, keepdims=True), alpha, p.astype(BF16)))
        out = []
        for h in range(nh):
            m_new, l, alpha, p = stats[h]
            acc = alpha * carry[h][2] + _bdot(vtj[h * FOX_DH:(h + 1) * FOX_DH, :], p)
            out.append((m_new, l, acc))
        return tuple(out)

    init = tuple((jnp.full((1, tq), -jnp.inf, F32), jnp.zeros((1, tq), F32), jnp.zeros((FOX_DH, tq), F32))
                 for _ in range(nh))
    carry = lax.fori_loop(0, i * per_q, lambda j, c: block(pl.multiple_of(j * tk, tk), c, None), init)
    for d in range(per_q):
        carry = block(pl.multiple_of(i * tq + d * tk, tk), carry, key + d * tk <= qry)
    for g in range(nh // 2):
        o_t = jnp.concatenate([carry[h][2] / carry[h][1] for h in (2 * g, 2 * g + 1)], axis=0)
        o_ref[:, g * LANES:(g + 1) * LANES] = o_t.T.astype(o_ref.dtype)


def _fox(q_aug, k_aug, v, bsz, seq, tq, tk):
    t = v.shape[0]
    nh = FOX_HPS
    groups = FOX_HEADS // nh
    nq = seq // tq
    v_t = v.reshape(bsz, seq, FOX_W).transpose(0, 2, 1).reshape(bsz * FOX_W, seq)
    return pl.pallas_call(
        functools.partial(_fox_kernel, tq=tq, tk=tk),
        grid=(bsz, groups, nq),
        in_specs=[pl.BlockSpec((tq, nh * LANES), lambda b, p, i: (b * nq + i, p)),
                  pl.BlockSpec((seq, nh * LANES), lambda b, p, i: (b, p)),
                  pl.BlockSpec((nh * FOX_DH, seq), lambda b, p, i: (b * groups + p, 0))],
        out_specs=pl.BlockSpec((tq, nh * FOX_DH), lambda b, p, i: (b * nq + i, p)),
        out_shape=jax.ShapeDtypeStruct((t, FOX_W), BF16),
        compiler_params=_cparams("parallel", "parallel", "arbitrary"),
        name="fox_attention",
    )(q_aug, k_aug, v_t)


_ODD_COLS = (("rq", 512), ("rk", 512), ("rv", 512), ("rg", 512), ("gq", 256), ("gk", 256),
             ("gv", 512), ("gr", 512), ("glr", LANES))


def _odd_offsets():
    off, out = 0, {}
    for name, w in _ODD_COLS:
        out[name] = (off, off + w)
        off += w
    return out, off


def _in_odd_kernel(x_ref, sh_ref, sc_ref, g_ref, w_ref, cos_ref, sin_ref, wup_ref, bg_ref,
                   rq_ref, rk_ref, rv_ref, sg_ref, gq_ref, gk_ref, gv_ref, sr_ref, la_ref):
    h = _prenorm(x_ref[...], g_ref, sc_ref, sh_ref)
    z = _bdot(h.astype(BF16), w_ref[...])
    off, _ = _odd_offsets()
    col = lambda n: z[:, off[n][0]:off[n][1]]
    cos, sin = cos_ref[...], sin_ref[...]

    def rope(t, scale):
        heads = []
        for hd in range(RET_HEADS):
            th = t[:, hd * RET_DK:(hd + 1) * RET_DK]
            heads.append((th * cos + pltpu.roll(th, RET_DK // 2, 1) * sin) * scale)
        return jnp.concatenate(heads, axis=1).astype(BF16)

    rq_ref[...] = rope(col("rq"), 1.0)
    rk_ref[...] = rope(col("rk"), RET_DK ** -0.5)
    rv_ref[...] = col("rv").astype(BF16)
    sg_ref[...] = _silu(col("rg"))
    gq_ref[...] = col("gq") * (GLA_DK ** -0.5)
    gk_ref[...] = col("gk")
    gv_ref[...] = col("gv").astype(BF16)
    sr_ref[...] = _silu(col("gr"))
    gate = jnp.dot(col("glr"), wup_ref[...], preferred_element_type=F32, precision=HIGHEST) + bg_ref[...]
    la_ref[...] = _log_sigmoid(gate) * (1.0 / GLA_TAU)


def _in_odd(x2, mod3, gain, w_in, w_up, b_gate, seq, tm):
    t, d = x2.shape
    tps = seq // tm
    ref_w = (512, 512, 512, 512, 256, 256, 512, GLA_RANK, 512)
    starts = [0]
    for wd in ref_w:
        starts.append(starts[-1] + wd)
    seg = lambda j: w_in[:, starts[j]:starts[j + 1]]
    w = jnp.concatenate([seg(0), seg(1), seg(2), seg(3), seg(4), seg(5), seg(6), seg(8),
                         jnp.pad(seg(7), ((0, 0), (0, LANES - GLA_RANK)))], axis=1).astype(BF16)
    wup = jnp.pad(w_up.astype(F32), ((0, LANES - GLA_RANK), (0, 0)))
    bg = b_gate.reshape(1, -1).astype(F32)
    half = RET_DK // 2
    inv = ROPE_BASE ** (-jnp.arange(half, dtype=F32) / half)
    ang = jnp.arange(seq, dtype=F32)[:, None] * inv[None, :]
    cos = jnp.concatenate([jnp.cos(ang), jnp.cos(ang)], axis=1)
    sin = jnp.concatenate([-jnp.sin(ang), jnp.sin(ang)], axis=1)
    tok = lambda n: pl.BlockSpec((tm, n), lambda i: (i, 0))
    full = lambda a: pl.BlockSpec(a.shape, lambda i: (0,) * a.ndim)
    pos = pl.BlockSpec((tm, RET_DK), lambda i: (i % tps, 0))
    widths = (512, 512, 512, 512, 256, 256, 512, 512, 256)
    dtypes = (BF16, BF16, BF16, F32, F32, F32, BF16, F32, F32)
    return pl.pallas_call(
        _in_odd_kernel,
        grid=(t // tm,),
        in_specs=[tok(d), _mod_spec(0, tps), _mod_spec(1, tps), full(gain), full(w), pos, pos,
                  full(wup), full(bg)],
        out_specs=[tok(n) for n in widths],
        out_shape=[jax.ShapeDtypeStruct((t, n), dt) for n, dt in zip(widths, dtypes)],
        compiler_params=_cparams("parallel"),
        name="in_proj_odd",
    )(x2, mod3, mod3, gain, w, cos, sin, wup, bg)


RET_CHUNK = 256


def _ret_kernel(q_ref, k_ref, v_ref, sg_ref, dm_ref, xi_ref, zeta_ref, gl_ref, y_ref, st_ref):
    @pl.when(pl.program_id(1) == 0)
    def _():
        st_ref[...] = jnp.zeros_like(st_ref)

    heads = range(RET_HEADS)
    col = lambda h: slice(h * RET_DK, (h + 1) * RET_DK)
    q = [q_ref[:, col(h)] for h in heads]
    k = [k_ref[:, col(h)] for h in heads]
    v = [v_ref[:, col(h)] for h in heads]
    st = [st_ref[h] for h in heads]
    s = [_dot_nt(q[h], k[h]) for h in heads]
    inter = [_bdot((q[h].astype(F32) * xi_ref[h]).astype(BF16), st[h].astype(BF16)) for h in heads]
    upd = [_dot_tn((k[h].astype(F32) * zeta_ref[h]).astype(BF16), v[h]) for h in heads]
    for h in heads:
        o = _bdot((s[h] * dm_ref[h]).astype(BF16), v[h]) + inter[h]
        st_ref[h] = gl_ref[h, 0:1, :] * st[h] + upd[h]
        y_ref[:, col(h)] = (sg_ref[:, col(h)] * _rms(o)).astype(y_ref.dtype)


def _retention(rq, rk, rv, sg, bsz, seq):
    t = rq.shape[0]
    ell = min(RET_CHUNK, seq)
    nc = seq // ell
    log_g = jnp.log(1.0 - jnp.exp2(-5.0 - jnp.arange(RET_HEADS, dtype=F32)))
    idx = jnp.arange(ell, dtype=F32)
    rel = idx[:, None] - idx[None, :]
    dmat = jnp.where(rel >= 0, jnp.exp(log_g[:, None, None] * jnp.maximum(rel, 0.0)), 0.0)
    lanes = lambda a: jnp.broadcast_to(a[..., None], a.shape + (RET_DK,))
    xi = lanes(jnp.exp(log_g[:, None] * (idx + 1.0)))
    zeta = lanes(jnp.exp(log_g[:, None] * (ell - 1.0 - idx)))
    gl = jnp.broadcast_to(jnp.exp(log_g * ell)[:, None, None], (RET_HEADS, SUBLANES, RET_DV))
    blk = pl.BlockSpec((ell, RET_HEADS * RET_DK), lambda b, c: (b * nc + c, 0))
    full = lambda a: pl.BlockSpec(a.shape, lambda b, c: (0, 0, 0))
    return pl.pallas_call(
        _ret_kernel,
        grid=(bsz, nc),
        in_specs=[blk, blk, blk, blk, full(dmat), full(xi), full(zeta), full(gl)],
        out_specs=blk,
        out_shape=jax.ShapeDtypeStruct((t, RET_HEADS * RET_DV), BF16),
        scratch_shapes=[pltpu.VMEM((RET_HEADS, RET_DK, RET_DV), F32)],
        compiler_params=_cparams("parallel", "arbitrary"),
        name="retention",
    )(rq, rk, rv, sg, dmat, xi, zeta, gl)


def _gla_kernel(q_ref, k_ref, la_ref, v_ref, sg_ref, y_ref, st_ref, b_scr, v_scr, p_scr, r_scr):
    @pl.when(pl.program_id(1) == 0)
    def _():
        st_ref[...] = jnp.zeros_like(st_ref)

    ell, sub = GLA_CHUNK, GLA_SUB
    n_sub = ell // sub
    nb = q_ref.shape[0]
    pairs = GLA_HEADS // 2
    streams = [(bl, p) for bl in range(nb) for p in range(pairs)]
    lane = lax.broadcasted_iota(jnp.int32, (1, LANES), 1)
    first = lane < GLA_DK
    head = (first, jnp.logical_not(first))
    pick = lambda h, a: jnp.where(head[h], a, 0.0).astype(BF16)
    tri = _lower_tri(ell)
    tau = lax.broadcasted_iota(jnp.int32, (sub, LANES), 0)
    rsub = lax.broadcasted_iota(jnp.int32, (LANES, 2 * LANES), 0)
    csub = lax.broadcasted_iota(jnp.int32, (LANES, 2 * LANES), 1)
    ind = ((rsub < GLA_DK) == (csub < LANES)).astype(BF16)

    val = {}
    for sid, (bl, p) in enumerate(streams):
        qk = slice(p * LANES, (p + 1) * LANES)
        q, k = q_ref[bl, :, qk], k_ref[bl, :, qk]
        b = _dot01(tri, la_ref[bl, :, qk])
        b_scr[sid] = b
        v_bf = v_ref[bl, :, p * 2 * GLA_DV:(p + 1) * 2 * GLA_DV]
        v_scr[sid] = v_bf.astype(F32)
        st = st_ref[sid]
        val[sid] = dict(q=q, k=k, b=b, st=st, vh=[v_bf[:, h * GLA_DV:(h + 1) * GLA_DV] for h in range(2)])

    for sid in val:
        d = val[sid]
        qe = d["q"] * jnp.exp(d["b"])
        st_bf = d["st"].astype(BF16)
        d["o"] = [_dot_nt(pick(h, qe), st_bf) for h in range(2)]

    row = lax.broadcasted_iota(jnp.int32, (ell, LANES), 0)
    for sid in val:
        d = val[sid]
        q, k, b = d["q"], d["k"], d["b"]
        qa, ka = [], []
        for i in range(1, n_sub):
            lo = i * sub
            ref_row = b[lo - 1:lo, :]
            in_i = (row >= lo) & (row < lo + sub)
            qa.append(jnp.where(in_i, q * jnp.exp(jnp.minimum(b - ref_row, 0.0)), 0.0))
            ka.append(jnp.where(row < lo, k * jnp.exp(jnp.minimum(ref_row - b, 0.0)), 0.0))
        k_cat = jnp.concatenate(ka, axis=1).astype(BF16)
        d["a_off"] = [_dot_nt(jnp.concatenate([pick(h, x) for x in qa], axis=1), k_cat) for h in range(2)]
    for sid in val:
        d = val[sid]
        d["off"] = [_bdot(d["a_off"][h].astype(BF16), d["vh"][h]) for h in range(2)]

    for sid, (bl, p) in enumerate(streams):
        d = val[sid]
        q, b = d["q"], d["b"]
        for i in range(n_sub):
            lo = i * sub
            qi, bi = q[lo:lo + sub], b[lo:lo + sub]
            for s in range(sub):
                k_row = k_ref[bl, pl.ds(lo + s, 1), p * LANES:(p + 1) * LANES]
                b_row = b_scr[sid, pl.ds(lo + s, 1), :]
                w = jnp.exp(jnp.minimum(bi - b_row, 0.0))
                p_scr[sid, pl.ds((lo + s) * sub, sub), :] = jnp.where(tau >= s, qi * k_row * w, 0.0).astype(BF16)
    for sid in val:
        r_scr[sid] = _bdot(p_scr[sid], ind)
    for sid in val:
        diag = [[], []]
        for i in range(n_sub):
            lo = i * sub
            for h in range(2):
                acc = jnp.zeros((sub, GLA_DV), F32)
                for s in range(sub):
                    v_row = v_scr[sid, pl.ds(lo + s, 1), h * GLA_DV:(h + 1) * GLA_DV]
                    acc = acc + r_scr[sid, pl.ds((lo + s) * sub, sub), h * LANES:(h + 1) * LANES] * v_row
                diag[h].append(acc)
        val[sid]["diag"] = diag

    for sid, (bl, p) in enumerate(streams):
        d = val[sid]
        b_last = d["b"][ell - 1:ell, :]
        kh = (d["k"] * jnp.exp(b_last - d["b"])).astype(BF16)
        upd = [_dot_tn(d["vh"][h], kh) for h in range(2)]
        st_ref[sid] = d["st"] * jnp.exp(b_last) + jnp.where(first, upd[0], upd[1])
        for h in range(2):
            oh = d["o"][h] + d["off"][h] + jnp.concatenate(d["diag"][h], axis=0)
            cols = slice((2 * p + h) * GLA_DV, (2 * p + h + 1) * GLA_DV)
            y_ref[bl, :, cols] = (sg_ref[bl, :, cols] * _rms(oh)).astype(y_ref.dtype)


GLA_BATCHES = 2


def _gla(gq, gk, la, gv, sr, bsz, seq):
    t = gq.shape[0]
    ell = GLA_CHUNK
    nc = seq // ell
    nb = min(GLA_BATCHES, bsz)
    ns = nb * (GLA_HEADS // 2)
    r3 = lambda a: a.reshape(bsz, seq, a.shape[1])
    spec = lambda w: pl.BlockSpec((nb, ell, w), lambda b, c: (b, c, 0))
    wq, wv = GLA_HEADS * GLA_DK, GLA_HEADS * GLA_DV
    y = pl.pallas_call(
        _gla_kernel,
        grid=(bsz // nb, nc),
        in_specs=[spec(wq), spec(wq), spec(wq), spec(wv), spec(wv)],
        out_specs=spec(wv),
        out_shape=jax.ShapeDtypeStruct((bsz, seq, wv), BF16),
        scratch_shapes=[pltpu.VMEM((ns, GLA_DV, LANES), F32),
                        pltpu.VMEM((ns, ell, LANES), F32),
                        pltpu.VMEM((ns, ell, 2 * GLA_DV), F32),
                        pltpu.VMEM((ns, ell * GLA_SUB, LANES), BF16),
                        pltpu.VMEM((ns, ell * GLA_SUB, 2 * LANES), F32)],
        compiler_params=_cparams("parallel", "arbitrary"),
        name="gla",
    )(r3(gq), r3(gk), r3(la), r3(gv), r3(sr))
    return y.reshape(t, wv)


def _post_tail(m, x_ref, g1_ref, gpost_ref, sh2_ref, sc2_ref, gpre_ref, rw_ref, rb_ref,
               x1_ref, h2_ref, topi_ref, gate_ref, rank_ref, cnt_ref, carry_ref):
    i = pl.program_id(0)
    tm = m.shape[0]
    x1 = x_ref[...] + g1_ref[0] * (_rms(m) * gpost_ref[...])
    x1_ref[...] = x1
    h2 = _rms(x1) * (gpre_ref[...] * (1.0 + sc2_ref[0])) + sh2_ref[0]
    for c in range(D_MODEL // LANES):
        h2_ref[pl.ds(c, tm, stride=SUBLANES), :] = h2[:, c * LANES:(c + 1) * LANES]
    h_hi = h2.astype(BF16)
    h_lo = (h2 - h_hi.astype(F32)).astype(BF16)
    w_hi, w_lo = rw_ref[0], rw_ref[1]
    logits = _bdot(h_hi, w_hi) + _bdot(h_lo, w_hi) + _bdot(h_hi, w_lo) + rb_ref[...]

    lane = lax.broadcasted_iota(jnp.int32, (tm, LANES), 1)
    lanef = lane.astype(F32)
    work = logits
    topv = jnp.full((tm, LANES), -jnp.inf, F32)
    topi = jnp.zeros((tm, LANES), F32)
    onehot = jnp.zeros((tm, LANES), F32)
    hits = []
    for k in range(TOP_K):
        mx = jnp.max(work, axis=-1, keepdims=True)
        idx = jnp.min(jnp.where(work == mx, lanef, float(LANES)), axis=-1, keepdims=True)
        hit = lanef == idx
        hits.append(hit)
        topv = jnp.where(lane == k, mx, topv)
        topi = jnp.where(lane == k, idx, topi)
        onehot = onehot + hit.astype(F32)
        work = jnp.where(hit, -jnp.inf, work)
    e = jnp.exp(topv - jnp.max(topv, axis=-1, keepdims=True))
    gate_ref[...] = e / jnp.sum(e, axis=-1, keepdims=True)
    topi_ref[...] = topi.astype(jnp.int32)

    @pl.when(i == 0)
    def _():
        carry_ref[...] = jnp.zeros_like(carry_ref)

    before = _bdot(_lower_tri(tm, strict=True), onehot.astype(BF16)) + carry_ref[...]
    rank = jnp.zeros((tm, LANES), F32)
    for k in range(TOP_K):
        rk = jnp.sum(jnp.where(hits[k], before, 0.0), axis=-1, keepdims=True)
        rank = jnp.where(lane == k, rk, rank)
    rank_ref[...] = rank.astype(jnp.int32)
    total = before[tm - 1:tm, :] + onehot[tm - 1:tm, :]
    carry_ref[...] = total
    cnt_ref[...] = jnp.broadcast_to(total, cnt_ref.shape)


def _out_even_kernel(ys_ref, u_ref, yb_ref, d_ref, gw_ref, gb_ref, wa_ref, wb_ref, *rest):
    y = ys_ref[...] + d_ref[...] * u_ref[...]
    g = jax.nn.gelu(y)
    ya = g * _sigmoid(_bdot(g.astype(BF16), gw_ref[...]) + gb_ref[...])
    m = _bdot(ya.astype(BF16), wa_ref[...]) + _bdot(yb_ref[...], wb_ref[...])
    _post_tail(m, *rest)


def _out_odd_kernel(yc_ref, yd_ref, wa_ref, wb_ref, *rest):
    m = _bdot(yc_ref[...], wa_ref[...]) + _bdot(yd_ref[...], wb_ref[...])
    _post_tail(m, *rest)


def _mixer_out(body, mix_args, mix_specs, x2, mod3, g_post, g_pre, router_w, router_b, seq, tm):
    t, d = x2.shape
    tps = seq // tm
    rw32 = jnp.pad(router_w.astype(F32), ((0, 0), (0, LANES - N_EXPERTS)))
    rw_hi = rw32.astype(BF16)
    rw = jnp.stack([rw_hi, (rw32 - rw_hi.astype(F32)).astype(BF16)])
    rb = jnp.pad(router_b.astype(F32), (0, LANES - N_EXPERTS), constant_values=-1e30).reshape(1, LANES)
    tok = lambda n: pl.BlockSpec((tm, n), lambda i: (i, 0))
    full = lambda a: pl.BlockSpec(a.shape, lambda i: (0,) * a.ndim)
    tail_args = [x2, mod3, g_post, mod3, mod3, g_pre, rw, rb]
    tail_specs = [tok(d), _mod_spec(2, tps), full(g_post), _mod_spec(3, tps), _mod_spec(4, tps),
                  full(g_pre), full(rw), full(rb)]
    return pl.pallas_call(
        body,
        grid=(t // tm,),
        in_specs=mix_specs + tail_specs,
        out_specs=[tok(d), pl.BlockSpec((tm * SUBLANES, LANES), lambda i: (i, 0)),
                   tok(LANES), tok(LANES), tok(LANES), pl.BlockSpec((SUBLANES, LANES), lambda i: (0, 0))],
        out_shape=[jax.ShapeDtypeStruct((t, d), F32),
                   jax.ShapeDtypeStruct((t * SUBLANES, LANES), F32),
                   jax.ShapeDtypeStruct((t, LANES), jnp.int32),
                   jax.ShapeDtypeStruct((t, LANES), F32),
                   jax.ShapeDtypeStruct((t, LANES), jnp.int32),
                   jax.ShapeDtypeStruct((SUBLANES, LANES), F32)],
        scratch_shapes=[pltpu.VMEM((1, LANES), F32)],
        compiler_params=_cparams("arbitrary"),
        name="mixer_out_router",
    )(*mix_args, *tail_args)


def _out_even(ys, u, yb, d_skip, glu_w, glu_b, w_out, *tail, seq, tm):
    tok = lambda n: pl.BlockSpec((tm, n), lambda i: (i, 0))
    full = lambda a: pl.BlockSpec(a.shape, lambda i: (0,) * a.ndim)
    args = [ys, u, yb, d_skip.reshape(1, -1), glu_w.astype(BF16), glu_b.reshape(1, -1),
            w_out[:S5_CH].astype(BF16), w_out[S5_CH:].astype(BF16)]
    specs = [tok(S5_CH), tok(S5_CH), tok(FOX_W)] + [full(a) for a in args[3:]]
    return _mixer_out(_out_even_kernel, args, specs, *tail, seq, tm)


def _out_odd(yc, yd, w_out, *tail, seq, tm):
    tok = lambda n: pl.BlockSpec((tm, n), lambda i: (i, 0))
    full = lambda a: pl.BlockSpec(a.shape, lambda i: (0,) * a.ndim)
    nc = yc.shape[1]
    args = [yc, yd, w_out[:nc].astype(BF16), w_out[nc:].astype(BF16)]
    specs = [tok(nc), tok(yd.shape[1])] + [full(a) for a in args[2:]]
    return _mixer_out(_out_odd_kernel, args, specs, *tail, seq, tm)


def _route_kernel(topi_ref, rank_ref, cnt_ref, dest_ref, blk_ref, meta_ref):
    tm = topi_ref.shape[0]
    cnt = cnt_ref[...]
    padded = jnp.floor((cnt + (MOE_BLOCK - 1.0)) * (1.0 / MOE_BLOCK)) * MOE_BLOCK
    r = lax.broadcasted_iota(jnp.int32, (LANES, LANES), 0)
    c = lax.broadcasted_iota(jnp.int32, (LANES, LANES), 1)
    hi, mid, lo = _split3(padded)
    incl = (r <= c).astype(BF16)
    pad_end = _bdot(hi, incl) + _bdot(mid, incl) + _bdot(lo, incl)
    pad_start = pad_end - padded
    lane = lax.broadcasted_iota(jnp.int32, (tm, LANES), 1)
    lanef = lane.astype(F32)
    topi = topi_ref[...].astype(F32)
    start_row = pad_start[0:1, :]
    dest = jnp.zeros((tm, LANES), F32)
    for k in range(TOP_K):
        idx = jnp.sum(jnp.where(lane == k, topi, 0.0), axis=-1, keepdims=True)
        st = jnp.sum(jnp.where(lanef == idx, start_row, 0.0), axis=-1, keepdims=True)
        dest = jnp.where(lane == k, st, dest)
    dest_ref[...] = dest.astype(jnp.int32) + rank_ref[...]

    nb = blk_ref.shape[1]
    end_col = jnp.sum(jnp.where(r == c, jnp.broadcast_to(pad_end[0:1, :], (LANES, LANES)), 0.0),
                      axis=-1, keepdims=True)
    jpos = lax.broadcasted_iota(jnp.int32, (LANES, nb), 1).astype(F32) * MOE_BLOCK
    esub = lax.broadcasted_iota(jnp.int32, (LANES, nb), 0)
    done = jnp.where((end_col <= jpos) & (esub < N_EXPERTS), 1.0, 0.0)
    be = jnp.minimum(jnp.sum(done, axis=0, keepdims=True), N_EXPERTS - 1.0)
    blk_ref[...] = jnp.broadcast_to(be, blk_ref.shape).astype(jnp.int32)
    lane1 = lax.broadcasted_iota(jnp.int32, (SUBLANES, LANES), 1)
    n_valid = jnp.sum(jnp.where(lane1 == N_EXPERTS - 1, pad_end, 0.0), axis=-1, keepdims=True) * (1.0 / MOE_BLOCK)
    sub1 = lax.broadcasted_iota(jnp.int32, (SUBLANES, LANES), 0)
    meta = jnp.where(sub1 == 0, pad_start + cnt, jnp.where(sub1 == 1, pad_end, jnp.broadcast_to(n_valid, (SUBLANES, LANES))))
    meta_ref[...] = meta.astype(jnp.int32)


def _route(topi, rank, cnt, n_blocks, tm):
    t = topi.shape[0]
    nb_pad = -(-n_blocks // LANES) * LANES
    tok = pl.BlockSpec((tm, LANES), lambda i: (i, 0))
    fix = lambda n: pl.BlockSpec((SUBLANES, n), lambda i: (0, 0))
    return pl.pallas_call(
        _route_kernel,
        grid=(t // tm,),
        in_specs=[tok, tok, fix(LANES)],
        out_specs=[tok, fix(nb_pad), fix(LANES)],
        out_shape=[jax.ShapeDtypeStruct((t, LANES), jnp.int32),
                   jax.ShapeDtypeStruct((SUBLANES, nb_pad), jnp.int32),
                   jax.ShapeDtypeStruct((SUBLANES, LANES), jnp.int32)],
        compiler_params=_cparams("arbitrary"),
        name="route_plan",
    )(topi, rank, cnt)


def _dispatch_kernel(pad_ref, dest_ref, h_ref, xb_ref, zero_ref, sem_z, sem_s):
    i = pl.program_id(0)
    tm = h_ref.shape[0]

    @pl.when(i == 0)
    def _():
        zero_ref[...] = jnp.zeros_like(zero_ref)
        sizes = [1 << b for b in range(int(math.log2(MOE_BLOCK)) - 1, -1, -1)]

        def fill(e, carry, do_wait):
            start = pad_ref[0, e]
            n_pad = pad_ref[1, e] - start
            off = start
            for sz in sizes:
                take = (n_pad & sz) != 0
                cp = pltpu.make_async_copy(zero_ref.at[pl.ds(0, sz)], xb_ref.at[pl.ds(off, sz)], sem_z)

                @pl.when(take)
                def _():
                    if do_wait:
                        cp.wait()
                    else:
                        cp.start()
                off = off + jnp.where(take, sz, 0)
            return carry

        half = zero_ref.shape[0]

        def fill_unused(j, carry, do_wait):
            for part in range(MOE_BLOCK // half):
                cp = pltpu.make_async_copy(zero_ref, xb_ref.at[pl.ds(j * MOE_BLOCK + part * half, half)], sem_z)
                if do_wait:
                    cp.wait()
                else:
                    cp.start()
            return carry

        n_blocks = xb_ref.shape[0] // MOE_BLOCK
        lax.fori_loop(0, N_EXPERTS, lambda e, c: fill(e, c, False), 0)
        lax.fori_loop(pad_ref[2, 0], n_blocks, lambda j, c: fill_unused(j, c, False), 0)
        lax.fori_loop(0, N_EXPERTS, lambda e, c: fill(e, c, True), 0)
        lax.fori_loop(pad_ref[2, 0], n_blocks, lambda j, c: fill_unused(j, c, True), 0)

    def issue(r, carry):
        for k in range(TOP_K):
            pltpu.make_async_copy(h_ref.at[r], xb_ref.at[dest_ref[r * TOP_K + k]], sem_s).start(priority=k % 2)
        return carry

    lax.fori_loop(0, tm, issue, 0)
    for k in range(TOP_K):
        pltpu.make_async_copy(h_ref, xb_ref.at[pl.ds(0, tm)], sem_s).wait()


def _dispatch(h2t, dest, meta, n_slots, tm):
    t = h2t.shape[0] // SUBLANES
    h3 = h2t.reshape(t, SUBLANES, LANES)
    return pl.pallas_call(
        _dispatch_kernel,
        grid_spec=pltpu.PrefetchScalarGridSpec(
            num_scalar_prefetch=1,
            grid=(t // tm,),
            in_specs=[pl.BlockSpec((tm * TOP_K,), lambda i, p: (i,), memory_space=pltpu.SMEM),
                      pl.BlockSpec((tm, SUBLANES, LANES), lambda i, p: (i, 0, 0))],
            out_specs=pl.BlockSpec(memory_space=pl.ANY),
            scratch_shapes=[pltpu.VMEM((MOE_BLOCK // 2, SUBLANES, LANES), F32),
                            pltpu.SemaphoreType.DMA, pltpu.SemaphoreType.DMA]),
        out_shape=jax.ShapeDtypeStruct((n_slots, SUBLANES, LANES), F32),
        compiler_params=_cparams("arbitrary"),
        name="moe_dispatch",
    )(meta[:3, :N_EXPERTS], dest, h3)


def _expert_kernel(be_ref, nv_ref, nxt_ref, x_ref, wgu_hbm, bgu_ref, wd_hbm, bd_ref, y_ref,
                   wgu_f32, wd_f32, wgu_bf, wd_bf, sem, *, layer):
    j = pl.program_id(0)
    valid = j < nv_ref[0]
    first = valid & ((j == 0) | (be_ref[j] != be_ref[jnp.maximum(j - 1, 0)]))

    def weight_copies(e):
        return (pltpu.make_async_copy(wgu_hbm.at[layer, e], wgu_f32, sem.at[0]),
                pltpu.make_async_copy(wd_hbm.at[layer, e], wd_f32, sem.at[1]))

    @pl.when(j == 0)
    def _():
        for cp in weight_copies(be_ref[0]):
            cp.start()

    @pl.when(first)
    def _():
        for cp in weight_copies(be_ref[j]):
            cp.wait()
        wgu_bf[...] = wgu_f32[...].astype(BF16)
        wd_bf[...] = wd_f32[...].astype(BF16)

        @pl.when(nxt_ref[j] >= 0)
        def _():
            for cp in weight_copies(nxt_ref[j]):
                cp.start()

    @pl.when(valid)
    def _():
        x = jnp.concatenate([x_ref[pl.ds(c, MOE_BLOCK, stride=SUBLANES), :] for c in range(D_MODEL // LANES)],
                            axis=1).astype(BF16)
        gu = _bdot(x, wgu_bf[...]) + bgu_ref[0]
        x_glu = jnp.minimum(gu[:, :D_EXPERT], SWIGLU_LIMIT)
        x_lin = jnp.clip(gu[:, D_EXPERT:], -SWIGLU_LIMIT, SWIGLU_LIMIT)
        act = x_glu * _sigmoid(SWIGLU_ALPHA * x_glu) * (x_lin + 1.0)
        y = _bdot(act.astype(BF16), wd_bf[...]) + bd_ref[0]
        for c in range(D_MODEL // LANES):
            y_ref[pl.ds(c, MOE_BLOCK, stride=SUBLANES), :] = y[:, c * LANES:(c + 1) * LANES]

    @pl.when(jnp.logical_not(valid))
    def _():
        y_ref[...] = jnp.zeros_like(y_ref)


def _experts(xb, block_expert, n_valid, w_gu, b_gu, w_down, b_down, layer):
    n_slots = xb.shape[0]
    n_blocks = n_slots // MOE_BLOCK
    rows = MOE_BLOCK * SUBLANES
    x2 = xb.reshape(n_slots * SUBLANES, LANES)
    depth, ne, d, de2 = w_gu.shape
    idx = jnp.arange(n_blocks, dtype=jnp.int32)
    is_first = ((idx == 0) | (block_expert != jnp.roll(block_expert, 1))) & (idx < n_valid[0])
    first_at = lax.cummin(jnp.where(is_first, idx, n_blocks)[::-1])[::-1]
    next_first = jnp.concatenate([first_at[1:], jnp.full((1,), n_blocks, jnp.int32)])
    nxt = jnp.where(next_first < n_blocks, block_expert[jnp.minimum(next_first, n_blocks - 1)], -1)
    last = lambda j, be, nv, nx: jnp.minimum(j, nv[0] - 1)
    bmap = lambda j, be, nv, nx: (layer, be[last(j, be, nv, nx)], 0, 0)
    return pl.pallas_call(
        functools.partial(_expert_kernel, layer=layer),
        grid_spec=pltpu.PrefetchScalarGridSpec(
            num_scalar_prefetch=3,
            grid=(n_blocks,),
            in_specs=[pl.BlockSpec((rows, LANES), lambda j, be, nv, nx: (last(j, be, nv, nx), 0)),
                      pl.BlockSpec(memory_space=pl.ANY),
                      pl.BlockSpec((None, 1, 1, de2), bmap),
                      pl.BlockSpec(memory_space=pl.ANY),
                      pl.BlockSpec((None, 1, 1, d), bmap)],
            out_specs=pl.BlockSpec((rows, LANES), lambda j, be, nv, nx: (j, 0)),
            scratch_shapes=[pltpu.VMEM((d, de2), F32), pltpu.VMEM((de2 // 2, d), F32),
                            pltpu.VMEM((d, de2), BF16), pltpu.VMEM((de2 // 2, d), BF16),
                            pltpu.SemaphoreType.DMA((2,))]),
        out_shape=jax.ShapeDtypeStruct((n_slots * SUBLANES, LANES), F32),
        compiler_params=_cparams("arbitrary"),
        name="moe_experts",
    )(block_expert, n_valid, nxt.astype(jnp.int32), x2, w_gu, b_gu.reshape(depth, ne, 1, de2), w_down,
      b_down.reshape(depth, ne, 1, d))


def _combine_kernel(dest_ref, dest_next_ref, yb_ref, gate_ref, x1_ref, g2_ref, gpost_ref, o_ref, buf, sem):
    i = pl.program_id(0)
    tm = x1_ref.shape[0]
    slot = i % 2

    def gather(idx_ref, into):
        def issue(r, carry):
            for k in range(TOP_K):
                src = pl.multiple_of(idx_ref[r * TOP_K + k] * SUBLANES, SUBLANES)
                dst = pl.multiple_of((k * tm + r) * SUBLANES, SUBLANES)
                pltpu.make_async_copy(yb_ref.at[pl.ds(src, SUBLANES), :], buf.at[into, pl.ds(dst, SUBLANES), :],
                                      sem.at[into]).start(priority=k % 2)
            return carry
        lax.fori_loop(0, tm, issue, 0)

    @pl.when(i == 0)
    def _():
        gather(dest_ref, 0)

    @pl.when(i + 1 < pl.num_programs(0))
    def _():
        gather(dest_next_ref, 1 - slot)

    pltpu.make_async_copy(yb_ref.at[pl.ds(0, TOP_K * tm * SUBLANES), :], buf.at[slot], sem.at[slot]).wait()
    gates = gate_ref[...]
    gk = [jnp.broadcast_to(gates[:, k:k + 1], (tm, LANES)) for k in range(TOP_K)]
    b2 = buf.at[slot]
    cols = []
    for c in range(D_MODEL // LANES):
        acc = jnp.zeros((tm, LANES), F32)
        for k in range(TOP_K):
            acc = acc + gk[k] * b2[pl.ds(k * tm * SUBLANES + c, tm, stride=SUBLANES), :]
        cols.append(acc)
    f = jnp.concatenate(cols, axis=1)
    o_ref[...] = x1_ref[...] + g2_ref[0] * (_rms(f) * gpost_ref[...])


def _combine(yb, dest, gates, x1, mod3, g_post, seq, tm):
    t, d = x1.shape
    tps = seq // tm
    n = t // tm
    return pl.pallas_call(
        _combine_kernel,
        grid=(n,),
        in_specs=[pl.BlockSpec((tm * TOP_K,), lambda i: (i,), memory_space=pltpu.SMEM),
                  pl.BlockSpec((tm * TOP_K,), lambda i: (jnp.minimum(i + 1, n - 1),), memory_space=pltpu.SMEM),
                  pl.BlockSpec(memory_space=pl.ANY),
                  pl.BlockSpec((tm, LANES), lambda i: (i, 0)),
                  pl.BlockSpec((tm, d), lambda i: (i, 0)),
                  _mod_spec(5, tps),
                  pl.BlockSpec(g_post.shape, lambda i: (0, 0))],
        out_specs=pl.BlockSpec((tm, d), lambda i: (i, 0)),
        out_shape=jax.ShapeDtypeStruct((t, d), F32),
        scratch_shapes=[pltpu.VMEM((2, TOP_K * tm * SUBLANES, LANES), F32), pltpu.SemaphoreType.DMA((2,))],
        compiler_params=_cparams("arbitrary"),
        name="moe_combine",
    )(dest, dest, yb, gates, x1, mod3, g_post)


def _moe(h2t, topi, gates, rank, cnt, x1, mod3, g_post, w_gu, b_gu, w_down, b_down, layer, seq):
    t = x1.shape[0]
    n_blocks = t * TOP_K // MOE_BLOCK + N_EXPERTS
    dest_l, blk, meta = _route(topi, rank, cnt, n_blocks, min(1024, t))
    dest = dest_l[:, :TOP_K].reshape(t * TOP_K)
    xb = _dispatch(h2t, dest, meta, n_blocks * MOE_BLOCK, MOE_BLOCK)
    yb = _experts(xb, blk[0, :n_blocks], meta[2, :1], w_gu, b_gu, w_down, b_down, layer)
    return _combine(yb, dest, gates, x1, mod3, g_post, seq, MOE_BLOCK)


TOKEN_TILE = 512
FOX_Q_TILE = 256
FOX_K_TILE = 256


def kernel(x, c, ada_w, ada_b, norm_pre_mix, norm_post_mix, norm_pre_ffn, norm_post_ffn, ev_w_in, fox_b_f, s5_lam_re, s5_lam_im, s5_log_dt, s5_b_re, s5_b_im, s5_c_re, s5_c_im, s5_d, s5_glu_w, s5_glu_b, ev_w_out, od_w_in, gla_w_up, gla_b_gate, od_w_out, router_w, router_b, exp_w_gu, exp_b_gu, exp_w_down, exp_b_down):
    bsz, seq, d = x.shape
    t = bsz * seq
    tm = min(TOKEN_TILE, seq)
    x2 = x.reshape(t, d)
    mod = _modulation(c, ada_w, ada_b)
    for l in range(DEPTH):
        i = l // 2
        mod3 = mod[l].reshape(bsz, 1, 6 * d)
        row = lambda a: a[l].reshape(1, -1)
        tail = (x2, mod3, row(norm_post_mix), row(norm_pre_ffn), router_w[l], router_b[l])
        if l % 2 == 0:
            u, q, k, v = _in_even(x2, mod3, row(norm_pre_mix), ev_w_in[i], fox_b_f[i], seq, tm)
            tables = _s5_tables(s5_lam_re[i], s5_lam_im[i], s5_log_dt[i], s5_b_re[i], s5_b_im[i],
                                s5_c_re[i], s5_c_im[i])
            ys = _s5_scan(u, bsz, seq, tables)
            yb = _fox(q, k, v, bsz, seq, min(FOX_Q_TILE, seq), min(FOX_K_TILE, seq))
            outs = _out_even(ys, u, yb, s5_d[i], s5_glu_w[i], s5_glu_b[i], ev_w_out[i], *tail, seq=seq, tm=tm)
        else:
            rq, rk, rv, sg, gq, gk, gv, sr, la = _in_odd(x2, mod3, row(norm_pre_mix), od_w_in[i],
                                                         gla_w_up[i], gla_b_gate[i], seq, tm)
            yc = _retention(rq, rk, rv, sg, bsz, seq)
            yd = _gla(gq, gk, la, gv, sr, bsz, seq)
            outs = _out_odd(yc, yd, od_w_out[i], *tail, seq=seq, tm=tm)
        x1, h2t, topi, gates, rank, cnt = outs
        x2 = _moe(h2t, topi, gates, rank, cnt, x1, mod3, row(norm_post_ffn),
                  exp_w_gu, exp_b_gu, exp_w_down, exp_b_down, l, seq)
    return x2.reshape(bsz, seq, d)
```

```python
import functools
import math

import jax
import jax.numpy as jnp
from jax import lax
from jax.experimental import pallas as pl
from jax.experimental.pallas import tpu as pltpu

F32 = jnp.float32
BF16 = jnp.bfloat16
HIGHEST = lax.Precision.HIGHEST

D_MODEL = 1024
DEPTH = 2
EPS = 1e-6
S5_CH = 512
S5_GROUP = 16
S5_GROUPS = S5_CH // S5_GROUP
S5_STATE = 64
S5_CHUNK = 8
FOX_HEADS = 8
FOX_DH = 64
FOX_W = FOX_HEADS * FOX_DH
LOG2_E = 1.4426950408889634
FOX_HPS = 8
RET_HEADS = 4
RET_DK = 128
RET_DV = 128
ROPE_BASE = 10000.0
GLA_HEADS = 4
GLA_DK = 64
GLA_DV = 128
GLA_RANK = 16
GLA_TAU = 16.0
GLA_CHUNK = 64
GLA_SUB = 16
N_EXPERTS = 32
TOP_K = 4
D_EXPERT = 1024
SWIGLU_LIMIT = 7.0
SWIGLU_ALPHA = 1.702
MOE_BLOCK = 256

LANES = 128
SUBLANES = 8
VMEM_LIMIT = 56 * 1024 * 1024


def _cparams(*sem):
    return pltpu.CompilerParams(dimension_semantics=sem, vmem_limit_bytes=VMEM_LIMIT)


def _bdot(a, b):
    return jnp.dot(a, b, preferred_element_type=F32)


def _dot_nt(a, b):
    return lax.dot_general(a, b, (((1,), (1,)), ((), ())), preferred_element_type=F32)


def _dot_tn(a, b):
    return lax.dot_general(a, b, (((0,), (0,)), ((), ())), preferred_element_type=F32)


def _split3(x):
    hi = x.astype(BF16)
    r = x - hi.astype(F32)
    mid = r.astype(BF16)
    lo = (r - mid.astype(F32)).astype(BF16)
    return hi, mid, lo


def _dot01(m01, x):
    hi, mid, lo = _split3(x)
    return _bdot(m01, hi) + _bdot(m01, mid) + _bdot(m01, lo)


def _lower_tri(n, strict=False):
    r = lax.broadcasted_iota(jnp.int32, (n, n), 0)
    c = lax.broadcasted_iota(jnp.int32, (n, n), 1)
    return ((r > c) if strict else (r >= c)).astype(BF16)


def _log_sigmoid(x):
    return jnp.minimum(x, 0.0) - jnp.log1p(jnp.exp(-jnp.abs(x)))


def _sigmoid(x):
    return 1.0 / (1.0 + jnp.exp(-x))


def _silu(x):
    return x * _sigmoid(x)


def _rms(x):
    return x * lax.rsqrt(jnp.mean(x * x, axis=-1, keepdims=True) + EPS)


def _mod_kernel(c_ref, w_ref, b_ref, o_ref):
    c = c_ref[...]
    o_ref[0] = jnp.dot(_silu(c), w_ref[0], preferred_element_type=F32, precision=HIGHEST) + b_ref[0]


def _modulation(c, ada_w, ada_b):
    depth, d, n = ada_w.shape
    bsz = c.shape[0]
    tn = D_MODEL
    return pl.pallas_call(
        _mod_kernel,
        grid=(depth, n // tn),
        in_specs=[pl.BlockSpec((bsz, d), lambda l, j: (0, 0)),
                  pl.BlockSpec((1, d, tn), lambda l, j: (l, 0, j)),
                  pl.BlockSpec((1, 1, tn), lambda l, j: (l, 0, j))],
        out_specs=pl.BlockSpec((1, bsz, tn), lambda l, j: (l, 0, j)),
        out_shape=jax.ShapeDtypeStruct((depth, bsz, n), F32),
        compiler_params=_cparams("parallel", "parallel"),
        name="adaln_mod",
    )(c, ada_w, ada_b.reshape(depth, 1, n))


def _mod_spec(chunk, tiles_per_seq):
    return pl.BlockSpec((1, 1, D_MODEL), lambda i: (i // tiles_per_seq, 0, chunk))


def _prenorm(x, g_ref, sc_ref, sh_ref):
    return _rms(x) * (g_ref[...] * (1.0 + sc_ref[0])) + sh_ref[0]


def _in_even_kernel(x_ref, sh_ref, sc_ref, g_ref, w_ref, bf_ref,
                    u_ref, q_ref, k_ref, v_ref, carry_ref, *, tiles_per_seq):
    i = pl.program_id(0)
    tm = x_ref.shape[0]
    h = _prenorm(x_ref[...], g_ref, sc_ref, sh_ref)
    z = _bdot(h.astype(BF16), w_ref[...])
    u_ref[...] = z[:, 0:S5_CH]
    v_ref[...] = z[:, S5_CH + 2 * FOX_W:S5_CH + 3 * FOX_W].astype(BF16)
    ls = _log_sigmoid(z[:, S5_CH + 3 * FOX_W:] + bf_ref[...])

    @pl.when(i % tiles_per_seq == 0)
    def _():
        carry_ref[...] = jnp.zeros_like(carry_ref)

    cum = _dot01(_lower_tri(tm), ls) + carry_ref[...]
    carry_ref[...] = cum[tm - 1:tm, :]

    lane = lax.broadcasted_iota(jnp.int32, (1, LANES), 1)
    feat = lane < FOX_DH
    ones = jnp.where(lane < FOX_DH + 3, 1.0, 0.0)
    for hd in range(FOX_HEADS):
        blk = (hd * FOX_DH) // LANES * LANES
        qs = z[:, S5_CH + blk:S5_CH + blk + LANES] * (FOX_DH ** -0.5 * LOG2_E)
        ks = z[:, S5_CH + FOX_W + blk:S5_CH + FOX_W + blk + LANES]
        if (hd * FOX_DH) % LANES:
            qs = pltpu.roll(qs, LANES - FOX_DH, 1)
            ks = pltpu.roll(ks, LANES - FOX_DH, 1)
        nf = jnp.broadcast_to(-LOG2_E * cum[:, hd:hd + 1], (tm, LANES))
        hi = nf.astype(BF16).astype(F32)
        mid = (nf - hi).astype(BF16).astype(F32)
        lo = nf - hi - mid
        bias = jnp.where(lane == FOX_DH, hi, jnp.where(lane == FOX_DH + 1, mid,
                                                       jnp.where(lane == FOX_DH + 2, lo, 0.0)))
        q_ref[:, hd * LANES:(hd + 1) * LANES] = jnp.where(feat, qs, ones).astype(BF16)
        k_ref[:, hd * LANES:(hd + 1) * LANES] = jnp.where(feat, ks, bias).astype(BF16)


def _in_even(x2, mod3, gain, w_in, b_f, seq, tm):
    t, d = x2.shape
    tiles_per_seq = seq // tm
    nw = S5_CH + 3 * FOX_W
    w = jnp.concatenate([w_in[:, :nw], jnp.pad(w_in[:, nw:], ((0, 0), (0, LANES - FOX_HEADS)))],
                        axis=1).astype(BF16)
    bf = jnp.pad(b_f, (0, LANES - FOX_HEADS)).reshape(1, LANES)
    tok = lambda n: pl.BlockSpec((tm, n), lambda i: (i, 0))
    full = lambda a: pl.BlockSpec(a.shape, lambda i: (0,) * a.ndim)
    return pl.pallas_call(
        functools.partial(_in_even_kernel, tiles_per_seq=tiles_per_seq),
        grid=(t // tm,),
        in_specs=[tok(d), _mod_spec(0, tiles_per_seq), _mod_spec(1, tiles_per_seq),
                  full(gain), full(w), full(bf)],
        out_specs=[tok(S5_CH), tok(FOX_HEADS * LANES), tok(FOX_HEADS * LANES), tok(FOX_W)],
        out_shape=[jax.ShapeDtypeStruct((t, S5_CH), F32),
                   jax.ShapeDtypeStruct((t, FOX_HEADS * LANES), BF16),
                   jax.ShapeDtypeStruct((t, FOX_HEADS * LANES), BF16),
                   jax.ShapeDtypeStruct((t, FOX_W), BF16)],
        scratch_shapes=[pltpu.VMEM((1, LANES), F32)],
        compiler_params=_cparams("arbitrary"),
        name="in_proj_even",
    )(x2, mod3, mod3, gain, w, bf)


S5_TILE_GROUPS = LANES // S5_GROUP
S5_SEQ_PARTS = 4


def _s5_kernel(u_ref, wt_ref, ws_ref, wc_ref, a_ref, y_ref, x_scr, e_scr, hp_scr, h_scr, *, nb, ncl):
    ell = S5_CHUNK
    sw = S5_TILE_GROUPS * 2 * S5_STATE

    @pl.when(pl.program_id(1) == 0)
    def _():
        h_scr[...] = jnp.zeros_like(h_scr)

    for b in range(nb):
        for t in range(ell):
            x_scr[b * ncl:(b + 1) * ncl, t * LANES:(t + 1) * LANES] = (
                u_ref[b, pl.ds(t, ncl, stride=ell), :].astype(BF16))
    x = x_scr[...]
    e = _bdot(x, ws_ref[0])
    tg = S5_TILE_GROUPS
    for j in range(2 * tg):
        e_scr[j] = e[:, j * LANES:(j + 1) * LANES]
    a1 = jnp.broadcast_to(a_ref[0, 0:1, :], (nb, sw))
    a2 = jnp.broadcast_to(a_ref[0, 1:2, :], (nb, sw))
    a2s = jnp.broadcast_to(a_ref[0, 2:3, :], (nb, sw))

    def body(c, carry):
        h, hs = carry
        rows_c = pl.ds(c, nb, stride=ncl)
        for j in range(tg):
            hp_scr[j, rows_c, :] = h[:, j * LANES:(j + 1) * LANES]
        e1 = jnp.concatenate([e_scr[j, rows_c, :] for j in range(tg)], axis=1)
        e2 = jnp.concatenate([e_scr[tg + j, rows_c, :] for j in range(tg)], axis=1)
        return a1 * h + a2 * hs + e1, a1 * hs + a2s * h + e2

    h, hs = lax.fori_loop(0, ncl, body, (h_scr[0], h_scr[1]))
    h_scr[0] = h
    h_scr[1] = hs
    hp = jnp.concatenate([hp_scr[j] for j in range(tg)], axis=1).astype(BF16)
    y = _bdot(x, wt_ref[0]) + _bdot(hp, wc_ref[0])
    for b in range(nb):
        for t in range(ell):
            y_ref[b, pl.ds(t, ncl, stride=ell), :] = y[b * ncl:(b + 1) * ncl, t * LANES:(t + 1) * LANES]


def _s5_tables(lam_re, lam_im, log_dt, b_re, b_im, c_re, c_im):
    ell, p, g = S5_CHUNK, S5_STATE, S5_GROUPS
    lr, li = lam_re.astype(F32), lam_im.astype(F32)
    dt = jnp.exp(log_dt.astype(F32))[:, None]
    mag = jnp.exp(lr * dt)
    a_re, a_im = mag * jnp.cos(li * dt), mag * jnp.sin(li * dt)
    den = lr * lr + li * li
    n_re, n_im = a_re - 1.0, a_im
    z_re = (n_re * lr + n_im * li) / den
    z_im = (n_im * lr - n_re * li) / den
    br, bi = b_re.astype(F32), b_im.astype(F32)
    bb_re = z_re[..., None] * br - z_im[..., None] * bi
    bb_im = z_re[..., None] * bi + z_im[..., None] * br
    j = jnp.arange(ell + 1, dtype=F32)[:, None, None]
    pmag = jnp.exp(j * (lr * dt)[None])
    pr, pi = pmag * jnp.cos(j * (li * dt)[None]), pmag * jnp.sin(j * (li * dt)[None])
    w_re = pr[..., None] * bb_re[None] - pi[..., None] * bb_im[None]
    w_im = pr[..., None] * bb_im[None] + pi[..., None] * bb_re[None]
    cr, ci = c_re.astype(F32), c_im.astype(F32)
    kern = (jnp.einsum('gcp,jgpd->jgcd', cr, w_re[:ell], precision=HIGHEST)
            - jnp.einsum('gcp,jgpd->jgcd', ci, w_im[:ell], precision=HIGHEST))
    s_idx = jnp.arange(ell)[:, None]
    t_idx = jnp.arange(ell)[None, :]
    lag = jnp.clip(t_idx - s_idx, 0, ell - 1)
    toep = kern[lag]
    toep = jnp.where((t_idx >= s_idx)[:, :, None, None, None], toep, 0.0)
    toep = toep.transpose(2, 0, 4, 1, 3)
    rev = jnp.arange(ell - 1, -1, -1)
    st_re = w_re[rev].transpose(1, 0, 3, 2)
    st_im = w_im[rev].transpose(1, 0, 3, 2)
    wst = jnp.stack([jnp.concatenate([st_re, st_im], -1), jnp.concatenate([st_im, st_re], -1)], axis=3)
    p1r, p1i = pr[1:], pi[1:]
    c_hr = cr[None] * p1r[:, :, None, :] - ci[None] * p1i[:, :, None, :]
    c_hi = -cr[None] * p1i[:, :, None, :] - ci[None] * p1r[:, :, None, :]
    cst = jnp.concatenate([c_hr, c_hi], axis=-1).transpose(1, 3, 0, 2)
    al_r, al_i = pr[ell], pi[ell]
    a_rows = jnp.stack([jnp.concatenate([al_r, al_r], -1), jnp.concatenate([-al_i, al_i], -1),
                        jnp.concatenate([al_i, -al_i], -1)], axis=1)

    tg = S5_TILE_GROUPS
    nt = g // tg
    tile = lambda a: a.reshape((nt, tg) + a.shape[1:])
    spread = jnp.tile(jnp.eye(S5_GROUP, dtype=BF16), (1, tg))
    lane_grp = jnp.arange(LANES) // S5_GROUP
    grp = jnp.arange(tg)
    toep_t = tile(toep).transpose(0, 2, 1, 3, 4, 5).reshape(nt, ell, LANES, ell, S5_GROUP).astype(BF16)
    wt = jnp.einsum('ksrth,hn->ksrtn', toep_t, spread)
    wt = jnp.where((lane_grp[:, None] == lane_grp[None, :])[None, None, :, None, :], wt, 0)
    wt = wt.reshape(nt, ell * LANES, ell * LANES)
    wst_t = tile(wst).transpose(0, 2, 1, 3, 4, 5).reshape(nt, ell, LANES, 2, 1, 2 * p).astype(BF16)
    ws = jnp.where((lane_grp[:, None] == grp[None, :])[None, None, :, None, :, None], wst_t, 0)
    ws = ws.reshape(nt, ell * LANES, 2 * tg * 2 * p)
    wc = jnp.einsum('kgpth,hn->kgptn', tile(cst).astype(BF16), spread)
    wc = jnp.where((grp[:, None] == lane_grp[None, :])[None, :, None, None, :], wc, 0)
    wc = wc.reshape(nt, tg * 2 * p, ell * LANES)
    a_t = tile(a_rows).transpose(0, 2, 1, 3).reshape(nt, 3, tg * 2 * p)
    a_t = jnp.pad(a_t, ((0, 0), (0, SUBLANES - 3), (0, 0)))
    return wt, ws, wc, a_t


def _s5_scan(u, bsz, seq, tables):
    wt, ws, wc, a_t = tables
    ell = S5_CHUNK
    nt = S5_GROUPS // S5_TILE_GROUPS
    parts = S5_SEQ_PARTS if seq % (S5_SEQ_PARTS * ell * 2 * SUBLANES) == 0 else 1
    ncl = seq // parts // ell
    rows = bsz * ncl
    sw = S5_TILE_GROUPS * 2 * S5_STATE
    u3 = u.reshape(bsz, seq, S5_CH)
    io = pl.BlockSpec((bsz, seq // parts, LANES), lambda k, s: (0, s, k))
    per_tile = lambda a: pl.BlockSpec((1,) + a.shape[1:], lambda k, s: (k, 0, 0))
    y = pl.pallas_call(
        functools.partial(_s5_kernel, nb=bsz, ncl=ncl),
        grid=(nt, parts),
        in_specs=[io, per_tile(wt), per_tile(ws), per_tile(wc), per_tile(a_t)],
        out_specs=io,
        out_shape=jax.ShapeDtypeStruct((bsz, seq, S5_CH), F32),
        scratch_shapes=[pltpu.VMEM((rows, ell * LANES), BF16),
                        pltpu.VMEM((2 * S5_TILE_GROUPS, rows, LANES), F32),
                        pltpu.VMEM((S5_TILE_GROUPS, rows, LANES), F32),
                        pltpu.VMEM((2, bsz, sw), F32)],
        compiler_params=_cparams("parallel", "arbitrary"),
        name="s5_scan",
    )(u3, wt, ws, wc, a_t)
    return y.reshape(bsz * seq, S5_CH)


def _fox_kernel(q_ref, k_ref, vt_ref, o_ref, *, tq, tk):
    i = pl.program_id(2)
    nh = FOX_HPS
    q = [q_ref[:, h * LANES:(h + 1) * LANES] for h in range(nh)]
    key = lax.broadcasted_iota(jnp.int32, (tk, tq), 0)
    qry = lax.broadcasted_iota(jnp.int32, (tk, tq), 1)
    per_q = tq // tk

    def block(j0, carry, mask):
        kj = k_ref[pl.ds(j0, tk), :]
        vtj = vt_ref[:, pl.ds(j0, tk)]
        ss = [_dot_nt(kj[:, h * LANES:(h + 1) * LANES], q[h]) for h in range(nh)]
        stats = []
        for h in range(nh):
            m, l, acc = carry[h]
            s = ss[h] if mask is None else jnp.where(mask, ss[h], -jnp.inf)
            m_new = jnp.maximum(m, jnp.max(s, axis=0, keepdims=True))
            p = jnp.exp2(s - m_new)
            alpha = jnp.exp2(m - m_new)
            stats.append((m_new, alpha * l + jnp.sum(p, axis=0, keepdims=True), alpha, p.astype(BF16)))
        out = []
        for h in range(nh):
            m_new, l, alpha, p = stats[h]
            acc = alpha * carry[h][2] + _bdot(vtj[h * FOX_DH:(h + 1) * FOX_DH, :], p)
            out.append((m_new, l, acc))
        return tuple(out)

    init = tuple((jnp.full((1, tq), -jnp.inf, F32), jnp.zeros((1, tq), F32), jnp.zeros((FOX_DH, tq), F32))
                 for _ in range(nh))
    carry = lax.fori_loop(0, i * per_q, lambda j, c: block(pl.multiple_of(j * tk, tk), c, None), init)
    for d in range(per_q):
        carry = block(pl.multiple_of(i * tq + d * tk, tk), carry, key + d * tk <= qry)
    for g in range(nh // 2):
        o_t = jnp.concatenate([carry[h][2] / carry[h][1] for h in (2 * g, 2 * g + 1)], axis=0)
        o_ref[:, g * LANES:(g + 1) * LANES] = o_t.T.astype(o_ref.dtype)


def _fox(q_aug, k_aug, v, bsz, seq, tq, tk):
    t = v.shape[0]
    nh = FOX_HPS
    groups = FOX_HEADS // nh
    nq = seq // tq
    v_t = v.reshape(bsz, seq, FOX_W).transpose(0, 2, 1).reshape(bsz * FOX_W, seq)
    return pl.pallas_call(
        functools.partial(_fox_kernel, tq=tq, tk=tk),
        grid=(bsz, groups, nq),
        in_specs=[pl.BlockSpec((tq, nh * LANES), lambda b, p, i: (b * nq + i, p)),
                  pl.BlockSpec((seq, nh * LANES), lambda b, p, i: (b, p)),
                  pl.BlockSpec((nh * FOX_DH, seq), lambda b, p, i: (b * groups + p, 0))],
        out_specs=pl.BlockSpec((tq, nh * FOX_DH), lambda b, p, i: (b * nq + i, p)),
        out_shape=jax.ShapeDtypeStruct((t, FOX_W), BF16),
        compiler_params=_cparams("parallel", "parallel", "arbitrary"),
        name="fox_attention",
    )(q_aug, k_aug, v_t)


_ODD_COLS = (("rq", 512), ("rk", 512), ("rv", 512), ("rg", 512), ("gq", 256), ("gk", 256),
             ("gv", 512), ("gr", 512), ("glr", LANES))


def _odd_offsets():
    off, out = 0, {}
    for name, w in _ODD_COLS:
        out[name] = (off, off + w)
        off += w
    return out, off


def _in_odd_kernel(x_ref, sh_ref, sc_ref, g_ref, w_ref, cos_ref, sin_ref, wup_ref, bg_ref,
                   rq_ref, rk_ref, rv_ref, sg_ref, gq_ref, gk_ref, gv_ref, sr_ref, la_ref):
    h = _prenorm(x_ref[...], g_ref, sc_ref, sh_ref)
    z = _bdot(h.astype(BF16), w_ref[...])
    off, _ = _odd_offsets()
    col = lambda n: z[:, off[n][0]:off[n][1]]
    cos, sin = cos_ref[...], sin_ref[...]

    def rope(t, scale):
        heads = []
        for hd in range(RET_HEADS):
            th = t[:, hd * RET_DK:(hd + 1) * RET_DK]
            heads.append((th * cos + pltpu.roll(th, RET_DK // 2, 1) * sin) * scale)
        return jnp.concatenate(heads, axis=1).astype(BF16)

    rq_ref[...] = rope(col("rq"), 1.0)
    rk_ref[...] = rope(col("rk"), RET_DK ** -0.5)
    rv_ref[...] = col("rv").astype(BF16)
    sg_ref[...] = _silu(col("rg"))
    gq_ref[...] = col("gq") * (GLA_DK ** -0.5)
    gk_ref[...] = col("gk")
    gv_ref[...] = col("gv").astype(BF16)
    sr_ref[...] = _silu(col("gr"))
    gate = jnp.dot(col("glr"), wup_ref[...], preferred_element_type=F32, precision=HIGHEST) + bg_ref[...]
    la_ref[...] = _log_sigmoid(gate) * (1.0 / GLA_TAU)


def _in_odd(x2, mod3, gain, w_in, w_up, b_gate, seq, tm):
    t, d = x2.shape
    tps = seq // tm
    ref_w = (512, 512, 512, 512, 256, 256, 512, GLA_RANK, 512)
    starts = [0]
    for wd in ref_w:
        starts.append(starts[-1] + wd)
    seg = lambda j: w_in[:, starts[j]:starts[j + 1]]
    w = jnp.concatenate([seg(0), seg(1), seg(2), seg(3), seg(4), seg(5), seg(6), seg(8),
                         jnp.pad(seg(7), ((0, 0), (0, LANES - GLA_RANK)))], axis=1).astype(BF16)
    wup = jnp.pad(w_up.astype(F32), ((0, LANES - GLA_RANK), (0, 0)))
    bg = b_gate.reshape(1, -1).astype(F32)
    half = RET_DK // 2
    inv = ROPE_BASE ** (-jnp.arange(half, dtype=F32) / half)
    ang = jnp.arange(seq, dtype=F32)[:, None] * inv[None, :]
    cos = jnp.concatenate([jnp.cos(ang), jnp.cos(ang)], axis=1)
    sin = jnp.concatenate([-jnp.sin(ang), jnp.sin(ang)], axis=1)
    tok = lambda n: pl.BlockSpec((tm, n), lambda i: (i, 0))
    full = lambda a: pl.BlockSpec(a.shape, lambda i: (0,) * a.ndim)
    pos = pl.BlockSpec((tm, RET_DK), lambda i: (i % tps, 0))
    widths = (512, 512, 512, 512, 256, 256, 512, 512, 256)
    dtypes = (BF16, BF16, BF16, F32, F32, F32, BF16, F32, F32)
    return pl.pallas_call(
        _in_odd_kernel,
        grid=(t // tm,),
        in_specs=[tok(d), _mod_spec(0, tps), _mod_spec(1, tps), full(gain), full(w), pos, pos,
                  full(wup), full(bg)],
        out_specs=[tok(n) for n in widths],
        out_shape=[jax.ShapeDtypeStruct((t, n), dt) for n, dt in zip(widths, dtypes)],
        compiler_params=_cparams("parallel"),
        name="in_proj_odd",
    )(x2, mod3, mod3, gain, w, cos, sin, wup, bg)


RET_CHUNK = 256


def _ret_kernel(q_ref, k_ref, v_ref, sg_ref, dm_ref, xi_ref, zeta_ref, gl_ref, y_ref, st_ref):
    @pl.when(pl.program_id(1) == 0)
    def _():
        st_ref[...] = jnp.zeros_like(st_ref)

    heads = range(RET_HEADS)
    col = lambda h: slice(h * RET_DK, (h + 1) * RET_DK)
    q = [q_ref[:, col(h)] for h in heads]
    k = [k_ref[:, col(h)] for h in heads]
    v = [v_ref[:, col(h)] for h in heads]
    st = [st_ref[h] for h in heads]
    s = [_dot_nt(q[h], k[h]) for h in heads]
    inter = [_bdot((q[h].astype(F32) * xi_ref[h]).astype(BF16), st[h].astype(BF16)) for h in heads]
    upd = [_dot_tn((k[h].astype(F32) * zeta_ref[h]).astype(BF16), v[h]) for h in heads]
    for h in heads:
        o = _bdot((s[h] * dm_ref[h]).astype(BF16), v[h]) + inter[h]
        st_ref[h] = gl_ref[h, 0:1, :] * st[h] + upd[h]
        y_ref[:, col(h)] = (sg_ref[:, col(h)] * _rms(o)).astype(y_ref.dtype)


def _retention(rq, rk, rv, sg, bsz, seq):
    t = rq.shape[0]
    ell = min(RET_CHUNK, seq)
    nc = seq // ell
    log_g = jnp.log(1.0 - jnp.exp2(-5.0 - jnp.arange(RET_HEADS, dtype=F32)))
    idx = jnp.arange(ell, dtype=F32)
    rel = idx[:, None] - idx[None, :]
    dmat = jnp.where(rel >= 0, jnp.exp(log_g[:, None, None] * jnp.maximum(rel, 0.0)), 0.0)
    lanes = lambda a: jnp.broadcast_to(a[..., None], a.shape + (RET_DK,))
    xi = lanes(jnp.exp(log_g[:, None] * (idx + 1.0)))
    zeta = lanes(jnp.exp(log_g[:, None] * (ell - 1.0 - idx)))
    gl = jnp.broadcast_to(jnp.exp(log_g * ell)[:, None, None], (RET_HEADS, SUBLANES, RET_DV))
    blk = pl.BlockSpec((ell, RET_HEADS * RET_DK), lambda b, c: (b * nc + c, 0))
    full = lambda a: pl.BlockSpec(a.shape, lambda b, c: (0, 0, 0))
    return pl.pallas_call(
        _ret_kernel,
        grid=(bsz, nc),
        in_specs=[blk, blk, blk, blk, full(dmat), full(xi), full(zeta), full(gl)],
        out_specs=blk,
        out_shape=jax.ShapeDtypeStruct((t, RET_HEADS * RET_DV), BF16),
        scratch_shapes=[pltpu.VMEM((RET_HEADS, RET_DK, RET_DV), F32)],
        compiler_params=_cparams("parallel", "arbitrary"),
        name="retention",
    )(rq, rk, rv, sg, dmat, xi, zeta, gl)


def _gla_kernel(q_ref, k_ref, la_ref, v_ref, sg_ref, y_ref, st_ref, b_scr, v_scr, p_scr, r_scr):
    @pl.when(pl.program_id(1) == 0)
    def _():
        st_ref[...] = jnp.zeros_like(st_ref)

    ell, sub = GLA_CHUNK, GLA_SUB
    n_sub = ell // sub
    nb = q_ref.shape[0]
    pairs = GLA_HEADS // 2
    streams = [(bl, p) for bl in range(nb) for p in range(pairs)]
    lane = lax.broadcasted_iota(jnp.int32, (1, LANES), 1)
    first = lane < GLA_DK
    head = (first, jnp.logical_not(first))
    pick = lambda h, a: jnp.where(head[h], a, 0.0).astype(BF16)
    tri = _lower_tri(ell)
    tau = lax.broadcasted_iota(jnp.int32, (sub, LANES), 0)
    rsub = lax.broadcasted_iota(jnp.int32, (LANES, 2 * LANES), 0)
    csub = lax.broadcasted_iota(jnp.int32, (LANES, 2 * LANES), 1)
    ind = ((rsub < GLA_DK) == (csub < LANES)).astype(BF16)

    val = {}
    for sid, (bl, p) in enumerate(streams):
        qk = slice(p * LANES, (p + 1) * LANES)
        q, k = q_ref[bl, :, qk], k_ref[bl, :, qk]
        b = _dot01(tri, la_ref[bl, :, qk])
        b_scr[sid] = b
        v_bf = v_ref[bl, :, p * 2 * GLA_DV:(p + 1) * 2 * GLA_DV]
        v_scr[sid] = v_bf.astype(F32)
        st = st_ref[sid]
        val[sid] = dict(q=q, k=k, b=b, st=st, vh=[v_bf[:, h * GLA_DV:(h + 1) * GLA_DV] for h in range(2)])

    for sid in val:
        d = val[sid]
        qe = d["q"] * jnp.exp(d["b"])
        st_bf = d["st"].astype(BF16)
        d["o"] = [_dot_nt(pick(h, qe), st_bf) for h in range(2)]

    row = lax.broadcasted_iota(jnp.int32, (ell, LANES), 0)
    for sid in val:
        d = val[sid]
        q, k, b = d["q"], d["k"], d["b"]
        qa, ka = [], []
        for i in range(1, n_sub):
            lo = i * sub
            ref_row = b[lo - 1:lo, :]
            in_i = (row >= lo) & (row < lo + sub)
            qa.append(jnp.where(in_i, q * jnp.exp(jnp.minimum(b - ref_row, 0.0)), 0.0))
            ka.append(jnp.where(row < lo, k * jnp.exp(jnp.minimum(ref_row - b, 0.0)), 0.0))
        k_cat = jnp.concatenate(ka, axis=1).astype(BF16)
        d["a_off"] = [_dot_nt(jnp.concatenate([pick(h, x) for x in qa], axis=1), k_cat) for h in range(2)]
    for sid in val:
        d = val[sid]
        d["off"] = [_bdot(d["a_off"][h].astype(BF16), d["vh"][h]) for h in range(2)]

    for sid, (bl, p) in enumerate(streams):
        d = val[sid]
        q, b = d["q"], d["b"]
        for i in range(n_sub):
            lo = i * sub
            qi, bi = q[lo:lo + sub], b[lo:lo + sub]
            for s in range(sub):
                k_row = k_ref[bl, pl.ds(lo + s, 1), p * LANES:(p + 1) * LANES]
                b_row = b_scr[sid, pl.ds(lo + s, 1), :]
                w = jnp.exp(jnp.minimum(bi - b_row, 0.0))
                p_scr[sid, pl.ds((lo + s) * sub, sub), :] = jnp.where(tau >= s, qi * k_row * w, 0.0).astype(BF16)
    for sid in val:
        r_scr[sid] = _bdot(p_scr[sid], ind)
    for sid in val:
        diag = [[], []]
        for i in range(n_sub):
            lo = i * sub
            for h in range(2):
                acc = jnp.zeros((sub, GLA_DV), F32)
                for s in range(sub):
                    v_row = v_scr[sid, pl.ds(lo + s, 1), h * GLA_DV:(h + 1) * GLA_DV]
                    acc = acc + r_scr[sid, pl.ds((lo + s) * sub, sub), h * LANES:(h + 1) * LANES] * v_row
                diag[h].append(acc)
        val[sid]["diag"] = diag

    for sid, (bl, p) in enumerate(streams):
        d = val[sid]
        b_last = d["b"][ell - 1:ell, :]
        kh = (d["k"] * jnp.exp(b_last - d["b"])).astype(BF16)
        upd = [_dot_tn(d["vh"][h], kh) for h in range(2)]
        st_ref[sid] = d["st"] * jnp.exp(b_last) + jnp.where(first, upd[0], upd[1])
        for h in range(2):
            oh = d["o"][h] + d["off"][h] + jnp.concatenate(d["diag"][h], axis=0)
            cols = slice((2 * p + h) * GLA_DV, (2 * p + h + 1) * GLA_DV)
            y_ref[bl, :, cols] = (sg_ref[bl, :, cols] * _rms(oh)).astype(y_ref.dtype)


GLA_BATCHES = 2


def _gla(gq, gk, la, gv, sr, bsz, seq):
    t = gq.shape[0]
    ell = GLA_CHUNK
    nc = seq // ell
    nb = min(GLA_BATCHES, bsz)
    ns = nb * (GLA_HEADS // 2)
    r3 = lambda a: a.reshape(bsz, seq, a.shape[1])
    spec = lambda w: pl.BlockSpec((nb, ell, w), lambda b, c: (b, c, 0))
    wq, wv = GLA_HEADS * GLA_DK, GLA_HEADS * GLA_DV
    y = pl.pallas_call(
        _gla_kernel,
        grid=(bsz // nb, nc),
        in_specs=[spec(wq), spec(wq), spec(wq), spec(wv), spec(wv)],
        out_specs=spec(wv),
        out_shape=jax.ShapeDtypeStruct((bsz, seq, wv), BF16),
        scratch_shapes=[pltpu.VMEM((ns, GLA_DV, LANES), F32),
                        pltpu.VMEM((ns, ell, LANES), F32),
                        pltpu.VMEM((ns, ell, 2 * GLA_DV), F32),
                        pltpu.VMEM((ns, ell * GLA_SUB, LANES), BF16),
                        pltpu.VMEM((ns, ell * GLA_SUB, 2 * LANES), F32)],
        compiler_params=_cparams("parallel", "arbitrary"),
        name="gla",
    )(r3(gq), r3(gk), r3(la), r3(gv), r3(sr))
    return y.reshape(t, wv)


def _post_tail(m, x_ref, g1_ref, gpost_ref, sh2_ref, sc2_ref, gpre_ref, rw_ref, rb_ref,
               x1_ref, h2_ref, topi_ref, gate_ref, rank_ref, cnt_ref, carry_ref):
    i = pl.program_id(0)
    tm = m.shape[0]
    x1 = x_ref[...] + g1_ref[0] * (_rms(m) * gpost_ref[...])
    x1_ref[...] = x1
    h2 = _rms(x1) * (gpre_ref[...] * (1.0 + sc2_ref[0])) + sh2_ref[0]
    for c in range(D_MODEL // LANES):
        h2_ref[pl.ds(c, tm, stride=SUBLANES), :] = h2[:, c * LANES:(c + 1) * LANES]
    h_hi = h2.astype(BF16)
    h_lo = (h2 - h_hi.astype(F32)).astype(BF16)
    w_hi, w_lo = rw_ref[0], rw_ref[1]
    logits = _bdot(h_hi, w_hi) + _bdot(h_lo, w_hi) + _bdot(h_hi, w_lo) + rb_ref[...]

    lane = lax.broadcasted_iota(jnp.int32, (tm, LANES), 1)
    lanef = lane.astype(F32)
    work = logits
    topv = jnp.full((tm, LANES), -jnp.inf, F32)
    topi = jnp.zeros((tm, LANES), F32)
    onehot = jnp.zeros((tm, LANES), F32)
    hits = []
    for k in range(TOP_K):
        mx = jnp.max(work, axis=-1, keepdims=True)
        idx = jnp.min(jnp.where(work == mx, lanef, float(LANES)), axis=-1, keepdims=True)
        hit = lanef == idx
        hits.append(hit)
        topv = jnp.where(lane == k, mx, topv)
        topi = jnp.where(lane == k, idx, topi)
        onehot = onehot + hit.astype(F32)
        work = jnp.where(hit, -jnp.inf, work)
    e = jnp.exp(topv - jnp.max(topv, axis=-1, keepdims=True))
    gate_ref[...] = e / jnp.sum(e, axis=-1, keepdims=True)
    topi_ref[...] = topi.astype(jnp.int32)

    @pl.when(i == 0)
    def _():
        carry_ref[...] = jnp.zeros_like(carry_ref)

    before = _bdot(_lower_tri(tm, strict=True), onehot.astype(BF16)) + carry_ref[...]
    rank = jnp.zeros((tm, LANES), F32)
    for k in range(TOP_K):
        rk = jnp.sum(jnp.where(hits[k], before, 0.0), axis=-1, keepdims=True)
        rank = jnp.where(lane == k, rk, rank)
    rank_ref[...] = rank.astype(jnp.int32)
    total = before[tm - 1:tm, :] + onehot[tm - 1:tm, :]
    carry_ref[...] = total
    cnt_ref[...] = jnp.broadcast_to(total, cnt_ref.shape)


def _out_even_kernel(ys_ref, u_ref, yb_ref, d_ref, gw_ref, gb_ref, wa_ref, wb_ref, *rest):
    y = ys_ref[...] + d_ref[...] * u_ref[...]
    g = jax.nn.gelu(y)
    ya = g * _sigmoid(_bdot(g.astype(BF16), gw_ref[...]) + gb_ref[...])
    m = _bdot(ya.astype(BF16), wa_ref[...]) + _bdot(yb_ref[...], wb_ref[...])
    _post_tail(m, *rest)


def _out_odd_kernel(yc_ref, yd_ref, wa_ref, wb_ref, *rest):
    m = _bdot(yc_ref[...], wa_ref[...]) + _bdot(yd_ref[...], wb_ref[...])
    _post_tail(m, *rest)


def _mixer_out(body, mix_args, mix_specs, x2, mod3, g_post, g_pre, router_w, router_b, seq, tm):
    t, d = x2.shape
    tps = seq // tm
    rw32 = jnp.pad(router_w.astype(F32), ((0, 0), (0, LANES - N_EXPERTS)))
    rw_hi = rw32.astype(BF16)
    rw = jnp.stack([rw_hi, (rw32 - rw_hi.astype(F32)).astype(BF16)])
    rb = jnp.pad(router_b.astype(F32), (0, LANES - N_EXPERTS), constant_values=-1e30).reshape(1, LANES)
    tok = lambda n: pl.BlockSpec((tm, n), lambda i: (i, 0))
    full = lambda a: pl.BlockSpec(a.shape, lambda i: (0,) * a.ndim)
    tail_args = [x2, mod3, g_post, mod3, mod3, g_pre, rw, rb]
    tail_specs = [tok(d), _mod_spec(2, tps), full(g_post), _mod_spec(3, tps), _mod_spec(4, tps),
                  full(g_pre), full(rw), full(rb)]
    return pl.pallas_call(
        body,
        grid=(t // tm,),
        in_specs=mix_specs + tail_specs,
        out_specs=[tok(d), pl.BlockSpec((tm * SUBLANES, LANES), lambda i: (i, 0)),
                   tok(LANES), tok(LANES), tok(LANES), pl.BlockSpec((SUBLANES, LANES), lambda i: (0, 0))],
        out_shape=[jax.ShapeDtypeStruct((t, d), F32),
                   jax.ShapeDtypeStruct((t * SUBLANES, LANES), F32),
                   jax.ShapeDtypeStruct((t, LANES), jnp.int32),
                   jax.ShapeDtypeStruct((t, LANES), F32),
                   jax.ShapeDtypeStruct((t, LANES), jnp.int32),
                   jax.ShapeDtypeStruct((SUBLANES, LANES), F32)],
        scratch_shapes=[pltpu.VMEM((1, LANES), F32)],
        compiler_params=_cparams("arbitrary"),
        name="mixer_out_router",
    )(*mix_args, *tail_args)


def _out_even(ys, u, yb, d_skip, glu_w, glu_b, w_out, *tail, seq, tm):
    tok = lambda n: pl.BlockSpec((tm, n), lambda i: (i, 0))
    full = lambda a: pl.BlockSpec(a.shape, lambda i: (0,) * a.ndim)
    args = [ys, u, yb, d_skip.reshape(1, -1), glu_w.astype(BF16), glu_b.reshape(1, -1),
            w_out[:S5_CH].astype(BF16), w_out[S5_CH:].astype(BF16)]
    specs = [tok(S5_CH), tok(S5_CH), tok(FOX_W)] + [full(a) for a in args[3:]]
    return _mixer_out(_out_even_kernel, args, specs, *tail, seq, tm)


def _out_odd(yc, yd, w_out, *tail, seq, tm):
    tok = lambda n: pl.BlockSpec((tm, n), lambda i: (i, 0))
    full = lambda a: pl.BlockSpec(a.shape, lambda i: (0,) * a.ndim)
    nc = yc.shape[1]
    args = [yc, yd, w_out[:nc].astype(BF16), w_out[nc:].astype(BF16)]
    specs = [tok(nc), tok(yd.shape[1])] + [full(a) for a in args[2:]]
    return _mixer_out(_out_odd_kernel, args, specs, *tail, seq, tm)


def _route_kernel(topi_ref, rank_ref, cnt_ref, dest_ref, blk_ref, meta_ref):
    tm = topi_ref.shape[0]
    cnt = cnt_ref[...]
    padded = jnp.floor((cnt + (MOE_BLOCK - 1.0)) * (1.0 / MOE_BLOCK)) * MOE_BLOCK
    r = lax.broadcasted_iota(jnp.int32, (LANES, LANES), 0)
    c = lax.broadcasted_iota(jnp.int32, (LANES, LANES), 1)
    hi, mid, lo = _split3(padded)
    incl = (r <= c).astype(BF16)
    pad_end = _bdot(hi, incl) + _bdot(mid, incl) + _bdot(lo, incl)
    pad_start = pad_end - padded
    lane = lax.broadcasted_iota(jnp.int32, (tm, LANES), 1)
    lanef = lane.astype(F32)
    topi = topi_ref[...].astype(F32)
    start_row = pad_start[0:1, :]
    dest = jnp.zeros((tm, LANES), F32)
    for k in range(TOP_K):
        idx = jnp.sum(jnp.where(lane == k, topi, 0.0), axis=-1, keepdims=True)
        st = jnp.sum(jnp.where(lanef == idx, start_row, 0.0), axis=-1, keepdims=True)
        dest = jnp.where(lane == k, st, dest)
    dest_ref[...] = dest.astype(jnp.int32) + rank_ref[...]

    nb = blk_ref.shape[1]
    end_col = jnp.sum(jnp.where(r == c, jnp.broadcast_to(pad_end[0:1, :], (LANES, LANES)), 0.0),
                      axis=-1, keepdims=True)
    jpos = lax.broadcasted_iota(jnp.int32, (LANES, nb), 1).astype(F32) * MOE_BLOCK
    esub = lax.broadcasted_iota(jnp.int32, (LANES, nb), 0)
    done = jnp.where((end_col <= jpos) & (esub < N_EXPERTS), 1.0, 0.0)
    be = jnp.minimum(jnp.sum(done, axis=0, keepdims=True), N_EXPERTS - 1.0)
    blk_ref[...] = jnp.broadcast_to(be, blk_ref.shape).astype(jnp.int32)
    lane1 = lax.broadcasted_iota(jnp.int32, (SUBLANES, LANES), 1)
    n_valid = jnp.sum(jnp.where(lane1 == N_EXPERTS - 1, pad_end, 0.0), axis=-1, keepdims=True) * (1.0 / MOE_BLOCK)
    sub1 = lax.broadcasted_iota(jnp.int32, (SUBLANES, LANES), 0)
    meta = jnp.where(sub1 == 0, pad_start + cnt, jnp.where(sub1 == 1, pad_end, jnp.broadcast_to(n_valid, (SUBLANES, LANES))))
    meta_ref[...] = meta.astype(jnp.int32)


def _route(topi, rank, cnt, n_blocks, tm):
    t = topi.shape[0]
    nb_pad = -(-n_blocks // LANES) * LANES
    tok = pl.BlockSpec((tm, LANES), lambda i: (i, 0))
    fix = lambda n: pl.BlockSpec((SUBLANES, n), lambda i: (0, 0))
    return pl.pallas_call(
        _route_kernel,
        grid=(t // tm,),
        in_specs=[tok, tok, fix(LANES)],
        out_specs=[tok, fix(nb_pad), fix(LANES)],
        out_shape=[jax.ShapeDtypeStruct((t, LANES), jnp.int32),
                   jax.ShapeDtypeStruct((SUBLANES, nb_pad), jnp.int32),
                   jax.ShapeDtypeStruct((SUBLANES, LANES), jnp.int32)],
        compiler_params=_cparams("arbitrary"),
        name="route_plan",
    )(topi, rank, cnt)


def _dispatch_kernel(pad_ref, dest_ref, h_ref, xb_ref, zero_ref, sem_z, sem_s):
    i = pl.program_id(0)
    tm = h_ref.shape[0]

    @pl.when(i == 0)
    def _():
        zero_ref[...] = jnp.zeros_like(zero_ref)
        sizes = [1 << b for b in range(int(math.log2(MOE_BLOCK)) - 1, -1, -1)]

        def fill(e, carry, do_wait):
            start = pad_ref[0, e]
            n_pad = pad_ref[1, e] - start
            off = start
            for sz in sizes:
                take = (n_pad & sz) != 0
                cp = pltpu.make_async_copy(zero_ref.at[pl.ds(0, sz)], xb_ref.at[pl.ds(off, sz)], sem_z)

                @pl.when(take)
                def _():
                    if do_wait:
                        cp.wait()
                    else:
                        cp.start()
                off = off + jnp.where(take, sz, 0)
            return carry

        half = zero_ref.shape[0]

        def fill_unused(j, carry, do_wait):
            for part in range(MOE_BLOCK // half):
                cp = pltpu.make_async_copy(zero_ref, xb_ref.at[pl.ds(j * MOE_BLOCK + part * half, half)], sem_z)
                if do_wait:
                    cp.wait()
                else:
                    cp.start()
            return carry

        n_blocks = xb_ref.shape[0] // MOE_BLOCK
        lax.fori_loop(0, N_EXPERTS, lambda e, c: fill(e, c, False), 0)
        lax.fori_loop(pad_ref[2, 0], n_blocks, lambda j, c: fill_unused(j, c, False), 0)
        lax.fori_loop(0, N_EXPERTS, lambda e, c: fill(e, c, True), 0)
        lax.fori_loop(pad_ref[2, 0], n_blocks, lambda j, c: fill_unused(j, c, True), 0)

    def issue(r, carry):
        for k in range(TOP_K):
            pltpu.make_async_copy(h_ref.at[r], xb_ref.at[dest_ref[r * TOP_K + k]], sem_s).start(priority=k % 2)
        return carry

    lax.fori_loop(0, tm, issue, 0)
    for k in range(TOP_K):
        pltpu.make_async_copy(h_ref, xb_ref.at[pl.ds(0, tm)], sem_s).wait()


def _dispatch(h2t, dest, meta, n_slots, tm):
    t = h2t.shape[0] // SUBLANES
    h3 = h2t.reshape(t, SUBLANES, LANES)
    return pl.pallas_call(
        _dispatch_kernel,
        grid_spec=pltpu.PrefetchScalarGridSpec(
            num_scalar_prefetch=1,
            grid=(t // tm,),
            in_specs=[pl.BlockSpec((tm * TOP_K,), lambda i, p: (i,), memory_space=pltpu.SMEM),
                      pl.BlockSpec((tm, SUBLANES, LANES), lambda i, p: (i, 0, 0))],
            out_specs=pl.BlockSpec(memory_space=pl.ANY),
            scratch_shapes=[pltpu.VMEM((MOE_BLOCK // 2, SUBLANES, LANES), F32),
                            pltpu.SemaphoreType.DMA, pltpu.SemaphoreType.DMA]),
        out_shape=jax.ShapeDtypeStruct((n_slots, SUBLANES, LANES), F32),
        compiler_params=_cparams("arbitrary"),
        name="moe_dispatch",
    )(meta[:3, :N_EXPERTS], dest, h3)


def _expert_kernel(be_ref, nv_ref, nxt_ref, x_ref, wgu_hbm, bgu_ref, wd_hbm, bd_ref, y_ref,
                   wgu_f32, wd_f32, wgu_bf, wd_bf, sem, *, layer):
    j = pl.program_id(0)
    valid = j < nv_ref[0]
    first = valid & ((j == 0) | (be_ref[j] != be_ref[jnp.maximum(j - 1, 0)]))

    def weight_copies(e):
        return (pltpu.make_async_copy(wgu_hbm.at[layer, e], wgu_f32, sem.at[0]),
                pltpu.make_async_copy(wd_hbm.at[layer, e], wd_f32, sem.at[1]))

    @pl.when(j == 0)
    def _():
        for cp in weight_copies(be_ref[0]):
            cp.start()

    @pl.when(first)
    def _():
        for cp in weight_copies(be_ref[j]):
            cp.wait()
        wgu_bf[...] = wgu_f32[...].astype(BF16)
        wd_bf[...] = wd_f32[...].astype(BF16)

        @pl.when(nxt_ref[j] >= 0)
        def _():
            for cp in weight_copies(nxt_ref[j]):
                cp.start()

    @pl.when(valid)
    def _():
        x = jnp.concatenate([x_ref[pl.ds(c, MOE_BLOCK, stride=SUBLANES), :] for c in range(D_MODEL // LANES)],
                            axis=1).astype(BF16)
        gu = _bdot(x, wgu_bf[...]) + bgu_ref[0]
        x_glu = jnp.minimum(gu[:, :D_EXPERT], SWIGLU_LIMIT)
        x_lin = jnp.clip(gu[:, D_EXPERT:], -SWIGLU_LIMIT, SWIGLU_LIMIT)
        act = x_glu * _sigmoid(SWIGLU_ALPHA * x_glu) * (x_lin + 1.0)
        y = _bdot(act.astype(BF16), wd_bf[...]) + bd_ref[0]
        for c in range(D_MODEL // LANES):
            y_ref[pl.ds(c, MOE_BLOCK, stride=SUBLANES), :] = y[:, c * LANES:(c + 1) * LANES]

    @pl.when(jnp.logical_not(valid))
    def _():
        y_ref[...] = jnp.zeros_like(y_ref)


def _experts(xb, block_expert, n_valid, w_gu, b_gu, w_down, b_down, layer):
    n_slots = xb.shape[0]
    n_blocks = n_slots // MOE_BLOCK
    rows = MOE_BLOCK * SUBLANES
    x2 = xb.reshape(n_slots * SUBLANES, LANES)
    depth, ne, d, de2 = w_gu.shape
    idx = jnp.arange(n_blocks, dtype=jnp.int32)
    is_first = ((idx == 0) | (block_expert != jnp.roll(block_expert, 1))) & (idx < n_valid[0])
    first_at = lax.cummin(jnp.where(is_first, idx, n_blocks)[::-1])[::-1]
    next_first = jnp.concatenate([first_at[1:], jnp.full((1,), n_blocks, jnp.int32)])
    nxt = jnp.where(next_first < n_blocks, block_expert[jnp.minimum(next_first, n_blocks - 1)], -1)
    last = lambda j, be, nv, nx: jnp.minimum(j, nv[0] - 1)
    bmap = lambda j, be, nv, nx: (layer, be[last(j, be, nv, nx)], 0, 0)
    return pl.pallas_call(
        functools.partial(_expert_kernel, layer=layer),
        grid_spec=pltpu.PrefetchScalarGridSpec(
            num_scalar_prefetch=3,
            grid=(n_blocks,),
            in_specs=[pl.BlockSpec((rows, LANES), lambda j, be, nv, nx: (last(j, be, nv, nx), 0)),
                      pl.BlockSpec(memory_space=pl.ANY),
                      pl.BlockSpec((None, 1, 1, de2), bmap),
                      pl.BlockSpec(memory_space=pl.ANY),
                      pl.BlockSpec((None, 1, 1, d), bmap)],
            out_specs=pl.BlockSpec((rows, LANES), lambda j, be, nv, nx: (j, 0)),
            scratch_shapes=[pltpu.VMEM((d, de2), F32), pltpu.VMEM((de2 // 2, d), F32),
                            pltpu.VMEM((d, de2), BF16), pltpu.VMEM((de2 // 2, d), BF16),
                            pltpu.SemaphoreType.DMA((2,))]),
        out_shape=jax.ShapeDtypeStruct((n_slots * SUBLANES, LANES), F32),
        compiler_params=_cparams("arbitrary"),
        name="moe_experts",
    )(block_expert, n_valid, nxt.astype(jnp.int32), x2, w_gu, b_gu.reshape(depth, ne, 1, de2), w_down,
      b_down.reshape(depth, ne, 1, d))


def _combine_kernel(dest_ref, dest_next_ref, yb_ref, gate_ref, x1_ref, g2_ref, gpost_ref, o_ref, buf, sem):
    i = pl.program_id(0)
    tm = x1_ref.shape[0]
    slot = i % 2

    def gather(idx_ref, into):
        def issue(r, carry):
            for k in range(TOP_K):
                src = pl.multiple_of(idx_ref[r * TOP_K + k] * SUBLANES, SUBLANES)
                dst = pl.multiple_of((k * tm + r) * SUBLANES, SUBLANES)
                pltpu.make_async_copy(yb_ref.at[pl.ds(src, SUBLANES), :], buf.at[into, pl.ds(dst, SUBLANES), :],
                                      sem.at[into]).start(priority=k % 2)
            return carry
        lax.fori_loop(0, tm, issue, 0)

    @pl.when(i == 0)
    def _():
        gather(dest_ref, 0)

    @pl.when(i + 1 < pl.num_programs(0))
    def _():
        gather(dest_next_ref, 1 - slot)

    pltpu.make_async_copy(yb_ref.at[pl.ds(0, TOP_K * tm * SUBLANES), :], buf.at[slot], sem.at[slot]).wait()
    gates = gate_ref[...]
    gk = [jnp.broadcast_to(gates[:, k:k + 1], (tm, LANES)) for k in range(TOP_K)]
    b2 = buf.at[slot]
    cols = []
    for c in range(D_MODEL // LANES):
        acc = jnp.zeros((tm, LANES), F32)
        for k in range(TOP_K):
            acc = acc + gk[k] * b2[pl.ds(k * tm * SUBLANES + c, tm, stride=SUBLANES), :]
        cols.append(acc)
    f = jnp.concatenate(cols, axis=1)
    o_ref[...] = x1_ref[...] + g2_ref[0] * (_rms(f) * gpost_ref[...])


def _combine(yb, dest, gates, x1, mod3, g_post, seq, tm):
    t, d = x1.shape
    tps = seq // tm
    n = t // tm
    return pl.pallas_call(
        _combine_kernel,
        grid=(n,),
        in_specs=[pl.BlockSpec((tm * TOP_K,), lambda i: (i,), memory_space=pltpu.SMEM),
                  pl.BlockSpec((tm * TOP_K,), lambda i: (jnp.minimum(i + 1, n - 1),), memory_space=pltpu.SMEM),
                  pl.BlockSpec(memory_space=pl.ANY),
                  pl.BlockSpec((tm, LANES), lambda i: (i, 0)),
                  pl.BlockSpec((tm, d), lambda i: (i, 0)),
                  _mod_spec(5, tps),
                  pl.BlockSpec(g_post.shape, lambda i: (0, 0))],
        out_specs=pl.BlockSpec((tm, d), lambda i: (i, 0)),
        out_shape=jax.ShapeDtypeStruct((t, d), F32),
        scratch_shapes=[pltpu.VMEM((2, TOP_K * tm * SUBLANES, LANES), F32), pltpu.SemaphoreType.DMA((2,))],
        compiler_params=_cparams("arbitrary"),
        name="moe_combine",
    )(dest, dest, yb, gates, x1, mod3, g_post)


def _moe(h2t, topi, gates, rank, cnt, x1, mod3, g_post, w_gu, b_gu, w_down, b_down, layer, seq):
    t = x1.shape[0]
    n_blocks = t * TOP_K // MOE_BLOCK + N_EXPERTS
    dest_l, blk, meta = _route(topi, rank, cnt, n_blocks, min(1024, t))
    dest = dest_l[:, :TOP_K].reshape(t * TOP_K)
    xb = _dispatch(h2t, dest, meta, n_blocks * MOE_BLOCK, MOE_BLOCK)
    yb = _experts(xb, blk[0, :n_blocks], meta[2, :1], w_gu, b_gu, w_down, b_down, layer)
    return _combine(yb, dest, gates, x1, mod3, g_post, seq, MOE_BLOCK)


TOKEN_TILE = 512
FOX_Q_TILE = 256
FOX_K_TILE = 256


def kernel(x, c, ada_w, ada_b, norm_pre_mix, norm_post_mix, norm_pre_ffn, norm_post_ffn, ev_w_in, fox_b_f, s5_lam_re, s5_lam_im, s5_log_dt, s5_b_re, s5_b_im, s5_c_re, s5_c_im, s5_d, s5_glu_w, s5_glu_b, ev_w_out, od_w_in, gla_w_up, gla_b_gate, od_w_out, router_w, router_b, exp_w_gu, exp_b_gu, exp_w_down, exp_b_down):
    bsz, seq, d = x.shape
    t = bsz * seq
    tm = min(TOKEN_TILE, seq)
    x2 = x.reshape(t, d)
    mod = _modulation(c, ada_w, ada_b)
    for l in range(DEPTH):
        i = l // 2
        mod3 = mod[l].reshape(bsz, 1, 6 * d)
        row = lambda a: a[l].reshape(1, -1)
        tail = (x2, mod3, row(norm_post_mix), row(norm_pre_ffn), router_w[l], router_b[l])
        if l % 2 == 0:
            u, q, k, v = _in_even(x2, mod3, row(norm_pre_mix), ev_w_in[i], fox_b_f[i], seq, tm)
            tables = _s5_tables(s5_lam_re[i], s5_lam_im[i], s5_log_dt[i], s5_b_re[i], s5_b_im[i],
                                s5_c_re[i], s5_c_im[i])
            ys = _s5_scan(u, bsz, seq, tables)
            yb = _fox(q, k, v, bsz, seq, min(FOX_Q_TILE, seq), min(FOX_K_TILE, seq))
            outs = _out_even(ys, u, yb, s5_d[i], s5_glu_w[i], s5_glu_b[i], ev_w_out[i], *tail, seq=seq, tm=tm)
        else:
            rq, rk, rv, sg, gq, gk, gv, sr, la = _in_odd(x2, mod3, row(norm_pre_mix), od_w_in[i],
                                                         gla_w_up[i], gla_b_gate[i], seq, tm)
            yc = _retention(rq, rk, rv, sg, bsz, seq)
            yd = _gla(gq, gk, la, gv, sr, bsz, seq)
            outs = _out_odd(yc, yd, od_w_out[i], *tail, seq=seq, tm=tm)
        x1, h2t, topi, gates, rank, cnt = outs
        x2 = _moe(h2t, topi, gates, rank, cnt, x1, mod3, row(norm_post_ffn),
                  exp_w_gu, exp_b_gu, exp_w_down, exp_b_down, l, seq)
    return x2.reshape(bsz, seq, d)
```

```python
import functools
import math

import jax
import jax.numpy as jnp
from jax import lax
from jax.experimental import pallas as pl
from jax.experimental.pallas import tpu as pltpu

F32 = jnp.float32
BF16 = jnp.bfloat16
HIGHEST = lax.Precision.HIGHEST

D_MODEL = 1024
DEPTH = 2
EPS = 1e-6
S5_CH = 512
S5_GROUP = 16
S5_GROUPS = S5_CH // S5_GROUP
S5_STATE = 64
S5_CHUNK = 8
FOX_HEADS = 8
FOX_DH = 64
FOX_W = FOX_HEADS * FOX_DH
LOG2_E = 1.4426950408889634
FOX_HPS = 8
RET_HEADS = 4
RET_DK = 128
RET_DV = 128
ROPE_BASE = 10000.0
GLA_HEADS = 4
GLA_DK = 64
GLA_DV = 128
GLA_RANK = 16
GLA_TAU = 16.0
GLA_CHUNK = 64
GLA_SUB = 16
N_EXPERTS = 32
TOP_K = 4
D_EXPERT = 1024
SWIGLU_LIMIT = 7.0
SWIGLU_ALPHA = 1.702
MOE_BLOCK = 256

LANES = 128
SUBLANES = 8
VMEM_LIMIT = 56 * 1024 * 1024


def _cparams(*sem):
    return pltpu.CompilerParams(dimension_semantics=sem, vmem_limit_bytes=VMEM_LIMIT)


def _bdot(a, b):
    return jnp.dot(a, b, preferred_element_type=F32)


def _dot_nt(a, b):
    return lax.dot_general(a, b, (((1,), (1,)), ((), ())), preferred_element_type=F32)


def _dot_tn(a, b):
    return lax.dot_general(a, b, (((0,), (0,)), ((), ())), preferred_element_type=F32)


def _split3(x):
    hi = x.astype(BF16)
    r = x - hi.astype(F32)
    mid = r.astype(BF16)
    lo = (r - mid.astype(F32)).astype(BF16)
    return hi, mid, lo


def _dot01(m01, x):
    hi, mid, lo = _split3(x)
    return _bdot(m01, hi) + _bdot(m01, mid) + _bdot(m01, lo)


def _lower_tri(n, strict=False):
    r = lax.broadcasted_iota(jnp.int32, (n, n), 0)
    c = lax.broadcasted_iota(jnp.int32, (n, n), 1)
    return ((r > c) if strict else (r >= c)).astype(BF16)


def _log_sigmoid(x):
    return jnp.minimum(x, 0.0) - jnp.log1p(jnp.exp(-jnp.abs(x)))


def _sigmoid(x):
    return 1.0 / (1.0 + jnp.exp(-x))


def _silu(x):
    return x * _sigmoid(x)


def _rms(x):
    return x * lax.rsqrt(jnp.mean(x * x, axis=-1, keepdims=True) + EPS)


def _mod_kernel(c_ref, w_ref, b_ref, o_ref):
    c = c_ref[...]
    o_ref[0] = jnp.dot(_silu(c), w_ref[0], preferred_element_type=F32, precision=HIGHEST) + b_ref[0]


def _modulation(c, ada_w, ada_b):
    depth, d, n = ada_w.shape
    bsz = c.shape[0]
    tn = D_MODEL
    return pl.pallas_call(
        _mod_kernel,
        grid=(depth, n // tn),
        in_specs=[pl.BlockSpec((bsz, d), lambda l, j: (0, 0)),
                  pl.BlockSpec((1, d, tn), lambda l, j: (l, 0, j)),
                  pl.BlockSpec((1, 1, tn), lambda l, j: (l, 0, j))],
        out_specs=pl.BlockSpec((1, bsz, tn), lambda l, j: (l, 0, j)),
        out_shape=jax.ShapeDtypeStruct((depth, bsz, n), F32),
        compiler_params=_cparams("parallel", "parallel"),
        name="adaln_mod",
    )(c, ada_w, ada_b.reshape(depth, 1, n))


def _mod_spec(chunk, tiles_per_seq):
    return pl.BlockSpec((1, 1, D_MODEL), lambda i: (i // tiles_per_seq, 0, chunk))


def _prenorm(x, g_ref, sc_ref, sh_ref):
    return _rms(x) * (g_ref[...] * (1.0 + sc_ref[0])) + sh_ref[0]


def _in_even_kernel(x_ref, sh_ref, sc_ref, g_ref, w_ref, bf_ref,
                    u_ref, q_ref, k_ref, v_ref, carry_ref, *, tiles_per_seq):
    i = pl.program_id(0)
    tm = x_ref.shape[0]
    h = _prenorm(x_ref[...], g_ref, sc_ref, sh_ref)
    z = _bdot(h.astype(BF16), w_ref[...])
    u_ref[...] = z[:, 0:S5_CH]
    v_ref[...] = z[:, S5_CH + 2 * FOX_W:S5_CH + 3 * FOX_W].astype(BF16)
    ls = _log_sigmoid(z[:, S5_CH + 3 * FOX_W:] + bf_ref[...])

    @pl.when(i % tiles_per_seq == 0)
    def _():
        carry_ref[...] = jnp.zeros_like(carry_ref)

    cum = _dot01(_lower_tri(tm), ls) + carry_ref[...]
    carry_ref[...] = cum[tm - 1:tm, :]

    lane = lax.broadcasted_iota(jnp.int32, (1, LANES), 1)
    feat = lane < FOX_DH
    ones = jnp.where(lane < FOX_DH + 3, 1.0, 0.0)
    for hd in range(FOX_HEADS):
        blk = (hd * FOX_DH) // LANES * LANES
        qs = z[:, S5_CH + blk:S5_CH + blk + LANES] * (FOX_DH ** -0.5 * LOG2_E)
        ks = z[:, S5_CH + FOX_W + blk:S5_CH + FOX_W + blk + LANES]
        if (hd * FOX_DH) % LANES:
            qs = pltpu.roll(qs, LANES - FOX_DH, 1)
            ks = pltpu.roll(ks, LANES - FOX_DH, 1)
        nf = jnp.broadcast_to(-LOG2_E * cum[:, hd:hd + 1], (tm, LANES))
        hi = nf.astype(BF16).astype(F32)
        mid = (nf - hi).astype(BF16).astype(F32)
        lo = nf - hi - mid
        bias = jnp.where(lane == FOX_DH, hi, jnp.where(lane == FOX_DH + 1, mid,
                                                       jnp.where(lane == FOX_DH + 2, lo, 0.0)))
        q_ref[:, hd * LANES:(hd + 1) * LANES] = jnp.where(feat, qs, ones).astype(BF16)
        k_ref[:, hd * LANES:(hd + 1) * LANES] = jnp.where(feat, ks, bias).astype(BF16)


def _in_even(x2, mod3, gain, w_in, b_f, seq, tm):
    t, d = x2.shape
    tiles_per_seq = seq // tm
    nw = S5_CH + 3 * FOX_W
    w = jnp.concatenate([w_in[:, :nw], jnp.pad(w_in[:, nw:], ((0, 0), (0, LANES - FOX_HEADS)))],
                        axis=1).astype(BF16)
    bf = jnp.pad(b_f, (0, LANES - FOX_HEADS)).reshape(1, LANES)
    tok = lambda n: pl.BlockSpec((tm, n), lambda i: (i, 0))
    full = lambda a: pl.BlockSpec(a.shape, lambda i: (0,) * a.ndim)
    return pl.pallas_call(
        functools.partial(_in_even_kernel, tiles_per_seq=tiles_per_seq),
        grid=(t // tm,),
        in_specs=[tok(d), _mod_spec(0, tiles_per_seq), _mod_spec(1, tiles_per_seq),
                  full(gain), full(w), full(bf)],
        out_specs=[tok(S5_CH), tok(FOX_HEADS * LANES), tok(FOX_HEADS * LANES), tok(FOX_W)],
        out_shape=[jax.ShapeDtypeStruct((t, S5_CH), F32),
                   jax.ShapeDtypeStruct((t, FOX_HEADS * LANES), BF16),
                   jax.ShapeDtypeStruct((t, FOX_HEADS * LANES), BF16),
                   jax.ShapeDtypeStruct((t, FOX_W), BF16)],
        scratch_shapes=[pltpu.VMEM((1, LANES), F32)],
        compiler_params=_cparams("arbitrary"),
        name="in_proj_even",
    )(x2, mod3, mod3, gain, w, bf)


S5_TILE_GROUPS = LANES // S5_GROUP
S5_SEQ_PARTS = 4


def _s5_kernel(u_ref, wt_ref, ws_ref, wc_ref, a_ref, y_ref, x_scr, e_scr, hp_scr, h_scr, *, nb, ncl):
    ell = S5_CHUNK
    sw = S5_TILE_GROUPS * 2 * S5_STATE

    @pl.when(pl.program_id(1) == 0)
    def _():
        h_scr[...] = jnp.zeros_like(h_scr)

    blk = nb * SUBLANES
    for hi in range(ncl // SUBLANES):
        for b in range(nb):
            for t in range(ell):
                x_scr[hi * blk + b * SUBLANES:hi * blk + (b + 1) * SUBLANES, t * LANES:(t + 1) * LANES] = (
                    u_ref[b, pl.ds(hi * SUBLANES * ell + t, SUBLANES, stride=ell), :])
    x = x_scr[...].astype(BF16)
    e = _bdot(x, ws_ref[0])
    tg = S5_TILE_GROUPS
    for j in range(tg):
        e_scr[j] = e[:, j * LANES:(j + 1) * LANES]
        e_scr[tg + j] = pltpu.roll(e[:, j * LANES:(j + 1) * LANES], S5_STATE, 1)
    a1 = jnp.broadcast_to(a_ref[0, 0:1, :], (nb, sw))
    a2 = jnp.broadcast_to(a_ref[0, 1:2, :], (nb, sw))
    a2s = jnp.broadcast_to(a_ref[0, 2:3, :], (nb, sw))

    def body(c, carry):
        h, hs = carry
        rows_c = pl.ds((c // SUBLANES) * blk + c % SUBLANES, nb, stride=SUBLANES)
        for j in range(tg):
            hp_scr[j, rows_c, :] = h[:, j * LANES:(j + 1) * LANES]
        e1 = jnp.concatenate([e_scr[j, rows_c, :] for j in range(tg)], axis=1)
        e2 = jnp.concatenate([e_scr[tg + j, rows_c, :] for j in range(tg)], axis=1)
        return a1 * h + a2 * hs + e1, a1 * hs + a2s * h + e2

    h, hs = lax.fori_loop(0, ncl, body, (h_scr[0], h_scr[1]))
    h_scr[0] = h
    h_scr[1] = hs
    hp = jnp.concatenate([hp_scr[j] for j in range(tg)], axis=1).astype(BF16)
    y = _bdot(x, wt_ref[0]) + _bdot(hp, wc_ref[0])
    for hi in range(ncl // SUBLANES):
        for b in range(nb):
            for t in range(ell):
                y_ref[b, pl.ds(hi * SUBLANES * ell + t, SUBLANES, stride=ell), :] = (
                    y[hi * blk + b * SUBLANES:hi * blk + (b + 1) * SUBLANES, t * LANES:(t + 1) * LANES])


def _s5_tables(lam_re, lam_im, log_dt, b_re, b_im, c_re, c_im):
    ell, p, g = S5_CHUNK, S5_STATE, S5_GROUPS
    lr, li = lam_re.astype(F32), lam_im.astype(F32)
    dt = jnp.exp(log_dt.astype(F32))[:, None]
    mag = jnp.exp(lr * dt)
    a_re, a_im = mag * jnp.cos(li * dt), mag * jnp.sin(li * dt)
    den = lr * lr + li * li
    n_re, n_im = a_re - 1.0, a_im
    z_re = (n_re * lr + n_im * li) / den
    z_im = (n_im * lr - n_re * li) / den
    br, bi = b_re.astype(F32), b_im.astype(F32)
    bb_re = z_re[..., None] * br - z_im[..., None] * bi
    bb_im = z_re[..., None] * bi + z_im[..., None] * br
    j = jnp.arange(ell + 1, dtype=F32)[:, None, None]
    pmag = jnp.exp(j * (lr * dt)[None])
    pr, pi = pmag * jnp.cos(j * (li * dt)[None]), pmag * jnp.sin(j * (li * dt)[None])
    w_re = pr[..., None] * bb_re[None] - pi[..., None] * bb_im[None]
    w_im = pr[..., None] * bb_im[None] + pi[..., None] * bb_re[None]
    cr, ci = c_re.astype(F32), c_im.astype(F32)
    kern = (jnp.einsum('gcp,jgpd->jgcd', cr, w_re[:ell], precision=HIGHEST)
            - jnp.einsum('gcp,jgpd->jgcd', ci, w_im[:ell], precision=HIGHEST))
    s_idx = jnp.arange(ell)[:, None]
    t_idx = jnp.arange(ell)[None, :]
    lag = jnp.clip(t_idx - s_idx, 0, ell - 1)
    toep = kern[lag]
    toep = jnp.where((t_idx >= s_idx)[:, :, None, None, None], toep, 0.0)
    toep = toep.transpose(2, 0, 4, 1, 3)
    rev = jnp.arange(ell - 1, -1, -1)
    st_re = w_re[rev].transpose(1, 0, 3, 2)
    st_im = w_im[rev].transpose(1, 0, 3, 2)
    wst = jnp.concatenate([st_re, st_im], -1)
    p1r, p1i = pr[1:], pi[1:]
    c_hr = cr[None] * p1r[:, :, None, :] - ci[None] * p1i[:, :, None, :]
    c_hi = -cr[None] * p1i[:, :, None, :] - ci[None] * p1r[:, :, None, :]
    cst = jnp.concatenate([c_hr, c_hi], axis=-1).transpose(1, 3, 0, 2)
    al_r, al_i = pr[ell], pi[ell]
    a_rows = jnp.stack([jnp.concatenate([al_r, al_r], -1), jnp.concatenate([-al_i, al_i], -1),
                        jnp.concatenate([al_i, -al_i], -1)], axis=1)

    tg = S5_TILE_GROUPS
    nt = g // tg
    tile = lambda a: a.reshape((nt, tg) + a.shape[1:])
    lane_grp = jnp.arange(LANES) // S5_GROUP
    step_lane_grp = jnp.tile(lane_grp, ell)
    spread_out = (jnp.eye(ell, dtype=BF16)[:, None, :, None, None]
                  * jnp.eye(S5_GROUP, dtype=BF16)[None, :, None, None, :]
                  * jnp.ones((1, 1, 1, tg, 1), BF16)).reshape(ell * S5_GROUP, ell * LANES)
    toep_t = tile(toep).transpose(0, 2, 1, 3, 4, 5).reshape(nt, ell, LANES, ell * S5_GROUP).astype(BF16)
    wt = jnp.einsum('ksrm,mn->ksrn', toep_t, spread_out)
    wt = jnp.where((lane_grp[:, None] == step_lane_grp[None, :])[None, None], wt, 0)
    wt = wt.reshape(nt, ell * LANES, ell * LANES)
    spread_state = jnp.tile(jnp.eye(2 * p, dtype=BF16), (1, tg))
    state_grp = jnp.arange(tg * 2 * p) // (2 * p)
    wst_t = tile(wst).transpose(0, 2, 1, 3, 4).reshape(nt, ell, LANES, 2 * p).astype(BF16)
    ws = jnp.einsum('ksrp,pn->ksrn', wst_t, spread_state)
    ws = jnp.where((lane_grp[:, None] == state_grp[None, :])[None, None], ws, 0)
    ws = ws.reshape(nt, ell * LANES, tg * 2 * p)
    cst_t = tile(cst).reshape(nt, tg * 2 * p, ell * S5_GROUP).astype(BF16)
    wc = jnp.einsum('krm,mn->krn', cst_t, spread_out)
    wc = jnp.where((state_grp[:, None] == step_lane_grp[None, :])[None], wc, 0)
    a_t = tile(a_rows).transpose(0, 2, 1, 3).reshape(nt, 3, tg * 2 * p)
    a_t = jnp.pad(a_t, ((0, 0), (0, SUBLANES - 3), (0, 0)))
    return wt, ws, wc, a_t


def _s5_scan(u, bsz, seq, tables):
    wt, ws, wc, a_t = tables
    ell = S5_CHUNK
    nt = S5_GROUPS // S5_TILE_GROUPS
    parts = S5_SEQ_PARTS if seq % (S5_SEQ_PARTS * ell * 2 * SUBLANES) == 0 else 1
    ncl = seq // parts // ell
    rows = bsz * ncl
    sw = S5_TILE_GROUPS * 2 * S5_STATE
    u3 = u.reshape(bsz, seq, S5_CH)
    io = pl.BlockSpec((bsz, seq // parts, LANES), lambda k, s: (0, s, k))
    per_tile = lambda a: pl.BlockSpec((1,) + a.shape[1:], lambda k, s: (k, 0, 0))
    y = pl.pallas_call(
        functools.partial(_s5_kernel, nb=bsz, ncl=ncl),
        grid=(nt, parts),
        in_specs=[io, per_tile(wt), per_tile(ws), per_tile(wc), per_tile(a_t)],
        out_specs=io,
        out_shape=jax.ShapeDtypeStruct((bsz, seq, S5_CH), F32),
        scratch_shapes=[pltpu.VMEM((rows, ell * LANES), F32),
                        pltpu.VMEM((2 * S5_TILE_GROUPS, rows, LANES), F32),
                        pltpu.VMEM((S5_TILE_GROUPS, rows, LANES), F32),
                        pltpu.VMEM((2, bsz, sw), F32)],
        compiler_params=_cparams("parallel", "arbitrary"),
        name="s5_scan",
    )(u3, wt, ws, wc, a_t)
    return y.reshape(bsz * seq, S5_CH)


def _fox_kernel(q_ref, k_ref, vt_ref, o_ref, *, tq, tk):
    i = pl.program_id(2)
    nh = FOX_HPS
    q = [q_ref[:, h * LANES:(h + 1) * LANES] for h in range(nh)]
    key = lax.broadcasted_iota(jnp.int32, (tk, tq), 0)
    qry = lax.broadcasted_iota(jnp.int32, (tk, tq), 1)
    per_q = tq // tk

    def block(j0, carry, mask):
        kj = k_ref[pl.ds(j0, tk), :]
        vtj = vt_ref[:, pl.ds(j0, tk)]
        ss = [_dot_nt(kj[:, h * LANES:(h + 1) * LANES], q[h]) for h in range(nh)]
        stats = []
        for h in range(nh):
            m, l, acc = carry[h]
            s = ss[h] if mask is None else jnp.where(mask, ss[h], -jnp.inf)
            m_new = jnp.maximum(m, jnp.max(s, axis=0, keepdims=True))
            p = jnp.exp2(s - m_new)
            alpha = jnp.exp2(m - m_new)
            stats.append((m_new, alpha * l + jnp.sum(p, axis=0, keepdims=True), alpha, p.astype(BF16)))
        out = []
        for h in range(nh):
            m_new, l, alpha, p = stats[h]
            acc = alpha * carry[h][2] + _bdot(vtj[h * FOX_DH:(h + 1) * FOX_DH, :], p)
            out.append((m_new, l, acc))
        return tuple(out)

    init = tuple((jnp.full((1, tq), -jnp.inf, F32), jnp.zeros((1, tq), F32), jnp.zeros((FOX_DH, tq), F32))
                 for _ in range(nh))
    carry = lax.fori_loop(0, i * per_q, lambda j, c: block(pl.multiple_of(j * tk, tk), c, None), init)
    for d in range(per_q):
        carry = block(pl.multiple_of(i * tq + d * tk, tk), carry, key + d * tk <= qry)
    for g in range(nh // 2):
        o_t = jnp.concatenate([carry[h][2] / carry[h][1] for h in (2 * g, 2 * g + 1)], axis=0)
        o_ref[:, g * LANES:(g + 1) * LANES] = o_t.T.astype(o_ref.dtype)


def _fox(q_aug, k_aug, v, bsz, seq, tq, tk):
    t = v.shape[0]
    nh = FOX_HPS
    groups = FOX_HEADS // nh
    nq = seq // tq
    v_t = v.reshape(bsz, seq, FOX_W).transpose(0, 2, 1).reshape(bsz * FOX_W, seq)
    return pl.pallas_call(
        functools.partial(_fox_kernel, tq=tq, tk=tk),
        grid=(bsz, groups, nq),
        in_specs=[pl.BlockSpec((tq, nh * LANES), lambda b, p, i: (b * nq + i, p)),
                  pl.BlockSpec((seq, nh * LANES), lambda b, p, i: (b, p)),
                  pl.BlockSpec((nh * FOX_DH, seq), lambda b, p, i: (b * groups + p, 0))],
        out_specs=pl.BlockSpec((tq, nh * FOX_DH), lambda b, p, i: (b * nq + i, p)),
        out_shape=jax.ShapeDtypeStruct((t, FOX_W), BF16),
        compiler_params=_cparams("parallel", "parallel", "arbitrary"),
        name="fox_attention",
    )(q_aug, k_aug, v_t)


_ODD_COLS = (("rq", 512), ("rk", 512), ("rv", 512), ("rg", 512), ("gq", 256), ("gk", 256),
             ("gv", 512), ("gr", 512), ("glr", LANES))


def _odd_offsets():
    off, out = 0, {}
    for name, w in _ODD_COLS:
        out[name] = (off, off + w)
        off += w
    return out, off


def _in_odd_kernel(x_ref, sh_ref, sc_ref, g_ref, w_ref, cos_ref, sin_ref, wup_ref, bg_ref,
                   rq_ref, rk_ref, rv_ref, sg_ref, gq_ref, gk_ref, gv_ref, sr_ref, la_ref):
    h = _prenorm(x_ref[...], g_ref, sc_ref, sh_ref)
    z = _bdot(h.astype(BF16), w_ref[...])
    off, _ = _odd_offsets()
    col = lambda n: z[:, off[n][0]:off[n][1]]
    cos, sin = cos_ref[...], sin_ref[...]

    def rope(t, scale):
        heads = []
        for hd in range(RET_HEADS):
            th = t[:, hd * RET_DK:(hd + 1) * RET_DK]
            heads.append((th * cos + pltpu.roll(th, RET_DK // 2, 1) * sin) * scale)
        return jnp.concatenate(heads, axis=1).astype(BF16)

    rq_ref[...] = rope(col("rq"), 1.0)
    rk_ref[...] = rope(col("rk"), RET_DK ** -0.5)
    rv_ref[...] = col("rv").astype(BF16)
    sg_ref[...] = _silu(col("rg"))
    gq_ref[...] = col("gq") * (GLA_DK ** -0.5)
    gk_ref[...] = col("gk")
    gv_ref[...] = col("gv").astype(BF16)
    sr_ref[...] = _silu(col("gr"))
    gate = jnp.dot(col("glr"), wup_ref[...], preferred_element_type=F32, precision=HIGHEST) + bg_ref[...]
    la_ref[...] = _log_sigmoid(gate) * (1.0 / GLA_TAU)


def _in_odd(x2, mod3, gain, w_in, w_up, b_gate, seq, tm):
    t, d = x2.shape
    tps = seq // tm
    ref_w = (512, 512, 512, 512, 256, 256, 512, GLA_RANK, 512)
    starts = [0]
    for wd in ref_w:
        starts.append(starts[-1] + wd)
    seg = lambda j: w_in[:, starts[j]:starts[j + 1]]
    w = jnp.concatenate([seg(0), seg(1), seg(2), seg(3), seg(4), seg(5), seg(6), seg(8),
                         jnp.pad(seg(7), ((0, 0), (0, LANES - GLA_RANK)))], axis=1).astype(BF16)
    wup = jnp.pad(w_up.astype(F32), ((0, LANES - GLA_RANK), (0, 0)))
    bg = b_gate.reshape(1, -1).astype(F32)
    half = RET_DK // 2
    inv = ROPE_BASE ** (-jnp.arange(half, dtype=F32) / half)
    ang = jnp.arange(seq, dtype=F32)[:, None] * inv[None, :]
    cos = jnp.concatenate([jnp.cos(ang), jnp.cos(ang)], axis=1)
    sin = jnp.concatenate([-jnp.sin(ang), jnp.sin(ang)], axis=1)
    tok = lambda n: pl.BlockSpec((tm, n), lambda i: (i, 0))
    full = lambda a: pl.BlockSpec(a.shape, lambda i: (0,) * a.ndim)
    pos = pl.BlockSpec((tm, RET_DK), lambda i: (i % tps, 0))
    widths = (512, 512, 512, 512, 256, 256, 512, 512, 256)
    dtypes = (BF16, BF16, BF16, F32, F32, F32, BF16, F32, F32)
    return pl.pallas_call(
        _in_odd_kernel,
        grid=(t // tm,),
        in_specs=[tok(d), _mod_spec(0, tps), _mod_spec(1, tps), full(gain), full(w), pos, pos,
                  full(wup), full(bg)],
        out_specs=[tok(n) for n in widths],
        out_shape=[jax.ShapeDtypeStruct((t, n), dt) for n, dt in zip(widths, dtypes)],
        compiler_params=_cparams("parallel"),
        name="in_proj_odd",
    )(x2, mod3, mod3, gain, w, cos, sin, wup, bg)


RET_CHUNK = 256


def _ret_kernel(q_ref, k_ref, v_ref, sg_ref, dm_ref, xi_ref, zeta_ref, gl_ref, y_ref, st_ref):
    @pl.when(pl.program_id(1) == 0)
    def _():
        st_ref[...] = jnp.zeros_like(st_ref)

    heads = range(RET_HEADS)
    col = lambda h: slice(h * RET_DK, (h + 1) * RET_DK)
    q = [q_ref[:, col(h)] for h in heads]
    k = [k_ref[:, col(h)] for h in heads]
    v = [v_ref[:, col(h)] for h in heads]
    st = [st_ref[h] for h in heads]
    s = [_dot_nt(q[h], k[h]) for h in heads]
    inter = [_bdot((q[h].astype(F32) * xi_ref[h]).astype(BF16), st[h].astype(BF16)) for h in heads]
    upd = [_dot_tn((k[h].astype(F32) * zeta_ref[h]).astype(BF16), v[h]) for h in heads]
    for h in heads:
        o = _bdot((s[h] * dm_ref[h]).astype(BF16), v[h]) + inter[h]
        st_ref[h] = gl_ref[h, 0:1, :] * st[h] + upd[h]
        y_ref[:, col(h)] = (sg_ref[:, col(h)] * _rms(o)).astype(y_ref.dtype)


def _retention(rq, rk, rv, sg, bsz, seq):
    t = rq.shape[0]
    ell = min(RET_CHUNK, seq)
    nc = seq // ell
    log_g = jnp.log(1.0 - jnp.exp2(-5.0 - jnp.arange(RET_HEADS, dtype=F32)))
    idx = jnp.arange(ell, dtype=F32)
    rel = idx[:, None] - idx[None, :]
    dmat = jnp.where(rel >= 0, jnp.exp(log_g[:, None, None] * jnp.maximum(rel, 0.0)), 0.0)
    lanes = lambda a: jnp.broadcast_to(a[..., None], a.shape + (RET_DK,))
    xi = lanes(jnp.exp(log_g[:, None] * (idx + 1.0)))
    zeta = lanes(jnp.exp(log_g[:, None] * (ell - 1.0 - idx)))
    gl = jnp.broadcast_to(jnp.exp(log_g * ell)[:, None, None], (RET_HEADS, SUBLANES, RET_DV))
    blk = pl.BlockSpec((ell, RET_HEADS * RET_DK), lambda b, c: (b * nc + c, 0))
    full = lambda a: pl.BlockSpec(a.shape, lambda b, c: (0, 0, 0))
    return pl.pallas_call(
        _ret_kernel,
        grid=(bsz, nc),
        in_specs=[blk, blk, blk, blk, full(dmat), full(xi), full(zeta), full(gl)],
        out_specs=blk,
        out_shape=jax.ShapeDtypeStruct((t, RET_HEADS * RET_DV), BF16),
        scratch_shapes=[pltpu.VMEM((RET_HEADS, RET_DK, RET_DV), F32)],
        compiler_params=_cparams("parallel", "arbitrary"),
        name="retention",
    )(rq, rk, rv, sg, dmat, xi, zeta, gl)


def _gla_kernel(q_ref, k_ref, la_ref, v_ref, sg_ref, y_ref, st_ref, b_scr, v_scr, p_scr, r_scr):
    @pl.when(pl.program_id(1) == 0)
    def _():
        st_ref[...] = jnp.zeros_like(st_ref)

    ell, sub = GLA_CHUNK, GLA_SUB
    n_sub = ell // sub
    nb = q_ref.shape[0]
    pairs = GLA_HEADS // 2
    streams = [(bl, p) for bl in range(nb) for p in range(pairs)]
    lane = lax.broadcasted_iota(jnp.int32, (1, LANES), 1)
    first = lane < GLA_DK
    head = (first, jnp.logical_not(first))
    pick = lambda h, a: jnp.where(head[h], a, 0.0).astype(BF16)
    tri = _lower_tri(ell)
    tau = lax.broadcasted_iota(jnp.int32, (sub, LANES), 0)
    rsub = lax.broadcasted_iota(jnp.int32, (LANES, 2 * LANES), 0)
    csub = lax.broadcasted_iota(jnp.int32, (LANES, 2 * LANES), 1)
    ind = ((rsub < GLA_DK) == (csub < LANES)).astype(BF16)

    val = {}
    for sid, (bl, p) in enumerate(streams):
        qk = slice(p * LANES, (p + 1) * LANES)
        q, k = q_ref[bl, :, qk], k_ref[bl, :, qk]
        b = _dot01(tri, la_ref[bl, :, qk])
        b_scr[sid] = b
        v_bf = v_ref[bl, :, p * 2 * GLA_DV:(p + 1) * 2 * GLA_DV]
        v_scr[sid] = v_bf.astype(F32)
        st = st_ref[sid]
        val[sid] = dict(q=q, k=k, b=b, st=st, vh=[v_bf[:, h * GLA_DV:(h + 1) * GLA_DV] for h in range(2)])

    for sid in val:
        d = val[sid]
        qe = d["q"] * jnp.exp(d["b"])
        st_bf = d["st"].astype(BF16)
        d["o"] = [_dot_nt(pick(h, qe), st_bf) for h in range(2)]

    row = lax.broadcasted_iota(jnp.int32, (ell, LANES), 0)
    for sid in val:
        d = val[sid]
        q, k, b = d["q"], d["k"], d["b"]
        qa, ka = [], []
        for i in range(1, n_sub):
            lo = i * sub
            ref_row = b[lo - 1:lo, :]
            in_i = (row >= lo) & (row < lo + sub)
            qa.append(jnp.where(in_i, q * jnp.exp(jnp.minimum(b - ref_row, 0.0)), 0.0))
            ka.append(jnp.where(row < lo, k * jnp.exp(jnp.minimum(ref_row - b, 0.0)), 0.0))
        k_cat = jnp.concatenate(ka, axis=1).astype(BF16)
        d["a_off"] = [_dot_nt(jnp.concatenate([pick(h, x) for x in qa], axis=1), k_cat) for h in range(2)]
    for sid in val:
        d = val[sid]
        d["off"] = [_bdot(d["a_off"][h].astype(BF16), d["vh"][h]) for h in range(2)]

    for sid, (bl, p) in enumerate(streams):
        d = val[sid]
        q, b = d["q"], d["b"]
        for i in range(n_sub):
            lo = i * sub
            qi, bi = q[lo:lo + sub], b[lo:lo + sub]
            for s in range(sub):
                k_row = k_ref[bl, pl.ds(lo + s, 1), p * LANES:(p + 1) * LANES]
                b_row = b_scr[sid, pl.ds(lo + s, 1), :]
                w = jnp.exp(jnp.minimum(bi - b_row, 0.0))
                p_scr[sid, pl.ds((lo + s) * sub, sub), :] = jnp.where(tau >= s, qi * k_row * w, 0.0).astype(BF16)
    for sid in val:
        r_scr[sid] = _bdot(p_scr[sid], ind)
    for sid in val:
        diag = [[], []]
        for i in range(n_sub):
            lo = i * sub
            for h in range(2):
                acc = jnp.zeros((sub, GLA_DV), F32)
                for s in range(sub):
                    v_row = v_scr[sid, pl.ds(lo + s, 1), h * GLA_DV:(h + 1) * GLA_DV]
                    acc = acc + r_scr[sid, pl.ds((lo + s) * sub, sub), h * LANES:(h + 1) * LANES] * v_row
                diag[h].append(acc)
        val[sid]["diag"] = diag

    for sid, (bl, p) in enumerate(streams):
        d = val[sid]
        b_last = d["b"][ell - 1:ell, :]
        kh = (d["k"] * jnp.exp(b_last - d["b"])).astype(BF16)
        upd = [_dot_tn(d["vh"][h], kh) for h in range(2)]
        st_ref[sid] = d["st"] * jnp.exp(b_last) + jnp.where(first, upd[0], upd[1])
        for h in range(2):
            oh = d["o"][h] + d["off"][h] + jnp.concatenate(d["diag"][h], axis=0)
            cols = slice((2 * p + h) * GLA_DV, (2 * p + h + 1) * GLA_DV)
            y_ref[bl, :, cols] = (sg_ref[bl, :, cols] * _rms(oh)).astype(y_ref.dtype)


GLA_BATCHES = 2


def _gla(gq, gk, la, gv, sr, bsz, seq):
    t = gq.shape[0]
    ell = GLA_CHUNK
    nc = seq // ell
    nb = min(GLA_BATCHES, bsz)
    ns = nb * (GLA_HEADS // 2)
    r3 = lambda a: a.reshape(bsz, seq, a.shape[1])
    spec = lambda w: pl.BlockSpec((nb, ell, w), lambda b, c: (b, c, 0))
    wq, wv = GLA_HEADS * GLA_DK, GLA_HEADS * GLA_DV
    y = pl.pallas_call(
        _gla_kernel,
        grid=(bsz // nb, nc),
        in_specs=[spec(wq), spec(wq), spec(wq), spec(wv), spec(wv)],
        out_specs=spec(wv),
        out_shape=jax.ShapeDtypeStruct((bsz, seq, wv), BF16),
        scratch_shapes=[pltpu.VMEM((ns, GLA_DV, LANES), F32),
                        pltpu.VMEM((ns, ell, LANES), F32),
                        pltpu.VMEM((ns, ell, 2 * GLA_DV), F32),
                        pltpu.VMEM((ns, ell * GLA_SUB, LANES), BF16),
                        pltpu.VMEM((ns, ell * GLA_SUB, 2 * LANES), F32)],
        compiler_params=_cparams("parallel", "arbitrary"),
        name="gla",
    )(r3(gq), r3(gk), r3(la), r3(gv), r3(sr))
    return y.reshape(t, wv)


def _post_tail(m, x_ref, g1_ref, gpost_ref, sh2_ref, sc2_ref, gpre_ref, rw_ref, rb_ref,
               x1_ref, h2_ref, topi_ref, gate_ref, rank_ref, cnt_ref, carry_ref):
    i = pl.program_id(0)
    tm = m.shape[0]
    x1 = x_ref[...] + g1_ref[0] * (_rms(m) * gpost_ref[...])
    x1_ref[...] = x1
    h2 = _rms(x1) * (gpre_ref[...] * (1.0 + sc2_ref[0])) + sh2_ref[0]
    for c in range(D_MODEL // LANES):
        h2_ref[pl.ds(c, tm, stride=SUBLANES), :] = h2[:, c * LANES:(c + 1) * LANES]
    h_hi = h2.astype(BF16)
    h_lo = (h2 - h_hi.astype(F32)).astype(BF16)
    w_hi, w_lo = rw_ref[0], rw_ref[1]
    logits = _bdot(h_hi, w_hi) + _bdot(h_lo, w_hi) + _bdot(h_hi, w_lo) + rb_ref[...]

    lane = lax.broadcasted_iota(jnp.int32, (tm, LANES), 1)
    lanef = lane.astype(F32)
    work = logits
    topv = jnp.full((tm, LANES), -jnp.inf, F32)
    topi = jnp.zeros((tm, LANES), F32)
    onehot = jnp.zeros((tm, LANES), F32)
    hits = []
    for k in range(TOP_K):
        mx = jnp.max(work, axis=-1, keepdims=True)
        idx = jnp.min(jnp.where(work == mx, lanef, float(LANES)), axis=-1, keepdims=True)
        hit = lanef == idx
        hits.append(hit)
        topv = jnp.where(lane == k, mx, topv)
        topi = jnp.where(lane == k, idx, topi)
        onehot = onehot + hit.astype(F32)
        work = jnp.where(hit, -jnp.inf, work)
    e = jnp.exp(topv - jnp.max(topv, axis=-1, keepdims=True))
    gate_ref[...] = e / jnp.sum(e, axis=-1, keepdims=True)
    topi_ref[...] = topi.astype(jnp.int32)

    @pl.when(i == 0)
    def _():
        carry_ref[...] = jnp.zeros_like(carry_ref)

    before = _bdot(_lower_tri(tm, strict=True), onehot.astype(BF16)) + carry_ref[...]
    rank = jnp.zeros((tm, LANES), F32)
    for k in range(TOP_K):
        rk = jnp.sum(jnp.where(hits[k], before, 0.0), axis=-1, keepdims=True)
        rank = jnp.where(lane == k, rk, rank)
    rank_ref[...] = rank.astype(jnp.int32)
    total = before[tm - 1:tm, :] + onehot[tm - 1:tm, :]
    carry_ref[...] = total
    cnt_ref[...] = jnp.broadcast_to(total, cnt_ref.shape)


def _out_even_kernel(ys_ref, u_ref, yb_ref, d_ref, gw_ref, gb_ref, wa_ref, wb_ref, *rest):
    y = ys_ref[...] + d_ref[...] * u_ref[...]
    g = jax.nn.gelu(y)
    ya = g * _sigmoid(_bdot(g.astype(BF16), gw_ref[...]) + gb_ref[...])
    m = _bdot(ya.astype(BF16), wa_ref[...]) + _bdot(yb_ref[...], wb_ref[...])
    _post_tail(m, *rest)


def _out_odd_kernel(yc_ref, yd_ref, wa_ref, wb_ref, *rest):
    m = _bdot(yc_ref[...], wa_ref[...]) + _bdot(yd_ref[...], wb_ref[...])
    _post_tail(m, *rest)


def _mixer_out(body, mix_args, mix_specs, x2, mod3, g_post, g_pre, router_w, router_b, seq, tm):
    t, d = x2.shape
    tps = seq // tm
    rw32 = jnp.pad(router_w.astype(F32), ((0, 0), (0, LANES - N_EXPERTS)))
    rw_hi = rw32.astype(BF16)
    rw = jnp.stack([rw_hi, (rw32 - rw_hi.astype(F32)).astype(BF16)])
    rb = jnp.pad(router_b.astype(F32), (0, LANES - N_EXPERTS), constant_values=-1e30).reshape(1, LANES)
    tok = lambda n: pl.BlockSpec((tm, n), lambda i: (i, 0))
    full = lambda a: pl.BlockSpec(a.shape, lambda i: (0,) * a.ndim)
    tail_args = [x2, mod3, g_post, mod3, mod3, g_pre, rw, rb]
    tail_specs = [tok(d), _mod_spec(2, tps), full(g_post), _mod_spec(3, tps), _mod_spec(4, tps),
                  full(g_pre), full(rw), full(rb)]
    return pl.pallas_call(
        body,
        grid=(t // tm,),
        in_specs=mix_specs + tail_specs,
        out_specs=[tok(d), pl.BlockSpec((tm * SUBLANES, LANES), lambda i: (i, 0)),
                   tok(LANES), tok(LANES), tok(LANES), pl.BlockSpec((SUBLANES, LANES), lambda i: (0, 0))],
        out_shape=[jax.ShapeDtypeStruct((t, d), F32),
                   jax.ShapeDtypeStruct((t * SUBLANES, LANES), F32),
                   jax.ShapeDtypeStruct((t, LANES), jnp.int32),
                   jax.ShapeDtypeStruct((t, LANES), F32),
                   jax.ShapeDtypeStruct((t, LANES), jnp.int32),
                   jax.ShapeDtypeStruct((SUBLANES, LANES), F32)],
        scratch_shapes=[pltpu.VMEM((1, LANES), F32)],
        compiler_params=_cparams("arbitrary"),
        name="mixer_out_router",
    )(*mix_args, *tail_args)


def _out_even(ys, u, yb, d_skip, glu_w, glu_b, w_out, *tail, seq, tm):
    tok = lambda n: pl.BlockSpec((tm, n), lambda i: (i, 0))
    full = lambda a: pl.BlockSpec(a.shape, lambda i: (0,) * a.ndim)
    args = [ys, u, yb, d_skip.reshape(1, -1), glu_w.astype(BF16), glu_b.reshape(1, -1),
            w_out[:S5_CH].astype(BF16), w_out[S5_CH:].astype(BF16)]
    specs = [tok(S5_CH), tok(S5_CH), tok(FOX_W)] + [full(a) for a in args[3:]]
    return _mixer_out(_out_even_kernel, args, specs, *tail, seq, tm)


def _out_odd(yc, yd, w_out, *tail, seq, tm):
    tok = lambda n: pl.BlockSpec((tm, n), lambda i: (i, 0))
    full = lambda a: pl.BlockSpec(a.shape, lambda i: (0,) * a.ndim)
    nc = yc.shape[1]
    args = [yc, yd, w_out[:nc].astype(BF16), w_out[nc:].astype(BF16)]
    specs = [tok(nc), tok(yd.shape[1])] + [full(a) for a in args[2:]]
    return _mixer_out(_out_odd_kernel, args, specs, *tail, seq, tm)


def _route_kernel(topi_ref, rank_ref, cnt_ref, dest_ref, blk_ref, meta_ref):
    tm = topi_ref.shape[0]
    cnt = cnt_ref[...]
    padded = jnp.floor((cnt + (MOE_BLOCK - 1.0)) * (1.0 / MOE_BLOCK)) * MOE_BLOCK
    r = lax.broadcasted_iota(jnp.int32, (LANES, LANES), 0)
    c = lax.broadcasted_iota(jnp.int32, (LANES, LANES), 1)
    hi, mid, lo = _split3(padded)
    incl = (r <= c).astype(BF16)
    pad_end = _bdot(hi, incl) + _bdot(mid, incl) + _bdot(lo, incl)
    pad_start = pad_end - padded
    lane = lax.broadcasted_iota(jnp.int32, (tm, LANES), 1)
    lanef = lane.astype(F32)
    topi = topi_ref[...].astype(F32)
    start_row = pad_start[0:1, :]
    dest = jnp.zeros((tm, LANES), F32)
    for k in range(TOP_K):
        idx = jnp.sum(jnp.where(lane == k, topi, 0.0), axis=-1, keepdims=True)
        st = jnp.sum(jnp.where(lanef == idx, start_row, 0.0), axis=-1, keepdims=True)
        dest = jnp.where(lane == k, st, dest)
    dest_ref[...] = dest.astype(jnp.int32) + rank_ref[...]

    nb = blk_ref.shape[1]
    end_col = jnp.sum(jnp.where(r == c, jnp.broadcast_to(pad_end[0:1, :], (LANES, LANES)), 0.0),
                      axis=-1, keepdims=True)
    jpos = lax.broadcasted_iota(jnp.int32, (LANES, nb), 1).astype(F32) * MOE_BLOCK
    esub = lax.broadcasted_iota(jnp.int32, (LANES, nb), 0)
    done = jnp.where((end_col <= jpos) & (esub < N_EXPERTS), 1.0, 0.0)
    be = jnp.minimum(jnp.sum(done, axis=0, keepdims=True), N_EXPERTS - 1.0)
    blk_ref[...] = jnp.broadcast_to(be, blk_ref.shape).astype(jnp.int32)
    lane1 = lax.broadcasted_iota(jnp.int32, (SUBLANES, LANES), 1)
    n_valid = jnp.sum(jnp.where(lane1 == N_EXPERTS - 1, pad_end, 0.0), axis=-1, keepdims=True) * (1.0 / MOE_BLOCK)
    sub1 = lax.broadcasted_iota(jnp.int32, (SUBLANES, LANES), 0)
    meta = jnp.where(sub1 == 0, pad_start + cnt, jnp.where(sub1 == 1, pad_end, jnp.broadcast_to(n_valid, (SUBLANES, LANES))))
    meta_ref[...] = meta.astype(jnp.int32)


def _route(topi, rank, cnt, n_blocks, tm):
    t = topi.shape[0]
    nb_pad = -(-n_blocks // LANES) * LANES
    tok = pl.BlockSpec((tm, LANES), lambda i: (i, 0))
    fix = lambda n: pl.BlockSpec((SUBLANES, n), lambda i: (0, 0))
    return pl.pallas_call(
        _route_kernel,
        grid=(t // tm,),
        in_specs=[tok, tok, fix(LANES)],
        out_specs=[tok, fix(nb_pad), fix(LANES)],
        out_shape=[jax.ShapeDtypeStruct((t, LANES), jnp.int32),
                   jax.ShapeDtypeStruct((SUBLANES, nb_pad), jnp.int32),
                   jax.ShapeDtypeStruct((SUBLANES, LANES), jnp.int32)],
        compiler_params=_cparams("arbitrary"),
        name="route_plan",
    )(topi, rank, cnt)


def _dispatch_kernel(pad_ref, dest_ref, h_ref, xb_ref, zero_ref, sem_z, sem_s):
    i = pl.program_id(0)
    tm = h_ref.shape[0]

    @pl.when(i == 0)
    def _():
        zero_ref[...] = jnp.zeros_like(zero_ref)
        sizes = [1 << b for b in range(int(math.log2(MOE_BLOCK)) - 1, -1, -1)]

        def fill(e, carry, do_wait):
            start = pad_ref[0, e]
            n_pad = pad_ref[1, e] - start
            off = start
            for sz in sizes:
                take = (n_pad & sz) != 0
                cp = pltpu.make_async_copy(zero_ref.at[pl.ds(0, sz)], xb_ref.at[pl.ds(off, sz)], sem_z)

                @pl.when(take)
                def _():
                    if do_wait:
                        cp.wait()
                    else:
                        cp.start()
                off = off + jnp.where(take, sz, 0)
            return carry

        half = zero_ref.shape[0]

        def fill_unused(j, carry, do_wait):
            for part in range(MOE_BLOCK // half):
                cp = pltpu.make_async_copy(zero_ref, xb_ref.at[pl.ds(j * MOE_BLOCK + part * half, half)], sem_z)
                if do_wait:
                    cp.wait()
                else:
                    cp.start()
            return carry

        n_blocks = xb_ref.shape[0] // MOE_BLOCK
        lax.fori_loop(0, N_EXPERTS, lambda e, c: fill(e, c, False), 0)
        lax.fori_loop(pad_ref[2, 0], n_blocks, lambda j, c: fill_unused(j, c, False), 0)
        lax.fori_loop(0, N_EXPERTS, lambda e, c: fill(e, c, True), 0)
        lax.fori_loop(pad_ref[2, 0], n_blocks, lambda j, c: fill_unused(j, c, True), 0)

    def issue(r, carry):
        for k in range(TOP_K):
            pltpu.make_async_copy(h_ref.at[r], xb_ref.at[dest_ref[r * TOP_K + k]], sem_s).start(priority=k % 2)
        return carry

    lax.fori_loop(0, tm, issue, 0)
    for k in range(TOP_K):
        pltpu.make_async_copy(h_ref, xb_ref.at[pl.ds(0, tm)], sem_s).wait()


def _dispatch(h2t, dest, meta, n_slots, tm):
    t = h2t.shape[0] // SUBLANES
    h3 = h2t.reshape(t, SUBLANES, LANES)
    return pl.pallas_call(
        _dispatch_kernel,
        grid_spec=pltpu.PrefetchScalarGridSpec(
            num_scalar_prefetch=1,
            grid=(t // tm,),
            in_specs=[pl.BlockSpec((tm * TOP_K,), lambda i, p: (i,), memory_space=pltpu.SMEM),
                      pl.BlockSpec((tm, SUBLANES, LANES), lambda i, p: (i, 0, 0))],
            out_specs=pl.BlockSpec(memory_space=pl.ANY),
            scratch_shapes=[pltpu.VMEM((MOE_BLOCK // 2, SUBLANES, LANES), F32),
                            pltpu.SemaphoreType.DMA, pltpu.SemaphoreType.DMA]),
        out_shape=jax.ShapeDtypeStruct((n_slots, SUBLANES, LANES), F32),
        compiler_params=_cparams("arbitrary"),
        name="moe_dispatch",
    )(meta[:3, :N_EXPERTS], dest, h3)


def _expert_kernel(be_ref, nv_ref, nxt_ref, x_ref, wgu_hbm, bgu_ref, wd_hbm, bd_ref, y_ref,
                   wgu_f32, wd_f32, wgu_bf, wd_bf, sem, *, layer):
    j = pl.program_id(0)
    valid = j < nv_ref[0]
    first = valid & ((j == 0) | (be_ref[j] != be_ref[jnp.maximum(j - 1, 0)]))

    def weight_copies(e):
        return (pltpu.make_async_copy(wgu_hbm.at[layer, e], wgu_f32, sem.at[0]),
                pltpu.make_async_copy(wd_hbm.at[layer, e], wd_f32, sem.at[1]))

    @pl.when(j == 0)
    def _():
        for cp in weight_copies(be_ref[0]):
            cp.start()

    @pl.when(first)
    def _():
        for cp in weight_copies(be_ref[j]):
            cp.wait()
        wgu_bf[...] = wgu_f32[...].astype(BF16)
        wd_bf[...] = wd_f32[...].astype(BF16)

        @pl.when(nxt_ref[j] >= 0)
        def _():
            for cp in weight_copies(nxt_ref[j]):
                cp.start()

    @pl.when(valid)
    def _():
        x = jnp.concatenate([x_ref[pl.ds(c, MOE_BLOCK, stride=SUBLANES), :] for c in range(D_MODEL // LANES)],
                            axis=1).astype(BF16)
        gu = _bdot(x, wgu_bf[...]) + bgu_ref[0]
        x_glu = jnp.minimum(gu[:, :D_EXPERT], SWIGLU_LIMIT)
        x_lin = jnp.clip(gu[:, D_EXPERT:], -SWIGLU_LIMIT, SWIGLU_LIMIT)
        act = x_glu * _sigmoid(SWIGLU_ALPHA * x_glu) * (x_lin + 1.0)
        y = _bdot(act.astype(BF16), wd_bf[...]) + bd_ref[0]
        for c in range(D_MODEL // LANES):
            y_ref[pl.ds(c, MOE_BLOCK, stride=SUBLANES), :] = y[:, c * LANES:(c + 1) * LANES]

    @pl.when(jnp.logical_not(valid))
    def _():
        y_ref[...] = jnp.zeros_like(y_ref)


def _experts(xb, block_expert, n_valid, w_gu, b_gu, w_down, b_down, layer):
    n_slots = xb.shape[0]
    n_blocks = n_slots // MOE_BLOCK
    rows = MOE_BLOCK * SUBLANES
    x2 = xb.reshape(n_slots * SUBLANES, LANES)
    depth, ne, d, de2 = w_gu.shape
    idx = jnp.arange(n_blocks, dtype=jnp.int32)
    is_first = ((idx == 0) | (block_expert != jnp.roll(block_expert, 1))) & (idx < n_valid[0])
    first_at = lax.cummin(jnp.where(is_first, idx, n_blocks)[::-1])[::-1]
    next_first = jnp.concatenate([first_at[1:], jnp.full((1,), n_blocks, jnp.int32)])
    nxt = jnp.where(next_first < n_blocks, block_expert[jnp.minimum(next_first, n_blocks - 1)], -1)
    last = lambda j, be, nv, nx: jnp.minimum(j, nv[0] - 1)
    bmap = lambda j, be, nv, nx: (layer, be[last(j, be, nv, nx)], 0, 0)
    return pl.pallas_call(
        functools.partial(_expert_kernel, layer=layer),
        grid_spec=pltpu.PrefetchScalarGridSpec(
            num_scalar_prefetch=3,
            grid=(n_blocks,),
            in_specs=[pl.BlockSpec((rows, LANES), lambda j, be, nv, nx: (last(j, be, nv, nx), 0)),
                      pl.BlockSpec(memory_space=pl.ANY),
                      pl.BlockSpec((None, 1, 1, de2), bmap),
                      pl.BlockSpec(memory_space=pl.ANY),
                      pl.BlockSpec((None, 1, 1, d), bmap)],
            out_specs=pl.BlockSpec((rows, LANES), lambda j, be, nv, nx: (j, 0)),
            scratch_shapes=[pltpu.VMEM((d, de2), F32), pltpu.VMEM((de2 // 2, d), F32),
                            pltpu.VMEM((d, de2), BF16), pltpu.VMEM((de2 // 2, d), BF16),
                            pltpu.SemaphoreType.DMA((2,))]),
        out_shape=jax.ShapeDtypeStruct((n_slots * SUBLANES, LANES), F32),
        compiler_params=_cparams("arbitrary"),
        name="moe_experts",
    )(block_expert, n_valid, nxt.astype(jnp.int32), x2, w_gu, b_gu.reshape(depth, ne, 1, de2), w_down,
      b_down.reshape(depth, ne, 1, d))


def _combine_kernel(dest_ref, dest_next_ref, yb_ref, gate_ref, x1_ref, g2_ref, gpost_ref, o_ref, buf, sem):
    i = pl.program_id(0)
    tm = x1_ref.shape[0]
    slot = i % 2

    def gather(idx_ref, into):
        def issue(r, carry):
            for k in range(TOP_K):
                src = pl.multiple_of(idx_ref[r * TOP_K + k] * SUBLANES, SUBLANES)
                dst = pl.multiple_of((k * tm + r) * SUBLANES, SUBLANES)
                pltpu.make_async_copy(yb_ref.at[pl.ds(src, SUBLANES), :], buf.at[into, pl.ds(dst, SUBLANES), :],
                                      sem.at[into]).start(priority=k % 2)
            return carry
        lax.fori_loop(0, tm, issue, 0)

    @pl.when(i == 0)
    def _():
        gather(dest_ref, 0)

    @pl.when(i + 1 < pl.num_programs(0))
    def _():
        gather(dest_next_ref, 1 - slot)

    pltpu.make_async_copy(yb_ref.at[pl.ds(0, TOP_K * tm * SUBLANES), :], buf.at[slot], sem.at[slot]).wait()
    gates = gate_ref[...]
    gk = [jnp.broadcast_to(gates[:, k:k + 1], (tm, LANES)) for k in range(TOP_K)]
    b2 = buf.at[slot]
    cols = []
    for c in range(D_MODEL // LANES):
        acc = jnp.zeros((tm, LANES), F32)
        for k in range(TOP_K):
            acc = acc + gk[k] * b2[pl.ds(k * tm * SUBLANES + c, tm, stride=SUBLANES), :]
        cols.append(acc)
    f = jnp.concatenate(cols, axis=1)
    o_ref[...] = x1_ref[...] + g2_ref[0] * (_rms(f) * gpost_ref[...])


def _combine(yb, dest, gates, x1, mod3, g_post, seq, tm):
    t, d = x1.shape
    tps = seq // tm
    n = t // tm
    return pl.pallas_call(
        _combine_kernel,
        grid=(n,),
        in_specs=[pl.BlockSpec((tm * TOP_K,), lambda i: (i,), memory_space=pltpu.SMEM),
                  pl.BlockSpec((tm * TOP_K,), lambda i: (jnp.minimum(i + 1, n - 1),), memory_space=pltpu.SMEM),
                  pl.BlockSpec(memory_space=pl.ANY),
                  pl.BlockSpec((tm, LANES), lambda i: (i, 0)),
                  pl.BlockSpec((tm, d), lambda i: (i, 0)),
                  _mod_spec(5, tps),
                  pl.BlockSpec(g_post.shape, lambda i: (0, 0))],
        out_specs=pl.BlockSpec((tm, d), lambda i: (i, 0)),
        out_shape=jax.ShapeDtypeStruct((t, d), F32),
        scratch_shapes=[pltpu.VMEM((2, TOP_K * tm * SUBLANES, LANES), F32), pltpu.SemaphoreType.DMA((2,))],
        compiler_params=_cparams("arbitrary"),
        name="moe_combine",
    )(dest, dest, yb, gates, x1, mod3, g_post)


def _moe(h2t, topi, gates, rank, cnt, x1, mod3, g_post, w_gu, b_gu, w_down, b_down, layer, seq):
    t = x1.shape[0]
    n_blocks = t * TOP_K // MOE_BLOCK + N_EXPERTS
    dest_l, blk, meta = _route(topi, rank, cnt, n_blocks, min(1024, t))
    dest = dest_l[:, :TOP_K].reshape(t * TOP_K)
    xb = _dispatch(h2t, dest, meta, n_blocks * MOE_BLOCK, MOE_BLOCK)
    yb = _experts(xb, blk[0, :n_blocks], meta[2, :1], w_gu, b_gu, w_down, b_down, layer)
    return _combine(yb, dest, gates, x1, mod3, g_post, seq, MOE_BLOCK)


TOKEN_TILE = 512
FOX_Q_TILE = 256
FOX_K_TILE = 256


def kernel(x, c, ada_w, ada_b, norm_pre_mix, norm_post_mix, norm_pre_ffn, norm_post_ffn, ev_w_in, fox_b_f, s5_lam_re, s5_lam_im, s5_log_dt, s5_b_re, s5_b_im, s5_c_re, s5_c_im, s5_d, s5_glu_w, s5_glu_b, ev_w_out, od_w_in, gla_w_up, gla_b_gate, od_w_out, router_w, router_b, exp_w_gu, exp_b_gu, exp_w_down, exp_b_down):
    bsz, seq, d = x.shape
    t = bsz * seq
    tm = min(TOKEN_TILE, seq)
    x2 = x.reshape(t, d)
    mod = _modulation(c, ada_w, ada_b)
    for l in range(DEPTH):
        i = l // 2
        mod3 = mod[l].reshape(bsz, 1, 6 * d)
        row = lambda a: a[l].reshape(1, -1)
        tail = (x2, mod3, row(norm_post_mix), row(norm_pre_ffn), router_w[l], router_b[l])
        if l % 2 == 0:
            u, q, k, v = _in_even(x2, mod3, row(norm_pre_mix), ev_w_in[i], fox_b_f[i], seq, tm)
            tables = _s5_tables(s5_lam_re[i], s5_lam_im[i], s5_log_dt[i], s5_b_re[i], s5_b_im[i],
                                s5_c_re[i], s5_c_im[i])
            ys = _s5_scan(u, bsz, seq, tables)
            yb = _fox(q, k, v, bsz, seq, min(FOX_Q_TILE, seq), min(FOX_K_TILE, seq))
            outs = _out_even(ys, u, yb, s5_d[i], s5_glu_w[i], s5_glu_b[i], ev_w_out[i], *tail, seq=seq, tm=tm)
        else:
            rq, rk, rv, sg, gq, gk, gv, sr, la = _in_odd(x2, mod3, row(norm_pre_mix), od_w_in[i],
                                                         gla_w_up[i], gla_b_gate[i], seq, tm)
            yc = _retention(rq, rk, rv, sg, bsz, seq)
            yd = _gla(gq, gk, la, gv, sr, bsz, seq)
            outs = _out_odd(yc, yd, od_w_out[i], *tail, seq=seq, tm=tm)
        x1, h2t, topi, gates, rank, cnt = outs
        x2 = _moe(h2t, topi, gates, rank, cnt, x1, mod3, row(norm_post_ffn),
                  exp_w_gu, exp_b_gu, exp_w_down, exp_b_down, l, seq)
    return x2.reshape(bsz, seq, d)
```

```python
import functools
import math

import jax
import jax.numpy as jnp
from jax import lax
from jax.experimental import pallas as pl
from jax.experimental.pallas import tpu as pltpu

F32 = jnp.float32
BF16 = jnp.bfloat16
HIGHEST = lax.Precision.HIGHEST

D_MODEL = 1024
DEPTH = 2
EPS = 1e-6
S5_CH = 512
S5_GROUP = 16
S5_GROUPS = S5_CH // S5_GROUP
S5_STATE = 64
S5_CHUNK = 8
FOX_HEADS = 8
FOX_DH = 64
FOX_W = FOX_HEADS * FOX_DH
LOG2_E = 1.4426950408889634
FOX_HPS = 8
RET_HEADS = 4
RET_DK = 128
RET_DV = 128
ROPE_BASE = 10000.0
GLA_HEADS = 4
GLA_DK = 64
GLA_DV = 128
GLA_RANK = 16
GLA_TAU = 16.0
GLA_CHUNK = 64
GLA_SUB = 16
N_EXPERTS = 32
TOP_K = 4
D_EXPERT = 1024
SWIGLU_LIMIT = 7.0
SWIGLU_ALPHA = 1.702
MOE_BLOCK = 256
ROW_DMA_UNROLL = 4

LANES = 128
SUBLANES = 8
VMEM_LIMIT = 56 * 1024 * 1024


def _cparams(*sem):
    return pltpu.CompilerParams(dimension_semantics=sem, vmem_limit_bytes=VMEM_LIMIT)


def _bdot(a, b):
    return jnp.dot(a, b, preferred_element_type=F32)


def _dot_nt(a, b):
    return lax.dot_general(a, b, (((1,), (1,)), ((), ())), preferred_element_type=F32)


def _dot_tn(a, b):
    return lax.dot_general(a, b, (((0,), (0,)), ((), ())), preferred_element_type=F32)


def _split3(x):
    hi = x.astype(BF16)
    r = x - hi.astype(F32)
    mid = r.astype(BF16)
    lo = (r - mid.astype(F32)).astype(BF16)
    return hi, mid, lo


def _dot01(m01, x):
    hi, mid, lo = _split3(x)
    return _bdot(m01, hi) + _bdot(m01, mid) + _bdot(m01, lo)


def _lower_tri(n, strict=False):
    r = lax.broadcasted_iota(jnp.int32, (n, n), 0)
    c = lax.broadcasted_iota(jnp.int32, (n, n), 1)
    return ((r > c) if strict else (r >= c)).astype(BF16)


def _log_sigmoid(x):
    return jnp.minimum(x, 0.0) - jnp.log1p(jnp.exp(-jnp.abs(x)))


def _sigmoid(x):
    return 1.0 / (1.0 + jnp.exp(-x))


def _silu(x):
    return x * _sigmoid(x)


def _rms(x):
    return x * lax.rsqrt(jnp.mean(x * x, axis=-1, keepdims=True) + EPS)


def _mod_kernel(c_ref, w_ref, b_ref, o_ref):
    c = c_ref[...]
    o_ref[0] = jnp.dot(_silu(c), w_ref[0], preferred_element_type=F32, precision=HIGHEST) + b_ref[0]


def _modulation(c, ada_w, ada_b):
    depth, d, n = ada_w.shape
    bsz = c.shape[0]
    tn = D_MODEL
    return pl.pallas_call(
        _mod_kernel,
        grid=(depth, n // tn),
        in_specs=[pl.BlockSpec((bsz, d), lambda l, j: (0, 0)),
                  pl.BlockSpec((1, d, tn), lambda l, j: (l, 0, j)),
                  pl.BlockSpec((1, 1, tn), lambda l, j: (l, 0, j))],
        out_specs=pl.BlockSpec((1, bsz, tn), lambda l, j: (l, 0, j)),
        out_shape=jax.ShapeDtypeStruct((depth, bsz, n), F32),
        compiler_params=_cparams("parallel", "parallel"),
        name="adaln_mod",
    )(c, ada_w, ada_b.reshape(depth, 1, n))


def _mod_spec(chunk, tiles_per_seq):
    return pl.BlockSpec((1, 1, D_MODEL), lambda i: (i // tiles_per_seq, 0, chunk))


def _prenorm(x, g_ref, sc_ref, sh_ref):
    return _rms(x) * (g_ref[...] * (1.0 + sc_ref[0])) + sh_ref[0]


def _in_even_kernel(x_ref, sh_ref, sc_ref, g_ref, w_ref, bf_ref,
                    u_ref, q_ref, k_ref, v_ref, carry_ref, *, tiles_per_seq):
    i = pl.program_id(0)
    tm = x_ref.shape[0]
    h = _prenorm(x_ref[...], g_ref, sc_ref, sh_ref)
    z = _bdot(h.astype(BF16), w_ref[...])
    u_ref[...] = z[:, 0:S5_CH]
    v_ref[...] = z[:, S5_CH + 2 * FOX_W:S5_CH + 3 * FOX_W].astype(BF16)
    ls = _log_sigmoid(z[:, S5_CH + 3 * FOX_W:] + bf_ref[...])

    @pl.when(i % tiles_per_seq == 0)
    def _():
        carry_ref[...] = jnp.zeros_like(carry_ref)

    cum = _dot01(_lower_tri(tm), ls) + carry_ref[...]
    carry_ref[...] = cum[tm - 1:tm, :]

    lane = lax.broadcasted_iota(jnp.int32, (1, LANES), 1)
    feat = lane < FOX_DH
    ones = jnp.where(lane < FOX_DH + 3, 1.0, 0.0)
    for hd in range(FOX_HEADS):
        blk = (hd * FOX_DH) // LANES * LANES
        qs = z[:, S5_CH + blk:S5_CH + blk + LANES] * (FOX_DH ** -0.5 * LOG2_E)
        ks = z[:, S5_CH + FOX_W + blk:S5_CH + FOX_W + blk + LANES]
        if (hd * FOX_DH) % LANES:
            qs = pltpu.roll(qs, LANES - FOX_DH, 1)
            ks = pltpu.roll(ks, LANES - FOX_DH, 1)
        nf = jnp.broadcast_to(-LOG2_E * cum[:, hd:hd + 1], (tm, LANES))
        hi = nf.astype(BF16).astype(F32)
        mid = (nf - hi).astype(BF16).astype(F32)
        lo = nf - hi - mid
        bias = jnp.where(lane == FOX_DH, hi, jnp.where(lane == FOX_DH + 1, mid,
                                                       jnp.where(lane == FOX_DH + 2, lo, 0.0)))
        q_ref[:, hd * LANES:(hd + 1) * LANES] = jnp.where(feat, qs, ones).astype(BF16)
        k_ref[:, hd * LANES:(hd + 1) * LANES] = jnp.where(feat, ks, bias).astype(BF16)


def _in_even(x2, mod3, gain, w_in, b_f, seq, tm):
    t, d = x2.shape
    tiles_per_seq = seq // tm
    nw = S5_CH + 3 * FOX_W
    w = jnp.concatenate([w_in[:, :nw], jnp.pad(w_in[:, nw:], ((0, 0), (0, LANES - FOX_HEADS)))],
                        axis=1).astype(BF16)
    bf = jnp.pad(b_f, (0, LANES - FOX_HEADS)).reshape(1, LANES)
    tok = lambda n: pl.BlockSpec((tm, n), lambda i: (i, 0))
    full = lambda a: pl.BlockSpec(a.shape, lambda i: (0,) * a.ndim)
    return pl.pallas_call(
        functools.partial(_in_even_kernel, tiles_per_seq=tiles_per_seq),
        grid=(t // tm,),
        in_specs=[tok(d), _mod_spec(0, tiles_per_seq), _mod_spec(1, tiles_per_seq),
                  full(gain), full(w), full(bf)],
        out_specs=[tok(S5_CH), tok(FOX_HEADS * LANES), tok(FOX_HEADS * LANES), tok(FOX_W)],
        out_shape=[jax.ShapeDtypeStruct((t, S5_CH), F32),
                   jax.ShapeDtypeStruct((t, FOX_HEADS * LANES), BF16),
                   jax.ShapeDtypeStruct((t, FOX_HEADS * LANES), BF16),
                   jax.ShapeDtypeStruct((t, FOX_W), BF16)],
        scratch_shapes=[pltpu.VMEM((1, LANES), F32)],
        compiler_params=_cparams("arbitrary"),
        name="in_proj_even",
    )(x2, mod3, mod3, gain, w, bf)


S5_TILE_GROUPS = LANES // S5_GROUP
S5_SEQ_PARTS = 4


def _s5_kernel(u_ref, wt_ref, ws_ref, wc_ref, a_ref, y_ref, x_scr, e_scr, hp_scr, h_scr, *, nb, ncl):
    ell = S5_CHUNK
    sw = S5_TILE_GROUPS * 2 * S5_STATE

    @pl.when(pl.program_id(1) == 0)
    def _():
        h_scr[...] = jnp.zeros_like(h_scr)

    blk = nb * SUBLANES
    for hi in range(ncl // SUBLANES):
        for b in range(nb):
            for t in range(ell):
                x_scr[hi * blk + b * SUBLANES:hi * blk + (b + 1) * SUBLANES, t * LANES:(t + 1) * LANES] = (
                    u_ref[b, pl.ds(hi * SUBLANES * ell + t, SUBLANES, stride=ell), :])
    x = x_scr[...].astype(BF16)
    e = _bdot(x, ws_ref[0])
    tg = S5_TILE_GROUPS
    for j in range(tg):
        e_scr[j] = e[:, j * LANES:(j + 1) * LANES]
        e_scr[tg + j] = pltpu.roll(e[:, j * LANES:(j + 1) * LANES], S5_STATE, 1)
    a1 = jnp.broadcast_to(a_ref[0, 0:1, :], (nb, sw))
    a2 = jnp.broadcast_to(a_ref[0, 1:2, :], (nb, sw))
    a2s = jnp.broadcast_to(a_ref[0, 2:3, :], (nb, sw))

    def body(c, carry):
        h, hs = carry
        rows_c = pl.ds((c // SUBLANES) * blk + c % SUBLANES, nb, stride=SUBLANES)
        for j in range(tg):
            hp_scr[j, rows_c, :] = h[:, j * LANES:(j + 1) * LANES]
        e1 = jnp.concatenate([e_scr[j, rows_c, :] for j in range(tg)], axis=1)
        e2 = jnp.concatenate([e_scr[tg + j, rows_c, :] for j in range(tg)], axis=1)
        return a1 * h + a2 * hs + e1, a1 * hs + a2s * h + e2

    h, hs = lax.fori_loop(0, ncl, body, (h_scr[0], h_scr[1]))
    h_scr[0] = h
    h_scr[1] = hs
    hp = jnp.concatenate([hp_scr[j] for j in range(tg)], axis=1).astype(BF16)
    y = _bdot(x, wt_ref[0]) + _bdot(hp, wc_ref[0])
    for hi in range(ncl // SUBLANES):
        for b in range(nb):
            for t in range(ell):
                y_ref[b, pl.ds(hi * SUBLANES * ell + t, SUBLANES, stride=ell), :] = (
                    y[hi * blk + b * SUBLANES:hi * blk + (b + 1) * SUBLANES, t * LANES:(t + 1) * LANES])


def _s5_tables(lam_re, lam_im, log_dt, b_re, b_im, c_re, c_im):
    ell, p, g = S5_CHUNK, S5_STATE, S5_GROUPS
    lr, li = lam_re.astype(F32), lam_im.astype(F32)
    dt = jnp.exp(log_dt.astype(F32))[:, None]
    mag = jnp.exp(lr * dt)
    a_re, a_im = mag * jnp.cos(li * dt), mag * jnp.sin(li * dt)
    den = lr * lr + li * li
    n_re, n_im = a_re - 1.0, a_im
    z_re = (n_re * lr + n_im * li) / den
    z_im = (n_im * lr - n_re * li) / den
    br, bi = b_re.astype(F32), b_im.astype(F32)
    bb_re = z_re[..., None] * br - z_im[..., None] * bi
    bb_im = z_re[..., None] * bi + z_im[..., None] * br
    j = jnp.arange(ell + 1, dtype=F32)[:, None, None]
    pmag = jnp.exp(j * (lr * dt)[None])
    pr, pi = pmag * jnp.cos(j * (li * dt)[None]), pmag * jnp.sin(j * (li * dt)[None])
    w_re = pr[..., None] * bb_re[None] - pi[..., None] * bb_im[None]
    w_im = pr[..., None] * bb_im[None] + pi[..., None] * bb_re[None]
    cr, ci = c_re.astype(F32), c_im.astype(F32)
    kern = (jnp.einsum('gcp,jgpd->jgcd', cr, w_re[:ell], precision=HIGHEST)
            - jnp.einsum('gcp,jgpd->jgcd', ci, w_im[:ell], precision=HIGHEST))
    s_idx = jnp.arange(ell)[:, None]
    t_idx = jnp.arange(ell)[None, :]
    lag = jnp.clip(t_idx - s_idx, 0, ell - 1)
    toep = kern[lag]
    toep = jnp.where((t_idx >= s_idx)[:, :, None, None, None], toep, 0.0)
    toep = toep.transpose(2, 0, 4, 1, 3)
    rev = jnp.arange(ell - 1, -1, -1)
    st_re = w_re[rev].transpose(1, 0, 3, 2)
    st_im = w_im[rev].transpose(1, 0, 3, 2)
    wst = jnp.concatenate([st_re, st_im], -1)
    p1r, p1i = pr[1:], pi[1:]
    c_hr = cr[None] * p1r[:, :, None, :] - ci[None] * p1i[:, :, None, :]
    c_hi = -cr[None] * p1i[:, :, None, :] - ci[None] * p1r[:, :, None, :]
    cst = jnp.concatenate([c_hr, c_hi], axis=-1).transpose(1, 3, 0, 2)
    al_r, al_i = pr[ell], pi[ell]
    a_rows = jnp.stack([jnp.concatenate([al_r, al_r], -1), jnp.concatenate([-al_i, al_i], -1),
                        jnp.concatenate([al_i, -al_i], -1)], axis=1)

    tg = S5_TILE_GROUPS
    nt = g // tg
    tile = lambda a: a.reshape((nt, tg) + a.shape[1:])
    lane_grp = jnp.arange(LANES) // S5_GROUP
    step_lane_grp = jnp.tile(lane_grp, ell)
    spread_out = (jnp.eye(ell, dtype=BF16)[:, None, :, None, None]
                  * jnp.eye(S5_GROUP, dtype=BF16)[None, :, None, None, :]
                  * jnp.ones((1, 1, 1, tg, 1), BF16)).reshape(ell * S5_GROUP, ell * LANES)
    toep_t = tile(toep).transpose(0, 2, 1, 3, 4, 5).reshape(nt, ell, LANES, ell * S5_GROUP).astype(BF16)
    wt = jnp.einsum('ksrm,mn->ksrn', toep_t, spread_out)
    wt = jnp.where((lane_grp[:, None] == step_lane_grp[None, :])[None, None], wt, 0)
    wt = wt.reshape(nt, ell * LANES, ell * LANES)
    spread_state = jnp.tile(jnp.eye(2 * p, dtype=BF16), (1, tg))
    state_grp = jnp.arange(tg * 2 * p) // (2 * p)
    wst_t = tile(wst).transpose(0, 2, 1, 3, 4).reshape(nt, ell, LANES, 2 * p).astype(BF16)
    ws = jnp.einsum('ksrp,pn->ksrn', wst_t, spread_state)
    ws = jnp.where((lane_grp[:, None] == state_grp[None, :])[None, None], ws, 0)
    ws = ws.reshape(nt, ell * LANES, tg * 2 * p)
    cst_t = tile(cst).reshape(nt, tg * 2 * p, ell * S5_GROUP).astype(BF16)
    wc = jnp.einsum('krm,mn->krn', cst_t, spread_out)
    wc = jnp.where((state_grp[:, None] == step_lane_grp[None, :])[None], wc, 0)
    a_t = tile(a_rows).transpose(0, 2, 1, 3).reshape(nt, 3, tg * 2 * p)
    a_t = jnp.pad(a_t, ((0, 0), (0, SUBLANES - 3), (0, 0)))
    return wt, ws, wc, a_t


def _s5_scan(u, bsz, seq, tables):
    wt, ws, wc, a_t = tables
    ell = S5_CHUNK
    nt = S5_GROUPS // S5_TILE_GROUPS
    parts = S5_SEQ_PARTS if seq % (S5_SEQ_PARTS * ell * 2 * SUBLANES) == 0 else 1
    ncl = seq // parts // ell
    rows = bsz * ncl
    sw = S5_TILE_GROUPS * 2 * S5_STATE
    u3 = u.reshape(bsz, seq, S5_CH)
    io = pl.BlockSpec((bsz, seq // parts, LANES), lambda k, s: (0, s, k))
    per_tile = lambda a: pl.BlockSpec((1,) + a.shape[1:], lambda k, s: (k, 0, 0))
    y = pl.pallas_call(
        functools.partial(_s5_kernel, nb=bsz, ncl=ncl),
        grid=(nt, parts),
        in_specs=[io, per_tile(wt), per_tile(ws), per_tile(wc), per_tile(a_t)],
        out_specs=io,
        out_shape=jax.ShapeDtypeStruct((bsz, seq, S5_CH), F32),
        scratch_shapes=[pltpu.VMEM((rows, ell * LANES), F32),
                        pltpu.VMEM((2 * S5_TILE_GROUPS, rows, LANES), F32),
                        pltpu.VMEM((S5_TILE_GROUPS, rows, LANES), F32),
                        pltpu.VMEM((2, bsz, sw), F32)],
        compiler_params=_cparams("parallel", "arbitrary"),
        name="s5_scan",
    )(u3, wt, ws, wc, a_t)
    return y.reshape(bsz * seq, S5_CH)


def _fox_kernel(q_ref, k_ref, vt_ref, o_ref, *, tq, tk):
    i = pl.program_id(2)
    nh = FOX_HPS
    q = [q_ref[:, h * LANES:(h + 1) * LANES] for h in range(nh)]
    key = lax.broadcasted_iota(jnp.int32, (tk, tq), 0)
    qry = lax.broadcasted_iota(jnp.int32, (tk, tq), 1)
    per_q = tq // tk

    def block(j0, carry, mask):
        kj = k_ref[pl.ds(j0, tk), :]
        vtj = vt_ref[:, pl.ds(j0, tk)]
        ss = [_dot_nt(kj[:, h * LANES:(h + 1) * LANES], q[h]) for h in range(nh)]
        stats = []
        for h in range(nh):
            m, l, acc = carry[h]
            s = ss[h] if mask is None else jnp.where(mask, ss[h], -jnp.inf)
            m_new = jnp.maximum(m, jnp.max(s, axis=0, keepdims=True))
            p = jnp.exp2(s - m_new)
            alpha = jnp.exp2(m - m_new)
            stats.append((m_new, alpha * l + jnp.sum(p, axis=0, keepdims=True), alpha, p.astype(BF16)))
        out = []
        for h in range(nh):
            m_new, l, alpha, p = stats[h]
            acc = alpha * carry[h][2] + _bdot(vtj[h * FOX_DH:(h + 1) * FOX_DH, :], p)
            out.append((m_new, l, acc))
        return tuple(out)

    init = tuple((jnp.full((1, tq), -jnp.inf, F32), jnp.zeros((1, tq), F32), jnp.zeros((FOX_DH, tq), F32))
                 for _ in range(nh))
    carry = lax.fori_loop(0, i * per_q, lambda j, c: block(pl.multiple_of(j * tk, tk), c, None), init)
    for d in range(per_q):
        carry = block(pl.multiple_of(i * tq + d * tk, tk), carry, key + d * tk <= qry)
    for g in range(nh // 2):
        o_t = jnp.concatenate([carry[h][2] / carry[h][1] for h in (2 * g, 2 * g + 1)], axis=0)
        o_ref[:, g * LANES:(g + 1) * LANES] = o_t.T.astype(o_ref.dtype)


def _fox(q_aug, k_aug, v, bsz, seq, tq, tk):
    t = v.shape[0]
    nh = FOX_HPS
    groups = FOX_HEADS // nh
    nq = seq // tq
    v_t = v.reshape(bsz, seq, FOX_W).transpose(0, 2, 1).reshape(bsz * FOX_W, seq)
    return pl.pallas_call(
        functools.partial(_fox_kernel, tq=tq, tk=tk),
        grid=(bsz, groups, nq),
        in_specs=[pl.BlockSpec((tq, nh * LANES), lambda b, p, i: (b * nq + i, p)),
                  pl.BlockSpec((seq, nh * LANES), lambda b, p, i: (b, p)),
                  pl.BlockSpec((nh * FOX_DH, seq), lambda b, p, i: (b * groups + p, 0))],
        out_specs=pl.BlockSpec((tq, nh * FOX_DH), lambda b, p, i: (b * nq + i, p)),
        out_shape=jax.ShapeDtypeStruct((t, FOX_W), BF16),
        compiler_params=_cparams("parallel", "parallel", "arbitrary"),
        name="fox_attention",
    )(q_aug, k_aug, v_t)


_ODD_COLS = (("rq", 512), ("rk", 512), ("rv", 512), ("rg", 512), ("gq", 256), ("gk", 256),
             ("gv", 512), ("gr", 512), ("glr", LANES))


def _odd_offsets():
    off, out = 0, {}
    for name, w in _ODD_COLS:
        out[name] = (off, off + w)
        off += w
    return out, off


def _in_odd_kernel(x_ref, sh_ref, sc_ref, g_ref, w_ref, cos_ref, sin_ref, wup_ref, bg_ref,
                   rq_ref, rk_ref, rv_ref, sg_ref, gq_ref, gk_ref, gv_ref, sr_ref, la_ref):
    h = _prenorm(x_ref[...], g_ref, sc_ref, sh_ref)
    z = _bdot(h.astype(BF16), w_ref[...])
    off, _ = _odd_offsets()
    col = lambda n: z[:, off[n][0]:off[n][1]]
    cos, sin = cos_ref[...], sin_ref[...]

    def rope(t, scale):
        heads = []
        for hd in range(RET_HEADS):
            th = t[:, hd * RET_DK:(hd + 1) * RET_DK]
            heads.append((th * cos + pltpu.roll(th, RET_DK // 2, 1) * sin) * scale)
        return jnp.concatenate(heads, axis=1).astype(BF16)

    rq_ref[...] = rope(col("rq"), 1.0)
    rk_ref[...] = rope(col("rk"), RET_DK ** -0.5)
    rv_ref[...] = col("rv").astype(BF16)
    sg_ref[...] = _silu(col("rg"))
    gq_ref[...] = col("gq") * (GLA_DK ** -0.5)
    gk_ref[...] = col("gk")
    gv_ref[...] = col("gv").astype(BF16)
    sr_ref[...] = _silu(col("gr"))
    gate = jnp.dot(col("glr"), wup_ref[...], preferred_element_type=F32, precision=HIGHEST) + bg_ref[...]
    la_ref[...] = _log_sigmoid(gate) * (1.0 / GLA_TAU)


def _in_odd(x2, mod3, gain, w_in, w_up, b_gate, seq, tm):
    t, d = x2.shape
    tps = seq // tm
    ref_w = (512, 512, 512, 512, 256, 256, 512, GLA_RANK, 512)
    starts = [0]
    for wd in ref_w:
        starts.append(starts[-1] + wd)
    seg = lambda j: w_in[:, starts[j]:starts[j + 1]]
    w = jnp.concatenate([seg(0), seg(1), seg(2), seg(3), seg(4), seg(5), seg(6), seg(8),
                         jnp.pad(seg(7), ((0, 0), (0, LANES - GLA_RANK)))], axis=1).astype(BF16)
    wup = jnp.pad(w_up.astype(F32), ((0, LANES - GLA_RANK), (0, 0)))
    bg = b_gate.reshape(1, -1).astype(F32)
    half = RET_DK // 2
    inv = ROPE_BASE ** (-jnp.arange(half, dtype=F32) / half)
    ang = jnp.arange(seq, dtype=F32)[:, None] * inv[None, :]
    cos = jnp.concatenate([jnp.cos(ang), jnp.cos(ang)], axis=1)
    sin = jnp.concatenate([-jnp.sin(ang), jnp.sin(ang)], axis=1)
    tok = lambda n: pl.BlockSpec((tm, n), lambda i: (i, 0))
    full = lambda a: pl.BlockSpec(a.shape, lambda i: (0,) * a.ndim)
    pos = pl.BlockSpec((tm, RET_DK), lambda i: (i % tps, 0))
    widths = (512, 512, 512, 512, 256, 256, 512, 512, 256)
    dtypes = (BF16, BF16, BF16, F32, F32, F32, BF16, F32, F32)
    return pl.pallas_call(
        _in_odd_kernel,
        grid=(t // tm,),
        in_specs=[tok(d), _mod_spec(0, tps), _mod_spec(1, tps), full(gain), full(w), pos, pos,
                  full(wup), full(bg)],
        out_specs=[tok(n) for n in widths],
        out_shape=[jax.ShapeDtypeStruct((t, n), dt) for n, dt in zip(widths, dtypes)],
        compiler_params=_cparams("parallel"),
        name="in_proj_odd",
    )(x2, mod3, mod3, gain, w, cos, sin, wup, bg)


RET_CHUNK = 256


def _ret_kernel(q_ref, k_ref, v_ref, sg_ref, dm_ref, xi_ref, zeta_ref, gl_ref, y_ref, st_ref):
    @pl.when(pl.program_id(1) == 0)
    def _():
        st_ref[...] = jnp.zeros_like(st_ref)

    heads = range(RET_HEADS)
    col = lambda h: slice(h * RET_DK, (h + 1) * RET_DK)
    q = [q_ref[:, col(h)] for h in heads]
    k = [k_ref[:, col(h)] for h in heads]
    v = [v_ref[:, col(h)] for h in heads]
    st = [st_ref[h] for h in heads]
    s = [_dot_nt(q[h], k[h]) for h in heads]
    inter = [_bdot((q[h].astype(F32) * xi_ref[h]).astype(BF16), st[h].astype(BF16)) for h in heads]
    upd = [_dot_tn((k[h].astype(F32) * zeta_ref[h]).astype(BF16), v[h]) for h in heads]
    for h in heads:
        o = _bdot((s[h] * dm_ref[h]).astype(BF16), v[h]) + inter[h]
        st_ref[h] = gl_ref[h, 0:1, :] * st[h] + upd[h]
        y_ref[:, col(h)] = (sg_ref[:, col(h)] * _rms(o)).astype(y_ref.dtype)


def _retention(rq, rk, rv, sg, bsz, seq):
    t = rq.shape[0]
    ell = min(RET_CHUNK, seq)
    nc = seq // ell
    log_g = jnp.log(1.0 - jnp.exp2(-5.0 - jnp.arange(RET_HEADS, dtype=F32)))
    idx = jnp.arange(ell, dtype=F32)
    rel = idx[:, None] - idx[None, :]
    dmat = jnp.where(rel >= 0, jnp.exp(log_g[:, None, None] * jnp.maximum(rel, 0.0)), 0.0)
    lanes = lambda a: jnp.broadcast_to(a[..., None], a.shape + (RET_DK,))
    xi = lanes(jnp.exp(log_g[:, None] * (idx + 1.0)))
    zeta = lanes(jnp.exp(log_g[:, None] * (ell - 1.0 - idx)))
    gl = jnp.broadcast_to(jnp.exp(log_g * ell)[:, None, None], (RET_HEADS, SUBLANES, RET_DV))
    blk = pl.BlockSpec((ell, RET_HEADS * RET_DK), lambda b, c: (b * nc + c, 0))
    full = lambda a: pl.BlockSpec(a.shape, lambda b, c: (0, 0, 0))
    return pl.pallas_call(
        _ret_kernel,
        grid=(bsz, nc),
        in_specs=[blk, blk, blk, blk, full(dmat), full(xi), full(zeta), full(gl)],
        out_specs=blk,
        out_shape=jax.ShapeDtypeStruct((t, RET_HEADS * RET_DV), BF16),
        scratch_shapes=[pltpu.VMEM((RET_HEADS, RET_DK, RET_DV), F32)],
        compiler_params=_cparams("parallel", "arbitrary"),
        name="retention",
    )(rq, rk, rv, sg, dmat, xi, zeta, gl)


def _gla_kernel(q_ref, k_ref, la_ref, v_ref, sg_ref, y_ref, st_ref, b_scr, v_scr, p_scr, r_scr):
    @pl.when(pl.program_id(1) == 0)
    def _():
        st_ref[...] = jnp.zeros_like(st_ref)

    ell, sub = GLA_CHUNK, GLA_SUB
    n_sub = ell // sub
    nb = q_ref.shape[0]
    pairs = GLA_HEADS // 2
    streams = [(bl, p) for bl in range(nb) for p in range(pairs)]
    lane = lax.broadcasted_iota(jnp.int32, (1, LANES), 1)
    first = lane < GLA_DK
    head = (first, jnp.logical_not(first))
    pick = lambda h, a: jnp.where(head[h], a, 0.0).astype(BF16)
    tri = _lower_tri(ell)
    tau = lax.broadcasted_iota(jnp.int32, (sub, LANES), 0)
    row_of, per_sub = [], 0
    for s_ in range(sub):
        row_of.append(per_sub)
        per_sub += sub - (s_ // SUBLANES) * SUBLANES
    rsub = lax.broadcasted_iota(jnp.int32, (LANES, 2 * LANES), 0)
    csub = lax.broadcasted_iota(jnp.int32, (LANES, 2 * LANES), 1)
    ind = ((rsub < GLA_DK) == (csub < LANES)).astype(BF16)

    val = {}
    for sid, (bl, p) in enumerate(streams):
        qk = slice(p * LANES, (p + 1) * LANES)
        q, k = q_ref[bl, :, qk], k_ref[bl, :, qk]
        b = _dot01(tri, la_ref[bl, :, qk])
        b_scr[sid] = b
        v_bf = v_ref[bl, :, p * 2 * GLA_DV:(p + 1) * 2 * GLA_DV]
        v_scr[sid] = v_bf.astype(F32)
        st = st_ref[sid]
        val[sid] = dict(q=q, k=k, b=b, st=st, vh=[v_bf[:, h * GLA_DV:(h + 1) * GLA_DV] for h in range(2)])

    for sid in val:
        d = val[sid]
        qe = d["q"] * jnp.exp(d["b"])
        st_bf = d["st"].astype(BF16)
        d["o"] = [_dot_nt(pick(h, qe), st_bf) for h in range(2)]

    row = lax.broadcasted_iota(jnp.int32, (ell, LANES), 0)
    for sid in val:
        d = val[sid]
        q, k, b = d["q"], d["k"], d["b"]
        qa, ka = [], []
        for i in range(1, n_sub):
            lo = i * sub
            ref_row = b[lo - 1:lo, :]
            in_i = (row >= lo) & (row < lo + sub)
            qa.append(jnp.where(in_i, q * jnp.exp(jnp.minimum(b - ref_row, 0.0)), 0.0))
            ka.append(jnp.where(row < lo, k * jnp.exp(jnp.minimum(ref_row - b, 0.0)), 0.0))
        k_cat = jnp.concatenate(ka, axis=1).astype(BF16)
        d["a_off"] = [_dot_nt(jnp.concatenate([pick(h, x) for x in qa], axis=1), k_cat) for h in range(2)]
    for sid in val:
        d = val[sid]
        d["off"] = [_bdot(d["a_off"][h].astype(BF16), d["vh"][h]) for h in range(2)]

    for sid, (bl, p) in enumerate(streams):
        d = val[sid]
        q, b = d["q"], d["b"]
        for i in range(n_sub):
            lo = i * sub
            qi, bi = q[lo:lo + sub], b[lo:lo + sub]
            for s in range(sub):
                k_row = k_ref[bl, pl.ds(lo + s, 1), p * LANES:(p + 1) * LANES]
                b_row = b_scr[sid, pl.ds(lo + s, 1), :]
                r0 = (s // SUBLANES) * SUBLANES
                w = jnp.exp(jnp.minimum(bi[r0:] - b_row, 0.0))
                tile_s = jnp.where(tau[r0:] >= s, qi[r0:] * k_row * w, 0.0)
                p_scr[sid, pl.ds(i * per_sub + row_of[s], sub - r0), :] = tile_s
    for sid in val:
        r_scr[sid] = _bdot(p_scr[sid].astype(BF16), ind)
    for sid in val:
        diag = [[], []]
        for i in range(n_sub):
            lo = i * sub
            for h in range(2):
                acc = [jnp.zeros((SUBLANES, GLA_DV), F32) for _ in range(sub // SUBLANES)]
                for s in range(sub):
                    v_row = v_scr[sid, pl.ds(lo + s, 1), h * GLA_DV:(h + 1) * GLA_DV]
                    for part in range(s // SUBLANES, sub // SUBLANES):
                        rows = pl.ds(i * per_sub + row_of[s] + (part - s // SUBLANES) * SUBLANES, SUBLANES)
                        acc[part] = acc[part] + r_scr[sid, rows, h * LANES:(h + 1) * LANES] * v_row
                diag[h].append(jnp.concatenate(acc, axis=0))
        val[sid]["diag"] = diag

    for sid, (bl, p) in enumerate(streams):
        d = val[sid]
        b_last = d["b"][ell - 1:ell, :]
        kh = (d["k"] * jnp.exp(b_last - d["b"])).astype(BF16)
        upd = [_dot_tn(d["vh"][h], kh) for h in range(2)]
        st_ref[sid] = d["st"] * jnp.exp(b_last) + jnp.where(first, upd[0], upd[1])
        for h in range(2):
            oh = d["o"][h] + d["off"][h] + jnp.concatenate(d["diag"][h], axis=0)
            cols = slice((2 * p + h) * GLA_DV, (2 * p + h + 1) * GLA_DV)
            y_ref[bl, :, cols] = (sg_ref[bl, :, cols] * _rms(oh)).astype(y_ref.dtype)


GLA_BATCHES = 4


def _gla(gq, gk, la, gv, sr, bsz, seq):
    t = gq.shape[0]
    ell = GLA_CHUNK
    nc = seq // ell
    nb = min(GLA_BATCHES, bsz)
    ns = nb * (GLA_HEADS // 2)
    prod_rows = (ell // GLA_SUB) * sum(GLA_SUB - (s // SUBLANES) * SUBLANES for s in range(GLA_SUB))
    r3 = lambda a: a.reshape(bsz, seq, a.shape[1])
    spec = lambda w: pl.BlockSpec((nb, ell, w), lambda b, c: (b, c, 0))
    wq, wv = GLA_HEADS * GLA_DK, GLA_HEADS * GLA_DV
    y = pl.pallas_call(
        _gla_kernel,
        grid=(bsz // nb, nc),
        in_specs=[spec(wq), spec(wq), spec(wq), spec(wv), spec(wv)],
        out_specs=spec(wv),
        out_shape=jax.ShapeDtypeStruct((bsz, seq, wv), BF16),
        scratch_shapes=[pltpu.VMEM((ns, GLA_DV, LANES), F32),
                        pltpu.VMEM((ns, ell, LANES), F32),
                        pltpu.VMEM((ns, ell, 2 * GLA_DV), F32),
                        pltpu.VMEM((ns, prod_rows, LANES), F32),
                        pltpu.VMEM((ns, prod_rows, 2 * LANES), F32)],
        compiler_params=_cparams("parallel", "arbitrary"),
        name="gla",
    )(r3(gq), r3(gk), r3(la), r3(gv), r3(sr))
    return y.reshape(t, wv)


def _post_tail(m, x_ref, g1_ref, gpost_ref, sh2_ref, sc2_ref, gpre_ref, rw_ref, rb_ref,
               x1_ref, h2_ref, topi_ref, gate_ref, rank_ref, cnt_ref, carry_ref):
    i = pl.program_id(0)
    tm = m.shape[0]
    x1 = x_ref[...] + g1_ref[0] * (_rms(m) * gpost_ref[...])
    x1_ref[...] = x1
    h2 = _rms(x1) * (gpre_ref[...] * (1.0 + sc2_ref[0])) + sh2_ref[0]
    for c in range(D_MODEL // LANES):
        h2_ref[pl.ds(c, tm, stride=SUBLANES), :] = h2[:, c * LANES:(c + 1) * LANES]
    h_hi = h2.astype(BF16)
    h_lo = (h2 - h_hi.astype(F32)).astype(BF16)
    w_both = rw_ref[...]
    hw = _bdot(h_hi, w_both)
    logits = hw[:, :LANES] + hw[:, LANES:] + _bdot(h_lo, w_both[:, :LANES]) + rb_ref[...]

    lane = lax.broadcasted_iota(jnp.int32, (tm, LANES), 1)
    lanef = lane.astype(F32)
    work = logits
    topv = jnp.full((tm, LANES), -jnp.inf, F32)
    topi = jnp.zeros((tm, LANES), F32)
    onehot = jnp.zeros((tm, LANES), F32)
    hits = []
    for k in range(TOP_K):
        mx = jnp.max(work, axis=-1, keepdims=True)
        idx = jnp.min(jnp.where(work == mx, lanef, float(LANES)), axis=-1, keepdims=True)
        hit = lanef == idx
        hits.append(hit)
        topv = jnp.where(lane == k, mx, topv)
        topi = jnp.where(lane == k, idx, topi)
        onehot = onehot + hit.astype(F32)
        work = jnp.where(hit, -jnp.inf, work)
    e = jnp.exp(topv - jnp.max(topv, axis=-1, keepdims=True))
    gate_ref[...] = e / jnp.sum(e, axis=-1, keepdims=True)
    topi_ref[...] = topi.astype(jnp.int32)

    @pl.when(i == 0)
    def _():
        carry_ref[...] = jnp.zeros_like(carry_ref)

    before = _bdot(_lower_tri(tm, strict=True), onehot.astype(BF16)) + carry_ref[...]
    rank = jnp.zeros((tm, LANES), F32)
    for k in range(TOP_K):
        rk = jnp.sum(jnp.where(hits[k], before, 0.0), axis=-1, keepdims=True)
        rank = jnp.where(lane == k, rk, rank)
    rank_ref[...] = rank.astype(jnp.int32)
    total = before[tm - 1:tm, :] + onehot[tm - 1:tm, :]
    carry_ref[...] = total
    cnt_ref[...] = jnp.broadcast_to(total, cnt_ref.shape)


def _out_even_kernel(ys_ref, u_ref, yb_ref, d_ref, gw_ref, gb_ref, wa_ref, wb_ref, *rest):
    y = ys_ref[...] + d_ref[...] * u_ref[...]
    g = jax.nn.gelu(y)
    ya = g * _sigmoid(_bdot(g.astype(BF16), gw_ref[...]) + gb_ref[...])
    m = _bdot(ya.astype(BF16), wa_ref[...]) + _bdot(yb_ref[...], wb_ref[...])
    _post_tail(m, *rest)


def _out_odd_kernel(yc_ref, yd_ref, wa_ref, wb_ref, *rest):
    m = _bdot(yc_ref[...], wa_ref[...]) + _bdot(yd_ref[...], wb_ref[...])
    _post_tail(m, *rest)


def _mixer_out(body, mix_args, mix_specs, x2, mod3, g_post, g_pre, router_w, router_b, seq, tm):
    t, d = x2.shape
    tps = seq // tm
    rw32 = jnp.pad(router_w.astype(F32), ((0, 0), (0, LANES - N_EXPERTS)))
    rw_hi = rw32.astype(BF16)
    rw = jnp.concatenate([rw_hi, (rw32 - rw_hi.astype(F32)).astype(BF16)], axis=1)
    rb = jnp.pad(router_b.astype(F32), (0, LANES - N_EXPERTS), constant_values=-1e30).reshape(1, LANES)
    tok = lambda n: pl.BlockSpec((tm, n), lambda i: (i, 0))
    full = lambda a: pl.BlockSpec(a.shape, lambda i: (0,) * a.ndim)
    tail_args = [x2, mod3, g_post, mod3, mod3, g_pre, rw, rb]
    tail_specs = [tok(d), _mod_spec(2, tps), full(g_post), _mod_spec(3, tps), _mod_spec(4, tps),
                  full(g_pre), full(rw), full(rb)]
    return pl.pallas_call(
        body,
        grid=(t // tm,),
        in_specs=mix_specs + tail_specs,
        out_specs=[tok(d), pl.BlockSpec((tm * SUBLANES, LANES), lambda i: (i, 0)),
                   tok(LANES), tok(LANES), tok(LANES), pl.BlockSpec((SUBLANES, LANES), lambda i: (0, 0))],
        out_shape=[jax.ShapeDtypeStruct((t, d), F32),
                   jax.ShapeDtypeStruct((t * SUBLANES, LANES), F32),
                   jax.ShapeDtypeStruct((t, LANES), jnp.int32),
                   jax.ShapeDtypeStruct((t, LANES), F32),
                   jax.ShapeDtypeStruct((t, LANES), jnp.int32),
                   jax.ShapeDtypeStruct((SUBLANES, LANES), F32)],
        scratch_shapes=[pltpu.VMEM((1, LANES), F32)],
        compiler_params=_cparams("arbitrary"),
        name="mixer_out_router",
    )(*mix_args, *tail_args)


def _out_even(ys, u, yb, d_skip, glu_w, glu_b, w_out, *tail, seq, tm):
    tok = lambda n: pl.BlockSpec((tm, n), lambda i: (i, 0))
    full = lambda a: pl.BlockSpec(a.shape, lambda i: (0,) * a.ndim)
    args = [ys, u, yb, d_skip.reshape(1, -1), glu_w.astype(BF16), glu_b.reshape(1, -1),
            w_out[:S5_CH].astype(BF16), w_out[S5_CH:].astype(BF16)]
    specs = [tok(S5_CH), tok(S5_CH), tok(FOX_W)] + [full(a) for a in args[3:]]
    return _mixer_out(_out_even_kernel, args, specs, *tail, seq, tm)


def _out_odd(yc, yd, w_out, *tail, seq, tm):
    tok = lambda n: pl.BlockSpec((tm, n), lambda i: (i, 0))
    full = lambda a: pl.BlockSpec(a.shape, lambda i: (0,) * a.ndim)
    nc = yc.shape[1]
    args = [yc, yd, w_out[:nc].astype(BF16), w_out[nc:].astype(BF16)]
    specs = [tok(nc), tok(yd.shape[1])] + [full(a) for a in args[2:]]
    return _mixer_out(_out_odd_kernel, args, specs, *tail, seq, tm)


def _route_kernel(topi_ref, rank_ref, cnt_ref, dest_ref, blk_ref, meta_ref):
    tm = topi_ref.shape[0]
    cnt = cnt_ref[...]
    padded = jnp.floor((cnt + (MOE_BLOCK - 1.0)) * (1.0 / MOE_BLOCK)) * MOE_BLOCK
    r = lax.broadcasted_iota(jnp.int32, (LANES, LANES), 0)
    c = lax.broadcasted_iota(jnp.int32, (LANES, LANES), 1)
    hi, mid, lo = _split3(padded)
    incl = (r <= c).astype(BF16)
    pad_end = _bdot(hi, incl) + _bdot(mid, incl) + _bdot(lo, incl)
    pad_start = pad_end - padded
    lane = lax.broadcasted_iota(jnp.int32, (tm, LANES), 1)
    lanef = lane.astype(F32)
    topi = topi_ref[...].astype(F32)
    start_row = pad_start[0:1, :]
    dest = jnp.zeros((tm, LANES), F32)
    for k in range(TOP_K):
        idx = jnp.sum(jnp.where(lane == k, topi, 0.0), axis=-1, keepdims=True)
        st = jnp.sum(jnp.where(lanef == idx, start_row, 0.0), axis=-1, keepdims=True)
        dest = jnp.where(lane == k, st, dest)
    dest_ref[...] = dest.astype(jnp.int32) + rank_ref[...]

    nb = blk_ref.shape[1]
    end_col = jnp.sum(jnp.where(r == c, jnp.broadcast_to(pad_end[0:1, :], (LANES, LANES)), 0.0),
                      axis=-1, keepdims=True)
    jpos = lax.broadcasted_iota(jnp.int32, (LANES, nb), 1).astype(F32) * MOE_BLOCK
    esub = lax.broadcasted_iota(jnp.int32, (LANES, nb), 0)
    done = jnp.where((end_col <= jpos) & (esub < N_EXPERTS), 1.0, 0.0)
    be = jnp.minimum(jnp.sum(done, axis=0, keepdims=True), N_EXPERTS - 1.0)
    blk_ref[...] = jnp.broadcast_to(be, blk_ref.shape).astype(jnp.int32)
    lane1 = lax.broadcasted_iota(jnp.int32, (SUBLANES, LANES), 1)
    n_valid = jnp.sum(jnp.where(lane1 == N_EXPERTS - 1, pad_end, 0.0), axis=-1, keepdims=True) * (1.0 / MOE_BLOCK)
    sub1 = lax.broadcasted_iota(jnp.int32, (SUBLANES, LANES), 0)
    meta = jnp.where(sub1 == 0, pad_start + cnt, jnp.where(sub1 == 1, pad_end, jnp.broadcast_to(n_valid, (SUBLANES, LANES))))
    meta_ref[...] = meta.astype(jnp.int32)


def _route(topi, rank, cnt, n_blocks, tm):
    t = topi.shape[0]
    nb_pad = -(-n_blocks // LANES) * LANES
    tok = pl.BlockSpec((tm, LANES), lambda i: (i, 0))
    fix = lambda n: pl.BlockSpec((SUBLANES, n), lambda i: (0, 0))
    return pl.pallas_call(
        _route_kernel,
        grid=(t // tm,),
        in_specs=[tok, tok, fix(LANES)],
        out_specs=[tok, fix(nb_pad), fix(LANES)],
        out_shape=[jax.ShapeDtypeStruct((t, LANES), jnp.int32),
                   jax.ShapeDtypeStruct((SUBLANES, nb_pad), jnp.int32),
                   jax.ShapeDtypeStruct((SUBLANES, LANES), jnp.int32)],
        compiler_params=_cparams("arbitrary"),
        name="route_plan",
    )(topi, rank, cnt)


def _dispatch_kernel(pad_ref, dest_ref, h_ref, xb_ref, zero_ref, sem_z, sem_s):
    i = pl.program_id(0)
    tm = h_ref.shape[0]

    @pl.when(i == 0)
    def _():
        zero_ref[...] = jnp.zeros_like(zero_ref)
        sizes = [1 << b for b in range(int(math.log2(MOE_BLOCK)) - 1, -1, -1)]

        def fill(e, carry, do_wait):
            start = pad_ref[0, e]
            n_pad = pad_ref[1, e] - start
            off = start
            for sz in sizes:
                take = (n_pad & sz) != 0
                cp = pltpu.make_async_copy(zero_ref.at[pl.ds(0, sz)], xb_ref.at[pl.ds(off, sz)], sem_z)

                @pl.when(take)
                def _():
                    if do_wait:
                        cp.wait()
                    else:
                        cp.start()
                off = off + jnp.where(take, sz, 0)
            return carry

        half = zero_ref.shape[0]

        def fill_unused(j, carry, do_wait):
            for part in range(MOE_BLOCK // half):
                cp = pltpu.make_async_copy(zero_ref, xb_ref.at[pl.ds(j * MOE_BLOCK + part * half, half)], sem_z)
                if do_wait:
                    cp.wait()
                else:
                    cp.start()
            return carry

        n_blocks = xb_ref.shape[0] // MOE_BLOCK
        lax.fori_loop(0, N_EXPERTS, lambda e, c: fill(e, c, False), 0)
        lax.fori_loop(pad_ref[2, 0], n_blocks, lambda j, c: fill_unused(j, c, False), 0)
        lax.fori_loop(0, N_EXPERTS, lambda e, c: fill(e, c, True), 0)
        lax.fori_loop(pad_ref[2, 0], n_blocks, lambda j, c: fill_unused(j, c, True), 0)

    def issue(r, carry):
        for k in range(TOP_K):
            pltpu.make_async_copy(h_ref.at[r], xb_ref.at[dest_ref[r * TOP_K + k]], sem_s).start(priority=k % 2)
        return carry

    lax.fori_loop(0, tm, issue, 0, unroll=ROW_DMA_UNROLL)
    for k in range(TOP_K):
        pltpu.make_async_copy(h_ref, xb_ref.at[pl.ds(0, tm)], sem_s).wait()


def _dispatch(h2t, dest, meta, n_slots, tm):
    t = h2t.shape[0] // SUBLANES
    h3 = h2t.reshape(t, SUBLANES, LANES)
    return pl.pallas_call(
        _dispatch_kernel,
        grid_spec=pltpu.PrefetchScalarGridSpec(
            num_scalar_prefetch=1,
            grid=(t // tm,),
            in_specs=[pl.BlockSpec((tm * TOP_K,), lambda i, p: (i,), memory_space=pltpu.SMEM),
                      pl.BlockSpec((tm, SUBLANES, LANES), lambda i, p: (i, 0, 0))],
            out_specs=pl.BlockSpec(memory_space=pl.ANY),
            scratch_shapes=[pltpu.VMEM((MOE_BLOCK // 2, SUBLANES, LANES), F32),
                            pltpu.SemaphoreType.DMA, pltpu.SemaphoreType.DMA]),
        out_shape=jax.ShapeDtypeStruct((n_slots, SUBLANES, LANES), F32),
        compiler_params=_cparams("arbitrary"),
        name="moe_dispatch",
    )(meta[:3, :N_EXPERTS], dest, h3)


def _expert_kernel(be_ref, nv_ref, nxt_ref, x_ref, wgu_hbm, bgu_ref, wd_hbm, bd_ref, y_ref,
                   wgu_f32, wd_f32, wgu_bf, wd_bf, sem, *, layer):
    j = pl.program_id(0)
    valid = j < nv_ref[0]
    first = valid & ((j == 0) | (be_ref[j] != be_ref[jnp.maximum(j - 1, 0)]))

    def weight_copies(e):
        return (pltpu.make_async_copy(wgu_hbm.at[layer, e], wgu_f32, sem.at[0]),
                pltpu.make_async_copy(wd_hbm.at[layer, e], wd_f32, sem.at[1]))

    @pl.when(j == 0)
    def _():
        for cp in weight_copies(be_ref[0]):
            cp.start()

    @pl.when(first)
    def _():
        for cp in weight_copies(be_ref[j]):
            cp.wait()
        wgu_bf[...] = wgu_f32[...].astype(BF16)
        wd_bf[...] = wd_f32[...].astype(BF16)

        @pl.when(nxt_ref[j] >= 0)
        def _():
            for cp in weight_copies(nxt_ref[j]):
                cp.start()

    @pl.when(valid)
    def _():
        x = jnp.concatenate([x_ref[pl.ds(c, MOE_BLOCK, stride=SUBLANES), :] for c in range(D_MODEL // LANES)],
                            axis=1).astype(BF16)
        gu = _bdot(x, wgu_bf[...]) + bgu_ref[0]
        x_glu = jnp.minimum(gu[:, :D_EXPERT], SWIGLU_LIMIT)
        x_lin = jnp.clip(gu[:, D_EXPERT:], -SWIGLU_LIMIT, SWIGLU_LIMIT)
        act = x_glu * _sigmoid(SWIGLU_ALPHA * x_glu) * (x_lin + 1.0)
        y = _bdot(act.astype(BF16), wd_bf[...]) + bd_ref[0]
        for c in range(D_MODEL // LANES):
            y_ref[pl.ds(c, MOE_BLOCK, stride=SUBLANES), :] = y[:, c * LANES:(c + 1) * LANES]

    @pl.when(jnp.logical_not(valid))
    def _():
        y_ref[...] = jnp.zeros_like(y_ref)


def _experts(xb, block_expert, n_valid, w_gu, b_gu, w_down, b_down, layer):
    n_slots = xb.shape[0]
    n_blocks = n_slots // MOE_BLOCK
    rows = MOE_BLOCK * SUBLANES
    x2 = xb.reshape(n_slots * SUBLANES, LANES)
    depth, ne, d, de2 = w_gu.shape
    idx = jnp.arange(n_blocks, dtype=jnp.int32)
    is_first = ((idx == 0) | (block_expert != jnp.roll(block_expert, 1))) & (idx < n_valid[0])
    first_at = lax.cummin(jnp.where(is_first, idx, n_blocks)[::-1])[::-1]
    next_first = jnp.concatenate([first_at[1:], jnp.full((1,), n_blocks, jnp.int32)])
    nxt = jnp.where(next_first < n_blocks, block_expert[jnp.minimum(next_first, n_blocks - 1)], -1)
    last = lambda j, be, nv, nx: jnp.minimum(j, nv[0] - 1)
    bmap = lambda j, be, nv, nx: (layer, be[last(j, be, nv, nx)], 0, 0)
    return pl.pallas_call(
        functools.partial(_expert_kernel, layer=layer),
        grid_spec=pltpu.PrefetchScalarGridSpec(
            num_scalar_prefetch=3,
            grid=(n_blocks,),
            in_specs=[pl.BlockSpec((rows, LANES), lambda j, be, nv, nx: (last(j, be, nv, nx), 0)),
                      pl.BlockSpec(memory_space=pl.ANY),
                      pl.BlockSpec((None, 1, 1, de2), bmap),
                      pl.BlockSpec(memory_space=pl.ANY),
                      pl.BlockSpec((None, 1, 1, d), bmap)],
            out_specs=pl.BlockSpec((rows, LANES), lambda j, be, nv, nx: (j, 0)),
            scratch_shapes=[pltpu.VMEM((d, de2), F32), pltpu.VMEM((de2 // 2, d), F32),
                            pltpu.VMEM((d, de2), BF16), pltpu.VMEM((de2 // 2, d), BF16),
                            pltpu.SemaphoreType.DMA((2,))]),
        out_shape=jax.ShapeDtypeStruct((n_slots * SUBLANES, LANES), F32),
        compiler_params=_cparams("arbitrary"),
        name="moe_experts",
    )(block_expert, n_valid, nxt.astype(jnp.int32), x2, w_gu, b_gu.reshape(depth, ne, 1, de2), w_down,
      b_down.reshape(depth, ne, 1, d))


def _combine_kernel(dest_ref, dest_next_ref, yb_ref, gate_ref, x1_ref, g2_ref, gpost_ref, o_ref, buf, sem):
    i = pl.program_id(0)
    tm = x1_ref.shape[0]
    slot = i % 2

    def gather(idx_ref, into):
        def issue(r, carry):
            for k in range(TOP_K):
                src = pl.multiple_of(idx_ref[r * TOP_K + k] * SUBLANES, SUBLANES)
                dst = pl.multiple_of((k * tm + r) * SUBLANES, SUBLANES)
                pltpu.make_async_copy(yb_ref.at[pl.ds(src, SUBLANES), :], buf.at[into, pl.ds(dst, SUBLANES), :],
                                      sem.at[into]).start(priority=k % 2)
            return carry
        lax.fori_loop(0, tm, issue, 0, unroll=ROW_DMA_UNROLL)

    @pl.when(i == 0)
    def _():
        gather(dest_ref, 0)

    @pl.when(i + 1 < pl.num_programs(0))
    def _():
        gather(dest_next_ref, 1 - slot)

    pltpu.make_async_copy(yb_ref.at[pl.ds(0, TOP_K * tm * SUBLANES), :], buf.at[slot], sem.at[slot]).wait()
    gates = gate_ref[...]
    gk = [jnp.broadcast_to(gates[:, k:k + 1], (tm, LANES)) for k in range(TOP_K)]
    b2 = buf.at[slot]
    cols = []
    for c in range(D_MODEL // LANES):
        acc = jnp.zeros((tm, LANES), F32)
        for k in range(TOP_K):
            acc = acc + gk[k] * b2[pl.ds(k * tm * SUBLANES + c, tm, stride=SUBLANES), :]
        cols.append(acc)
    f = jnp.concatenate(cols, axis=1)
    o_ref[...] = x1_ref[...] + g2_ref[0] * (_rms(f) * gpost_ref[...])


def _combine(yb, dest, gates, x1, mod3, g_post, seq, tm):
    t, d = x1.shape
    tps = seq // tm
    n = t // tm
    return pl.pallas_call(
        _combine_kernel,
        grid=(n,),
        in_specs=[pl.BlockSpec((tm * TOP_K,), lambda i: (i,), memory_space=pltpu.SMEM),
                  pl.BlockSpec((tm * TOP_K,), lambda i: (jnp.minimum(i + 1, n - 1),), memory_space=pltpu.SMEM),
                  pl.BlockSpec(memory_space=pl.ANY),
                  pl.BlockSpec((tm, LANES), lambda i: (i, 0)),
                  pl.BlockSpec((tm, d), lambda i: (i, 0)),
                  _mod_spec(5, tps),
                  pl.BlockSpec(g_post.shape, lambda i: (0, 0))],
        out_specs=pl.BlockSpec((tm, d), lambda i: (i, 0)),
        out_shape=jax.ShapeDtypeStruct((t, d), F32),
        scratch_shapes=[pltpu.VMEM((2, TOP_K * tm * SUBLANES, LANES), F32), pltpu.SemaphoreType.DMA((2,))],
        compiler_params=_cparams("arbitrary"),
        name="moe_combine",
    )(dest, dest, yb, gates, x1, mod3, g_post)


def _moe(h2t, topi, gates, rank, cnt, x1, mod3, g_post, w_gu, b_gu, w_down, b_down, layer, seq):
    t = x1.shape[0]
    n_blocks = t * TOP_K // MOE_BLOCK + N_EXPERTS
    dest_l, blk, meta = _route(topi, rank, cnt, n_blocks, min(1024, t))
    dest = dest_l[:, :TOP_K].reshape(t * TOP_K)
    xb = _dispatch(h2t, dest, meta, n_blocks * MOE_BLOCK, MOE_BLOCK)
    yb = _experts(xb, blk[0, :n_blocks], meta[2, :1], w_gu, b_gu, w_down, b_down, layer)
    return _combine(yb, dest, gates, x1, mod3, g_post, seq, MOE_BLOCK)


TOKEN_TILE = 512
FOX_Q_TILE = 256
FOX_K_TILE = 256


def kernel(x, c, ada_w, ada_b, norm_pre_mix, norm_post_mix, norm_pre_ffn, norm_post_ffn, ev_w_in, fox_b_f, s5_lam_re, s5_lam_im, s5_log_dt, s5_b_re, s5_b_im, s5_c_re, s5_c_im, s5_d, s5_glu_w, s5_glu_b, ev_w_out, od_w_in, gla_w_up, gla_b_gate, od_w_out, router_w, router_b, exp_w_gu, exp_b_gu, exp_w_down, exp_b_down):
    bsz, seq, d = x.shape
    t = bsz * seq
    tm = min(TOKEN_TILE, seq)
    x2 = x.reshape(t, d)
    mod = _modulation(c, ada_w, ada_b)
    for l in range(DEPTH):
        i = l // 2
        mod3 = mod[l].reshape(bsz, 1, 6 * d)
        row = lambda a: a[l].reshape(1, -1)
        tail = (x2, mod3, row(norm_post_mix), row(norm_pre_ffn), router_w[l], router_b[l])
        if l % 2 == 0:
            u, q, k, v = _in_even(x2, mod3, row(norm_pre_mix), ev_w_in[i], fox_b_f[i], seq, tm)
            tables = _s5_tables(s5_lam_re[i], s5_lam_im[i], s5_log_dt[i], s5_b_re[i], s5_b_im[i],
                                s5_c_re[i], s5_c_im[i])
            ys = _s5_scan(u, bsz, seq, tables)
            yb = _fox(q, k, v, bsz, seq, min(FOX_Q_TILE, seq), min(FOX_K_TILE, seq))
            outs = _out_even(ys, u, yb, s5_d[i], s5_glu_w[i], s5_glu_b[i], ev_w_out[i], *tail, seq=seq, tm=tm)
        else:
            rq, rk, rv, sg, gq, gk, gv, sr, la = _in_odd(x2, mod3, row(norm_pre_mix), od_w_in[i],
                                                         gla_w_up[i], gla_b_gate[i], seq, tm)
            yc = _retention(rq, rk, rv, sg, bsz, seq)
            yd = _gla(gq, gk, la, gv, sr, bsz, seq)
            outs = _out_odd(yc, yd, od_w_out[i], *tail, seq=seq, tm=tm)
        x1, h2t, topi, gates, rank, cnt = outs
        x2 = _moe(h2t, topi, gates, rank, cnt, x1, mod3, row(norm_post_ffn),
                  exp_w_gu, exp_b_gu, exp_w_down, exp_b_down, l, seq)
    return x2.reshape(bsz, seq, d)
```

```python
import functools
import math

import jax
import jax.numpy as jnp
from jax import lax
from jax.experimental import pallas as pl
from jax.experimental.pallas import tpu as pltpu

F32 = jnp.float32
BF16 = jnp.bfloat16
HIGHEST = lax.Precision.HIGHEST

D_MODEL = 1024
DEPTH = 2
EPS = 1e-6
S5_CH = 512
S5_GROUP = 16
S5_GROUPS = S5_CH // S5_GROUP
S5_STATE = 64
S5_CHUNK = 8
FOX_HEADS = 8
FOX_DH = 64
FOX_W = FOX_HEADS * FOX_DH
LOG2_E = 1.4426950408889634
FOX_HPS = 8
RET_HEADS = 4
RET_DK = 128
RET_DV = 128
ROPE_BASE = 10000.0
GLA_HEADS = 4
GLA_DK = 64
GLA_DV = 128
GLA_RANK = 16
GLA_TAU = 16.0
GLA_CHUNK = 64
GLA_SUB = 16
N_EXPERTS = 32
TOP_K = 4
D_EXPERT = 1024
SWIGLU_LIMIT = 7.0
SWIGLU_ALPHA = 1.702
MOE_BLOCK = 256
ROW_DMA_UNROLL = 4

LANES = 128
SUBLANES = 8
VMEM_LIMIT = 56 * 1024 * 1024


def _cparams(*sem):
    return pltpu.CompilerParams(dimension_semantics=sem, vmem_limit_bytes=VMEM_LIMIT)


def _bdot(a, b):
    return jnp.dot(a, b, preferred_element_type=F32)


def _dot_nt(a, b):
    return lax.dot_general(a, b, (((1,), (1,)), ((), ())), preferred_element_type=F32)


def _dot_tn(a, b):
    return lax.dot_general(a, b, (((0,), (0,)), ((), ())), preferred_element_type=F32)


def _split3(x):
    hi = x.astype(BF16)
    r = x - hi.astype(F32)
    mid = r.astype(BF16)
    lo = (r - mid.astype(F32)).astype(BF16)
    return hi, mid, lo


def _dot01(m01, x):
    hi, mid, lo = _split3(x)
    return _bdot(m01, hi) + _bdot(m01, mid) + _bdot(m01, lo)


def _lower_tri(n, strict=False):
    r = lax.broadcasted_iota(jnp.int32, (n, n), 0)
    c = lax.broadcasted_iota(jnp.int32, (n, n), 1)
    return ((r > c) if strict else (r >= c)).astype(BF16)


def _log_sigmoid(x):
    return jnp.minimum(x, 0.0) - jnp.log1p(jnp.exp(-jnp.abs(x)))


def _sigmoid(x):
    return 1.0 / (1.0 + jnp.exp(-x))


def _silu(x):
    return x * _sigmoid(x)


def _rms(x):
    return x * lax.rsqrt(jnp.mean(x * x, axis=-1, keepdims=True) + EPS)


def _mod_kernel(c_ref, w_ref, b_ref, o_ref):
    c = c_ref[...]
    o_ref[0] = jnp.dot(_silu(c), w_ref[0], preferred_element_type=F32, precision=HIGHEST) + b_ref[0]


def _modulation(c, ada_w, ada_b):
    depth, d, n = ada_w.shape
    bsz = c.shape[0]
    tn = D_MODEL
    return pl.pallas_call(
        _mod_kernel,
        grid=(depth, n // tn),
        in_specs=[pl.BlockSpec((bsz, d), lambda l, j: (0, 0)),
                  pl.BlockSpec((1, d, tn), lambda l, j: (l, 0, j)),
                  pl.BlockSpec((1, 1, tn), lambda l, j: (l, 0, j))],
        out_specs=pl.BlockSpec((1, bsz, tn), lambda l, j: (l, 0, j)),
        out_shape=jax.ShapeDtypeStruct((depth, bsz, n), F32),
        compiler_params=_cparams("parallel", "parallel"),
        name="adaln_mod",
    )(c, ada_w, ada_b.reshape(depth, 1, n))


def _mod_spec(chunk, tiles_per_seq):
    return pl.BlockSpec((1, 1, D_MODEL), lambda i: (i // tiles_per_seq, 0, chunk))


def _prenorm(x, g_ref, sc_ref, sh_ref):
    return _rms(x) * (g_ref[...] * (1.0 + sc_ref[0])) + sh_ref[0]


def _in_even_kernel(x_ref, sh_ref, sc_ref, g_ref, w_ref, bf_ref,
                    u_ref, q_ref, k_ref, v_ref, carry_ref, *, tiles_per_seq):
    i = pl.program_id(0)
    tm = x_ref.shape[0]
    h = _prenorm(x_ref[...], g_ref, sc_ref, sh_ref)
    z = _bdot(h.astype(BF16), w_ref[...])
    u_ref[...] = z[:, 0:S5_CH]
    v_ref[...] = z[:, S5_CH + 2 * FOX_W:S5_CH + 3 * FOX_W].astype(BF16)
    ls = _log_sigmoid(z[:, S5_CH + 3 * FOX_W:] + bf_ref[...])

    @pl.when(i % tiles_per_seq == 0)
    def _():
        carry_ref[...] = jnp.zeros_like(carry_ref)

    cum = _dot01(_lower_tri(tm), ls) + carry_ref[...]
    carry_ref[...] = cum[tm - 1:tm, :]

    lane = lax.broadcasted_iota(jnp.int32, (1, LANES), 1)
    feat = lane < FOX_DH
    ones = jnp.where(lane < FOX_DH + 3, 1.0, 0.0)
    for hd in range(FOX_HEADS):
        blk = (hd * FOX_DH) // LANES * LANES
        qs = z[:, S5_CH + blk:S5_CH + blk + LANES] * (FOX_DH ** -0.5 * LOG2_E)
        ks = z[:, S5_CH + FOX_W + blk:S5_CH + FOX_W + blk + LANES]
        if (hd * FOX_DH) % LANES:
            qs = pltpu.roll(qs, LANES - FOX_DH, 1)
            ks = pltpu.roll(ks, LANES - FOX_DH, 1)
        nf = jnp.broadcast_to(-LOG2_E * cum[:, hd:hd + 1], (tm, LANES))
        hi = nf.astype(BF16).astype(F32)
        mid = (nf - hi).astype(BF16).astype(F32)
        lo = nf - hi - mid
        bias = jnp.where(lane == FOX_DH, hi, jnp.where(lane == FOX_DH + 1, mid,
                                                       jnp.where(lane == FOX_DH + 2, lo, 0.0)))
        q_ref[:, hd * LANES:(hd + 1) * LANES] = jnp.where(feat, qs, ones).astype(BF16)
        k_ref[:, hd * LANES:(hd + 1) * LANES] = jnp.where(feat, ks, bias).astype(BF16)


def _in_even(x2, mod3, gain, w_in, b_f, seq, tm):
    t, d = x2.shape
    tiles_per_seq = seq // tm
    nw = S5_CH + 3 * FOX_W
    w = jnp.concatenate([w_in[:, :nw], jnp.pad(w_in[:, nw:], ((0, 0), (0, LANES - FOX_HEADS)))],
                        axis=1).astype(BF16)
    bf = jnp.pad(b_f, (0, LANES - FOX_HEADS)).reshape(1, LANES)
    tok = lambda n: pl.BlockSpec((tm, n), lambda i: (i, 0))
    full = lambda a: pl.BlockSpec(a.shape, lambda i: (0,) * a.ndim)
    return pl.pallas_call(
        functools.partial(_in_even_kernel, tiles_per_seq=tiles_per_seq),
        grid=(t // tm,),
        in_specs=[tok(d), _mod_spec(0, tiles_per_seq), _mod_spec(1, tiles_per_seq),
                  full(gain), full(w), full(bf)],
        out_specs=[tok(S5_CH), tok(FOX_HEADS * LANES), tok(FOX_HEADS * LANES), tok(FOX_W)],
        out_shape=[jax.ShapeDtypeStruct((t, S5_CH), F32),
                   jax.ShapeDtypeStruct((t, FOX_HEADS * LANES), BF16),
                   jax.ShapeDtypeStruct((t, FOX_HEADS * LANES), BF16),
                   jax.ShapeDtypeStruct((t, FOX_W), BF16)],
        scratch_shapes=[pltpu.VMEM((1, LANES), F32)],
        compiler_params=_cparams("arbitrary"),
        name="in_proj_even",
    )(x2, mod3, mod3, gain, w, bf)


S5_TILE_GROUPS = LANES // S5_GROUP
S5_SEQ_PARTS = 4


def _s5_kernel(u_ref, wt_ref, ws_ref, wc_ref, a_ref, y_ref, x_scr, e_scr, hp_scr, h_scr, *, nb, ncl):
    ell = S5_CHUNK
    sw = S5_TILE_GROUPS * 2 * S5_STATE

    @pl.when(pl.program_id(1) == 0)
    def _():
        h_scr[...] = jnp.zeros_like(h_scr)

    blk = nb * SUBLANES
    for hi in range(ncl // SUBLANES):
        for b in range(nb):
            for t in range(ell):
                x_scr[hi * blk + b * SUBLANES:hi * blk + (b + 1) * SUBLANES, t * LANES:(t + 1) * LANES] = (
                    u_ref[b, pl.ds(hi * SUBLANES * ell + t, SUBLANES, stride=ell), :])
    x = x_scr[...].astype(BF16)
    e = _bdot(x, ws_ref[0])
    tg = S5_TILE_GROUPS
    for j in range(tg):
        e_scr[j] = e[:, j * LANES:(j + 1) * LANES]
        e_scr[tg + j] = pltpu.roll(e[:, j * LANES:(j + 1) * LANES], S5_STATE, 1)
    a1 = jnp.broadcast_to(a_ref[0, 0:1, :], (nb, sw))
    a2 = jnp.broadcast_to(a_ref[0, 1:2, :], (nb, sw))
    a2s = jnp.broadcast_to(a_ref[0, 2:3, :], (nb, sw))

    def body(c, carry):
        h, hs = carry
        rows_c = pl.ds((c // SUBLANES) * blk + c % SUBLANES, nb, stride=SUBLANES)
        for j in range(tg):
            hp_scr[j, rows_c, :] = h[:, j * LANES:(j + 1) * LANES]
        e1 = jnp.concatenate([e_scr[j, rows_c, :] for j in range(tg)], axis=1)
        e2 = jnp.concatenate([e_scr[tg + j, rows_c, :] for j in range(tg)], axis=1)
        return a1 * h + a2 * hs + e1, a1 * hs + a2s * h + e2

    h, hs = lax.fori_loop(0, ncl, body, (h_scr[0], h_scr[1]))
    h_scr[0] = h
    h_scr[1] = hs
    hp = jnp.concatenate([hp_scr[j] for j in range(tg)], axis=1).astype(BF16)
    y = _bdot(x, wt_ref[0]) + _bdot(hp, wc_ref[0])
    for hi in range(ncl // SUBLANES):
        for b in range(nb):
            for t in range(ell):
                y_ref[b, pl.ds(hi * SUBLANES * ell + t, SUBLANES, stride=ell), :] = (
                    y[hi * blk + b * SUBLANES:hi * blk + (b + 1) * SUBLANES, t * LANES:(t + 1) * LANES])


def _s5_tables(lam_re, lam_im, log_dt, b_re, b_im, c_re, c_im):
    ell, p, g = S5_CHUNK, S5_STATE, S5_GROUPS
    lr, li = lam_re.astype(F32), lam_im.astype(F32)
    dt = jnp.exp(log_dt.astype(F32))[:, None]
    mag = jnp.exp(lr * dt)
    a_re, a_im = mag * jnp.cos(li * dt), mag * jnp.sin(li * dt)
    den = lr * lr + li * li
    n_re, n_im = a_re - 1.0, a_im
    z_re = (n_re * lr + n_im * li) / den
    z_im = (n_im * lr - n_re * li) / den
    br, bi = b_re.astype(F32), b_im.astype(F32)
    bb_re = z_re[..., None] * br - z_im[..., None] * bi
    bb_im = z_re[..., None] * bi + z_im[..., None] * br
    j = jnp.arange(ell + 1, dtype=F32)[:, None, None]
    pmag = jnp.exp(j * (lr * dt)[None])
    pr, pi = pmag * jnp.cos(j * (li * dt)[None]), pmag * jnp.sin(j * (li * dt)[None])
    w_re = pr[..., None] * bb_re[None] - pi[..., None] * bb_im[None]
    w_im = pr[..., None] * bb_im[None] + pi[..., None] * bb_re[None]
    cr, ci = c_re.astype(F32), c_im.astype(F32)
    kern = (jnp.einsum('gcp,jgpd->jgcd', cr, w_re[:ell], precision=HIGHEST)
            - jnp.einsum('gcp,jgpd->jgcd', ci, w_im[:ell], precision=HIGHEST))
    s_idx = jnp.arange(ell)[:, None]
    t_idx = jnp.arange(ell)[None, :]
    lag = jnp.clip(t_idx - s_idx, 0, ell - 1)
    toep = kern[lag]
    toep = jnp.where((t_idx >= s_idx)[:, :, None, None, None], toep, 0.0)
    toep = toep.transpose(2, 0, 4, 1, 3)
    rev = jnp.arange(ell - 1, -1, -1)
    st_re = w_re[rev].transpose(1, 0, 3, 2)
    st_im = w_im[rev].transpose(1, 0, 3, 2)
    wst = jnp.concatenate([st_re, st_im], -1)
    p1r, p1i = pr[1:], pi[1:]
    c_hr = cr[None] * p1r[:, :, None, :] - ci[None] * p1i[:, :, None, :]
    c_hi = -cr[None] * p1i[:, :, None, :] - ci[None] * p1r[:, :, None, :]
    cst = jnp.concatenate([c_hr, c_hi], axis=-1).transpose(1, 3, 0, 2)
    al_r, al_i = pr[ell], pi[ell]
    a_rows = jnp.stack([jnp.concatenate([al_r, al_r], -1), jnp.concatenate([-al_i, al_i], -1),
                        jnp.concatenate([al_i, -al_i], -1)], axis=1)

    tg = S5_TILE_GROUPS
    nt = g // tg
    tile = lambda a: a.reshape((nt, tg) + a.shape[1:])
    lane_grp = jnp.arange(LANES) // S5_GROUP
    step_lane_grp = jnp.tile(lane_grp, ell)
    spread_out = (jnp.eye(ell, dtype=BF16)[:, None, :, None, None]
                  * jnp.eye(S5_GROUP, dtype=BF16)[None, :, None, None, :]
                  * jnp.ones((1, 1, 1, tg, 1), BF16)).reshape(ell * S5_GROUP, ell * LANES)
    toep_t = tile(toep).transpose(0, 2, 1, 3, 4, 5).reshape(nt, ell, LANES, ell * S5_GROUP).astype(BF16)
    wt = jnp.einsum('ksrm,mn->ksrn', toep_t, spread_out)
    wt = jnp.where((lane_grp[:, None] == step_lane_grp[None, :])[None, None], wt, 0)
    wt = wt.reshape(nt, ell * LANES, ell * LANES)
    spread_state = jnp.tile(jnp.eye(2 * p, dtype=BF16), (1, tg))
    state_grp = jnp.arange(tg * 2 * p) // (2 * p)
    wst_t = tile(wst).transpose(0, 2, 1, 3, 4).reshape(nt, ell, LANES, 2 * p).astype(BF16)
    ws = jnp.einsum('ksrp,pn->ksrn', wst_t, spread_state)
    ws = jnp.where((lane_grp[:, None] == state_grp[None, :])[None, None], ws, 0)
    ws = ws.reshape(nt, ell * LANES, tg * 2 * p)
    cst_t = tile(cst).reshape(nt, tg * 2 * p, ell * S5_GROUP).astype(BF16)
    wc = jnp.einsum('krm,mn->krn', cst_t, spread_out)
    wc = jnp.where((state_grp[:, None] == step_lane_grp[None, :])[None], wc, 0)
    a_t = tile(a_rows).transpose(0, 2, 1, 3).reshape(nt, 3, tg * 2 * p)
    a_t = jnp.pad(a_t, ((0, 0), (0, SUBLANES - 3), (0, 0)))
    return wt, ws, wc, a_t


def _s5_scan(u, bsz, seq, tables):
    wt, ws, wc, a_t = tables
    ell = S5_CHUNK
    nt = S5_GROUPS // S5_TILE_GROUPS
    parts = S5_SEQ_PARTS if seq % (S5_SEQ_PARTS * ell * 2 * SUBLANES) == 0 else 1
    ncl = seq // parts // ell
    rows = bsz * ncl
    sw = S5_TILE_GROUPS * 2 * S5_STATE
    u3 = u.reshape(bsz, seq, S5_CH)
    io = pl.BlockSpec((bsz, seq // parts, LANES), lambda k, s: (0, s, k))
    per_tile = lambda a: pl.BlockSpec((1,) + a.shape[1:], lambda k, s: (k, 0, 0))
    y = pl.pallas_call(
        functools.partial(_s5_kernel, nb=bsz, ncl=ncl),
        grid=(nt, parts),
        in_specs=[io, per_tile(wt), per_tile(ws), per_tile(wc), per_tile(a_t)],
        out_specs=io,
        out_shape=jax.ShapeDtypeStruct((bsz, seq, S5_CH), F32),
        scratch_shapes=[pltpu.VMEM((rows, ell * LANES), F32),
                        pltpu.VMEM((2 * S5_TILE_GROUPS, rows, LANES), F32),
                        pltpu.VMEM((S5_TILE_GROUPS, rows, LANES), F32),
                        pltpu.VMEM((2, bsz, sw), F32)],
        compiler_params=_cparams("parallel", "arbitrary"),
        name="s5_scan",
    )(u3, wt, ws, wc, a_t)
    return y.reshape(bsz * seq, S5_CH)


def _fox_kernel(q_ref, k_ref, vt_ref, o_ref, *, tq, tk):
    i = pl.program_id(2)
    nh = FOX_HPS
    q = [q_ref[:, h * LANES:(h + 1) * LANES] for h in range(nh)]
    key = lax.broadcasted_iota(jnp.int32, (tk, tq), 0)
    qry = lax.broadcasted_iota(jnp.int32, (tk, tq), 1)
    per_q = tq // tk

    def block(j0, carry, mask):
        kj = k_ref[pl.ds(j0, tk), :]
        vtj = vt_ref[:, pl.ds(j0, tk)]
        ss = [_dot_nt(kj[:, h * LANES:(h + 1) * LANES], q[h]) for h in range(nh)]
        stats = []
        for h in range(nh):
            m, l, acc = carry[h]
            s = ss[h] if mask is None else jnp.where(mask, ss[h], -jnp.inf)
            m_new = jnp.maximum(m, jnp.max(s, axis=0, keepdims=True))
            p = jnp.exp2(s - m_new)
            alpha = jnp.exp2(m - m_new)
            stats.append((m_new, alpha * l + jnp.sum(p, axis=0, keepdims=True), alpha, p.astype(BF16)))
        out = []
        for h in range(nh):
            m_new, l, alpha, p = stats[h]
            acc = alpha * carry[h][2] + _bdot(vtj[h * FOX_DH:(h + 1) * FOX_DH, :], p)
            out.append((m_new, l, acc))
        return tuple(out)

    init = tuple((jnp.full((1, tq), -jnp.inf, F32), jnp.zeros((1, tq), F32), jnp.zeros((FOX_DH, tq), F32))
                 for _ in range(nh))
    carry = lax.fori_loop(0, i * per_q, lambda j, c: block(pl.multiple_of(j * tk, tk), c, None), init)
    for d in range(per_q):
        carry = block(pl.multiple_of(i * tq + d * tk, tk), carry, key + d * tk <= qry)
    for g in range(nh // 2):
        o_t = jnp.concatenate([carry[h][2] / carry[h][1] for h in (2 * g, 2 * g + 1)], axis=0)
        o_ref[:, g * LANES:(g + 1) * LANES] = o_t.T.astype(o_ref.dtype)


def _fox(q_aug, k_aug, v, bsz, seq, tq, tk):
    t = v.shape[0]
    nh = FOX_HPS
    groups = FOX_HEADS // nh
    nq = seq // tq
    v_t = v.reshape(bsz, seq, FOX_W).transpose(0, 2, 1).reshape(bsz * FOX_W, seq)
    return pl.pallas_call(
        functools.partial(_fox_kernel, tq=tq, tk=tk),
        grid=(bsz, groups, nq),
        in_specs=[pl.BlockSpec((tq, nh * LANES), lambda b, p, i: (b * nq + i, p)),
                  pl.BlockSpec((seq, nh * LANES), lambda b, p, i: (b, p)),
                  pl.BlockSpec((nh * FOX_DH, seq), lambda b, p, i: (b * groups + p, 0))],
        out_specs=pl.BlockSpec((tq, nh * FOX_DH), lambda b, p, i: (b * nq + i, p)),
        out_shape=jax.ShapeDtypeStruct((t, FOX_W), BF16),
        compiler_params=_cparams("parallel", "parallel", "arbitrary"),
        name="fox_attention",
    )(q_aug, k_aug, v_t)


_ODD_COLS = (("rq", 512), ("rk", 512), ("rv", 512), ("rg", 512), ("gq", 256), ("gk", 256),
             ("gv", 512), ("gr", 512), ("glr", LANES))


def _odd_offsets():
    off, out = 0, {}
    for name, w in _ODD_COLS:
        out[name] = (off, off + w)
        off += w
    return out, off


def _in_odd_kernel(x_ref, sh_ref, sc_ref, g_ref, w_ref, cos_ref, sin_ref, wup_ref, bg_ref,
                   rq_ref, rk_ref, rv_ref, sg_ref, gq_ref, gk_ref, gv_ref, sr_ref, la_ref):
    h = _prenorm(x_ref[...], g_ref, sc_ref, sh_ref)
    z = _bdot(h.astype(BF16), w_ref[...])
    off, _ = _odd_offsets()
    col = lambda n: z[:, off[n][0]:off[n][1]]
    cos, sin = cos_ref[...], sin_ref[...]

    def rope(t, scale):
        heads = []
        for hd in range(RET_HEADS):
            th = t[:, hd * RET_DK:(hd + 1) * RET_DK]
            heads.append((th * cos + pltpu.roll(th, RET_DK // 2, 1) * sin) * scale)
        return jnp.concatenate(heads, axis=1).astype(BF16)

    rq_ref[...] = rope(col("rq"), 1.0)
    rk_ref[...] = rope(col("rk"), RET_DK ** -0.5)
    rv_ref[...] = col("rv").astype(BF16)
    sg_ref[...] = _silu(col("rg"))
    gq_ref[...] = col("gq") * (GLA_DK ** -0.5)
    gk_ref[...] = col("gk")
    gv_ref[...] = col("gv").astype(BF16)
    sr_ref[...] = _silu(col("gr"))
    gate = jnp.dot(col("glr"), wup_ref[...], preferred_element_type=F32, precision=HIGHEST) + bg_ref[...]
    la_ref[...] = _log_sigmoid(gate) * (1.0 / GLA_TAU)


def _in_odd(x2, mod3, gain, w_in, w_up, b_gate, seq, tm):
    t, d = x2.shape
    tps = seq // tm
    ref_w = (512, 512, 512, 512, 256, 256, 512, GLA_RANK, 512)
    starts = [0]
    for wd in ref_w:
        starts.append(starts[-1] + wd)
    seg = lambda j: w_in[:, starts[j]:starts[j + 1]]
    w = jnp.concatenate([seg(0), seg(1), seg(2), seg(3), seg(4), seg(5), seg(6), seg(8),
                         jnp.pad(seg(7), ((0, 0), (0, LANES - GLA_RANK)))], axis=1).astype(BF16)
    wup = jnp.pad(w_up.astype(F32), ((0, LANES - GLA_RANK), (0, 0)))
    bg = b_gate.reshape(1, -1).astype(F32)
    half = RET_DK // 2
    inv = ROPE_BASE ** (-jnp.arange(half, dtype=F32) / half)
    ang = jnp.arange(seq, dtype=F32)[:, None] * inv[None, :]
    cos = jnp.concatenate([jnp.cos(ang), jnp.cos(ang)], axis=1)
    sin = jnp.concatenate([-jnp.sin(ang), jnp.sin(ang)], axis=1)
    tok = lambda n: pl.BlockSpec((tm, n), lambda i: (i, 0))
    full = lambda a: pl.BlockSpec(a.shape, lambda i: (0,) * a.ndim)
    pos = pl.BlockSpec((tm, RET_DK), lambda i: (i % tps, 0))
    widths = (512, 512, 512, 512, 256, 256, 512, 512, 256)
    dtypes = (BF16, BF16, BF16, F32, F32, F32, BF16, F32, F32)
    return pl.pallas_call(
        _in_odd_kernel,
        grid=(t // tm,),
        in_specs=[tok(d), _mod_spec(0, tps), _mod_spec(1, tps), full(gain), full(w), pos, pos,
                  full(wup), full(bg)],
        out_specs=[tok(n) for n in widths],
        out_shape=[jax.ShapeDtypeStruct((t, n), dt) for n, dt in zip(widths, dtypes)],
        compiler_params=_cparams("parallel"),
        name="in_proj_odd",
    )(x2, mod3, mod3, gain, w, cos, sin, wup, bg)


RET_CHUNK = 256


def _ret_kernel(q_ref, k_ref, v_ref, sg_ref, dm_ref, xi_ref, zeta_ref, gl_ref, y_ref, st_ref):
    @pl.when(pl.program_id(1) == 0)
    def _():
        st_ref[...] = jnp.zeros_like(st_ref)

    nb = q_ref.shape[0]
    chains = [(bl, h) for bl in range(nb) for h in range(RET_HEADS)]
    col = lambda h: slice(h * RET_DK, (h + 1) * RET_DK)
    q = [q_ref[bl, :, col(h)] for bl, h in chains]
    k = [k_ref[bl, :, col(h)] for bl, h in chains]
    v = [v_ref[bl, :, col(h)] for bl, h in chains]
    st = [st_ref[c] for c in range(len(chains))]
    s = [_dot_nt(q[c], k[c]) for c in range(len(chains))]
    inter = [_bdot((q[c].astype(F32) * xi_ref[h]).astype(BF16), st[c].astype(BF16))
             for c, (bl, h) in enumerate(chains)]
    upd = [_dot_tn((k[c].astype(F32) * zeta_ref[h]).astype(BF16), v[c]) for c, (bl, h) in enumerate(chains)]
    for c, (bl, h) in enumerate(chains):
        o = _bdot((s[c] * dm_ref[h]).astype(BF16), v[c]) + inter[c]
        st_ref[c] = gl_ref[h, 0:1, :] * st[c] + upd[c]
        y_ref[bl, :, col(h)] = (sg_ref[bl, :, col(h)] * _rms(o)).astype(y_ref.dtype)


RET_BATCHES = 2


def _retention(rq, rk, rv, sg, bsz, seq):
    t = rq.shape[0]
    ell = min(RET_CHUNK, seq)
    nc = seq // ell
    log_g = jnp.log(1.0 - jnp.exp2(-5.0 - jnp.arange(RET_HEADS, dtype=F32)))
    idx = jnp.arange(ell, dtype=F32)
    rel = idx[:, None] - idx[None, :]
    dmat = jnp.where(rel >= 0, jnp.exp(log_g[:, None, None] * jnp.maximum(rel, 0.0)), 0.0)
    lanes = lambda a: jnp.broadcast_to(a[..., None], a.shape + (RET_DK,))
    xi = lanes(jnp.exp(log_g[:, None] * (idx + 1.0)))
    zeta = lanes(jnp.exp(log_g[:, None] * (ell - 1.0 - idx)))
    gl = jnp.broadcast_to(jnp.exp(log_g * ell)[:, None, None], (RET_HEADS, SUBLANES, RET_DV))
    nb = min(RET_BATCHES, bsz)
    width = RET_HEADS * RET_DK
    r3 = lambda a: a.reshape(bsz, seq, width)
    blk = pl.BlockSpec((nb, ell, width), lambda b, c: (b, c, 0))
    full = lambda a: pl.BlockSpec(a.shape, lambda b, c: (0, 0, 0))
    y = pl.pallas_call(
        _ret_kernel,
        grid=(bsz // nb, nc),
        in_specs=[blk, blk, blk, blk, full(dmat), full(xi), full(zeta), full(gl)],
        out_specs=blk,
        out_shape=jax.ShapeDtypeStruct((bsz, seq, width), BF16),
        scratch_shapes=[pltpu.VMEM((nb * RET_HEADS, RET_DK, RET_DV), F32)],
        compiler_params=_cparams("parallel", "arbitrary"),
        name="retention",
    )(r3(rq), r3(rk), r3(rv), r3(sg), dmat, xi, zeta, gl)
    return y.reshape(t, width)


def _gla_kernel(q_ref, k_ref, la_ref, v_ref, sg_ref, y_ref, st_ref, b_scr, v_scr, p_scr, r_scr):
    @pl.when(pl.program_id(1) == 0)
    def _():
        st_ref[...] = jnp.zeros_like(st_ref)

    ell, sub = GLA_CHUNK, GLA_SUB
    n_sub = ell // sub
    nb = q_ref.shape[0]
    pairs = GLA_HEADS // 2
    streams = [(bl, p) for bl in range(nb) for p in range(pairs)]
    lane = lax.broadcasted_iota(jnp.int32, (1, LANES), 1)
    first = lane < GLA_DK
    head = (first, jnp.logical_not(first))
    pick = lambda h, a: jnp.where(head[h], a, 0.0).astype(BF16)
    tri = _lower_tri(ell)
    tau = lax.broadcasted_iota(jnp.int32, (sub, LANES), 0)
    row_of, per_sub = [], 0
    for s_ in range(sub):
        row_of.append(per_sub)
        per_sub += sub - (s_ // SUBLANES) * SUBLANES
    rsub = lax.broadcasted_iota(jnp.int32, (LANES, 2 * LANES), 0)
    csub = lax.broadcasted_iota(jnp.int32, (LANES, 2 * LANES), 1)
    ind = ((rsub < GLA_DK) == (csub < LANES)).astype(BF16)

    val = {}
    for sid, (bl, p) in enumerate(streams):
        qk = slice(p * LANES, (p + 1) * LANES)
        q, k = q_ref[bl, :, qk], k_ref[bl, :, qk]
        b = _dot01(tri, la_ref[bl, :, qk])
        b_scr[sid] = b
        v_bf = v_ref[bl, :, p * 2 * GLA_DV:(p + 1) * 2 * GLA_DV]
        v_scr[sid] = v_bf.astype(F32)
        st = st_ref[sid]
        val[sid] = dict(q=q, k=k, b=b, st=st, vh=[v_bf[:, h * GLA_DV:(h + 1) * GLA_DV] for h in range(2)])

    for sid in val:
        d = val[sid]
        qe = d["q"] * jnp.exp(d["b"])
        st_bf = d["st"].astype(BF16)
        d["o"] = [_dot_nt(pick(h, qe), st_bf) for h in range(2)]

    row = lax.broadcasted_iota(jnp.int32, (ell, LANES), 0)
    for sid in val:
        d = val[sid]
        q, k, b = d["q"], d["k"], d["b"]
        qa, ka = [], []
        for i in range(1, n_sub):
            lo = i * sub
            ref_row = b[lo - 1:lo, :]
            in_i = (row >= lo) & (row < lo + sub)
            qa.append(jnp.where(in_i, q * jnp.exp(jnp.minimum(b - ref_row, 0.0)), 0.0))
            ka.append(jnp.where(row < lo, k * jnp.exp(jnp.minimum(ref_row - b, 0.0)), 0.0))
        k_cat = jnp.concatenate(ka, axis=1).astype(BF16)
        d["a_off"] = [_dot_nt(jnp.concatenate([pick(h, x) for x in qa], axis=1), k_cat) for h in range(2)]
    for sid in val:
        d = val[sid]
        d["off"] = [_bdot(d["a_off"][h].astype(BF16), d["vh"][h]) for h in range(2)]

    for sid, (bl, p) in enumerate(streams):
        d = val[sid]
        q, b = d["q"], d["b"]
        for i in range(n_sub):
            lo = i * sub
            qi, bi = q[lo:lo + sub], b[lo:lo + sub]
            for s in range(sub):
                k_row = k_ref[bl, pl.ds(lo + s, 1), p * LANES:(p + 1) * LANES]
                b_row = b_scr[sid, pl.ds(lo + s, 1), :]
                r0 = (s // SUBLANES) * SUBLANES
                w = jnp.exp(jnp.minimum(bi[r0:] - b_row, 0.0))
                tile_s = jnp.where(tau[r0:] >= s, qi[r0:] * k_row * w, 0.0)
                p_scr[sid, pl.ds(i * per_sub + row_of[s], sub - r0), :] = tile_s
    for sid in val:
        r_scr[sid] = _bdot(p_scr[sid].astype(BF16), ind)
    for sid in val:
        diag = [[], []]
        for i in range(n_sub):
            lo = i * sub
            for h in range(2):
                acc = [jnp.zeros((SUBLANES, GLA_DV), F32) for _ in range(sub // SUBLANES)]
                for s in range(sub):
                    v_row = v_scr[sid, pl.ds(lo + s, 1), h * GLA_DV:(h + 1) * GLA_DV]
                    for part in range(s // SUBLANES, sub // SUBLANES):
                        rows = pl.ds(i * per_sub + row_of[s] + (part - s // SUBLANES) * SUBLANES, SUBLANES)
                        acc[part] = acc[part] + r_scr[sid, rows, h * LANES:(h + 1) * LANES] * v_row
                diag[h].append(jnp.concatenate(acc, axis=0))
        val[sid]["diag"] = diag

    for sid, (bl, p) in enumerate(streams):
        d = val[sid]
        b_last = d["b"][ell - 1:ell, :]
        kh = (d["k"] * jnp.exp(b_last - d["b"])).astype(BF16)
        upd = [_dot_tn(d["vh"][h], kh) for h in range(2)]
        st_ref[sid] = d["st"] * jnp.exp(b_last) + jnp.where(first, upd[0], upd[1])
        for h in range(2):
            oh = d["o"][h] + d["off"][h] + jnp.concatenate(d["diag"][h], axis=0)
            cols = slice((2 * p + h) * GLA_DV, (2 * p + h + 1) * GLA_DV)
            y_ref[bl, :, cols] = (sg_ref[bl, :, cols] * _rms(oh)).astype(y_ref.dtype)


GLA_BATCHES = 4


def _gla(gq, gk, la, gv, sr, bsz, seq):
    t = gq.shape[0]
    ell = GLA_CHUNK
    nc = seq // ell
    nb = min(GLA_BATCHES, bsz)
    ns = nb * (GLA_HEADS // 2)
    prod_rows = (ell // GLA_SUB) * sum(GLA_SUB - (s // SUBLANES) * SUBLANES for s in range(GLA_SUB))
    r3 = lambda a: a.reshape(bsz, seq, a.shape[1])
    spec = lambda w: pl.BlockSpec((nb, ell, w), lambda b, c: (b, c, 0))
    wq, wv = GLA_HEADS * GLA_DK, GLA_HEADS * GLA_DV
    y = pl.pallas_call(
        _gla_kernel,
        grid=(bsz // nb, nc),
        in_specs=[spec(wq), spec(wq), spec(wq), spec(wv), spec(wv)],
        out_specs=spec(wv),
        out_shape=jax.ShapeDtypeStruct((bsz, seq, wv), BF16),
        scratch_shapes=[pltpu.VMEM((ns, GLA_DV, LANES), F32),
                        pltpu.VMEM((ns, ell, LANES), F32),
                        pltpu.VMEM((ns, ell, 2 * GLA_DV), F32),
                        pltpu.VMEM((ns, prod_rows, LANES), F32),
                        pltpu.VMEM((ns, prod_rows, 2 * LANES), F32)],
        compiler_params=_cparams("parallel", "arbitrary"),
        name="gla",
    )(r3(gq), r3(gk), r3(la), r3(gv), r3(sr))
    return y.reshape(t, wv)


def _post_tail(m, x_ref, g1_ref, gpost_ref, sh2_ref, sc2_ref, gpre_ref, rw_ref, rb_ref,
               x1_ref, h2_ref, topi_ref, gate_ref, rank_ref, cnt_ref, carry_ref):
    i = pl.program_id(0)
    tm = m.shape[0]
    x1 = x_ref[...] + g1_ref[0] * (_rms(m) * gpost_ref[...])
    x1_ref[...] = x1
    h2 = _rms(x1) * (gpre_ref[...] * (1.0 + sc2_ref[0])) + sh2_ref[0]
    for c in range(D_MODEL // LANES):
        h2_ref[pl.ds(c, tm, stride=SUBLANES), :] = h2[:, c * LANES:(c + 1) * LANES]
    h_hi = h2.astype(BF16)
    h_lo = (h2 - h_hi.astype(F32)).astype(BF16)
    w_both = rw_ref[...]
    hw = _bdot(h_hi, w_both)
    logits = hw[:, :LANES] + hw[:, LANES:] + _bdot(h_lo, w_both[:, :LANES]) + rb_ref[...]

    lane = lax.broadcasted_iota(jnp.int32, (tm, LANES), 1)
    lanef = lane.astype(F32)
    work = logits
    topv = jnp.full((tm, LANES), -jnp.inf, F32)
    topi = jnp.zeros((tm, LANES), F32)
    onehot = jnp.zeros((tm, LANES), F32)
    hits = []
    for k in range(TOP_K):
        mx = jnp.max(work, axis=-1, keepdims=True)
        idx = jnp.min(jnp.where(work == mx, lanef, float(LANES)), axis=-1, keepdims=True)
        hit = lanef == idx
        hits.append(hit)
        topv = jnp.where(lane == k, mx, topv)
        topi = jnp.where(lane == k, idx, topi)
        onehot = onehot + hit.astype(F32)
        work = jnp.where(hit, -jnp.inf, work)
    e = jnp.exp(topv - jnp.max(topv, axis=-1, keepdims=True))
    gate_ref[...] = e / jnp.sum(e, axis=-1, keepdims=True)
    topi_ref[...] = topi.astype(jnp.int32)

    @pl.when(i == 0)
    def _():
        carry_ref[...] = jnp.zeros_like(carry_ref)

    before = _bdot(_lower_tri(tm, strict=True), onehot.astype(BF16)) + carry_ref[...]
    rank = jnp.zeros((tm, LANES), F32)
    for k in range(TOP_K):
        rk = jnp.sum(jnp.where(hits[k], before, 0.0), axis=-1, keepdims=True)
        rank = jnp.where(lane == k, rk, rank)
    rank_ref[...] = rank.astype(jnp.int32)
    total = before[tm - 1:tm, :] + onehot[tm - 1:tm, :]
    carry_ref[...] = total
    cnt_ref[...] = jnp.broadcast_to(total, cnt_ref.shape)


def _out_even_kernel(ys_ref, u_ref, yb_ref, d_ref, gw_ref, gb_ref, wa_ref, wb_ref, *rest):
    y = ys_ref[...] + d_ref[...] * u_ref[...]
    g = jax.nn.gelu(y)
    ya = g * _sigmoid(_bdot(g.astype(BF16), gw_ref[...]) + gb_ref[...])
    m = _bdot(ya.astype(BF16), wa_ref[...]) + _bdot(yb_ref[...], wb_ref[...])
    _post_tail(m, *rest)


def _out_odd_kernel(yc_ref, yd_ref, wa_ref, wb_ref, *rest):
    m = _bdot(yc_ref[...], wa_ref[...]) + _bdot(yd_ref[...], wb_ref[...])
    _post_tail(m, *rest)


def _mixer_out(body, mix_args, mix_specs, x2, mod3, g_post, g_pre, router_w, router_b, seq, tm):
    t, d = x2.shape
    tps = seq // tm
    rw32 = jnp.pad(router_w.astype(F32), ((0, 0), (0, LANES - N_EXPERTS)))
    rw_hi = rw32.astype(BF16)
    rw = jnp.concatenate([rw_hi, (rw32 - rw_hi.astype(F32)).astype(BF16)], axis=1)
    rb = jnp.pad(router_b.astype(F32), (0, LANES - N_EXPERTS), constant_values=-1e30).reshape(1, LANES)
    tok = lambda n: pl.BlockSpec((tm, n), lambda i: (i, 0))
    full = lambda a: pl.BlockSpec(a.shape, lambda i: (0,) * a.ndim)
    tail_args = [x2, mod3, g_post, mod3, mod3, g_pre, rw, rb]
    tail_specs = [tok(d), _mod_spec(2, tps), full(g_post), _mod_spec(3, tps), _mod_spec(4, tps),
                  full(g_pre), full(rw), full(rb)]
    return pl.pallas_call(
        body,
        grid=(t // tm,),
        in_specs=mix_specs + tail_specs,
        out_specs=[tok(d), pl.BlockSpec((tm * SUBLANES, LANES), lambda i: (i, 0)),
                   tok(LANES), tok(LANES), tok(LANES), pl.BlockSpec((SUBLANES, LANES), lambda i: (0, 0))],
        out_shape=[jax.ShapeDtypeStruct((t, d), F32),
                   jax.ShapeDtypeStruct((t * SUBLANES, LANES), F32),
                   jax.ShapeDtypeStruct((t, LANES), jnp.int32),
                   jax.ShapeDtypeStruct((t, LANES), F32),
                   jax.ShapeDtypeStruct((t, LANES), jnp.int32),
                   jax.ShapeDtypeStruct((SUBLANES, LANES), F32)],
        scratch_shapes=[pltpu.VMEM((1, LANES), F32)],
        compiler_params=_cparams("arbitrary"),
        name="mixer_out_router",
    )(*mix_args, *tail_args)


def _out_even(ys, u, yb, d_skip, glu_w, glu_b, w_out, *tail, seq, tm):
    tok = lambda n: pl.BlockSpec((tm, n), lambda i: (i, 0))
    full = lambda a: pl.BlockSpec(a.shape, lambda i: (0,) * a.ndim)
    args = [ys, u, yb, d_skip.reshape(1, -1), glu_w.astype(BF16), glu_b.reshape(1, -1),
            w_out[:S5_CH].astype(BF16), w_out[S5_CH:].astype(BF16)]
    specs = [tok(S5_CH), tok(S5_CH), tok(FOX_W)] + [full(a) for a in args[3:]]
    return _mixer_out(_out_even_kernel, args, specs, *tail, seq, tm)


def _out_odd(yc, yd, w_out, *tail, seq, tm):
    tok = lambda n: pl.BlockSpec((tm, n), lambda i: (i, 0))
    full = lambda a: pl.BlockSpec(a.shape, lambda i: (0,) * a.ndim)
    nc = yc.shape[1]
    args = [yc, yd, w_out[:nc].astype(BF16), w_out[nc:].astype(BF16)]
    specs = [tok(nc), tok(yd.shape[1])] + [full(a) for a in args[2:]]
    return _mixer_out(_out_odd_kernel, args, specs, *tail, seq, tm)


def _route_kernel(topi_ref, rank_ref, cnt_ref, dest_ref, blk_ref, meta_ref):
    tm = topi_ref.shape[0]
    cnt = cnt_ref[...]
    padded = jnp.floor((cnt + (MOE_BLOCK - 1.0)) * (1.0 / MOE_BLOCK)) * MOE_BLOCK
    r = lax.broadcasted_iota(jnp.int32, (LANES, LANES), 0)
    c = lax.broadcasted_iota(jnp.int32, (LANES, LANES), 1)
    hi, mid, lo = _split3(padded)
    incl = (r <= c).astype(BF16)
    pad_end = _bdot(hi, incl) + _bdot(mid, incl) + _bdot(lo, incl)
    pad_start = pad_end - padded
    lane = lax.broadcasted_iota(jnp.int32, (tm, LANES), 1)
    lanef = lane.astype(F32)
    topi = topi_ref[...].astype(F32)
    start_row = pad_start[0:1, :]
    dest = jnp.zeros((tm, LANES), F32)
    for k in range(TOP_K):
        idx = jnp.sum(jnp.where(lane == k, topi, 0.0), axis=-1, keepdims=True)
        st = jnp.sum(jnp.where(lanef == idx, start_row, 0.0), axis=-1, keepdims=True)
        dest = jnp.where(lane == k, st, dest)
    dest_ref[...] = dest.astype(jnp.int32) + rank_ref[...]

    nb = blk_ref.shape[1]
    end_col = jnp.sum(jnp.where(r == c, jnp.broadcast_to(pad_end[0:1, :], (LANES, LANES)), 0.0),
                      axis=-1, keepdims=True)
    jpos = lax.broadcasted_iota(jnp.int32, (LANES, nb), 1).astype(F32) * MOE_BLOCK
    esub = lax.broadcasted_iota(jnp.int32, (LANES, nb), 0)
    done = jnp.where((end_col <= jpos) & (esub < N_EXPERTS), 1.0, 0.0)
    be = jnp.minimum(jnp.sum(done, axis=0, keepdims=True), N_EXPERTS - 1.0)
    blk_ref[...] = jnp.broadcast_to(be, blk_ref.shape).astype(jnp.int32)
    lane1 = lax.broadcasted_iota(jnp.int32, (SUBLANES, LANES), 1)
    n_valid = jnp.sum(jnp.where(lane1 == N_EXPERTS - 1, pad_end, 0.0), axis=-1, keepdims=True) * (1.0 / MOE_BLOCK)
    sub1 = lax.broadcasted_iota(jnp.int32, (SUBLANES, LANES), 0)
    meta = jnp.where(sub1 == 0, pad_start + cnt, jnp.where(sub1 == 1, pad_end, jnp.broadcast_to(n_valid, (SUBLANES, LANES))))
    meta_ref[...] = meta.astype(jnp.int32)


def _route(topi, rank, cnt, n_blocks, tm):
    t = topi.shape[0]
    nb_pad = -(-n_blocks // LANES) * LANES
    tok = pl.BlockSpec((tm, LANES), lambda i: (i, 0))
    fix = lambda n: pl.BlockSpec((SUBLANES, n), lambda i: (0, 0))
    return pl.pallas_call(
        _route_kernel,
        grid=(t // tm,),
        in_specs=[tok, tok, fix(LANES)],
        out_specs=[tok, fix(nb_pad), fix(LANES)],
        out_shape=[jax.ShapeDtypeStruct((t, LANES), jnp.int32),
                   jax.ShapeDtypeStruct((SUBLANES, nb_pad), jnp.int32),
                   jax.ShapeDtypeStruct((SUBLANES, LANES), jnp.int32)],
        compiler_params=_cparams("arbitrary"),
        name="route_plan",
    )(topi, rank, cnt)


def _dispatch_kernel(pad_ref, dest_ref, h_ref, h_hbm, xb_ref, zero_ref, sem_z, sem_s):
    i = pl.program_id(0)
    tm = h_ref.shape[0]

    @pl.when(i == 0)
    def _():
        zero_ref[...] = jnp.zeros_like(zero_ref)
        sizes = [1 << b for b in range(int(math.log2(MOE_BLOCK)) - 1, -1, -1)]

        def fill(e, carry, do_wait):
            start = pad_ref[0, e]
            n_pad = pad_ref[1, e] - start
            off = start
            for sz in sizes:
                take = (n_pad & sz) != 0
                cp = pltpu.make_async_copy(zero_ref.at[pl.ds(0, sz)], xb_ref.at[pl.ds(off, sz)], sem_z)

                @pl.when(take)
                def _():
                    if do_wait:
                        cp.wait()
                    else:
                        cp.start()
                off = off + jnp.where(take, sz, 0)
            return carry

        half = zero_ref.shape[0]

        def fill_unused(j, carry, do_wait):
            for part in range(MOE_BLOCK // half):
                cp = pltpu.make_async_copy(zero_ref, xb_ref.at[pl.ds(j * MOE_BLOCK + part * half, half)], sem_z)
                if do_wait:
                    cp.wait()
                else:
                    cp.start()
            return carry

        n_blocks = xb_ref.shape[0] // MOE_BLOCK
        lax.fori_loop(0, N_EXPERTS, lambda e, c: fill(e, c, False), 0)
        lax.fori_loop(pad_ref[2, 0], n_blocks, lambda j, c: fill_unused(j, c, False), 0)
        lax.fori_loop(0, N_EXPERTS, lambda e, c: fill(e, c, True), 0)
        lax.fori_loop(pad_ref[2, 0], n_blocks, lambda j, c: fill_unused(j, c, True), 0)

    def issue(r, carry):
        for k in range(TOP_K):
            src = h_ref.at[r] if k < TOP_K // 2 else h_hbm.at[i * tm + r]
            pltpu.make_async_copy(src, xb_ref.at[dest_ref[r * TOP_K + k]], sem_s).start(priority=k % 2)
        return carry

    lax.fori_loop(0, tm, issue, 0, unroll=ROW_DMA_UNROLL)
    for k in range(TOP_K):
        pltpu.make_async_copy(h_ref, xb_ref.at[pl.ds(0, tm)], sem_s).wait()


def _dispatch(h2t, dest, meta, n_slots, tm):
    t = h2t.shape[0] // SUBLANES
    h3 = h2t.reshape(t, SUBLANES, LANES)
    return pl.pallas_call(
        _dispatch_kernel,
        grid_spec=pltpu.PrefetchScalarGridSpec(
            num_scalar_prefetch=1,
            grid=(t // tm,),
            in_specs=[pl.BlockSpec((tm * TOP_K,), lambda i, p: (i,), memory_space=pltpu.SMEM),
                      pl.BlockSpec((tm, SUBLANES, LANES), lambda i, p: (i, 0, 0)),
                      pl.BlockSpec(memory_space=pl.ANY)],
            out_specs=pl.BlockSpec(memory_space=pl.ANY),
            scratch_shapes=[pltpu.VMEM((MOE_BLOCK // 2, SUBLANES, LANES), F32),
                            pltpu.SemaphoreType.DMA, pltpu.SemaphoreType.DMA]),
        out_shape=jax.ShapeDtypeStruct((n_slots, SUBLANES, LANES), F32),
        compiler_params=_cparams("arbitrary"),
        name="moe_dispatch",
    )(meta[:3, :N_EXPERTS], dest, h3, h3)


def _expert_kernel(be_ref, nv_ref, nxt_ref, x_ref, wgu_hbm, bgu_ref, wd_hbm, bd_ref, y_ref,
                   wgu_f32, wd_f32, wgu_bf, wd_bf, sem, *, layer):
    j = pl.program_id(0)
    valid = j < nv_ref[0]
    first = valid & ((j == 0) | (be_ref[j] != be_ref[jnp.maximum(j - 1, 0)]))

    def weight_copies(e):
        return (pltpu.make_async_copy(wgu_hbm.at[layer, e], wgu_f32, sem.at[0]),
                pltpu.make_async_copy(wd_hbm.at[layer, e], wd_f32, sem.at[1]))

    @pl.when(j == 0)
    def _():
        for cp in weight_copies(be_ref[0]):
            cp.start()

    @pl.when(first)
    def _():
        for cp in weight_copies(be_ref[j]):
            cp.wait()
        wgu_bf[...] = wgu_f32[...].astype(BF16)
        wd_bf[...] = wd_f32[...].astype(BF16)

        @pl.when(nxt_ref[j] >= 0)
        def _():
            for cp in weight_copies(nxt_ref[j]):
                cp.start()

    @pl.when(valid)
    def _():
        x = jnp.concatenate([x_ref[pl.ds(c, MOE_BLOCK, stride=SUBLANES), :] for c in range(D_MODEL // LANES)],
                            axis=1).astype(BF16)
        gu = _bdot(x, wgu_bf[...]) + bgu_ref[0]
        x_glu = jnp.minimum(gu[:, :D_EXPERT], SWIGLU_LIMIT)
        x_lin = jnp.clip(gu[:, D_EXPERT:], -SWIGLU_LIMIT, SWIGLU_LIMIT)
        act = x_glu * _sigmoid(SWIGLU_ALPHA * x_glu) * (x_lin + 1.0)
        y = _bdot(act.astype(BF16), wd_bf[...]) + bd_ref[0]
        for c in range(D_MODEL // LANES):
            y_ref[pl.ds(c, MOE_BLOCK, stride=SUBLANES), :] = y[:, c * LANES:(c + 1) * LANES]

    @pl.when(jnp.logical_not(valid))
    def _():
        y_ref[...] = jnp.zeros_like(y_ref)


def _experts(xb, block_expert, n_valid, w_gu, b_gu, w_down, b_down, layer):
    n_slots = xb.shape[0]
    n_blocks = n_slots // MOE_BLOCK
    rows = MOE_BLOCK * SUBLANES
    x2 = xb.reshape(n_slots * SUBLANES, LANES)
    depth, ne, d, de2 = w_gu.shape
    idx = jnp.arange(n_blocks, dtype=jnp.int32)
    is_first = ((idx == 0) | (block_expert != jnp.roll(block_expert, 1))) & (idx < n_valid[0])
    first_at = lax.cummin(jnp.where(is_first, idx, n_blocks)[::-1])[::-1]
    next_first = jnp.concatenate([first_at[1:], jnp.full((1,), n_blocks, jnp.int32)])
    nxt = jnp.where(next_first < n_blocks, block_expert[jnp.minimum(next_first, n_blocks - 1)], -1)
    last = lambda j, be, nv, nx: jnp.minimum(j, nv[0] - 1)
    bmap = lambda j, be, nv, nx: (layer, be[last(j, be, nv, nx)], 0, 0)
    return pl.pallas_call(
        functools.partial(_expert_kernel, layer=layer),
        grid_spec=pltpu.PrefetchScalarGridSpec(
            num_scalar_prefetch=3,
            grid=(n_blocks,),
            in_specs=[pl.BlockSpec((rows, LANES), lambda j, be, nv, nx: (last(j, be, nv, nx), 0)),
                      pl.BlockSpec(memory_space=pl.ANY),
                      pl.BlockSpec((None, 1, 1, de2), bmap),
                      pl.BlockSpec(memory_space=pl.ANY),
                      pl.BlockSpec((None, 1, 1, d), bmap)],
            out_specs=pl.BlockSpec((rows, LANES), lambda j, be, nv, nx: (j, 0)),
            scratch_shapes=[pltpu.VMEM((d, de2), F32), pltpu.VMEM((de2 // 2, d), F32),
                            pltpu.VMEM((d, de2), BF16), pltpu.VMEM((de2 // 2, d), BF16),
                            pltpu.SemaphoreType.DMA((2,))]),
        out_shape=jax.ShapeDtypeStruct((n_slots * SUBLANES, LANES), F32),
        compiler_params=_cparams("arbitrary"),
        name="moe_experts",
    )(block_expert, n_valid, nxt.astype(jnp.int32), x2, w_gu, b_gu.reshape(depth, ne, 1, de2), w_down,
      b_down.reshape(depth, ne, 1, d))


def _combine_kernel(dest_ref, dest_next_ref, yb_ref, gate_ref, x1_ref, g2_ref, gpost_ref, o_ref, buf, sem):
    i = pl.program_id(0)
    tm = x1_ref.shape[0]
    slot = i % 2

    def gather(idx_ref, into):
        def issue(r, carry):
            for k in range(TOP_K):
                src = pl.multiple_of(idx_ref[r * TOP_K + k] * SUBLANES, SUBLANES)
                dst = pl.multiple_of((k * tm + r) * SUBLANES, SUBLANES)
                pltpu.make_async_copy(yb_ref.at[pl.ds(src, SUBLANES), :], buf.at[into, pl.ds(dst, SUBLANES), :],
                                      sem.at[into]).start(priority=k % 2)
            return carry
        lax.fori_loop(0, tm, issue, 0, unroll=ROW_DMA_UNROLL)

    @pl.when(i == 0)
    def _():
        gather(dest_ref, 0)

    @pl.when(i + 1 < pl.num_programs(0))
    def _():
        gather(dest_next_ref, 1 - slot)

    pltpu.make_async_copy(yb_ref.at[pl.ds(0, TOP_K * tm * SUBLANES), :], buf.at[slot], sem.at[slot]).wait()
    gates = gate_ref[...]
    gk = [jnp.broadcast_to(gates[:, k:k + 1], (tm, LANES)) for k in range(TOP_K)]
    b2 = buf.at[slot]
    cols = []
    for c in range(D_MODEL // LANES):
        acc = jnp.zeros((tm, LANES), F32)
        for k in range(TOP_K):
            acc = acc + gk[k] * b2[pl.ds(k * tm * SUBLANES + c, tm, stride=SUBLANES), :]
        cols.append(acc)
    f = jnp.concatenate(cols, axis=1)
    o_ref[...] = x1_ref[...] + g2_ref[0] * (_rms(f) * gpost_ref[...])


def _combine(yb, dest, gates, x1, mod3, g_post, seq, tm):
    t, d = x1.shape
    tps = seq // tm
    n = t // tm
    return pl.pallas_call(
        _combine_kernel,
        grid=(n,),
        in_specs=[pl.BlockSpec((tm * TOP_K,), lambda i: (i,), memory_space=pltpu.SMEM),
                  pl.BlockSpec((tm * TOP_K,), lambda i: (jnp.minimum(i + 1, n - 1),), memory_space=pltpu.SMEM),
                  pl.BlockSpec(memory_space=pl.ANY),
                  pl.BlockSpec((tm, LANES), lambda i: (i, 0)),
                  pl.BlockSpec((tm, d), lambda i: (i, 0)),
                  _mod_spec(5, tps),
                  pl.BlockSpec(g_post.shape, lambda i: (0, 0))],
        out_specs=pl.BlockSpec((tm, d), lambda i: (i, 0)),
        out_shape=jax.ShapeDtypeStruct((t, d), F32),
        scratch_shapes=[pltpu.VMEM((2, TOP_K * tm * SUBLANES, LANES), F32), pltpu.SemaphoreType.DMA((2,))],
        compiler_params=_cparams("arbitrary"),
        name="moe_combine",
    )(dest, dest, yb, gates, x1, mod3, g_post)


def _moe(h2t, topi, gates, rank, cnt, x1, mod3, g_post, w_gu, b_gu, w_down, b_down, layer, seq):
    t = x1.shape[0]
    n_blocks = t * TOP_K // MOE_BLOCK + N_EXPERTS
    dest_l, blk, meta = _route(topi, rank, cnt, n_blocks, min(1024, t))
    dest = dest_l[:, :TOP_K].reshape(t * TOP_K)
    xb = _dispatch(h2t, dest, meta, n_blocks * MOE_BLOCK, MOE_BLOCK)
    yb = _experts(xb, blk[0, :n_blocks], meta[2, :1], w_gu, b_gu, w_down, b_down, layer)
    return _combine(yb, dest, gates, x1, mod3, g_post, seq, MOE_BLOCK)


TOKEN_TILE = 512
OUT_TILE = 512
FOX_Q_TILE = 256
FOX_K_TILE = 256


def kernel(x, c, ada_w, ada_b, norm_pre_mix, norm_post_mix, norm_pre_ffn, norm_post_ffn, ev_w_in, fox_b_f, s5_lam_re, s5_lam_im, s5_log_dt, s5_b_re, s5_b_im, s5_c_re, s5_c_im, s5_d, s5_glu_w, s5_glu_b, ev_w_out, od_w_in, gla_w_up, gla_b_gate, od_w_out, router_w, router_b, exp_w_gu, exp_b_gu, exp_w_down, exp_b_down):
    bsz, seq, d = x.shape
    t = bsz * seq
    tm = min(TOKEN_TILE, seq)
    x2 = x.reshape(t, d)
    mod = _modulation(c, ada_w, ada_b)
    for l in range(DEPTH):
        i = l // 2
        mod3 = mod[l].reshape(bsz, 1, 6 * d)
        row = lambda a: a[l].reshape(1, -1)
        tail = (x2, mod3, row(norm_post_mix), row(norm_pre_ffn), router_w[l], router_b[l])
        if l % 2 == 0:
            u, q, k, v = _in_even(x2, mod3, row(norm_pre_mix), ev_w_in[i], fox_b_f[i], seq, tm)
            tables = _s5_tables(s5_lam_re[i], s5_lam_im[i], s5_log_dt[i], s5_b_re[i], s5_b_im[i],
                                s5_c_re[i], s5_c_im[i])
            ys = _s5_scan(u, bsz, seq, tables)
            yb = _fox(q, k, v, bsz, seq, min(FOX_Q_TILE, seq), min(FOX_K_TILE, seq))
            outs = _out_even(ys, u, yb, s5_d[i], s5_glu_w[i], s5_glu_b[i], ev_w_out[i], *tail, seq=seq, tm=min(OUT_TILE, seq))
        else:
            rq, rk, rv, sg, gq, gk, gv, sr, la = _in_odd(x2, mod3, row(norm_pre_mix), od_w_in[i],
                                                         gla_w_up[i], gla_b_gate[i], seq, tm)
            yc = _retention(rq, rk, rv, sg, bsz, seq)
            yd = _gla(gq, gk, la, gv, sr, bsz, seq)
            outs = _out_odd(yc, yd, od_w_out[i], *tail, seq=seq, tm=min(OUT_TILE, seq))
        x1, h2t, topi, gates, rank, cnt = outs
        x2 = _moe(h2t, topi, gates, rank, cnt, x1, mod3, row(norm_post_ffn),
                  exp_w_gu, exp_b_gu, exp_w_down, exp_b_down, l, seq)
    return x2.reshape(bsz, seq, d)
```

```python
import functools
import math

import jax
import jax.numpy as jnp
from jax import lax
from jax.experimental import pallas as pl
from jax.experimental.pallas import tpu as pltpu

F32 = jnp.float32
BF16 = jnp.bfloat16
HIGHEST = lax.Precision.HIGHEST

D_MODEL = 1024
DEPTH = 2
EPS = 1e-6
S5_CH = 512
S5_GROUP = 16
S5_GROUPS = S5_CH // S5_GROUP
S5_STATE = 64
S5_CHUNK = 8
FOX_HEADS = 8
FOX_DH = 64
FOX_W = FOX_HEADS * FOX_DH
LOG2_E = 1.4426950408889634
FOX_VROWS = FOX_DH + 16
FOX_HPS = 8
RET_HEADS = 4
RET_DK = 128
RET_DV = 128
ROPE_BASE = 10000.0
GLA_HEADS = 4
GLA_DK = 64
GLA_DV = 128
GLA_RANK = 16
GLA_TAU = 16.0
GLA_CHUNK = 64
GLA_SUB = 16
N_EXPERTS = 32
TOP_K = 4
D_EXPERT = 1024
SWIGLU_LIMIT = 7.0
SWIGLU_ALPHA = 1.702
MOE_BLOCK = 256
ROW_DMA_UNROLL = 4

LANES = 128
SUBLANES = 8
VMEM_LIMIT = 56 * 1024 * 1024


def _cparams(*sem):
    return pltpu.CompilerParams(dimension_semantics=sem, vmem_limit_bytes=VMEM_LIMIT)


def _bdot(a, b):
    return jnp.dot(a, b, preferred_element_type=F32)


def _dot_nt(a, b):
    return lax.dot_general(a, b, (((1,), (1,)), ((), ())), preferred_element_type=F32)


def _dot_tn(a, b):
    return lax.dot_general(a, b, (((0,), (0,)), ((), ())), preferred_element_type=F32)


def _split3(x):
    hi = x.astype(BF16)
    r = x - hi.astype(F32)
    mid = r.astype(BF16)
    lo = (r - mid.astype(F32)).astype(BF16)
    return hi, mid, lo


def _dot01(m01, x):
    hi, mid, lo = _split3(x)
    return _bdot(m01, hi) + _bdot(m01, mid) + _bdot(m01, lo)


def _lower_tri(n, strict=False):
    r = lax.broadcasted_iota(jnp.int32, (n, n), 0)
    c = lax.broadcasted_iota(jnp.int32, (n, n), 1)
    return ((r > c) if strict else (r >= c)).astype(BF16)


def _log_sigmoid(x):
    return jnp.minimum(x, 0.0) - jnp.log1p(jnp.exp(-jnp.abs(x)))


def _sigmoid(x):
    return 1.0 / (1.0 + jnp.exp(-x))


def _silu(x):
    return x * _sigmoid(x)


def _rms(x):
    return x * lax.rsqrt(jnp.mean(x * x, axis=-1, keepdims=True) + EPS)


def _mod_kernel(c_ref, w_ref, b_ref, o_ref):
    c = c_ref[...]
    o_ref[0] = jnp.dot(_silu(c), w_ref[0], preferred_element_type=F32, precision=HIGHEST) + b_ref[0]


def _modulation(c, ada_w, ada_b):
    depth, d, n = ada_w.shape
    bsz = c.shape[0]
    tn = D_MODEL
    return pl.pallas_call(
        _mod_kernel,
        grid=(depth, n // tn),
        in_specs=[pl.BlockSpec((bsz, d), lambda l, j: (0, 0)),
                  pl.BlockSpec((1, d, tn), lambda l, j: (l, 0, j)),
                  pl.BlockSpec((1, 1, tn), lambda l, j: (l, 0, j))],
        out_specs=pl.BlockSpec((1, bsz, tn), lambda l, j: (l, 0, j)),
        out_shape=jax.ShapeDtypeStruct((depth, bsz, n), F32),
        compiler_params=_cparams("parallel", "parallel"),
        name="adaln_mod",
    )(c, ada_w, ada_b.reshape(depth, 1, n))


def _mod_spec(chunk, tiles_per_seq):
    return pl.BlockSpec((1, 1, D_MODEL), lambda i: (i // tiles_per_seq, 0, chunk))


def _prenorm(x, g_ref, sc_ref, sh_ref):
    return _rms(x) * (g_ref[...] * (1.0 + sc_ref[0])) + sh_ref[0]


def _in_even_kernel(x_ref, sh_ref, sc_ref, g_ref, w_ref, bf_ref,
                    u_ref, q_ref, k_ref, vt_ref, carry_ref, *, tiles_per_seq):
    i = pl.program_id(0)
    tm = x_ref.shape[0]
    h = _prenorm(x_ref[...], g_ref, sc_ref, sh_ref)
    z = _bdot(h.astype(BF16), w_ref[...])
    u_ref[...] = z[:, 0:S5_CH]
    tail = jnp.concatenate([jnp.ones((1, tm), F32), jnp.zeros((FOX_VROWS - FOX_DH - 1, tm), F32)], axis=0)
    for pr in range(FOX_HEADS // 2):
        c0 = S5_CH + 2 * FOX_W + pr * LANES
        v_pair = z[:, c0:c0 + LANES].T
        for hh in range(2):
            r0 = (2 * pr + hh) * FOX_VROWS
            vt_ref[r0:r0 + FOX_VROWS, :] = jnp.concatenate(
                [v_pair[hh * FOX_DH:(hh + 1) * FOX_DH], tail], axis=0).astype(BF16)
    ls = _log_sigmoid(z[:, S5_CH + 3 * FOX_W:] + bf_ref[...])

    @pl.when(i % tiles_per_seq == 0)
    def _():
        carry_ref[...] = jnp.zeros_like(carry_ref)

    cum = _dot01(_lower_tri(tm), ls) + carry_ref[...]
    carry_ref[...] = cum[tm - 1:tm, :]

    lane = lax.broadcasted_iota(jnp.int32, (1, LANES), 1)
    feat = lane < FOX_DH
    ones = jnp.where(lane < FOX_DH + 3, 1.0, 0.0)
    for hd in range(FOX_HEADS):
        blk = (hd * FOX_DH) // LANES * LANES
        qs = z[:, S5_CH + blk:S5_CH + blk + LANES] * (FOX_DH ** -0.5 * LOG2_E)
        ks = z[:, S5_CH + FOX_W + blk:S5_CH + FOX_W + blk + LANES]
        if (hd * FOX_DH) % LANES:
            qs = pltpu.roll(qs, LANES - FOX_DH, 1)
            ks = pltpu.roll(ks, LANES - FOX_DH, 1)
        nf = jnp.broadcast_to(-LOG2_E * cum[:, hd:hd + 1], (tm, LANES))
        hi = nf.astype(BF16).astype(F32)
        mid = (nf - hi).astype(BF16).astype(F32)
        lo = nf - hi - mid
        bias = jnp.where(lane == FOX_DH, hi, jnp.where(lane == FOX_DH + 1, mid,
                                                       jnp.where(lane == FOX_DH + 2, lo, 0.0)))
        q_ref[:, hd * LANES:(hd + 1) * LANES] = jnp.where(feat, qs, ones).astype(BF16)
        k_ref[:, hd * LANES:(hd + 1) * LANES] = jnp.where(feat, ks, bias).astype(BF16)


def _in_even(x2, mod3, gain, w_in, b_f, seq, tm):
    t, d = x2.shape
    tiles_per_seq = seq // tm
    nw = S5_CH + 3 * FOX_W
    w = jnp.concatenate([w_in[:, :nw], jnp.pad(w_in[:, nw:], ((0, 0), (0, LANES - FOX_HEADS)))],
                        axis=1).astype(BF16)
    bf = jnp.pad(b_f, (0, LANES - FOX_HEADS)).reshape(1, LANES)
    tok = lambda n: pl.BlockSpec((tm, n), lambda i: (i, 0))
    full = lambda a: pl.BlockSpec(a.shape, lambda i: (0,) * a.ndim)
    return pl.pallas_call(
        functools.partial(_in_even_kernel, tiles_per_seq=tiles_per_seq),
        grid=(t // tm,),
        in_specs=[tok(d), _mod_spec(0, tiles_per_seq), _mod_spec(1, tiles_per_seq),
                  full(gain), full(w), full(bf)],
        out_specs=[tok(S5_CH), tok(FOX_HEADS * LANES), tok(FOX_HEADS * LANES),
                   pl.BlockSpec((FOX_HEADS * FOX_VROWS, tm), lambda i: (i // tiles_per_seq, i % tiles_per_seq))],
        out_shape=[jax.ShapeDtypeStruct((t, S5_CH), F32),
                   jax.ShapeDtypeStruct((t, FOX_HEADS * LANES), BF16),
                   jax.ShapeDtypeStruct((t, FOX_HEADS * LANES), BF16),
                   jax.ShapeDtypeStruct((t // seq * FOX_HEADS * FOX_VROWS, seq), BF16)],
        scratch_shapes=[pltpu.VMEM((1, LANES), F32)],
        compiler_params=_cparams("arbitrary"),
        name="in_proj_even",
    )(x2, mod3, mod3, gain, w, bf)


S5_TILE_GROUPS = LANES // S5_GROUP
S5_SEQ_PARTS = 4


def _s5_kernel(u_ref, wt_ref, ws_ref, wc_ref, a_ref, y_ref, x_scr, e_scr, hp_scr, h_scr, *, nb, ncl):
    ell = S5_CHUNK
    sw = S5_TILE_GROUPS * 2 * S5_STATE

    @pl.when(pl.program_id(1) == 0)
    def _():
        h_scr[...] = jnp.zeros_like(h_scr)

    blk = nb * SUBLANES
    for hi in range(ncl // SUBLANES):
        for b in range(nb):
            for t in range(ell):
                x_scr[hi * blk + b * SUBLANES:hi * blk + (b + 1) * SUBLANES, t * LANES:(t + 1) * LANES] = (
                    u_ref[b, pl.ds(hi * SUBLANES * ell + t, SUBLANES, stride=ell), :])
    x = x_scr[...].astype(BF16)
    e = _bdot(x, ws_ref[0])
    tg = S5_TILE_GROUPS
    for j in range(tg):
        e_scr[j] = e[:, j * LANES:(j + 1) * LANES]
        e_scr[tg + j] = pltpu.roll(e[:, j * LANES:(j + 1) * LANES], S5_STATE, 1)
    a1 = jnp.broadcast_to(a_ref[0, 0:1, :], (nb, sw))
    a2 = jnp.broadcast_to(a_ref[0, 1:2, :], (nb, sw))
    a2s = jnp.broadcast_to(a_ref[0, 2:3, :], (nb, sw))

    def body(c, carry):
        h, hs = carry
        rows_c = pl.ds((c // SUBLANES) * blk + c % SUBLANES, nb, stride=SUBLANES)
        for j in range(tg):
            hp_scr[j, rows_c, :] = h[:, j * LANES:(j + 1) * LANES]
        e1 = jnp.concatenate([e_scr[j, rows_c, :] for j in range(tg)], axis=1)
        e2 = jnp.concatenate([e_scr[tg + j, rows_c, :] for j in range(tg)], axis=1)
        return a1 * h + a2 * hs + e1, a1 * hs + a2s * h + e2

    h, hs = lax.fori_loop(0, ncl, body, (h_scr[0], h_scr[1]))
    h_scr[0] = h
    h_scr[1] = hs
    hp = jnp.concatenate([hp_scr[j] for j in range(tg)], axis=1).astype(BF16)
    y = _bdot(x, wt_ref[0]) + _bdot(hp, wc_ref[0])
    for hi in range(ncl // SUBLANES):
        for b in range(nb):
            for t in range(ell):
                y_ref[b, pl.ds(hi * SUBLANES * ell + t, SUBLANES, stride=ell), :] = (
                    y[hi * blk + b * SUBLANES:hi * blk + (b + 1) * SUBLANES, t * LANES:(t + 1) * LANES])


def _s5_tables(lam_re, lam_im, log_dt, b_re, b_im, c_re, c_im):
    ell, p, g = S5_CHUNK, S5_STATE, S5_GROUPS
    lr, li = lam_re.astype(F32), lam_im.astype(F32)
    dt = jnp.exp(log_dt.astype(F32))[:, None]
    mag = jnp.exp(lr * dt)
    a_re, a_im = mag * jnp.cos(li * dt), mag * jnp.sin(li * dt)
    den = lr * lr + li * li
    n_re, n_im = a_re - 1.0, a_im
    z_re = (n_re * lr + n_im * li) / den
    z_im = (n_im * lr - n_re * li) / den
    br, bi = b_re.astype(F32), b_im.astype(F32)
    bb_re = z_re[..., None] * br - z_im[..., None] * bi
    bb_im = z_re[..., None] * bi + z_im[..., None] * br
    j = jnp.arange(ell + 1, dtype=F32)[:, None, None]
    pmag = jnp.exp(j * (lr * dt)[None])
    pr, pi = pmag * jnp.cos(j * (li * dt)[None]), pmag * jnp.sin(j * (li * dt)[None])
    w_re = pr[..., None] * bb_re[None] - pi[..., None] * bb_im[None]
    w_im = pr[..., None] * bb_im[None] + pi[..., None] * bb_re[None]
    cr, ci = c_re.astype(F32), c_im.astype(F32)
    kern = (jnp.einsum('gcp,jgpd->jgcd', cr, w_re[:ell], precision=HIGHEST)
            - jnp.einsum('gcp,jgpd->jgcd', ci, w_im[:ell], precision=HIGHEST))
    s_idx = jnp.arange(ell)[:, None]
    t_idx = jnp.arange(ell)[None, :]
    lag = jnp.clip(t_idx - s_idx, 0, ell - 1)
    toep = kern[lag]
    toep = jnp.where((t_idx >= s_idx)[:, :, None, None, None], toep, 0.0)
    toep = toep.transpose(2, 0, 4, 1, 3)
    rev = jnp.arange(ell - 1, -1, -1)
    st_re = w_re[rev].transpose(1, 0, 3, 2)
    st_im = w_im[rev].transpose(1, 0, 3, 2)
    wst = jnp.concatenate([st_re, st_im], -1)
    p1r, p1i = pr[1:], pi[1:]
    c_hr = cr[None] * p1r[:, :, None, :] - ci[None] * p1i[:, :, None, :]
    c_hi = -cr[None] * p1i[:, :, None, :] - ci[None] * p1r[:, :, None, :]
    cst = jnp.concatenate([c_hr, c_hi], axis=-1).transpose(1, 3, 0, 2)
    al_r, al_i = pr[ell], pi[ell]
    a_rows = jnp.stack([jnp.concatenate([al_r, al_r], -1), jnp.concatenate([-al_i, al_i], -1),
                        jnp.concatenate([al_i, -al_i], -1)], axis=1)

    tg = S5_TILE_GROUPS
    nt = g // tg
    tile = lambda a: a.reshape((nt, tg) + a.shape[1:])
    lane_grp = jnp.arange(LANES) // S5_GROUP
    step_lane_grp = jnp.tile(lane_grp, ell)
    spread_out = (jnp.eye(ell, dtype=BF16)[:, None, :, None, None]
                  * jnp.eye(S5_GROUP, dtype=BF16)[None, :, None, None, :]
                  * jnp.ones((1, 1, 1, tg, 1), BF16)).reshape(ell * S5_GROUP, ell * LANES)
    toep_t = tile(toep).transpose(0, 2, 1, 3, 4, 5).reshape(nt, ell, LANES, ell * S5_GROUP).astype(BF16)
    wt = jnp.einsum('ksrm,mn->ksrn', toep_t, spread_out)
    wt = jnp.where((lane_grp[:, None] == step_lane_grp[None, :])[None, None], wt, 0)
    wt = wt.reshape(nt, ell * LANES, ell * LANES)
    spread_state = jnp.tile(jnp.eye(2 * p, dtype=BF16), (1, tg))
    state_grp = jnp.arange(tg * 2 * p) // (2 * p)
    wst_t = tile(wst).transpose(0, 2, 1, 3, 4).reshape(nt, ell, LANES, 2 * p).astype(BF16)
    ws = jnp.einsum('ksrp,pn->ksrn', wst_t, spread_state)
    ws = jnp.where((lane_grp[:, None] == state_grp[None, :])[None, None], ws, 0)
    ws = ws.reshape(nt, ell * LANES, tg * 2 * p)
    cst_t = tile(cst).reshape(nt, tg * 2 * p, ell * S5_GROUP).astype(BF16)
    wc = jnp.einsum('krm,mn->krn', cst_t, spread_out)
    wc = jnp.where((state_grp[:, None] == step_lane_grp[None, :])[None], wc, 0)
    a_t = tile(a_rows).transpose(0, 2, 1, 3).reshape(nt, 3, tg * 2 * p)
    a_t = jnp.pad(a_t, ((0, 0), (0, SUBLANES - 3), (0, 0)))
    return wt, ws, wc, a_t


def _s5_scan(u, bsz, seq, tables):
    wt, ws, wc, a_t = tables
    ell = S5_CHUNK
    nt = S5_GROUPS // S5_TILE_GROUPS
    parts = S5_SEQ_PARTS if seq % (S5_SEQ_PARTS * ell * 2 * SUBLANES) == 0 else 1
    ncl = seq // parts // ell
    rows = bsz * ncl
    sw = S5_TILE_GROUPS * 2 * S5_STATE
    u3 = u.reshape(bsz, seq, S5_CH)
    io = pl.BlockSpec((bsz, seq // parts, LANES), lambda k, s: (0, s, k))
    per_tile = lambda a: pl.BlockSpec((1,) + a.shape[1:], lambda k, s: (k, 0, 0))
    y = pl.pallas_call(
        functools.partial(_s5_kernel, nb=bsz, ncl=ncl),
        grid=(nt, parts),
        in_specs=[io, per_tile(wt), per_tile(ws), per_tile(wc), per_tile(a_t)],
        out_specs=io,
        out_shape=jax.ShapeDtypeStruct((bsz, seq, S5_CH), F32),
        scratch_shapes=[pltpu.VMEM((rows, ell * LANES), F32),
                        pltpu.VMEM((2 * S5_TILE_GROUPS, rows, LANES), F32),
                        pltpu.VMEM((S5_TILE_GROUPS, rows, LANES), F32),
                        pltpu.VMEM((2, bsz, sw), F32)],
        compiler_params=_cparams("parallel", "arbitrary"),
        name="s5_scan",
    )(u3, wt, ws, wc, a_t)
    return y.reshape(bsz * seq, S5_CH)


def _fox_kernel(q_ref, k_ref, vt_ref, o_ref, *, tq, tk):
    i = pl.program_id(2)
    nh = FOX_HPS
    q = [q_ref[:, h * LANES:(h + 1) * LANES] for h in range(nh)]
    key = lax.broadcasted_iota(jnp.int32, (tk, tq), 0)
    qry = lax.broadcasted_iota(jnp.int32, (tk, tq), 1)
    per_q = tq // tk

    def block(j0, carry, mask):
        kj = k_ref[pl.ds(j0, tk), :]
        vtj = vt_ref[:, pl.ds(j0, tk)]
        ss = [_dot_nt(kj[:, h * LANES:(h + 1) * LANES], q[h]) for h in range(nh)]
        stats = []
        for h in range(nh):
            m, acc = carry[h]
            s = ss[h] if mask is None else jnp.where(mask, ss[h], -jnp.inf)
            m_new = jnp.maximum(m, jnp.max(s, axis=0, keepdims=True))
            p = jnp.exp2(s - m_new)
            stats.append((m_new, jnp.exp2(m - m_new), p.astype(BF16)))
        out = []
        for h in range(nh):
            m_new, alpha, p = stats[h]
            acc = alpha * carry[h][1] + _bdot(vtj[h * FOX_VROWS:(h + 1) * FOX_VROWS, :], p)
            out.append((m_new, acc))
        return tuple(out)

    init = tuple((jnp.full((1, tq), -jnp.inf, F32), jnp.zeros((FOX_VROWS, tq), F32)) for _ in range(nh))
    carry = lax.fori_loop(0, i * per_q, lambda j, c: block(pl.multiple_of(j * tk, tk), c, None), init)
    for d in range(per_q):
        carry = block(pl.multiple_of(i * tq + d * tk, tk), carry, key + d * tk <= qry)
    for g in range(nh // 2):
        o_t = jnp.concatenate([carry[h][1][:FOX_DH] / carry[h][1][FOX_DH:FOX_DH + 1]
                               for h in (2 * g, 2 * g + 1)], axis=0)
        o_ref[:, g * LANES:(g + 1) * LANES] = o_t.T.astype(o_ref.dtype)


def _fox(q_aug, k_aug, v_t, bsz, seq, tq, tk):
    t = q_aug.shape[0]
    nh = FOX_HPS
    groups = FOX_HEADS // nh
    nq = seq // tq
    return pl.pallas_call(
        functools.partial(_fox_kernel, tq=tq, tk=tk),
        grid=(bsz, groups, nq),
        in_specs=[pl.BlockSpec((tq, nh * LANES), lambda b, p, i: (b * nq + i, p)),
                  pl.BlockSpec((seq, nh * LANES), lambda b, p, i: (b, p)),
                  pl.BlockSpec((nh * FOX_VROWS, seq), lambda b, p, i: (b * groups + p, 0))],
        out_specs=pl.BlockSpec((tq, nh * FOX_DH), lambda b, p, i: (b * nq + i, p)),
        out_shape=jax.ShapeDtypeStruct((t, FOX_W), BF16),
        compiler_params=_cparams("parallel", "parallel", "arbitrary"),
        name="fox_attention",
    )(q_aug, k_aug, v_t)


_ODD_COLS = (("rq", 512), ("rk", 512), ("rv", 512), ("rg", 512), ("gq", 256), ("gk", 256),
             ("gv", 512), ("gr", 512), ("glr", LANES))


def _odd_offsets():
    off, out = 0, {}
    for name, w in _ODD_COLS:
        out[name] = (off, off + w)
        off += w
    return out, off


def _in_odd_kernel(x_ref, sh_ref, sc_ref, g_ref, w_ref, cos_ref, sin_ref, wup_ref, bg_ref,
                   rq_ref, rk_ref, rv_ref, sg_ref, gq_ref, gk_ref, gv_ref, sr_ref, la_ref):
    h = _prenorm(x_ref[...], g_ref, sc_ref, sh_ref)
    z = _bdot(h.astype(BF16), w_ref[...])
    off, _ = _odd_offsets()
    col = lambda n: z[:, off[n][0]:off[n][1]]
    cos, sin = cos_ref[...], sin_ref[...]

    def rope(t, scale):
        heads = []
        for hd in range(RET_HEADS):
            th = t[:, hd * RET_DK:(hd + 1) * RET_DK]
            heads.append((th * cos + pltpu.roll(th, RET_DK // 2, 1) * sin) * scale)
        return jnp.concatenate(heads, axis=1).astype(BF16)

    rq_ref[...] = rope(col("rq"), 1.0)
    rk_ref[...] = rope(col("rk"), RET_DK ** -0.5)
    rv_ref[...] = col("rv").astype(BF16)
    sg_ref[...] = _silu(col("rg"))
    gq_ref[...] = col("gq") * (GLA_DK ** -0.5)
    gk_ref[...] = col("gk")
    gv_ref[...] = col("gv").astype(BF16)
    sr_ref[...] = _silu(col("gr"))
    gate = jnp.dot(col("glr"), wup_ref[...], preferred_element_type=F32, precision=HIGHEST) + bg_ref[...]
    la_ref[...] = _log_sigmoid(gate) * (1.0 / GLA_TAU)


def _in_odd(x2, mod3, gain, w_in, w_up, b_gate, seq, tm):
    t, d = x2.shape
    tps = seq // tm
    ref_w = (512, 512, 512, 512, 256, 256, 512, GLA_RANK, 512)
    starts = [0]
    for wd in ref_w:
        starts.append(starts[-1] + wd)
    seg = lambda j: w_in[:, starts[j]:starts[j + 1]]
    w = jnp.concatenate([seg(0), seg(1), seg(2), seg(3), seg(4), seg(5), seg(6), seg(8),
                         jnp.pad(seg(7), ((0, 0), (0, LANES - GLA_RANK)))], axis=1).astype(BF16)
    wup = jnp.pad(w_up.astype(F32), ((0, LANES - GLA_RANK), (0, 0)))
    bg = b_gate.reshape(1, -1).astype(F32)
    half = RET_DK // 2
    inv = ROPE_BASE ** (-jnp.arange(half, dtype=F32) / half)
    ang = jnp.arange(seq, dtype=F32)[:, None] * inv[None, :]
    cos = jnp.concatenate([jnp.cos(ang), jnp.cos(ang)], axis=1)
    sin = jnp.concatenate([-jnp.sin(ang), jnp.sin(ang)], axis=1)
    tok = lambda n: pl.BlockSpec((tm, n), lambda i: (i, 0))
    full = lambda a: pl.BlockSpec(a.shape, lambda i: (0,) * a.ndim)
    pos = pl.BlockSpec((tm, RET_DK), lambda i: (i % tps, 0))
    widths = (512, 512, 512, 512, 256, 256, 512, 512, 256)
    dtypes = (BF16, BF16, BF16, F32, F32, F32, BF16, F32, F32)
    return pl.pallas_call(
        _in_odd_kernel,
        grid=(t // tm,),
        in_specs=[tok(d), _mod_spec(0, tps), _mod_spec(1, tps), full(gain), full(w), pos, pos,
                  full(wup), full(bg)],
        out_specs=[tok(n) for n in widths],
        out_shape=[jax.ShapeDtypeStruct((t, n), dt) for n, dt in zip(widths, dtypes)],
        compiler_params=_cparams("parallel"),
        name="in_proj_odd",
    )(x2, mod3, mod3, gain, w, cos, sin, wup, bg)


RET_CHUNK = 256


def _ret_kernel(q_ref, k_ref, v_ref, sg_ref, dm_ref, xi_ref, zeta_ref, gl_ref, y_ref, st_ref):
    @pl.when(pl.program_id(1) == 0)
    def _():
        st_ref[...] = jnp.zeros_like(st_ref)

    nb = q_ref.shape[0]
    chains = [(bl, h) for bl in range(nb) for h in range(RET_HEADS)]
    col = lambda h: slice(h * RET_DK, (h + 1) * RET_DK)
    q = [q_ref[bl, :, col(h)] for bl, h in chains]
    k = [k_ref[bl, :, col(h)] for bl, h in chains]
    v = [v_ref[bl, :, col(h)] for bl, h in chains]
    st = [st_ref[c] for c in range(len(chains))]
    s = [_dot_nt(q[c], k[c]) for c in range(len(chains))]
    inter = [_bdot((q[c].astype(F32) * xi_ref[h]).astype(BF16), st[c].astype(BF16))
             for c, (bl, h) in enumerate(chains)]
    upd = [_dot_tn((k[c].astype(F32) * zeta_ref[h]).astype(BF16), v[c]) for c, (bl, h) in enumerate(chains)]
    for c, (bl, h) in enumerate(chains):
        o = _bdot((s[c] * dm_ref[h]).astype(BF16), v[c]) + inter[c]
        st_ref[c] = gl_ref[h, 0:1, :] * st[c] + upd[c]
        y_ref[bl, :, col(h)] = (sg_ref[bl, :, col(h)] * _rms(o)).astype(y_ref.dtype)


RET_BATCHES = 2


def _retention(rq, rk, rv, sg, bsz, seq):
    t = rq.shape[0]
    ell = min(RET_CHUNK, seq)
    nc = seq // ell
    log_g = jnp.log(1.0 - jnp.exp2(-5.0 - jnp.arange(RET_HEADS, dtype=F32)))
    idx = jnp.arange(ell, dtype=F32)
    rel = idx[:, None] - idx[None, :]
    dmat = jnp.where(rel >= 0, jnp.exp(log_g[:, None, None] * jnp.maximum(rel, 0.0)), 0.0)
    lanes = lambda a: jnp.broadcast_to(a[..., None], a.shape + (RET_DK,))
    xi = lanes(jnp.exp(log_g[:, None] * (idx + 1.0)))
    zeta = lanes(jnp.exp(log_g[:, None] * (ell - 1.0 - idx)))
    gl = jnp.broadcast_to(jnp.exp(log_g * ell)[:, None, None], (RET_HEADS, SUBLANES, RET_DV))
    nb = min(RET_BATCHES, bsz)
    width = RET_HEADS * RET_DK
    r3 = lambda a: a.reshape(bsz, seq, width)
    blk = pl.BlockSpec((nb, ell, width), lambda b, c: (b, c, 0))
    full = lambda a: pl.BlockSpec(a.shape, lambda b, c: (0, 0, 0))
    y = pl.pallas_call(
        _ret_kernel,
        grid=(bsz // nb, nc),
        in_specs=[blk, blk, blk, blk, full(dmat), full(xi), full(zeta), full(gl)],
        out_specs=blk,
        out_shape=jax.ShapeDtypeStruct((bsz, seq, width), BF16),
        scratch_shapes=[pltpu.VMEM((nb * RET_HEADS, RET_DK, RET_DV), F32)],
        compiler_params=_cparams("parallel", "arbitrary"),
        name="retention",
    )(r3(rq), r3(rk), r3(rv), r3(sg), dmat, xi, zeta, gl)
    return y.reshape(t, width)


def _gla_kernel(q_ref, k_ref, la_ref, v_ref, sg_ref, y_ref, st_ref, b_scr, v_scr, p_scr, r_scr):
    @pl.when(pl.program_id(1) == 0)
    def _():
        st_ref[...] = jnp.zeros_like(st_ref)

    ell, sub = GLA_CHUNK, GLA_SUB
    n_sub = ell // sub
    nb = q_ref.shape[0]
    pairs = GLA_HEADS // 2
    streams = [(bl, p) for bl in range(nb) for p in range(pairs)]
    lane = lax.broadcasted_iota(jnp.int32, (1, LANES), 1)
    first = lane < GLA_DK
    head = (first, jnp.logical_not(first))
    pick = lambda h, a: jnp.where(head[h], a, 0.0).astype(BF16)
    tri = _lower_tri(ell)
    tau = lax.broadcasted_iota(jnp.int32, (sub, LANES), 0)
    row_of, per_sub = [], 0
    for s_ in range(sub):
        row_of.append(per_sub)
        per_sub += sub - (s_ // SUBLANES) * SUBLANES
    rsub = lax.broadcasted_iota(jnp.int32, (LANES, 2 * LANES), 0)
    csub = lax.broadcasted_iota(jnp.int32, (LANES, 2 * LANES), 1)
    ind = ((rsub < GLA_DK) == (csub < LANES)).astype(BF16)

    val = {}
    for sid, (bl, p) in enumerate(streams):
        qk = slice(p * LANES, (p + 1) * LANES)
        q, k = q_ref[bl, :, qk], k_ref[bl, :, qk]
        b = _dot01(tri, la_ref[bl, :, qk])
        b_scr[sid] = b
        v_bf = v_ref[bl, :, p * 2 * GLA_DV:(p + 1) * 2 * GLA_DV]
        v_scr[sid] = v_bf.astype(F32)
        st = st_ref[sid]
        val[sid] = dict(q=q, k=k, b=b, st=st, vh=[v_bf[:, h * GLA_DV:(h + 1) * GLA_DV] for h in range(2)])

    for sid in val:
        d = val[sid]
        qe = d["q"] * jnp.exp(d["b"])
        st_bf = d["st"].astype(BF16)
        d["o"] = [_dot_nt(pick(h, qe), st_bf) for h in range(2)]

    row = lax.broadcasted_iota(jnp.int32, (ell, LANES), 0)
    for sid in val:
        d = val[sid]
        q, k, b = d["q"], d["k"], d["b"]
        qa, ka = [], []
        for i in range(1, n_sub):
            lo = i * sub
            ref_row = b[lo - 1:lo, :]
            in_i = (row >= lo) & (row < lo + sub)
            qa.append(jnp.where(in_i, q * jnp.exp(jnp.minimum(b - ref_row, 0.0)), 0.0))
            ka.append(jnp.where(row < lo, k * jnp.exp(jnp.minimum(ref_row - b, 0.0)), 0.0))
        k_cat = jnp.concatenate(ka, axis=1).astype(BF16)
        d["a_off"] = [_dot_nt(jnp.concatenate([pick(h, x) for x in qa], axis=1), k_cat) for h in range(2)]
    for sid in val:
        d = val[sid]
        d["off"] = [_bdot(d["a_off"][h].astype(BF16), d["vh"][h]) for h in range(2)]

    for sid, (bl, p) in enumerate(streams):
        d = val[sid]
        q, b = d["q"], d["b"]
        for i in range(n_sub):
            lo = i * sub
            qi, bi = q[lo:lo + sub], b[lo:lo + sub]
            for s in range(sub):
                k_row = k_ref[bl, pl.ds(lo + s, 1), p * LANES:(p + 1) * LANES]
                b_row = b_scr[sid, pl.ds(lo + s, 1), :]
                r0 = (s // SUBLANES) * SUBLANES
                w = jnp.exp(jnp.minimum(bi[r0:] - b_row, 0.0))
                tile_s = jnp.where(tau[r0:] >= s, qi[r0:] * k_row * w, 0.0)
                p_scr[sid, pl.ds(i * per_sub + row_of[s], sub - r0), :] = tile_s
    for sid in val:
        r_scr[sid] = _bdot(p_scr[sid].astype(BF16), ind)
    for sid in val:
        diag = [[], []]
        for i in range(n_sub):
            lo = i * sub
            for h in range(2):
                acc = [jnp.zeros((SUBLANES, GLA_DV), F32) for _ in range(sub // SUBLANES)]
                for s in range(sub):
                    v_row = v_scr[sid, pl.ds(lo + s, 1), h * GLA_DV:(h + 1) * GLA_DV]
                    for part in range(s // SUBLANES, sub // SUBLANES):
                        rows = pl.ds(i * per_sub + row_of[s] + (part - s // SUBLANES) * SUBLANES, SUBLANES)
                        acc[part] = acc[part] + r_scr[sid, rows, h * LANES:(h + 1) * LANES] * v_row
                diag[h].append(jnp.concatenate(acc, axis=0))
        val[sid]["diag"] = diag

    for sid, (bl, p) in enumerate(streams):
        d = val[sid]
        b_last = d["b"][ell - 1:ell, :]
        kh = (d["k"] * jnp.exp(b_last - d["b"])).astype(BF16)
        upd = [_dot_tn(d["vh"][h], kh) for h in range(2)]
        st_ref[sid] = d["st"] * jnp.exp(b_last) + jnp.where(first, upd[0], upd[1])
        for h in range(2):
            oh = d["o"][h] + d["off"][h] + jnp.concatenate(d["diag"][h], axis=0)
            cols = slice((2 * p + h) * GLA_DV, (2 * p + h + 1) * GLA_DV)
            y_ref[bl, :, cols] = (sg_ref[bl, :, cols] * _rms(oh)).astype(y_ref.dtype)


GLA_BATCHES = 4


def _gla(gq, gk, la, gv, sr, bsz, seq):
    t = gq.shape[0]
    ell = GLA_CHUNK
    nc = seq // ell
    nb = min(GLA_BATCHES, bsz)
    ns = nb * (GLA_HEADS // 2)
    prod_rows = (ell // GLA_SUB) * sum(GLA_SUB - (s // SUBLANES) * SUBLANES for s in range(GLA_SUB))
    r3 = lambda a: a.reshape(bsz, seq, a.shape[1])
    spec = lambda w: pl.BlockSpec((nb, ell, w), lambda b, c: (b, c, 0))
    wq, wv = GLA_HEADS * GLA_DK, GLA_HEADS * GLA_DV
    y = pl.pallas_call(
        _gla_kernel,
        grid=(bsz // nb, nc),
        in_specs=[spec(wq), spec(wq), spec(wq), spec(wv), spec(wv)],
        out_specs=spec(wv),
        out_shape=jax.ShapeDtypeStruct((bsz, seq, wv), BF16),
        scratch_shapes=[pltpu.VMEM((ns, GLA_DV, LANES), F32),
                        pltpu.VMEM((ns, ell, LANES), F32),
                        pltpu.VMEM((ns, ell, 2 * GLA_DV), F32),
                        pltpu.VMEM((ns, prod_rows, LANES), F32),
                        pltpu.VMEM((ns, prod_rows, 2 * LANES), F32)],
        compiler_params=_cparams("parallel", "arbitrary"),
        name="gla",
    )(r3(gq), r3(gk), r3(la), r3(gv), r3(sr))
    return y.reshape(t, wv)


def _post_tail(m, x_ref, g1_ref, gpost_ref, sh2_ref, sc2_ref, gpre_ref, rw_ref, rb_ref,
               x1_ref, h2_ref, topi_ref, gate_ref, rank_ref, cnt_ref, carry_ref):
    i = pl.program_id(0)
    tm = m.shape[0]
    x1 = x_ref[...] + g1_ref[0] * (_rms(m) * gpost_ref[...])
    x1_ref[...] = x1
    h2 = _rms(x1) * (gpre_ref[...] * (1.0 + sc2_ref[0])) + sh2_ref[0]
    for c in range(D_MODEL // LANES):
        h2_ref[pl.ds(c, tm, stride=SUBLANES), :] = h2[:, c * LANES:(c + 1) * LANES]
    h_hi = h2.astype(BF16)
    h_lo = (h2 - h_hi.astype(F32)).astype(BF16)
    w_both = rw_ref[...]
    hw = _bdot(h_hi, w_both)
    logits = hw[:, :LANES] + hw[:, LANES:] + _bdot(h_lo, w_both[:, :LANES]) + rb_ref[...]

    lane = lax.broadcasted_iota(jnp.int32, (tm, LANES), 1)
    lanef = lane.astype(F32)
    work = logits
    topv = jnp.full((tm, LANES), -jnp.inf, F32)
    topi = jnp.zeros((tm, LANES), F32)
    onehot = jnp.zeros((tm, LANES), F32)
    hits = []
    for k in range(TOP_K):
        mx = jnp.max(work, axis=-1, keepdims=True)
        idx = jnp.min(jnp.where(work == mx, lanef, float(LANES)), axis=-1, keepdims=True)
        hit = lanef == idx
        hits.append(hit)
        topv = jnp.where(lane == k, mx, topv)
        topi = jnp.where(lane == k, idx, topi)
        onehot = onehot + hit.astype(F32)
        work = jnp.where(hit, -jnp.inf, work)
    e = jnp.exp(topv - jnp.max(topv, axis=-1, keepdims=True))
    gate_ref[...] = e / jnp.sum(e, axis=-1, keepdims=True)
    topi_ref[...] = topi.astype(jnp.int32)

    @pl.when(i == 0)
    def _():
        carry_ref[...] = jnp.zeros_like(carry_ref)

    before = _bdot(_lower_tri(tm, strict=True), onehot.astype(BF16)) + carry_ref[...]
    rank = jnp.zeros((tm, LANES), F32)
    for k in range(TOP_K):
        rk = jnp.sum(jnp.where(hits[k], before, 0.0), axis=-1, keepdims=True)
        rank = jnp.where(lane == k, rk, rank)
    rank_ref[...] = rank.astype(jnp.int32)
    total = before[tm - 1:tm, :] + onehot[tm - 1:tm, :]
    carry_ref[...] = total
    cnt_ref[...] = jnp.broadcast_to(total, cnt_ref.shape)


def _out_even_kernel(ys_ref, u_ref, yb_ref, d_ref, gw_ref, gb_ref, wa_ref, wb_ref, *rest):
    y = ys_ref[...] + d_ref[...] * u_ref[...]
    g = jax.nn.gelu(y)
    ya = g * _sigmoid(_bdot(g.astype(BF16), gw_ref[...]) + gb_ref[...])
    m = _bdot(ya.astype(BF16), wa_ref[...]) + _bdot(yb_ref[...], wb_ref[...])
    _post_tail(m, *rest)


def _out_odd_kernel(yc_ref, yd_ref, wa_ref, wb_ref, *rest):
    m = _bdot(yc_ref[...], wa_ref[...]) + _bdot(yd_ref[...], wb_ref[...])
    _post_tail(m, *rest)


def _mixer_out(body, mix_args, mix_specs, x2, mod3, g_post, g_pre, router_w, router_b, seq, tm):
    t, d = x2.shape
    tps = seq // tm
    rw32 = jnp.pad(router_w.astype(F32), ((0, 0), (0, LANES - N_EXPERTS)))
    rw_hi = rw32.astype(BF16)
    rw = jnp.concatenate([rw_hi, (rw32 - rw_hi.astype(F32)).astype(BF16)], axis=1)
    rb = jnp.pad(router_b.astype(F32), (0, LANES - N_EXPERTS), constant_values=-1e30).reshape(1, LANES)
    tok = lambda n: pl.BlockSpec((tm, n), lambda i: (i, 0))
    full = lambda a: pl.BlockSpec(a.shape, lambda i: (0,) * a.ndim)
    tail_args = [x2, mod3, g_post, mod3, mod3, g_pre, rw, rb]
    tail_specs = [tok(d), _mod_spec(2, tps), full(g_post), _mod_spec(3, tps), _mod_spec(4, tps),
                  full(g_pre), full(rw), full(rb)]
    return pl.pallas_call(
        body,
        grid=(t // tm,),
        in_specs=mix_specs + tail_specs,
        out_specs=[tok(d), pl.BlockSpec((tm * SUBLANES, LANES), lambda i: (i, 0)),
                   tok(LANES), tok(LANES), tok(LANES), pl.BlockSpec((SUBLANES, LANES), lambda i: (0, 0))],
        out_shape=[jax.ShapeDtypeStruct((t, d), F32),
                   jax.ShapeDtypeStruct((t * SUBLANES, LANES), F32),
                   jax.ShapeDtypeStruct((t, LANES), jnp.int32),
                   jax.ShapeDtypeStruct((t, LANES), F32),
                   jax.ShapeDtypeStruct((t, LANES), jnp.int32),
                   jax.ShapeDtypeStruct((SUBLANES, LANES), F32)],
        scratch_shapes=[pltpu.VMEM((1, LANES), F32)],
        compiler_params=_cparams("arbitrary"),
        name="mixer_out_router",
    )(*mix_args, *tail_args)


def _out_even(ys, u, yb, d_skip, glu_w, glu_b, w_out, *tail, seq, tm):
    tok = lambda n: pl.BlockSpec((tm, n), lambda i: (i, 0))
    full = lambda a: pl.BlockSpec(a.shape, lambda i: (0,) * a.ndim)
    args = [ys, u, yb, d_skip.reshape(1, -1), glu_w.astype(BF16), glu_b.reshape(1, -1),
            w_out[:S5_CH].astype(BF16), w_out[S5_CH:].astype(BF16)]
    specs = [tok(S5_CH), tok(S5_CH), tok(FOX_W)] + [full(a) for a in args[3:]]
    return _mixer_out(_out_even_kernel, args, specs, *tail, seq, tm)


def _out_odd(yc, yd, w_out, *tail, seq, tm):
    tok = lambda n: pl.BlockSpec((tm, n), lambda i: (i, 0))
    full = lambda a: pl.BlockSpec(a.shape, lambda i: (0,) * a.ndim)
    nc = yc.shape[1]
    args = [yc, yd, w_out[:nc].astype(BF16), w_out[nc:].astype(BF16)]
    specs = [tok(nc), tok(yd.shape[1])] + [full(a) for a in args[2:]]
    return _mixer_out(_out_odd_kernel, args, specs, *tail, seq, tm)


def _route_kernel(topi_ref, rank_ref, cnt_ref, dest_ref, blk_ref, meta_ref):
    tm = topi_ref.shape[0]
    cnt = cnt_ref[...]
    padded = jnp.floor((cnt + (MOE_BLOCK - 1.0)) * (1.0 / MOE_BLOCK)) * MOE_BLOCK
    r = lax.broadcasted_iota(jnp.int32, (LANES, LANES), 0)
    c = lax.broadcasted_iota(jnp.int32, (LANES, LANES), 1)
    hi, mid, lo = _split3(padded)
    incl = (r <= c).astype(BF16)
    pad_end = _bdot(hi, incl) + _bdot(mid, incl) + _bdot(lo, incl)
    pad_start = pad_end - padded
    lane = lax.broadcasted_iota(jnp.int32, (tm, LANES), 1)
    lanef = lane.astype(F32)
    topi = topi_ref[...].astype(F32)
    start_row = pad_start[0:1, :]
    dest = jnp.zeros((tm, LANES), F32)
    for k in range(TOP_K):
        idx = jnp.sum(jnp.where(lane == k, topi, 0.0), axis=-1, keepdims=True)
        st = jnp.sum(jnp.where(lanef == idx, start_row, 0.0), axis=-1, keepdims=True)
        dest = jnp.where(lane == k, st, dest)
    dest_ref[...] = dest.astype(jnp.int32) + rank_ref[...]

    nb = blk_ref.shape[1]
    end_col = jnp.sum(jnp.where(r == c, jnp.broadcast_to(pad_end[0:1, :], (LANES, LANES)), 0.0),
                      axis=-1, keepdims=True)
    jpos = lax.broadcasted_iota(jnp.int32, (LANES, nb), 1).astype(F32) * MOE_BLOCK
    esub = lax.broadcasted_iota(jnp.int32, (LANES, nb), 0)
    done = jnp.where((end_col <= jpos) & (esub < N_EXPERTS), 1.0, 0.0)
    be = jnp.minimum(jnp.sum(done, axis=0, keepdims=True), N_EXPERTS - 1.0)
    blk_ref[...] = jnp.broadcast_to(be, blk_ref.shape).astype(jnp.int32)
    lane1 = lax.broadcasted_iota(jnp.int32, (SUBLANES, LANES), 1)
    n_valid = jnp.sum(jnp.where(lane1 == N_EXPERTS - 1, pad_end, 0.0), axis=-1, keepdims=True) * (1.0 / MOE_BLOCK)
    sub1 = lax.broadcasted_iota(jnp.int32, (SUBLANES, LANES), 0)
    meta = jnp.where(sub1 == 0, pad_start + cnt, jnp.where(sub1 == 1, pad_end, jnp.broadcast_to(n_valid, (SUBLANES, LANES))))
    meta_ref[...] = meta.astype(jnp.int32)


def _route(topi, rank, cnt, n_blocks, tm):
    t = topi.shape[0]
    nb_pad = -(-n_blocks // LANES) * LANES
    tok = pl.BlockSpec((tm, LANES), lambda i: (i, 0))
    fix = lambda n: pl.BlockSpec((SUBLANES, n), lambda i: (0, 0))
    return pl.pallas_call(
        _route_kernel,
        grid=(t // tm,),
        in_specs=[tok, tok, fix(LANES)],
        out_specs=[tok, fix(nb_pad), fix(LANES)],
        out_shape=[jax.ShapeDtypeStruct((t, LANES), jnp.int32),
                   jax.ShapeDtypeStruct((SUBLANES, nb_pad), jnp.int32),
                   jax.ShapeDtypeStruct((SUBLANES, LANES), jnp.int32)],
        compiler_params=_cparams("arbitrary"),
        name="route_plan",
    )(topi, rank, cnt)


def _dispatch_kernel(pad_ref, dest_ref, h_ref, xb_ref, zero_ref, sem_z, sem_s):
    i = pl.program_id(0)
    tm = h_ref.shape[0]

    @pl.when(i == 0)
    def _():
        zero_ref[...] = jnp.zeros_like(zero_ref)
        sizes = [1 << b for b in range(int(math.log2(MOE_BLOCK)) - 1, -1, -1)]

        def fill(e, carry, do_wait):
            start = pad_ref[0, e]
            n_pad = pad_ref[1, e] - start
            off = start
            for sz in sizes:
                take = (n_pad & sz) != 0
                cp = pltpu.make_async_copy(zero_ref.at[pl.ds(0, sz)], xb_ref.at[pl.ds(off, sz)], sem_z)

                @pl.when(take)
                def _():
                    if do_wait:
                        cp.wait()
                    else:
                        cp.start()
                off = off + jnp.where(take, sz, 0)
            return carry

        half = zero_ref.shape[0]

        def fill_unused(j, carry, do_wait):
            for part in range(MOE_BLOCK // half):
                cp = pltpu.make_async_copy(zero_ref, xb_ref.at[pl.ds(j * MOE_BLOCK + part * half, half)], sem_z)
                if do_wait:
                    cp.wait()
                else:
                    cp.start()
            return carry

        n_blocks = xb_ref.shape[0] // MOE_BLOCK
        lax.fori_loop(0, N_EXPERTS, lambda e, c: fill(e, c, False), 0)
        lax.fori_loop(pad_ref[2, 0], n_blocks, lambda j, c: fill_unused(j, c, False), 0)
        lax.fori_loop(0, N_EXPERTS, lambda e, c: fill(e, c, True), 0)
        lax.fori_loop(pad_ref[2, 0], n_blocks, lambda j, c: fill_unused(j, c, True), 0)

    def issue(r, carry):
        for k in range(TOP_K):
            pltpu.make_async_copy(h_ref.at[r], xb_ref.at[dest_ref[r * TOP_K + k]], sem_s).start(priority=k % 2)
        return carry

    lax.fori_loop(0, tm, issue, 0, unroll=ROW_DMA_UNROLL)
    for k in range(TOP_K):
        pltpu.make_async_copy(h_ref, xb_ref.at[pl.ds(0, tm)], sem_s).wait()


def _dispatch(h2t, dest, meta, n_slots, tm):
    t = h2t.shape[0] // SUBLANES
    h3 = h2t.reshape(t, SUBLANES, LANES)
    return pl.pallas_call(
        _dispatch_kernel,
        grid_spec=pltpu.PrefetchScalarGridSpec(
            num_scalar_prefetch=1,
            grid=(t // tm,),
            in_specs=[pl.BlockSpec((tm * TOP_K,), lambda i, p: (i,), memory_space=pltpu.SMEM),
                      pl.BlockSpec((tm, SUBLANES, LANES), lambda i, p: (i, 0, 0))],
            out_specs=pl.BlockSpec(memory_space=pl.ANY),
            scratch_shapes=[pltpu.VMEM((MOE_BLOCK // 2, SUBLANES, LANES), F32),
                            pltpu.SemaphoreType.DMA, pltpu.SemaphoreType.DMA]),
        out_shape=jax.ShapeDtypeStruct((n_slots, SUBLANES, LANES), F32),
        compiler_params=_cparams("arbitrary"),
        name="moe_dispatch",
    )(meta[:3, :N_EXPERTS], dest, h3)


def _expert_kernel(be_ref, nv_ref, nxt_ref, x_ref, wgu_hbm, bgu_ref, wd_hbm, bd_ref, y_ref,
                   wgu_f32, wd_f32, wgu_bf, wd_bf, sem, *, layer):
    j = pl.program_id(0)
    valid = j < nv_ref[0]
    first = valid & ((j == 0) | (be_ref[j] != be_ref[jnp.maximum(j - 1, 0)]))

    def weight_copies(e):
        return (pltpu.make_async_copy(wgu_hbm.at[layer, e], wgu_f32, sem.at[0]),
                pltpu.make_async_copy(wd_hbm.at[layer, e], wd_f32, sem.at[1]))

    @pl.when(j == 0)
    def _():
        for cp in weight_copies(be_ref[0]):
            cp.start()

    @pl.when(first)
    def _():
        for cp in weight_copies(be_ref[j]):
            cp.wait()
        wgu_bf[...] = wgu_f32[...].astype(BF16)
        wd_bf[...] = wd_f32[...].astype(BF16)

        @pl.when(nxt_ref[j] >= 0)
        def _():
            for cp in weight_copies(nxt_ref[j]):
                cp.start()

    @pl.when(valid)
    def _():
        x = jnp.concatenate([x_ref[pl.ds(c, MOE_BLOCK, stride=SUBLANES), :] for c in range(D_MODEL // LANES)],
                            axis=1).astype(BF16)
        gu = _bdot(x, wgu_bf[...]) + bgu_ref[0]
        x_glu = jnp.minimum(gu[:, :D_EXPERT], SWIGLU_LIMIT)
        x_lin = jnp.clip(gu[:, D_EXPERT:], -SWIGLU_LIMIT, SWIGLU_LIMIT)
        act = x_glu * _sigmoid(SWIGLU_ALPHA * x_glu) * (x_lin + 1.0)
        y = _bdot(act.astype(BF16), wd_bf[...]) + bd_ref[0]
        for c in range(D_MODEL // LANES):
            y_ref[pl.ds(c, MOE_BLOCK, stride=SUBLANES), :] = y[:, c * LANES:(c + 1) * LANES]

    @pl.when(jnp.logical_not(valid))
    def _():
        y_ref[...] = jnp.zeros_like(y_ref)


def _experts(xb, block_expert, n_valid, w_gu, b_gu, w_down, b_down, layer):
    n_slots = xb.shape[0]
    n_blocks = n_slots // MOE_BLOCK
    rows = MOE_BLOCK * SUBLANES
    x2 = xb.reshape(n_slots * SUBLANES, LANES)
    depth, ne, d, de2 = w_gu.shape
    idx = jnp.arange(n_blocks, dtype=jnp.int32)
    is_first = ((idx == 0) | (block_expert != jnp.roll(block_expert, 1))) & (idx < n_valid[0])
    first_at = lax.cummin(jnp.where(is_first, idx, n_blocks)[::-1])[::-1]
    next_first = jnp.concatenate([first_at[1:], jnp.full((1,), n_blocks, jnp.int32)])
    nxt = jnp.where(next_first < n_blocks, block_expert[jnp.minimum(next_first, n_blocks - 1)], -1)
    last = lambda j, be, nv, nx: jnp.minimum(j, nv[0] - 1)
    bmap = lambda j, be, nv, nx: (layer, be[last(j, be, nv, nx)], 0, 0)
    return pl.pallas_call(
        functools.partial(_expert_kernel, layer=layer),
        grid_spec=pltpu.PrefetchScalarGridSpec(
            num_scalar_prefetch=3,
            grid=(n_blocks,),
            in_specs=[pl.BlockSpec((rows, LANES), lambda j, be, nv, nx: (last(j, be, nv, nx), 0)),
                      pl.BlockSpec(memory_space=pl.ANY),
                      pl.BlockSpec((None, 1, 1, de2), bmap),
                      pl.BlockSpec(memory_space=pl.ANY),
                      pl.BlockSpec((None, 1, 1, d), bmap)],
            out_specs=pl.BlockSpec((rows, LANES), lambda j, be, nv, nx: (j, 0)),
            scratch_shapes=[pltpu.VMEM((d, de2), F32), pltpu.VMEM((de2 // 2, d), F32),
                            pltpu.VMEM((d, de2), BF16), pltpu.VMEM((de2 // 2, d), BF16),
                            pltpu.SemaphoreType.DMA((2,))]),
        out_shape=jax.ShapeDtypeStruct((n_slots * SUBLANES, LANES), F32),
        compiler_params=_cparams("arbitrary"),
        name="moe_experts",
    )(block_expert, n_valid, nxt.astype(jnp.int32), x2, w_gu, b_gu.reshape(depth, ne, 1, de2), w_down,
      b_down.reshape(depth, ne, 1, d))


def _combine_kernel(dest_ref, dest_next_ref, yb_ref, gate_ref, x1_ref, g2_ref, gpost_ref, o_ref, buf, sem):
    i = pl.program_id(0)
    tm = x1_ref.shape[0]
    slot = i % 2

    def gather(idx_ref, into):
        def issue(r, carry):
            for k in range(TOP_K):
                src = pl.multiple_of(idx_ref[r * TOP_K + k] * SUBLANES, SUBLANES)
                dst = pl.multiple_of((k * tm + r) * SUBLANES, SUBLANES)
                pltpu.make_async_copy(yb_ref.at[pl.ds(src, SUBLANES), :], buf.at[into, pl.ds(dst, SUBLANES), :],
                                      sem.at[into]).start(priority=k % 2)
            return carry
        lax.fori_loop(0, tm, issue, 0, unroll=ROW_DMA_UNROLL)

    @pl.when(i == 0)
    def _():
        gather(dest_ref, 0)

    @pl.when(i + 1 < pl.num_programs(0))
    def _():
        gather(dest_next_ref, 1 - slot)

    pltpu.make_async_copy(yb_ref.at[pl.ds(0, TOP_K * tm * SUBLANES), :], buf.at[slot], sem.at[slot]).wait()
    gates = gate_ref[...]
    gk = [jnp.broadcast_to(gates[:, k:k + 1], (tm, LANES)) for k in range(TOP_K)]
    b2 = buf.at[slot]
    cols = []
    for c in range(D_MODEL // LANES):
        acc = jnp.zeros((tm, LANES), F32)
        for k in range(TOP_K):
            acc = acc + gk[k] * b2[pl.ds(k * tm * SUBLANES + c, tm, stride=SUBLANES), :]
        cols.append(acc)
    f = jnp.concatenate(cols, axis=1)
    o_ref[...] = x1_ref[...] + g2_ref[0] * (_rms(f) * gpost_ref[...])


def _combine(yb, dest, gates, x1, mod3, g_post, seq, tm):
    t, d = x1.shape
    tps = seq // tm
    n = t // tm
    return pl.pallas_call(
        _combine_kernel,
        grid=(n,),
        in_specs=[pl.BlockSpec((tm * TOP_K,), lambda i: (i,), memory_space=pltpu.SMEM),
                  pl.BlockSpec((tm * TOP_K,), lambda i: (jnp.minimum(i + 1, n - 1),), memory_space=pltpu.SMEM),
                  pl.BlockSpec(memory_space=pl.ANY),
                  pl.BlockSpec((tm, LANES), lambda i: (i, 0)),
                  pl.BlockSpec((tm, d), lambda i: (i, 0)),
                  _mod_spec(5, tps),
                  pl.BlockSpec(g_post.shape, lambda i: (0, 0))],
        out_specs=pl.BlockSpec((tm, d), lambda i: (i, 0)),
        out_shape=jax.ShapeDtypeStruct((t, d), F32),
        scratch_shapes=[pltpu.VMEM((2, TOP_K * tm * SUBLANES, LANES), F32), pltpu.SemaphoreType.DMA((2,))],
        compiler_params=_cparams("arbitrary"),
        name="moe_combine",
    )(dest, dest, yb, gates, x1, mod3, g_post)


def _moe(h2t, topi, gates, rank, cnt, x1, mod3, g_post, w_gu, b_gu, w_down, b_down, layer, seq):
    t = x1.shape[0]
    n_blocks = t * TOP_K // MOE_BLOCK + N_EXPERTS
    dest_l, blk, meta = _route(topi, rank, cnt, n_blocks, min(1024, t))
    dest = dest_l[:, :TOP_K].reshape(t * TOP_K)
    xb = _dispatch(h2t, dest, meta, n_blocks * MOE_BLOCK, MOE_BLOCK)
    yb = _experts(xb, blk[0, :n_blocks], meta[2, :1], w_gu, b_gu, w_down, b_down, layer)
    return _combine(yb, dest, gates, x1, mod3, g_post, seq, MOE_BLOCK)


TOKEN_TILE = 512
OUT_TILE = 512
FOX_Q_TILE = 256
FOX_K_TILE = 256


def kernel(x, c, ada_w, ada_b, norm_pre_mix, norm_post_mix, norm_pre_ffn, norm_post_ffn, ev_w_in, fox_b_f, s5_lam_re, s5_lam_im, s5_log_dt, s5_b_re, s5_b_im, s5_c_re, s5_c_im, s5_d, s5_glu_w, s5_glu_b, ev_w_out, od_w_in, gla_w_up, gla_b_gate, od_w_out, router_w, router_b, exp_w_gu, exp_b_gu, exp_w_down, exp_b_down):
    bsz, seq, d = x.shape
    t = bsz * seq
    tm = min(TOKEN_TILE, seq)
    x2 = x.reshape(t, d)
    mod = _modulation(c, ada_w, ada_b)
    for l in range(DEPTH):
        i = l // 2
        mod3 = mod[l].reshape(bsz, 1, 6 * d)
        row = lambda a: a[l].reshape(1, -1)
        tail = (x2, mod3, row(norm_post_mix), row(norm_pre_ffn), router_w[l], router_b[l])
        if l % 2 == 0:
            u, q, k, v_t = _in_even(x2, mod3, row(norm_pre_mix), ev_w_in[i], fox_b_f[i], seq, tm)
            tables = _s5_tables(s5_lam_re[i], s5_lam_im[i], s5_log_dt[i], s5_b_re[i], s5_b_im[i],
                                s5_c_re[i], s5_c_im[i])
            ys = _s5_scan(u, bsz, seq, tables)
            yb = _fox(q, k, v_t, bsz, seq, min(FOX_Q_TILE, seq), min(FOX_K_TILE, seq))
            outs = _out_even(ys, u, yb, s5_d[i], s5_glu_w[i], s5_glu_b[i], ev_w_out[i], *tail, seq=seq, tm=min(OUT_TILE, seq))
        else:
            rq, rk, rv, sg, gq, gk, gv, sr, la = _in_odd(x2, mod3, row(norm_pre_mix), od_w_in[i],
                                                         gla_w_up[i], gla_b_gate[i], seq, tm)
            yc = _retention(rq, rk, rv, sg, bsz, seq)
            yd = _gla(gq, gk, la, gv, sr, bsz, seq)
            outs = _out_odd(yc, yd, od_w_out[i], *tail, seq=seq, tm=min(OUT_TILE, seq))
        x1, h2t, topi, gates, rank, cnt = outs
        x2 = _moe(h2t, topi, gates, rank, cnt, x1, mod3, row(norm_post_ffn),
                  exp_w_gu, exp_b_gu, exp_w_down, exp_b_down, l, seq)
    return x2.reshape(bsz, seq, d)
```

```python
import functools
import math

import jax
import jax.numpy as jnp
from jax import lax
from jax.experimental import pallas as pl
from jax.experimental.pallas import tpu as pltpu

F32 = jnp.float32
BF16 = jnp.bfloat16
HIGHEST = lax.Precision.HIGHEST

D_MODEL = 1024
DEPTH = 2
EPS = 1e-6
S5_CH = 512
S5_GROUP = 16
S5_GROUPS = S5_CH // S5_GROUP
S5_STATE = 64
S5_CHUNK = 8
FOX_HEADS = 8
FOX_DH = 64
FOX_W = FOX_HEADS * FOX_DH
LOG2_E = 1.4426950408889634
FOX_VROWS = FOX_DH + 16
FOX_HPS = 8
RET_HEADS = 4
RET_DK = 128
RET_DV = 128
ROPE_BASE = 10000.0
GLA_HEADS = 4
GLA_DK = 64
GLA_DV = 128
GLA_RANK = 16
GLA_TAU = 16.0
GLA_CHUNK = 64
GLA_SUB = 16
N_EXPERTS = 32
TOP_K = 4
D_EXPERT = 1024
SWIGLU_LIMIT = 7.0
SWIGLU_ALPHA = 1.702
MOE_BLOCK = 256
ROW_DMA_UNROLL = 4

LANES = 128
SUBLANES = 8
VMEM_LIMIT = 56 * 1024 * 1024


def _cparams(*sem):
    return pltpu.CompilerParams(dimension_semantics=sem, vmem_limit_bytes=VMEM_LIMIT)


def _bdot(a, b):
    return jnp.dot(a, b, preferred_element_type=F32)


def _dot_nt(a, b):
    return lax.dot_general(a, b, (((1,), (1,)), ((), ())), preferred_element_type=F32)


def _dot_tn(a, b):
    return lax.dot_general(a, b, (((0,), (0,)), ((), ())), preferred_element_type=F32)


def _split3(x):
    hi = x.astype(BF16)
    r = x - hi.astype(F32)
    mid = r.astype(BF16)
    lo = (r - mid.astype(F32)).astype(BF16)
    return hi, mid, lo


def _dot01(m01, x):
    hi, mid, lo = _split3(x)
    return _bdot(m01, hi) + _bdot(m01, mid) + _bdot(m01, lo)


def _lower_tri(n, strict=False):
    r = lax.broadcasted_iota(jnp.int32, (n, n), 0)
    c = lax.broadcasted_iota(jnp.int32, (n, n), 1)
    return ((r > c) if strict else (r >= c)).astype(BF16)


def _log_sigmoid(x):
    return jnp.minimum(x, 0.0) - jnp.log1p(jnp.exp(-jnp.abs(x)))


def _sigmoid(x):
    return 1.0 / (1.0 + jnp.exp(-x))


def _silu(x):
    return x * _sigmoid(x)


def _rms(x):
    return x * lax.rsqrt(jnp.mean(x * x, axis=-1, keepdims=True) + EPS)


def _mod_kernel(c_ref, w_ref, b_ref, o_ref):
    c = c_ref[...]
    o_ref[0] = jnp.dot(_silu(c), w_ref[0], preferred_element_type=F32, precision=HIGHEST) + b_ref[0]


def _modulation(c, ada_w, ada_b):
    depth, d, n = ada_w.shape
    bsz = c.shape[0]
    tn = D_MODEL
    return pl.pallas_call(
        _mod_kernel,
        grid=(depth, n // tn),
        in_specs=[pl.BlockSpec((bsz, d), lambda l, j: (0, 0)),
                  pl.BlockSpec((1, d, tn), lambda l, j: (l, 0, j)),
                  pl.BlockSpec((1, 1, tn), lambda l, j: (l, 0, j))],
        out_specs=pl.BlockSpec((1, bsz, tn), lambda l, j: (l, 0, j)),
        out_shape=jax.ShapeDtypeStruct((depth, bsz, n), F32),
        compiler_params=_cparams("parallel", "parallel"),
        name="adaln_mod",
    )(c, ada_w, ada_b.reshape(depth, 1, n))


def _mod_spec(chunk, tiles_per_seq):
    return pl.BlockSpec((1, 1, D_MODEL), lambda i: (i // tiles_per_seq, 0, chunk))


def _prenorm(x, g_ref, sc_ref, sh_ref):
    return _rms(x) * (g_ref[...] * (1.0 + sc_ref[0])) + sh_ref[0]


def _in_even_kernel(x_ref, sh_ref, sc_ref, g_ref, w_ref, bf_ref,
                    u_ref, q_ref, k_ref, vt_ref, carry_ref, *, tiles_per_seq):
    i = pl.program_id(0)
    tm = x_ref.shape[0]
    h = _prenorm(x_ref[...], g_ref, sc_ref, sh_ref)
    z = _bdot(h.astype(BF16), w_ref[...])
    u_ref[...] = z[:, 0:S5_CH]
    tail = jnp.concatenate([jnp.ones((1, tm), F32), jnp.zeros((FOX_VROWS - FOX_DH - 1, tm), F32)], axis=0)
    for pr in range(FOX_HEADS // 2):
        c0 = S5_CH + 2 * FOX_W + pr * LANES
        v_pair = z[:, c0:c0 + LANES].T
        for hh in range(2):
            r0 = (2 * pr + hh) * FOX_VROWS
            vt_ref[r0:r0 + FOX_VROWS, :] = jnp.concatenate(
                [v_pair[hh * FOX_DH:(hh + 1) * FOX_DH], tail], axis=0).astype(BF16)
    ls = _log_sigmoid(z[:, S5_CH + 3 * FOX_W:] + bf_ref[...])

    @pl.when(i % tiles_per_seq == 0)
    def _():
        carry_ref[...] = jnp.zeros_like(carry_ref)

    cum = _dot01(_lower_tri(tm), ls) + carry_ref[...]
    carry_ref[...] = cum[tm - 1:tm, :]

    lane = lax.broadcasted_iota(jnp.int32, (1, LANES), 1)
    feat = lane < FOX_DH
    ones = jnp.where(lane < FOX_DH + 3, 1.0, 0.0)
    for hd in range(FOX_HEADS):
        blk = (hd * FOX_DH) // LANES * LANES
        qs = z[:, S5_CH + blk:S5_CH + blk + LANES] * (FOX_DH ** -0.5 * LOG2_E)
        ks = z[:, S5_CH + FOX_W + blk:S5_CH + FOX_W + blk + LANES]
        if (hd * FOX_DH) % LANES:
            qs = pltpu.roll(qs, LANES - FOX_DH, 1)
            ks = pltpu.roll(ks, LANES - FOX_DH, 1)
        nf = jnp.broadcast_to(-LOG2_E * cum[:, hd:hd + 1], (tm, LANES))
        hi = nf.astype(BF16).astype(F32)
        mid = (nf - hi).astype(BF16).astype(F32)
        lo = nf - hi - mid
        bias = jnp.where(lane == FOX_DH, hi, jnp.where(lane == FOX_DH + 1, mid,
                                                       jnp.where(lane == FOX_DH + 2, lo, 0.0)))
        q_ref[:, hd * LANES:(hd + 1) * LANES] = jnp.where(feat, qs, ones).astype(BF16)
        k_ref[:, hd * LANES:(hd + 1) * LANES] = jnp.where(feat, ks, bias).astype(BF16)


def _in_even(x2, mod3, gain, w_in, b_f, seq, tm):
    t, d = x2.shape
    tiles_per_seq = seq // tm
    nw = S5_CH + 3 * FOX_W
    w = jnp.concatenate([w_in[:, :nw], jnp.pad(w_in[:, nw:], ((0, 0), (0, LANES - FOX_HEADS)))],
                        axis=1).astype(BF16)
    bf = jnp.pad(b_f, (0, LANES - FOX_HEADS)).reshape(1, LANES)
    tok = lambda n: pl.BlockSpec((tm, n), lambda i: (i, 0))
    full = lambda a: pl.BlockSpec(a.shape, lambda i: (0,) * a.ndim)
    return pl.pallas_call(
        functools.partial(_in_even_kernel, tiles_per_seq=tiles_per_seq),
        grid=(t // tm,),
        in_specs=[tok(d), _mod_spec(0, tiles_per_seq), _mod_spec(1, tiles_per_seq),
                  full(gain), full(w), full(bf)],
        out_specs=[tok(S5_CH), tok(FOX_HEADS * LANES), tok(FOX_HEADS * LANES),
                   pl.BlockSpec((FOX_HEADS * FOX_VROWS, tm), lambda i: (i // tiles_per_seq, i % tiles_per_seq))],
        out_shape=[jax.ShapeDtypeStruct((t, S5_CH), F32),
                   jax.ShapeDtypeStruct((t, FOX_HEADS * LANES), BF16),
                   jax.ShapeDtypeStruct((t, FOX_HEADS * LANES), BF16),
                   jax.ShapeDtypeStruct((t // seq * FOX_HEADS * FOX_VROWS, seq), BF16)],
        scratch_shapes=[pltpu.VMEM((1, LANES), F32)],
        compiler_params=_cparams("arbitrary"),
        name="in_proj_even",
    )(x2, mod3, mod3, gain, w, bf)


S5_TILE_GROUPS = LANES // S5_GROUP
S5_SEQ_PARTS = 4


def _s5_kernel(u_ref, wt_ref, ws_ref, wc_ref, a_ref, y_ref, x_scr, e_scr, hp_scr, h_scr, *, nb, ncl):
    ell = S5_CHUNK
    sw = S5_TILE_GROUPS * 2 * S5_STATE

    @pl.when(pl.program_id(1) == 0)
    def _():
        h_scr[...] = jnp.zeros_like(h_scr)

    blk = nb * SUBLANES
    for hi in range(ncl // SUBLANES):
        for b in range(nb):
            for t in range(ell):
                x_scr[hi * blk + b * SUBLANES:hi * blk + (b + 1) * SUBLANES, t * LANES:(t + 1) * LANES] = (
                    u_ref[b, pl.ds(hi * SUBLANES * ell + t, SUBLANES, stride=ell), :])
    x = x_scr[...].astype(BF16)
    e = _bdot(x, ws_ref[0])
    tg = S5_TILE_GROUPS
    for j in range(tg):
        e_scr[j] = e[:, j * LANES:(j + 1) * LANES]
        e_scr[tg + j] = pltpu.roll(e[:, j * LANES:(j + 1) * LANES], S5_STATE, 1)
    a1 = jnp.broadcast_to(a_ref[0, 0:1, :], (nb, sw))
    a2 = jnp.broadcast_to(a_ref[0, 1:2, :], (nb, sw))
    a2s = jnp.broadcast_to(a_ref[0, 2:3, :], (nb, sw))

    def body(c, carry):
        h, hs = carry
        rows_c = pl.ds((c // SUBLANES) * blk + c % SUBLANES, nb, stride=SUBLANES)
        for j in range(tg):
            hp_scr[j, rows_c, :] = h[:, j * LANES:(j + 1) * LANES]
        e1 = jnp.concatenate([e_scr[j, rows_c, :] for j in range(tg)], axis=1)
        e2 = jnp.concatenate([e_scr[tg + j, rows_c, :] for j in range(tg)], axis=1)
        return a1 * h + a2 * hs + e1, a1 * hs + a2s * h + e2

    h, hs = lax.fori_loop(0, ncl, body, (h_scr[0], h_scr[1]))
    h_scr[0] = h
    h_scr[1] = hs
    hp = jnp.concatenate([hp_scr[j] for j in range(tg)], axis=1).astype(BF16)
    y = _bdot(x, wt_ref[0]) + _bdot(hp, wc_ref[0])
    for hi in range(ncl // SUBLANES):
        for b in range(nb):
            for t in range(ell):
                y_ref[b, pl.ds(hi * SUBLANES * ell + t, SUBLANES, stride=ell), :] = (
                    y[hi * blk + b * SUBLANES:hi * blk + (b + 1) * SUBLANES, t * LANES:(t + 1) * LANES])


def _s5_tables(lam_re, lam_im, log_dt, b_re, b_im, c_re, c_im):
    ell, p, g = S5_CHUNK, S5_STATE, S5_GROUPS
    lr, li = lam_re.astype(F32), lam_im.astype(F32)
    dt = jnp.exp(log_dt.astype(F32))[:, None]
    mag = jnp.exp(lr * dt)
    a_re, a_im = mag * jnp.cos(li * dt), mag * jnp.sin(li * dt)
    den = lr * lr + li * li
    n_re, n_im = a_re - 1.0, a_im
    z_re = (n_re * lr + n_im * li) / den
    z_im = (n_im * lr - n_re * li) / den
    br, bi = b_re.astype(F32), b_im.astype(F32)
    bb_re = z_re[..., None] * br - z_im[..., None] * bi
    bb_im = z_re[..., None] * bi + z_im[..., None] * br
    j = jnp.arange(ell + 1, dtype=F32)[:, None, None]
    pmag = jnp.exp(j * (lr * dt)[None])
    pr, pi = pmag * jnp.cos(j * (li * dt)[None]), pmag * jnp.sin(j * (li * dt)[None])
    w_re = pr[..., None] * bb_re[None] - pi[..., None] * bb_im[None]
    w_im = pr[..., None] * bb_im[None] + pi[..., None] * bb_re[None]
    cr, ci = c_re.astype(F32), c_im.astype(F32)
    kern = (jnp.einsum('gcp,jgpd->jgcd', cr, w_re[:ell], precision=HIGHEST)
            - jnp.einsum('gcp,jgpd->jgcd', ci, w_im[:ell], precision=HIGHEST))
    s_idx = jnp.arange(ell)[:, None]
    t_idx = jnp.arange(ell)[None, :]
    lag = jnp.clip(t_idx - s_idx, 0, ell - 1)
    toep = kern[lag]
    toep = jnp.where((t_idx >= s_idx)[:, :, None, None, None], toep, 0.0)
    toep = toep.transpose(2, 0, 4, 1, 3)
    rev = jnp.arange(ell - 1, -1, -1)
    st_re = w_re[rev].transpose(1, 0, 3, 2)
    st_im = w_im[rev].transpose(1, 0, 3, 2)
    wst = jnp.concatenate([st_re, st_im], -1)
    p1r, p1i = pr[1:], pi[1:]
    c_hr = cr[None] * p1r[:, :, None, :] - ci[None] * p1i[:, :, None, :]
    c_hi = -cr[None] * p1i[:, :, None, :] - ci[None] * p1r[:, :, None, :]
    cst = jnp.concatenate([c_hr, c_hi], axis=-1).transpose(1, 3, 0, 2)
    al_r, al_i = pr[ell], pi[ell]
    a_rows = jnp.stack([jnp.concatenate([al_r, al_r], -1), jnp.concatenate([-al_i, al_i], -1),
                        jnp.concatenate([al_i, -al_i], -1)], axis=1)

    tg = S5_TILE_GROUPS
    nt = g // tg
    tile = lambda a: a.reshape((nt, tg) + a.shape[1:])
    lane_grp = jnp.arange(LANES) // S5_GROUP
    step_lane_grp = jnp.tile(lane_grp, ell)
    spread_out = (jnp.eye(ell, dtype=BF16)[:, None, :, None, None]
                  * jnp.eye(S5_GROUP, dtype=BF16)[None, :, None, None, :]
                  * jnp.ones((1, 1, 1, tg, 1), BF16)).reshape(ell * S5_GROUP, ell * LANES)
    toep_t = tile(toep).transpose(0, 2, 1, 3, 4, 5).reshape(nt, ell, LANES, ell * S5_GROUP).astype(BF16)
    wt = jnp.einsum('ksrm,mn->ksrn', toep_t, spread_out)
    wt = jnp.where((lane_grp[:, None] == step_lane_grp[None, :])[None, None], wt, 0)
    wt = wt.reshape(nt, ell * LANES, ell * LANES)
    spread_state = jnp.tile(jnp.eye(2 * p, dtype=BF16), (1, tg))
    state_grp = jnp.arange(tg * 2 * p) // (2 * p)
    wst_t = tile(wst).transpose(0, 2, 1, 3, 4).reshape(nt, ell, LANES, 2 * p).astype(BF16)
    ws = jnp.einsum('ksrp,pn->ksrn', wst_t, spread_state)
    ws = jnp.where((lane_grp[:, None] == state_grp[None, :])[None, None], ws, 0)
    ws = ws.reshape(nt, ell * LANES, tg * 2 * p)
    cst_t = tile(cst).reshape(nt, tg * 2 * p, ell * S5_GROUP).astype(BF16)
    wc = jnp.einsum('krm,mn->krn', cst_t, spread_out)
    wc = jnp.where((state_grp[:, None] == step_lane_grp[None, :])[None], wc, 0)
    a_t = tile(a_rows).transpose(0, 2, 1, 3).reshape(nt, 3, tg * 2 * p)
    a_t = jnp.pad(a_t, ((0, 0), (0, SUBLANES - 3), (0, 0)))
    return wt, ws, wc, a_t


def _s5_scan(u, bsz, seq, tables):
    wt, ws, wc, a_t = tables
    ell = S5_CHUNK
    nt = S5_GROUPS // S5_TILE_GROUPS
    parts = S5_SEQ_PARTS if seq % (S5_SEQ_PARTS * ell * 2 * SUBLANES) == 0 else 1
    ncl = seq // parts // ell
    rows = bsz * ncl
    sw = S5_TILE_GROUPS * 2 * S5_STATE
    u3 = u.reshape(bsz, seq, S5_CH)
    io = pl.BlockSpec((bsz, seq // parts, LANES), lambda k, s: (0, s, k))
    per_tile = lambda a: pl.BlockSpec((1,) + a.shape[1:], lambda k, s: (k, 0, 0))
    y = pl.pallas_call(
        functools.partial(_s5_kernel, nb=bsz, ncl=ncl),
        grid=(nt, parts),
        in_specs=[io, per_tile(wt), per_tile(ws), per_tile(wc), per_tile(a_t)],
        out_specs=io,
        out_shape=jax.ShapeDtypeStruct((bsz, seq, S5_CH), F32),
        scratch_shapes=[pltpu.VMEM((rows, ell * LANES), F32),
                        pltpu.VMEM((2 * S5_TILE_GROUPS, rows, LANES), F32),
                        pltpu.VMEM((S5_TILE_GROUPS, rows, LANES), F32),
                        pltpu.VMEM((2, bsz, sw), F32)],
        compiler_params=_cparams("parallel", "arbitrary"),
        name="s5_scan",
    )(u3, wt, ws, wc, a_t)
    return y.reshape(bsz * seq, S5_CH)


def _fox_kernel(q_ref, k_ref, vt_ref, o_ref, *, tq, tk):
    i = pl.program_id(2)
    nh = FOX_HPS
    q = [q_ref[:, h * LANES:(h + 1) * LANES] for h in range(nh)]
    key = lax.broadcasted_iota(jnp.int32, (tk, tq), 0)
    qry = lax.broadcasted_iota(jnp.int32, (tk, tq), 1)
    per_q = tq // tk

    def block(j0, carry, mask):
        kj = k_ref[pl.ds(j0, tk), :]
        vtj = vt_ref[:, pl.ds(j0, tk)]
        ss = [_dot_nt(kj[:, h * LANES:(h + 1) * LANES], q[h]) for h in range(nh)]
        stats = []
        for h in range(nh):
            m, acc = carry[h]
            s = ss[h] if mask is None else jnp.where(mask, ss[h], -jnp.inf)
            m_new = jnp.maximum(m, jnp.max(s, axis=0, keepdims=True))
            p = jnp.exp2(s - m_new)
            stats.append((m_new, jnp.exp2(m - m_new), p.astype(BF16)))
        out = []
        for h in range(nh):
            m_new, alpha, p = stats[h]
            acc = alpha * carry[h][1] + _bdot(vtj[h * FOX_VROWS:(h + 1) * FOX_VROWS, :], p)
            out.append((m_new, acc))
        return tuple(out)

    init = tuple((jnp.full((1, tq), -jnp.inf, F32), jnp.zeros((FOX_VROWS, tq), F32)) for _ in range(nh))
    carry = lax.fori_loop(0, i * per_q, lambda j, c: block(pl.multiple_of(j * tk, tk), c, None), init)
    for d in range(per_q):
        carry = block(pl.multiple_of(i * tq + d * tk, tk), carry, key + d * tk <= qry)
    for g in range(nh // 2):
        o_t = jnp.concatenate([carry[h][1][:FOX_DH] / carry[h][1][FOX_DH:FOX_DH + 1]
                               for h in (2 * g, 2 * g + 1)], axis=0)
        o_ref[:, g * LANES:(g + 1) * LANES] = o_t.T.astype(o_ref.dtype)


def _fox(q_aug, k_aug, v_t, bsz, seq, tq, tk):
    t = q_aug.shape[0]
    nh = FOX_HPS
    groups = FOX_HEADS // nh
    nq = seq // tq
    return pl.pallas_call(
        functools.partial(_fox_kernel, tq=tq, tk=tk),
        grid=(bsz, groups, nq),
        in_specs=[pl.BlockSpec((tq, nh * LANES), lambda b, p, i: (b * nq + i, p)),
                  pl.BlockSpec((seq, nh * LANES), lambda b, p, i: (b, p)),
                  pl.BlockSpec((nh * FOX_VROWS, seq), lambda b, p, i: (b * groups + p, 0))],
        out_specs=pl.BlockSpec((tq, nh * FOX_DH), lambda b, p, i: (b * nq + i, p)),
        out_shape=jax.ShapeDtypeStruct((t, FOX_W), BF16),
        compiler_params=_cparams("parallel", "parallel", "arbitrary"),
        name="fox_attention",
    )(q_aug, k_aug, v_t)


_ODD_COLS = (("rq", 512), ("rk", 512), ("rv", 512), ("rg", 512), ("gq", 256), ("gk", 256),
             ("gv", 512), ("gr", 512), ("glr", LANES))


def _odd_offsets():
    off, out = 0, {}
    for name, w in _ODD_COLS:
        out[name] = (off, off + w)
        off += w
    return out, off


def _in_odd_kernel(x_ref, sh_ref, sc_ref, g_ref, w_ref, cos_ref, sin_ref, wup_ref, bg_ref,
                   rq_ref, rk_ref, rv_ref, sg_ref, gq_ref, gk_ref, gv_ref, sr_ref, la_ref):
    h = _prenorm(x_ref[...], g_ref, sc_ref, sh_ref)
    z = _bdot(h.astype(BF16), w_ref[...])
    off, _ = _odd_offsets()
    col = lambda n: z[:, off[n][0]:off[n][1]]
    cos, sin = cos_ref[...], sin_ref[...]

    def rope(t, scale):
        heads = []
        for hd in range(RET_HEADS):
            th = t[:, hd * RET_DK:(hd + 1) * RET_DK]
            heads.append((th * cos + pltpu.roll(th, RET_DK // 2, 1) * sin) * scale)
        return jnp.concatenate(heads, axis=1).astype(BF16)

    rq_ref[...] = rope(col("rq"), 1.0)
    rk_ref[...] = rope(col("rk"), RET_DK ** -0.5)
    rv_ref[...] = col("rv").astype(BF16)
    sg_ref[...] = _silu(col("rg"))
    gq_ref[...] = col("gq") * (GLA_DK ** -0.5)
    gk_ref[...] = col("gk")
    gv_ref[...] = col("gv").astype(BF16)
    sr_ref[...] = _silu(col("gr"))
    gate = jnp.dot(col("glr"), wup_ref[...], preferred_element_type=F32, precision=HIGHEST) + bg_ref[...]
    la_ref[...] = _log_sigmoid(gate) * (1.0 / GLA_TAU)


def _in_odd(x2, mod3, gain, w_in, w_up, b_gate, seq, tm):
    t, d = x2.shape
    tps = seq // tm
    ref_w = (512, 512, 512, 512, 256, 256, 512, GLA_RANK, 512)
    starts = [0]
    for wd in ref_w:
        starts.append(starts[-1] + wd)
    seg = lambda j: w_in[:, starts[j]:starts[j + 1]]
    w = jnp.concatenate([seg(0), seg(1), seg(2), seg(3), seg(4), seg(5), seg(6), seg(8),
                         jnp.pad(seg(7), ((0, 0), (0, LANES - GLA_RANK)))], axis=1).astype(BF16)
    wup = jnp.pad(w_up.astype(F32), ((0, LANES - GLA_RANK), (0, 0)))
    bg = b_gate.reshape(1, -1).astype(F32)
    half = RET_DK // 2
    inv = ROPE_BASE ** (-jnp.arange(half, dtype=F32) / half)
    ang = jnp.arange(seq, dtype=F32)[:, None] * inv[None, :]
    cos = jnp.concatenate([jnp.cos(ang), jnp.cos(ang)], axis=1)
    sin = jnp.concatenate([-jnp.sin(ang), jnp.sin(ang)], axis=1)
    tok = lambda n: pl.BlockSpec((tm, n), lambda i: (i, 0))
    full = lambda a: pl.BlockSpec(a.shape, lambda i: (0,) * a.ndim)
    pos = pl.BlockSpec((tm, RET_DK), lambda i: (i % tps, 0))
    widths = (512, 512, 512, 512, 256, 256, 512, 512, 256)
    dtypes = (BF16, BF16, BF16, F32, F32, F32, BF16, F32, F32)
    return pl.pallas_call(
        _in_odd_kernel,
        grid=(t // tm,),
        in_specs=[tok(d), _mod_spec(0, tps), _mod_spec(1, tps), full(gain), full(w), pos, pos,
                  full(wup), full(bg)],
        out_specs=[tok(n) for n in widths],
        out_shape=[jax.ShapeDtypeStruct((t, n), dt) for n, dt in zip(widths, dtypes)],
        compiler_params=_cparams("parallel"),
        name="in_proj_odd",
    )(x2, mod3, mod3, gain, w, cos, sin, wup, bg)


RET_CHUNK = 256


def _ret_kernel(q_ref, k_ref, v_ref, sg_ref, dm_ref, xi_ref, zeta_ref, gl_ref, y_ref, st_ref):
    @pl.when(pl.program_id(1) == 0)
    def _():
        st_ref[...] = jnp.zeros_like(st_ref)

    nb = q_ref.shape[0]
    chains = [(bl, h) for bl in range(nb) for h in range(RET_HEADS)]
    col = lambda h: slice(h * RET_DK, (h + 1) * RET_DK)
    q = [q_ref[bl, :, col(h)] for bl, h in chains]
    k = [k_ref[bl, :, col(h)] for bl, h in chains]
    v = [v_ref[bl, :, col(h)] for bl, h in chains]
    st = [st_ref[c] for c in range(len(chains))]
    s = [_dot_nt(q[c], k[c]) for c in range(len(chains))]
    inter = [_bdot((q[c].astype(F32) * xi_ref[h]).astype(BF16), st[c].astype(BF16))
             for c, (bl, h) in enumerate(chains)]
    upd = [_dot_tn((k[c].astype(F32) * zeta_ref[h]).astype(BF16), v[c]) for c, (bl, h) in enumerate(chains)]
    for c, (bl, h) in enumerate(chains):
        o = _bdot((s[c] * dm_ref[h]).astype(BF16), v[c]) + inter[c]
        st_ref[c] = gl_ref[h, 0:1, :] * st[c] + upd[c]
        y_ref[bl, :, col(h)] = (sg_ref[bl, :, col(h)] * _rms(o)).astype(y_ref.dtype)


RET_BATCHES = 2


def _retention(rq, rk, rv, sg, bsz, seq):
    t = rq.shape[0]
    ell = min(RET_CHUNK, seq)
    nc = seq // ell
    log_g = jnp.log(1.0 - jnp.exp2(-5.0 - jnp.arange(RET_HEADS, dtype=F32)))
    idx = jnp.arange(ell, dtype=F32)
    rel = idx[:, None] - idx[None, :]
    dmat = jnp.where(rel >= 0, jnp.exp(log_g[:, None, None] * jnp.maximum(rel, 0.0)), 0.0)
    lanes = lambda a: jnp.broadcast_to(a[..., None], a.shape + (RET_DK,))
    xi = lanes(jnp.exp(log_g[:, None] * (idx + 1.0)))
    zeta = lanes(jnp.exp(log_g[:, None] * (ell - 1.0 - idx)))
    gl = jnp.broadcast_to(jnp.exp(log_g * ell)[:, None, None], (RET_HEADS, SUBLANES, RET_DV))
    nb = min(RET_BATCHES, bsz)
    width = RET_HEADS * RET_DK
    r3 = lambda a: a.reshape(bsz, seq, width)
    blk = pl.BlockSpec((nb, ell, width), lambda b, c: (b, c, 0))
    full = lambda a: pl.BlockSpec(a.shape, lambda b, c: (0, 0, 0))
    y = pl.pallas_call(
        _ret_kernel,
        grid=(bsz // nb, nc),
        in_specs=[blk, blk, blk, blk, full(dmat), full(xi), full(zeta), full(gl)],
        out_specs=blk,
        out_shape=jax.ShapeDtypeStruct((bsz, seq, width), BF16),
        scratch_shapes=[pltpu.VMEM((nb * RET_HEADS, RET_DK, RET_DV), F32)],
        compiler_params=_cparams("parallel", "arbitrary"),
        name="retention",
    )(r3(rq), r3(rk), r3(rv), r3(sg), dmat, xi, zeta, gl)
    return y.reshape(t, width)


def _gla_kernel(q_ref, k_ref, la_ref, v_ref, sg_ref, y_ref, st_ref, b_scr, v_scr, p_scr, r_scr):
    @pl.when(pl.program_id(1) == 0)
    def _():
        st_ref[...] = jnp.zeros_like(st_ref)

    ell, sub = GLA_CHUNK, GLA_SUB
    n_sub = ell // sub
    nb = q_ref.shape[0]
    pairs = GLA_HEADS // 2
    streams = [(bl, p) for bl in range(nb) for p in range(pairs)]
    lane = lax.broadcasted_iota(jnp.int32, (1, LANES), 1)
    first = lane < GLA_DK
    head = (first, jnp.logical_not(first))
    pick = lambda h, a: jnp.where(head[h], a, 0.0).astype(BF16)
    tri = _lower_tri(ell)
    tau = lax.broadcasted_iota(jnp.int32, (sub, LANES), 0)
    row_of, per_sub = [], 0
    for s_ in range(sub):
        row_of.append(per_sub)
        per_sub += sub - (s_ // SUBLANES) * SUBLANES
    rsub = lax.broadcasted_iota(jnp.int32, (LANES, 2 * LANES), 0)
    csub = lax.broadcasted_iota(jnp.int32, (LANES, 2 * LANES), 1)
    ind = ((rsub < GLA_DK) == (csub < LANES)).astype(BF16)

    val = {}
    for sid, (bl, p) in enumerate(streams):
        qk = slice(p * LANES, (p + 1) * LANES)
        q, k = q_ref[bl, :, qk], k_ref[bl, :, qk]
        b = _dot01(tri, la_ref[bl, :, qk])
        b_scr[sid] = b
        v_bf = v_ref[bl, :, p * 2 * GLA_DV:(p + 1) * 2 * GLA_DV]
        v_scr[sid] = v_bf.astype(F32)
        st = st_ref[sid]
        val[sid] = dict(q=q, k=k, b=b, st=st, vh=[v_bf[:, h * GLA_DV:(h + 1) * GLA_DV] for h in range(2)])

    for sid in val:
        d = val[sid]
        qe = d["q"] * jnp.exp(d["b"])
        st_bf = d["st"].astype(BF16)
        d["o"] = [_dot_nt(pick(h, qe), st_bf) for h in range(2)]

    row = lax.broadcasted_iota(jnp.int32, (ell, LANES), 0)
    for sid in val:
        d = val[sid]
        q, k, b = d["q"], d["k"], d["b"]
        qa, ka = [], []
        for i in range(1, n_sub):
            lo = i * sub
            ref_row = b[lo - 1:lo, :]
            in_i = (row >= lo) & (row < lo + sub)
            qa.append(jnp.where(in_i, q * jnp.exp(jnp.minimum(b - ref_row, 0.0)), 0.0))
            ka.append(jnp.where(row < lo, k * jnp.exp(jnp.minimum(ref_row - b, 0.0)), 0.0))
        k_cat = jnp.concatenate(ka, axis=1).astype(BF16)
        d["a_off"] = [_dot_nt(jnp.concatenate([pick(h, x) for x in qa], axis=1), k_cat) for h in range(2)]
    for sid in val:
        d = val[sid]
        d["off"] = [_bdot(d["a_off"][h].astype(BF16), d["vh"][h]) for h in range(2)]

    for sid, (bl, p) in enumerate(streams):
        d = val[sid]
        q, b = d["q"], d["b"]
        for i in range(n_sub):
            lo = i * sub
            qi, bi = q[lo:lo + sub], b[lo:lo + sub]
            for s in range(sub):
                k_row = k_ref[bl, pl.ds(lo + s, 1), p * LANES:(p + 1) * LANES]
                b_row = b_scr[sid, pl.ds(lo + s, 1), :]
                r0 = (s // SUBLANES) * SUBLANES
                w = jnp.exp(jnp.minimum(bi[r0:] - b_row, 0.0))
                tile_s = jnp.where(tau[r0:] >= s, qi[r0:] * k_row * w, 0.0)
                p_scr[sid, pl.ds(i * per_sub + row_of[s], sub - r0), :] = tile_s
    for sid in val:
        r_scr[sid] = _bdot(p_scr[sid].astype(BF16), ind)
    for sid in val:
        diag = [[], []]
        for i in range(n_sub):
            lo = i * sub
            for h in range(2):
                acc = [jnp.zeros((SUBLANES, GLA_DV), F32) for _ in range(sub // SUBLANES)]
                for s in range(sub):
                    v_row = v_scr[sid, pl.ds(lo + s, 1), h * GLA_DV:(h + 1) * GLA_DV]
                    for part in range(s // SUBLANES, sub // SUBLANES):
                        rows = pl.ds(i * per_sub + row_of[s] + (part - s // SUBLANES) * SUBLANES, SUBLANES)
                        acc[part] = acc[part] + r_scr[sid, rows, h * LANES:(h + 1) * LANES] * v_row
                diag[h].append(jnp.concatenate(acc, axis=0))
        val[sid]["diag"] = diag

    for sid, (bl, p) in enumerate(streams):
        d = val[sid]
        b_last = d["b"][ell - 1:ell, :]
        kh = (d["k"] * jnp.exp(b_last - d["b"])).astype(BF16)
        upd = [_dot_tn(d["vh"][h], kh) for h in range(2)]
        st_ref[sid] = d["st"] * jnp.exp(b_last) + jnp.where(first, upd[0], upd[1])
        for h in range(2):
            oh = d["o"][h] + d["off"][h] + jnp.concatenate(d["diag"][h], axis=0)
            cols = slice((2 * p + h) * GLA_DV, (2 * p + h + 1) * GLA_DV)
            y_ref[bl, :, cols] = (sg_ref[bl, :, cols] * _rms(oh)).astype(y_ref.dtype)


GLA_BATCHES = 4


def _gla(gq, gk, la, gv, sr, bsz, seq):
    t = gq.shape[0]
    ell = GLA_CHUNK
    nc = seq // ell
    nb = min(GLA_BATCHES, bsz)
    ns = nb * (GLA_HEADS // 2)
    prod_rows = (ell // GLA_SUB) * sum(GLA_SUB - (s // SUBLANES) * SUBLANES for s in range(GLA_SUB))
    r3 = lambda a: a.reshape(bsz, seq, a.shape[1])
    spec = lambda w: pl.BlockSpec((nb, ell, w), lambda b, c: (b, c, 0))
    wq, wv = GLA_HEADS * GLA_DK, GLA_HEADS * GLA_DV
    y = pl.pallas_call(
        _gla_kernel,
        grid=(bsz // nb, nc),
        in_specs=[spec(wq), spec(wq), spec(wq), spec(wv), spec(wv)],
        out_specs=spec(wv),
        out_shape=jax.ShapeDtypeStruct((bsz, seq, wv), BF16),
        scratch_shapes=[pltpu.VMEM((ns, GLA_DV, LANES), F32),
                        pltpu.VMEM((ns, ell, LANES), F32),
                        pltpu.VMEM((ns, ell, 2 * GLA_DV), F32),
                        pltpu.VMEM((ns, prod_rows, LANES), F32),
                        pltpu.VMEM((ns, prod_rows, 2 * LANES), F32)],
        compiler_params=_cparams("parallel", "arbitrary"),
        name="gla",
    )(r3(gq), r3(gk), r3(la), r3(gv), r3(sr))
    return y.reshape(t, wv)


ROUTE_IDX, ROUTE_GATE, ROUTE_RANK = 0, SUBLANES, 2 * SUBLANES


def _post_tail(m, x_ref, g1_ref, gpost_ref, sh2_ref, sc2_ref, gpre_ref, rw_ref, rb_ref,
               x1_ref, h2_ref, route_ref, cnt_ref, carry_ref):
    i = pl.program_id(0)
    tm = m.shape[0]
    x1 = x_ref[...] + g1_ref[0] * (_rms(m) * gpost_ref[...])
    x1_ref[...] = x1
    h2 = _rms(x1) * (gpre_ref[...] * (1.0 + sc2_ref[0])) + sh2_ref[0]
    for c in range(D_MODEL // LANES):
        h2_ref[pl.ds(c, tm, stride=SUBLANES), :] = h2[:, c * LANES:(c + 1) * LANES]

    ne = N_EXPERTS
    h_hi = h2.astype(BF16)
    h_lo = (h2 - h_hi.astype(F32)).astype(BF16)
    w_both = rw_ref[...]
    hw = _dot_nt(w_both, h_hi)
    bias = jnp.concatenate([rb_ref[...]] * (tm // LANES), axis=1)
    work = hw[:ne] + hw[ne:] + _dot_nt(w_both[:ne], h_lo) + bias
    esub = lax.broadcasted_iota(jnp.int32, (ne, tm), 0).astype(F32)
    onehot = jnp.zeros((ne, tm), F32)
    hits, vals, idxs = [], [], []
    for k in range(TOP_K):
        mx = jnp.max(work, axis=0, keepdims=True)
        idx = jnp.min(jnp.where(work == mx, esub, float(ne)), axis=0, keepdims=True)
        hit = esub == idx
        hits.append(hit)
        vals.append(mx)
        idxs.append(idx)
        onehot = onehot + hit.astype(F32)
        work = jnp.where(hit, -jnp.inf, work)
    es = [jnp.exp(v - vals[0]) for v in vals]
    inv = 1.0 / (es[0] + es[1] + es[2] + es[3])
    gates = [e * inv for e in es]

    @pl.when(i == 0)
    def _():
        carry_ref[...] = jnp.zeros_like(carry_ref)

    r = lax.broadcasted_iota(jnp.int32, (tm, tm), 0)
    c = lax.broadcasted_iota(jnp.int32, (tm, tm), 1)
    carry = carry_ref[...]
    before = _bdot(onehot.astype(BF16), (r < c).astype(BF16)) + jnp.concatenate([carry[:ne]] * (tm // LANES), axis=1)
    ranks = [jnp.sum(jnp.where(hits[k], before, 0.0), axis=0, keepdims=True) for k in range(TOP_K)]
    carry = carry + jnp.concatenate(
        [jnp.broadcast_to(jnp.sum(onehot, axis=1, keepdims=True), (ne, LANES)),
         jnp.zeros((LANES - ne, LANES), F32)], axis=0)
    carry_ref[...] = carry
    cnt_ref[...] = carry.T[:SUBLANES]

    sub = lax.broadcasted_iota(jnp.int32, (SUBLANES, tm), 0)

    def rows(vs):
        out = jnp.zeros((SUBLANES, tm), F32)
        for k, v in enumerate(vs):
            out = jnp.where(sub == k, jnp.broadcast_to(v, (SUBLANES, tm)), out)
        return out

    rec = jnp.concatenate([rows(idxs), rows(gates), rows(ranks), jnp.zeros((LANES - 3 * SUBLANES, tm), F32)], axis=0)
    route_ref[...] = rec.T


def _out_even_kernel(ys_ref, u_ref, yb_ref, d_ref, gw_ref, gb_ref, wa_ref, wb_ref, *rest):
    y = ys_ref[...] + d_ref[...] * u_ref[...]
    g = jax.nn.gelu(y)
    ya = g * _sigmoid(_bdot(g.astype(BF16), gw_ref[...]) + gb_ref[...])
    m = _bdot(ya.astype(BF16), wa_ref[...]) + _bdot(yb_ref[...], wb_ref[...])
    _post_tail(m, *rest)


def _out_odd_kernel(yc_ref, yd_ref, wa_ref, wb_ref, *rest):
    m = _bdot(yc_ref[...], wa_ref[...]) + _bdot(yd_ref[...], wb_ref[...])
    _post_tail(m, *rest)


def _mixer_out(body, mix_args, mix_specs, x2, mod3, g_post, g_pre, router_w, router_b, seq, tm):
    t, d = x2.shape
    tps = seq // tm
    rw32 = router_w.astype(F32).T
    rw_hi = rw32.astype(BF16)
    rw = jnp.concatenate([rw_hi, (rw32 - rw_hi.astype(F32)).astype(BF16)], axis=0)
    rb = jnp.broadcast_to(router_b.astype(F32)[:, None], (N_EXPERTS, LANES))
    tok = lambda n: pl.BlockSpec((tm, n), lambda i: (i, 0))
    full = lambda a: pl.BlockSpec(a.shape, lambda i: (0,) * a.ndim)
    tail_args = [x2, mod3, g_post, mod3, mod3, g_pre, rw, rb]
    tail_specs = [tok(d), _mod_spec(2, tps), full(g_post), _mod_spec(3, tps), _mod_spec(4, tps),
                  full(g_pre), full(rw), full(rb)]
    return pl.pallas_call(
        body,
        grid=(t // tm,),
        in_specs=mix_specs + tail_specs,
        out_specs=[tok(d), pl.BlockSpec((tm * SUBLANES, LANES), lambda i: (i, 0)),
                   tok(LANES), pl.BlockSpec((SUBLANES, LANES), lambda i: (0, 0))],
        out_shape=[jax.ShapeDtypeStruct((t, d), F32),
                   jax.ShapeDtypeStruct((t * SUBLANES, LANES), F32),
                   jax.ShapeDtypeStruct((t, LANES), F32),
                   jax.ShapeDtypeStruct((SUBLANES, LANES), F32)],
        scratch_shapes=[pltpu.VMEM((LANES, LANES), F32)],
        compiler_params=_cparams("arbitrary"),
        name="mixer_out_router",
    )(*mix_args, *tail_args)


def _out_even(ys, u, yb, d_skip, glu_w, glu_b, w_out, *tail, seq, tm):
    tok = lambda n: pl.BlockSpec((tm, n), lambda i: (i, 0))
    full = lambda a: pl.BlockSpec(a.shape, lambda i: (0,) * a.ndim)
    args = [ys, u, yb, d_skip.reshape(1, -1), glu_w.astype(BF16), glu_b.reshape(1, -1),
            w_out[:S5_CH].astype(BF16), w_out[S5_CH:].astype(BF16)]
    specs = [tok(S5_CH), tok(S5_CH), tok(FOX_W)] + [full(a) for a in args[3:]]
    return _mixer_out(_out_even_kernel, args, specs, *tail, seq, tm)


def _out_odd(yc, yd, w_out, *tail, seq, tm):
    tok = lambda n: pl.BlockSpec((tm, n), lambda i: (i, 0))
    full = lambda a: pl.BlockSpec(a.shape, lambda i: (0,) * a.ndim)
    nc = yc.shape[1]
    args = [yc, yd, w_out[:nc].astype(BF16), w_out[nc:].astype(BF16)]
    specs = [tok(nc), tok(yd.shape[1])] + [full(a) for a in args[2:]]
    return _mixer_out(_out_odd_kernel, args, specs, *tail, seq, tm)


def _route_kernel(route_ref, cnt_ref, dest_ref, blk_ref, meta_ref):
    tm = route_ref.shape[0]
    cnt = cnt_ref[...]
    padded = jnp.floor((cnt + (MOE_BLOCK - 1.0)) * (1.0 / MOE_BLOCK)) * MOE_BLOCK
    r = lax.broadcasted_iota(jnp.int32, (LANES, LANES), 0)
    c = lax.broadcasted_iota(jnp.int32, (LANES, LANES), 1)
    hi, mid, lo = _split3(padded)
    incl = (r <= c).astype(BF16)
    pad_end = _bdot(hi, incl) + _bdot(mid, incl) + _bdot(lo, incl)
    pad_start = pad_end - padded
    lane = lax.broadcasted_iota(jnp.int32, (tm, LANES), 1)
    lanef = lane.astype(F32)
    rec = route_ref[...]
    topi = rec
    rank = pltpu.roll(rec, LANES - ROUTE_RANK, 1)
    start_row = pad_start[0:1, :]
    dest = jnp.zeros((tm, LANES), F32)
    for k in range(TOP_K):
        idx = jnp.sum(jnp.where(lane == k, topi, 0.0), axis=-1, keepdims=True)
        st = jnp.sum(jnp.where(lanef == idx, start_row, 0.0), axis=-1, keepdims=True)
        dest = jnp.where(lane == k, st, dest)
    dest_ref[...] = (dest + rank).astype(jnp.int32)

    nb = blk_ref.shape[1]
    end_col = jnp.sum(jnp.where(r == c, jnp.broadcast_to(pad_end[0:1, :], (LANES, LANES)), 0.0),
                      axis=-1, keepdims=True)
    jpos = lax.broadcasted_iota(jnp.int32, (LANES, nb), 1).astype(F32) * MOE_BLOCK
    esub = lax.broadcasted_iota(jnp.int32, (LANES, nb), 0)
    done = jnp.where((end_col <= jpos) & (esub < N_EXPERTS), 1.0, 0.0)
    be = jnp.minimum(jnp.sum(done, axis=0, keepdims=True), N_EXPERTS - 1.0)
    blk_ref[...] = jnp.broadcast_to(be, blk_ref.shape).astype(jnp.int32)
    lane1 = lax.broadcasted_iota(jnp.int32, (SUBLANES, LANES), 1)
    n_valid = jnp.sum(jnp.where(lane1 == N_EXPERTS - 1, pad_end, 0.0), axis=-1, keepdims=True) * (1.0 / MOE_BLOCK)
    sub1 = lax.broadcasted_iota(jnp.int32, (SUBLANES, LANES), 0)
    meta = jnp.where(sub1 == 0, pad_start + cnt, jnp.where(sub1 == 1, pad_end, jnp.broadcast_to(n_valid, (SUBLANES, LANES))))
    meta_ref[...] = meta.astype(jnp.int32)


def _route(route, cnt, n_blocks, tm):
    t = route.shape[0]
    nb_pad = -(-n_blocks // LANES) * LANES
    tok = pl.BlockSpec((tm, LANES), lambda i: (i, 0))
    fix = lambda n: pl.BlockSpec((SUBLANES, n), lambda i: (0, 0))
    return pl.pallas_call(
        _route_kernel,
        grid=(t // tm,),
        in_specs=[tok, fix(LANES)],
        out_specs=[tok, fix(nb_pad), fix(LANES)],
        out_shape=[jax.ShapeDtypeStruct((t, LANES), jnp.int32),
                   jax.ShapeDtypeStruct((SUBLANES, nb_pad), jnp.int32),
                   jax.ShapeDtypeStruct((SUBLANES, LANES), jnp.int32)],
        compiler_params=_cparams("arbitrary"),
        name="route_plan",
    )(route, cnt)


def _dispatch_kernel(pad_ref, dest_ref, h_ref, xb_ref, zero_ref, sem_z, sem_s):
    i = pl.program_id(0)
    tm = h_ref.shape[0]

    @pl.when(i == 0)
    def _():
        zero_ref[...] = jnp.zeros_like(zero_ref)
        sizes = [1 << b for b in range(int(math.log2(MOE_BLOCK)) - 1, -1, -1)]

        def fill(e, carry, do_wait):
            start = pad_ref[0, e]
            n_pad = pad_ref[1, e] - start
            off = start
            for sz in sizes:
                take = (n_pad & sz) != 0
                cp = pltpu.make_async_copy(zero_ref.at[pl.ds(0, sz)], xb_ref.at[pl.ds(off, sz)], sem_z)

                @pl.when(take)
                def _():
                    if do_wait:
                        cp.wait()
                    else:
                        cp.start()
                off = off + jnp.where(take, sz, 0)
            return carry

        half = zero_ref.shape[0]

        def fill_unused(j, carry, do_wait):
            for part in range(MOE_BLOCK // half):
                cp = pltpu.make_async_copy(zero_ref, xb_ref.at[pl.ds(j * MOE_BLOCK + part * half, half)], sem_z)
                if do_wait:
                    cp.wait()
                else:
                    cp.start()
            return carry

        n_blocks = xb_ref.shape[0] // MOE_BLOCK
        lax.fori_loop(0, N_EXPERTS, lambda e, c: fill(e, c, False), 0)
        lax.fori_loop(pad_ref[2, 0], n_blocks, lambda j, c: fill_unused(j, c, False), 0)
        lax.fori_loop(0, N_EXPERTS, lambda e, c: fill(e, c, True), 0)
        lax.fori_loop(pad_ref[2, 0], n_blocks, lambda j, c: fill_unused(j, c, True), 0)

    def issue(r, carry):
        for k in range(TOP_K):
            pltpu.make_async_copy(h_ref.at[r], xb_ref.at[dest_ref[r * TOP_K + k]], sem_s).start(priority=k % 2)
        return carry

    lax.fori_loop(0, tm, issue, 0, unroll=ROW_DMA_UNROLL)
    for k in range(TOP_K):
        pltpu.make_async_copy(h_ref, xb_ref.at[pl.ds(0, tm)], sem_s).wait()


def _dispatch(h2t, dest, meta, n_slots, tm):
    t = h2t.shape[0] // SUBLANES
    h3 = h2t.reshape(t, SUBLANES, LANES)
    return pl.pallas_call(
        _dispatch_kernel,
        grid_spec=pltpu.PrefetchScalarGridSpec(
            num_scalar_prefetch=1,
            grid=(t // tm,),
            in_specs=[pl.BlockSpec((tm * TOP_K,), lambda i, p: (i,), memory_space=pltpu.SMEM),
                      pl.BlockSpec((tm, SUBLANES, LANES), lambda i, p: (i, 0, 0))],
            out_specs=pl.BlockSpec(memory_space=pl.ANY),
            scratch_shapes=[pltpu.VMEM((MOE_BLOCK // 2, SUBLANES, LANES), F32),
                            pltpu.SemaphoreType.DMA, pltpu.SemaphoreType.DMA]),
        out_shape=jax.ShapeDtypeStruct((n_slots, SUBLANES, LANES), F32),
        compiler_params=_cparams("arbitrary"),
        name="moe_dispatch",
    )(meta[:3, :N_EXPERTS], dest, h3)


def _expert_kernel(be_ref, nv_ref, nxt_ref, x_ref, wgu_hbm, bgu_ref, wd_hbm, bd_ref, y_ref,
                   wgu_f32, wd_f32, wgu_bf, wd_bf, sem, *, layer):
    j = pl.program_id(0)
    valid = j < nv_ref[0]
    first = valid & ((j == 0) | (be_ref[j] != be_ref[jnp.maximum(j - 1, 0)]))

    def weight_copies(e):
        return (pltpu.make_async_copy(wgu_hbm.at[layer, e], wgu_f32, sem.at[0]),
                pltpu.make_async_copy(wd_hbm.at[layer, e], wd_f32, sem.at[1]))

    @pl.when(j == 0)
    def _():
        for cp in weight_copies(be_ref[0]):
            cp.start()

    @pl.when(first)
    def _():
        for cp in weight_copies(be_ref[j]):
            cp.wait()
        wgu_bf[...] = wgu_f32[...].astype(BF16)
        wd_bf[...] = wd_f32[...].astype(BF16)

        @pl.when(nxt_ref[j] >= 0)
        def _():
            for cp in weight_copies(nxt_ref[j]):
                cp.start()

    @pl.when(valid)
    def _():
        x = jnp.concatenate([x_ref[pl.ds(c, MOE_BLOCK, stride=SUBLANES), :] for c in range(D_MODEL // LANES)],
                            axis=1).astype(BF16)
        gu = _bdot(x, wgu_bf[...]) + bgu_ref[0]
        x_glu = jnp.minimum(gu[:, :D_EXPERT], SWIGLU_LIMIT)
        x_lin = jnp.clip(gu[:, D_EXPERT:], -SWIGLU_LIMIT, SWIGLU_LIMIT)
        act = x_glu * _sigmoid(SWIGLU_ALPHA * x_glu) * (x_lin + 1.0)
        y = _bdot(act.astype(BF16), wd_bf[...]) + bd_ref[0]
        for c in range(D_MODEL // LANES):
            y_ref[pl.ds(c, MOE_BLOCK, stride=SUBLANES), :] = y[:, c * LANES:(c + 1) * LANES]

    @pl.when(jnp.logical_not(valid))
    def _():
        y_ref[...] = jnp.zeros_like(y_ref)


def _experts(xb, block_expert, n_valid, w_gu, b_gu, w_down, b_down, layer):
    n_slots = xb.shape[0]
    n_blocks = n_slots // MOE_BLOCK
    rows = MOE_BLOCK * SUBLANES
    x2 = xb.reshape(n_slots * SUBLANES, LANES)
    depth, ne, d, de2 = w_gu.shape
    idx = jnp.arange(n_blocks, dtype=jnp.int32)
    is_first = ((idx == 0) | (block_expert != jnp.roll(block_expert, 1))) & (idx < n_valid[0])
    first_at = lax.cummin(jnp.where(is_first, idx, n_blocks)[::-1])[::-1]
    next_first = jnp.concatenate([first_at[1:], jnp.full((1,), n_blocks, jnp.int32)])
    nxt = jnp.where(next_first < n_blocks, block_expert[jnp.minimum(next_first, n_blocks - 1)], -1)
    last = lambda j, be, nv, nx: jnp.minimum(j, nv[0] - 1)
    bmap = lambda j, be, nv, nx: (layer, be[last(j, be, nv, nx)], 0, 0)
    return pl.pallas_call(
        functools.partial(_expert_kernel, layer=layer),
        grid_spec=pltpu.PrefetchScalarGridSpec(
            num_scalar_prefetch=3,
            grid=(n_blocks,),
            in_specs=[pl.BlockSpec((rows, LANES), lambda j, be, nv, nx: (last(j, be, nv, nx), 0)),
                      pl.BlockSpec(memory_space=pl.ANY),
                      pl.BlockSpec((None, 1, 1, de2), bmap),
                      pl.BlockSpec(memory_space=pl.ANY),
                      pl.BlockSpec((None, 1, 1, d), bmap)],
            out_specs=pl.BlockSpec((rows, LANES), lambda j, be, nv, nx: (j, 0)),
            scratch_shapes=[pltpu.VMEM((d, de2), F32), pltpu.VMEM((de2 // 2, d), F32),
                            pltpu.VMEM((d, de2), BF16), pltpu.VMEM((de2 // 2, d), BF16),
                            pltpu.SemaphoreType.DMA((2,))]),
        out_shape=jax.ShapeDtypeStruct((n_slots * SUBLANES, LANES), F32),
        compiler_params=_cparams("arbitrary"),
        name="moe_experts",
    )(block_expert, n_valid, nxt.astype(jnp.int32), x2, w_gu, b_gu.reshape(depth, ne, 1, de2), w_down,
      b_down.reshape(depth, ne, 1, d))


def _combine_kernel(dest_ref, dest_next_ref, yb_ref, gate_ref, x1_ref, g2_ref, gpost_ref, o_ref, buf, sem):
    i = pl.program_id(0)
    tm = x1_ref.shape[0]
    slot = i % 2

    def gather(idx_ref, into):
        def issue(r, carry):
            for k in range(TOP_K):
                src = pl.multiple_of(idx_ref[r * TOP_K + k] * SUBLANES, SUBLANES)
                dst = pl.multiple_of((k * tm + r) * SUBLANES, SUBLANES)
                pltpu.make_async_copy(yb_ref.at[pl.ds(src, SUBLANES), :], buf.at[into, pl.ds(dst, SUBLANES), :],
                                      sem.at[into]).start(priority=k % 2)
            return carry
        lax.fori_loop(0, tm, issue, 0, unroll=ROW_DMA_UNROLL)

    @pl.when(i == 0)
    def _():
        gather(dest_ref, 0)

    @pl.when(i + 1 < pl.num_programs(0))
    def _():
        gather(dest_next_ref, 1 - slot)

    pltpu.make_async_copy(yb_ref.at[pl.ds(0, TOP_K * tm * SUBLANES), :], buf.at[slot], sem.at[slot]).wait()
    gates = gate_ref[...]
    gk = [jnp.broadcast_to(gates[:, ROUTE_GATE + k:ROUTE_GATE + k + 1], (tm, LANES)) for k in range(TOP_K)]
    b2 = buf.at[slot]
    cols = []
    for c in range(D_MODEL // LANES):
        acc = jnp.zeros((tm, LANES), F32)
        for k in range(TOP_K):
            acc = acc + gk[k] * b2[pl.ds(k * tm * SUBLANES + c, tm, stride=SUBLANES), :]
        cols.append(acc)
    f = jnp.concatenate(cols, axis=1)
    o_ref[...] = x1_ref[...] + g2_ref[0] * (_rms(f) * gpost_ref[...])


def _combine(yb, dest, gates, x1, mod3, g_post, seq, tm):
    t, d = x1.shape
    tps = seq // tm
    n = t // tm
    return pl.pallas_call(
        _combine_kernel,
        grid=(n,),
        in_specs=[pl.BlockSpec((tm * TOP_K,), lambda i: (i,), memory_space=pltpu.SMEM),
                  pl.BlockSpec((tm * TOP_K,), lambda i: (jnp.minimum(i + 1, n - 1),), memory_space=pltpu.SMEM),
                  pl.BlockSpec(memory_space=pl.ANY),
                  pl.BlockSpec((tm, LANES), lambda i: (i, 0)),
                  pl.BlockSpec((tm, d), lambda i: (i, 0)),
                  _mod_spec(5, tps),
                  pl.BlockSpec(g_post.shape, lambda i: (0, 0))],
        out_specs=pl.BlockSpec((tm, d), lambda i: (i, 0)),
        out_shape=jax.ShapeDtypeStruct((t, d), F32),
        scratch_shapes=[pltpu.VMEM((2, TOP_K * tm * SUBLANES, LANES), F32), pltpu.SemaphoreType.DMA((2,))],
        compiler_params=_cparams("arbitrary"),
        name="moe_combine",
    )(dest, dest, yb, gates, x1, mod3, g_post)


def _moe(h2t, route, cnt, x1, mod3, g_post, w_gu, b_gu, w_down, b_down, layer, seq):
    t = x1.shape[0]
    n_blocks = t * TOP_K // MOE_BLOCK + N_EXPERTS
    dest_l, blk, meta = _route(route, cnt, n_blocks, min(1024, t))
    dest = dest_l[:, :TOP_K].reshape(t * TOP_K)
    xb = _dispatch(h2t, dest, meta, n_blocks * MOE_BLOCK, MOE_BLOCK)
    yb = _experts(xb, blk[0, :n_blocks], meta[2, :1], w_gu, b_gu, w_down, b_down, layer)
    return _combine(yb, dest, route, x1, mod3, g_post, seq, MOE_BLOCK)


TOKEN_TILE = 512
OUT_TILE = 512
FOX_Q_TILE = 256
FOX_K_TILE = 256


def kernel(x, c, ada_w, ada_b, norm_pre_mix, norm_post_mix, norm_pre_ffn, norm_post_ffn, ev_w_in, fox_b_f, s5_lam_re, s5_lam_im, s5_log_dt, s5_b_re, s5_b_im, s5_c_re, s5_c_im, s5_d, s5_glu_w, s5_glu_b, ev_w_out, od_w_in, gla_w_up, gla_b_gate, od_w_out, router_w, router_b, exp_w_gu, exp_b_gu, exp_w_down, exp_b_down):
    bsz, seq, d = x.shape
    t = bsz * seq
    tm = min(TOKEN_TILE, seq)
    x2 = x.reshape(t, d)
    mod = _modulation(c, ada_w, ada_b)
    for l in range(DEPTH):
        i = l // 2
        mod3 = mod[l].reshape(bsz, 1, 6 * d)
        row = lambda a: a[l].reshape(1, -1)
        tail = (x2, mod3, row(norm_post_mix), row(norm_pre_ffn), router_w[l], router_b[l])
        if l % 2 == 0:
            u, q, k, v_t = _in_even(x2, mod3, row(norm_pre_mix), ev_w_in[i], fox_b_f[i], seq, tm)
            tables = _s5_tables(s5_lam_re[i], s5_lam_im[i], s5_log_dt[i], s5_b_re[i], s5_b_im[i],
                                s5_c_re[i], s5_c_im[i])
            ys = _s5_scan(u, bsz, seq, tables)
            yb = _fox(q, k, v_t, bsz, seq, min(FOX_Q_TILE, seq), min(FOX_K_TILE, seq))
            outs = _out_even(ys, u, yb, s5_d[i], s5_glu_w[i], s5_glu_b[i], ev_w_out[i], *tail, seq=seq, tm=min(OUT_TILE, seq))
        else:
            rq, rk, rv, sg, gq, gk, gv, sr, la = _in_odd(x2, mod3, row(norm_pre_mix), od_w_in[i],
                                                         gla_w_up[i], gla_b_gate[i], seq, tm)
            yc = _retention(rq, rk, rv, sg, bsz, seq)
            yd = _gla(gq, gk, la, gv, sr, bsz, seq)
            outs = _out_odd(yc, yd, od_w_out[i], *tail, seq=seq, tm=min(OUT_TILE, seq))
        x1, h2t, route, cnt = outs
        x2 = _moe(h2t, route, cnt, x1, mod3, row(norm_post_ffn),
                  exp_w_gu, exp_b_gu, exp_w_down, exp_b_down, l, seq)
    return x2.reshape(bsz, seq, d)
```

```python
import functools
import math

import jax
import jax.numpy as jnp
from jax import lax
from jax.experimental import pallas as pl
from jax.experimental.pallas import tpu as pltpu

F32 = jnp.float32
BF16 = jnp.bfloat16
HIGHEST = lax.Precision.HIGHEST

D_MODEL = 1024
DEPTH = 2
EPS = 1e-6
S5_CH = 512
S5_GROUP = 16
S5_GROUPS = S5_CH // S5_GROUP
S5_STATE = 64
S5_CHUNK = 8
FOX_HEADS = 8
FOX_DH = 64
FOX_W = FOX_HEADS * FOX_DH
LOG2_E = 1.4426950408889634
FOX_VROWS = FOX_DH + 16
FOX_HPS = 8
RET_HEADS = 4
RET_DK = 128
RET_DV = 128
ROPE_BASE = 10000.0
GLA_HEADS = 4
GLA_DK = 64
GLA_DV = 128
GLA_RANK = 16
GLA_TAU = 16.0
GLA_CHUNK = 64
GLA_SUB = 16
N_EXPERTS = 32
TOP_K = 4
D_EXPERT = 1024
SWIGLU_LIMIT = 7.0
SWIGLU_ALPHA = 1.702
MOE_BLOCK = 256
ROW_DMA_UNROLL = 4

LANES = 128
SUBLANES = 8
VMEM_LIMIT = 56 * 1024 * 1024


def _cparams(*sem):
    return pltpu.CompilerParams(dimension_semantics=sem, vmem_limit_bytes=VMEM_LIMIT)


def _bdot(a, b):
    return jnp.dot(a, b, preferred_element_type=F32)


def _dot_nt(a, b):
    return lax.dot_general(a, b, (((1,), (1,)), ((), ())), preferred_element_type=F32)


def _dot_tn(a, b):
    return lax.dot_general(a, b, (((0,), (0,)), ((), ())), preferred_element_type=F32)


def _split3(x):
    hi = x.astype(BF16)
    r = x - hi.astype(F32)
    mid = r.astype(BF16)
    lo = (r - mid.astype(F32)).astype(BF16)
    return hi, mid, lo


def _dot01(m01, x):
    hi, mid, lo = _split3(x)
    return _bdot(m01, hi) + _bdot(m01, mid) + _bdot(m01, lo)


def _lower_tri(n, strict=False):
    r = lax.broadcasted_iota(jnp.int32, (n, n), 0)
    c = lax.broadcasted_iota(jnp.int32, (n, n), 1)
    return ((r > c) if strict else (r >= c)).astype(BF16)


def _log_sigmoid(x):
    return jnp.minimum(x, 0.0) - jnp.log1p(jnp.exp(-jnp.abs(x)))


def _sigmoid(x):
    return 1.0 / (1.0 + jnp.exp(-x))


def _silu(x):
    return x * _sigmoid(x)


def _rms(x):
    return x * lax.rsqrt(jnp.mean(x * x, axis=-1, keepdims=True) + EPS)


def _mod_kernel(c_ref, w_ref, b_ref, o_ref):
    s = _silu(c_ref[...])
    w = w_ref[0]
    s_hi, w_hi = s.astype(BF16), w.astype(BF16)
    s_lo = (s - s_hi.astype(F32)).astype(BF16)
    w_lo = (w - w_hi.astype(F32)).astype(BF16)
    o_ref[0] = _bdot(s_hi, w_hi) + _bdot(s_lo, w_hi) + _bdot(s_hi, w_lo) + b_ref[0]


def _modulation(c, ada_w, ada_b):
    depth, d, n = ada_w.shape
    bsz = c.shape[0]
    tn = D_MODEL
    return pl.pallas_call(
        _mod_kernel,
        grid=(depth, n // tn),
        in_specs=[pl.BlockSpec((bsz, d), lambda l, j: (0, 0)),
                  pl.BlockSpec((1, d, tn), lambda l, j: (l, 0, j)),
                  pl.BlockSpec((1, 1, tn), lambda l, j: (l, 0, j))],
        out_specs=pl.BlockSpec((1, bsz, tn), lambda l, j: (l, 0, j)),
        out_shape=jax.ShapeDtypeStruct((depth, bsz, n), F32),
        compiler_params=_cparams("parallel", "parallel"),
        name="adaln_mod",
    )(c, ada_w, ada_b.reshape(depth, 1, n))


def _mod_spec(chunk, tiles_per_seq):
    return pl.BlockSpec((1, 1, D_MODEL), lambda i: (i // tiles_per_seq, 0, chunk))


def _prenorm(x, g_ref, sc_ref, sh_ref):
    return _rms(x) * (g_ref[...] * (1.0 + sc_ref[0])) + sh_ref[0]


def _in_even_kernel(x_ref, sh_ref, sc_ref, g_ref, w_ref, bf_ref,
                    u_ref, q_ref, k_ref, vt_ref, carry_ref, *, tiles_per_seq):
    i = pl.program_id(0)
    tm = x_ref.shape[0]
    h = _prenorm(x_ref[...], g_ref, sc_ref, sh_ref)
    z = _bdot(h.astype(BF16), w_ref[...])
    u_ref[...] = z[:, 0:S5_CH]
    tail = jnp.concatenate([jnp.ones((1, tm), F32), jnp.zeros((FOX_VROWS - FOX_DH - 1, tm), F32)], axis=0)
    for pr in range(FOX_HEADS // 2):
        c0 = S5_CH + 2 * FOX_W + pr * LANES
        v_pair = z[:, c0:c0 + LANES].T
        for hh in range(2):
            r0 = (2 * pr + hh) * FOX_VROWS
            vt_ref[r0:r0 + FOX_VROWS, :] = jnp.concatenate(
                [v_pair[hh * FOX_DH:(hh + 1) * FOX_DH], tail], axis=0).astype(BF16)
    ls = _log_sigmoid(z[:, S5_CH + 3 * FOX_W:] + bf_ref[...])

    @pl.when(i % tiles_per_seq == 0)
    def _():
        carry_ref[...] = jnp.zeros_like(carry_ref)

    cum = _dot01(_lower_tri(tm), ls) + carry_ref[...]
    carry_ref[...] = cum[tm - 1:tm, :]

    lane = lax.broadcasted_iota(jnp.int32, (1, LANES), 1)
    feat = lane < FOX_DH
    ones = jnp.where(lane < FOX_DH + 3, 1.0, 0.0)
    for hd in range(FOX_HEADS):
        blk = (hd * FOX_DH) // LANES * LANES
        qs = z[:, S5_CH + blk:S5_CH + blk + LANES] * (FOX_DH ** -0.5 * LOG2_E)
        ks = z[:, S5_CH + FOX_W + blk:S5_CH + FOX_W + blk + LANES]
        if (hd * FOX_DH) % LANES:
            qs = pltpu.roll(qs, LANES - FOX_DH, 1)
            ks = pltpu.roll(ks, LANES - FOX_DH, 1)
        nf = jnp.broadcast_to(-LOG2_E * cum[:, hd:hd + 1], (tm, LANES))
        hi = nf.astype(BF16).astype(F32)
        mid = (nf - hi).astype(BF16).astype(F32)
        lo = nf - hi - mid
        bias = jnp.where(lane == FOX_DH, hi, jnp.where(lane == FOX_DH + 1, mid,
                                                       jnp.where(lane == FOX_DH + 2, lo, 0.0)))
        q_ref[:, hd * LANES:(hd + 1) * LANES] = jnp.where(feat, qs, ones).astype(BF16)
        k_ref[:, hd * LANES:(hd + 1) * LANES] = jnp.where(feat, ks, bias).astype(BF16)


def _in_even(x2, mod3, gain, w_in, b_f, seq, tm):
    t, d = x2.shape
    tiles_per_seq = seq // tm
    nw = S5_CH + 3 * FOX_W
    w = jnp.concatenate([w_in[:, :nw], jnp.pad(w_in[:, nw:], ((0, 0), (0, LANES - FOX_HEADS)))],
                        axis=1).astype(BF16)
    bf = jnp.pad(b_f, (0, LANES - FOX_HEADS)).reshape(1, LANES)
    tok = lambda n: pl.BlockSpec((tm, n), lambda i: (i, 0))
    full = lambda a: pl.BlockSpec(a.shape, lambda i: (0,) * a.ndim)
    return pl.pallas_call(
        functools.partial(_in_even_kernel, tiles_per_seq=tiles_per_seq),
        grid=(t // tm,),
        in_specs=[tok(d), _mod_spec(0, tiles_per_seq), _mod_spec(1, tiles_per_seq),
                  full(gain), full(w), full(bf)],
        out_specs=[tok(S5_CH), tok(FOX_HEADS * LANES), tok(FOX_HEADS * LANES),
                   pl.BlockSpec((FOX_HEADS * FOX_VROWS, tm), lambda i: (i // tiles_per_seq, i % tiles_per_seq))],
        out_shape=[jax.ShapeDtypeStruct((t, S5_CH), F32),
                   jax.ShapeDtypeStruct((t, FOX_HEADS * LANES), BF16),
                   jax.ShapeDtypeStruct((t, FOX_HEADS * LANES), BF16),
                   jax.ShapeDtypeStruct((t // seq * FOX_HEADS * FOX_VROWS, seq), BF16)],
        scratch_shapes=[pltpu.VMEM((1, LANES), F32)],
        compiler_params=_cparams("arbitrary"),
        name="in_proj_even",
    )(x2, mod3, mod3, gain, w, bf)


S5_TILE_GROUPS = LANES // S5_GROUP
S5_SEQ_PARTS = 4


def _s5_kernel(u_ref, wt_ref, ws_ref, wc_ref, a_ref, y_ref, x_scr, e_scr, hp_scr, h_scr, *, nb, ncl):
    ell = S5_CHUNK
    sw = S5_TILE_GROUPS * 2 * S5_STATE

    @pl.when(pl.program_id(1) == 0)
    def _():
        h_scr[...] = jnp.zeros_like(h_scr)

    blk = nb * SUBLANES
    for hi in range(ncl // SUBLANES):
        for b in range(nb):
            for t in range(ell):
                x_scr[hi * blk + b * SUBLANES:hi * blk + (b + 1) * SUBLANES, t * LANES:(t + 1) * LANES] = (
                    u_ref[b, pl.ds(hi * SUBLANES * ell + t, SUBLANES, stride=ell), :])
    x = x_scr[...].astype(BF16)
    e = _bdot(x, ws_ref[0])
    tg = S5_TILE_GROUPS
    for j in range(tg):
        e_scr[j] = e[:, j * LANES:(j + 1) * LANES]
        e_scr[tg + j] = pltpu.roll(e[:, j * LANES:(j + 1) * LANES], S5_STATE, 1)
    a1 = jnp.broadcast_to(a_ref[0, 0:1, :], (nb, sw))
    a2 = jnp.broadcast_to(a_ref[0, 1:2, :], (nb, sw))
    a2s = jnp.broadcast_to(a_ref[0, 2:3, :], (nb, sw))

    def body(c, carry):
        h, hs = carry
        rows_c = pl.ds((c // SUBLANES) * blk + c % SUBLANES, nb, stride=SUBLANES)
        for j in range(tg):
            hp_scr[j, rows_c, :] = h[:, j * LANES:(j + 1) * LANES]
        e1 = jnp.concatenate([e_scr[j, rows_c, :] for j in range(tg)], axis=1)
        e2 = jnp.concatenate([e_scr[tg + j, rows_c, :] for j in range(tg)], axis=1)
        return a1 * h + a2 * hs + e1, a1 * hs + a2s * h + e2

    h, hs = lax.fori_loop(0, ncl, body, (h_scr[0], h_scr[1]))
    h_scr[0] = h
    h_scr[1] = hs
    hp = jnp.concatenate([hp_scr[j] for j in range(tg)], axis=1).astype(BF16)
    y = _bdot(x, wt_ref[0]) + _bdot(hp, wc_ref[0])
    for hi in range(ncl // SUBLANES):
        for b in range(nb):
            for t in range(ell):
                y_ref[b, pl.ds(hi * SUBLANES * ell + t, SUBLANES, stride=ell), :] = (
                    y[hi * blk + b * SUBLANES:hi * blk + (b + 1) * SUBLANES, t * LANES:(t + 1) * LANES])


def _s5_tables(lam_re, lam_im, log_dt, b_re, b_im, c_re, c_im):
    ell, p, g = S5_CHUNK, S5_STATE, S5_GROUPS
    lr, li = lam_re.astype(F32), lam_im.astype(F32)
    dt = jnp.exp(log_dt.astype(F32))[:, None]
    mag = jnp.exp(lr * dt)
    a_re, a_im = mag * jnp.cos(li * dt), mag * jnp.sin(li * dt)
    den = lr * lr + li * li
    n_re, n_im = a_re - 1.0, a_im
    z_re = (n_re * lr + n_im * li) / den
    z_im = (n_im * lr - n_re * li) / den
    br, bi = b_re.astype(F32), b_im.astype(F32)
    bb_re = z_re[..., None] * br - z_im[..., None] * bi
    bb_im = z_re[..., None] * bi + z_im[..., None] * br
    j = jnp.arange(ell + 1, dtype=F32)[:, None, None]
    pmag = jnp.exp(j * (lr * dt)[None])
    pr, pi = pmag * jnp.cos(j * (li * dt)[None]), pmag * jnp.sin(j * (li * dt)[None])
    w_re = pr[..., None] * bb_re[None] - pi[..., None] * bb_im[None]
    w_im = pr[..., None] * bb_im[None] + pi[..., None] * bb_re[None]
    cr, ci = c_re.astype(F32), c_im.astype(F32)
    kern = (jnp.einsum('gcp,jgpd->jgcd', cr, w_re[:ell], precision=HIGHEST)
            - jnp.einsum('gcp,jgpd->jgcd', ci, w_im[:ell], precision=HIGHEST))
    s_idx = jnp.arange(ell)[:, None]
    t_idx = jnp.arange(ell)[None, :]
    lag = jnp.clip(t_idx - s_idx, 0, ell - 1)
    toep = kern[lag]
    toep = jnp.where((t_idx >= s_idx)[:, :, None, None, None], toep, 0.0)
    toep = toep.transpose(2, 0, 4, 1, 3)
    rev = jnp.arange(ell - 1, -1, -1)
    st_re = w_re[rev].transpose(1, 0, 3, 2)
    st_im = w_im[rev].transpose(1, 0, 3, 2)
    wst = jnp.concatenate([st_re, st_im], -1)
    p1r, p1i = pr[1:], pi[1:]
    c_hr = cr[None] * p1r[:, :, None, :] - ci[None] * p1i[:, :, None, :]
    c_hi = -cr[None] * p1i[:, :, None, :] - ci[None] * p1r[:, :, None, :]
    cst = jnp.concatenate([c_hr, c_hi], axis=-1).transpose(1, 3, 0, 2)
    al_r, al_i = pr[ell], pi[ell]
    a_rows = jnp.stack([jnp.concatenate([al_r, al_r], -1), jnp.concatenate([-al_i, al_i], -1),
                        jnp.concatenate([al_i, -al_i], -1)], axis=1)

    tg = S5_TILE_GROUPS
    nt = g // tg
    tile = lambda a: a.reshape((nt, tg) + a.shape[1:])
    lane_grp = jnp.arange(LANES) // S5_GROUP
    step_lane_grp = jnp.tile(lane_grp, ell)
    spread_out = (jnp.eye(ell, dtype=BF16)[:, None, :, None, None]
                  * jnp.eye(S5_GROUP, dtype=BF16)[None, :, None, None, :]
                  * jnp.ones((1, 1, 1, tg, 1), BF16)).reshape(ell * S5_GROUP, ell * LANES)
    toep_t = tile(toep).transpose(0, 2, 1, 3, 4, 5).reshape(nt, ell, LANES, ell * S5_GROUP).astype(BF16)
    wt = jnp.einsum('ksrm,mn->ksrn', toep_t, spread_out)
    wt = jnp.where((lane_grp[:, None] == step_lane_grp[None, :])[None, None], wt, 0)
    wt = wt.reshape(nt, ell * LANES, ell * LANES)
    spread_state = jnp.tile(jnp.eye(2 * p, dtype=BF16), (1, tg))
    state_grp = jnp.arange(tg * 2 * p) // (2 * p)
    wst_t = tile(wst).transpose(0, 2, 1, 3, 4).reshape(nt, ell, LANES, 2 * p).astype(BF16)
    ws = jnp.einsum('ksrp,pn->ksrn', wst_t, spread_state)
    ws = jnp.where((lane_grp[:, None] == state_grp[None, :])[None, None], ws, 0)
    ws = ws.reshape(nt, ell * LANES, tg * 2 * p)
    cst_t = tile(cst).reshape(nt, tg * 2 * p, ell * S5_GROUP).astype(BF16)
    wc = jnp.einsum('krm,mn->krn', cst_t, spread_out)
    wc = jnp.where((state_grp[:, None] == step_lane_grp[None, :])[None], wc, 0)
    a_t = tile(a_rows).transpose(0, 2, 1, 3).reshape(nt, 3, tg * 2 * p)
    a_t = jnp.pad(a_t, ((0, 0), (0, SUBLANES - 3), (0, 0)))
    return wt, ws, wc, a_t


def _s5_scan(u, bsz, seq, tables):
    wt, ws, wc, a_t = tables
    ell = S5_CHUNK
    nt = S5_GROUPS // S5_TILE_GROUPS
    parts = S5_SEQ_PARTS if seq % (S5_SEQ_PARTS * ell * 2 * SUBLANES) == 0 else 1
    ncl = seq // parts // ell
    rows = bsz * ncl
    sw = S5_TILE_GROUPS * 2 * S5_STATE
    u3 = u.reshape(bsz, seq, S5_CH)
    io = pl.BlockSpec((bsz, seq // parts, LANES), lambda k, s: (0, s, k))
    per_tile = lambda a: pl.BlockSpec((1,) + a.shape[1:], lambda k, s: (k, 0, 0))
    y = pl.pallas_call(
        functools.partial(_s5_kernel, nb=bsz, ncl=ncl),
        grid=(nt, parts),
        in_specs=[io, per_tile(wt), per_tile(ws), per_tile(wc), per_tile(a_t)],
        out_specs=io,
        out_shape=jax.ShapeDtypeStruct((bsz, seq, S5_CH), F32),
        scratch_shapes=[pltpu.VMEM((rows, ell * LANES), F32),
                        pltpu.VMEM((2 * S5_TILE_GROUPS, rows, LANES), F32),
                        pltpu.VMEM((S5_TILE_GROUPS, rows, LANES), F32),
                        pltpu.VMEM((2, bsz, sw), F32)],
        compiler_params=_cparams("parallel", "arbitrary"),
        name="s5_scan",
    )(u3, wt, ws, wc, a_t)
    return y.reshape(bsz * seq, S5_CH)


def _fox_kernel(q_ref, k_ref, vt_ref, o_ref, *, tq, tk):
    i = pl.program_id(2)
    nh = FOX_HPS
    q = [q_ref[:, h * LANES:(h + 1) * LANES] for h in range(nh)]
    key = lax.broadcasted_iota(jnp.int32, (tk, tq), 0)
    qry = lax.broadcasted_iota(jnp.int32, (tk, tq), 1)
    per_q = tq // tk

    def block(j0, carry, mask, q0=0):
        kj = k_ref[pl.ds(j0, tk), :]
        vtj = vt_ref[:, pl.ds(j0, tk)]
        ss = [_dot_nt(kj[:, h * LANES:(h + 1) * LANES], q[h][q0:]) for h in range(nh)]
        stats = []
        for h in range(nh):
            m = carry[h][0][:, q0:]
            s = ss[h] if mask is None else jnp.where(mask[:, q0:], ss[h], -jnp.inf)
            m_new = jnp.maximum(m, jnp.max(s, axis=0, keepdims=True))
            p = jnp.exp2(s - m_new)
            stats.append((m_new, jnp.exp2(m - m_new), p.astype(BF16)))
        out = []
        for h in range(nh):
            m_new, alpha, p = stats[h]
            acc = alpha * carry[h][1][:, q0:] + _bdot(vtj[h * FOX_VROWS:(h + 1) * FOX_VROWS, :], p)
            if q0:
                m_new = jnp.concatenate([carry[h][0][:, :q0], m_new], axis=1)
                acc = jnp.concatenate([carry[h][1][:, :q0], acc], axis=1)
            out.append((m_new, acc))
        return tuple(out)

    init = tuple((jnp.full((1, tq), -jnp.inf, F32), jnp.zeros((FOX_VROWS, tq), F32)) for _ in range(nh))
    carry = lax.fori_loop(0, i * per_q, lambda j, c: block(pl.multiple_of(j * tk, tk), c, None), init)
    for d in range(per_q):
        carry = block(pl.multiple_of(i * tq + d * tk, tk), carry, key + d * tk <= qry, d * tk)
    for g in range(nh // 2):
        o_t = jnp.concatenate([carry[h][1][:FOX_DH] / carry[h][1][FOX_DH:FOX_DH + 1]
                               for h in (2 * g, 2 * g + 1)], axis=0)
        o_ref[:, g * LANES:(g + 1) * LANES] = o_t.T.astype(o_ref.dtype)


def _fox(q_aug, k_aug, v_t, bsz, seq, tq, tk):
    t = q_aug.shape[0]
    nh = FOX_HPS
    groups = FOX_HEADS // nh
    nq = seq // tq
    return pl.pallas_call(
        functools.partial(_fox_kernel, tq=tq, tk=tk),
        grid=(bsz, groups, nq),
        in_specs=[pl.BlockSpec((tq, nh * LANES), lambda b, p, i: (b * nq + i, p)),
                  pl.BlockSpec((seq, nh * LANES), lambda b, p, i: (b, p)),
                  pl.BlockSpec((nh * FOX_VROWS, seq), lambda b, p, i: (b * groups + p, 0))],
        out_specs=pl.BlockSpec((tq, nh * FOX_DH), lambda b, p, i: (b * nq + i, p)),
        out_shape=jax.ShapeDtypeStruct((t, FOX_W), BF16),
        compiler_params=_cparams("parallel", "parallel", "arbitrary"),
        name="fox_attention",
    )(q_aug, k_aug, v_t)


_ODD_COLS = (("rq", 512), ("rk", 512), ("rv", 512), ("rg", 512), ("gq", 256), ("gk", 256),
             ("gv", 512), ("gr", 512), ("glr", LANES))


def _odd_offsets():
    off, out = 0, {}
    for name, w in _ODD_COLS:
        out[name] = (off, off + w)
        off += w
    return out, off


def _in_odd_kernel(x_ref, sh_ref, sc_ref, g_ref, w_ref, cos_ref, sin_ref, wup_ref, bg_ref,
                   rq_ref, rk_ref, rv_ref, sg_ref, gq_ref, gk_ref, gv_ref, sr_ref, la_ref):
    h = _prenorm(x_ref[...], g_ref, sc_ref, sh_ref)
    z = _bdot(h.astype(BF16), w_ref[...])
    off, _ = _odd_offsets()
    col = lambda n: z[:, off[n][0]:off[n][1]]
    cos, sin = cos_ref[...], sin_ref[...]

    def rope(t, scale):
        heads = []
        for hd in range(RET_HEADS):
            th = t[:, hd * RET_DK:(hd + 1) * RET_DK]
            heads.append((th * cos + pltpu.roll(th, RET_DK // 2, 1) * sin) * scale)
        return jnp.concatenate(heads, axis=1).astype(BF16)

    rq_ref[...] = rope(col("rq"), 1.0)
    rk_ref[...] = rope(col("rk"), RET_DK ** -0.5)
    rv_ref[...] = col("rv").astype(BF16)
    sg_ref[...] = _silu(col("rg"))
    gq_ref[...] = col("gq") * (GLA_DK ** -0.5)
    gk_ref[...] = col("gk")
    gv_ref[...] = col("gv").astype(BF16)
    sr_ref[...] = _silu(col("gr"))
    glr = col("glr")
    g_hi = glr.astype(BF16)
    g_lo = (glr - g_hi.astype(F32)).astype(BF16)
    nqk = GLA_HEADS * GLA_DK
    gw = _bdot(g_hi, wup_ref[...])
    gate = gw[:, :nqk] + gw[:, nqk:] + _bdot(g_lo, wup_ref[:, :nqk]) + bg_ref[...]
    la_ref[...] = _log_sigmoid(gate) * (1.0 / GLA_TAU)


def _in_odd(x2, mod3, gain, w_in, w_up, b_gate, seq, tm):
    t, d = x2.shape
    tps = seq // tm
    ref_w = (512, 512, 512, 512, 256, 256, 512, GLA_RANK, 512)
    starts = [0]
    for wd in ref_w:
        starts.append(starts[-1] + wd)
    seg = lambda j: w_in[:, starts[j]:starts[j + 1]]
    w = jnp.concatenate([seg(0), seg(1), seg(2), seg(3), seg(4), seg(5), seg(6), seg(8),
                         jnp.pad(seg(7), ((0, 0), (0, LANES - GLA_RANK)))], axis=1).astype(BF16)
    wup32 = jnp.pad(w_up.astype(F32), ((0, LANES - GLA_RANK), (0, 0)))
    wup_hi = wup32.astype(BF16)
    wup = jnp.concatenate([wup_hi, (wup32 - wup_hi.astype(F32)).astype(BF16)], axis=1)
    bg = b_gate.reshape(1, -1).astype(F32)
    half = RET_DK // 2
    inv = ROPE_BASE ** (-jnp.arange(half, dtype=F32) / half)
    ang = jnp.arange(seq, dtype=F32)[:, None] * inv[None, :]
    cos = jnp.concatenate([jnp.cos(ang), jnp.cos(ang)], axis=1)
    sin = jnp.concatenate([-jnp.sin(ang), jnp.sin(ang)], axis=1)
    tok = lambda n: pl.BlockSpec((tm, n), lambda i: (i, 0))
    full = lambda a: pl.BlockSpec(a.shape, lambda i: (0,) * a.ndim)
    pos = pl.BlockSpec((tm, RET_DK), lambda i: (i % tps, 0))
    widths = (512, 512, 512, 512, 256, 256, 512, 512, 256)
    dtypes = (BF16, BF16, BF16, F32, F32, F32, BF16, F32, F32)
    return pl.pallas_call(
        _in_odd_kernel,
        grid=(t // tm,),
        in_specs=[tok(d), _mod_spec(0, tps), _mod_spec(1, tps), full(gain), full(w), pos, pos,
                  full(wup), full(bg)],
        out_specs=[tok(n) for n in widths],
        out_shape=[jax.ShapeDtypeStruct((t, n), dt) for n, dt in zip(widths, dtypes)],
        compiler_params=_cparams("parallel"),
        name="in_proj_odd",
    )(x2, mod3, mod3, gain, w, cos, sin, wup, bg)


RET_CHUNK = 256


def _ret_kernel(q_ref, k_ref, v_ref, sg_ref, dm_ref, xi_ref, zeta_ref, gl_ref, y_ref, st_ref):
    @pl.when(pl.program_id(1) == 0)
    def _():
        st_ref[...] = jnp.zeros_like(st_ref)

    nb = q_ref.shape[0]
    chains = [(bl, h) for bl in range(nb) for h in range(RET_HEADS)]
    col = lambda h: slice(h * RET_DK, (h + 1) * RET_DK)
    q = [q_ref[bl, :, col(h)] for bl, h in chains]
    k = [k_ref[bl, :, col(h)] for bl, h in chains]
    v = [v_ref[bl, :, col(h)] for bl, h in chains]
    st = [st_ref[c] for c in range(len(chains))]
    s = [_dot_nt(q[c], k[c]) for c in range(len(chains))]
    inter = [_bdot((q[c].astype(F32) * xi_ref[h]).astype(BF16), st[c].astype(BF16))
             for c, (bl, h) in enumerate(chains)]
    upd = [_dot_tn((k[c].astype(F32) * zeta_ref[h]).astype(BF16), v[c]) for c, (bl, h) in enumerate(chains)]
    for c, (bl, h) in enumerate(chains):
        o = _bdot((s[c] * dm_ref[h]).astype(BF16), v[c]) + inter[c]
        st_ref[c] = gl_ref[h, 0:1, :] * st[c] + upd[c]
        y_ref[bl, :, col(h)] = (sg_ref[bl, :, col(h)] * _rms(o)).astype(y_ref.dtype)


RET_BATCHES = 2


def _retention(rq, rk, rv, sg, bsz, seq):
    t = rq.shape[0]
    ell = min(RET_CHUNK, seq)
    nc = seq // ell
    log_g = jnp.log(1.0 - jnp.exp2(-5.0 - jnp.arange(RET_HEADS, dtype=F32)))
    idx = jnp.arange(ell, dtype=F32)
    rel = idx[:, None] - idx[None, :]
    dmat = jnp.where(rel >= 0, jnp.exp(log_g[:, None, None] * jnp.maximum(rel, 0.0)), 0.0)
    lanes = lambda a: jnp.broadcast_to(a[..., None], a.shape + (RET_DK,))
    xi = lanes(jnp.exp(log_g[:, None] * (idx + 1.0)))
    zeta = lanes(jnp.exp(log_g[:, None] * (ell - 1.0 - idx)))
    gl = jnp.broadcast_to(jnp.exp(log_g * ell)[:, None, None], (RET_HEADS, SUBLANES, RET_DV))
    nb = min(RET_BATCHES, bsz)
    width = RET_HEADS * RET_DK
    r3 = lambda a: a.reshape(bsz, seq, width)
    blk = pl.BlockSpec((nb, ell, width), lambda b, c: (b, c, 0))
    full = lambda a: pl.BlockSpec(a.shape, lambda b, c: (0, 0, 0))
    y = pl.pallas_call(
        _ret_kernel,
        grid=(bsz // nb, nc),
        in_specs=[blk, blk, blk, blk, full(dmat), full(xi), full(zeta), full(gl)],
        out_specs=blk,
        out_shape=jax.ShapeDtypeStruct((bsz, seq, width), BF16),
        scratch_shapes=[pltpu.VMEM((nb * RET_HEADS, RET_DK, RET_DV), F32)],
        compiler_params=_cparams("parallel", "arbitrary"),
        name="retention",
    )(r3(rq), r3(rk), r3(rv), r3(sg), dmat, xi, zeta, gl)
    return y.reshape(t, width)


def _gla_kernel(q_ref, k_ref, la_ref, v_ref, sg_ref, y_ref, st_ref, b_scr, v_scr, p_scr, r_scr):
    @pl.when(pl.program_id(1) == 0)
    def _():
        st_ref[...] = jnp.zeros_like(st_ref)

    ell, sub = GLA_CHUNK, GLA_SUB
    n_sub = ell // sub
    nb = q_ref.shape[0]
    pairs = GLA_HEADS // 2
    streams = [(bl, p) for bl in range(nb) for p in range(pairs)]
    lane = lax.broadcasted_iota(jnp.int32, (1, LANES), 1)
    first = lane < GLA_DK
    head = (first, jnp.logical_not(first))
    pick = lambda h, a: jnp.where(head[h], a, 0.0).astype(BF16)
    tri = _lower_tri(ell)
    tau = lax.broadcasted_iota(jnp.int32, (sub, LANES), 0)
    row_of, per_sub = [], 0
    for s_ in range(sub):
        row_of.append(per_sub)
        per_sub += sub - (s_ // SUBLANES) * SUBLANES
    rsub = lax.broadcasted_iota(jnp.int32, (LANES, 2 * LANES), 0)
    csub = lax.broadcasted_iota(jnp.int32, (LANES, 2 * LANES), 1)
    ind = ((rsub < GLA_DK) == (csub < LANES)).astype(BF16)

    val = {}
    for sid, (bl, p) in enumerate(streams):
        qk = slice(p * LANES, (p + 1) * LANES)
        q, k = q_ref[bl, :, qk], k_ref[bl, :, qk]
        b = _dot01(tri, la_ref[bl, :, qk])
        b_scr[sid] = b
        v_bf = v_ref[bl, :, p * 2 * GLA_DV:(p + 1) * 2 * GLA_DV]
        v_scr[sid] = v_bf.astype(F32)
        st = st_ref[sid]
        val[sid] = dict(q=q, k=k, b=b, st=st, vh=[v_bf[:, h * GLA_DV:(h + 1) * GLA_DV] for h in range(2)])

    for sid in val:
        d = val[sid]
        qe = d["q"] * jnp.exp(d["b"])
        st_bf = d["st"].astype(BF16)
        d["o"] = [_dot_nt(pick(h, qe), st_bf) for h in range(2)]

    row = lax.broadcasted_iota(jnp.int32, (ell, LANES), 0)
    for sid in val:
        d = val[sid]
        q, k, b = d["q"], d["k"], d["b"]
        qa, ka = [], []
        for i in range(1, n_sub):
            lo = i * sub
            ref_row = b[lo - 1:lo, :]
            in_i = (row >= lo) & (row < lo + sub)
            qa.append(jnp.where(in_i, q * jnp.exp(jnp.minimum(b - ref_row, 0.0)), 0.0))
            ka.append(jnp.where(row < lo, k * jnp.exp(jnp.minimum(ref_row - b, 0.0)), 0.0))
        k_cat = jnp.concatenate(ka, axis=1).astype(BF16)
        d["a_off"] = [_dot_nt(jnp.concatenate([pick(h, x) for x in qa], axis=1), k_cat) for h in range(2)]
    for sid in val:
        d = val[sid]
        d["off"] = [_bdot(d["a_off"][h].astype(BF16), d["vh"][h]) for h in range(2)]

    for sid, (bl, p) in enumerate(streams):
        d = val[sid]
        q, b = d["q"], d["b"]
        for i in range(n_sub):
            lo = i * sub
            qi, bi = q[lo:lo + sub], b[lo:lo + sub]
            for s in range(sub):
                k_row = k_ref[bl, pl.ds(lo + s, 1), p * LANES:(p + 1) * LANES]
                b_row = b_scr[sid, pl.ds(lo + s, 1), :]
                r0 = (s // SUBLANES) * SUBLANES
                w = jnp.exp(jnp.minimum(bi[r0:] - b_row, 0.0))
                tile_s = jnp.where(tau[r0:] >= s, qi[r0:] * k_row * w, 0.0)
                p_scr[sid, pl.ds(i * per_sub + row_of[s], sub - r0), :] = tile_s
    for sid in val:
        r_scr[sid] = _bdot(p_scr[sid].astype(BF16), ind)
    for sid in val:
        diag = [[], []]
        for i in range(n_sub):
            lo = i * sub
            for h in range(2):
                acc = [jnp.zeros((SUBLANES, GLA_DV), F32) for _ in range(sub // SUBLANES)]
                for s in range(sub):
                    v_row = v_scr[sid, pl.ds(lo + s, 1), h * GLA_DV:(h + 1) * GLA_DV]
                    for part in range(s // SUBLANES, sub // SUBLANES):
                        rows = pl.ds(i * per_sub + row_of[s] + (part - s // SUBLANES) * SUBLANES, SUBLANES)
                        acc[part] = acc[part] + r_scr[sid, rows, h * LANES:(h + 1) * LANES] * v_row
                diag[h].append(jnp.concatenate(acc, axis=0))
        val[sid]["diag"] = diag

    for sid, (bl, p) in enumerate(streams):
        d = val[sid]
        b_last = d["b"][ell - 1:ell, :]
        kh = (d["k"] * jnp.exp(b_last - d["b"])).astype(BF16)
        upd = [_dot_tn(d["vh"][h], kh) for h in range(2)]
        st_ref[sid] = d["st"] * jnp.exp(b_last) + jnp.where(first, upd[0], upd[1])
        for h in range(2):
            oh = d["o"][h] + d["off"][h] + jnp.concatenate(d["diag"][h], axis=0)
            cols = slice((2 * p + h) * GLA_DV, (2 * p + h + 1) * GLA_DV)
            y_ref[bl, :, cols] = (sg_ref[bl, :, cols] * _rms(oh)).astype(y_ref.dtype)


GLA_BATCHES = 4


def _gla(gq, gk, la, gv, sr, bsz, seq):
    t = gq.shape[0]
    ell = GLA_CHUNK
    nc = seq // ell
    nb = min(GLA_BATCHES, bsz)
    ns = nb * (GLA_HEADS // 2)
    prod_rows = (ell // GLA_SUB) * sum(GLA_SUB - (s // SUBLANES) * SUBLANES for s in range(GLA_SUB))
    r3 = lambda a: a.reshape(bsz, seq, a.shape[1])
    spec = lambda w: pl.BlockSpec((nb, ell, w), lambda b, c: (b, c, 0))
    wq, wv = GLA_HEADS * GLA_DK, GLA_HEADS * GLA_DV
    y = pl.pallas_call(
        _gla_kernel,
        grid=(bsz // nb, nc),
        in_specs=[spec(wq), spec(wq), spec(wq), spec(wv), spec(wv)],
        out_specs=spec(wv),
        out_shape=jax.ShapeDtypeStruct((bsz, seq, wv), BF16),
        scratch_shapes=[pltpu.VMEM((ns, GLA_DV, LANES), F32),
                        pltpu.VMEM((ns, ell, LANES), F32),
                        pltpu.VMEM((ns, ell, 2 * GLA_DV), F32),
                        pltpu.VMEM((ns, prod_rows, LANES), F32),
                        pltpu.VMEM((ns, prod_rows, 2 * LANES), F32)],
        compiler_params=_cparams("parallel", "arbitrary"),
        name="gla",
    )(r3(gq), r3(gk), r3(la), r3(gv), r3(sr))
    return y.reshape(t, wv)


ROUTE_IDX, ROUTE_GATE, ROUTE_RANK = 0, SUBLANES, 2 * SUBLANES


def _post_tail(m, x_ref, g1_ref, gpost_ref, sh2_ref, sc2_ref, gpre_ref, rw_ref, rb_ref,
               x1_ref, h2_ref, route_ref, cnt_ref, carry_ref):
    i = pl.program_id(0)
    tm = m.shape[0]
    x1 = x_ref[...] + g1_ref[0] * (_rms(m) * gpost_ref[...])
    x1_ref[...] = x1
    h2 = _rms(x1) * (gpre_ref[...] * (1.0 + sc2_ref[0])) + sh2_ref[0]
    for c in range(D_MODEL // LANES):
        h2_ref[pl.ds(c, tm, stride=SUBLANES), :] = h2[:, c * LANES:(c + 1) * LANES]

    ne = N_EXPERTS
    h_hi = h2.astype(BF16)
    h_lo = (h2 - h_hi.astype(F32)).astype(BF16)
    w_both = rw_ref[...]
    hw = _dot_nt(w_both, h_hi)
    bias = jnp.concatenate([rb_ref[...]] * (tm // LANES), axis=1)
    work = hw[:ne] + hw[ne:] + _dot_nt(w_both[:ne], h_lo) + bias
    esub = lax.broadcasted_iota(jnp.int32, (ne, tm), 0).astype(F32)
    onehot = jnp.zeros((ne, tm), F32)
    hits, vals, idxs = [], [], []
    for k in range(TOP_K):
        mx = jnp.max(work, axis=0, keepdims=True)
        idx = jnp.min(jnp.where(work == mx, esub, float(ne)), axis=0, keepdims=True)
        hit = esub == idx
        hits.append(hit)
        vals.append(mx)
        idxs.append(idx)
        onehot = onehot + hit.astype(F32)
        work = jnp.where(hit, -jnp.inf, work)
    es = [jnp.exp(v - vals[0]) for v in vals]
    inv = 1.0 / (es[0] + es[1] + es[2] + es[3])
    gates = [e * inv for e in es]

    @pl.when(i == 0)
    def _():
        carry_ref[...] = jnp.zeros_like(carry_ref)

    r = lax.broadcasted_iota(jnp.int32, (tm, tm), 0)
    c = lax.broadcasted_iota(jnp.int32, (tm, tm), 1)
    carry = carry_ref[...]
    before = _bdot(onehot.astype(BF16), (r < c).astype(BF16)) + jnp.concatenate([carry[:ne]] * (tm // LANES), axis=1)
    ranks = [jnp.sum(jnp.where(hits[k], before, 0.0), axis=0, keepdims=True) for k in range(TOP_K)]
    carry = carry + jnp.concatenate(
        [jnp.broadcast_to(jnp.sum(onehot, axis=1, keepdims=True), (ne, LANES)),
         jnp.zeros((LANES - ne, LANES), F32)], axis=0)
    carry_ref[...] = carry
    cnt_ref[...] = carry.T[:SUBLANES]

    sub = lax.broadcasted_iota(jnp.int32, (SUBLANES, tm), 0)

    def rows(vs):
        out = jnp.zeros((SUBLANES, tm), F32)
        for k, v in enumerate(vs):
            out = jnp.where(sub == k, jnp.broadcast_to(v, (SUBLANES, tm)), out)
        return out

    rec = jnp.concatenate([rows(idxs), rows(gates), rows(ranks), jnp.zeros((LANES - 3 * SUBLANES, tm), F32)], axis=0)
    route_ref[...] = rec.T


def _out_even_kernel(ys_ref, u_ref, yb_ref, d_ref, gw_ref, gb_ref, wa_ref, wb_ref, *rest):
    y = ys_ref[...] + d_ref[...] * u_ref[...]
    g = jax.nn.gelu(y)
    ya = g * _sigmoid(_bdot(g.astype(BF16), gw_ref[...]) + gb_ref[...])
    m = _bdot(ya.astype(BF16), wa_ref[...]) + _bdot(yb_ref[...], wb_ref[...])
    _post_tail(m, *rest)


def _out_odd_kernel(yc_ref, yd_ref, wa_ref, wb_ref, *rest):
    m = _bdot(yc_ref[...], wa_ref[...]) + _bdot(yd_ref[...], wb_ref[...])
    _post_tail(m, *rest)


def _mixer_out(body, mix_args, mix_specs, x2, mod3, g_post, g_pre, router_w, router_b, seq, tm):
    t, d = x2.shape
    tps = seq // tm
    rw32 = router_w.astype(F32).T
    rw_hi = rw32.astype(BF16)
    rw = jnp.concatenate([rw_hi, (rw32 - rw_hi.astype(F32)).astype(BF16)], axis=0)
    rb = jnp.broadcast_to(router_b.astype(F32)[:, None], (N_EXPERTS, LANES))
    tok = lambda n: pl.BlockSpec((tm, n), lambda i: (i, 0))
    full = lambda a: pl.BlockSpec(a.shape, lambda i: (0,) * a.ndim)
    tail_args = [x2, mod3, g_post, mod3, mod3, g_pre, rw, rb]
    tail_specs = [tok(d), _mod_spec(2, tps), full(g_post), _mod_spec(3, tps), _mod_spec(4, tps),
                  full(g_pre), full(rw), full(rb)]
    return pl.pallas_call(
        body,
        grid=(t // tm,),
        in_specs=mix_specs + tail_specs,
        out_specs=[tok(d), pl.BlockSpec((tm * SUBLANES, LANES), lambda i: (i, 0)),
                   tok(LANES), pl.BlockSpec((SUBLANES, LANES), lambda i: (0, 0))],
        out_shape=[jax.ShapeDtypeStruct((t, d), F32),
                   jax.ShapeDtypeStruct((t * SUBLANES, LANES), F32),
                   jax.ShapeDtypeStruct((t, LANES), F32),
                   jax.ShapeDtypeStruct((SUBLANES, LANES), F32)],
        scratch_shapes=[pltpu.VMEM((LANES, LANES), F32)],
        compiler_params=_cparams("arbitrary"),
        name="mixer_out_router",
    )(*mix_args, *tail_args)


def _out_even(ys, u, yb, d_skip, glu_w, glu_b, w_out, *tail, seq, tm):
    tok = lambda n: pl.BlockSpec((tm, n), lambda i: (i, 0))
    full = lambda a: pl.BlockSpec(a.shape, lambda i: (0,) * a.ndim)
    args = [ys, u, yb, d_skip.reshape(1, -1), glu_w.astype(BF16), glu_b.reshape(1, -1),
            w_out[:S5_CH].astype(BF16), w_out[S5_CH:].astype(BF16)]
    specs = [tok(S5_CH), tok(S5_CH), tok(FOX_W)] + [full(a) for a in args[3:]]
    return _mixer_out(_out_even_kernel, args, specs, *tail, seq, tm)


def _out_odd(yc, yd, w_out, *tail, seq, tm):
    tok = lambda n: pl.BlockSpec((tm, n), lambda i: (i, 0))
    full = lambda a: pl.BlockSpec(a.shape, lambda i: (0,) * a.ndim)
    nc = yc.shape[1]
    args = [yc, yd, w_out[:nc].astype(BF16), w_out[nc:].astype(BF16)]
    specs = [tok(nc), tok(yd.shape[1])] + [full(a) for a in args[2:]]
    return _mixer_out(_out_odd_kernel, args, specs, *tail, seq, tm)


def _route_kernel(route_ref, cnt_ref, dest_ref, blk_ref, meta_ref):
    tm = route_ref.shape[0]
    cnt = cnt_ref[...]
    padded = jnp.floor((cnt + (MOE_BLOCK - 1.0)) * (1.0 / MOE_BLOCK)) * MOE_BLOCK
    r = lax.broadcasted_iota(jnp.int32, (LANES, LANES), 0)
    c = lax.broadcasted_iota(jnp.int32, (LANES, LANES), 1)
    hi, mid, lo = _split3(padded)
    incl = (r <= c).astype(BF16)
    pad_end = _bdot(hi, incl) + _bdot(mid, incl) + _bdot(lo, incl)
    pad_start = pad_end - padded
    lane = lax.broadcasted_iota(jnp.int32, (tm, LANES), 1)
    lanef = lane.astype(F32)
    rec = route_ref[...]
    topi = rec
    rank = pltpu.roll(rec, LANES - ROUTE_RANK, 1)
    start_row = pad_start[0:1, :]
    dest = jnp.zeros((tm, LANES), F32)
    for k in range(TOP_K):
        idx = jnp.sum(jnp.where(lane == k, topi, 0.0), axis=-1, keepdims=True)
        st = jnp.sum(jnp.where(lanef == idx, start_row, 0.0), axis=-1, keepdims=True)
        dest = jnp.where(lane == k, st, dest)
    dest_ref[...] = (dest + rank).astype(jnp.int32)

    nb = blk_ref.shape[1]
    end_col = jnp.sum(jnp.where(r == c, jnp.broadcast_to(pad_end[0:1, :], (LANES, LANES)), 0.0),
                      axis=-1, keepdims=True)
    jpos = lax.broadcasted_iota(jnp.int32, (LANES, nb), 1).astype(F32) * MOE_BLOCK
    esub = lax.broadcasted_iota(jnp.int32, (LANES, nb), 0)
    done = jnp.where((end_col <= jpos) & (esub < N_EXPERTS), 1.0, 0.0)
    be = jnp.minimum(jnp.sum(done, axis=0, keepdims=True), N_EXPERTS - 1.0)
    blk_ref[...] = jnp.broadcast_to(be, blk_ref.shape).astype(jnp.int32)
    lane1 = lax.broadcasted_iota(jnp.int32, (SUBLANES, LANES), 1)
    n_valid = jnp.sum(jnp.where(lane1 == N_EXPERTS - 1, pad_end, 0.0), axis=-1, keepdims=True) * (1.0 / MOE_BLOCK)
    sub1 = lax.broadcasted_iota(jnp.int32, (SUBLANES, LANES), 0)
    meta = jnp.where(sub1 == 0, pad_start + cnt, jnp.where(sub1 == 1, pad_end, jnp.broadcast_to(n_valid, (SUBLANES, LANES))))
    meta_ref[...] = meta.astype(jnp.int32)


def _route(route, cnt, n_blocks, tm):
    t = route.shape[0]
    nb_pad = -(-n_blocks // LANES) * LANES
    tok = pl.BlockSpec((tm, LANES), lambda i: (i, 0))
    fix = lambda n: pl.BlockSpec((SUBLANES, n), lambda i: (0, 0))
    return pl.pallas_call(
        _route_kernel,
        grid=(t // tm,),
        in_specs=[tok, fix(LANES)],
        out_specs=[tok, fix(nb_pad), fix(LANES)],
        out_shape=[jax.ShapeDtypeStruct((t, LANES), jnp.int32),
                   jax.ShapeDtypeStruct((SUBLANES, nb_pad), jnp.int32),
                   jax.ShapeDtypeStruct((SUBLANES, LANES), jnp.int32)],
        compiler_params=_cparams("arbitrary"),
        name="route_plan",
    )(route, cnt)


def _dispatch_kernel(pad_ref, dest_ref, h_ref, xb_ref, zero_ref, sem_z, sem_s):
    i = pl.program_id(0)
    tm = h_ref.shape[0]

    @pl.when(i == 0)
    def _():
        zero_ref[...] = jnp.zeros_like(zero_ref)
        sizes = [1 << b for b in range(int(math.log2(MOE_BLOCK)) - 1, -1, -1)]

        def fill(e, carry, do_wait):
            start = pad_ref[0, e]
            n_pad = pad_ref[1, e] - start
            off = start
            for sz in sizes:
                take = (n_pad & sz) != 0
                cp = pltpu.make_async_copy(zero_ref.at[pl.ds(0, sz)], xb_ref.at[pl.ds(off, sz)], sem_z)

                @pl.when(take)
                def _():
                    if do_wait:
                        cp.wait()
                    else:
                        cp.start()
                off = off + jnp.where(take, sz, 0)
            return carry

        half = zero_ref.shape[0]

        def fill_unused(j, carry, do_wait):
            for part in range(MOE_BLOCK // half):
                cp = pltpu.make_async_copy(zero_ref, xb_ref.at[pl.ds(j * MOE_BLOCK + part * half, half)], sem_z)
                if do_wait:
                    cp.wait()
                else:
                    cp.start()
            return carry

        n_blocks = xb_ref.shape[0] // MOE_BLOCK
        lax.fori_loop(0, N_EXPERTS, lambda e, c: fill(e, c, False), 0)
        lax.fori_loop(pad_ref[2, 0], n_blocks, lambda j, c: fill_unused(j, c, False), 0)
        lax.fori_loop(0, N_EXPERTS, lambda e, c: fill(e, c, True), 0)
        lax.fori_loop(pad_ref[2, 0], n_blocks, lambda j, c: fill_unused(j, c, True), 0)

    def issue(r, carry):
        for k in range(TOP_K):
            pltpu.make_async_copy(h_ref.at[r], xb_ref.at[dest_ref[r * TOP_K + k]], sem_s).start(priority=k % 2)
        return carry

    lax.fori_loop(0, tm, issue, 0, unroll=ROW_DMA_UNROLL)
    for k in range(TOP_K):
        pltpu.make_async_copy(h_ref, xb_ref.at[pl.ds(0, tm)], sem_s).wait()


def _dispatch(h2t, dest, meta, n_slots, tm):
    t = h2t.shape[0] // SUBLANES
    h3 = h2t.reshape(t, SUBLANES, LANES)
    return pl.pallas_call(
        _dispatch_kernel,
        grid_spec=pltpu.PrefetchScalarGridSpec(
            num_scalar_prefetch=1,
            grid=(t // tm,),
            in_specs=[pl.BlockSpec((tm * TOP_K,), lambda i, p: (i,), memory_space=pltpu.SMEM),
                      pl.BlockSpec((tm, SUBLANES, LANES), lambda i, p: (i, 0, 0))],
            out_specs=pl.BlockSpec(memory_space=pl.ANY),
            scratch_shapes=[pltpu.VMEM((MOE_BLOCK // 2, SUBLANES, LANES), F32),
                            pltpu.SemaphoreType.DMA, pltpu.SemaphoreType.DMA]),
        out_shape=jax.ShapeDtypeStruct((n_slots, SUBLANES, LANES), F32),
        compiler_params=_cparams("arbitrary"),
        name="moe_dispatch",
    )(meta[:3, :N_EXPERTS], dest, h3)


def _expert_kernel(be_ref, nv_ref, nxt_ref, x_ref, wgu_hbm, bgu_ref, wd_hbm, bd_ref, y_ref,
                   wgu_f32, wd_f32, wgu_bf, wd_bf, sem, *, layer):
    j = pl.program_id(0)
    valid = j < nv_ref[0]
    first = valid & ((j == 0) | (be_ref[j] != be_ref[jnp.maximum(j - 1, 0)]))

    def weight_copies(e):
        return (pltpu.make_async_copy(wgu_hbm.at[layer, e], wgu_f32, sem.at[0]),
                pltpu.make_async_copy(wd_hbm.at[layer, e], wd_f32, sem.at[1]))

    @pl.when(j == 0)
    def _():
        for cp in weight_copies(be_ref[0]):
            cp.start()

    @pl.when(first)
    def _():
        for cp in weight_copies(be_ref[j]):
            cp.wait()
        wgu_bf[...] = wgu_f32[...].astype(BF16)
        wd_bf[...] = wd_f32[...].astype(BF16)

        @pl.when(nxt_ref[j] >= 0)
        def _():
            for cp in weight_copies(nxt_ref[j]):
                cp.start()

    @pl.when(valid)
    def _():
        x = jnp.concatenate([x_ref[pl.ds(c, MOE_BLOCK, stride=SUBLANES), :] for c in range(D_MODEL // LANES)],
                            axis=1).astype(BF16)
        gu = _bdot(x, wgu_bf[...]) + bgu_ref[0]
        x_glu = jnp.minimum(gu[:, :D_EXPERT], SWIGLU_LIMIT)
        x_lin = jnp.clip(gu[:, D_EXPERT:], -SWIGLU_LIMIT, SWIGLU_LIMIT)
        act = x_glu * _sigmoid(SWIGLU_ALPHA * x_glu) * (x_lin + 1.0)
        y = _bdot(act.astype(BF16), wd_bf[...]) + bd_ref[0]
        for c in range(D_MODEL // LANES):
            y_ref[pl.ds(c, MOE_BLOCK, stride=SUBLANES), :] = y[:, c * LANES:(c + 1) * LANES]

    @pl.when(jnp.logical_not(valid))
    def _():
        y_ref[...] = jnp.zeros_like(y_ref)


def _experts(xb, block_expert, n_valid, w_gu, b_gu, w_down, b_down, layer):
    n_slots = xb.shape[0]
    n_blocks = n_slots // MOE_BLOCK
    rows = MOE_BLOCK * SUBLANES
    x2 = xb.reshape(n_slots * SUBLANES, LANES)
    depth, ne, d, de2 = w_gu.shape
    idx = jnp.arange(n_blocks, dtype=jnp.int32)
    is_first = ((idx == 0) | (block_expert != jnp.roll(block_expert, 1))) & (idx < n_valid[0])
    first_at = lax.cummin(jnp.where(is_first, idx, n_blocks)[::-1])[::-1]
    next_first = jnp.concatenate([first_at[1:], jnp.full((1,), n_blocks, jnp.int32)])
    nxt = jnp.where(next_first < n_blocks, block_expert[jnp.minimum(next_first, n_blocks - 1)], -1)
    last = lambda j, be, nv, nx: jnp.minimum(j, nv[0] - 1)
    bmap = lambda j, be, nv, nx: (layer, be[last(j, be, nv, nx)], 0, 0)
    return pl.pallas_call(
        functools.partial(_expert_kernel, layer=layer),
        grid_spec=pltpu.PrefetchScalarGridSpec(
            num_scalar_prefetch=3,
            grid=(n_blocks,),
            in_specs=[pl.BlockSpec((rows, LANES), lambda j, be, nv, nx: (last(j, be, nv, nx), 0)),
                      pl.BlockSpec(memory_space=pl.ANY),
                      pl.BlockSpec((None, 1, 1, de2), bmap),
                      pl.BlockSpec(memory_space=pl.ANY),
                      pl.BlockSpec((None, 1, 1, d), bmap)],
            out_specs=pl.BlockSpec((rows, LANES), lambda j, be, nv, nx: (j, 0)),
            scratch_shapes=[pltpu.VMEM((d, de2), F32), pltpu.VMEM((de2 // 2, d), F32),
                            pltpu.VMEM((d, de2), BF16), pltpu.VMEM((de2 // 2, d), BF16),
                            pltpu.SemaphoreType.DMA((2,))]),
        out_shape=jax.ShapeDtypeStruct((n_slots * SUBLANES, LANES), F32),
        compiler_params=_cparams("arbitrary"),
        name="moe_experts",
    )(block_expert, n_valid, nxt.astype(jnp.int32), x2, w_gu, b_gu.reshape(depth, ne, 1, de2), w_down,
      b_down.reshape(depth, ne, 1, d))


def _combine_kernel(dest_ref, dest_next_ref, yb_ref, gate_ref, x1_ref, g2_ref, gpost_ref, o_ref, buf, sem):
    i = pl.program_id(0)
    tm = x1_ref.shape[0]
    slot = i % 2

    def gather(idx_ref, into):
        def issue(r, carry):
            for k in range(TOP_K):
                src = pl.multiple_of(idx_ref[r * TOP_K + k] * SUBLANES, SUBLANES)
                dst = pl.multiple_of((k * tm + r) * SUBLANES, SUBLANES)
                pltpu.make_async_copy(yb_ref.at[pl.ds(src, SUBLANES), :], buf.at[into, pl.ds(dst, SUBLANES), :],
                                      sem.at[into]).start(priority=k % 2)
            return carry
        lax.fori_loop(0, tm, issue, 0, unroll=ROW_DMA_UNROLL)

    @pl.when(i == 0)
    def _():
        gather(dest_ref, 0)

    @pl.when(i + 1 < pl.num_programs(0))
    def _():
        gather(dest_next_ref, 1 - slot)

    pltpu.make_async_copy(yb_ref.at[pl.ds(0, TOP_K * tm * SUBLANES), :], buf.at[slot], sem.at[slot]).wait()
    gates = gate_ref[...]
    gk = [jnp.broadcast_to(gates[:, ROUTE_GATE + k:ROUTE_GATE + k + 1], (tm, LANES)) for k in range(TOP_K)]
    b2 = buf.at[slot]
    cols = []
    for c in range(D_MODEL // LANES):
        acc = jnp.zeros((tm, LANES), F32)
        for k in range(TOP_K):
            acc = acc + gk[k] * b2[pl.ds(k * tm * SUBLANES + c, tm, stride=SUBLANES), :]
        cols.append(acc)
    f = jnp.concatenate(cols, axis=1)
    o_ref[...] = x1_ref[...] + g2_ref[0] * (_rms(f) * gpost_ref[...])


def _combine(yb, dest, gates, x1, mod3, g_post, seq, tm):
    t, d = x1.shape
    tps = seq // tm
    n = t // tm
    return pl.pallas_call(
        _combine_kernel,
        grid=(n,),
        in_specs=[pl.BlockSpec((tm * TOP_K,), lambda i: (i,), memory_space=pltpu.SMEM),
                  pl.BlockSpec((tm * TOP_K,), lambda i: (jnp.minimum(i + 1, n - 1),), memory_space=pltpu.SMEM),
                  pl.BlockSpec(memory_space=pl.ANY),
                  pl.BlockSpec((tm, LANES), lambda i: (i, 0)),
                  pl.BlockSpec((tm, d), lambda i: (i, 0)),
                  _mod_spec(5, tps),
                  pl.BlockSpec(g_post.shape, lambda i: (0, 0))],
        out_specs=pl.BlockSpec((tm, d), lambda i: (i, 0)),
        out_shape=jax.ShapeDtypeStruct((t, d), F32),
        scratch_shapes=[pltpu.VMEM((2, TOP_K * tm * SUBLANES, LANES), F32), pltpu.SemaphoreType.DMA((2,))],
        compiler_params=_cparams("arbitrary"),
        name="moe_combine",
    )(dest, dest, yb, gates, x1, mod3, g_post)


def _moe(h2t, route, cnt, x1, mod3, g_post, w_gu, b_gu, w_down, b_down, layer, seq):
    t = x1.shape[0]
    n_blocks = t * TOP_K // MOE_BLOCK + N_EXPERTS
    dest_l, blk, meta = _route(route, cnt, n_blocks, min(1024, t))
    dest = dest_l[:, :TOP_K].reshape(t * TOP_K)
    xb = _dispatch(h2t, dest, meta, n_blocks * MOE_BLOCK, MOE_BLOCK)
    yb = _experts(xb, blk[0, :n_blocks], meta[2, :1], w_gu, b_gu, w_down, b_down, layer)
    return _combine(yb, dest, route, x1, mod3, g_post, seq, MOE_BLOCK)


TOKEN_TILE = 512
OUT_TILE = 512
FOX_Q_TILE = 512
FOX_K_TILE = 256


def kernel(x, c, ada_w, ada_b, norm_pre_mix, norm_post_mix, norm_pre_ffn, norm_post_ffn, ev_w_in, fox_b_f, s5_lam_re, s5_lam_im, s5_log_dt, s5_b_re, s5_b_im, s5_c_re, s5_c_im, s5_d, s5_glu_w, s5_glu_b, ev_w_out, od_w_in, gla_w_up, gla_b_gate, od_w_out, router_w, router_b, exp_w_gu, exp_b_gu, exp_w_down, exp_b_down):
    bsz, seq, d = x.shape
    t = bsz * seq
    tm = min(TOKEN_TILE, seq)
    x2 = x.reshape(t, d)
    mod = _modulation(c, ada_w, ada_b)
    for l in range(DEPTH):
        i = l // 2
        mod3 = mod[l].reshape(bsz, 1, 6 * d)
        row = lambda a: a[l].reshape(1, -1)
        tail = (x2, mod3, row(norm_post_mix), row(norm_pre_ffn), router_w[l], router_b[l])
        if l % 2 == 0:
            u, q, k, v_t = _in_even(x2, mod3, row(norm_pre_mix), ev_w_in[i], fox_b_f[i], seq, tm)
            tables = _s5_tables(s5_lam_re[i], s5_lam_im[i], s5_log_dt[i], s5_b_re[i], s5_b_im[i],
                                s5_c_re[i], s5_c_im[i])
            ys = _s5_scan(u, bsz, seq, tables)
            yb = _fox(q, k, v_t, bsz, seq, min(FOX_Q_TILE, seq), min(FOX_K_TILE, seq))
            outs = _out_even(ys, u, yb, s5_d[i], s5_glu_w[i], s5_glu_b[i], ev_w_out[i], *tail, seq=seq, tm=min(OUT_TILE, seq))
        else:
            rq, rk, rv, sg, gq, gk, gv, sr, la = _in_odd(x2, mod3, row(norm_pre_mix), od_w_in[i],
                                                         gla_w_up[i], gla_b_gate[i], seq, tm)
            yc = _retention(rq, rk, rv, sg, bsz, seq)
            yd = _gla(gq, gk, la, gv, sr, bsz, seq)
            outs = _out_odd(yc, yd, od_w_out[i], *tail, seq=seq, tm=min(OUT_TILE, seq))
        x1, h2t, route, cnt = outs
        x2 = _moe(h2t, route, cnt, x1, mod3, row(norm_post_ffn),
                  exp_w_gu, exp_b_gu, exp_w_down, exp_b_down, l, seq)
    return x2.reshape(bsz, seq, d)
```

```python
import functools
import math

import jax
import jax.numpy as jnp
from jax import lax
from jax.experimental import pallas as pl
from jax.experimental.pallas import tpu as pltpu

F32 = jnp.float32
BF16 = jnp.bfloat16
HIGHEST = lax.Precision.HIGHEST

D_MODEL = 1024
DEPTH = 2
EPS = 1e-6
S5_CH = 512
S5_GROUP = 16
S5_GROUPS = S5_CH // S5_GROUP
S5_STATE = 64
S5_CHUNK = 8
FOX_HEADS = 8
FOX_DH = 64
FOX_W = FOX_HEADS * FOX_DH
LOG2_E = 1.4426950408889634
FOX_VROWS = FOX_DH + 16
FOX_HPS = 8
RET_HEADS = 4
RET_DK = 128
RET_DV = 128
ROPE_BASE = 10000.0
GLA_HEADS = 4
GLA_DK = 64
GLA_DV = 128
GLA_RANK = 16
GLA_TAU = 16.0
GLA_CHUNK = 64
GLA_SUB = 16
N_EXPERTS = 32
TOP_K = 4
D_EXPERT = 1024
SWIGLU_LIMIT = 7.0
SWIGLU_ALPHA = 1.702
MOE_BLOCK = 256
ROW_DMA_UNROLL = 4

LANES = 128
SUBLANES = 8
VMEM_LIMIT = 56 * 1024 * 1024


def _cparams(*sem):
    return pltpu.CompilerParams(dimension_semantics=sem, vmem_limit_bytes=VMEM_LIMIT)


def _bdot(a, b):
    return jnp.dot(a, b, preferred_element_type=F32)


def _dot_nt(a, b):
    return lax.dot_general(a, b, (((1,), (1,)), ((), ())), preferred_element_type=F32)


def _dot_tn(a, b):
    return lax.dot_general(a, b, (((0,), (0,)), ((), ())), preferred_element_type=F32)


def _split3(x):
    hi = x.astype(BF16)
    r = x - hi.astype(F32)
    mid = r.astype(BF16)
    lo = (r - mid.astype(F32)).astype(BF16)
    return hi, mid, lo


def _dot01(m01, x):
    hi, mid, lo = _split3(x)
    return _bdot(m01, hi) + _bdot(m01, mid) + _bdot(m01, lo)


def _lower_tri(n, strict=False):
    r = lax.broadcasted_iota(jnp.int32, (n, n), 0)
    c = lax.broadcasted_iota(jnp.int32, (n, n), 1)
    return ((r > c) if strict else (r >= c)).astype(BF16)


def _log_sigmoid(x):
    return jnp.minimum(x, 0.0) - jnp.log1p(jnp.exp(-jnp.abs(x)))


def _sigmoid(x):
    return 1.0 / (1.0 + jnp.exp(-x))


def _silu(x):
    return x * _sigmoid(x)


def _rms(x):
    return x * lax.rsqrt(jnp.mean(x * x, axis=-1, keepdims=True) + EPS)


def _mod_kernel(c_ref, w_ref, b_ref, o_ref):
    s = _silu(c_ref[...])
    w = w_ref[0]
    s_hi, w_hi = s.astype(BF16), w.astype(BF16)
    s_lo = (s - s_hi.astype(F32)).astype(BF16)
    w_lo = (w - w_hi.astype(F32)).astype(BF16)
    o_ref[0] = _bdot(s_hi, w_hi) + _bdot(s_lo, w_hi) + _bdot(s_hi, w_lo) + b_ref[0]


def _modulation(c, ada_w, ada_b):
    depth, d, n = ada_w.shape
    bsz = c.shape[0]
    tn = D_MODEL
    return pl.pallas_call(
        _mod_kernel,
        grid=(depth, n // tn),
        in_specs=[pl.BlockSpec((bsz, d), lambda l, j: (0, 0)),
                  pl.BlockSpec((1, d, tn), lambda l, j: (l, 0, j)),
                  pl.BlockSpec((1, 1, tn), lambda l, j: (l, 0, j))],
        out_specs=pl.BlockSpec((1, bsz, tn), lambda l, j: (l, 0, j)),
        out_shape=jax.ShapeDtypeStruct((depth, bsz, n), F32),
        compiler_params=_cparams("parallel", "parallel"),
        name="adaln_mod",
    )(c, ada_w, ada_b.reshape(depth, 1, n))


def _mod_spec(chunk, tiles_per_seq):
    return pl.BlockSpec((1, 1, D_MODEL), lambda i: (i // tiles_per_seq, 0, chunk))


def _prenorm(x, g_ref, sc_ref, sh_ref):
    return _rms(x) * (g_ref[...] * (1.0 + sc_ref[0])) + sh_ref[0]


def _in_even_kernel(x_ref, sh_ref, sc_ref, g_ref, w_ref, bf_ref,
                    u_ref, q_ref, k_ref, vt_ref, carry_ref, *, tiles_per_seq):
    i = pl.program_id(0)
    tm = x_ref.shape[0]
    h = _prenorm(x_ref[...], g_ref, sc_ref, sh_ref)
    z = _bdot(h.astype(BF16), w_ref[...])
    u_ref[...] = z[:, 0:S5_CH]
    tail = jnp.concatenate([jnp.ones((1, tm), F32), jnp.zeros((FOX_VROWS - FOX_DH - 1, tm), F32)], axis=0)
    for pr in range(FOX_HEADS // 2):
        c0 = S5_CH + 2 * FOX_W + pr * LANES
        v_pair = z[:, c0:c0 + LANES].T
        for hh in range(2):
            r0 = (2 * pr + hh) * FOX_VROWS
            vt_ref[r0:r0 + FOX_VROWS, :] = jnp.concatenate(
                [v_pair[hh * FOX_DH:(hh + 1) * FOX_DH], tail], axis=0).astype(BF16)
    ls = _log_sigmoid(z[:, S5_CH + 3 * FOX_W:] + bf_ref[...])

    @pl.when(i % tiles_per_seq == 0)
    def _():
        carry_ref[...] = jnp.zeros_like(carry_ref)

    cum = _dot01(_lower_tri(tm), ls) + carry_ref[...]
    carry_ref[...] = cum[tm - 1:tm, :]

    lane = lax.broadcasted_iota(jnp.int32, (1, LANES), 1)
    feat = lane < FOX_DH
    ones = jnp.where(lane < FOX_DH + 3, 1.0, 0.0)
    for hd in range(FOX_HEADS):
        blk = (hd * FOX_DH) // LANES * LANES
        qs = z[:, S5_CH + blk:S5_CH + blk + LANES] * (FOX_DH ** -0.5 * LOG2_E)
        ks = z[:, S5_CH + FOX_W + blk:S5_CH + FOX_W + blk + LANES]
        if (hd * FOX_DH) % LANES:
            qs = pltpu.roll(qs, LANES - FOX_DH, 1)
            ks = pltpu.roll(ks, LANES - FOX_DH, 1)
        nf = jnp.broadcast_to(-LOG2_E * cum[:, hd:hd + 1], (tm, LANES))
        hi = nf.astype(BF16).astype(F32)
        mid = (nf - hi).astype(BF16).astype(F32)
        lo = nf - hi - mid
        bias = jnp.where(lane == FOX_DH, hi, jnp.where(lane == FOX_DH + 1, mid,
                                                       jnp.where(lane == FOX_DH + 2, lo, 0.0)))
        q_ref[:, hd * LANES:(hd + 1) * LANES] = jnp.where(feat, qs, ones).astype(BF16)
        k_ref[:, hd * LANES:(hd + 1) * LANES] = jnp.where(feat, ks, bias).astype(BF16)


def _in_even(x2, mod3, gain, w_in, b_f, seq, tm):
    t, d = x2.shape
    tiles_per_seq = seq // tm
    nw = S5_CH + 3 * FOX_W
    w = jnp.concatenate([w_in[:, :nw], jnp.pad(w_in[:, nw:], ((0, 0), (0, LANES - FOX_HEADS)))],
                        axis=1).astype(BF16)
    bf = jnp.pad(b_f, (0, LANES - FOX_HEADS)).reshape(1, LANES)
    tok = lambda n: pl.BlockSpec((tm, n), lambda i: (i, 0))
    full = lambda a: pl.BlockSpec(a.shape, lambda i: (0,) * a.ndim)
    return pl.pallas_call(
        functools.partial(_in_even_kernel, tiles_per_seq=tiles_per_seq),
        grid=(t // tm,),
        in_specs=[tok(d), _mod_spec(0, tiles_per_seq), _mod_spec(1, tiles_per_seq),
                  full(gain), full(w), full(bf)],
        out_specs=[tok(S5_CH), tok(FOX_HEADS * LANES), tok(FOX_HEADS * LANES),
                   pl.BlockSpec((FOX_HEADS * FOX_VROWS, tm), lambda i: (i // tiles_per_seq, i % tiles_per_seq))],
        out_shape=[jax.ShapeDtypeStruct((t, S5_CH), F32),
                   jax.ShapeDtypeStruct((t, FOX_HEADS * LANES), BF16),
                   jax.ShapeDtypeStruct((t, FOX_HEADS * LANES), BF16),
                   jax.ShapeDtypeStruct((t // seq * FOX_HEADS * FOX_VROWS, seq), BF16)],
        scratch_shapes=[pltpu.VMEM((1, LANES), F32)],
        compiler_params=_cparams("arbitrary"),
        name="in_proj_even",
    )(x2, mod3, mod3, gain, w, bf)


S5_TILE_GROUPS = LANES // S5_GROUP
S5_SEQ_PARTS = 4


def _s5_kernel(u_ref, wt_ref, ws_ref, wc_ref, a_ref, y_ref, x_scr, e_scr, hp_scr, h_scr, *, nb, ncl):
    ell = S5_CHUNK
    sw = S5_TILE_GROUPS * 2 * S5_STATE

    @pl.when(pl.program_id(1) == 0)
    def _():
        h_scr[...] = jnp.zeros_like(h_scr)

    blk = nb * SUBLANES
    for hi in range(ncl // SUBLANES):
        for b in range(nb):
            for t in range(ell):
                x_scr[hi * blk + b * SUBLANES:hi * blk + (b + 1) * SUBLANES, t * LANES:(t + 1) * LANES] = (
                    u_ref[b, pl.ds(hi * SUBLANES * ell + t, SUBLANES, stride=ell), :])
    x = x_scr[...].astype(BF16)
    e = _bdot(x, ws_ref[0])
    tg = S5_TILE_GROUPS
    for j in range(tg):
        e_scr[j] = e[:, j * LANES:(j + 1) * LANES]
        e_scr[tg + j] = pltpu.roll(e[:, j * LANES:(j + 1) * LANES], S5_STATE, 1)
    a1 = jnp.broadcast_to(a_ref[0, 0:1, :], (nb, sw))
    a2 = jnp.broadcast_to(a_ref[0, 1:2, :], (nb, sw))
    a2s = jnp.broadcast_to(a_ref[0, 2:3, :], (nb, sw))

    def body(c, carry):
        h, hs = carry
        rows_c = pl.ds((c // SUBLANES) * blk + c % SUBLANES, nb, stride=SUBLANES)
        for j in range(tg):
            hp_scr[j, rows_c, :] = h[:, j * LANES:(j + 1) * LANES]
        e1 = jnp.concatenate([e_scr[j, rows_c, :] for j in range(tg)], axis=1)
        e2 = jnp.concatenate([e_scr[tg + j, rows_c, :] for j in range(tg)], axis=1)
        return a1 * h + a2 * hs + e1, a1 * hs + a2s * h + e2

    h, hs = lax.fori_loop(0, ncl, body, (h_scr[0], h_scr[1]))
    h_scr[0] = h
    h_scr[1] = hs
    hp = jnp.concatenate([hp_scr[j] for j in range(tg)], axis=1).astype(BF16)
    y = _bdot(x, wt_ref[0]) + _bdot(hp, wc_ref[0])
    for hi in range(ncl // SUBLANES):
        for b in range(nb):
            for t in range(ell):
                y_ref[b, pl.ds(hi * SUBLANES * ell + t, SUBLANES, stride=ell), :] = (
                    y[hi * blk + b * SUBLANES:hi * blk + (b + 1) * SUBLANES, t * LANES:(t + 1) * LANES])


def _s5_tables(lam_re, lam_im, log_dt, b_re, b_im, c_re, c_im):
    ell, p, g = S5_CHUNK, S5_STATE, S5_GROUPS
    lr, li = lam_re.astype(F32), lam_im.astype(F32)
    dt = jnp.exp(log_dt.astype(F32))[:, None]
    mag = jnp.exp(lr * dt)
    a_re, a_im = mag * jnp.cos(li * dt), mag * jnp.sin(li * dt)
    den = lr * lr + li * li
    n_re, n_im = a_re - 1.0, a_im
    z_re = (n_re * lr + n_im * li) / den
    z_im = (n_im * lr - n_re * li) / den
    br, bi = b_re.astype(F32), b_im.astype(F32)
    bb_re = z_re[..., None] * br - z_im[..., None] * bi
    bb_im = z_re[..., None] * bi + z_im[..., None] * br
    j = jnp.arange(ell + 1, dtype=F32)[:, None, None]
    pmag = jnp.exp(j * (lr * dt)[None])
    pr, pi = pmag * jnp.cos(j * (li * dt)[None]), pmag * jnp.sin(j * (li * dt)[None])
    w_re = pr[..., None] * bb_re[None] - pi[..., None] * bb_im[None]
    w_im = pr[..., None] * bb_im[None] + pi[..., None] * bb_re[None]
    cr, ci = c_re.astype(F32), c_im.astype(F32)
    kern = (jnp.einsum('gcp,jgpd->jgcd', cr, w_re[:ell], precision=HIGHEST)
            - jnp.einsum('gcp,jgpd->jgcd', ci, w_im[:ell], precision=HIGHEST))
    s_idx = jnp.arange(ell)[:, None]
    t_idx = jnp.arange(ell)[None, :]
    lag = jnp.clip(t_idx - s_idx, 0, ell - 1)
    toep = kern[lag]
    toep = jnp.where((t_idx >= s_idx)[:, :, None, None, None], toep, 0.0)
    toep = toep.transpose(2, 0, 4, 1, 3)
    rev = jnp.arange(ell - 1, -1, -1)
    st_re = w_re[rev].transpose(1, 0, 3, 2)
    st_im = w_im[rev].transpose(1, 0, 3, 2)
    wst = jnp.concatenate([st_re, st_im], -1)
    p1r, p1i = pr[1:], pi[1:]
    c_hr = cr[None] * p1r[:, :, None, :] - ci[None] * p1i[:, :, None, :]
    c_hi = -cr[None] * p1i[:, :, None, :] - ci[None] * p1r[:, :, None, :]
    cst = jnp.concatenate([c_hr, c_hi], axis=-1).transpose(1, 3, 0, 2)
    al_r, al_i = pr[ell], pi[ell]
    a_rows = jnp.stack([jnp.concatenate([al_r, al_r], -1), jnp.concatenate([-al_i, al_i], -1),
                        jnp.concatenate([al_i, -al_i], -1)], axis=1)

    tg = S5_TILE_GROUPS
    nt = g // tg
    tile = lambda a: a.reshape((nt, tg) + a.shape[1:])
    lane_grp = jnp.arange(LANES) // S5_GROUP
    step_lane_grp = jnp.tile(lane_grp, ell)
    spread_out = (jnp.eye(ell, dtype=BF16)[:, None, :, None, None]
                  * jnp.eye(S5_GROUP, dtype=BF16)[None, :, None, None, :]
                  * jnp.ones((1, 1, 1, tg, 1), BF16)).reshape(ell * S5_GROUP, ell * LANES)
    toep_t = tile(toep).transpose(0, 2, 1, 3, 4, 5).reshape(nt, ell, LANES, ell * S5_GROUP).astype(BF16)
    wt = jnp.einsum('ksrm,mn->ksrn', toep_t, spread_out)
    wt = jnp.where((lane_grp[:, None] == step_lane_grp[None, :])[None, None], wt, 0)
    wt = wt.reshape(nt, ell * LANES, ell * LANES)
    spread_state = jnp.tile(jnp.eye(2 * p, dtype=BF16), (1, tg))
    state_grp = jnp.arange(tg * 2 * p) // (2 * p)
    wst_t = tile(wst).transpose(0, 2, 1, 3, 4).reshape(nt, ell, LANES, 2 * p).astype(BF16)
    ws = jnp.einsum('ksrp,pn->ksrn', wst_t, spread_state)
    ws = jnp.where((lane_grp[:, None] == state_grp[None, :])[None, None], ws, 0)
    ws = ws.reshape(nt, ell * LANES, tg * 2 * p)
    cst_t = tile(cst).reshape(nt, tg * 2 * p, ell * S5_GROUP).astype(BF16)
    wc = jnp.einsum('krm,mn->krn', cst_t, spread_out)
    wc = jnp.where((state_grp[:, None] == step_lane_grp[None, :])[None], wc, 0)
    a_t = tile(a_rows).transpose(0, 2, 1, 3).reshape(nt, 3, tg * 2 * p)
    a_t = jnp.pad(a_t, ((0, 0), (0, SUBLANES - 3), (0, 0)))
    return wt, ws, wc, a_t


def _s5_scan(u, bsz, seq, tables):
    wt, ws, wc, a_t = tables
    ell = S5_CHUNK
    nt = S5_GROUPS // S5_TILE_GROUPS
    parts = S5_SEQ_PARTS if seq % (S5_SEQ_PARTS * ell * 2 * SUBLANES) == 0 else 1
    ncl = seq // parts // ell
    rows = bsz * ncl
    sw = S5_TILE_GROUPS * 2 * S5_STATE
    u3 = u.reshape(bsz, seq, S5_CH)
    io = pl.BlockSpec((bsz, seq // parts, LANES), lambda k, s: (0, s, k))
    per_tile = lambda a: pl.BlockSpec((1,) + a.shape[1:], lambda k, s: (k, 0, 0))
    y = pl.pallas_call(
        functools.partial(_s5_kernel, nb=bsz, ncl=ncl),
        grid=(nt, parts),
        in_specs=[io, per_tile(wt), per_tile(ws), per_tile(wc), per_tile(a_t)],
        out_specs=io,
        out_shape=jax.ShapeDtypeStruct((bsz, seq, S5_CH), F32),
        scratch_shapes=[pltpu.VMEM((rows, ell * LANES), F32),
                        pltpu.VMEM((2 * S5_TILE_GROUPS, rows, LANES), F32),
                        pltpu.VMEM((S5_TILE_GROUPS, rows, LANES), F32),
                        pltpu.VMEM((2, bsz, sw), F32)],
        compiler_params=_cparams("parallel", "arbitrary"),
        name="s5_scan",
    )(u3, wt, ws, wc, a_t)
    return y.reshape(bsz * seq, S5_CH)


def _fox_kernel(q_ref, k_ref, vt_ref, o_ref, *, tq, tk):
    i = pl.program_id(2)
    nh = FOX_HPS
    q = [q_ref[:, h * LANES:(h + 1) * LANES] for h in range(nh)]
    key = lax.broadcasted_iota(jnp.int32, (tk, tq), 0)
    qry = lax.broadcasted_iota(jnp.int32, (tk, tq), 1)
    per_q = tq // tk

    def block(j0, carry, mask, q0=0):
        kj = k_ref[pl.ds(j0, tk), :]
        vtj = vt_ref[:, pl.ds(j0, tk)]
        ss = [_dot_nt(kj[:, h * LANES:(h + 1) * LANES], q[h][q0:]) for h in range(nh)]
        stats = []
        for h in range(nh):
            m = carry[h][0][:, q0:]
            s = ss[h] if mask is None else jnp.where(mask[:, q0:], ss[h], -jnp.inf)
            m_new = jnp.maximum(m, jnp.max(s, axis=0, keepdims=True))
            p = jnp.exp2(s - m_new)
            stats.append((m_new, jnp.exp2(m - m_new), p.astype(BF16)))
        out = []
        for h in range(nh):
            m_new, alpha, p = stats[h]
            acc = alpha * carry[h][1][:, q0:] + _bdot(vtj[h * FOX_VROWS:(h + 1) * FOX_VROWS, :], p)
            if q0:
                m_new = jnp.concatenate([carry[h][0][:, :q0], m_new], axis=1)
                acc = jnp.concatenate([carry[h][1][:, :q0], acc], axis=1)
            out.append((m_new, acc))
        return tuple(out)

    init = tuple((jnp.full((1, tq), -jnp.inf, F32), jnp.zeros((FOX_VROWS, tq), F32)) for _ in range(nh))
    carry = lax.fori_loop(0, i * per_q, lambda j, c: block(pl.multiple_of(j * tk, tk), c, None), init)
    for d in range(per_q):
        carry = block(pl.multiple_of(i * tq + d * tk, tk), carry, key + d * tk <= qry, d * tk)
    for g in range(nh // 2):
        o_t = jnp.concatenate([carry[h][1][:FOX_DH] / carry[h][1][FOX_DH:FOX_DH + 1]
                               for h in (2 * g, 2 * g + 1)], axis=0)
        o_ref[:, g * LANES:(g + 1) * LANES] = o_t.T.astype(o_ref.dtype)


def _fox(q_aug, k_aug, v_t, bsz, seq, tq, tk):
    t = q_aug.shape[0]
    nh = FOX_HPS
    groups = FOX_HEADS // nh
    nq = seq // tq
    return pl.pallas_call(
        functools.partial(_fox_kernel, tq=tq, tk=tk),
        grid=(bsz, groups, nq),
        in_specs=[pl.BlockSpec((tq, nh * LANES), lambda b, p, i: (b * nq + i, p)),
                  pl.BlockSpec((seq, nh * LANES), lambda b, p, i: (b, p)),
                  pl.BlockSpec((nh * FOX_VROWS, seq), lambda b, p, i: (b * groups + p, 0))],
        out_specs=pl.BlockSpec((tq, nh * FOX_DH), lambda b, p, i: (b * nq + i, p)),
        out_shape=jax.ShapeDtypeStruct((t, FOX_W), BF16),
        compiler_params=_cparams("parallel", "parallel", "arbitrary"),
        name="fox_attention",
    )(q_aug, k_aug, v_t)


_ODD_COLS = (("rq", 512), ("rk", 512), ("rv", 512), ("rg", 512), ("gq", 256), ("gk", 256),
             ("gv", 512), ("gr", 512), ("glr", LANES))


def _odd_offsets():
    off, out = 0, {}
    for name, w in _ODD_COLS:
        out[name] = (off, off + w)
        off += w
    return out, off


def _in_odd_kernel(x_ref, sh_ref, sc_ref, g_ref, w_ref, cos_ref, sin_ref, wup_ref, bg_ref,
                   rq_ref, rk_ref, rv_ref, sg_ref, gq_ref, gk_ref, gv_ref, sr_ref, la_ref):
    h = _prenorm(x_ref[...], g_ref, sc_ref, sh_ref)
    z = _bdot(h.astype(BF16), w_ref[...])
    off, _ = _odd_offsets()
    col = lambda n: z[:, off[n][0]:off[n][1]]
    cos, sin = cos_ref[...], sin_ref[...]

    def rope(t, scale):
        heads = []
        for hd in range(RET_HEADS):
            th = t[:, hd * RET_DK:(hd + 1) * RET_DK]
            heads.append((th * cos + pltpu.roll(th, RET_DK // 2, 1) * sin) * scale)
        return jnp.concatenate(heads, axis=1).astype(BF16)

    rq_ref[...] = rope(col("rq"), 1.0)
    rk_ref[...] = rope(col("rk"), RET_DK ** -0.5)
    rv_ref[...] = col("rv").astype(BF16)
    sg_ref[...] = _silu(col("rg"))
    gq_ref[...] = col("gq") * (GLA_DK ** -0.5)
    gk_ref[...] = col("gk")
    gv_ref[...] = col("gv").astype(BF16)
    sr_ref[...] = _silu(col("gr"))
    glr = col("glr")
    g_hi = glr.astype(BF16)
    g_lo = (glr - g_hi.astype(F32)).astype(BF16)
    nqk = GLA_HEADS * GLA_DK
    gw = _bdot(g_hi, wup_ref[...])
    gate = gw[:, :nqk] + gw[:, nqk:] + _bdot(g_lo, wup_ref[:, :nqk]) + bg_ref[...]
    la_ref[...] = _log_sigmoid(gate) * (1.0 / GLA_TAU)


def _in_odd(x2, mod3, gain, w_in, w_up, b_gate, seq, tm):
    t, d = x2.shape
    tps = seq // tm
    ref_w = (512, 512, 512, 512, 256, 256, 512, GLA_RANK, 512)
    starts = [0]
    for wd in ref_w:
        starts.append(starts[-1] + wd)
    seg = lambda j: w_in[:, starts[j]:starts[j + 1]]
    w = jnp.concatenate([seg(0), seg(1), seg(2), seg(3), seg(4), seg(5), seg(6), seg(8),
                         jnp.pad(seg(7), ((0, 0), (0, LANES - GLA_RANK)))], axis=1).astype(BF16)
    wup32 = jnp.pad(w_up.astype(F32), ((0, LANES - GLA_RANK), (0, 0)))
    wup_hi = wup32.astype(BF16)
    wup = jnp.concatenate([wup_hi, (wup32 - wup_hi.astype(F32)).astype(BF16)], axis=1)
    bg = b_gate.reshape(1, -1).astype(F32)
    half = RET_DK // 2
    inv = ROPE_BASE ** (-jnp.arange(half, dtype=F32) / half)
    ang = jnp.arange(seq, dtype=F32)[:, None] * inv[None, :]
    cos = jnp.concatenate([jnp.cos(ang), jnp.cos(ang)], axis=1)
    sin = jnp.concatenate([-jnp.sin(ang), jnp.sin(ang)], axis=1)
    tok = lambda n: pl.BlockSpec((tm, n), lambda i: (i, 0))
    full = lambda a: pl.BlockSpec(a.shape, lambda i: (0,) * a.ndim)
    pos = pl.BlockSpec((tm, RET_DK), lambda i: (i % tps, 0))
    widths = (512, 512, 512, 512, 256, 256, 512, 512, 256)
    dtypes = (BF16, BF16, BF16, F32, F32, F32, BF16, F32, F32)
    return pl.pallas_call(
        _in_odd_kernel,
        grid=(t // tm,),
        in_specs=[tok(d), _mod_spec(0, tps), _mod_spec(1, tps), full(gain), full(w), pos, pos,
                  full(wup), full(bg)],
        out_specs=[tok(n) for n in widths],
        out_shape=[jax.ShapeDtypeStruct((t, n), dt) for n, dt in zip(widths, dtypes)],
        compiler_params=_cparams("parallel"),
        name="in_proj_odd",
    )(x2, mod3, mod3, gain, w, cos, sin, wup, bg)


RET_CHUNK = 256


def _ret_kernel(q_ref, k_ref, v_ref, sg_ref, dm_ref, xi_ref, zeta_ref, gl_ref, y_ref, st_ref):
    @pl.when(pl.program_id(1) == 0)
    def _():
        st_ref[...] = jnp.zeros_like(st_ref)

    nb = q_ref.shape[0]
    chains = [(bl, h) for bl in range(nb) for h in range(RET_HEADS)]
    col = lambda h: slice(h * RET_DK, (h + 1) * RET_DK)
    q = [q_ref[bl, :, col(h)] for bl, h in chains]
    k = [k_ref[bl, :, col(h)] for bl, h in chains]
    v = [v_ref[bl, :, col(h)] for bl, h in chains]
    st = [st_ref[c] for c in range(len(chains))]
    s = [_dot_nt(q[c], k[c]) for c in range(len(chains))]
    inter = [_bdot((q[c].astype(F32) * xi_ref[h]).astype(BF16), st[c].astype(BF16))
             for c, (bl, h) in enumerate(chains)]
    upd = [_dot_tn((k[c].astype(F32) * zeta_ref[h]).astype(BF16), v[c]) for c, (bl, h) in enumerate(chains)]
    for c, (bl, h) in enumerate(chains):
        o = _bdot((s[c] * dm_ref[h]).astype(BF16), v[c]) + inter[c]
        st_ref[c] = gl_ref[h, 0:1, :] * st[c] + upd[c]
        y_ref[bl, :, col(h)] = (sg_ref[bl, :, col(h)] * _rms(o)).astype(y_ref.dtype)


RET_BATCHES = 2


def _retention(rq, rk, rv, sg, bsz, seq):
    t = rq.shape[0]
    ell = min(RET_CHUNK, seq)
    nc = seq // ell
    log_g = jnp.log(1.0 - jnp.exp2(-5.0 - jnp.arange(RET_HEADS, dtype=F32)))
    idx = jnp.arange(ell, dtype=F32)
    rel = idx[:, None] - idx[None, :]
    dmat = jnp.where(rel >= 0, jnp.exp(log_g[:, None, None] * jnp.maximum(rel, 0.0)), 0.0)
    lanes = lambda a: jnp.broadcast_to(a[..., None], a.shape + (RET_DK,))
    xi = lanes(jnp.exp(log_g[:, None] * (idx + 1.0)))
    zeta = lanes(jnp.exp(log_g[:, None] * (ell - 1.0 - idx)))
    gl = jnp.broadcast_to(jnp.exp(log_g * ell)[:, None, None], (RET_HEADS, SUBLANES, RET_DV))
    nb = min(RET_BATCHES, bsz)
    width = RET_HEADS * RET_DK
    r3 = lambda a: a.reshape(bsz, seq, width)
    blk = pl.BlockSpec((nb, ell, width), lambda b, c: (b, c, 0))
    full = lambda a: pl.BlockSpec(a.shape, lambda b, c: (0, 0, 0))
    y = pl.pallas_call(
        _ret_kernel,
        grid=(bsz // nb, nc),
        in_specs=[blk, blk, blk, blk, full(dmat), full(xi), full(zeta), full(gl)],
        out_specs=blk,
        out_shape=jax.ShapeDtypeStruct((bsz, seq, width), BF16),
        scratch_shapes=[pltpu.VMEM((nb * RET_HEADS, RET_DK, RET_DV), F32)],
        compiler_params=_cparams("parallel", "arbitrary"),
        name="retention",
    )(r3(rq), r3(rk), r3(rv), r3(sg), dmat, xi, zeta, gl)
    return y.reshape(t, width)


def _gla_kernel(q_ref, k_ref, la_ref, v_ref, sg_ref, y_ref, st_ref, b_scr, v_scr, p_scr, r_scr):
    @pl.when(pl.program_id(1) == 0)
    def _():
        st_ref[...] = jnp.zeros_like(st_ref)

    ell, sub = GLA_CHUNK, GLA_SUB
    n_sub = ell // sub
    nb = q_ref.shape[0]
    pairs = GLA_HEADS // 2
    streams = [(bl, p) for bl in range(nb) for p in range(pairs)]
    lane = lax.broadcasted_iota(jnp.int32, (1, LANES), 1)
    first = lane < GLA_DK
    head = (first, jnp.logical_not(first))
    pick = lambda h, a: jnp.where(head[h], a, 0.0).astype(BF16)
    tri = _lower_tri(ell)
    tau = lax.broadcasted_iota(jnp.int32, (sub, LANES), 0)
    row_of, per_sub = [], 0
    for s_ in range(sub):
        row_of.append(per_sub)
        per_sub += sub - (s_ // SUBLANES) * SUBLANES
    rsub = lax.broadcasted_iota(jnp.int32, (LANES, 2 * LANES), 0)
    csub = lax.broadcasted_iota(jnp.int32, (LANES, 2 * LANES), 1)
    ind = ((rsub < GLA_DK) == (csub < LANES)).astype(BF16)

    val = {}
    for sid, (bl, p) in enumerate(streams):
        qk = slice(p * LANES, (p + 1) * LANES)
        q, k = q_ref[bl, :, qk], k_ref[bl, :, qk]
        b = _dot01(tri, la_ref[bl, :, qk])
        b_scr[sid] = b
        v_bf = v_ref[bl, :, p * 2 * GLA_DV:(p + 1) * 2 * GLA_DV]
        v_scr[sid] = v_bf.astype(F32)
        st = st_ref[sid]
        val[sid] = dict(q=q, k=k, b=b, st=st, vh=[v_bf[:, h * GLA_DV:(h + 1) * GLA_DV] for h in range(2)])

    for sid in val:
        d = val[sid]
        qe = d["q"] * jnp.exp(d["b"])
        st_bf = d["st"].astype(BF16)
        d["o"] = [_dot_nt(pick(h, qe), st_bf) for h in range(2)]

    row = lax.broadcasted_iota(jnp.int32, (ell, LANES), 0)
    for sid in val:
        d = val[sid]
        q, k, b = d["q"], d["k"], d["b"]
        qa, ka = [], []
        for i in range(1, n_sub):
            lo = i * sub
            ref_row = b[lo - 1:lo, :]
            in_i = (row >= lo) & (row < lo + sub)
            qa.append(jnp.where(in_i, q * jnp.exp(jnp.minimum(b - ref_row, 0.0)), 0.0))
            ka.append(jnp.where(row < lo, k * jnp.exp(jnp.minimum(ref_row - b, 0.0)), 0.0))
        k_cat = jnp.concatenate(ka, axis=1).astype(BF16)
        d["a_off"] = [_dot_nt(jnp.concatenate([pick(h, x) for x in qa], axis=1), k_cat) for h in range(2)]
    for sid in val:
        d = val[sid]
        d["off"] = [_bdot(d["a_off"][h].astype(BF16), d["vh"][h]) for h in range(2)]

    for sid, (bl, p) in enumerate(streams):
        d = val[sid]
        q, b = d["q"], d["b"]
        for i in range(n_sub):
            lo = i * sub
            qi, bi = q[lo:lo + sub], b[lo:lo + sub]
            for s in range(sub):
                k_row = k_ref[bl, pl.ds(lo + s, 1), p * LANES:(p + 1) * LANES]
                b_row = b_scr[sid, pl.ds(lo + s, 1), :]
                r0 = (s // SUBLANES) * SUBLANES
                w = jnp.exp(jnp.minimum(bi[r0:] - b_row, 0.0))
                tile_s = jnp.where(tau[r0:] >= s, qi[r0:] * k_row * w, 0.0)
                p_scr[sid, pl.ds(i * per_sub + row_of[s], sub - r0), :] = tile_s
    for sid in val:
        r_scr[sid] = _bdot(p_scr[sid].astype(BF16), ind)
    for sid in val:
        diag = [[], []]
        for i in range(n_sub):
            lo = i * sub
            for h in range(2):
                acc = [jnp.zeros((SUBLANES, GLA_DV), F32) for _ in range(sub // SUBLANES)]
                for s in range(sub):
                    v_row = v_scr[sid, pl.ds(lo + s, 1), h * GLA_DV:(h + 1) * GLA_DV]
                    for part in range(s // SUBLANES, sub // SUBLANES):
                        rows = pl.ds(i * per_sub + row_of[s] + (part - s // SUBLANES) * SUBLANES, SUBLANES)
                        acc[part] = acc[part] + r_scr[sid, rows, h * LANES:(h + 1) * LANES] * v_row
                diag[h].append(jnp.concatenate(acc, axis=0))
        val[sid]["diag"] = diag

    for sid, (bl, p) in enumerate(streams):
        d = val[sid]
        b_last = d["b"][ell - 1:ell, :]
        kh = (d["k"] * jnp.exp(b_last - d["b"])).astype(BF16)
        upd = [_dot_tn(d["vh"][h], kh) for h in range(2)]
        st_ref[sid] = d["st"] * jnp.exp(b_last) + jnp.where(first, upd[0], upd[1])
        for h in range(2):
            oh = d["o"][h] + d["off"][h] + jnp.concatenate(d["diag"][h], axis=0)
            cols = slice((2 * p + h) * GLA_DV, (2 * p + h + 1) * GLA_DV)
            y_ref[bl, :, cols] = (sg_ref[bl, :, cols] * _rms(oh)).astype(y_ref.dtype)


GLA_BATCHES = 4


def _gla(gq, gk, la, gv, sr, bsz, seq):
    t = gq.shape[0]
    ell = GLA_CHUNK
    nc = seq // ell
    nb = min(GLA_BATCHES, bsz)
    ns = nb * (GLA_HEADS // 2)
    prod_rows = (ell // GLA_SUB) * sum(GLA_SUB - (s // SUBLANES) * SUBLANES for s in range(GLA_SUB))
    r3 = lambda a: a.reshape(bsz, seq, a.shape[1])
    spec = lambda w: pl.BlockSpec((nb, ell, w), lambda b, c: (b, c, 0))
    wq, wv = GLA_HEADS * GLA_DK, GLA_HEADS * GLA_DV
    y = pl.pallas_call(
        _gla_kernel,
        grid=(bsz // nb, nc),
        in_specs=[spec(wq), spec(wq), spec(wq), spec(wv), spec(wv)],
        out_specs=spec(wv),
        out_shape=jax.ShapeDtypeStruct((bsz, seq, wv), BF16),
        scratch_shapes=[pltpu.VMEM((ns, GLA_DV, LANES), F32),
                        pltpu.VMEM((ns, ell, LANES), F32),
                        pltpu.VMEM((ns, ell, 2 * GLA_DV), F32),
                        pltpu.VMEM((ns, prod_rows, LANES), F32),
                        pltpu.VMEM((ns, prod_rows, 2 * LANES), F32)],
        compiler_params=_cparams("parallel", "arbitrary"),
        name="gla",
    )(r3(gq), r3(gk), r3(la), r3(gv), r3(sr))
    return y.reshape(t, wv)


ROUTE_IDX, ROUTE_GATE, ROUTE_RANK = 0, SUBLANES, 2 * SUBLANES


def _post_tail(m, x_ref, g1_ref, gpost_ref, sh2_ref, sc2_ref, gpre_ref, rw_ref, rb_ref,
               x1_ref, h2_ref, route_ref, plan_ref, cnt_ref, carry_ref):
    i = pl.program_id(0)
    tm = m.shape[0]
    x1 = x_ref[...] + g1_ref[0] * (_rms(m) * gpost_ref[...])
    x1_ref[...] = x1
    h2 = _rms(x1) * (gpre_ref[...] * (1.0 + sc2_ref[0])) + sh2_ref[0]
    for c in range(D_MODEL // LANES):
        h2_ref[pl.ds(c, tm, stride=SUBLANES), :] = h2[:, c * LANES:(c + 1) * LANES]

    ne = N_EXPERTS
    h_hi = h2.astype(BF16)
    h_lo = (h2 - h_hi.astype(F32)).astype(BF16)
    w_both = rw_ref[...]
    hw = _dot_nt(w_both, h_hi)
    bias = jnp.concatenate([rb_ref[...]] * (tm // LANES), axis=1)
    work = hw[:ne] + hw[ne:] + _dot_nt(w_both[:ne], h_lo) + bias
    esub = lax.broadcasted_iota(jnp.int32, (ne, tm), 0).astype(F32)
    onehot = jnp.zeros((ne, tm), F32)
    hits, vals, idxs = [], [], []
    for k in range(TOP_K):
        mx = jnp.max(work, axis=0, keepdims=True)
        idx = jnp.min(jnp.where(work == mx, esub, float(ne)), axis=0, keepdims=True)
        hit = esub == idx
        hits.append(hit)
        vals.append(mx)
        idxs.append(idx)
        onehot = onehot + hit.astype(F32)
        work = jnp.where(hit, -jnp.inf, work)
    es = [jnp.exp(v - vals[0]) for v in vals]
    inv = 1.0 / (es[0] + es[1] + es[2] + es[3])
    gates = [e * inv for e in es]

    @pl.when(i == 0)
    def _():
        carry_ref[...] = jnp.zeros_like(carry_ref)

    r = lax.broadcasted_iota(jnp.int32, (tm, tm), 0)
    c = lax.broadcasted_iota(jnp.int32, (tm, tm), 1)
    carry = carry_ref[...]
    before = _bdot(onehot.astype(BF16), (r < c).astype(BF16)) + jnp.concatenate([carry[:ne]] * (tm // LANES), axis=1)
    ranks = [jnp.sum(jnp.where(hits[k], before, 0.0), axis=0, keepdims=True) for k in range(TOP_K)]
    carry = carry + jnp.concatenate(
        [jnp.broadcast_to(jnp.sum(onehot, axis=1, keepdims=True), (ne, LANES)),
         jnp.zeros((LANES - ne, LANES), F32)], axis=0)
    carry_ref[...] = carry
    cnt_ref[...] = carry.T[:SUBLANES]

    sub = lax.broadcasted_iota(jnp.int32, (SUBLANES, tm), 0)

    def rows(vs):
        out = jnp.zeros((SUBLANES, tm), F32)
        for k, v in enumerate(vs):
            out = jnp.where(sub == k, jnp.broadcast_to(v, (SUBLANES, tm)), out)
        return out

    rec = jnp.concatenate([rows(idxs), rows(gates), rows(ranks), jnp.zeros((LANES - 3 * SUBLANES, tm), F32)], axis=0)
    route_ref[...] = rec.T
    plan_ref[...] = rows(idxs + ranks)


def _out_even_kernel(ys_ref, u_ref, yb_ref, d_ref, gw_ref, gb_ref, wa_ref, wb_ref, *rest):
    y = ys_ref[...] + d_ref[...] * u_ref[...]
    g = jax.nn.gelu(y)
    ya = g * _sigmoid(_bdot(g.astype(BF16), gw_ref[...]) + gb_ref[...])
    m = _bdot(ya.astype(BF16), wa_ref[...]) + _bdot(yb_ref[...], wb_ref[...])
    _post_tail(m, *rest)


def _out_odd_kernel(yc_ref, yd_ref, wa_ref, wb_ref, *rest):
    m = _bdot(yc_ref[...], wa_ref[...]) + _bdot(yd_ref[...], wb_ref[...])
    _post_tail(m, *rest)


def _mixer_out(body, mix_args, mix_specs, x2, mod3, g_post, g_pre, router_w, router_b, seq, tm):
    t, d = x2.shape
    tps = seq // tm
    rw32 = router_w.astype(F32).T
    rw_hi = rw32.astype(BF16)
    rw = jnp.concatenate([rw_hi, (rw32 - rw_hi.astype(F32)).astype(BF16)], axis=0)
    rb = jnp.broadcast_to(router_b.astype(F32)[:, None], (N_EXPERTS, LANES))
    tok = lambda n: pl.BlockSpec((tm, n), lambda i: (i, 0))
    full = lambda a: pl.BlockSpec(a.shape, lambda i: (0,) * a.ndim)
    tail_args = [x2, mod3, g_post, mod3, mod3, g_pre, rw, rb]
    tail_specs = [tok(d), _mod_spec(2, tps), full(g_post), _mod_spec(3, tps), _mod_spec(4, tps),
                  full(g_pre), full(rw), full(rb)]
    return pl.pallas_call(
        body,
        grid=(t // tm,),
        in_specs=mix_specs + tail_specs,
        out_specs=[tok(d), pl.BlockSpec((tm * SUBLANES, LANES), lambda i: (i, 0)),
                   tok(LANES), pl.BlockSpec((SUBLANES, tm), lambda i: (0, i)),
                   pl.BlockSpec((SUBLANES, LANES), lambda i: (0, 0))],
        out_shape=[jax.ShapeDtypeStruct((t, d), F32),
                   jax.ShapeDtypeStruct((t * SUBLANES, LANES), F32),
                   jax.ShapeDtypeStruct((t, LANES), F32),
                   jax.ShapeDtypeStruct((SUBLANES, t), F32),
                   jax.ShapeDtypeStruct((SUBLANES, LANES), F32)],
        scratch_shapes=[pltpu.VMEM((LANES, LANES), F32)],
        compiler_params=_cparams("arbitrary"),
        name="mixer_out_router",
    )(*mix_args, *tail_args)


def _out_even(ys, u, yb, d_skip, glu_w, glu_b, w_out, *tail, seq, tm):
    tok = lambda n: pl.BlockSpec((tm, n), lambda i: (i, 0))
    full = lambda a: pl.BlockSpec(a.shape, lambda i: (0,) * a.ndim)
    args = [ys, u, yb, d_skip.reshape(1, -1), glu_w.astype(BF16), glu_b.reshape(1, -1),
            w_out[:S5_CH].astype(BF16), w_out[S5_CH:].astype(BF16)]
    specs = [tok(S5_CH), tok(S5_CH), tok(FOX_W)] + [full(a) for a in args[3:]]
    return _mixer_out(_out_even_kernel, args, specs, *tail, seq, tm)


def _out_odd(yc, yd, w_out, *tail, seq, tm):
    tok = lambda n: pl.BlockSpec((tm, n), lambda i: (i, 0))
    full = lambda a: pl.BlockSpec(a.shape, lambda i: (0,) * a.ndim)
    nc = yc.shape[1]
    args = [yc, yd, w_out[:nc].astype(BF16), w_out[nc:].astype(BF16)]
    specs = [tok(nc), tok(yd.shape[1])] + [full(a) for a in args[2:]]
    return _mixer_out(_out_odd_kernel, args, specs, *tail, seq, tm)


def _route_kernel(plan_ref, cnt_ref, dest_ref, blk_ref, meta_ref):
    tm = plan_ref.shape[1]
    cnt = cnt_ref[...]
    padded = jnp.floor((cnt + (MOE_BLOCK - 1.0)) * (1.0 / MOE_BLOCK)) * MOE_BLOCK
    r = lax.broadcasted_iota(jnp.int32, (LANES, LANES), 0)
    c = lax.broadcasted_iota(jnp.int32, (LANES, LANES), 1)
    hi, mid, lo = _split3(padded)
    incl = (r <= c).astype(BF16)
    pad_end = _bdot(hi, incl) + _bdot(mid, incl) + _bdot(lo, incl)
    pad_start = pad_end - padded
    start_col = jnp.broadcast_to(pad_start[0:1, :], (LANES, LANES)).T[:N_EXPERTS]
    start = jnp.concatenate([start_col] * (tm // LANES), axis=1)
    esub = lax.broadcasted_iota(jnp.int32, (N_EXPERTS, tm), 0).astype(F32)
    plan = plan_ref[...]
    sub = lax.broadcasted_iota(jnp.int32, (SUBLANES, tm), 0)
    dest = jnp.zeros((SUBLANES, tm), F32)
    for k in range(TOP_K):
        st = jnp.sum(jnp.where(esub == plan[k:k + 1, :], start, 0.0), axis=0, keepdims=True)
        dest = jnp.where(sub == k, jnp.broadcast_to(st + plan[TOP_K + k:TOP_K + k + 1, :], (SUBLANES, tm)), dest)
    dest_ref[...] = dest.astype(jnp.int32)

    nb = blk_ref.shape[1]
    end_col = jnp.sum(jnp.where(r == c, jnp.broadcast_to(pad_end[0:1, :], (LANES, LANES)), 0.0),
                      axis=-1, keepdims=True)
    jpos = lax.broadcasted_iota(jnp.int32, (LANES, nb), 1).astype(F32) * MOE_BLOCK
    esub = lax.broadcasted_iota(jnp.int32, (LANES, nb), 0)
    done = jnp.where((end_col <= jpos) & (esub < N_EXPERTS), 1.0, 0.0)
    be = jnp.minimum(jnp.sum(done, axis=0, keepdims=True), N_EXPERTS - 1.0)
    blk_ref[...] = jnp.broadcast_to(be, blk_ref.shape).astype(jnp.int32)
    lane1 = lax.broadcasted_iota(jnp.int32, (SUBLANES, LANES), 1)
    n_valid = jnp.sum(jnp.where(lane1 == N_EXPERTS - 1, pad_end, 0.0), axis=-1, keepdims=True) * (1.0 / MOE_BLOCK)
    sub1 = lax.broadcasted_iota(jnp.int32, (SUBLANES, LANES), 0)
    meta = jnp.where(sub1 == 0, pad_start + cnt, jnp.where(sub1 == 1, pad_end, jnp.broadcast_to(n_valid, (SUBLANES, LANES))))
    meta_ref[...] = meta.astype(jnp.int32)


def _route(plan, cnt, n_blocks, tm):
    t = plan.shape[1]
    nb_pad = -(-n_blocks // LANES) * LANES
    tok = pl.BlockSpec((SUBLANES, tm), lambda i: (0, i))
    fix = lambda n: pl.BlockSpec((SUBLANES, n), lambda i: (0, 0))
    return pl.pallas_call(
        _route_kernel,
        grid=(t // tm,),
        in_specs=[tok, fix(LANES)],
        out_specs=[tok, fix(nb_pad), fix(LANES)],
        out_shape=[jax.ShapeDtypeStruct((SUBLANES, t), jnp.int32),
                   jax.ShapeDtypeStruct((SUBLANES, nb_pad), jnp.int32),
                   jax.ShapeDtypeStruct((SUBLANES, LANES), jnp.int32)],
        compiler_params=_cparams("arbitrary"),
        name="route_plan",
    )(plan, cnt)


def _dispatch_kernel(pad_ref, dest_ref, h_ref, xb_ref, zero_ref, sem_z, sem_s):
    i = pl.program_id(0)
    tm = h_ref.shape[0]

    @pl.when(i == 0)
    def _():
        zero_ref[...] = jnp.zeros_like(zero_ref)
        sizes = [1 << b for b in range(int(math.log2(MOE_BLOCK)) - 1, -1, -1)]

        def fill(e, carry, do_wait):
            start = pad_ref[0, e]
            n_pad = pad_ref[1, e] - start
            off = start
            for sz in sizes:
                take = (n_pad & sz) != 0
                cp = pltpu.make_async_copy(zero_ref.at[pl.ds(0, sz)], xb_ref.at[pl.ds(off, sz)], sem_z)

                @pl.when(take)
                def _():
                    if do_wait:
                        cp.wait()
                    else:
                        cp.start()
                off = off + jnp.where(take, sz, 0)
            return carry

        half = zero_ref.shape[0]

        def fill_unused(j, carry, do_wait):
            for part in range(MOE_BLOCK // half):
                cp = pltpu.make_async_copy(zero_ref, xb_ref.at[pl.ds(j * MOE_BLOCK + part * half, half)], sem_z)
                if do_wait:
                    cp.wait()
                else:
                    cp.start()
            return carry

        n_blocks = xb_ref.shape[0] // MOE_BLOCK
        lax.fori_loop(0, N_EXPERTS, lambda e, c: fill(e, c, False), 0)
        lax.fori_loop(pad_ref[2, 0], n_blocks, lambda j, c: fill_unused(j, c, False), 0)
        lax.fori_loop(0, N_EXPERTS, lambda e, c: fill(e, c, True), 0)
        lax.fori_loop(pad_ref[2, 0], n_blocks, lambda j, c: fill_unused(j, c, True), 0)

    def issue(r, carry):
        for k in range(TOP_K):
            pltpu.make_async_copy(h_ref.at[r], xb_ref.at[dest_ref[k, r]], sem_s).start(priority=k % 2)
        return carry

    lax.fori_loop(0, tm, issue, 0, unroll=ROW_DMA_UNROLL)
    for k in range(TOP_K):
        pltpu.make_async_copy(h_ref, xb_ref.at[pl.ds(0, tm)], sem_s).wait()


def _dispatch(h2t, dest, meta, n_slots, tm):
    t = h2t.shape[0] // SUBLANES
    h3 = h2t.reshape(t, SUBLANES, LANES)
    return pl.pallas_call(
        _dispatch_kernel,
        grid_spec=pltpu.PrefetchScalarGridSpec(
            num_scalar_prefetch=1,
            grid=(t // tm,),
            in_specs=[pl.BlockSpec((SUBLANES, tm), lambda i, p: (0, i), memory_space=pltpu.SMEM),
                      pl.BlockSpec((tm, SUBLANES, LANES), lambda i, p: (i, 0, 0))],
            out_specs=pl.BlockSpec(memory_space=pl.ANY),
            scratch_shapes=[pltpu.VMEM((MOE_BLOCK // 2, SUBLANES, LANES), F32),
                            pltpu.SemaphoreType.DMA, pltpu.SemaphoreType.DMA]),
        out_shape=jax.ShapeDtypeStruct((n_slots, SUBLANES, LANES), F32),
        compiler_params=_cparams("arbitrary"),
        name="moe_dispatch",
    )(meta[:3, :N_EXPERTS], dest, h3)


def _expert_kernel(be_ref, nv_ref, nxt_ref, x_ref, wgu_hbm, bgu_ref, wd_hbm, bd_ref, y_ref,
                   wgu_f32, wd_f32, wgu_bf, wd_bf, sem, *, layer):
    j = pl.program_id(0)
    valid = j < nv_ref[0]
    first = valid & ((j == 0) | (be_ref[j] != be_ref[jnp.maximum(j - 1, 0)]))

    def weight_copies(e):
        return (pltpu.make_async_copy(wgu_hbm.at[layer, e], wgu_f32, sem.at[0]),
                pltpu.make_async_copy(wd_hbm.at[layer, e], wd_f32, sem.at[1]))

    @pl.when(j == 0)
    def _():
        for cp in weight_copies(be_ref[0]):
            cp.start()

    @pl.when(first)
    def _():
        for cp in weight_copies(be_ref[j]):
            cp.wait()
        wgu_bf[...] = wgu_f32[...].astype(BF16)
        wd_bf[...] = wd_f32[...].astype(BF16)

        @pl.when(nxt_ref[j] >= 0)
        def _():
            for cp in weight_copies(nxt_ref[j]):
                cp.start()

    @pl.when(valid)
    def _():
        x = jnp.concatenate([x_ref[pl.ds(c, MOE_BLOCK, stride=SUBLANES), :] for c in range(D_MODEL // LANES)],
                            axis=1).astype(BF16)
        gu = _bdot(x, wgu_bf[...]) + bgu_ref[0]
        x_glu = jnp.minimum(gu[:, :D_EXPERT], SWIGLU_LIMIT)
        x_lin = jnp.clip(gu[:, D_EXPERT:], -SWIGLU_LIMIT, SWIGLU_LIMIT)
        act = x_glu * _sigmoid(SWIGLU_ALPHA * x_glu) * (x_lin + 1.0)
        y = _bdot(act.astype(BF16), wd_bf[...]) + bd_ref[0]
        for c in range(D_MODEL // LANES):
            y_ref[pl.ds(c, MOE_BLOCK, stride=SUBLANES), :] = y[:, c * LANES:(c + 1) * LANES]

    @pl.when(jnp.logical_not(valid))
    def _():
        y_ref[...] = jnp.zeros_like(y_ref)


def _experts(xb, block_expert, n_valid, w_gu, b_gu, w_down, b_down, layer):
    n_slots = xb.shape[0]
    n_blocks = n_slots // MOE_BLOCK
    rows = MOE_BLOCK * SUBLANES
    x2 = xb.reshape(n_slots * SUBLANES, LANES)
    depth, ne, d, de2 = w_gu.shape
    idx = jnp.arange(n_blocks, dtype=jnp.int32)
    is_first = ((idx == 0) | (block_expert != jnp.roll(block_expert, 1))) & (idx < n_valid[0])
    first_at = lax.cummin(jnp.where(is_first, idx, n_blocks)[::-1])[::-1]
    next_first = jnp.concatenate([first_at[1:], jnp.full((1,), n_blocks, jnp.int32)])
    nxt = jnp.where(next_first < n_blocks, block_expert[jnp.minimum(next_first, n_blocks - 1)], -1)
    last = lambda j, be, nv, nx: jnp.minimum(j, nv[0] - 1)
    bmap = lambda j, be, nv, nx: (layer, be[last(j, be, nv, nx)], 0, 0)
    return pl.pallas_call(
        functools.partial(_expert_kernel, layer=layer),
        grid_spec=pltpu.PrefetchScalarGridSpec(
            num_scalar_prefetch=3,
            grid=(n_blocks,),
            in_specs=[pl.BlockSpec((rows, LANES), lambda j, be, nv, nx: (last(j, be, nv, nx), 0)),
                      pl.BlockSpec(memory_space=pl.ANY),
                      pl.BlockSpec((None, 1, 1, de2), bmap),
                      pl.BlockSpec(memory_space=pl.ANY),
                      pl.BlockSpec((None, 1, 1, d), bmap)],
            out_specs=pl.BlockSpec((rows, LANES), lambda j, be, nv, nx: (j, 0)),
            scratch_shapes=[pltpu.VMEM((d, de2), F32), pltpu.VMEM((de2 // 2, d), F32),
                            pltpu.VMEM((d, de2), BF16), pltpu.VMEM((de2 // 2, d), BF16),
                            pltpu.SemaphoreType.DMA((2,))]),
        out_shape=jax.ShapeDtypeStruct((n_slots * SUBLANES, LANES), F32),
        compiler_params=_cparams("arbitrary"),
        name="moe_experts",
    )(block_expert, n_valid, nxt.astype(jnp.int32), x2, w_gu, b_gu.reshape(depth, ne, 1, de2), w_down,
      b_down.reshape(depth, ne, 1, d))


def _combine_kernel(dest_ref, dest_next_ref, yb_ref, gate_ref, x1_ref, g2_ref, gpost_ref, o_ref, buf, sem):
    i = pl.program_id(0)
    tm = x1_ref.shape[0]
    slot = i % 2

    def gather(idx_ref, into):
        def issue(r, carry):
            for k in range(TOP_K):
                src = pl.multiple_of(idx_ref[k, r] * SUBLANES, SUBLANES)
                dst = pl.multiple_of((k * tm + r) * SUBLANES, SUBLANES)
                pltpu.make_async_copy(yb_ref.at[pl.ds(src, SUBLANES), :], buf.at[into, pl.ds(dst, SUBLANES), :],
                                      sem.at[into]).start(priority=k % 2)
            return carry
        lax.fori_loop(0, tm, issue, 0, unroll=ROW_DMA_UNROLL)

    @pl.when(i == 0)
    def _():
        gather(dest_ref, 0)

    @pl.when(i + 1 < pl.num_programs(0))
    def _():
        gather(dest_next_ref, 1 - slot)

    pltpu.make_async_copy(yb_ref.at[pl.ds(0, TOP_K * tm * SUBLANES), :], buf.at[slot], sem.at[slot]).wait()
    gates = gate_ref[...]
    gk = [jnp.broadcast_to(gates[:, ROUTE_GATE + k:ROUTE_GATE + k + 1], (tm, LANES)) for k in range(TOP_K)]
    b2 = buf.at[slot]
    cols = []
    for c in range(D_MODEL // LANES):
        acc = jnp.zeros((tm, LANES), F32)
        for k in range(TOP_K):
            acc = acc + gk[k] * b2[pl.ds(k * tm * SUBLANES + c, tm, stride=SUBLANES), :]
        cols.append(acc)
    f = jnp.concatenate(cols, axis=1)
    o_ref[...] = x1_ref[...] + g2_ref[0] * (_rms(f) * gpost_ref[...])


def _combine(yb, dest, gates, x1, mod3, g_post, seq, tm):
    t, d = x1.shape
    tps = seq // tm
    n = t // tm
    return pl.pallas_call(
        _combine_kernel,
        grid=(n,),
        in_specs=[pl.BlockSpec((SUBLANES, tm), lambda i: (0, i), memory_space=pltpu.SMEM),
                  pl.BlockSpec((SUBLANES, tm), lambda i: (0, jnp.minimum(i + 1, n - 1)), memory_space=pltpu.SMEM),
                  pl.BlockSpec(memory_space=pl.ANY),
                  pl.BlockSpec((tm, LANES), lambda i: (i, 0)),
                  pl.BlockSpec((tm, d), lambda i: (i, 0)),
                  _mod_spec(5, tps),
                  pl.BlockSpec(g_post.shape, lambda i: (0, 0))],
        out_specs=pl.BlockSpec((tm, d), lambda i: (i, 0)),
        out_shape=jax.ShapeDtypeStruct((t, d), F32),
        scratch_shapes=[pltpu.VMEM((2, TOP_K * tm * SUBLANES, LANES), F32), pltpu.SemaphoreType.DMA((2,))],
        compiler_params=_cparams("arbitrary"),
        name="moe_combine",
    )(dest, dest, yb, gates, x1, mod3, g_post)


def _moe(h2t, route, plan, cnt, x1, mod3, g_post, w_gu, b_gu, w_down, b_down, layer, seq):
    t = x1.shape[0]
    n_blocks = t * TOP_K // MOE_BLOCK + N_EXPERTS
    dest, blk, meta = _route(plan, cnt, n_blocks, min(2048, t))
    xb = _dispatch(h2t, dest, meta, n_blocks * MOE_BLOCK, MOE_BLOCK)
    yb = _experts(xb, blk[0, :n_blocks], meta[2, :1], w_gu, b_gu, w_down, b_down, layer)
    return _combine(yb, dest, route, x1, mod3, g_post, seq, MOE_BLOCK)


TOKEN_TILE = 512
OUT_TILE = 512
FOX_Q_TILE = 512
FOX_K_TILE = 256


def kernel(x, c, ada_w, ada_b, norm_pre_mix, norm_post_mix, norm_pre_ffn, norm_post_ffn, ev_w_in, fox_b_f, s5_lam_re, s5_lam_im, s5_log_dt, s5_b_re, s5_b_im, s5_c_re, s5_c_im, s5_d, s5_glu_w, s5_glu_b, ev_w_out, od_w_in, gla_w_up, gla_b_gate, od_w_out, router_w, router_b, exp_w_gu, exp_b_gu, exp_w_down, exp_b_down):
    bsz, seq, d = x.shape
    t = bsz * seq
    tm = min(TOKEN_TILE, seq)
    x2 = x.reshape(t, d)
    mod = _modulation(c, ada_w, ada_b)
    for l in range(DEPTH):
        i = l // 2
        mod3 = mod[l].reshape(bsz, 1, 6 * d)
        row = lambda a: a[l].reshape(1, -1)
        tail = (x2, mod3, row(norm_post_mix), row(norm_pre_ffn), router_w[l], router_b[l])
        if l % 2 == 0:
            u, q, k, v_t = _in_even(x2, mod3, row(norm_pre_mix), ev_w_in[i], fox_b_f[i], seq, tm)
            tables = _s5_tables(s5_lam_re[i], s5_lam_im[i], s5_log_dt[i], s5_b_re[i], s5_b_im[i],
                                s5_c_re[i], s5_c_im[i])
            ys = _s5_scan(u, bsz, seq, tables)
            yb = _fox(q, k, v_t, bsz, seq, min(FOX_Q_TILE, seq), min(FOX_K_TILE, seq))
            outs = _out_even(ys, u, yb, s5_d[i], s5_glu_w[i], s5_glu_b[i], ev_w_out[i], *tail, seq=seq, tm=min(OUT_TILE, seq))
        else:
            rq, rk, rv, sg, gq, gk, gv, sr, la = _in_odd(x2, mod3, row(norm_pre_mix), od_w_in[i],
                                                         gla_w_up[i], gla_b_gate[i], seq, tm)
            yc = _retention(rq, rk, rv, sg, bsz, seq)
            yd = _gla(gq, gk, la, gv, sr, bsz, seq)
            outs = _out_odd(yc, yd, od_w_out[i], *tail, seq=seq, tm=min(OUT_TILE, seq))
        x1, h2t, route, plan, cnt = outs
        x2 = _moe(h2t, route, plan, cnt, x1, mod3, row(norm_post_ffn),
                  exp_w_gu, exp_b_gu, exp_w_down, exp_b_down, l, seq)
    return x2.reshape(bsz, seq, d)
```

```python
import functools
import math

import jax
import jax.numpy as jnp
from jax import lax
from jax.experimental import pallas as pl
from jax.experimental.pallas import tpu as pltpu

F32 = jnp.float32
BF16 = jnp.bfloat16
HIGHEST = lax.Precision.HIGHEST

D_MODEL = 1024
DEPTH = 2
EPS = 1e-6
S5_CH = 512
S5_GROUP = 16
S5_GROUPS = S5_CH // S5_GROUP
S5_STATE = 64
S5_CHUNK = 8
FOX_HEADS = 8
FOX_DH = 64
FOX_W = FOX_HEADS * FOX_DH
LOG2_E = 1.4426950408889634
FOX_VROWS = FOX_DH + 16
FOX_HPS = 8
RET_HEADS = 4
RET_DK = 128
RET_DV = 128
ROPE_BASE = 10000.0
GLA_HEADS = 4
GLA_DK = 64
GLA_DV = 128
GLA_RANK = 16
GLA_TAU = 16.0
GLA_CHUNK = 64
GLA_SUB = 16
N_EXPERTS = 32
TOP_K = 4
D_EXPERT = 1024
SWIGLU_LIMIT = 7.0
SWIGLU_ALPHA = 1.702
MOE_BLOCK = 256
ROW_DMA_UNROLL = 4

LANES = 128
SUBLANES = 8
VMEM_LIMIT = 56 * 1024 * 1024


def _cparams(*sem):
    return pltpu.CompilerParams(dimension_semantics=sem, vmem_limit_bytes=VMEM_LIMIT)


def _bdot(a, b):
    return jnp.dot(a, b, preferred_element_type=F32)


def _dot_nt(a, b):
    return lax.dot_general(a, b, (((1,), (1,)), ((), ())), preferred_element_type=F32)


def _dot_tn(a, b):
    return lax.dot_general(a, b, (((0,), (0,)), ((), ())), preferred_element_type=F32)


def _split3(x):
    hi = x.astype(BF16)
    r = x - hi.astype(F32)
    mid = r.astype(BF16)
    lo = (r - mid.astype(F32)).astype(BF16)
    return hi, mid, lo


def _dot01(m01, x):
    hi, mid, lo = _split3(x)
    return _bdot(m01, hi) + _bdot(m01, mid) + _bdot(m01, lo)


def _lower_tri(n, strict=False):
    r = lax.broadcasted_iota(jnp.int32, (n, n), 0)
    c = lax.broadcasted_iota(jnp.int32, (n, n), 1)
    return ((r > c) if strict else (r >= c)).astype(BF16)


def _log_sigmoid(x):
    return jnp.minimum(x, 0.0) - jnp.log1p(jnp.exp(-jnp.abs(x)))


def _sigmoid(x):
    return 1.0 / (1.0 + jnp.exp(-x))


def _silu(x):
    return x * _sigmoid(x)


def _rms(x):
    return x * lax.rsqrt(jnp.mean(x * x, axis=-1, keepdims=True) + EPS)


def _mod_kernel(c_ref, w_ref, b_ref, o_ref):
    s = _silu(c_ref[...])
    w = w_ref[0]
    s_hi, w_hi = s.astype(BF16), w.astype(BF16)
    s_lo = (s - s_hi.astype(F32)).astype(BF16)
    w_lo = (w - w_hi.astype(F32)).astype(BF16)
    o_ref[0] = _bdot(s_hi, w_hi) + _bdot(s_lo, w_hi) + _bdot(s_hi, w_lo) + b_ref[0]


def _modulation(c, ada_w, ada_b):
    depth, d, n = ada_w.shape
    bsz = c.shape[0]
    tn = D_MODEL
    return pl.pallas_call(
        _mod_kernel,
        grid=(depth, n // tn),
        in_specs=[pl.BlockSpec((bsz, d), lambda l, j: (0, 0)),
                  pl.BlockSpec((1, d, tn), lambda l, j: (l, 0, j)),
                  pl.BlockSpec((1, 1, tn), lambda l, j: (l, 0, j))],
        out_specs=pl.BlockSpec((1, bsz, tn), lambda l, j: (l, 0, j)),
        out_shape=jax.ShapeDtypeStruct((depth, bsz, n), F32),
        compiler_params=_cparams("parallel", "parallel"),
        name="adaln_mod",
    )(c, ada_w, ada_b.reshape(depth, 1, n))


def _mod_spec(chunk, tiles_per_seq):
    return pl.BlockSpec((1, 1, D_MODEL), lambda i: (i // tiles_per_seq, 0, chunk))


def _prenorm(x, g_ref, sc_ref, sh_ref):
    return _rms(x) * (g_ref[...] * (1.0 + sc_ref[0])) + sh_ref[0]


def _in_even_kernel(x_ref, sh_ref, sc_ref, g_ref, w_ref, bf_ref,
                    u_ref, q_ref, k_ref, vt_ref, carry_ref, *, tiles_per_seq):
    i = pl.program_id(0)
    tm = x_ref.shape[0]
    h = _prenorm(x_ref[...], g_ref, sc_ref, sh_ref)
    z = _bdot(h.astype(BF16), w_ref[...])
    u_ref[...] = z[:, 0:S5_CH]
    tail = jnp.concatenate([jnp.ones((1, tm), F32), jnp.zeros((FOX_VROWS - FOX_DH - 1, tm), F32)], axis=0)
    for pr in range(FOX_HEADS // 2):
        c0 = S5_CH + 2 * FOX_W + pr * LANES
        v_pair = z[:, c0:c0 + LANES].T
        for hh in range(2):
            r0 = (2 * pr + hh) * FOX_VROWS
            vt_ref[r0:r0 + FOX_VROWS, :] = jnp.concatenate(
                [v_pair[hh * FOX_DH:(hh + 1) * FOX_DH], tail], axis=0).astype(BF16)
    ls = _log_sigmoid(z[:, S5_CH + 3 * FOX_W:] + bf_ref[...])

    @pl.when(i % tiles_per_seq == 0)
    def _():
        carry_ref[...] = jnp.zeros_like(carry_ref)

    cum = _dot01(_lower_tri(tm), ls) + carry_ref[...]
    carry_ref[...] = cum[tm - 1:tm, :]

    lane = lax.broadcasted_iota(jnp.int32, (1, LANES), 1)
    feat = lane < FOX_DH
    ones = jnp.where(lane < FOX_DH + 3, 1.0, 0.0)
    for hd in range(FOX_HEADS):
        blk = (hd * FOX_DH) // LANES * LANES
        qs = z[:, S5_CH + blk:S5_CH + blk + LANES] * (FOX_DH ** -0.5 * LOG2_E)
        ks = z[:, S5_CH + FOX_W + blk:S5_CH + FOX_W + blk + LANES]
        if (hd * FOX_DH) % LANES:
            qs = pltpu.roll(qs, LANES - FOX_DH, 1)
            ks = pltpu.roll(ks, LANES - FOX_DH, 1)
        nf = jnp.broadcast_to(-LOG2_E * cum[:, hd:hd + 1], (tm, LANES))
        hi = nf.astype(BF16).astype(F32)
        mid = (nf - hi).astype(BF16).astype(F32)
        lo = nf - hi - mid
        bias = jnp.where(lane == FOX_DH, hi, jnp.where(lane == FOX_DH + 1, mid,
                                                       jnp.where(lane == FOX_DH + 2, lo, 0.0)))
        q_ref[:, hd * LANES:(hd + 1) * LANES] = jnp.where(feat, qs, ones).astype(BF16)
        k_ref[:, hd * LANES:(hd + 1) * LANES] = jnp.where(feat, ks, bias).astype(BF16)


def _in_even(x2, mod3, gain, w_in, b_f, seq, tm):
    t, d = x2.shape
    tiles_per_seq = seq // tm
    nw = S5_CH + 3 * FOX_W
    w = jnp.concatenate([w_in[:, :nw], jnp.pad(w_in[:, nw:], ((0, 0), (0, LANES - FOX_HEADS)))],
                        axis=1).astype(BF16)
    bf = jnp.pad(b_f, (0, LANES - FOX_HEADS)).reshape(1, LANES)
    tok = lambda n: pl.BlockSpec((tm, n), lambda i: (i, 0))
    full = lambda a: pl.BlockSpec(a.shape, lambda i: (0,) * a.ndim)
    return pl.pallas_call(
        functools.partial(_in_even_kernel, tiles_per_seq=tiles_per_seq),
        grid=(t // tm,),
        in_specs=[tok(d), _mod_spec(0, tiles_per_seq), _mod_spec(1, tiles_per_seq),
                  full(gain), full(w), full(bf)],
        out_specs=[tok(S5_CH), tok(FOX_HEADS * LANES), tok(FOX_HEADS * LANES),
                   pl.BlockSpec((FOX_HEADS * FOX_VROWS, tm), lambda i: (i // tiles_per_seq, i % tiles_per_seq))],
        out_shape=[jax.ShapeDtypeStruct((t, S5_CH), F32),
                   jax.ShapeDtypeStruct((t, FOX_HEADS * LANES), BF16),
                   jax.ShapeDtypeStruct((t, FOX_HEADS * LANES), BF16),
                   jax.ShapeDtypeStruct((t // seq * FOX_HEADS * FOX_VROWS, seq), BF16)],
        scratch_shapes=[pltpu.VMEM((1, LANES), F32)],
        compiler_params=_cparams("arbitrary"),
        name="in_proj_even",
    )(x2, mod3, mod3, gain, w, bf)


S5_TILE_GROUPS = LANES // S5_GROUP
S5_SEQ_PARTS = 4


def _s5_kernel(u_ref, wt_ref, ws_ref, wc_ref, a_ref, y_ref, x_scr, e_scr, hp_scr, h_scr, *, nb, ncl):
    ell = S5_CHUNK
    sw = S5_TILE_GROUPS * 2 * S5_STATE

    @pl.when(pl.program_id(1) == 0)
    def _():
        h_scr[...] = jnp.zeros_like(h_scr)

    blk = nb * SUBLANES
    for hi in range(ncl // SUBLANES):
        for b in range(nb):
            for t in range(ell):
                x_scr[hi * blk + b * SUBLANES:hi * blk + (b + 1) * SUBLANES, t * LANES:(t + 1) * LANES] = (
                    u_ref[b, pl.ds(hi * SUBLANES * ell + t, SUBLANES, stride=ell), :])
    x = x_scr[...].astype(BF16)
    e = _bdot(x, ws_ref[0])
    tg = S5_TILE_GROUPS
    for j in range(tg):
        e_scr[j] = e[:, j * LANES:(j + 1) * LANES]
        e_scr[tg + j] = pltpu.roll(e[:, j * LANES:(j + 1) * LANES], S5_STATE, 1)
    a1 = jnp.broadcast_to(a_ref[0, 0:1, :], (nb, sw))
    a2 = jnp.broadcast_to(a_ref[0, 1:2, :], (nb, sw))
    a2s = jnp.broadcast_to(a_ref[0, 2:3, :], (nb, sw))

    def body(c, carry):
        h, hs = carry
        rows_c = pl.ds((c // SUBLANES) * blk + c % SUBLANES, nb, stride=SUBLANES)
        for j in range(tg):
            hp_scr[j, rows_c, :] = h[:, j * LANES:(j + 1) * LANES]
        e1 = jnp.concatenate([e_scr[j, rows_c, :] for j in range(tg)], axis=1)
        e2 = jnp.concatenate([e_scr[tg + j, rows_c, :] for j in range(tg)], axis=1)
        return a1 * h + a2 * hs + e1, a1 * hs + a2s * h + e2

    h, hs = lax.fori_loop(0, ncl, body, (h_scr[0], h_scr[1]))
    h_scr[0] = h
    h_scr[1] = hs
    hp = jnp.concatenate([hp_scr[j] for j in range(tg)], axis=1).astype(BF16)
    y = _bdot(x, wt_ref[0]) + _bdot(hp, wc_ref[0])
    for hi in range(ncl // SUBLANES):
        for b in range(nb):
            for t in range(ell):
                y_ref[b, pl.ds(hi * SUBLANES * ell + t, SUBLANES, stride=ell), :] = (
                    y[hi * blk + b * SUBLANES:hi * blk + (b + 1) * SUBLANES, t * LANES:(t + 1) * LANES])


def _s5_tables(lam_re, lam_im, log_dt, b_re, b_im, c_re, c_im):
    ell, p, g = S5_CHUNK, S5_STATE, S5_GROUPS
    lr, li = lam_re.astype(F32), lam_im.astype(F32)
    dt = jnp.exp(log_dt.astype(F32))[:, None]
    mag = jnp.exp(lr * dt)
    a_re, a_im = mag * jnp.cos(li * dt), mag * jnp.sin(li * dt)
    den = lr * lr + li * li
    n_re, n_im = a_re - 1.0, a_im
    z_re = (n_re * lr + n_im * li) / den
    z_im = (n_im * lr - n_re * li) / den
    br, bi = b_re.astype(F32), b_im.astype(F32)
    bb_re = z_re[..., None] * br - z_im[..., None] * bi
    bb_im = z_re[..., None] * bi + z_im[..., None] * br
    j = jnp.arange(ell + 1, dtype=F32)[:, None, None]
    pmag = jnp.exp(j * (lr * dt)[None])
    pr, pi = pmag * jnp.cos(j * (li * dt)[None]), pmag * jnp.sin(j * (li * dt)[None])
    w_re = pr[..., None] * bb_re[None] - pi[..., None] * bb_im[None]
    w_im = pr[..., None] * bb_im[None] + pi[..., None] * bb_re[None]
    cr, ci = c_re.astype(F32), c_im.astype(F32)
    kern = (jnp.einsum('gcp,jgpd->jgcd', cr, w_re[:ell], precision=HIGHEST)
            - jnp.einsum('gcp,jgpd->jgcd', ci, w_im[:ell], precision=HIGHEST))
    s_idx = jnp.arange(ell)[:, None]
    t_idx = jnp.arange(ell)[None, :]
    lag = jnp.clip(t_idx - s_idx, 0, ell - 1)
    toep = kern[lag]
    toep = jnp.where((t_idx >= s_idx)[:, :, None, None, None], toep, 0.0)
    toep = toep.transpose(2, 0, 4, 1, 3)
    rev = jnp.arange(ell - 1, -1, -1)
    st_re = w_re[rev].transpose(1, 0, 3, 2)
    st_im = w_im[rev].transpose(1, 0, 3, 2)
    wst = jnp.concatenate([st_re, st_im], -1)
    p1r, p1i = pr[1:], pi[1:]
    c_hr = cr[None] * p1r[:, :, None, :] - ci[None] * p1i[:, :, None, :]
    c_hi = -cr[None] * p1i[:, :, None, :] - ci[None] * p1r[:, :, None, :]
    cst = jnp.concatenate([c_hr, c_hi], axis=-1).transpose(1, 3, 0, 2)
    al_r, al_i = pr[ell], pi[ell]
    a_rows = jnp.stack([jnp.concatenate([al_r, al_r], -1), jnp.concatenate([-al_i, al_i], -1),
                        jnp.concatenate([al_i, -al_i], -1)], axis=1)

    tg = S5_TILE_GROUPS
    nt = g // tg
    tile = lambda a: a.reshape((nt, tg) + a.shape[1:])
    lane_grp = jnp.arange(LANES) // S5_GROUP
    step_lane_grp = jnp.tile(lane_grp, ell)
    spread_out = (jnp.eye(ell, dtype=BF16)[:, None, :, None, None]
                  * jnp.eye(S5_GROUP, dtype=BF16)[None, :, None, None, :]
                  * jnp.ones((1, 1, 1, tg, 1), BF16)).reshape(ell * S5_GROUP, ell * LANES)
    toep_t = tile(toep).transpose(0, 2, 1, 3, 4, 5).reshape(nt, ell, LANES, ell * S5_GROUP).astype(BF16)
    wt = jnp.einsum('ksrm,mn->ksrn', toep_t, spread_out)
    wt = jnp.where((lane_grp[:, None] == step_lane_grp[None, :])[None, None], wt, 0)
    wt = wt.reshape(nt, ell * LANES, ell * LANES)
    spread_state = jnp.tile(jnp.eye(2 * p, dtype=BF16), (1, tg))
    state_grp = jnp.arange(tg * 2 * p) // (2 * p)
    wst_t = tile(wst).transpose(0, 2, 1, 3, 4).reshape(nt, ell, LANES, 2 * p).astype(BF16)
    ws = jnp.einsum('ksrp,pn->ksrn', wst_t, spread_state)
    ws = jnp.where((lane_grp[:, None] == state_grp[None, :])[None, None], ws, 0)
    ws = ws.reshape(nt, ell * LANES, tg * 2 * p)
    cst_t = tile(cst).reshape(nt, tg * 2 * p, ell * S5_GROUP).astype(BF16)
    wc = jnp.einsum('krm,mn->krn', cst_t, spread_out)
    wc = jnp.where((state_grp[:, None] == step_lane_grp[None, :])[None], wc, 0)
    a_t = tile(a_rows).transpose(0, 2, 1, 3).reshape(nt, 3, tg * 2 * p)
    a_t = jnp.pad(a_t, ((0, 0), (0, SUBLANES - 3), (0, 0)))
    return wt, ws, wc, a_t


def _s5_scan(u, bsz, seq, tables):
    wt, ws, wc, a_t = tables
    ell = S5_CHUNK
    nt = S5_GROUPS // S5_TILE_GROUPS
    parts = S5_SEQ_PARTS if seq % (S5_SEQ_PARTS * ell * 2 * SUBLANES) == 0 else 1
    ncl = seq // parts // ell
    rows = bsz * ncl
    sw = S5_TILE_GROUPS * 2 * S5_STATE
    u3 = u.reshape(bsz, seq, S5_CH)
    io = pl.BlockSpec((bsz, seq // parts, LANES), lambda k, s: (0, s, k))
    per_tile = lambda a: pl.BlockSpec((1,) + a.shape[1:], lambda k, s: (k, 0, 0))
    y = pl.pallas_call(
        functools.partial(_s5_kernel, nb=bsz, ncl=ncl),
        grid=(nt, parts),
        in_specs=[io, per_tile(wt), per_tile(ws), per_tile(wc), per_tile(a_t)],
        out_specs=io,
        out_shape=jax.ShapeDtypeStruct((bsz, seq, S5_CH), F32),
        scratch_shapes=[pltpu.VMEM((rows, ell * LANES), F32),
                        pltpu.VMEM((2 * S5_TILE_GROUPS, rows, LANES), F32),
                        pltpu.VMEM((S5_TILE_GROUPS, rows, LANES), F32),
                        pltpu.VMEM((2, bsz, sw), F32)],
        compiler_params=_cparams("parallel", "arbitrary"),
        name="s5_scan",
    )(u3, wt, ws, wc, a_t)
    return y.reshape(bsz * seq, S5_CH)


def _fox_kernel(q_ref, k_ref, vt_ref, o_ref, *, tq, tk):
    i = pl.program_id(2)
    nh = FOX_HPS
    q = [q_ref[:, h * LANES:(h + 1) * LANES] for h in range(nh)]
    key = lax.broadcasted_iota(jnp.int32, (tk, tq), 0)
    qry = lax.broadcasted_iota(jnp.int32, (tk, tq), 1)
    per_q = tq // tk

    def block(j0, carry, mask, q0=0):
        kj = k_ref[pl.ds(j0, tk), :]
        vtj = vt_ref[:, pl.ds(j0, tk)]
        ss = [_dot_nt(kj[:, h * LANES:(h + 1) * LANES], q[h][q0:]) for h in range(nh)]
        stats = []
        for h in range(nh):
            m = carry[h][0][:, q0:]
            s = ss[h] if mask is None else jnp.where(mask[:, q0:], ss[h], -jnp.inf)
            m_new = jnp.maximum(m, jnp.max(s, axis=0, keepdims=True))
            p = jnp.exp2(s - m_new)
            stats.append((m_new, jnp.exp2(m - m_new), p.astype(BF16)))
        out = []
        for h in range(nh):
            m_new, alpha, p = stats[h]
            acc = alpha * carry[h][1][:, q0:] + _bdot(vtj[h * FOX_VROWS:(h + 1) * FOX_VROWS, :], p)
            if q0:
                m_new = jnp.concatenate([carry[h][0][:, :q0], m_new], axis=1)
                acc = jnp.concatenate([carry[h][1][:, :q0], acc], axis=1)
            out.append((m_new, acc))
        return tuple(out)

    init = tuple((jnp.full((1, tq), -jnp.inf, F32), jnp.zeros((FOX_VROWS, tq), F32)) for _ in range(nh))
    carry = lax.fori_loop(0, i * per_q, lambda j, c: block(pl.multiple_of(j * tk, tk), c, None), init)
    for d in range(per_q):
        carry = block(pl.multiple_of(i * tq + d * tk, tk), carry, key + d * tk <= qry, d * tk)
    for g in range(nh // 2):
        o_t = jnp.concatenate([carry[h][1][:FOX_DH] / carry[h][1][FOX_DH:FOX_DH + 1]
                               for h in (2 * g, 2 * g + 1)], axis=0)
        o_ref[:, g * LANES:(g + 1) * LANES] = o_t.T.astype(o_ref.dtype)


def _fox(q_aug, k_aug, v_t, bsz, seq, tq, tk):
    t = q_aug.shape[0]
    nh = FOX_HPS
    groups = FOX_HEADS // nh
    nq = seq // tq
    return pl.pallas_call(
        functools.partial(_fox_kernel, tq=tq, tk=tk),
        grid=(bsz, groups, nq),
        in_specs=[pl.BlockSpec((tq, nh * LANES), lambda b, p, i: (b * nq + i, p)),
                  pl.BlockSpec((seq, nh * LANES), lambda b, p, i: (b, p)),
                  pl.BlockSpec((nh * FOX_VROWS, seq), lambda b, p, i: (b * groups + p, 0))],
        out_specs=pl.BlockSpec((tq, nh * FOX_DH), lambda b, p, i: (b * nq + i, p)),
        out_shape=jax.ShapeDtypeStruct((t, FOX_W), BF16),
        compiler_params=_cparams("parallel", "parallel", "arbitrary"),
        name="fox_attention",
    )(q_aug, k_aug, v_t)


_ODD_COLS = (("rq", 512), ("rk", 512), ("rv", 512), ("rg", 512), ("gq", 256), ("gk", 256),
             ("gv", 512), ("gr", 512), ("glr", LANES))


def _odd_offsets():
    off, out = 0, {}
    for name, w in _ODD_COLS:
        out[name] = (off, off + w)
        off += w
    return out, off


def _in_odd_kernel(x_ref, sh_ref, sc_ref, g_ref, w_ref, cos_ref, sin_ref, wup_ref, bg_ref,
                   rq_ref, rk_ref, rv_ref, sg_ref, gq_ref, gk_ref, gv_ref, sr_ref, la_ref):
    h = _prenorm(x_ref[...], g_ref, sc_ref, sh_ref)
    z = _bdot(h.astype(BF16), w_ref[...])
    off, _ = _odd_offsets()
    col = lambda n: z[:, off[n][0]:off[n][1]]
    cos, sin = cos_ref[...], sin_ref[...]

    def rope(t, scale):
        heads = []
        for hd in range(RET_HEADS):
            th = t[:, hd * RET_DK:(hd + 1) * RET_DK]
            heads.append((th * cos + pltpu.roll(th, RET_DK // 2, 1) * sin) * scale)
        return jnp.concatenate(heads, axis=1).astype(BF16)

    rq_ref[...] = rope(col("rq"), 1.0)
    rk_ref[...] = rope(col("rk"), RET_DK ** -0.5)
    rv_ref[...] = col("rv").astype(BF16)
    sg_ref[...] = _silu(col("rg"))
    gq_ref[...] = col("gq") * (GLA_DK ** -0.5)
    gk_ref[...] = col("gk")
    gv_ref[...] = col("gv").astype(BF16)
    sr_ref[...] = _silu(col("gr"))
    glr = col("glr")
    g_hi = glr.astype(BF16)
    g_lo = (glr - g_hi.astype(F32)).astype(BF16)
    nqk = GLA_HEADS * GLA_DK
    gw = _bdot(g_hi, wup_ref[...])
    gate = gw[:, :nqk] + gw[:, nqk:] + _bdot(g_lo, wup_ref[:, :nqk]) + bg_ref[...]
    la_ref[...] = _log_sigmoid(gate) * (1.0 / GLA_TAU)


def _in_odd(x2, mod3, gain, w_in, w_up, b_gate, seq, tm):
    t, d = x2.shape
    tps = seq // tm
    ref_w = (512, 512, 512, 512, 256, 256, 512, GLA_RANK, 512)
    starts = [0]
    for wd in ref_w:
        starts.append(starts[-1] + wd)
    seg = lambda j: w_in[:, starts[j]:starts[j + 1]]
    w = jnp.concatenate([seg(0), seg(1), seg(2), seg(3), seg(4), seg(5), seg(6), seg(8),
                         jnp.pad(seg(7), ((0, 0), (0, LANES - GLA_RANK)))], axis=1).astype(BF16)
    wup32 = jnp.pad(w_up.astype(F32), ((0, LANES - GLA_RANK), (0, 0)))
    wup_hi = wup32.astype(BF16)
    wup = jnp.concatenate([wup_hi, (wup32 - wup_hi.astype(F32)).astype(BF16)], axis=1)
    bg = b_gate.reshape(1, -1).astype(F32)
    half = RET_DK // 2
    inv = ROPE_BASE ** (-jnp.arange(half, dtype=F32) / half)
    ang = jnp.arange(seq, dtype=F32)[:, None] * inv[None, :]
    cos = jnp.concatenate([jnp.cos(ang), jnp.cos(ang)], axis=1)
    sin = jnp.concatenate([-jnp.sin(ang), jnp.sin(ang)], axis=1)
    tok = lambda n: pl.BlockSpec((tm, n), lambda i: (i, 0))
    full = lambda a: pl.BlockSpec(a.shape, lambda i: (0,) * a.ndim)
    pos = pl.BlockSpec((tm, RET_DK), lambda i: (i % tps, 0))
    widths = (512, 512, 512, 512, 256, 256, 512, 512, 256)
    dtypes = (BF16, BF16, BF16, F32, F32, F32, BF16, F32, F32)
    return pl.pallas_call(
        _in_odd_kernel,
        grid=(t // tm,),
        in_specs=[tok(d), _mod_spec(0, tps), _mod_spec(1, tps), full(gain), full(w), pos, pos,
                  full(wup), full(bg)],
        out_specs=[tok(n) for n in widths],
        out_shape=[jax.ShapeDtypeStruct((t, n), dt) for n, dt in zip(widths, dtypes)],
        compiler_params=_cparams("parallel"),
        name="in_proj_odd",
    )(x2, mod3, mod3, gain, w, cos, sin, wup, bg)


RET_CHUNK = 256


def _ret_kernel(q_ref, k_ref, v_ref, sg_ref, dm_ref, xi_ref, zeta_ref, gl_ref, y_ref, st_ref):
    @pl.when(pl.program_id(1) == 0)
    def _():
        st_ref[...] = jnp.zeros_like(st_ref)

    nb = q_ref.shape[0]
    chains = [(bl, h) for bl in range(nb) for h in range(RET_HEADS)]
    col = lambda h: slice(h * RET_DK, (h + 1) * RET_DK)
    q = [q_ref[bl, :, col(h)] for bl, h in chains]
    k = [k_ref[bl, :, col(h)] for bl, h in chains]
    v = [v_ref[bl, :, col(h)] for bl, h in chains]
    st = [st_ref[c] for c in range(len(chains))]
    s = [_dot_nt(q[c], k[c]) for c in range(len(chains))]
    inter = [_bdot((q[c].astype(F32) * xi_ref[h]).astype(BF16), st[c].astype(BF16))
             for c, (bl, h) in enumerate(chains)]
    upd = [_dot_tn((k[c].astype(F32) * zeta_ref[h]).astype(BF16), v[c]) for c, (bl, h) in enumerate(chains)]
    for c, (bl, h) in enumerate(chains):
        o = _bdot((s[c] * dm_ref[h]).astype(BF16), v[c]) + inter[c]
        st_ref[c] = gl_ref[h, 0:1, :] * st[c] + upd[c]
        y_ref[bl, :, col(h)] = (sg_ref[bl, :, col(h)] * _rms(o)).astype(y_ref.dtype)


RET_BATCHES = 2


def _retention(rq, rk, rv, sg, bsz, seq):
    t = rq.shape[0]
    ell = min(RET_CHUNK, seq)
    nc = seq // ell
    log_g = jnp.log(1.0 - jnp.exp2(-5.0 - jnp.arange(RET_HEADS, dtype=F32)))
    idx = jnp.arange(ell, dtype=F32)
    rel = idx[:, None] - idx[None, :]
    dmat = jnp.where(rel >= 0, jnp.exp(log_g[:, None, None] * jnp.maximum(rel, 0.0)), 0.0)
    lanes = lambda a: jnp.broadcast_to(a[..., None], a.shape + (RET_DK,))
    xi = lanes(jnp.exp(log_g[:, None] * (idx + 1.0)))
    zeta = lanes(jnp.exp(log_g[:, None] * (ell - 1.0 - idx)))
    gl = jnp.broadcast_to(jnp.exp(log_g * ell)[:, None, None], (RET_HEADS, SUBLANES, RET_DV))
    nb = min(RET_BATCHES, bsz)
    width = RET_HEADS * RET_DK
    r3 = lambda a: a.reshape(bsz, seq, width)
    blk = pl.BlockSpec((nb, ell, width), lambda b, c: (b, c, 0))
    full = lambda a: pl.BlockSpec(a.shape, lambda b, c: (0, 0, 0))
    y = pl.pallas_call(
        _ret_kernel,
        grid=(bsz // nb, nc),
        in_specs=[blk, blk, blk, blk, full(dmat), full(xi), full(zeta), full(gl)],
        out_specs=blk,
        out_shape=jax.ShapeDtypeStruct((bsz, seq, width), BF16),
        scratch_shapes=[pltpu.VMEM((nb * RET_HEADS, RET_DK, RET_DV), F32)],
        compiler_params=_cparams("parallel", "arbitrary"),
        name="retention",
    )(r3(rq), r3(rk), r3(rv), r3(sg), dmat, xi, zeta, gl)
    return y.reshape(t, width)


def _gla_kernel(q_ref, k_ref, la_ref, v_ref, sg_ref, y_ref, st_ref, b_scr, v_scr, p_scr, r_scr):
    @pl.when(pl.program_id(1) == 0)
    def _():
        st_ref[...] = jnp.zeros_like(st_ref)

    ell, sub = GLA_CHUNK, GLA_SUB
    n_sub = ell // sub
    nb = q_ref.shape[0]
    pairs = GLA_HEADS // 2
    streams = [(bl, p) for bl in range(nb) for p in range(pairs)]
    lane = lax.broadcasted_iota(jnp.int32, (1, LANES), 1)
    first = lane < GLA_DK
    head = (first, jnp.logical_not(first))
    pick = lambda h, a: jnp.where(head[h], a, 0.0).astype(BF16)
    tri = _lower_tri(ell)
    tau = lax.broadcasted_iota(jnp.int32, (sub, LANES), 0)
    row_of, per_sub = [], 0
    for s_ in range(sub):
        row_of.append(per_sub)
        per_sub += sub - (s_ // SUBLANES) * SUBLANES
    rsub = lax.broadcasted_iota(jnp.int32, (LANES, 2 * LANES), 0)
    csub = lax.broadcasted_iota(jnp.int32, (LANES, 2 * LANES), 1)
    ind = ((rsub < GLA_DK) == (csub < LANES)).astype(BF16)

    val = {}
    for sid, (bl, p) in enumerate(streams):
        qk = slice(p * LANES, (p + 1) * LANES)
        q, k = q_ref[bl, :, qk], k_ref[bl, :, qk]
        b = _dot01(tri, la_ref[bl, :, qk])
        b_scr[sid] = b
        v_bf = v_ref[bl, :, p * 2 * GLA_DV:(p + 1) * 2 * GLA_DV]
        v_scr[sid] = v_bf.astype(F32)
        st = st_ref[sid]
        val[sid] = dict(q=q, k=k, b=b, st=st, vh=[v_bf[:, h * GLA_DV:(h + 1) * GLA_DV] for h in range(2)])

    for sid in val:
        d = val[sid]
        qe = d["q"] * jnp.exp(d["b"])
        st_bf = d["st"].astype(BF16)
        d["o"] = [_dot_nt(pick(h, qe), st_bf) for h in range(2)]

    row = lax.broadcasted_iota(jnp.int32, (ell, LANES), 0)
    for sid in val:
        d = val[sid]
        q, k, b = d["q"], d["k"], d["b"]
        qa, ka = [], []
        for i in range(1, n_sub):
            lo = i * sub
            ref_row = b[lo - 1:lo, :]
            in_i = (row >= lo) & (row < lo + sub)
            qa.append(jnp.where(in_i, q * jnp.exp(jnp.minimum(b - ref_row, 0.0)), 0.0))
            ka.append(jnp.where(row < lo, k * jnp.exp(jnp.minimum(ref_row - b, 0.0)), 0.0))
        k_cat = jnp.concatenate(ka, axis=1).astype(BF16)
        d["a_off"] = [_dot_nt(jnp.concatenate([pick(h, x) for x in qa], axis=1), k_cat) for h in range(2)]
    for sid in val:
        d = val[sid]
        d["off"] = [_bdot(d["a_off"][h].astype(BF16), d["vh"][h]) for h in range(2)]

    for sid, (bl, p) in enumerate(streams):
        d = val[sid]
        q, b = d["q"], d["b"]
        for i in range(n_sub):
            lo = i * sub
            qi, bi = q[lo:lo + sub], b[lo:lo + sub]
            for s in range(sub):
                k_row = k_ref[bl, pl.ds(lo + s, 1), p * LANES:(p + 1) * LANES]
                b_row = b_scr[sid, pl.ds(lo + s, 1), :]
                r0 = (s // SUBLANES) * SUBLANES
                w = jnp.exp(jnp.minimum(bi[r0:] - b_row, 0.0))
                tile_s = jnp.where(tau[r0:] >= s, qi[r0:] * k_row * w, 0.0)
                p_scr[sid, pl.ds(i * per_sub + row_of[s], sub - r0), :] = tile_s
    for sid in val:
        r_scr[sid] = _bdot(p_scr[sid].astype(BF16), ind)
    for sid in val:
        diag = [[], []]
        for i in range(n_sub):
            lo = i * sub
            for h in range(2):
                acc = [jnp.zeros((SUBLANES, GLA_DV), F32) for _ in range(sub // SUBLANES)]
                for s in range(sub):
                    v_row = v_scr[sid, pl.ds(lo + s, 1), h * GLA_DV:(h + 1) * GLA_DV]
                    for part in range(s // SUBLANES, sub // SUBLANES):
                        rows = pl.ds(i * per_sub + row_of[s] + (part - s // SUBLANES) * SUBLANES, SUBLANES)
                        acc[part] = acc[part] + r_scr[sid, rows, h * LANES:(h + 1) * LANES] * v_row
                diag[h].append(jnp.concatenate(acc, axis=0))
        val[sid]["diag"] = diag

    for sid, (bl, p) in enumerate(streams):
        d = val[sid]
        b_last = d["b"][ell - 1:ell, :]
        kh = (d["k"] * jnp.exp(b_last - d["b"])).astype(BF16)
        upd = [_dot_tn(d["vh"][h], kh) for h in range(2)]
        st_ref[sid] = d["st"] * jnp.exp(b_last) + jnp.where(first, upd[0], upd[1])
        for h in range(2):
            oh = d["o"][h] + d["off"][h] + jnp.concatenate(d["diag"][h], axis=0)
            cols = slice((2 * p + h) * GLA_DV, (2 * p + h + 1) * GLA_DV)
            y_ref[bl, :, cols] = (sg_ref[bl, :, cols] * _rms(oh)).astype(y_ref.dtype)


GLA_BATCHES = 4


def _gla(gq, gk, la, gv, sr, bsz, seq):
    t = gq.shape[0]
    ell = GLA_CHUNK
    nc = seq // ell
    nb = min(GLA_BATCHES, bsz)
    ns = nb * (GLA_HEADS // 2)
    prod_rows = (ell // GLA_SUB) * sum(GLA_SUB - (s // SUBLANES) * SUBLANES for s in range(GLA_SUB))
    r3 = lambda a: a.reshape(bsz, seq, a.shape[1])
    spec = lambda w: pl.BlockSpec((nb, ell, w), lambda b, c: (b, c, 0))
    wq, wv = GLA_HEADS * GLA_DK, GLA_HEADS * GLA_DV
    y = pl.pallas_call(
        _gla_kernel,
        grid=(bsz // nb, nc),
        in_specs=[spec(wq), spec(wq), spec(wq), spec(wv), spec(wv)],
        out_specs=spec(wv),
        out_shape=jax.ShapeDtypeStruct((bsz, seq, wv), BF16),
        scratch_shapes=[pltpu.VMEM((ns, GLA_DV, LANES), F32),
                        pltpu.VMEM((ns, ell, LANES), F32),
                        pltpu.VMEM((ns, ell, 2 * GLA_DV), F32),
                        pltpu.VMEM((ns, prod_rows, LANES), F32),
                        pltpu.VMEM((ns, prod_rows, 2 * LANES), F32)],
        compiler_params=_cparams("parallel", "arbitrary"),
        name="gla",
    )(r3(gq), r3(gk), r3(la), r3(gv), r3(sr))
    return y.reshape(t, wv)


ROUTE_IDX, ROUTE_GATE, ROUTE_RANK = 0, SUBLANES, 2 * SUBLANES


def _post_tail(m, x_ref, g1_ref, gpost_ref, sh2_ref, sc2_ref, gpre_ref, rw_ref, rb_ref,
               x1_ref, h2_ref, route_ref, plan_ref, cnt_ref, carry_ref):
    i = pl.program_id(0)
    tm = m.shape[0]
    x1 = x_ref[...] + g1_ref[0] * (_rms(m) * gpost_ref[...])
    x1_ref[...] = x1
    h2 = _rms(x1) * (gpre_ref[...] * (1.0 + sc2_ref[0])) + sh2_ref[0]
    for c in range(D_MODEL // LANES):
        h2_ref[pl.ds(c, tm, stride=SUBLANES), :] = h2[:, c * LANES:(c + 1) * LANES]

    ne = N_EXPERTS
    h_hi = h2.astype(BF16)
    h_lo = (h2 - h_hi.astype(F32)).astype(BF16)
    w_both = rw_ref[...]
    hw = _dot_nt(w_both, h_hi)
    bias = jnp.concatenate([rb_ref[...]] * (tm // LANES), axis=1)
    work = hw[:ne] + hw[ne:] + _dot_nt(w_both[:ne], h_lo) + bias
    esub = lax.broadcasted_iota(jnp.int32, (ne, tm), 0).astype(F32)
    onehot = jnp.zeros((ne, tm), F32)
    hits, vals, idxs = [], [], []
    for k in range(TOP_K):
        mx = jnp.max(work, axis=0, keepdims=True)
        idx = jnp.min(jnp.where(work == mx, esub, float(ne)), axis=0, keepdims=True)
        hit = esub == idx
        hits.append(hit)
        vals.append(mx)
        idxs.append(idx)
        onehot = onehot + hit.astype(F32)
        work = jnp.where(hit, -jnp.inf, work)
    es = [jnp.exp(v - vals[0]) for v in vals]
    inv = 1.0 / (es[0] + es[1] + es[2] + es[3])
    gates = [e * inv for e in es]

    @pl.when(i == 0)
    def _():
        carry_ref[...] = jnp.zeros_like(carry_ref)

    r = lax.broadcasted_iota(jnp.int32, (tm, tm), 0)
    c = lax.broadcasted_iota(jnp.int32, (tm, tm), 1)
    carry = carry_ref[...]
    before = _bdot(onehot.astype(BF16), (r < c).astype(BF16)) + jnp.concatenate([carry[:ne]] * (tm // LANES), axis=1)
    ranks = [jnp.sum(jnp.where(hits[k], before, 0.0), axis=0, keepdims=True) for k in range(TOP_K)]
    carry = carry + jnp.concatenate(
        [jnp.broadcast_to(jnp.sum(onehot, axis=1, keepdims=True), (ne, LANES)),
         jnp.zeros((LANES - ne, LANES), F32)], axis=0)
    carry_ref[...] = carry
    cnt_ref[...] = carry.T[:SUBLANES]

    sub = lax.broadcasted_iota(jnp.int32, (SUBLANES, tm), 0)

    def rows(vs):
        out = jnp.zeros((SUBLANES, tm), F32)
        for k, v in enumerate(vs):
            out = jnp.where(sub == k, jnp.broadcast_to(v, (SUBLANES, tm)), out)
        return out

    rec = jnp.concatenate([rows(idxs), rows(gates), rows(ranks), jnp.zeros((LANES - 3 * SUBLANES, tm), F32)], axis=0)
    route_ref[...] = rec.T
    plan_ref[...] = rows(idxs + ranks)


def _out_even_kernel(ys_ref, u_ref, yb_ref, d_ref, gw_ref, gb_ref, wa_ref, wb_ref, *rest):
    y = ys_ref[...] + d_ref[...] * u_ref[...]
    g = jax.nn.gelu(y)
    ya = g * _sigmoid(_bdot(g.astype(BF16), gw_ref[...]) + gb_ref[...])
    m = _bdot(ya.astype(BF16), wa_ref[...]) + _bdot(yb_ref[...], wb_ref[...])
    _post_tail(m, *rest)


def _out_odd_kernel(yc_ref, yd_ref, wa_ref, wb_ref, *rest):
    m = _bdot(yc_ref[...], wa_ref[...]) + _bdot(yd_ref[...], wb_ref[...])
    _post_tail(m, *rest)


def _mixer_out(body, mix_args, mix_specs, x2, mod3, g_post, g_pre, router_w, router_b, seq, tm):
    t, d = x2.shape
    tps = seq // tm
    rw32 = router_w.astype(F32).T
    rw_hi = rw32.astype(BF16)
    rw = jnp.concatenate([rw_hi, (rw32 - rw_hi.astype(F32)).astype(BF16)], axis=0)
    rb = jnp.broadcast_to(router_b.astype(F32)[:, None], (N_EXPERTS, LANES))
    tok = lambda n: pl.BlockSpec((tm, n), lambda i: (i, 0))
    full = lambda a: pl.BlockSpec(a.shape, lambda i: (0,) * a.ndim)
    tail_args = [x2, mod3, g_post, mod3, mod3, g_pre, rw, rb]
    tail_specs = [tok(d), _mod_spec(2, tps), full(g_post), _mod_spec(3, tps), _mod_spec(4, tps),
                  full(g_pre), full(rw), full(rb)]
    return pl.pallas_call(
        body,
        grid=(t // tm,),
        in_specs=mix_specs + tail_specs,
        out_specs=[tok(d), pl.BlockSpec((tm * SUBLANES, LANES), lambda i: (i, 0)),
                   tok(LANES), pl.BlockSpec((SUBLANES, tm), lambda i: (0, i)),
                   pl.BlockSpec((SUBLANES, LANES), lambda i: (0, 0))],
        out_shape=[jax.ShapeDtypeStruct((t, d), F32),
                   jax.ShapeDtypeStruct((t * SUBLANES, LANES), F32),
                   jax.ShapeDtypeStruct((t, LANES), F32),
                   jax.ShapeDtypeStruct((SUBLANES, t), F32),
                   jax.ShapeDtypeStruct((SUBLANES, LANES), F32)],
        scratch_shapes=[pltpu.VMEM((LANES, LANES), F32)],
        compiler_params=_cparams("arbitrary"),
        name="mixer_out_router",
    )(*mix_args, *tail_args)


def _out_even(ys, u, yb, d_skip, glu_w, glu_b, w_out, *tail, seq, tm):
    tok = lambda n: pl.BlockSpec((tm, n), lambda i: (i, 0))
    full = lambda a: pl.BlockSpec(a.shape, lambda i: (0,) * a.ndim)
    args = [ys, u, yb, d_skip.reshape(1, -1), glu_w.astype(BF16), glu_b.reshape(1, -1),
            w_out[:S5_CH].astype(BF16), w_out[S5_CH:].astype(BF16)]
    specs = [tok(S5_CH), tok(S5_CH), tok(FOX_W)] + [full(a) for a in args[3:]]
    return _mixer_out(_out_even_kernel, args, specs, *tail, seq, tm)


def _out_odd(yc, yd, w_out, *tail, seq, tm):
    tok = lambda n: pl.BlockSpec((tm, n), lambda i: (i, 0))
    full = lambda a: pl.BlockSpec(a.shape, lambda i: (0,) * a.ndim)
    nc = yc.shape[1]
    args = [yc, yd, w_out[:nc].astype(BF16), w_out[nc:].astype(BF16)]
    specs = [tok(nc), tok(yd.shape[1])] + [full(a) for a in args[2:]]
    return _mixer_out(_out_odd_kernel, args, specs, *tail, seq, tm)


def _route_kernel(plan_ref, cnt_ref, dest_ref, blk_ref, meta_ref):
    tm = plan_ref.shape[1]
    cnt = cnt_ref[...]
    padded = jnp.floor((cnt + (MOE_BLOCK - 1.0)) * (1.0 / MOE_BLOCK)) * MOE_BLOCK
    r = lax.broadcasted_iota(jnp.int32, (LANES, LANES), 0)
    c = lax.broadcasted_iota(jnp.int32, (LANES, LANES), 1)
    hi, mid, lo = _split3(padded)
    incl = (r <= c).astype(BF16)
    pad_end = _bdot(hi, incl) + _bdot(mid, incl) + _bdot(lo, incl)
    pad_start = pad_end - padded
    start_col = jnp.broadcast_to(pad_start[0:1, :], (LANES, LANES)).T[:N_EXPERTS]
    start = jnp.concatenate([start_col] * (tm // LANES), axis=1)
    esub = lax.broadcasted_iota(jnp.int32, (N_EXPERTS, tm), 0).astype(F32)
    plan = plan_ref[...]
    sub = lax.broadcasted_iota(jnp.int32, (SUBLANES, tm), 0)
    dest = jnp.zeros((SUBLANES, tm), F32)
    for k in range(TOP_K):
        st = jnp.sum(jnp.where(esub == plan[k:k + 1, :], start, 0.0), axis=0, keepdims=True)
        dest = jnp.where(sub == k, jnp.broadcast_to(st + plan[TOP_K + k:TOP_K + k + 1, :], (SUBLANES, tm)), dest)
    dest_ref[...] = dest.astype(jnp.int32)

    nb = blk_ref.shape[1]
    end_col = jnp.sum(jnp.where(r == c, jnp.broadcast_to(pad_end[0:1, :], (LANES, LANES)), 0.0),
                      axis=-1, keepdims=True)
    jpos = lax.broadcasted_iota(jnp.int32, (LANES, nb), 1).astype(F32) * MOE_BLOCK
    esub = lax.broadcasted_iota(jnp.int32, (LANES, nb), 0)
    done = jnp.where((end_col <= jpos) & (esub < N_EXPERTS), 1.0, 0.0)
    be = jnp.minimum(jnp.sum(done, axis=0, keepdims=True), N_EXPERTS - 1.0)
    blk_ref[...] = jnp.broadcast_to(be, blk_ref.shape).astype(jnp.int32)
    lane1 = lax.broadcasted_iota(jnp.int32, (SUBLANES, LANES), 1)
    n_valid = jnp.sum(jnp.where(lane1 == N_EXPERTS - 1, pad_end, 0.0), axis=-1, keepdims=True) * (1.0 / MOE_BLOCK)
    sub1 = lax.broadcasted_iota(jnp.int32, (SUBLANES, LANES), 0)
    meta = jnp.where(sub1 == 0, pad_start + cnt, jnp.where(sub1 == 1, pad_end, jnp.broadcast_to(n_valid, (SUBLANES, LANES))))
    meta_ref[...] = meta.astype(jnp.int32)


def _route(plan, cnt, n_blocks, tm):
    t = plan.shape[1]
    nb_pad = -(-n_blocks // LANES) * LANES
    tok = pl.BlockSpec((SUBLANES, tm), lambda i: (0, i))
    fix = lambda n: pl.BlockSpec((SUBLANES, n), lambda i: (0, 0))
    return pl.pallas_call(
        _route_kernel,
        grid=(t // tm,),
        in_specs=[tok, fix(LANES)],
        out_specs=[tok, fix(nb_pad), fix(LANES)],
        out_shape=[jax.ShapeDtypeStruct((SUBLANES, t), jnp.int32),
                   jax.ShapeDtypeStruct((SUBLANES, nb_pad), jnp.int32),
                   jax.ShapeDtypeStruct((SUBLANES, LANES), jnp.int32)],
        compiler_params=_cparams("arbitrary"),
        name="route_plan",
    )(plan, cnt)


def _dispatch_kernel(pad_ref, dest_ref, h_ref, xb_ref, zero_ref, sem_z, sem_s):
    i = pl.program_id(0)
    tm = h_ref.shape[0]

    @pl.when(i == 0)
    def _():
        zero_ref[...] = jnp.zeros_like(zero_ref)
        sizes = [1 << b for b in range(int(math.log2(MOE_BLOCK)) - 1, -1, -1)]

        def fill(e, carry, do_wait):
            start = pad_ref[0, e]
            n_pad = pad_ref[1, e] - start
            off = start
            for sz in sizes:
                take = (n_pad & sz) != 0
                cp = pltpu.make_async_copy(zero_ref.at[pl.ds(0, sz)], xb_ref.at[pl.ds(off, sz)], sem_z)

                @pl.when(take)
                def _():
                    if do_wait:
                        cp.wait()
                    else:
                        cp.start()
                off = off + jnp.where(take, sz, 0)
            return carry

        half = zero_ref.shape[0]

        def fill_unused(j, carry, do_wait):
            for part in range(MOE_BLOCK // half):
                cp = pltpu.make_async_copy(zero_ref, xb_ref.at[pl.ds(j * MOE_BLOCK + part * half, half)], sem_z)
                if do_wait:
                    cp.wait()
                else:
                    cp.start()
            return carry

        n_blocks = xb_ref.shape[0] // MOE_BLOCK
        lax.fori_loop(0, N_EXPERTS, lambda e, c: fill(e, c, False), 0)
        lax.fori_loop(pad_ref[2, 0], n_blocks, lambda j, c: fill_unused(j, c, False), 0)
        lax.fori_loop(0, N_EXPERTS, lambda e, c: fill(e, c, True), 0)
        lax.fori_loop(pad_ref[2, 0], n_blocks, lambda j, c: fill_unused(j, c, True), 0)

    def issue(r, carry):
        for k in range(TOP_K):
            pltpu.make_async_copy(h_ref.at[r], xb_ref.at[dest_ref[r * TOP_K + k]], sem_s).start(priority=k % 2)
        return carry

    lax.fori_loop(0, tm, issue, 0, unroll=ROW_DMA_UNROLL)
    for k in range(TOP_K):
        pltpu.make_async_copy(h_ref, xb_ref.at[pl.ds(0, tm)], sem_s).wait()


def _dispatch(h2t, dest, meta, n_slots, tm):
    t = h2t.shape[0] // SUBLANES
    h3 = h2t.reshape(t, SUBLANES, LANES)
    return pl.pallas_call(
        _dispatch_kernel,
        grid_spec=pltpu.PrefetchScalarGridSpec(
            num_scalar_prefetch=1,
            grid=(t // tm,),
            in_specs=[pl.BlockSpec((tm * TOP_K,), lambda i, p: (i,), memory_space=pltpu.SMEM),
                      pl.BlockSpec((tm, SUBLANES, LANES), lambda i, p: (i, 0, 0))],
            out_specs=pl.BlockSpec(memory_space=pl.ANY),
            scratch_shapes=[pltpu.VMEM((MOE_BLOCK // 2, SUBLANES, LANES), F32),
                            pltpu.SemaphoreType.DMA, pltpu.SemaphoreType.DMA]),
        out_shape=jax.ShapeDtypeStruct((n_slots, SUBLANES, LANES), F32),
        compiler_params=_cparams("arbitrary"),
        name="moe_dispatch",
    )(meta[:3, :N_EXPERTS], dest, h3)


def _expert_kernel(be_ref, nv_ref, nxt_ref, x_ref, wgu_hbm, bgu_ref, wd_hbm, bd_ref, y_ref,
                   wgu_f32, wd_f32, wgu_bf, wd_bf, sem, *, layer):
    j = pl.program_id(0)
    valid = j < nv_ref[0]
    first = valid & ((j == 0) | (be_ref[j] != be_ref[jnp.maximum(j - 1, 0)]))

    def weight_copies(e):
        return (pltpu.make_async_copy(wgu_hbm.at[layer, e], wgu_f32, sem.at[0]),
                pltpu.make_async_copy(wd_hbm.at[layer, e], wd_f32, sem.at[1]))

    @pl.when(j == 0)
    def _():
        for cp in weight_copies(be_ref[0]):
            cp.start()

    @pl.when(first)
    def _():
        for cp in weight_copies(be_ref[j]):
            cp.wait()
        wgu_bf[...] = wgu_f32[...].astype(BF16)
        wd_bf[...] = wd_f32[...].astype(BF16)

        @pl.when(nxt_ref[j] >= 0)
        def _():
            for cp in weight_copies(nxt_ref[j]):
                cp.start()

    @pl.when(valid)
    def _():
        x = jnp.concatenate([x_ref[pl.ds(c, MOE_BLOCK, stride=SUBLANES), :] for c in range(D_MODEL // LANES)],
                            axis=1).astype(BF16)
        gu = _bdot(x, wgu_bf[...]) + bgu_ref[0]
        x_glu = jnp.minimum(gu[:, :D_EXPERT], SWIGLU_LIMIT)
        x_lin = jnp.clip(gu[:, D_EXPERT:], -SWIGLU_LIMIT, SWIGLU_LIMIT)
        act = x_glu * _sigmoid(SWIGLU_ALPHA * x_glu) * (x_lin + 1.0)
        y = _bdot(act.astype(BF16), wd_bf[...]) + bd_ref[0]
        for c in range(D_MODEL // LANES):
            y_ref[pl.ds(c, MOE_BLOCK, stride=SUBLANES), :] = y[:, c * LANES:(c + 1) * LANES]

    @pl.when(jnp.logical_not(valid))
    def _():
        y_ref[...] = jnp.zeros_like(y_ref)


def _experts(xb, block_expert, n_valid, w_gu, b_gu, w_down, b_down, layer):
    n_slots = xb.shape[0]
    n_blocks = n_slots // MOE_BLOCK
    rows = MOE_BLOCK * SUBLANES
    x2 = xb.reshape(n_slots * SUBLANES, LANES)
    depth, ne, d, de2 = w_gu.shape
    idx = jnp.arange(n_blocks, dtype=jnp.int32)
    is_first = ((idx == 0) | (block_expert != jnp.roll(block_expert, 1))) & (idx < n_valid[0])
    first_at = lax.cummin(jnp.where(is_first, idx, n_blocks)[::-1])[::-1]
    next_first = jnp.concatenate([first_at[1:], jnp.full((1,), n_blocks, jnp.int32)])
    nxt = jnp.where(next_first < n_blocks, block_expert[jnp.minimum(next_first, n_blocks - 1)], -1)
    last = lambda j, be, nv, nx: jnp.minimum(j, nv[0] - 1)
    bmap = lambda j, be, nv, nx: (layer, be[last(j, be, nv, nx)], 0, 0)
    return pl.pallas_call(
        functools.partial(_expert_kernel, layer=layer),
        grid_spec=pltpu.PrefetchScalarGridSpec(
            num_scalar_prefetch=3,
            grid=(n_blocks,),
            in_specs=[pl.BlockSpec((rows, LANES), lambda j, be, nv, nx: (last(j, be, nv, nx), 0)),
                      pl.BlockSpec(memory_space=pl.ANY),
                      pl.BlockSpec((None, 1, 1, de2), bmap),
                      pl.BlockSpec(memory_space=pl.ANY),
                      pl.BlockSpec((None, 1, 1, d), bmap)],
            out_specs=pl.BlockSpec((rows, LANES), lambda j, be, nv, nx: (j, 0)),
            scratch_shapes=[pltpu.VMEM((d, de2), F32), pltpu.VMEM((de2 // 2, d), F32),
                            pltpu.VMEM((d, de2), BF16), pltpu.VMEM((de2 // 2, d), BF16),
                            pltpu.SemaphoreType.DMA((2,))]),
        out_shape=jax.ShapeDtypeStruct((n_slots * SUBLANES, LANES), F32),
        compiler_params=_cparams("arbitrary"),
        name="moe_experts",
    )(block_expert, n_valid, nxt.astype(jnp.int32), x2, w_gu, b_gu.reshape(depth, ne, 1, de2), w_down,
      b_down.reshape(depth, ne, 1, d))


def _combine_kernel(dest_ref, dest_next_ref, yb_ref, gate_ref, x1_ref, g2_ref, gpost_ref, o_ref, buf, sem):
    i = pl.program_id(0)
    tm = x1_ref.shape[0]
    slot = i % 2

    def gather(idx_ref, into):
        def issue(r, carry):
            for k in range(TOP_K):
                src = pl.multiple_of(idx_ref[r * TOP_K + k] * SUBLANES, SUBLANES)
                dst = pl.multiple_of((k * tm + r) * SUBLANES, SUBLANES)
                pltpu.make_async_copy(yb_ref.at[pl.ds(src, SUBLANES), :], buf.at[into, pl.ds(dst, SUBLANES), :],
                                      sem.at[into]).start(priority=k % 2)
            return carry
        lax.fori_loop(0, tm, issue, 0, unroll=ROW_DMA_UNROLL)

    @pl.when(i == 0)
    def _():
        gather(dest_ref, 0)

    @pl.when(i + 1 < pl.num_programs(0))
    def _():
        gather(dest_next_ref, 1 - slot)

    pltpu.make_async_copy(yb_ref.at[pl.ds(0, TOP_K * tm * SUBLANES), :], buf.at[slot], sem.at[slot]).wait()
    gates = gate_ref[...]
    gk = [jnp.broadcast_to(gates[:, ROUTE_GATE + k:ROUTE_GATE + k + 1], (tm, LANES)) for k in range(TOP_K)]
    b2 = buf.at[slot]
    cols = []
    for c in range(D_MODEL // LANES):
        acc = jnp.zeros((tm, LANES), F32)
        for k in range(TOP_K):
            acc = acc + gk[k] * b2[pl.ds(k * tm * SUBLANES + c, tm, stride=SUBLANES), :]
        cols.append(acc)
    f = jnp.concatenate(cols, axis=1)
    o_ref[...] = x1_ref[...] + g2_ref[0] * (_rms(f) * gpost_ref[...])


def _combine(yb, dest, gates, x1, mod3, g_post, seq, tm):
    t, d = x1.shape
    tps = seq // tm
    n = t // tm
    return pl.pallas_call(
        _combine_kernel,
        grid=(n,),
        in_specs=[pl.BlockSpec((tm * TOP_K,), lambda i: (i,), memory_space=pltpu.SMEM),
                  pl.BlockSpec((tm * TOP_K,), lambda i: (jnp.minimum(i + 1, n - 1),), memory_space=pltpu.SMEM),
                  pl.BlockSpec(memory_space=pl.ANY),
                  pl.BlockSpec((tm, LANES), lambda i: (i, 0)),
                  pl.BlockSpec((tm, d), lambda i: (i, 0)),
                  _mod_spec(5, tps),
                  pl.BlockSpec(g_post.shape, lambda i: (0, 0))],
        out_specs=pl.BlockSpec((tm, d), lambda i: (i, 0)),
        out_shape=jax.ShapeDtypeStruct((t, d), F32),
        scratch_shapes=[pltpu.VMEM((2, TOP_K * tm * SUBLANES, LANES), F32), pltpu.SemaphoreType.DMA((2,))],
        compiler_params=_cparams("arbitrary"),
        name="moe_combine",
    )(dest, dest, yb, gates, x1, mod3, g_post)


def _moe(h2t, route, plan, cnt, x1, mod3, g_post, w_gu, b_gu, w_down, b_down, layer, seq):
    t = x1.shape[0]
    n_blocks = t * TOP_K // MOE_BLOCK + N_EXPERTS
    dest_rows, blk, meta = _route(plan, cnt, n_blocks, min(2048, t))
    dest = dest_rows[:TOP_K].T.reshape(t * TOP_K)
    xb = _dispatch(h2t, dest, meta, n_blocks * MOE_BLOCK, MOE_BLOCK)
    yb = _experts(xb, blk[0, :n_blocks], meta[2, :1], w_gu, b_gu, w_down, b_down, layer)
    return _combine(yb, dest, route, x1, mod3, g_post, seq, MOE_BLOCK)


TOKEN_TILE = 512
OUT_TILE = 512
FOX_Q_TILE = 512
FOX_K_TILE = 256


def kernel(x, c, ada_w, ada_b, norm_pre_mix, norm_post_mix, norm_pre_ffn, norm_post_ffn, ev_w_in, fox_b_f, s5_lam_re, s5_lam_im, s5_log_dt, s5_b_re, s5_b_im, s5_c_re, s5_c_im, s5_d, s5_glu_w, s5_glu_b, ev_w_out, od_w_in, gla_w_up, gla_b_gate, od_w_out, router_w, router_b, exp_w_gu, exp_b_gu, exp_w_down, exp_b_down):
    bsz, seq, d = x.shape
    t = bsz * seq
    tm = min(TOKEN_TILE, seq)
    x2 = x.reshape(t, d)
    mod = _modulation(c, ada_w, ada_b)
    for l in range(DEPTH):
        i = l // 2
        mod3 = mod[l].reshape(bsz, 1, 6 * d)
        row = lambda a: a[l].reshape(1, -1)
        tail = (x2, mod3, row(norm_post_mix), row(norm_pre_ffn), router_w[l], router_b[l])
        if l % 2 == 0:
            u, q, k, v_t = _in_even(x2, mod3, row(norm_pre_mix), ev_w_in[i], fox_b_f[i], seq, tm)
            tables = _s5_tables(s5_lam_re[i], s5_lam_im[i], s5_log_dt[i], s5_b_re[i], s5_b_im[i],
                                s5_c_re[i], s5_c_im[i])
            ys = _s5_scan(u, bsz, seq, tables)
            yb = _fox(q, k, v_t, bsz, seq, min(FOX_Q_TILE, seq), min(FOX_K_TILE, seq))
            outs = _out_even(ys, u, yb, s5_d[i], s5_glu_w[i], s5_glu_b[i], ev_w_out[i], *tail, seq=seq, tm=min(OUT_TILE, seq))
        else:
            rq, rk, rv, sg, gq, gk, gv, sr, la = _in_odd(x2, mod3, row(norm_pre_mix), od_w_in[i],
                                                         gla_w_up[i], gla_b_gate[i], seq, tm)
            yc = _retention(rq, rk, rv, sg, bsz, seq)
            yd = _gla(gq, gk, la, gv, sr, bsz, seq)
            outs = _out_odd(yc, yd, od_w_out[i], *tail, seq=seq, tm=min(OUT_TILE, seq))
        x1, h2t, route, plan, cnt = outs
        x2 = _moe(h2t, route, plan, cnt, x1, mod3, row(norm_post_ffn),
                  exp_w_gu, exp_b_gu, exp_w_down, exp_b_down, l, seq)
    return x2.reshape(bsz, seq, d)
```

```python
import functools
import math

import jax
import jax.numpy as jnp
from jax import lax
from jax.experimental import pallas as pl
from jax.experimental.pallas import tpu as pltpu

F32 = jnp.float32
BF16 = jnp.bfloat16
HIGHEST = lax.Precision.HIGHEST

D_MODEL = 1024
DEPTH = 2
EPS = 1e-6
S5_CH = 512
S5_GROUP = 16
S5_GROUPS = S5_CH // S5_GROUP
S5_STATE = 64
S5_CHUNK = 8
FOX_HEADS = 8
FOX_DH = 64
FOX_W = FOX_HEADS * FOX_DH
LOG2_E = 1.4426950408889634
FOX_VROWS = FOX_DH + 16
FOX_HPS = 8
RET_HEADS = 4
RET_DK = 128
RET_DV = 128
ROPE_BASE = 10000.0
GLA_HEADS = 4
GLA_DK = 64
GLA_DV = 128
GLA_RANK = 16
GLA_TAU = 16.0
GLA_CHUNK = 64
GLA_SUB = 16
N_EXPERTS = 32
TOP_K = 4
D_EXPERT = 1024
SWIGLU_LIMIT = 7.0
SWIGLU_ALPHA = 1.702
MOE_BLOCK = 256
ROW_DMA_UNROLL = 8

LANES = 128
SUBLANES = 8
VMEM_LIMIT = 56 * 1024 * 1024


def _cparams(*sem):
    return pltpu.CompilerParams(dimension_semantics=sem, vmem_limit_bytes=VMEM_LIMIT)


def _bdot(a, b):
    return jnp.dot(a, b, preferred_element_type=F32)


def _dot_nt(a, b):
    return lax.dot_general(a, b, (((1,), (1,)), ((), ())), preferred_element_type=F32)


def _dot_tn(a, b):
    return lax.dot_general(a, b, (((0,), (0,)), ((), ())), preferred_element_type=F32)


def _split3(x):
    hi = x.astype(BF16)
    r = x - hi.astype(F32)
    mid = r.astype(BF16)
    lo = (r - mid.astype(F32)).astype(BF16)
    return hi, mid, lo


def _dot01(m01, x):
    hi, mid, lo = _split3(x)
    return _bdot(m01, hi) + _bdot(m01, mid) + _bdot(m01, lo)


def _lower_tri(n, strict=False):
    r = lax.broadcasted_iota(jnp.int32, (n, n), 0)
    c = lax.broadcasted_iota(jnp.int32, (n, n), 1)
    return ((r > c) if strict else (r >= c)).astype(BF16)


def _log_sigmoid(x):
    return jnp.minimum(x, 0.0) - jnp.log1p(jnp.exp(-jnp.abs(x)))


def _sigmoid(x):
    return 1.0 / (1.0 + jnp.exp(-x))


def _silu(x):
    return x * _sigmoid(x)


def _rms(x):
    return x * lax.rsqrt(jnp.mean(x * x, axis=-1, keepdims=True) + EPS)


def _mod_kernel(c_ref, w_ref, b_ref, o_ref):
    s = _silu(c_ref[...])
    w = w_ref[0]
    s_hi, w_hi = s.astype(BF16), w.astype(BF16)
    s_lo = (s - s_hi.astype(F32)).astype(BF16)
    w_lo = (w - w_hi.astype(F32)).astype(BF16)
    o_ref[0] = _bdot(s_hi, w_hi) + _bdot(s_lo, w_hi) + _bdot(s_hi, w_lo) + b_ref[0]


def _modulation(c, ada_w, ada_b):
    depth, d, n = ada_w.shape
    bsz = c.shape[0]
    tn = D_MODEL
    return pl.pallas_call(
        _mod_kernel,
        grid=(depth, n // tn),
        in_specs=[pl.BlockSpec((bsz, d), lambda l, j: (0, 0)),
                  pl.BlockSpec((1, d, tn), lambda l, j: (l, 0, j)),
                  pl.BlockSpec((1, 1, tn), lambda l, j: (l, 0, j))],
        out_specs=pl.BlockSpec((1, bsz, tn), lambda l, j: (l, 0, j)),
        out_shape=jax.ShapeDtypeStruct((depth, bsz, n), F32),
        compiler_params=_cparams("parallel", "parallel"),
        name="adaln_mod",
    )(c, ada_w, ada_b.reshape(depth, 1, n))


def _mod_spec(chunk, tiles_per_seq):
    return pl.BlockSpec((1, 1, D_MODEL), lambda i: (i // tiles_per_seq, 0, chunk))


def _prenorm(x, g_ref, sc_ref, sh_ref):
    return _rms(x) * (g_ref[...] * (1.0 + sc_ref[0])) + sh_ref[0]


def _in_even_kernel(x_ref, sh_ref, sc_ref, g_ref, w_ref, bf_ref,
                    u_ref, q_ref, k_ref, vt_ref, carry_ref, *, tiles_per_seq):
    i = pl.program_id(0)
    tm = x_ref.shape[0]
    h = _prenorm(x_ref[...], g_ref, sc_ref, sh_ref)
    z = _bdot(h.astype(BF16), w_ref[...])
    u_ref[...] = z[:, 0:S5_CH]
    tail = jnp.concatenate([jnp.ones((1, tm), F32), jnp.zeros((FOX_VROWS - FOX_DH - 1, tm), F32)], axis=0)
    for pr in range(FOX_HEADS // 2):
        c0 = S5_CH + 2 * FOX_W + pr * LANES
        v_pair = z[:, c0:c0 + LANES].T
        for hh in range(2):
            r0 = (2 * pr + hh) * FOX_VROWS
            vt_ref[r0:r0 + FOX_VROWS, :] = jnp.concatenate(
                [v_pair[hh * FOX_DH:(hh + 1) * FOX_DH], tail], axis=0).astype(BF16)
    ls = _log_sigmoid(z[:, S5_CH + 3 * FOX_W:] + bf_ref[...])

    @pl.when(i % tiles_per_seq == 0)
    def _():
        carry_ref[...] = jnp.zeros_like(carry_ref)

    cum = _dot01(_lower_tri(tm), ls) + carry_ref[...]
    carry_ref[...] = cum[tm - 1:tm, :]

    lane = lax.broadcasted_iota(jnp.int32, (1, LANES), 1)
    feat = lane < FOX_DH
    ones = jnp.where(lane < FOX_DH + 3, 1.0, 0.0)
    for hd in range(FOX_HEADS):
        blk = (hd * FOX_DH) // LANES * LANES
        qs = z[:, S5_CH + blk:S5_CH + blk + LANES] * (FOX_DH ** -0.5 * LOG2_E)
        ks = z[:, S5_CH + FOX_W + blk:S5_CH + FOX_W + blk + LANES]
        if (hd * FOX_DH) % LANES:
            qs = pltpu.roll(qs, LANES - FOX_DH, 1)
            ks = pltpu.roll(ks, LANES - FOX_DH, 1)
        nf = jnp.broadcast_to(-LOG2_E * cum[:, hd:hd + 1], (tm, LANES))
        hi = nf.astype(BF16).astype(F32)
        mid = (nf - hi).astype(BF16).astype(F32)
        lo = nf - hi - mid
        bias = jnp.where(lane == FOX_DH, hi, jnp.where(lane == FOX_DH + 1, mid,
                                                       jnp.where(lane == FOX_DH + 2, lo, 0.0)))
        q_ref[:, hd * LANES:(hd + 1) * LANES] = jnp.where(feat, qs, ones).astype(BF16)
        k_ref[:, hd * LANES:(hd + 1) * LANES] = jnp.where(feat, ks, bias).astype(BF16)


def _in_even(x2, mod3, gain, w_in, b_f, seq, tm):
    t, d = x2.shape
    tiles_per_seq = seq // tm
    nw = S5_CH + 3 * FOX_W
    w = jnp.concatenate([w_in[:, :nw], jnp.pad(w_in[:, nw:], ((0, 0), (0, LANES - FOX_HEADS)))],
                        axis=1).astype(BF16)
    bf = jnp.pad(b_f, (0, LANES - FOX_HEADS)).reshape(1, LANES)
    tok = lambda n: pl.BlockSpec((tm, n), lambda i: (i, 0))
    full = lambda a: pl.BlockSpec(a.shape, lambda i: (0,) * a.ndim)
    return pl.pallas_call(
        functools.partial(_in_even_kernel, tiles_per_seq=tiles_per_seq),
        grid=(t // tm,),
        in_specs=[tok(d), _mod_spec(0, tiles_per_seq), _mod_spec(1, tiles_per_seq),
                  full(gain), full(w), full(bf)],
        out_specs=[tok(S5_CH), tok(FOX_HEADS * LANES), tok(FOX_HEADS * LANES),
                   pl.BlockSpec((FOX_HEADS * FOX_VROWS, tm), lambda i: (i // tiles_per_seq, i % tiles_per_seq))],
        out_shape=[jax.ShapeDtypeStruct((t, S5_CH), F32),
                   jax.ShapeDtypeStruct((t, FOX_HEADS * LANES), BF16),
                   jax.ShapeDtypeStruct((t, FOX_HEADS * LANES), BF16),
                   jax.ShapeDtypeStruct((t // seq * FOX_HEADS * FOX_VROWS, seq), BF16)],
        scratch_shapes=[pltpu.VMEM((1, LANES), F32)],
        compiler_params=_cparams("arbitrary"),
        name="in_proj_even",
    )(x2, mod3, mod3, gain, w, bf)


S5_TILE_GROUPS = LANES // S5_GROUP
S5_SEQ_PARTS = 4


def _s5_kernel(u_ref, wt_ref, ws_ref, wc_ref, a_ref, y_ref, x_scr, e_scr, hp_scr, h_scr, *, nb, ncl):
    ell = S5_CHUNK
    sw = S5_TILE_GROUPS * 2 * S5_STATE

    @pl.when(pl.program_id(1) == 0)
    def _():
        h_scr[...] = jnp.zeros_like(h_scr)

    blk = nb * SUBLANES
    for hi in range(ncl // SUBLANES):
        for b in range(nb):
            for t in range(ell):
                x_scr[hi * blk + b * SUBLANES:hi * blk + (b + 1) * SUBLANES, t * LANES:(t + 1) * LANES] = (
                    u_ref[b, pl.ds(hi * SUBLANES * ell + t, SUBLANES, stride=ell), :])
    x = x_scr[...].astype(BF16)
    e = _bdot(x, ws_ref[0])
    tg = S5_TILE_GROUPS
    for j in range(tg):
        e_scr[j] = e[:, j * LANES:(j + 1) * LANES]
        e_scr[tg + j] = pltpu.roll(e[:, j * LANES:(j + 1) * LANES], S5_STATE, 1)
    a1 = jnp.broadcast_to(a_ref[0, 0:1, :], (nb, sw))
    a2 = jnp.broadcast_to(a_ref[0, 1:2, :], (nb, sw))
    a2s = jnp.broadcast_to(a_ref[0, 2:3, :], (nb, sw))

    def body(c, carry):
        h, hs = carry
        rows_c = pl.ds((c // SUBLANES) * blk + c % SUBLANES, nb, stride=SUBLANES)
        for j in range(tg):
            hp_scr[j, rows_c, :] = h[:, j * LANES:(j + 1) * LANES]
        e1 = jnp.concatenate([e_scr[j, rows_c, :] for j in range(tg)], axis=1)
        e2 = jnp.concatenate([e_scr[tg + j, rows_c, :] for j in range(tg)], axis=1)
        return a1 * h + a2 * hs + e1, a1 * hs + a2s * h + e2

    h, hs = lax.fori_loop(0, ncl, body, (h_scr[0], h_scr[1]))
    h_scr[0] = h
    h_scr[1] = hs
    hp = jnp.concatenate([hp_scr[j] for j in range(tg)], axis=1).astype(BF16)
    y = _bdot(x, wt_ref[0]) + _bdot(hp, wc_ref[0])
    for hi in range(ncl // SUBLANES):
        for b in range(nb):
            for t in range(ell):
                y_ref[b, pl.ds(hi * SUBLANES * ell + t, SUBLANES, stride=ell), :] = (
                    y[hi * blk + b * SUBLANES:hi * blk + (b + 1) * SUBLANES, t * LANES:(t + 1) * LANES])


def _s5_tables(lam_re, lam_im, log_dt, b_re, b_im, c_re, c_im):
    ell, p, g = S5_CHUNK, S5_STATE, S5_GROUPS
    lr, li = lam_re.astype(F32), lam_im.astype(F32)
    dt = jnp.exp(log_dt.astype(F32))[:, None]
    mag = jnp.exp(lr * dt)
    a_re, a_im = mag * jnp.cos(li * dt), mag * jnp.sin(li * dt)
    den = lr * lr + li * li
    n_re, n_im = a_re - 1.0, a_im
    z_re = (n_re * lr + n_im * li) / den
    z_im = (n_im * lr - n_re * li) / den
    br, bi = b_re.astype(F32), b_im.astype(F32)
    bb_re = z_re[..., None] * br - z_im[..., None] * bi
    bb_im = z_re[..., None] * bi + z_im[..., None] * br
    j = jnp.arange(ell + 1, dtype=F32)[:, None, None]
    pmag = jnp.exp(j * (lr * dt)[None])
    pr, pi = pmag * jnp.cos(j * (li * dt)[None]), pmag * jnp.sin(j * (li * dt)[None])
    w_re = pr[..., None] * bb_re[None] - pi[..., None] * bb_im[None]
    w_im = pr[..., None] * bb_im[None] + pi[..., None] * bb_re[None]
    cr, ci = c_re.astype(F32), c_im.astype(F32)
    kern = (jnp.einsum('gcp,jgpd->jgcd', cr, w_re[:ell], precision=HIGHEST)
            - jnp.einsum('gcp,jgpd->jgcd', ci, w_im[:ell], precision=HIGHEST))
    s_idx = jnp.arange(ell)[:, None]
    t_idx = jnp.arange(ell)[None, :]
    lag = jnp.clip(t_idx - s_idx, 0, ell - 1)
    toep = kern[lag]
    toep = jnp.where((t_idx >= s_idx)[:, :, None, None, None], toep, 0.0)
    toep = toep.transpose(2, 0, 4, 1, 3)
    rev = jnp.arange(ell - 1, -1, -1)
    st_re = w_re[rev].transpose(1, 0, 3, 2)
    st_im = w_im[rev].transpose(1, 0, 3, 2)
    wst = jnp.concatenate([st_re, st_im], -1)
    p1r, p1i = pr[1:], pi[1:]
    c_hr = cr[None] * p1r[:, :, None, :] - ci[None] * p1i[:, :, None, :]
    c_hi = -cr[None] * p1i[:, :, None, :] - ci[None] * p1r[:, :, None, :]
    cst = jnp.concatenate([c_hr, c_hi], axis=-1).transpose(1, 3, 0, 2)
    al_r, al_i = pr[ell], pi[ell]
    a_rows = jnp.stack([jnp.concatenate([al_r, al_r], -1), jnp.concatenate([-al_i, al_i], -1),
                        jnp.concatenate([al_i, -al_i], -1)], axis=1)

    tg = S5_TILE_GROUPS
    nt = g // tg
    tile = lambda a: a.reshape((nt, tg) + a.shape[1:])
    lane_grp = jnp.arange(LANES) // S5_GROUP
    step_lane_grp = jnp.tile(lane_grp, ell)
    spread_out = (jnp.eye(ell, dtype=BF16)[:, None, :, None, None]
                  * jnp.eye(S5_GROUP, dtype=BF16)[None, :, None, None, :]
                  * jnp.ones((1, 1, 1, tg, 1), BF16)).reshape(ell * S5_GROUP, ell * LANES)
    toep_t = tile(toep).transpose(0, 2, 1, 3, 4, 5).reshape(nt, ell, LANES, ell * S5_GROUP).astype(BF16)
    wt = jnp.einsum('ksrm,mn->ksrn', toep_t, spread_out)
    wt = jnp.where((lane_grp[:, None] == step_lane_grp[None, :])[None, None], wt, 0)
    wt = wt.reshape(nt, ell * LANES, ell * LANES)
    spread_state = jnp.tile(jnp.eye(2 * p, dtype=BF16), (1, tg))
    state_grp = jnp.arange(tg * 2 * p) // (2 * p)
    wst_t = tile(wst).transpose(0, 2, 1, 3, 4).reshape(nt, ell, LANES, 2 * p).astype(BF16)
    ws = jnp.einsum('ksrp,pn->ksrn', wst_t, spread_state)
    ws = jnp.where((lane_grp[:, None] == state_grp[None, :])[None, None], ws, 0)
    ws = ws.reshape(nt, ell * LANES, tg * 2 * p)
    cst_t = tile(cst).reshape(nt, tg * 2 * p, ell * S5_GROUP).astype(BF16)
    wc = jnp.einsum('krm,mn->krn', cst_t, spread_out)
    wc = jnp.where((state_grp[:, None] == step_lane_grp[None, :])[None], wc, 0)
    a_t = tile(a_rows).transpose(0, 2, 1, 3).reshape(nt, 3, tg * 2 * p)
    a_t = jnp.pad(a_t, ((0, 0), (0, SUBLANES - 3), (0, 0)))
    return wt, ws, wc, a_t


def _s5_scan(u, bsz, seq, tables):
    wt, ws, wc, a_t = tables
    ell = S5_CHUNK
    nt = S5_GROUPS // S5_TILE_GROUPS
    parts = S5_SEQ_PARTS if seq % (S5_SEQ_PARTS * ell * 2 * SUBLANES) == 0 else 1
    ncl = seq // parts // ell
    rows = bsz * ncl
    sw = S5_TILE_GROUPS * 2 * S5_STATE
    u3 = u.reshape(bsz, seq, S5_CH)
    io = pl.BlockSpec((bsz, seq // parts, LANES), lambda k, s: (0, s, k))
    per_tile = lambda a: pl.BlockSpec((1,) + a.shape[1:], lambda k, s: (k, 0, 0))
    y = pl.pallas_call(
        functools.partial(_s5_kernel, nb=bsz, ncl=ncl),
        grid=(nt, parts),
        in_specs=[io, per_tile(wt), per_tile(ws), per_tile(wc), per_tile(a_t)],
        out_specs=io,
        out_shape=jax.ShapeDtypeStruct((bsz, seq, S5_CH), F32),
        scratch_shapes=[pltpu.VMEM((rows, ell * LANES), F32),
                        pltpu.VMEM((2 * S5_TILE_GROUPS, rows, LANES), F32),
                        pltpu.VMEM((S5_TILE_GROUPS, rows, LANES), F32),
                        pltpu.VMEM((2, bsz, sw), F32)],
        compiler_params=_cparams("parallel", "arbitrary"),
        name="s5_scan",
    )(u3, wt, ws, wc, a_t)
    return y.reshape(bsz * seq, S5_CH)


def _fox_kernel(q_ref, k_ref, vt_ref, o_ref, *, tq, tk):
    i = pl.program_id(2)
    nh = FOX_HPS
    q = [q_ref[:, h * LANES:(h + 1) * LANES] for h in range(nh)]
    key = lax.broadcasted_iota(jnp.int32, (tk, tq), 0)
    qry = lax.broadcasted_iota(jnp.int32, (tk, tq), 1)
    per_q = tq // tk

    def block(j0, carry, mask, q0=0):
        kj = k_ref[pl.ds(j0, tk), :]
        vtj = vt_ref[:, pl.ds(j0, tk)]
        ss = [_dot_nt(kj[:, h * LANES:(h + 1) * LANES], q[h][q0:]) for h in range(nh)]
        stats = []
        for h in range(nh):
            m = carry[h][0][:, q0:]
            s = ss[h] if mask is None else jnp.where(mask[:, q0:], ss[h], -jnp.inf)
            m_new = jnp.maximum(m, jnp.max(s, axis=0, keepdims=True))
            p = jnp.exp2(s - m_new)
            stats.append((m_new, jnp.exp2(m - m_new), p.astype(BF16)))
        out = []
        for h in range(nh):
            m_new, alpha, p = stats[h]
            acc = alpha * carry[h][1][:, q0:] + _bdot(vtj[h * FOX_VROWS:(h + 1) * FOX_VROWS, :], p)
            if q0:
                m_new = jnp.concatenate([carry[h][0][:, :q0], m_new], axis=1)
                acc = jnp.concatenate([carry[h][1][:, :q0], acc], axis=1)
            out.append((m_new, acc))
        return tuple(out)

    init = tuple((jnp.full((1, tq), -jnp.inf, F32), jnp.zeros((FOX_VROWS, tq), F32)) for _ in range(nh))
    carry = lax.fori_loop(0, i * per_q, lambda j, c: block(pl.multiple_of(j * tk, tk), c, None), init)
    for d in range(per_q):
        carry = block(pl.multiple_of(i * tq + d * tk, tk), carry, key + d * tk <= qry, d * tk)
    for g in range(nh // 2):
        o_t = jnp.concatenate([carry[h][1][:FOX_DH] / carry[h][1][FOX_DH:FOX_DH + 1]
                               for h in (2 * g, 2 * g + 1)], axis=0)
        o_ref[:, g * LANES:(g + 1) * LANES] = o_t.T.astype(o_ref.dtype)


def _fox(q_aug, k_aug, v_t, bsz, seq, tq, tk):
    t = q_aug.shape[0]
    nh = FOX_HPS
    groups = FOX_HEADS // nh
    nq = seq // tq
    return pl.pallas_call(
        functools.partial(_fox_kernel, tq=tq, tk=tk),
        grid=(bsz, groups, nq),
        in_specs=[pl.BlockSpec((tq, nh * LANES), lambda b, p, i: (b * nq + i, p)),
                  pl.BlockSpec((seq, nh * LANES), lambda b, p, i: (b, p)),
                  pl.BlockSpec((nh * FOX_VROWS, seq), lambda b, p, i: (b * groups + p, 0))],
        out_specs=pl.BlockSpec((tq, nh * FOX_DH), lambda b, p, i: (b * nq + i, p)),
        out_shape=jax.ShapeDtypeStruct((t, FOX_W), BF16),
        compiler_params=_cparams("parallel", "parallel", "arbitrary"),
        name="fox_attention",
    )(q_aug, k_aug, v_t)


_ODD_COLS = (("rq", 512), ("rk", 512), ("rv", 512), ("rg", 512), ("gq", 256), ("gk", 256),
             ("gv", 512), ("gr", 512), ("glr", LANES))


def _odd_offsets():
    off, out = 0, {}
    for name, w in _ODD_COLS:
        out[name] = (off, off + w)
        off += w
    return out, off


def _in_odd_kernel(x_ref, sh_ref, sc_ref, g_ref, w_ref, cos_ref, sin_ref, wup_ref, bg_ref,
                   rq_ref, rk_ref, rv_ref, sg_ref, gq_ref, gk_ref, gv_ref, sr_ref, la_ref):
    h = _prenorm(x_ref[...], g_ref, sc_ref, sh_ref)
    z = _bdot(h.astype(BF16), w_ref[...])
    off, _ = _odd_offsets()
    col = lambda n: z[:, off[n][0]:off[n][1]]
    cos, sin = cos_ref[...], sin_ref[...]

    def rope(t, scale):
        heads = []
        for hd in range(RET_HEADS):
            th = t[:, hd * RET_DK:(hd + 1) * RET_DK]
            heads.append((th * cos + pltpu.roll(th, RET_DK // 2, 1) * sin) * scale)
        return jnp.concatenate(heads, axis=1).astype(BF16)

    rq_ref[...] = rope(col("rq"), 1.0)
    rk_ref[...] = rope(col("rk"), RET_DK ** -0.5)
    rv_ref[...] = col("rv").astype(BF16)
    sg_ref[...] = _silu(col("rg"))
    gq_ref[...] = col("gq") * (GLA_DK ** -0.5)
    gk_ref[...] = col("gk")
    gv_ref[...] = col("gv").astype(BF16)
    sr_ref[...] = _silu(col("gr"))
    glr = col("glr")
    g_hi = glr.astype(BF16)
    g_lo = (glr - g_hi.astype(F32)).astype(BF16)
    nqk = GLA_HEADS * GLA_DK
    gw = _bdot(g_hi, wup_ref[...])
    gate = gw[:, :nqk] + gw[:, nqk:] + _bdot(g_lo, wup_ref[:, :nqk]) + bg_ref[...]
    la_ref[...] = _log_sigmoid(gate) * (1.0 / GLA_TAU)


def _in_odd(x2, mod3, gain, w_in, w_up, b_gate, seq, tm):
    t, d = x2.shape
    tps = seq // tm
    ref_w = (512, 512, 512, 512, 256, 256, 512, GLA_RANK, 512)
    starts = [0]
    for wd in ref_w:
        starts.append(starts[-1] + wd)
    seg = lambda j: w_in[:, starts[j]:starts[j + 1]]
    w = jnp.concatenate([seg(0), seg(1), seg(2), seg(3), seg(4), seg(5), seg(6), seg(8),
                         jnp.pad(seg(7), ((0, 0), (0, LANES - GLA_RANK)))], axis=1).astype(BF16)
    wup32 = jnp.pad(w_up.astype(F32), ((0, LANES - GLA_RANK), (0, 0)))
    wup_hi = wup32.astype(BF16)
    wup = jnp.concatenate([wup_hi, (wup32 - wup_hi.astype(F32)).astype(BF16)], axis=1)
    bg = b_gate.reshape(1, -1).astype(F32)
    half = RET_DK // 2
    inv = ROPE_BASE ** (-jnp.arange(half, dtype=F32) / half)
    ang = jnp.arange(seq, dtype=F32)[:, None] * inv[None, :]
    cos = jnp.concatenate([jnp.cos(ang), jnp.cos(ang)], axis=1)
    sin = jnp.concatenate([-jnp.sin(ang), jnp.sin(ang)], axis=1)
    tok = lambda n: pl.BlockSpec((tm, n), lambda i: (i, 0))
    full = lambda a: pl.BlockSpec(a.shape, lambda i: (0,) * a.ndim)
    pos = pl.BlockSpec((tm, RET_DK), lambda i: (i % tps, 0))
    widths = (512, 512, 512, 512, 256, 256, 512, 512, 256)
    dtypes = (BF16, BF16, BF16, F32, F32, F32, BF16, F32, F32)
    return pl.pallas_call(
        _in_odd_kernel,
        grid=(t // tm,),
        in_specs=[tok(d), _mod_spec(0, tps), _mod_spec(1, tps), full(gain), full(w), pos, pos,
                  full(wup), full(bg)],
        out_specs=[tok(n) for n in widths],
        out_shape=[jax.ShapeDtypeStruct((t, n), dt) for n, dt in zip(widths, dtypes)],
        compiler_params=_cparams("parallel"),
        name="in_proj_odd",
    )(x2, mod3, mod3, gain, w, cos, sin, wup, bg)


RET_CHUNK = 256


def _ret_kernel(q_ref, k_ref, v_ref, sg_ref, dm_ref, xi_ref, zeta_ref, gl_ref, y_ref, st_ref):
    @pl.when(pl.program_id(1) == 0)
    def _():
        st_ref[...] = jnp.zeros_like(st_ref)

    nb = q_ref.shape[0]
    chains = [(bl, h) for bl in range(nb) for h in range(RET_HEADS)]
    col = lambda h: slice(h * RET_DK, (h + 1) * RET_DK)
    q = [q_ref[bl, :, col(h)] for bl, h in chains]
    k = [k_ref[bl, :, col(h)] for bl, h in chains]
    v = [v_ref[bl, :, col(h)] for bl, h in chains]
    st = [st_ref[c] for c in range(len(chains))]
    s = [_dot_nt(q[c], k[c]) for c in range(len(chains))]
    inter = [_bdot((q[c].astype(F32) * xi_ref[h]).astype(BF16), st[c].astype(BF16))
             for c, (bl, h) in enumerate(chains)]
    upd = [_dot_tn((k[c].astype(F32) * zeta_ref[h]).astype(BF16), v[c]) for c, (bl, h) in enumerate(chains)]
    for c, (bl, h) in enumerate(chains):
        o = _bdot((s[c] * dm_ref[h]).astype(BF16), v[c]) + inter[c]
        st_ref[c] = gl_ref[h, 0:1, :] * st[c] + upd[c]
        y_ref[bl, :, col(h)] = (sg_ref[bl, :, col(h)] * _rms(o)).astype(y_ref.dtype)


RET_BATCHES = 4


def _retention(rq, rk, rv, sg, bsz, seq):
    t = rq.shape[0]
    ell = min(RET_CHUNK, seq)
    nc = seq // ell
    log_g = jnp.log(1.0 - jnp.exp2(-5.0 - jnp.arange(RET_HEADS, dtype=F32)))
    idx = jnp.arange(ell, dtype=F32)
    rel = idx[:, None] - idx[None, :]
    dmat = jnp.where(rel >= 0, jnp.exp(log_g[:, None, None] * jnp.maximum(rel, 0.0)), 0.0)
    lanes = lambda a: jnp.broadcast_to(a[..., None], a.shape + (RET_DK,))
    xi = lanes(jnp.exp(log_g[:, None] * (idx + 1.0)))
    zeta = lanes(jnp.exp(log_g[:, None] * (ell - 1.0 - idx)))
    gl = jnp.broadcast_to(jnp.exp(log_g * ell)[:, None, None], (RET_HEADS, SUBLANES, RET_DV))
    nb = min(RET_BATCHES, bsz)
    width = RET_HEADS * RET_DK
    r3 = lambda a: a.reshape(bsz, seq, width)
    blk = pl.BlockSpec((nb, ell, width), lambda b, c: (b, c, 0))
    full = lambda a: pl.BlockSpec(a.shape, lambda b, c: (0, 0, 0))
    y = pl.pallas_call(
        _ret_kernel,
        grid=(bsz // nb, nc),
        in_specs=[blk, blk, blk, blk, full(dmat), full(xi), full(zeta), full(gl)],
        out_specs=blk,
        out_shape=jax.ShapeDtypeStruct((bsz, seq, width), BF16),
        scratch_shapes=[pltpu.VMEM((nb * RET_HEADS, RET_DK, RET_DV), F32)],
        compiler_params=_cparams("parallel", "arbitrary"),
        name="retention",
    )(r3(rq), r3(rk), r3(rv), r3(sg), dmat, xi, zeta, gl)
    return y.reshape(t, width)


def _gla_kernel(q_ref, k_ref, la_ref, v_ref, sg_ref, y_ref, st_ref, b_scr, v_scr, p_scr, r_scr):
    @pl.when(pl.program_id(1) == 0)
    def _():
        st_ref[...] = jnp.zeros_like(st_ref)

    ell, sub = GLA_CHUNK, GLA_SUB
    n_sub = ell // sub
    nb = q_ref.shape[0]
    pairs = GLA_HEADS // 2
    streams = [(bl, p) for bl in range(nb) for p in range(pairs)]
    lane = lax.broadcasted_iota(jnp.int32, (1, LANES), 1)
    first = lane < GLA_DK
    head = (first, jnp.logical_not(first))
    pick = lambda h, a: jnp.where(head[h], a, 0.0).astype(BF16)
    tri = _lower_tri(ell)
    tau = lax.broadcasted_iota(jnp.int32, (sub, LANES), 0)
    row_of, per_sub = [], 0
    for s_ in range(sub):
        row_of.append(per_sub)
        per_sub += sub - (s_ // SUBLANES) * SUBLANES
    rsub = lax.broadcasted_iota(jnp.int32, (LANES, 2 * LANES), 0)
    csub = lax.broadcasted_iota(jnp.int32, (LANES, 2 * LANES), 1)
    ind = ((rsub < GLA_DK) == (csub < LANES)).astype(BF16)

    val = {}
    for sid, (bl, p) in enumerate(streams):
        qk = slice(p * LANES, (p + 1) * LANES)
        q, k = q_ref[bl, :, qk], k_ref[bl, :, qk]
        b = _dot01(tri, la_ref[bl, :, qk])
        b_scr[sid] = b
        v_bf = v_ref[bl, :, p * 2 * GLA_DV:(p + 1) * 2 * GLA_DV]
        v_scr[sid] = v_bf.astype(F32)
        st = st_ref[sid]
        val[sid] = dict(q=q, k=k, b=b, st=st, vh=[v_bf[:, h * GLA_DV:(h + 1) * GLA_DV] for h in range(2)])

    for sid in val:
        d = val[sid]
        qe = d["q"] * jnp.exp(d["b"])
        st_bf = d["st"].astype(BF16)
        d["o"] = [_dot_nt(pick(h, qe), st_bf) for h in range(2)]

    row = lax.broadcasted_iota(jnp.int32, (ell, LANES), 0)
    for sid in val:
        d = val[sid]
        q, k, b = d["q"], d["k"], d["b"]
        qa, ka = [], []
        for i in range(1, n_sub):
            lo = i * sub
            ref_row = b[lo - 1:lo, :]
            in_i = (row >= lo) & (row < lo + sub)
            qa.append(jnp.where(in_i, q * jnp.exp(jnp.minimum(b - ref_row, 0.0)), 0.0))
            ka.append(jnp.where(row < lo, k * jnp.exp(jnp.minimum(ref_row - b, 0.0)), 0.0))
        k_cat = jnp.concatenate(ka, axis=1).astype(BF16)
        d["a_off"] = [_dot_nt(jnp.concatenate([pick(h, x) for x in qa], axis=1), k_cat) for h in range(2)]
    for sid in val:
        d = val[sid]
        d["off"] = [_bdot(d["a_off"][h].astype(BF16), d["vh"][h]) for h in range(2)]

    for sid, (bl, p) in enumerate(streams):
        d = val[sid]
        q, b = d["q"], d["b"]
        for i in range(n_sub):
            lo = i * sub
            qi, bi = q[lo:lo + sub], b[lo:lo + sub]
            for s in range(sub):
                k_row = k_ref[bl, pl.ds(lo + s, 1), p * LANES:(p + 1) * LANES]
                b_row = b_scr[sid, pl.ds(lo + s, 1), :]
                r0 = (s // SUBLANES) * SUBLANES
                w = jnp.exp(jnp.minimum(bi[r0:] - b_row, 0.0))
                tile_s = jnp.where(tau[r0:] >= s, qi[r0:] * k_row * w, 0.0)
                p_scr[sid, pl.ds(i * per_sub + row_of[s], sub - r0), :] = tile_s
    for sid in val:
        r_scr[sid] = _bdot(p_scr[sid].astype(BF16), ind)
    for sid in val:
        diag = [[], []]
        for i in range(n_sub):
            lo = i * sub
            for h in range(2):
                acc = [jnp.zeros((SUBLANES, GLA_DV), F32) for _ in range(sub // SUBLANES)]
                for s in range(sub):
                    v_row = v_scr[sid, pl.ds(lo + s, 1), h * GLA_DV:(h + 1) * GLA_DV]
                    for part in range(s // SUBLANES, sub // SUBLANES):
                        rows = pl.ds(i * per_sub + row_of[s] + (part - s // SUBLANES) * SUBLANES, SUBLANES)
                        acc[part] = acc[part] + r_scr[sid, rows, h * LANES:(h + 1) * LANES] * v_row
                diag[h].append(jnp.concatenate(acc, axis=0))
        val[sid]["diag"] = diag

    for sid, (bl, p) in enumerate(streams):
        d = val[sid]
        b_last = d["b"][ell - 1:ell, :]
        kh = (d["k"] * jnp.exp(b_last - d["b"])).astype(BF16)
        upd = [_dot_tn(d["vh"][h], kh) for h in range(2)]
        st_ref[sid] = d["st"] * jnp.exp(b_last) + jnp.where(first, upd[0], upd[1])
        for h in range(2):
            oh = d["o"][h] + d["off"][h] + jnp.concatenate(d["diag"][h], axis=0)
            cols = slice((2 * p + h) * GLA_DV, (2 * p + h + 1) * GLA_DV)
            y_ref[bl, :, cols] = (sg_ref[bl, :, cols] * _rms(oh)).astype(y_ref.dtype)


GLA_BATCHES = 4


def _gla(gq, gk, la, gv, sr, bsz, seq):
    t = gq.shape[0]
    ell = GLA_CHUNK
    nc = seq // ell
    nb = min(GLA_BATCHES, bsz)
    ns = nb * (GLA_HEADS // 2)
    prod_rows = (ell // GLA_SUB) * sum(GLA_SUB - (s // SUBLANES) * SUBLANES for s in range(GLA_SUB))
    r3 = lambda a: a.reshape(bsz, seq, a.shape[1])
    spec = lambda w: pl.BlockSpec((nb, ell, w), lambda b, c: (b, c, 0))
    wq, wv = GLA_HEADS * GLA_DK, GLA_HEADS * GLA_DV
    y = pl.pallas_call(
        _gla_kernel,
        grid=(bsz // nb, nc),
        in_specs=[spec(wq), spec(wq), spec(wq), spec(wv), spec(wv)],
        out_specs=spec(wv),
        out_shape=jax.ShapeDtypeStruct((bsz, seq, wv), BF16),
        scratch_shapes=[pltpu.VMEM((ns, GLA_DV, LANES), F32),
                        pltpu.VMEM((ns, ell, LANES), F32),
                        pltpu.VMEM((ns, ell, 2 * GLA_DV), F32),
                        pltpu.VMEM((ns, prod_rows, LANES), F32),
                        pltpu.VMEM((ns, prod_rows, 2 * LANES), F32)],
        compiler_params=_cparams("parallel", "arbitrary"),
        name="gla",
    )(r3(gq), r3(gk), r3(la), r3(gv), r3(sr))
    return y.reshape(t, wv)


ROUTE_IDX, ROUTE_GATE, ROUTE_RANK = 0, SUBLANES, 2 * SUBLANES


def _post_tail(m, x_ref, g1_ref, gpost_ref, sh2_ref, sc2_ref, gpre_ref, rw_ref, rb_ref,
               x1_ref, h2_ref, route_ref, plan_ref, cnt_ref, carry_ref):
    i = pl.program_id(0)
    tm = m.shape[0]
    x1 = x_ref[...] + g1_ref[0] * (_rms(m) * gpost_ref[...])
    x1_ref[...] = x1
    h2 = _rms(x1) * (gpre_ref[...] * (1.0 + sc2_ref[0])) + sh2_ref[0]
    for c in range(D_MODEL // LANES):
        h2_ref[pl.ds(c, tm, stride=SUBLANES), :] = h2[:, c * LANES:(c + 1) * LANES]

    ne = N_EXPERTS
    h_hi = h2.astype(BF16)
    h_lo = (h2 - h_hi.astype(F32)).astype(BF16)
    w_both = rw_ref[...]
    hw = _dot_nt(w_both, h_hi)
    bias = jnp.concatenate([rb_ref[...]] * (tm // LANES), axis=1)
    work = hw[:ne] + hw[ne:] + _dot_nt(w_both[:ne], h_lo) + bias
    esub = lax.broadcasted_iota(jnp.int32, (ne, tm), 0).astype(F32)
    onehot = jnp.zeros((ne, tm), F32)
    hits, vals, idxs = [], [], []
    for k in range(TOP_K):
        mx = jnp.max(work, axis=0, keepdims=True)
        idx = jnp.min(jnp.where(work == mx, esub, float(ne)), axis=0, keepdims=True)
        hit = esub == idx
        hits.append(hit)
        vals.append(mx)
        idxs.append(idx)
        onehot = onehot + hit.astype(F32)
        work = jnp.where(hit, -jnp.inf, work)
    es = [jnp.exp(v - vals[0]) for v in vals]
    inv = 1.0 / (es[0] + es[1] + es[2] + es[3])
    gates = [e * inv for e in es]

    @pl.when(i == 0)
    def _():
        carry_ref[...] = jnp.zeros_like(carry_ref)

    r = lax.broadcasted_iota(jnp.int32, (tm, tm), 0)
    c = lax.broadcasted_iota(jnp.int32, (tm, tm), 1)
    carry = carry_ref[...]
    before = _bdot(onehot.astype(BF16), (r < c).astype(BF16)) + jnp.concatenate([carry[:ne]] * (tm // LANES), axis=1)
    ranks = [jnp.sum(jnp.where(hits[k], before, 0.0), axis=0, keepdims=True) for k in range(TOP_K)]
    carry = carry + jnp.concatenate(
        [jnp.broadcast_to(jnp.sum(onehot, axis=1, keepdims=True), (ne, LANES)),
         jnp.zeros((LANES - ne, LANES), F32)], axis=0)
    carry_ref[...] = carry
    cnt_ref[...] = carry.T[:SUBLANES]

    sub = lax.broadcasted_iota(jnp.int32, (SUBLANES, tm), 0)

    def rows(vs):
        out = jnp.zeros((SUBLANES, tm), F32)
        for k, v in enumerate(vs):
            out = jnp.where(sub == k, jnp.broadcast_to(v, (SUBLANES, tm)), out)
        return out

    rec = jnp.concatenate([rows(idxs), rows(gates), rows(ranks), jnp.zeros((LANES - 3 * SUBLANES, tm), F32)], axis=0)
    route_ref[...] = rec.T
    plan_ref[...] = rows(idxs + ranks)


def _out_even_kernel(ys_ref, u_ref, yb_ref, d_ref, gw_ref, gb_ref, wa_ref, wb_ref, *rest):
    y = ys_ref[...] + d_ref[...] * u_ref[...]
    g = jax.nn.gelu(y)
    ya = g * _sigmoid(_bdot(g.astype(BF16), gw_ref[...]) + gb_ref[...])
    m = _bdot(ya.astype(BF16), wa_ref[...]) + _bdot(yb_ref[...], wb_ref[...])
    _post_tail(m, *rest)


def _out_odd_kernel(yc_ref, yd_ref, wa_ref, wb_ref, *rest):
    m = _bdot(yc_ref[...], wa_ref[...]) + _bdot(yd_ref[...], wb_ref[...])
    _post_tail(m, *rest)


def _mixer_out(body, mix_args, mix_specs, x2, mod3, g_post, g_pre, router_w, router_b, seq, tm):
    t, d = x2.shape
    tps = seq // tm
    rw32 = router_w.astype(F32).T
    rw_hi = rw32.astype(BF16)
    rw = jnp.concatenate([rw_hi, (rw32 - rw_hi.astype(F32)).astype(BF16)], axis=0)
    rb = jnp.broadcast_to(router_b.astype(F32)[:, None], (N_EXPERTS, LANES))
    tok = lambda n: pl.BlockSpec((tm, n), lambda i: (i, 0))
    full = lambda a: pl.BlockSpec(a.shape, lambda i: (0,) * a.ndim)
    tail_args = [x2, mod3, g_post, mod3, mod3, g_pre, rw, rb]
    tail_specs = [tok(d), _mod_spec(2, tps), full(g_post), _mod_spec(3, tps), _mod_spec(4, tps),
                  full(g_pre), full(rw), full(rb)]
    return pl.pallas_call(
        body,
        grid=(t // tm,),
        in_specs=mix_specs + tail_specs,
        out_specs=[tok(d), pl.BlockSpec((tm * SUBLANES, LANES), lambda i: (i, 0)),
                   tok(LANES), pl.BlockSpec((SUBLANES, tm), lambda i: (0, i)),
                   pl.BlockSpec((SUBLANES, LANES), lambda i: (0, 0))],
        out_shape=[jax.ShapeDtypeStruct((t, d), F32),
                   jax.ShapeDtypeStruct((t * SUBLANES, LANES), F32),
                   jax.ShapeDtypeStruct((t, LANES), F32),
                   jax.ShapeDtypeStruct((SUBLANES, t), F32),
                   jax.ShapeDtypeStruct((SUBLANES, LANES), F32)],
        scratch_shapes=[pltpu.VMEM((LANES, LANES), F32)],
        compiler_params=_cparams("arbitrary"),
        name="mixer_out_router",
    )(*mix_args, *tail_args)


def _out_even(ys, u, yb, d_skip, glu_w, glu_b, w_out, *tail, seq, tm):
    tok = lambda n: pl.BlockSpec((tm, n), lambda i: (i, 0))
    full = lambda a: pl.BlockSpec(a.shape, lambda i: (0,) * a.ndim)
    args = [ys, u, yb, d_skip.reshape(1, -1), glu_w.astype(BF16), glu_b.reshape(1, -1),
            w_out[:S5_CH].astype(BF16), w_out[S5_CH:].astype(BF16)]
    specs = [tok(S5_CH), tok(S5_CH), tok(FOX_W)] + [full(a) for a in args[3:]]
    return _mixer_out(_out_even_kernel, args, specs, *tail, seq, tm)


def _out_odd(yc, yd, w_out, *tail, seq, tm):
    tok = lambda n: pl.BlockSpec((tm, n), lambda i: (i, 0))
    full = lambda a: pl.BlockSpec(a.shape, lambda i: (0,) * a.ndim)
    nc = yc.shape[1]
    args = [yc, yd, w_out[:nc].astype(BF16), w_out[nc:].astype(BF16)]
    specs = [tok(nc), tok(yd.shape[1])] + [full(a) for a in args[2:]]
    return _mixer_out(_out_odd_kernel, args, specs, *tail, seq, tm)


def _route_kernel(plan_ref, cnt_ref, dest_ref, blk_ref, meta_ref):
    tm = plan_ref.shape[1]
    cnt = cnt_ref[...]
    padded = jnp.floor((cnt + (MOE_BLOCK - 1.0)) * (1.0 / MOE_BLOCK)) * MOE_BLOCK
    r = lax.broadcasted_iota(jnp.int32, (LANES, LANES), 0)
    c = lax.broadcasted_iota(jnp.int32, (LANES, LANES), 1)
    hi, mid, lo = _split3(padded)
    incl = (r <= c).astype(BF16)
    pad_end = _bdot(hi, incl) + _bdot(mid, incl) + _bdot(lo, incl)
    pad_start = pad_end - padded
    start_col = jnp.broadcast_to(pad_start[0:1, :], (LANES, LANES)).T[:N_EXPERTS]
    start = jnp.concatenate([start_col] * (tm // LANES), axis=1)
    esub = lax.broadcasted_iota(jnp.int32, (N_EXPERTS, tm), 0).astype(F32)
    plan = plan_ref[...]
    sub = lax.broadcasted_iota(jnp.int32, (SUBLANES, tm), 0)
    dest = jnp.zeros((SUBLANES, tm), F32)
    for k in range(TOP_K):
        st = jnp.sum(jnp.where(esub == plan[k:k + 1, :], start, 0.0), axis=0, keepdims=True)
        dest = jnp.where(sub == k, jnp.broadcast_to(st + plan[TOP_K + k:TOP_K + k + 1, :], (SUBLANES, tm)), dest)
    dest_ref[...] = dest.astype(jnp.int32)

    nb = blk_ref.shape[1]
    end_col = jnp.sum(jnp.where(r == c, jnp.broadcast_to(pad_end[0:1, :], (LANES, LANES)), 0.0),
                      axis=-1, keepdims=True)
    jpos = lax.broadcasted_iota(jnp.int32, (LANES, nb), 1).astype(F32) * MOE_BLOCK
    esub = lax.broadcasted_iota(jnp.int32, (LANES, nb), 0)
    done = jnp.where((end_col <= jpos) & (esub < N_EXPERTS), 1.0, 0.0)
    be = jnp.minimum(jnp.sum(done, axis=0, keepdims=True), N_EXPERTS - 1.0)
    blk_ref[...] = jnp.broadcast_to(be, blk_ref.shape).astype(jnp.int32)
    lane1 = lax.broadcasted_iota(jnp.int32, (SUBLANES, LANES), 1)
    n_valid = jnp.sum(jnp.where(lane1 == N_EXPERTS - 1, pad_end, 0.0), axis=-1, keepdims=True) * (1.0 / MOE_BLOCK)
    sub1 = lax.broadcasted_iota(jnp.int32, (SUBLANES, LANES), 0)
    meta = jnp.where(sub1 == 0, pad_start + cnt, jnp.where(sub1 == 1, pad_end, jnp.broadcast_to(n_valid, (SUBLANES, LANES))))
    meta_ref[...] = meta.astype(jnp.int32)


def _route(plan, cnt, n_blocks, tm):
    t = plan.shape[1]
    nb_pad = -(-n_blocks // LANES) * LANES
    tok = pl.BlockSpec((SUBLANES, tm), lambda i: (0, i))
    fix = lambda n: pl.BlockSpec((SUBLANES, n), lambda i: (0, 0))
    return pl.pallas_call(
        _route_kernel,
        grid=(t // tm,),
        in_specs=[tok, fix(LANES)],
        out_specs=[tok, fix(nb_pad), fix(LANES)],
        out_shape=[jax.ShapeDtypeStruct((SUBLANES, t), jnp.int32),
                   jax.ShapeDtypeStruct((SUBLANES, nb_pad), jnp.int32),
                   jax.ShapeDtypeStruct((SUBLANES, LANES), jnp.int32)],
        compiler_params=_cparams("arbitrary"),
        name="route_plan",
    )(plan, cnt)


def _dispatch_kernel(pad_ref, dest_ref, h_ref, xb_ref, zero_ref, sem_z, sem_s):
    i = pl.program_id(0)
    tm = h_ref.shape[0]

    @pl.when(i == 0)
    def _():
        zero_ref[...] = jnp.zeros_like(zero_ref)
        sizes = [1 << b for b in range(int(math.log2(MOE_BLOCK)) - 1, -1, -1)]

        def fill(e, carry, do_wait):
            start = pad_ref[0, e]
            n_pad = pad_ref[1, e] - start
            off = start
            for sz in sizes:
                take = (n_pad & sz) != 0
                cp = pltpu.make_async_copy(zero_ref.at[pl.ds(0, sz)], xb_ref.at[pl.ds(off, sz)], sem_z)

                @pl.when(take)
                def _():
                    if do_wait:
                        cp.wait()
                    else:
                        cp.start()
                off = off + jnp.where(take, sz, 0)
            return carry

        half = zero_ref.shape[0]

        def fill_unused(j, carry, do_wait):
            for part in range(MOE_BLOCK // half):
                cp = pltpu.make_async_copy(zero_ref, xb_ref.at[pl.ds(j * MOE_BLOCK + part * half, half)], sem_z)
                if do_wait:
                    cp.wait()
                else:
                    cp.start()
            return carry

        n_blocks = xb_ref.shape[0] // MOE_BLOCK
        lax.fori_loop(0, N_EXPERTS, lambda e, c: fill(e, c, False), 0)
        lax.fori_loop(pad_ref[2, 0], n_blocks, lambda j, c: fill_unused(j, c, False), 0)
        lax.fori_loop(0, N_EXPERTS, lambda e, c: fill(e, c, True), 0)
        lax.fori_loop(pad_ref[2, 0], n_blocks, lambda j, c: fill_unused(j, c, True), 0)

    def issue(r, carry):
        for k in range(TOP_K):
            pltpu.make_async_copy(h_ref.at[r], xb_ref.at[dest_ref[r * TOP_K + k]], sem_s).start(priority=k % 2)
        return carry

    lax.fori_loop(0, tm, issue, 0, unroll=ROW_DMA_UNROLL)
    for k in range(TOP_K):
        pltpu.make_async_copy(h_ref, xb_ref.at[pl.ds(0, tm)], sem_s).wait()


def _dispatch(h2t, dest, meta, n_slots, tm):
    t = h2t.shape[0] // SUBLANES
    h3 = h2t.reshape(t, SUBLANES, LANES)
    return pl.pallas_call(
        _dispatch_kernel,
        grid_spec=pltpu.PrefetchScalarGridSpec(
            num_scalar_prefetch=1,
            grid=(t // tm,),
            in_specs=[pl.BlockSpec((tm * TOP_K,), lambda i, p: (i,), memory_space=pltpu.SMEM),
                      pl.BlockSpec((tm, SUBLANES, LANES), lambda i, p: (i, 0, 0))],
            out_specs=pl.BlockSpec(memory_space=pl.ANY),
            scratch_shapes=[pltpu.VMEM((MOE_BLOCK // 2, SUBLANES, LANES), F32),
                            pltpu.SemaphoreType.DMA, pltpu.SemaphoreType.DMA]),
        out_shape=jax.ShapeDtypeStruct((n_slots, SUBLANES, LANES), F32),
        compiler_params=_cparams("arbitrary"),
        name="moe_dispatch",
    )(meta[:3, :N_EXPERTS], dest, h3)


def _expert_kernel(be_ref, nv_ref, nxt_ref, x_ref, wgu_hbm, bgu_ref, wd_hbm, bd_ref, y_ref,
                   wgu_f32, wd_f32, wgu_bf, wd_bf, sem, *, layer):
    j = pl.program_id(0)
    valid = j < nv_ref[0]
    first = valid & ((j == 0) | (be_ref[j] != be_ref[jnp.maximum(j - 1, 0)]))

    def weight_copies(e):
        return (pltpu.make_async_copy(wgu_hbm.at[layer, e], wgu_f32, sem.at[0]),
                pltpu.make_async_copy(wd_hbm.at[layer, e], wd_f32, sem.at[1]))

    @pl.when(j == 0)
    def _():
        for cp in weight_copies(be_ref[0]):
            cp.start()

    @pl.when(first)
    def _():
        for cp in weight_copies(be_ref[j]):
            cp.wait()
        wgu_bf[...] = wgu_f32[...].astype(BF16)
        wd_bf[...] = wd_f32[...].astype(BF16)

        @pl.when(nxt_ref[j] >= 0)
        def _():
            for cp in weight_copies(nxt_ref[j]):
                cp.start()

    @pl.when(valid)
    def _():
        x = jnp.concatenate([x_ref[pl.ds(c, MOE_BLOCK, stride=SUBLANES), :] for c in range(D_MODEL // LANES)],
                            axis=1).astype(BF16)
        gu = _bdot(x, wgu_bf[...]) + bgu_ref[0]
        x_glu = jnp.minimum(gu[:, :D_EXPERT], SWIGLU_LIMIT)
        x_lin = jnp.clip(gu[:, D_EXPERT:], -SWIGLU_LIMIT, SWIGLU_LIMIT)
        act = x_glu * _sigmoid(SWIGLU_ALPHA * x_glu) * (x_lin + 1.0)
        y = _bdot(act.astype(BF16), wd_bf[...]) + bd_ref[0]
        for c in range(D_MODEL // LANES):
            y_ref[pl.ds(c, MOE_BLOCK, stride=SUBLANES), :] = y[:, c * LANES:(c + 1) * LANES]

    @pl.when(jnp.logical_not(valid))
    def _():
        y_ref[...] = jnp.zeros_like(y_ref)


def _experts(xb, block_expert, n_valid, w_gu, b_gu, w_down, b_down, layer):
    n_slots = xb.shape[0]
    n_blocks = n_slots // MOE_BLOCK
    rows = MOE_BLOCK * SUBLANES
    x2 = xb.reshape(n_slots * SUBLANES, LANES)
    depth, ne, d, de2 = w_gu.shape
    idx = jnp.arange(n_blocks, dtype=jnp.int32)
    is_first = ((idx == 0) | (block_expert != jnp.roll(block_expert, 1))) & (idx < n_valid[0])
    first_at = lax.cummin(jnp.where(is_first, idx, n_blocks)[::-1])[::-1]
    next_first = jnp.concatenate([first_at[1:], jnp.full((1,), n_blocks, jnp.int32)])
    nxt = jnp.where(next_first < n_blocks, block_expert[jnp.minimum(next_first, n_blocks - 1)], -1)
    last = lambda j, be, nv, nx: jnp.minimum(j, nv[0] - 1)
    bmap = lambda j, be, nv, nx: (layer, be[last(j, be, nv, nx)], 0, 0)
    return pl.pallas_call(
        functools.partial(_expert_kernel, layer=layer),
        grid_spec=pltpu.PrefetchScalarGridSpec(
            num_scalar_prefetch=3,
            grid=(n_blocks,),
            in_specs=[pl.BlockSpec((rows, LANES), lambda j, be, nv, nx: (last(j, be, nv, nx), 0)),
                      pl.BlockSpec(memory_space=pl.ANY),
                      pl.BlockSpec((None, 1, 1, de2), bmap),
                      pl.BlockSpec(memory_space=pl.ANY),
                      pl.BlockSpec((None, 1, 1, d), bmap)],
            out_specs=pl.BlockSpec((rows, LANES), lambda j, be, nv, nx: (j, 0)),
            scratch_shapes=[pltpu.VMEM((d, de2), F32), pltpu.VMEM((de2 // 2, d), F32),
                            pltpu.VMEM((d, de2), BF16), pltpu.VMEM((de2 // 2, d), BF16),
                            pltpu.SemaphoreType.DMA((2,))]),
        out_shape=jax.ShapeDtypeStruct((n_slots * SUBLANES, LANES), F32),
        compiler_params=_cparams("arbitrary"),
        name="moe_experts",
    )(block_expert, n_valid, nxt.astype(jnp.int32), x2, w_gu, b_gu.reshape(depth, ne, 1, de2), w_down,
      b_down.reshape(depth, ne, 1, d))


def _combine_kernel(dest_ref, dest_next_ref, yb_ref, gate_ref, x1_ref, g2_ref, gpost_ref, o_ref, buf, sem):
    i = pl.program_id(0)
    tm = x1_ref.shape[0]
    slot = i % 2

    def gather(idx_ref, into):
        def issue(r, carry):
            for k in range(TOP_K):
                src = pl.multiple_of(idx_ref[r * TOP_K + k] * SUBLANES, SUBLANES)
                dst = pl.multiple_of((k * tm + r) * SUBLANES, SUBLANES)
                pltpu.make_async_copy(yb_ref.at[pl.ds(src, SUBLANES), :], buf.at[into, pl.ds(dst, SUBLANES), :],
                                      sem.at[into]).start(priority=k % 2)
            return carry
        lax.fori_loop(0, tm, issue, 0, unroll=ROW_DMA_UNROLL)

    @pl.when(i == 0)
    def _():
        gather(dest_ref, 0)

    @pl.when(i + 1 < pl.num_programs(0))
    def _():
        gather(dest_next_ref, 1 - slot)

    pltpu.make_async_copy(yb_ref.at[pl.ds(0, TOP_K * tm * SUBLANES), :], buf.at[slot], sem.at[slot]).wait()
    gates = gate_ref[...]
    gk = [jnp.broadcast_to(gates[:, ROUTE_GATE + k:ROUTE_GATE + k + 1], (tm, LANES)) for k in range(TOP_K)]
    b2 = buf.at[slot]
    cols = []
    for c in range(D_MODEL // LANES):
        acc = jnp.zeros((tm, LANES), F32)
        for k in range(TOP_K):
            acc = acc + gk[k] * b2[pl.ds(k * tm * SUBLANES + c, tm, stride=SUBLANES), :]
        cols.append(acc)
    f = jnp.concatenate(cols, axis=1)
    o_ref[...] = x1_ref[...] + g2_ref[0] * (_rms(f) * gpost_ref[...])


def _combine(yb, dest, gates, x1, mod3, g_post, seq, tm):
    t, d = x1.shape
    tps = seq // tm
    n = t // tm
    return pl.pallas_call(
        _combine_kernel,
        grid=(n,),
        in_specs=[pl.BlockSpec((tm * TOP_K,), lambda i: (i,), memory_space=pltpu.SMEM),
                  pl.BlockSpec((tm * TOP_K,), lambda i: (jnp.minimum(i + 1, n - 1),), memory_space=pltpu.SMEM),
                  pl.BlockSpec(memory_space=pl.ANY),
                  pl.BlockSpec((tm, LANES), lambda i: (i, 0)),
                  pl.BlockSpec((tm, d), lambda i: (i, 0)),
                  _mod_spec(5, tps),
                  pl.BlockSpec(g_post.shape, lambda i: (0, 0))],
        out_specs=pl.BlockSpec((tm, d), lambda i: (i, 0)),
        out_shape=jax.ShapeDtypeStruct((t, d), F32),
        scratch_shapes=[pltpu.VMEM((2, TOP_K * tm * SUBLANES, LANES), F32), pltpu.SemaphoreType.DMA((2,))],
        compiler_params=_cparams("arbitrary"),
        name="moe_combine",
    )(dest, dest, yb, gates, x1, mod3, g_post)


def _moe(h2t, route, plan, cnt, x1, mod3, g_post, w_gu, b_gu, w_down, b_down, layer, seq):
    t = x1.shape[0]
    n_blocks = t * TOP_K // MOE_BLOCK + N_EXPERTS
    dest_rows, blk, meta = _route(plan, cnt, n_blocks, min(2048, t))
    dest = dest_rows[:TOP_K].T.reshape(t * TOP_K)
    xb = _dispatch(h2t, dest, meta, n_blocks * MOE_BLOCK, MOE_BLOCK)
    yb = _experts(xb, blk[0, :n_blocks], meta[2, :1], w_gu, b_gu, w_down, b_down, layer)
    return _combine(yb, dest, route, x1, mod3, g_post, seq, MOE_BLOCK)


TOKEN_TILE = 512
OUT_TILE = 512
FOX_Q_TILE = 512
FOX_K_TILE = 256


def kernel(x, c, ada_w, ada_b, norm_pre_mix, norm_post_mix, norm_pre_ffn, norm_post_ffn, ev_w_in, fox_b_f, s5_lam_re, s5_lam_im, s5_log_dt, s5_b_re, s5_b_im, s5_c_re, s5_c_im, s5_d, s5_glu_w, s5_glu_b, ev_w_out, od_w_in, gla_w_up, gla_b_gate, od_w_out, router_w, router_b, exp_w_gu, exp_b_gu, exp_w_down, exp_b_down):
    bsz, seq, d = x.shape
    t = bsz * seq
    tm = min(TOKEN_TILE, seq)
    x2 = x.reshape(t, d)
    mod = _modulation(c, ada_w, ada_b)
    for l in range(DEPTH):
        i = l // 2
        mod3 = mod[l].reshape(bsz, 1, 6 * d)
        row = lambda a: a[l].reshape(1, -1)
        tail = (x2, mod3, row(norm_post_mix), row(norm_pre_ffn), router_w[l], router_b[l])
        if l % 2 == 0:
            u, q, k, v_t = _in_even(x2, mod3, row(norm_pre_mix), ev_w_in[i], fox_b_f[i], seq, tm)
            tables = _s5_tables(s5_lam_re[i], s5_lam_im[i], s5_log_dt[i], s5_b_re[i], s5_b_im[i],
                                s5_c_re[i], s5_c_im[i])
            ys = _s5_scan(u, bsz, seq, tables)
            yb = _fox(q, k, v_t, bsz, seq, min(FOX_Q_TILE, seq), min(FOX_K_TILE, seq))
            outs = _out_even(ys, u, yb, s5_d[i], s5_glu_w[i], s5_glu_b[i], ev_w_out[i], *tail, seq=seq, tm=min(OUT_TILE, seq))
        else:
            rq, rk, rv, sg, gq, gk, gv, sr, la = _in_odd(x2, mod3, row(norm_pre_mix), od_w_in[i],
                                                         gla_w_up[i], gla_b_gate[i], seq, tm)
            yc = _retention(rq, rk, rv, sg, bsz, seq)
            yd = _gla(gq, gk, la, gv, sr, bsz, seq)
            outs = _out_odd(yc, yd, od_w_out[i], *tail, seq=seq, tm=min(OUT_TILE, seq))
        x1, h2t, route, plan, cnt = outs
        x2 = _moe(h2t, route, plan, cnt, x1, mod3, row(norm_post_ffn),
                  exp_w_gu, exp_b_gu, exp_w_down, exp_b_down, l, seq)
    return x2.reshape(bsz, seq, d)
```

```python
import functools
import math

import jax
import jax.numpy as jnp
from jax import lax
from jax.experimental import pallas as pl
from jax.experimental.pallas import tpu as pltpu

F32 = jnp.float32
BF16 = jnp.bfloat16
HIGHEST = lax.Precision.HIGHEST

D_MODEL = 1024
DEPTH = 2
EPS = 1e-6
S5_CH = 512
S5_GROUP = 16
S5_GROUPS = S5_CH // S5_GROUP
S5_STATE = 64
S5_CHUNK = 8
FOX_HEADS = 8
FOX_DH = 64
FOX_W = FOX_HEADS * FOX_DH
LOG2_E = 1.4426950408889634
FOX_VROWS = FOX_DH + 16
FOX_HPS = 8
RET_HEADS = 4
RET_DK = 128
RET_DV = 128
ROPE_BASE = 10000.0
GLA_HEADS = 4
GLA_DK = 64
GLA_DV = 128
GLA_RANK = 16
GLA_TAU = 16.0
GLA_CHUNK = 64
GLA_SUB = 16
N_EXPERTS = 32
TOP_K = 4
D_EXPERT = 1024
SWIGLU_LIMIT = 7.0
SWIGLU_ALPHA = 1.702
MOE_BLOCK = 256
ROW_DMA_UNROLL = 8

LANES = 128
SUBLANES = 8
VMEM_LIMIT = 56 * 1024 * 1024


def _cparams(*sem):
    return pltpu.CompilerParams(dimension_semantics=sem, vmem_limit_bytes=VMEM_LIMIT)


def _bdot(a, b):
    return jnp.dot(a, b, preferred_element_type=F32)


def _dot_nt(a, b):
    return lax.dot_general(a, b, (((1,), (1,)), ((), ())), preferred_element_type=F32)


def _dot_tn(a, b):
    return lax.dot_general(a, b, (((0,), (0,)), ((), ())), preferred_element_type=F32)


def _split3(x):
    hi = x.astype(BF16)
    r = x - hi.astype(F32)
    mid = r.astype(BF16)
    lo = (r - mid.astype(F32)).astype(BF16)
    return hi, mid, lo


def _dot01(m01, x):
    hi, mid, lo = _split3(x)
    return _bdot(m01, hi) + _bdot(m01, mid) + _bdot(m01, lo)


def _lower_tri(n, strict=False):
    r = lax.broadcasted_iota(jnp.int32, (n, n), 0)
    c = lax.broadcasted_iota(jnp.int32, (n, n), 1)
    return ((r > c) if strict else (r >= c)).astype(BF16)


def _log_sigmoid(x):
    return jnp.minimum(x, 0.0) - jnp.log1p(jnp.exp(-jnp.abs(x)))


def _sigmoid(x):
    return 1.0 / (1.0 + jnp.exp(-x))


def _silu(x):
    return x * _sigmoid(x)


def _rms(x):
    return x * lax.rsqrt(jnp.mean(x * x, axis=-1, keepdims=True) + EPS)


def _mod_kernel(c_ref, w_ref, b_ref, o_ref):
    s = _silu(c_ref[...])
    w = w_ref[0]
    s_hi, w_hi = s.astype(BF16), w.astype(BF16)
    s_lo = (s - s_hi.astype(F32)).astype(BF16)
    w_lo = (w - w_hi.astype(F32)).astype(BF16)
    o_ref[0] = _bdot(s_hi, w_hi) + _bdot(s_lo, w_hi) + _bdot(s_hi, w_lo) + b_ref[0]


def _modulation(c, ada_w, ada_b):
    depth, d, n = ada_w.shape
    bsz = c.shape[0]
    tn = D_MODEL
    return pl.pallas_call(
        _mod_kernel,
        grid=(depth, n // tn),
        in_specs=[pl.BlockSpec((bsz, d), lambda l, j: (0, 0)),
                  pl.BlockSpec((1, d, tn), lambda l, j: (l, 0, j)),
                  pl.BlockSpec((1, 1, tn), lambda l, j: (l, 0, j))],
        out_specs=pl.BlockSpec((1, bsz, tn), lambda l, j: (l, 0, j)),
        out_shape=jax.ShapeDtypeStruct((depth, bsz, n), F32),
        compiler_params=_cparams("parallel", "parallel"),
        name="adaln_mod",
    )(c, ada_w, ada_b.reshape(depth, 1, n))


def _mod_spec(chunk, tiles_per_seq):
    return pl.BlockSpec((1, 1, D_MODEL), lambda i: (i // tiles_per_seq, 0, chunk))


def _prenorm(x, g_ref, sc_ref, sh_ref):
    return _rms(x) * (g_ref[...] * (1.0 + sc_ref[0])) + sh_ref[0]


def _in_even_kernel(x_ref, sh_ref, sc_ref, g_ref, w_ref, bf_ref,
                    u_ref, q_ref, k_ref, vt_ref, carry_ref, *, tiles_per_seq):
    i = pl.program_id(0)
    tm = x_ref.shape[0]
    h = _prenorm(x_ref[...], g_ref, sc_ref, sh_ref)
    z = _bdot(h.astype(BF16), w_ref[...])
    u_ref[...] = z[:, 0:S5_CH]
    tail = jnp.concatenate([jnp.ones((1, tm), F32), jnp.zeros((FOX_VROWS - FOX_DH - 1, tm), F32)], axis=0)
    for pr in range(FOX_HEADS // 2):
        c0 = S5_CH + 2 * FOX_W + pr * LANES
        v_pair = z[:, c0:c0 + LANES].T
        for hh in range(2):
            r0 = (2 * pr + hh) * FOX_VROWS
            vt_ref[r0:r0 + FOX_VROWS, :] = jnp.concatenate(
                [v_pair[hh * FOX_DH:(hh + 1) * FOX_DH], tail], axis=0).astype(BF16)
    ls = _log_sigmoid(z[:, S5_CH + 3 * FOX_W:] + bf_ref[...])

    @pl.when(i % tiles_per_seq == 0)
    def _():
        carry_ref[...] = jnp.zeros_like(carry_ref)

    cum = _dot01(_lower_tri(tm), ls) + carry_ref[...]
    carry_ref[...] = cum[tm - 1:tm, :]

    lane = lax.broadcasted_iota(jnp.int32, (1, LANES), 1)
    feat = lane < FOX_DH
    ones = jnp.where(lane < FOX_DH + 3, 1.0, 0.0)
    for hd in range(FOX_HEADS):
        blk = (hd * FOX_DH) // LANES * LANES
        qs = z[:, S5_CH + blk:S5_CH + blk + LANES] * (FOX_DH ** -0.5 * LOG2_E)
        ks = z[:, S5_CH + FOX_W + blk:S5_CH + FOX_W + blk + LANES]
        if (hd * FOX_DH) % LANES:
            qs = pltpu.roll(qs, LANES - FOX_DH, 1)
            ks = pltpu.roll(ks, LANES - FOX_DH, 1)
        nf = jnp.broadcast_to(-LOG2_E * cum[:, hd:hd + 1], (tm, LANES))
        hi = nf.astype(BF16).astype(F32)
        mid = (nf - hi).astype(BF16).astype(F32)
        lo = nf - hi - mid
        bias = jnp.where(lane == FOX_DH, hi, jnp.where(lane == FOX_DH + 1, mid,
                                                       jnp.where(lane == FOX_DH + 2, lo, 0.0)))
        q_ref[:, hd * LANES:(hd + 1) * LANES] = jnp.where(feat, qs, ones).astype(BF16)
        k_ref[:, hd * LANES:(hd + 1) * LANES] = jnp.where(feat, ks, bias).astype(BF16)


def _in_even(x2, mod3, gain, w_in, b_f, seq, tm):
    t, d = x2.shape
    tiles_per_seq = seq // tm
    nw = S5_CH + 3 * FOX_W
    w = jnp.concatenate([w_in[:, :nw], jnp.pad(w_in[:, nw:], ((0, 0), (0, LANES - FOX_HEADS)))],
                        axis=1).astype(BF16)
    bf = jnp.pad(b_f, (0, LANES - FOX_HEADS)).reshape(1, LANES)
    tok = lambda n: pl.BlockSpec((tm, n), lambda i: (i, 0))
    full = lambda a: pl.BlockSpec(a.shape, lambda i: (0,) * a.ndim)
    return pl.pallas_call(
        functools.partial(_in_even_kernel, tiles_per_seq=tiles_per_seq),
        grid=(t // tm,),
        in_specs=[tok(d), _mod_spec(0, tiles_per_seq), _mod_spec(1, tiles_per_seq),
                  full(gain), full(w), full(bf)],
        out_specs=[tok(S5_CH), tok(FOX_HEADS * LANES), tok(FOX_HEADS * LANES),
                   pl.BlockSpec((FOX_HEADS * FOX_VROWS, tm), lambda i: (i // tiles_per_seq, i % tiles_per_seq))],
        out_shape=[jax.ShapeDtypeStruct((t, S5_CH), F32),
                   jax.ShapeDtypeStruct((t, FOX_HEADS * LANES), BF16),
                   jax.ShapeDtypeStruct((t, FOX_HEADS * LANES), BF16),
                   jax.ShapeDtypeStruct((t // seq * FOX_HEADS * FOX_VROWS, seq), BF16)],
        scratch_shapes=[pltpu.VMEM((1, LANES), F32)],
        compiler_params=_cparams("arbitrary"),
        name="in_proj_even",
    )(x2, mod3, mod3, gain, w, bf)


S5_TILE_GROUPS = LANES // S5_GROUP
S5_SEQ_PARTS = 4


def _s5_kernel(u_ref, wt_ref, ws_ref, wc_ref, a_ref, y_ref, x_scr, e_scr, hp_scr, h_scr, *, nb, ncl):
    ell = S5_CHUNK
    sw = S5_TILE_GROUPS * 2 * S5_STATE

    @pl.when(pl.program_id(1) == 0)
    def _():
        h_scr[...] = jnp.zeros_like(h_scr)

    blk = nb * SUBLANES
    for hi in range(ncl // SUBLANES):
        for b in range(nb):
            for t in range(ell):
                x_scr[hi * blk + b * SUBLANES:hi * blk + (b + 1) * SUBLANES, t * LANES:(t + 1) * LANES] = (
                    u_ref[b, pl.ds(hi * SUBLANES * ell + t, SUBLANES, stride=ell), :])
    x = x_scr[...].astype(BF16)
    e = _bdot(x, ws_ref[0])
    tg = S5_TILE_GROUPS
    for j in range(tg):
        e_scr[j] = e[:, j * LANES:(j + 1) * LANES]
        e_scr[tg + j] = pltpu.roll(e[:, j * LANES:(j + 1) * LANES], S5_STATE, 1)
    a1 = jnp.broadcast_to(a_ref[0, 0:1, :], (nb, sw))
    a2 = jnp.broadcast_to(a_ref[0, 1:2, :], (nb, sw))
    a2s = jnp.broadcast_to(a_ref[0, 2:3, :], (nb, sw))

    def body(c, carry):
        h, hs = carry
        rows_c = pl.ds((c // SUBLANES) * blk + c % SUBLANES, nb, stride=SUBLANES)
        for j in range(tg):
            hp_scr[j, rows_c, :] = h[:, j * LANES:(j + 1) * LANES]
        e1 = jnp.concatenate([e_scr[j, rows_c, :] for j in range(tg)], axis=1)
        e2 = jnp.concatenate([e_scr[tg + j, rows_c, :] for j in range(tg)], axis=1)
        return a1 * h + a2 * hs + e1, a1 * hs + a2s * h + e2

    h, hs = lax.fori_loop(0, ncl, body, (h_scr[0], h_scr[1]))
    h_scr[0] = h
    h_scr[1] = hs
    hp = jnp.concatenate([hp_scr[j] for j in range(tg)], axis=1).astype(BF16)
    y = _bdot(x, wt_ref[0]) + _bdot(hp, wc_ref[0])
    for hi in range(ncl // SUBLANES):
        for b in range(nb):
            for t in range(ell):
                y_ref[b, pl.ds(hi * SUBLANES * ell + t, SUBLANES, stride=ell), :] = (
                    y[hi * blk + b * SUBLANES:hi * blk + (b + 1) * SUBLANES, t * LANES:(t + 1) * LANES])


def _s5_tables(lam_re, lam_im, log_dt, b_re, b_im, c_re, c_im):
    ell, p, g = S5_CHUNK, S5_STATE, S5_GROUPS
    lr, li = lam_re.astype(F32), lam_im.astype(F32)
    dt = jnp.exp(log_dt.astype(F32))[:, None]
    mag = jnp.exp(lr * dt)
    a_re, a_im = mag * jnp.cos(li * dt), mag * jnp.sin(li * dt)
    den = lr * lr + li * li
    n_re, n_im = a_re - 1.0, a_im
    z_re = (n_re * lr + n_im * li) / den
    z_im = (n_im * lr - n_re * li) / den
    br, bi = b_re.astype(F32), b_im.astype(F32)
    bb_re = z_re[..., None] * br - z_im[..., None] * bi
    bb_im = z_re[..., None] * bi + z_im[..., None] * br
    j = jnp.arange(ell + 1, dtype=F32)[:, None, None]
    pmag = jnp.exp(j * (lr * dt)[None])
    pr, pi = pmag * jnp.cos(j * (li * dt)[None]), pmag * jnp.sin(j * (li * dt)[None])
    w_re = pr[..., None] * bb_re[None] - pi[..., None] * bb_im[None]
    w_im = pr[..., None] * bb_im[None] + pi[..., None] * bb_re[None]
    cr, ci = c_re.astype(F32), c_im.astype(F32)
    kern = (jnp.einsum('gcp,jgpd->jgcd', cr, w_re[:ell], precision=HIGHEST)
            - jnp.einsum('gcp,jgpd->jgcd', ci, w_im[:ell], precision=HIGHEST))
    s_idx = jnp.arange(ell)[:, None]
    t_idx = jnp.arange(ell)[None, :]
    lag = jnp.clip(t_idx - s_idx, 0, ell - 1)
    toep = kern[lag]
    toep = jnp.where((t_idx >= s_idx)[:, :, None, None, None], toep, 0.0)
    toep = toep.transpose(2, 0, 4, 1, 3)
    rev = jnp.arange(ell - 1, -1, -1)
    st_re = w_re[rev].transpose(1, 0, 3, 2)
    st_im = w_im[rev].transpose(1, 0, 3, 2)
    wst = jnp.concatenate([st_re, st_im], -1)
    p1r, p1i = pr[1:], pi[1:]
    c_hr = cr[None] * p1r[:, :, None, :] - ci[None] * p1i[:, :, None, :]
    c_hi = -cr[None] * p1i[:, :, None, :] - ci[None] * p1r[:, :, None, :]
    cst = jnp.concatenate([c_hr, c_hi], axis=-1).transpose(1, 3, 0, 2)
    al_r, al_i = pr[ell], pi[ell]
    a_rows = jnp.stack([jnp.concatenate([al_r, al_r], -1), jnp.concatenate([-al_i, al_i], -1),
                        jnp.concatenate([al_i, -al_i], -1)], axis=1)

    tg = S5_TILE_GROUPS
    nt = g // tg
    tile = lambda a: a.reshape((nt, tg) + a.shape[1:])
    lane_grp = jnp.arange(LANES) // S5_GROUP
    step_lane_grp = jnp.tile(lane_grp, ell)
    spread_out = (jnp.eye(ell, dtype=BF16)[:, None, :, None, None]
                  * jnp.eye(S5_GROUP, dtype=BF16)[None, :, None, None, :]
                  * jnp.ones((1, 1, 1, tg, 1), BF16)).reshape(ell * S5_GROUP, ell * LANES)
    toep_t = tile(toep).transpose(0, 2, 1, 3, 4, 5).reshape(nt, ell, LANES, ell * S5_GROUP).astype(BF16)
    wt = jnp.einsum('ksrm,mn->ksrn', toep_t, spread_out)
    wt = jnp.where((lane_grp[:, None] == step_lane_grp[None, :])[None, None], wt, 0)
    wt = wt.reshape(nt, ell * LANES, ell * LANES)
    spread_state = jnp.tile(jnp.eye(2 * p, dtype=BF16), (1, tg))
    state_grp = jnp.arange(tg * 2 * p) // (2 * p)
    wst_t = tile(wst).transpose(0, 2, 1, 3, 4).reshape(nt, ell, LANES, 2 * p).astype(BF16)
    ws = jnp.einsum('ksrp,pn->ksrn', wst_t, spread_state)
    ws = jnp.where((lane_grp[:, None] == state_grp[None, :])[None, None], ws, 0)
    ws = ws.reshape(nt, ell * LANES, tg * 2 * p)
    cst_t = tile(cst).reshape(nt, tg * 2 * p, ell * S5_GROUP).astype(BF16)
    wc = jnp.einsum('krm,mn->krn', cst_t, spread_out)
    wc = jnp.where((state_grp[:, None] == step_lane_grp[None, :])[None], wc, 0)
    a_t = tile(a_rows).transpose(0, 2, 1, 3).reshape(nt, 3, tg * 2 * p)
    a_t = jnp.pad(a_t, ((0, 0), (0, SUBLANES - 3), (0, 0)))
    return wt, ws, wc, a_t


def _s5_scan(u, bsz, seq, tables):
    wt, ws, wc, a_t = tables
    ell = S5_CHUNK
    nt = S5_GROUPS // S5_TILE_GROUPS
    parts = S5_SEQ_PARTS if seq % (S5_SEQ_PARTS * ell * 2 * SUBLANES) == 0 else 1
    ncl = seq // parts // ell
    rows = bsz * ncl
    sw = S5_TILE_GROUPS * 2 * S5_STATE
    u3 = u.reshape(bsz, seq, S5_CH)
    io = pl.BlockSpec((bsz, seq // parts, LANES), lambda k, s: (0, s, k))
    per_tile = lambda a: pl.BlockSpec((1,) + a.shape[1:], lambda k, s: (k, 0, 0))
    y = pl.pallas_call(
        functools.partial(_s5_kernel, nb=bsz, ncl=ncl),
        grid=(nt, parts),
        in_specs=[io, per_tile(wt), per_tile(ws), per_tile(wc), per_tile(a_t)],
        out_specs=io,
        out_shape=jax.ShapeDtypeStruct((bsz, seq, S5_CH), F32),
        scratch_shapes=[pltpu.VMEM((rows, ell * LANES), F32),
                        pltpu.VMEM((2 * S5_TILE_GROUPS, rows, LANES), F32),
                        pltpu.VMEM((S5_TILE_GROUPS, rows, LANES), F32),
                        pltpu.VMEM((2, bsz, sw), F32)],
        compiler_params=_cparams("parallel", "arbitrary"),
        name="s5_scan",
    )(u3, wt, ws, wc, a_t)
    return y.reshape(bsz * seq, S5_CH)


def _fox_kernel(q_ref, k_ref, vt_ref, o_ref, *, tq, tk):
    i = pl.program_id(2)
    nh = FOX_HPS
    q = [q_ref[:, h * LANES:(h + 1) * LANES] for h in range(nh)]
    key = lax.broadcasted_iota(jnp.int32, (tk, tq), 0)
    qry = lax.broadcasted_iota(jnp.int32, (tk, tq), 1)
    per_q = tq // tk

    def block(j0, carry, mask, q0=0):
        kj = k_ref[pl.ds(j0, tk), :]
        vtj = vt_ref[:, pl.ds(j0, tk)]
        ss = [_dot_nt(kj[:, h * LANES:(h + 1) * LANES], q[h][q0:]) for h in range(nh)]
        stats = []
        for h in range(nh):
            m = carry[h][0][:, q0:]
            s = ss[h] if mask is None else jnp.where(mask[:, q0:], ss[h], -jnp.inf)
            m_new = jnp.maximum(m, jnp.max(s, axis=0, keepdims=True))
            p = jnp.exp2(s - m_new)
            stats.append((m_new, jnp.exp2(m - m_new), p.astype(BF16)))
        out = []
        for h in range(nh):
            m_new, alpha, p = stats[h]
            acc = alpha * carry[h][1][:, q0:] + _bdot(vtj[h * FOX_VROWS:(h + 1) * FOX_VROWS, :], p)
            if q0:
                m_new = jnp.concatenate([carry[h][0][:, :q0], m_new], axis=1)
                acc = jnp.concatenate([carry[h][1][:, :q0], acc], axis=1)
            out.append((m_new, acc))
        return tuple(out)

    init = tuple((jnp.full((1, tq), -jnp.inf, F32), jnp.zeros((FOX_VROWS, tq), F32)) for _ in range(nh))
    carry = lax.fori_loop(0, i * per_q, lambda j, c: block(pl.multiple_of(j * tk, tk), c, None), init)
    for d in range(per_q):
        carry = block(pl.multiple_of(i * tq + d * tk, tk), carry, key + d * tk <= qry, d * tk)
    for g in range(nh // 2):
        o_t = jnp.concatenate([carry[h][1][:FOX_DH] / carry[h][1][FOX_DH:FOX_DH + 1]
                               for h in (2 * g, 2 * g + 1)], axis=0)
        o_ref[:, g * LANES:(g + 1) * LANES] = o_t.T.astype(o_ref.dtype)


def _fox(q_aug, k_aug, v_t, bsz, seq, tq, tk):
    t = q_aug.shape[0]
    nh = FOX_HPS
    groups = FOX_HEADS // nh
    nq = seq // tq
    return pl.pallas_call(
        functools.partial(_fox_kernel, tq=tq, tk=tk),
        grid=(bsz, groups, nq),
        in_specs=[pl.BlockSpec((tq, nh * LANES), lambda b, p, i: (b * nq + i, p)),
                  pl.BlockSpec((seq, nh * LANES), lambda b, p, i: (b, p)),
                  pl.BlockSpec((nh * FOX_VROWS, seq), lambda b, p, i: (b * groups + p, 0))],
        out_specs=pl.BlockSpec((tq, nh * FOX_DH), lambda b, p, i: (b * nq + i, p)),
        out_shape=jax.ShapeDtypeStruct((t, FOX_W), BF16),
        compiler_params=_cparams("parallel", "parallel", "arbitrary"),
        name="fox_attention",
    )(q_aug, k_aug, v_t)


_ODD_COLS = (("rq", 512), ("rk", 512), ("rv", 512), ("rg", 512), ("gq", 256), ("gk", 256),
             ("gv", 512), ("gr", 512), ("glr", LANES))


def _odd_offsets():
    off, out = 0, {}
    for name, w in _ODD_COLS:
        out[name] = (off, off + w)
        off += w
    return out, off


def _in_odd_kernel(x_ref, sh_ref, sc_ref, g_ref, w_ref, cos_ref, sin_ref, wup_ref, bg_ref,
                   rq_ref, rk_ref, rv_ref, sg_ref, gq_ref, gk_ref, gv_ref, sr_ref, la_ref):
    h = _prenorm(x_ref[...], g_ref, sc_ref, sh_ref)
    z = _bdot(h.astype(BF16), w_ref[...])
    off, _ = _odd_offsets()
    col = lambda n: z[:, off[n][0]:off[n][1]]
    cos, sin = cos_ref[...], sin_ref[...]

    def rope(t, scale):
        heads = []
        for hd in range(RET_HEADS):
            th = t[:, hd * RET_DK:(hd + 1) * RET_DK]
            heads.append((th * cos + pltpu.roll(th, RET_DK // 2, 1) * sin) * scale)
        return jnp.concatenate(heads, axis=1).astype(BF16)

    rq_ref[...] = rope(col("rq"), 1.0)
    rk_ref[...] = rope(col("rk"), RET_DK ** -0.5)
    rv_ref[...] = col("rv").astype(BF16)
    sg_ref[...] = _silu(col("rg"))
    gq_ref[...] = col("gq") * (GLA_DK ** -0.5)
    gk_ref[...] = col("gk")
    gv_ref[...] = col("gv").astype(BF16)
    sr_ref[...] = _silu(col("gr"))
    glr = col("glr")
    g_hi = glr.astype(BF16)
    g_lo = (glr - g_hi.astype(F32)).astype(BF16)
    nqk = GLA_HEADS * GLA_DK
    gw = _bdot(g_hi, wup_ref[...])
    gate = gw[:, :nqk] + gw[:, nqk:] + _bdot(g_lo, wup_ref[:, :nqk]) + bg_ref[...]
    la_ref[...] = _log_sigmoid(gate) * (1.0 / GLA_TAU)


def _in_odd(x2, mod3, gain, w_in, w_up, b_gate, seq, tm):
    t, d = x2.shape
    tps = seq // tm
    ref_w = (512, 512, 512, 512, 256, 256, 512, GLA_RANK, 512)
    starts = [0]
    for wd in ref_w:
        starts.append(starts[-1] + wd)
    seg = lambda j: w_in[:, starts[j]:starts[j + 1]]
    w = jnp.concatenate([seg(0), seg(1), seg(2), seg(3), seg(4), seg(5), seg(6), seg(8),
                         jnp.pad(seg(7), ((0, 0), (0, LANES - GLA_RANK)))], axis=1).astype(BF16)
    wup32 = jnp.pad(w_up.astype(F32), ((0, LANES - GLA_RANK), (0, 0)))
    wup_hi = wup32.astype(BF16)
    wup = jnp.concatenate([wup_hi, (wup32 - wup_hi.astype(F32)).astype(BF16)], axis=1)
    bg = b_gate.reshape(1, -1).astype(F32)
    half = RET_DK // 2
    inv = ROPE_BASE ** (-jnp.arange(half, dtype=F32) / half)
    ang = jnp.arange(seq, dtype=F32)[:, None] * inv[None, :]
    cos = jnp.concatenate([jnp.cos(ang), jnp.cos(ang)], axis=1)
    sin = jnp.concatenate([-jnp.sin(ang), jnp.sin(ang)], axis=1)
    tok = lambda n: pl.BlockSpec((tm, n), lambda i: (i, 0))
    full = lambda a: pl.BlockSpec(a.shape, lambda i: (0,) * a.ndim)
    pos = pl.BlockSpec((tm, RET_DK), lambda i: (i % tps, 0))
    widths = (512, 512, 512, 512, 256, 256, 512, 512, 256)
    dtypes = (BF16, BF16, BF16, F32, F32, F32, BF16, F32, F32)
    return pl.pallas_call(
        _in_odd_kernel,
        grid=(t // tm,),
        in_specs=[tok(d), _mod_spec(0, tps), _mod_spec(1, tps), full(gain), full(w), pos, pos,
                  full(wup), full(bg)],
        out_specs=[tok(n) for n in widths],
        out_shape=[jax.ShapeDtypeStruct((t, n), dt) for n, dt in zip(widths, dtypes)],
        compiler_params=_cparams("parallel"),
        name="in_proj_odd",
    )(x2, mod3, mod3, gain, w, cos, sin, wup, bg)


RET_CHUNK = 256


def _ret_kernel(q_ref, k_ref, v_ref, sg_ref, dm_ref, xi_ref, zeta_ref, gl_ref, y_ref, st_ref):
    @pl.when(pl.program_id(1) == 0)
    def _():
        st_ref[...] = jnp.zeros_like(st_ref)

    nb = q_ref.shape[0]
    chains = [(bl, h) for bl in range(nb) for h in range(RET_HEADS)]
    col = lambda h: slice(h * RET_DK, (h + 1) * RET_DK)
    q = [q_ref[bl, :, col(h)] for bl, h in chains]
    k = [k_ref[bl, :, col(h)] for bl, h in chains]
    v = [v_ref[bl, :, col(h)] for bl, h in chains]
    st = [st_ref[c] for c in range(len(chains))]
    s = [_dot_nt(q[c], k[c]) for c in range(len(chains))]
    inter = [_bdot((q[c].astype(F32) * xi_ref[h]).astype(BF16), st[c].astype(BF16))
             for c, (bl, h) in enumerate(chains)]
    upd = [_dot_tn((k[c].astype(F32) * zeta_ref[h]).astype(BF16), v[c]) for c, (bl, h) in enumerate(chains)]
    for c, (bl, h) in enumerate(chains):
        o = _bdot((s[c] * dm_ref[h]).astype(BF16), v[c]) + inter[c]
        st_ref[c] = gl_ref[h, 0:1, :] * st[c] + upd[c]
        y_ref[bl, :, col(h)] = (sg_ref[bl, :, col(h)] * _rms(o)).astype(y_ref.dtype)


RET_BATCHES = 4


def _retention(rq, rk, rv, sg, bsz, seq):
    t = rq.shape[0]
    ell = min(RET_CHUNK, seq)
    nc = seq // ell
    log_g = jnp.log(1.0 - jnp.exp2(-5.0 - jnp.arange(RET_HEADS, dtype=F32)))
    idx = jnp.arange(ell, dtype=F32)
    rel = idx[:, None] - idx[None, :]
    dmat = jnp.where(rel >= 0, jnp.exp(log_g[:, None, None] * jnp.maximum(rel, 0.0)), 0.0)
    lanes = lambda a: jnp.broadcast_to(a[..., None], a.shape + (RET_DK,))
    xi = lanes(jnp.exp(log_g[:, None] * (idx + 1.0)))
    zeta = lanes(jnp.exp(log_g[:, None] * (ell - 1.0 - idx)))
    gl = jnp.broadcast_to(jnp.exp(log_g * ell)[:, None, None], (RET_HEADS, SUBLANES, RET_DV))
    nb = min(RET_BATCHES, bsz)
    width = RET_HEADS * RET_DK
    r3 = lambda a: a.reshape(bsz, seq, width)
    blk = pl.BlockSpec((nb, ell, width), lambda b, c: (b, c, 0))
    full = lambda a: pl.BlockSpec(a.shape, lambda b, c: (0, 0, 0))
    y = pl.pallas_call(
        _ret_kernel,
        grid=(bsz // nb, nc),
        in_specs=[blk, blk, blk, blk, full(dmat), full(xi), full(zeta), full(gl)],
        out_specs=blk,
        out_shape=jax.ShapeDtypeStruct((bsz, seq, width), BF16),
        scratch_shapes=[pltpu.VMEM((nb * RET_HEADS, RET_DK, RET_DV), F32)],
        compiler_params=_cparams("parallel", "arbitrary"),
        name="retention",
    )(r3(rq), r3(rk), r3(rv), r3(sg), dmat, xi, zeta, gl)
    return y.reshape(t, width)


def _gla_kernel(q_ref, k_ref, la_ref, v_ref, sg_ref, y_ref, st_ref, b_scr, v_scr, p_scr, r_scr):
    @pl.when(pl.program_id(1) == 0)
    def _():
        st_ref[...] = jnp.zeros_like(st_ref)

    ell, sub = GLA_CHUNK, GLA_SUB
    n_sub = ell // sub
    nb = q_ref.shape[0]
    pairs = GLA_HEADS // 2
    streams = [(bl, p) for bl in range(nb) for p in range(pairs)]
    lane = lax.broadcasted_iota(jnp.int32, (1, LANES), 1)
    first = lane < GLA_DK
    head = (first, jnp.logical_not(first))
    pick = lambda h, a: jnp.where(head[h], a, 0.0).astype(BF16)
    tri = _lower_tri(ell)
    tau = lax.broadcasted_iota(jnp.int32, (sub, LANES), 0)
    row_of, per_sub = [], 0
    for s_ in range(sub):
        row_of.append(per_sub)
        per_sub += sub - (s_ // SUBLANES) * SUBLANES
    rsub = lax.broadcasted_iota(jnp.int32, (LANES, 2 * LANES), 0)
    csub = lax.broadcasted_iota(jnp.int32, (LANES, 2 * LANES), 1)
    ind = ((rsub < GLA_DK) == (csub < LANES)).astype(BF16)

    val = {}
    for sid, (bl, p) in enumerate(streams):
        qk = slice(p * LANES, (p + 1) * LANES)
        q, k = q_ref[bl, :, qk], k_ref[bl, :, qk]
        b = _dot01(tri, la_ref[bl, :, qk])
        b_scr[sid] = b
        v_bf = v_ref[bl, :, p * 2 * GLA_DV:(p + 1) * 2 * GLA_DV]
        v_scr[sid] = v_bf.astype(F32)
        st = st_ref[sid]
        val[sid] = dict(q=q, k=k, b=b, st=st, vh=[v_bf[:, h * GLA_DV:(h + 1) * GLA_DV] for h in range(2)])

    for sid in val:
        d = val[sid]
        qe = d["q"] * jnp.exp(d["b"])
        st_bf = d["st"].astype(BF16)
        d["o"] = [_dot_nt(pick(h, qe), st_bf) for h in range(2)]

    row = lax.broadcasted_iota(jnp.int32, (ell, LANES), 0)
    for sid in val:
        d = val[sid]
        q, k, b = d["q"], d["k"], d["b"]
        qa, ka = [], []
        for i in range(1, n_sub):
            lo = i * sub
            ref_row = b[lo - 1:lo, :]
            in_i = (row >= lo) & (row < lo + sub)
            qa.append(jnp.where(in_i, q * jnp.exp(jnp.minimum(b - ref_row, 0.0)), 0.0))
            ka.append(jnp.where(row < lo, k * jnp.exp(jnp.minimum(ref_row - b, 0.0)), 0.0))
        k_cat = jnp.concatenate(ka, axis=1).astype(BF16)
        d["a_off"] = [_dot_nt(jnp.concatenate([pick(h, x) for x in qa], axis=1), k_cat) for h in range(2)]
    for sid in val:
        d = val[sid]
        d["off"] = [_bdot(d["a_off"][h].astype(BF16), d["vh"][h]) for h in range(2)]

    for sid, (bl, p) in enumerate(streams):
        d = val[sid]
        q, b = d["q"], d["b"]
        for i in range(n_sub):
            lo = i * sub
            qi, bi = q[lo:lo + sub], b[lo:lo + sub]
            for s in range(sub):
                k_row = k_ref[bl, pl.ds(lo + s, 1), p * LANES:(p + 1) * LANES]
                b_row = b_scr[sid, pl.ds(lo + s, 1), :]
                r0 = (s // SUBLANES) * SUBLANES
                w = jnp.exp(jnp.minimum(bi[r0:] - b_row, 0.0))
                tile_s = jnp.where(tau[r0:] >= s, qi[r0:] * k_row * w, 0.0)
                p_scr[sid, pl.ds(i * per_sub + row_of[s], sub - r0), :] = tile_s
    for sid in val:
        r_scr[sid] = _bdot(p_scr[sid].astype(BF16), ind)
    for sid in val:
        diag = [[], []]
        for i in range(n_sub):
            lo = i * sub
            for h in range(2):
                acc = [jnp.zeros((SUBLANES, GLA_DV), F32) for _ in range(sub // SUBLANES)]
                for s in range(sub):
                    v_row = v_scr[sid, pl.ds(lo + s, 1), h * GLA_DV:(h + 1) * GLA_DV]
                    for part in range(s // SUBLANES, sub // SUBLANES):
                        rows = pl.ds(i * per_sub + row_of[s] + (part - s // SUBLANES) * SUBLANES, SUBLANES)
                        acc[part] = acc[part] + r_scr[sid, rows, h * LANES:(h + 1) * LANES] * v_row
                diag[h].append(jnp.concatenate(acc, axis=0))
        val[sid]["diag"] = diag

    for sid, (bl, p) in enumerate(streams):
        d = val[sid]
        b_last = d["b"][ell - 1:ell, :]
        kh = (d["k"] * jnp.exp(b_last - d["b"])).astype(BF16)
        upd = [_dot_tn(d["vh"][h], kh) for h in range(2)]
        st_ref[sid] = d["st"] * jnp.exp(b_last) + jnp.where(first, upd[0], upd[1])
        for h in range(2):
            oh = d["o"][h] + d["off"][h] + jnp.concatenate(d["diag"][h], axis=0)
            cols = slice((2 * p + h) * GLA_DV, (2 * p + h + 1) * GLA_DV)
            y_ref[bl, :, cols] = (sg_ref[bl, :, cols] * _rms(oh)).astype(y_ref.dtype)


GLA_BATCHES = 8


def _gla(gq, gk, la, gv, sr, bsz, seq):
    t = gq.shape[0]
    ell = GLA_CHUNK
    nc = seq // ell
    nb = min(GLA_BATCHES, bsz)
    ns = nb * (GLA_HEADS // 2)
    prod_rows = (ell // GLA_SUB) * sum(GLA_SUB - (s // SUBLANES) * SUBLANES for s in range(GLA_SUB))
    r3 = lambda a: a.reshape(bsz, seq, a.shape[1])
    spec = lambda w: pl.BlockSpec((nb, ell, w), lambda b, c: (b, c, 0))
    wq, wv = GLA_HEADS * GLA_DK, GLA_HEADS * GLA_DV
    y = pl.pallas_call(
        _gla_kernel,
        grid=(bsz // nb, nc),
        in_specs=[spec(wq), spec(wq), spec(wq), spec(wv), spec(wv)],
        out_specs=spec(wv),
        out_shape=jax.ShapeDtypeStruct((bsz, seq, wv), BF16),
        scratch_shapes=[pltpu.VMEM((ns, GLA_DV, LANES), F32),
                        pltpu.VMEM((ns, ell, LANES), F32),
                        pltpu.VMEM((ns, ell, 2 * GLA_DV), F32),
                        pltpu.VMEM((ns, prod_rows, LANES), F32),
                        pltpu.VMEM((ns, prod_rows, 2 * LANES), F32)],
        compiler_params=_cparams("parallel", "arbitrary"),
        name="gla",
    )(r3(gq), r3(gk), r3(la), r3(gv), r3(sr))
    return y.reshape(t, wv)


ROUTE_IDX, ROUTE_GATE, ROUTE_RANK = 0, SUBLANES, 2 * SUBLANES


def _post_tail(m, x_ref, g1_ref, gpost_ref, sh2_ref, sc2_ref, gpre_ref, rw_ref, rb_ref,
               x1_ref, h2_ref, route_ref, plan_ref, cnt_ref, carry_ref):
    i = pl.program_id(0)
    tm = m.shape[0]
    x1 = x_ref[...] + g1_ref[0] * (_rms(m) * gpost_ref[...])
    x1_ref[...] = x1
    h2 = _rms(x1) * (gpre_ref[...] * (1.0 + sc2_ref[0])) + sh2_ref[0]
    for c in range(D_MODEL // LANES):
        h2_ref[pl.ds(c, tm, stride=SUBLANES), :] = h2[:, c * LANES:(c + 1) * LANES]

    ne = N_EXPERTS
    h_hi = h2.astype(BF16)
    h_lo = (h2 - h_hi.astype(F32)).astype(BF16)
    w_both = rw_ref[...]
    hw = _dot_nt(w_both, h_hi)
    bias = jnp.concatenate([rb_ref[...]] * (tm // LANES), axis=1)
    work = hw[:ne] + hw[ne:] + _dot_nt(w_both[:ne], h_lo) + bias
    esub = lax.broadcasted_iota(jnp.int32, (ne, tm), 0).astype(F32)
    onehot = jnp.zeros((ne, tm), F32)
    hits, vals, idxs = [], [], []
    for k in range(TOP_K):
        mx = jnp.max(work, axis=0, keepdims=True)
        idx = jnp.min(jnp.where(work == mx, esub, float(ne)), axis=0, keepdims=True)
        hit = esub == idx
        hits.append(hit)
        vals.append(mx)
        idxs.append(idx)
        onehot = onehot + hit.astype(F32)
        work = jnp.where(hit, -jnp.inf, work)
    es = [jnp.exp(v - vals[0]) for v in vals]
    inv = 1.0 / (es[0] + es[1] + es[2] + es[3])
    gates = [e * inv for e in es]

    @pl.when(i == 0)
    def _():
        carry_ref[...] = jnp.zeros_like(carry_ref)

    r = lax.broadcasted_iota(jnp.int32, (tm, tm), 0)
    c = lax.broadcasted_iota(jnp.int32, (tm, tm), 1)
    carry = carry_ref[...]
    before = _bdot(onehot.astype(BF16), (r < c).astype(BF16)) + jnp.concatenate([carry[:ne]] * (tm // LANES), axis=1)
    ranks = [jnp.sum(jnp.where(hits[k], before, 0.0), axis=0, keepdims=True) for k in range(TOP_K)]
    carry = carry + jnp.concatenate(
        [jnp.broadcast_to(jnp.sum(onehot, axis=1, keepdims=True), (ne, LANES)),
         jnp.zeros((LANES - ne, LANES), F32)], axis=0)
    carry_ref[...] = carry
    cnt_ref[...] = carry.T[:SUBLANES]

    sub = lax.broadcasted_iota(jnp.int32, (SUBLANES, tm), 0)

    def rows(vs):
        out = jnp.zeros((SUBLANES, tm), F32)
        for k, v in enumerate(vs):
            out = jnp.where(sub == k, jnp.broadcast_to(v, (SUBLANES, tm)), out)
        return out

    rec = jnp.concatenate([rows(idxs), rows(gates), rows(ranks), jnp.zeros((LANES - 3 * SUBLANES, tm), F32)], axis=0)
    route_ref[...] = rec.T
    plan_ref[...] = rows(idxs + ranks)


def _out_even_kernel(ys_ref, u_ref, yb_ref, d_ref, gw_ref, gb_ref, wa_ref, wb_ref, *rest):
    y = ys_ref[...] + d_ref[...] * u_ref[...]
    g = jax.nn.gelu(y)
    ya = g * _sigmoid(_bdot(g.astype(BF16), gw_ref[...]) + gb_ref[...])
    m = _bdot(ya.astype(BF16), wa_ref[...]) + _bdot(yb_ref[...], wb_ref[...])
    _post_tail(m, *rest)


def _out_odd_kernel(yc_ref, yd_ref, wa_ref, wb_ref, *rest):
    m = _bdot(yc_ref[...], wa_ref[...]) + _bdot(yd_ref[...], wb_ref[...])
    _post_tail(m, *rest)


def _mixer_out(body, mix_args, mix_specs, x2, mod3, g_post, g_pre, router_w, router_b, seq, tm):
    t, d = x2.shape
    tps = seq // tm
    rw32 = router_w.astype(F32).T
    rw_hi = rw32.astype(BF16)
    rw = jnp.concatenate([rw_hi, (rw32 - rw_hi.astype(F32)).astype(BF16)], axis=0)
    rb = jnp.broadcast_to(router_b.astype(F32)[:, None], (N_EXPERTS, LANES))
    tok = lambda n: pl.BlockSpec((tm, n), lambda i: (i, 0))
    full = lambda a: pl.BlockSpec(a.shape, lambda i: (0,) * a.ndim)
    tail_args = [x2, mod3, g_post, mod3, mod3, g_pre, rw, rb]
    tail_specs = [tok(d), _mod_spec(2, tps), full(g_post), _mod_spec(3, tps), _mod_spec(4, tps),
                  full(g_pre), full(rw), full(rb)]
    return pl.pallas_call(
        body,
        grid=(t // tm,),
        in_specs=mix_specs + tail_specs,
        out_specs=[tok(d), pl.BlockSpec((tm * SUBLANES, LANES), lambda i: (i, 0)),
                   tok(LANES), pl.BlockSpec((SUBLANES, tm), lambda i: (0, i)),
                   pl.BlockSpec((SUBLANES, LANES), lambda i: (0, 0))],
        out_shape=[jax.ShapeDtypeStruct((t, d), F32),
                   jax.ShapeDtypeStruct((t * SUBLANES, LANES), F32),
                   jax.ShapeDtypeStruct((t, LANES), F32),
                   jax.ShapeDtypeStruct((SUBLANES, t), F32),
                   jax.ShapeDtypeStruct((SUBLANES, LANES), F32)],
        scratch_shapes=[pltpu.VMEM((LANES, LANES), F32)],
        compiler_params=_cparams("arbitrary"),
        name="mixer_out_router",
    )(*mix_args, *tail_args)


def _out_even(ys, u, yb, d_skip, glu_w, glu_b, w_out, *tail, seq, tm):
    tok = lambda n: pl.BlockSpec((tm, n), lambda i: (i, 0))
    full = lambda a: pl.BlockSpec(a.shape, lambda i: (0,) * a.ndim)
    args = [ys, u, yb, d_skip.reshape(1, -1), glu_w.astype(BF16), glu_b.reshape(1, -1),
            w_out[:S5_CH].astype(BF16), w_out[S5_CH:].astype(BF16)]
    specs = [tok(S5_CH), tok(S5_CH), tok(FOX_W)] + [full(a) for a in args[3:]]
    return _mixer_out(_out_even_kernel, args, specs, *tail, seq, tm)


def _out_odd(yc, yd, w_out, *tail, seq, tm):
    tok = lambda n: pl.BlockSpec((tm, n), lambda i: (i, 0))
    full = lambda a: pl.BlockSpec(a.shape, lambda i: (0,) * a.ndim)
    nc = yc.shape[1]
    args = [yc, yd, w_out[:nc].astype(BF16), w_out[nc:].astype(BF16)]
    specs = [tok(nc), tok(yd.shape[1])] + [full(a) for a in args[2:]]
    return _mixer_out(_out_odd_kernel, args, specs, *tail, seq, tm)


def _route_kernel(plan_ref, cnt_ref, dest_ref, blk_ref, meta_ref):
    tm = plan_ref.shape[1]
    cnt = cnt_ref[...]
    padded = jnp.floor((cnt + (MOE_BLOCK - 1.0)) * (1.0 / MOE_BLOCK)) * MOE_BLOCK
    r = lax.broadcasted_iota(jnp.int32, (LANES, LANES), 0)
    c = lax.broadcasted_iota(jnp.int32, (LANES, LANES), 1)
    hi, mid, lo = _split3(padded)
    incl = (r <= c).astype(BF16)
    pad_end = _bdot(hi, incl) + _bdot(mid, incl) + _bdot(lo, incl)
    pad_start = pad_end - padded
    start_col = jnp.broadcast_to(pad_start[0:1, :], (LANES, LANES)).T[:N_EXPERTS]
    start = jnp.concatenate([start_col] * (tm // LANES), axis=1)
    esub = lax.broadcasted_iota(jnp.int32, (N_EXPERTS, tm), 0).astype(F32)
    plan = plan_ref[...]
    sub = lax.broadcasted_iota(jnp.int32, (SUBLANES, tm), 0)
    dest = jnp.zeros((SUBLANES, tm), F32)
    for k in range(TOP_K):
        st = jnp.sum(jnp.where(esub == plan[k:k + 1, :], start, 0.0), axis=0, keepdims=True)
        dest = jnp.where(sub == k, jnp.broadcast_to(st + plan[TOP_K + k:TOP_K + k + 1, :], (SUBLANES, tm)), dest)
    dest_ref[...] = dest.astype(jnp.int32)

    nb = blk_ref.shape[1]
    end_col = jnp.sum(jnp.where(r == c, jnp.broadcast_to(pad_end[0:1, :], (LANES, LANES)), 0.0),
                      axis=-1, keepdims=True)
    jpos = lax.broadcasted_iota(jnp.int32, (LANES, nb), 1).astype(F32) * MOE_BLOCK
    esub = lax.broadcasted_iota(jnp.int32, (LANES, nb), 0)
    done = jnp.where((end_col <= jpos) & (esub < N_EXPERTS), 1.0, 0.0)
    be = jnp.minimum(jnp.sum(done, axis=0, keepdims=True), N_EXPERTS - 1.0)
    blk_ref[...] = jnp.broadcast_to(be, blk_ref.shape).astype(jnp.int32)
    lane1 = lax.broadcasted_iota(jnp.int32, (SUBLANES, LANES), 1)
    n_valid = jnp.sum(jnp.where(lane1 == N_EXPERTS - 1, pad_end, 0.0), axis=-1, keepdims=True) * (1.0 / MOE_BLOCK)
    sub1 = lax.broadcasted_iota(jnp.int32, (SUBLANES, LANES), 0)
    meta = jnp.where(sub1 == 0, pad_start + cnt, jnp.where(sub1 == 1, pad_end, jnp.broadcast_to(n_valid, (SUBLANES, LANES))))
    meta_ref[...] = meta.astype(jnp.int32)


def _route(plan, cnt, n_blocks, tm):
    t = plan.shape[1]
    nb_pad = -(-n_blocks // LANES) * LANES
    tok = pl.BlockSpec((SUBLANES, tm), lambda i: (0, i))
    fix = lambda n: pl.BlockSpec((SUBLANES, n), lambda i: (0, 0))
    return pl.pallas_call(
        _route_kernel,
        grid=(t // tm,),
        in_specs=[tok, fix(LANES)],
        out_specs=[tok, fix(nb_pad), fix(LANES)],
        out_shape=[jax.ShapeDtypeStruct((SUBLANES, t), jnp.int32),
                   jax.ShapeDtypeStruct((SUBLANES, nb_pad), jnp.int32),
                   jax.ShapeDtypeStruct((SUBLANES, LANES), jnp.int32)],
        compiler_params=_cparams("arbitrary"),
        name="route_plan",
    )(plan, cnt)


def _dispatch_kernel(pad_ref, dest_ref, h_ref, xb_ref, zero_ref, sem_z, sem_s):
    i = pl.program_id(0)
    tm = h_ref.shape[0]

    @pl.when(i == 0)
    def _():
        zero_ref[...] = jnp.zeros_like(zero_ref)
        sizes = [1 << b for b in range(int(math.log2(MOE_BLOCK)) - 1, -1, -1)]

        def fill(e, carry, do_wait):
            start = pad_ref[0, e]
            n_pad = pad_ref[1, e] - start
            off = start
            for sz in sizes:
                take = (n_pad & sz) != 0
                cp = pltpu.make_async_copy(zero_ref.at[pl.ds(0, sz)], xb_ref.at[pl.ds(off, sz)], sem_z)

                @pl.when(take)
                def _():
                    if do_wait:
                        cp.wait()
                    else:
                        cp.start()
                off = off + jnp.where(take, sz, 0)
            return carry

        half = zero_ref.shape[0]

        def fill_unused(j, carry, do_wait):
            for part in range(MOE_BLOCK // half):
                cp = pltpu.make_async_copy(zero_ref, xb_ref.at[pl.ds(j * MOE_BLOCK + part * half, half)], sem_z)
                if do_wait:
                    cp.wait()
                else:
                    cp.start()
            return carry

        n_blocks = xb_ref.shape[0] // MOE_BLOCK
        lax.fori_loop(0, N_EXPERTS, lambda e, c: fill(e, c, False), 0)
        lax.fori_loop(pad_ref[2, 0], n_blocks, lambda j, c: fill_unused(j, c, False), 0)
        lax.fori_loop(0, N_EXPERTS, lambda e, c: fill(e, c, True), 0)
        lax.fori_loop(pad_ref[2, 0], n_blocks, lambda j, c: fill_unused(j, c, True), 0)

    def issue(r, carry):
        for k in range(TOP_K):
            pltpu.make_async_copy(h_ref.at[r], xb_ref.at[dest_ref[r * TOP_K + k]], sem_s).start(priority=k % 2)
        return carry

    lax.fori_loop(0, tm, issue, 0, unroll=ROW_DMA_UNROLL)
    for k in range(TOP_K):
        pltpu.make_async_copy(h_ref, xb_ref.at[pl.ds(0, tm)], sem_s).wait()


def _dispatch(h2t, dest, meta, n_slots, tm):
    t = h2t.shape[0] // SUBLANES
    h3 = h2t.reshape(t, SUBLANES, LANES)
    return pl.pallas_call(
        _dispatch_kernel,
        grid_spec=pltpu.PrefetchScalarGridSpec(
            num_scalar_prefetch=1,
            grid=(t // tm,),
            in_specs=[pl.BlockSpec((tm * TOP_K,), lambda i, p: (i,), memory_space=pltpu.SMEM),
                      pl.BlockSpec((tm, SUBLANES, LANES), lambda i, p: (i, 0, 0))],
            out_specs=pl.BlockSpec(memory_space=pl.ANY),
            scratch_shapes=[pltpu.VMEM((MOE_BLOCK // 2, SUBLANES, LANES), F32),
                            pltpu.SemaphoreType.DMA, pltpu.SemaphoreType.DMA]),
        out_shape=jax.ShapeDtypeStruct((n_slots, SUBLANES, LANES), F32),
        compiler_params=_cparams("arbitrary"),
        name="moe_dispatch",
    )(meta[:3, :N_EXPERTS], dest, h3)


def _expert_kernel(be_ref, nv_ref, nxt_ref, x_ref, wgu_hbm, bgu_ref, wd_hbm, bd_ref, y_ref,
                   wgu_f32, wd_f32, wgu_bf, wd_bf, sem, *, layer):
    j = pl.program_id(0)
    valid = j < nv_ref[0]
    first = valid & ((j == 0) | (be_ref[j] != be_ref[jnp.maximum(j - 1, 0)]))

    def weight_copies(e):
        return (pltpu.make_async_copy(wgu_hbm.at[layer, e], wgu_f32, sem.at[0]),
                pltpu.make_async_copy(wd_hbm.at[layer, e], wd_f32, sem.at[1]))

    @pl.when(j == 0)
    def _():
        for cp in weight_copies(be_ref[0]):
            cp.start()

    @pl.when(first)
    def _():
        for cp in weight_copies(be_ref[j]):
            cp.wait()
        wgu_bf[...] = wgu_f32[...].astype(BF16)
        wd_bf[...] = wd_f32[...].astype(BF16)

        @pl.when(nxt_ref[j] >= 0)
        def _():
            for cp in weight_copies(nxt_ref[j]):
                cp.start()

    @pl.when(valid)
    def _():
        x = jnp.concatenate([x_ref[pl.ds(c, MOE_BLOCK, stride=SUBLANES), :] for c in range(D_MODEL // LANES)],
                            axis=1).astype(BF16)
        gu = _bdot(x, wgu_bf[...]) + bgu_ref[0]
        x_glu = jnp.minimum(gu[:, :D_EXPERT], SWIGLU_LIMIT)
        x_lin = jnp.clip(gu[:, D_EXPERT:], -SWIGLU_LIMIT, SWIGLU_LIMIT)
        act = x_glu * _sigmoid(SWIGLU_ALPHA * x_glu) * (x_lin + 1.0)
        y = _bdot(act.astype(BF16), wd_bf[...]) + bd_ref[0]
        for c in range(D_MODEL // LANES):
            y_ref[pl.ds(c, MOE_BLOCK, stride=SUBLANES), :] = y[:, c * LANES:(c + 1) * LANES]

    @pl.when(jnp.logical_not(valid))
    def _():
        y_ref[...] = jnp.zeros_like(y_ref)


def _experts(xb, block_expert, n_valid, w_gu, b_gu, w_down, b_down, layer):
    n_slots = xb.shape[0]
    n_blocks = n_slots // MOE_BLOCK
    rows = MOE_BLOCK * SUBLANES
    x2 = xb.reshape(n_slots * SUBLANES, LANES)
    depth, ne, d, de2 = w_gu.shape
    idx = jnp.arange(n_blocks, dtype=jnp.int32)
    is_first = ((idx == 0) | (block_expert != jnp.roll(block_expert, 1))) & (idx < n_valid[0])
    first_at = lax.cummin(jnp.where(is_first, idx, n_blocks)[::-1])[::-1]
    next_first = jnp.concatenate([first_at[1:], jnp.full((1,), n_blocks, jnp.int32)])
    nxt = jnp.where(next_first < n_blocks, block_expert[jnp.minimum(next_first, n_blocks - 1)], -1)
    last = lambda j, be, nv, nx: jnp.minimum(j, nv[0] - 1)
    bmap = lambda j, be, nv, nx: (layer, be[last(j, be, nv, nx)], 0, 0)
    return pl.pallas_call(
        functools.partial(_expert_kernel, layer=layer),
        grid_spec=pltpu.PrefetchScalarGridSpec(
            num_scalar_prefetch=3,
            grid=(n_blocks,),
            in_specs=[pl.BlockSpec((rows, LANES), lambda j, be, nv, nx: (last(j, be, nv, nx), 0)),
                      pl.BlockSpec(memory_space=pl.ANY),
                      pl.BlockSpec((None, 1, 1, de2), bmap),
                      pl.BlockSpec(memory_space=pl.ANY),
                      pl.BlockSpec((None, 1, 1, d), bmap)],
            out_specs=pl.BlockSpec((rows, LANES), lambda j, be, nv, nx: (j, 0)),
            scratch_shapes=[pltpu.VMEM((d, de2), F32), pltpu.VMEM((de2 // 2, d), F32),
                            pltpu.VMEM((d, de2), BF16), pltpu.VMEM((de2 // 2, d), BF16),
                            pltpu.SemaphoreType.DMA((2,))]),
        out_shape=jax.ShapeDtypeStruct((n_slots * SUBLANES, LANES), F32),
        compiler_params=_cparams("arbitrary"),
        name="moe_experts",
    )(block_expert, n_valid, nxt.astype(jnp.int32), x2, w_gu, b_gu.reshape(depth, ne, 1, de2), w_down,
      b_down.reshape(depth, ne, 1, d))


def _combine_kernel(dest_ref, dest_next_ref, yb_ref, gate_ref, x1_ref, g2_ref, gpost_ref, o_ref, buf, sem):
    i = pl.program_id(0)
    tm = x1_ref.shape[0]
    slot = i % 2

    def gather(idx_ref, into):
        def issue(r, carry):
            for k in range(TOP_K):
                src = pl.multiple_of(idx_ref[r * TOP_K + k] * SUBLANES, SUBLANES)
                dst = pl.multiple_of((k * tm + r) * SUBLANES, SUBLANES)
                pltpu.make_async_copy(yb_ref.at[pl.ds(src, SUBLANES), :], buf.at[into, pl.ds(dst, SUBLANES), :],
                                      sem.at[into]).start(priority=k % 2)
            return carry
        lax.fori_loop(0, tm, issue, 0, unroll=ROW_DMA_UNROLL)

    @pl.when(i == 0)
    def _():
        gather(dest_ref, 0)

    @pl.when(i + 1 < pl.num_programs(0))
    def _():
        gather(dest_next_ref, 1 - slot)

    pltpu.make_async_copy(yb_ref.at[pl.ds(0, TOP_K * tm * SUBLANES), :], buf.at[slot], sem.at[slot]).wait()
    gates = gate_ref[...]
    gk = [jnp.broadcast_to(gates[:, ROUTE_GATE + k:ROUTE_GATE + k + 1], (tm, LANES)) for k in range(TOP_K)]
    b2 = buf.at[slot]
    cols = []
    for c in range(D_MODEL // LANES):
        acc = jnp.zeros((tm, LANES), F32)
        for k in range(TOP_K):
            acc = acc + gk[k] * b2[pl.ds(k * tm * SUBLANES + c, tm, stride=SUBLANES), :]
        cols.append(acc)
    f = jnp.concatenate(cols, axis=1)
    o_ref[...] = x1_ref[...] + g2_ref[0] * (_rms(f) * gpost_ref[...])


def _combine(yb, dest, gates, x1, mod3, g_post, seq, tm):
    t, d = x1.shape
    tps = seq // tm
    n = t // tm
    return pl.pallas_call(
        _combine_kernel,
        grid=(n,),
        in_specs=[pl.BlockSpec((tm * TOP_K,), lambda i: (i,), memory_space=pltpu.SMEM),
                  pl.BlockSpec((tm * TOP_K,), lambda i: (jnp.minimum(i + 1, n - 1),), memory_space=pltpu.SMEM),
                  pl.BlockSpec(memory_space=pl.ANY),
                  pl.BlockSpec((tm, LANES), lambda i: (i, 0)),
                  pl.BlockSpec((tm, d), lambda i: (i, 0)),
                  _mod_spec(5, tps),
                  pl.BlockSpec(g_post.shape, lambda i: (0, 0))],
        out_specs=pl.BlockSpec((tm, d), lambda i: (i, 0)),
        out_shape=jax.ShapeDtypeStruct((t, d), F32),
        scratch_shapes=[pltpu.VMEM((2, TOP_K * tm * SUBLANES, LANES), F32), pltpu.SemaphoreType.DMA((2,))],
        compiler_params=_cparams("arbitrary"),
        name="moe_combine",
    )(dest, dest, yb, gates, x1, mod3, g_post)


def _moe(h2t, route, plan, cnt, x1, mod3, g_post, w_gu, b_gu, w_down, b_down, layer, seq):
    t = x1.shape[0]
    n_blocks = t * TOP_K // MOE_BLOCK + N_EXPERTS
    dest_rows, blk, meta = _route(plan, cnt, n_blocks, min(2048, t))
    dest = dest_rows[:TOP_K].T.reshape(t * TOP_K)
    xb = _dispatch(h2t, dest, meta, n_blocks * MOE_BLOCK, MOE_BLOCK)
    yb = _experts(xb, blk[0, :n_blocks], meta[2, :1], w_gu, b_gu, w_down, b_down, layer)
    return _combine(yb, dest, route, x1, mod3, g_post, seq, MOE_BLOCK)


TOKEN_TILE = 512
OUT_TILE = 512
FOX_Q_TILE = 512
FOX_K_TILE = 256


def kernel(x, c, ada_w, ada_b, norm_pre_mix, norm_post_mix, norm_pre_ffn, norm_post_ffn, ev_w_in, fox_b_f, s5_lam_re, s5_lam_im, s5_log_dt, s5_b_re, s5_b_im, s5_c_re, s5_c_im, s5_d, s5_glu_w, s5_glu_b, ev_w_out, od_w_in, gla_w_up, gla_b_gate, od_w_out, router_w, router_b, exp_w_gu, exp_b_gu, exp_w_down, exp_b_down):
    bsz, seq, d = x.shape
    t = bsz * seq
    tm = min(TOKEN_TILE, seq)
    x2 = x.reshape(t, d)
    mod = _modulation(c, ada_w, ada_b)
    for l in range(DEPTH):
        i = l // 2
        mod3 = mod[l].reshape(bsz, 1, 6 * d)
        row = lambda a: a[l].reshape(1, -1)
        tail = (x2, mod3, row(norm_post_mix), row(norm_pre_ffn), router_w[l], router_b[l])
        if l % 2 == 0:
            u, q, k, v_t = _in_even(x2, mod3, row(norm_pre_mix), ev_w_in[i], fox_b_f[i], seq, tm)
            tables = _s5_tables(s5_lam_re[i], s5_lam_im[i], s5_log_dt[i], s5_b_re[i], s5_b_im[i],
                                s5_c_re[i], s5_c_im[i])
            ys = _s5_scan(u, bsz, seq, tables)
            yb = _fox(q, k, v_t, bsz, seq, min(FOX_Q_TILE, seq), min(FOX_K_TILE, seq))
            outs = _out_even(ys, u, yb, s5_d[i], s5_glu_w[i], s5_glu_b[i], ev_w_out[i], *tail, seq=seq, tm=min(OUT_TILE, seq))
        else:
            rq, rk, rv, sg, gq, gk, gv, sr, la = _in_odd(x2, mod3, row(norm_pre_mix), od_w_in[i],
                                                         gla_w_up[i], gla_b_gate[i], seq, tm)
            yc = _retention(rq, rk, rv, sg, bsz, seq)
            yd = _gla(gq, gk, la, gv, sr, bsz, seq)
            outs = _out_odd(yc, yd, od_w_out[i], *tail, seq=seq, tm=min(OUT_TILE, seq))
        x1, h2t, route, plan, cnt = outs
        x2 = _moe(h2t, route, plan, cnt, x1, mod3, row(norm_post_ffn),
                  exp_w_gu, exp_b_gu, exp_w_down, exp_b_down, l, seq)
    return x2.reshape(bsz, seq, d)
```
